```python
import jax, jax.numpy as jnp
from jax import lax
import numpy as np

D_MODEL = 1024
BATCH = 8
SEQ = 4096
DEPTH = 1

N_META = 16
BLOCK = 128
META_PAD = BLOCK - N_META
HEAD_DIM = 64
SB_HEADS = (D_MODEL // 2) // HEAD_DIM
RWKV_HEADS = (D_MODEL // 2) // HEAD_DIM
SB_WIDTH = SB_HEADS * HEAD_DIM
RWKV_WIDTH = RWKV_HEADS * HEAD_DIM
MIX_WIDTH = SB_WIDTH + RWKV_WIDTH
D_FF = 2816
W_LORA = 32
A_LORA = 32
G_LORA = 96
N_RWKV_COLS = 3 * RWKV_WIDTH + W_LORA + A_LORA + G_LORA
IN_COLS = 3 * SB_WIDTH + N_RWKV_COLS
RMS_EPS = 1e-6
LNX_EPS = 64e-5

kernel_name = "hymba_sb_rwkv7_macaron"


def rms_norm(x, g):
    xf = x.astype(jnp.float32)
    y = xf * lax.rsqrt(jnp.mean(xf * xf, axis=-1, keepdims=True) + RMS_EPS)
    return (y * g.astype(jnp.float32)).astype(x.dtype)


def swiglu(h, w_gate, w_up, w_down):
    return (jax.nn.silu(h @ w_gate) * (h @ w_up)) @ w_down


def stick_breaking_attention(q, k, v):
    B, L, H, Dh = q.shape
    pad = ((0, 0), (META_PAD, 0), (0, 0), (0, 0))
    q, k, v = [jnp.pad(t, pad).transpose(0, 2, 1, 3) for t in (q, k, v)]
    Lp = L + META_PAD
    nblk = Lp // BLOCK
    key_pos = jnp.arange(Lp)
    scale = Dh ** -0.5
    qb = q.reshape(B, H, nblk, BLOCK, Dh).transpose(2, 0, 1, 3, 4)

    def block(args):
        q_blk, i = args
        q_pos = i * BLOCK + jnp.arange(BLOCK)
        z = jnp.einsum('bhqd,bhkd->bhqk', q_blk, k).astype(jnp.float32) * scale
        valid = (key_pos[None, :] < q_pos[:, None]) & (key_pos[None, :] >= META_PAD)
        log_keep = jnp.where(valid, -jax.nn.softplus(z), 0.0)
        log_rest = lax.cumsum(log_keep, axis=3, reverse=True) - log_keep
        attn = jnp.where(valid, jnp.exp(jax.nn.log_sigmoid(z) + log_rest), 0.0)
        return jnp.einsum('bhqk,bhkd->bhqd', attn.astype(v.dtype), v)

    out = lax.map(block, (qb, jnp.arange(nblk)))
    out = out.transpose(1, 0, 3, 2, 4).reshape(B, Lp, H * Dh)
    return out[:, META_PAD:]


def rwkv7_time_mix(p, mu, w0, w_up, a0, a_up, g_up, k_k, k_a, r_k, lnx_w, lnx_b):
    B, L, _ = p.shape
    C, H, N = RWKV_WIDTH, RWKV_HEADS, HEAD_DIM
    p_prev = jnp.pad(p, ((0, 0), (1, 0), (0, 0)))[:, :-1]
    p = p + (p_prev - p) * mu
    r = p[..., :C]
    k = p[..., C:2 * C]
    v = p[..., 2 * C:3 * C]
    xw = p[..., 3 * C:3 * C + W_LORA]
    xa = p[..., 3 * C + W_LORA:3 * C + W_LORA + A_LORA]
    xg = p[..., 3 * C + W_LORA + A_LORA:]
    w = -jax.nn.softplus(-(w0 + jnp.tanh(xw) @ w_up)) - 0.5
    decay = jnp.exp(-jnp.exp(w.astype(jnp.float32)))
    a = jax.nn.sigmoid(a0 + xa @ a_up)
    g = jax.nn.sigmoid(xg) @ g_up
    kk = (k * k_k).astype(jnp.float32).reshape(B, L, H, N)
    kk = kk / jnp.maximum(jnp.sqrt(jnp.sum(kk * kk, axis=-1, keepdims=True)), 1e-12)
    k = k * (1.0 + (a - 1.0) * k_a)

    rh, kh, vh, ah, dh = [t.astype(jnp.float32).reshape(B, L, H, N) for t in (r, k, v, a, decay)]

    def step(S, inp):
        r_t, w_t, k_t, v_t, kk_t, a_t = inp
        sa = jnp.einsum('bhij,bhj->bhi', S, -kk_t)
        S = (S * w_t[:, :, None, :] + sa[..., None] * (kk_t * a_t)[:, :, None, :]
             + v_t[..., None] * k_t[:, :, None, :])
        return S, jnp.einsum('bhij,bhj->bhi', S, r_t)

    xs = tuple(t.transpose(1, 0, 2, 3) for t in (rh, dh, kh, vh, kk, ah))
    S0 = jnp.zeros((B, H, N, N), jnp.float32)
    _, ys = lax.scan(step, S0, xs)
    y = ys.transpose(1, 0, 2, 3)
    mean = jnp.mean(y, axis=-1, keepdims=True)
    var = jnp.mean(jnp.square(y - mean), axis=-1, keepdims=True)
    y = ((y - mean) * lax.rsqrt(var + LNX_EPS)).reshape(B, L, C)
    y = y * lnx_w.astype(jnp.float32) + lnx_b.astype(jnp.float32)
    bonus = jnp.sum(rh * kh * r_k.astype(jnp.float32), axis=-1, keepdims=True) * vh
    y = y + bonus.reshape(B, L, C)
    return (y * g.astype(jnp.float32)).astype(p.dtype)


def _fwd_setup_inputs(seed: int = 0) -> dict:
    key = jax.random.key(seed)
    ks = jax.random.split(key, 32)
    nrm = lambda k, s: jax.random.normal(k, s, jnp.float32)
    gain = lambda k, s: 1.0 + 0.02 * nrm(k, s)
    Dd = DEPTH
    return {
        "x": nrm(ks[0], (BATCH, SEQ, D_MODEL)),
        "meta_tokens": nrm(ks[1], (N_META, D_MODEL)),
        "ffn1_norm": gain(ks[2], (Dd, D_MODEL)),
        "ffn1_w_gate": nrm(ks[3], (Dd, D_MODEL, D_FF)) * D_MODEL ** -0.5,
        "ffn1_w_up": nrm(ks[4], (Dd, D_MODEL, D_FF)) * D_MODEL ** -0.5,
        "ffn1_w_down": nrm(ks[5], (Dd, D_FF, D_MODEL)) * D_FF ** -0.5,
        "mix_norm": gain(ks[6], (Dd, D_MODEL)),
        "w_in": nrm(ks[7], (Dd, D_MODEL, IN_COLS)) * D_MODEL ** -0.5,
        "rwkv_mu": jax.random.uniform(ks[8], (Dd, N_RWKV_COLS), jnp.float32),
        "rwkv_w0": jax.random.uniform(ks[9], (Dd, RWKV_WIDTH), jnp.float32, -4.0, 1.0),
        "rwkv_w_up": nrm(ks[10], (Dd, W_LORA, RWKV_WIDTH)) * 0.5 * W_LORA ** -0.5,
        "rwkv_a0": 0.5 * nrm(ks[11], (Dd, RWKV_WIDTH)),
        "rwkv_a_up": nrm(ks[12], (Dd, A_LORA, RWKV_WIDTH)) * 0.5 * A_LORA ** -0.5,
        "rwkv_g_up": nrm(ks[13], (Dd, G_LORA, RWKV_WIDTH)) * G_LORA ** -0.5,
        "rwkv_k_k": 0.85 + 0.05 * nrm(ks[14], (Dd, RWKV_WIDTH)),
        "rwkv_k_a": 1.0 + 0.05 * nrm(ks[15], (Dd, RWKV_WIDTH)),
        "rwkv_r_k": 0.1 * nrm(ks[16], (Dd, RWKV_HEADS, HEAD_DIM)),
        "rwkv_lnx_w": gain(ks[17], (Dd, RWKV_WIDTH)),
        "rwkv_lnx_b": 0.02 * nrm(ks[18], (Dd, RWKV_WIDTH)),
        "w_out": nrm(ks[19], (Dd, MIX_WIDTH, D_MODEL)) * MIX_WIDTH ** -0.5,
        "ffn2_norm": gain(ks[20], (Dd, D_MODEL)),
        "ffn2_w_gate": nrm(ks[21], (Dd, D_MODEL, D_FF)) * D_MODEL ** -0.5,
        "ffn2_w_up": nrm(ks[22], (Dd, D_MODEL, D_FF)) * D_MODEL ** -0.5,
        "ffn2_w_down": nrm(ks[23], (Dd, D_FF, D_MODEL)) * D_FF ** -0.5,
        "final_norm": gain(ks[24], (D_MODEL,)),
    }


def _fwd_reference(x, meta_tokens, ffn1_norm, ffn1_w_gate, ffn1_w_up, ffn1_w_down, mix_norm, w_in,
              rwkv_mu, rwkv_w0, rwkv_w_up, rwkv_a0, rwkv_a_up, rwkv_g_up, rwkv_k_k, rwkv_k_a,
              rwkv_r_k, rwkv_lnx_w, rwkv_lnx_b, w_out, ffn2_norm, ffn2_w_gate, ffn2_w_up,
              ffn2_w_down, final_norm):
    B = x.shape[0]
    meta = jnp.broadcast_to(meta_tokens[None].astype(x.dtype), (B, N_META, D_MODEL))
    h = jnp.concatenate([meta, x], axis=1)
    for l in range(DEPTH):
        n = rms_norm(h, ffn1_norm[l])
        h = h + 0.5 * swiglu(n, ffn1_w_gate[l], ffn1_w_up[l], ffn1_w_down[l])
        n = rms_norm(h, mix_norm[l])
        proj = n @ w_in[l]
        Bn, L, _ = proj.shape
        q = proj[..., :SB_WIDTH].reshape(Bn, L, SB_HEADS, HEAD_DIM)
        k = proj[..., SB_WIDTH:2 * SB_WIDTH].reshape(Bn, L, SB_HEADS, HEAD_DIM)
        v = proj[..., 2 * SB_WIDTH:3 * SB_WIDTH].reshape(Bn, L, SB_HEADS, HEAD_DIM)
        sb_out = stick_breaking_attention(q, k, v)
        rw_out = rwkv7_time_mix(proj[..., 3 * SB_WIDTH:], rwkv_mu[l], rwkv_w0[l], rwkv_w_up[l],
                                rwkv_a0[l], rwkv_a_up[l], rwkv_g_up[l], rwkv_k_k[l], rwkv_k_a[l],
                                rwkv_r_k[l], rwkv_lnx_w[l], rwkv_lnx_b[l])
        h = h + jnp.concatenate([sb_out, rw_out], axis=-1) @ w_out[l]
        n = rms_norm(h, ffn2_norm[l])
        h = h + 0.5 * swiglu(n, ffn2_w_gate[l], ffn2_w_up[l], ffn2_w_down[l])
    return rms_norm(h, final_norm)[:, N_META:]


import jax as _jax
import jax.numpy as _jnp

TWIN_FORMAT = 'train_step'
FWD_PARAMS = ['x', 'meta_tokens', 'ffn1_norm', 'ffn1_w_gate', 'ffn1_w_up', 'ffn1_w_down', 'mix_norm', 'w_in', 'rwkv_mu', 'rwkv_w0', 'rwkv_w_up', 'rwkv_a0', 'rwkv_a_up', 'rwkv_g_up', 'rwkv_k_k', 'rwkv_k_a', 'rwkv_r_k', 'rwkv_lnx_w', 'rwkv_lnx_b', 'w_out', 'ffn2_norm', 'ffn2_w_gate', 'ffn2_w_up', 'ffn2_w_down', 'final_norm']
TWIN_WEIGHTS = ['meta_tokens', 'ffn1_norm', 'ffn1_w_gate', 'ffn1_w_up', 'ffn1_w_down', 'mix_norm', 'w_in', 'rwkv_mu', 'rwkv_w0', 'rwkv_w_up', 'rwkv_a0', 'rwkv_a_up', 'rwkv_g_up', 'rwkv_k_k', 'rwkv_k_a', 'rwkv_r_k', 'rwkv_lnx_w', 'rwkv_lnx_b', 'w_out', 'ffn2_norm', 'ffn2_w_gate', 'ffn2_w_up', 'ffn2_w_down', 'final_norm']
TWIN_DIFF_INPUT = 'x'
TWIN_INPUTS = ['x', 'meta_tokens', 'ffn1_norm', 'ffn1_w_gate', 'ffn1_w_up', 'ffn1_w_down', 'mix_norm', 'w_in', 'rwkv_mu', 'rwkv_w0', 'rwkv_w_up', 'rwkv_a0', 'rwkv_a_up', 'rwkv_g_up', 'rwkv_k_k', 'rwkv_k_a', 'rwkv_r_k', 'rwkv_lnx_w', 'rwkv_lnx_b', 'w_out', 'ffn2_norm', 'ffn2_w_gate', 'ffn2_w_up', 'ffn2_w_down', 'final_norm', 'loss_target', 'm_meta_tokens', 'm_ffn1_norm', 'm_ffn1_w_gate', 'm_ffn1_w_up', 'm_ffn1_w_down', 'm_mix_norm', 'm_w_in', 'm_rwkv_mu', 'm_rwkv_w0', 'm_rwkv_w_up', 'm_rwkv_a0', 'm_rwkv_a_up', 'm_rwkv_g_up', 'm_rwkv_k_k', 'm_rwkv_k_a', 'm_rwkv_r_k', 'm_rwkv_lnx_w', 'm_rwkv_lnx_b', 'm_w_out', 'm_ffn2_norm', 'm_ffn2_w_gate', 'm_ffn2_w_up', 'm_ffn2_w_down', 'm_final_norm', 'v_meta_tokens', 'v_ffn1_norm', 'v_ffn1_w_gate', 'v_ffn1_w_up', 'v_ffn1_w_down', 'v_mix_norm', 'v_w_in', 'v_rwkv_mu', 'v_rwkv_w0', 'v_rwkv_w_up', 'v_rwkv_a0', 'v_rwkv_a_up', 'v_rwkv_g_up', 'v_rwkv_k_k', 'v_rwkv_k_a', 'v_rwkv_r_k', 'v_rwkv_lnx_w', 'v_rwkv_lnx_b', 'v_w_out', 'v_ffn2_norm', 'v_ffn2_w_gate', 'v_ffn2_w_up', 'v_ffn2_w_down', 'v_final_norm']
TWIN_OUTPUTS = ['loss', 'grad_x', 'grad_meta_tokens', 'grad_ffn1_norm', 'grad_ffn1_w_gate', 'grad_ffn1_w_up', 'grad_ffn1_w_down', 'grad_mix_norm', 'grad_w_in', 'grad_rwkv_mu', 'grad_rwkv_w0', 'grad_rwkv_w_up', 'grad_rwkv_a0', 'grad_rwkv_a_up', 'grad_rwkv_g_up', 'grad_rwkv_k_k', 'grad_rwkv_k_a', 'grad_rwkv_r_k', 'grad_rwkv_lnx_w', 'grad_rwkv_lnx_b', 'grad_w_out', 'grad_ffn2_norm', 'grad_ffn2_w_gate', 'grad_ffn2_w_up', 'grad_ffn2_w_down', 'grad_final_norm', 'delta_meta_tokens', 'delta_ffn1_norm', 'delta_ffn1_w_gate', 'delta_ffn1_w_up', 'delta_ffn1_w_down', 'delta_mix_norm', 'delta_w_in', 'delta_rwkv_mu', 'delta_rwkv_w0', 'delta_rwkv_w_up', 'delta_rwkv_a0', 'delta_rwkv_a_up', 'delta_rwkv_g_up', 'delta_rwkv_k_k', 'delta_rwkv_k_a', 'delta_rwkv_r_k', 'delta_rwkv_lnx_w', 'delta_rwkv_lnx_b', 'delta_w_out', 'delta_ffn2_norm', 'delta_ffn2_w_gate', 'delta_ffn2_w_up', 'delta_ffn2_w_down', 'delta_final_norm', 'new_m_meta_tokens', 'new_m_ffn1_norm', 'new_m_ffn1_w_gate', 'new_m_ffn1_w_up', 'new_m_ffn1_w_down', 'new_m_mix_norm', 'new_m_w_in', 'new_m_rwkv_mu', 'new_m_rwkv_w0', 'new_m_rwkv_w_up', 'new_m_rwkv_a0', 'new_m_rwkv_a_up', 'new_m_rwkv_g_up', 'new_m_rwkv_k_k', 'new_m_rwkv_k_a', 'new_m_rwkv_r_k', 'new_m_rwkv_lnx_w', 'new_m_rwkv_lnx_b', 'new_m_w_out', 'new_m_ffn2_norm', 'new_m_ffn2_w_gate', 'new_m_ffn2_w_up', 'new_m_ffn2_w_down', 'new_m_final_norm', 'new_v_meta_tokens', 'new_v_ffn1_norm', 'new_v_ffn1_w_gate', 'new_v_ffn1_w_up', 'new_v_ffn1_w_down', 'new_v_mix_norm', 'new_v_w_in', 'new_v_rwkv_mu', 'new_v_rwkv_w0', 'new_v_rwkv_w_up', 'new_v_rwkv_a0', 'new_v_rwkv_a_up', 'new_v_rwkv_g_up', 'new_v_rwkv_k_k', 'new_v_rwkv_k_a', 'new_v_rwkv_r_k', 'new_v_rwkv_lnx_w', 'new_v_rwkv_lnx_b', 'new_v_w_out', 'new_v_ffn2_norm', 'new_v_ffn2_w_gate', 'new_v_ffn2_w_up', 'new_v_ffn2_w_down', 'new_v_final_norm']
TWIN_LEAF_KINDS = {'loss': 'loss', 'grad_x': 'grad_x', 'grad_meta_tokens': 'grad_w', 'grad_ffn1_norm': 'grad_w', 'grad_ffn1_w_gate': 'grad_w', 'grad_ffn1_w_up': 'grad_w', 'grad_ffn1_w_down': 'grad_w', 'grad_mix_norm': 'grad_w', 'grad_w_in': 'grad_w', 'grad_rwkv_mu': 'grad_w', 'grad_rwkv_w0': 'grad_w', 'grad_rwkv_w_up': 'grad_w', 'grad_rwkv_a0': 'grad_w', 'grad_rwkv_a_up': 'grad_w', 'grad_rwkv_g_up': 'grad_w', 'grad_rwkv_k_k': 'grad_w', 'grad_rwkv_k_a': 'grad_w', 'grad_rwkv_r_k': 'grad_w', 'grad_rwkv_lnx_w': 'grad_w', 'grad_rwkv_lnx_b': 'grad_w', 'grad_w_out': 'grad_w', 'grad_ffn2_norm': 'grad_w', 'grad_ffn2_w_gate': 'grad_w', 'grad_ffn2_w_up': 'grad_w', 'grad_ffn2_w_down': 'grad_w', 'grad_final_norm': 'grad_w', 'delta_meta_tokens': 'delta_w', 'delta_ffn1_norm': 'delta_w', 'delta_ffn1_w_gate': 'delta_w', 'delta_ffn1_w_up': 'delta_w', 'delta_ffn1_w_down': 'delta_w', 'delta_mix_norm': 'delta_w', 'delta_w_in': 'delta_w', 'delta_rwkv_mu': 'delta_w', 'delta_rwkv_w0': 'delta_w', 'delta_rwkv_w_up': 'delta_w', 'delta_rwkv_a0': 'delta_w', 'delta_rwkv_a_up': 'delta_w', 'delta_rwkv_g_up': 'delta_w', 'delta_rwkv_k_k': 'delta_w', 'delta_rwkv_k_a': 'delta_w', 'delta_rwkv_r_k': 'delta_w', 'delta_rwkv_lnx_w': 'delta_w', 'delta_rwkv_lnx_b': 'delta_w', 'delta_w_out': 'delta_w', 'delta_ffn2_norm': 'delta_w', 'delta_ffn2_w_gate': 'delta_w', 'delta_ffn2_w_up': 'delta_w', 'delta_ffn2_w_down': 'delta_w', 'delta_final_norm': 'delta_w', 'new_m_meta_tokens': 'new_m', 'new_m_ffn1_norm': 'new_m', 'new_m_ffn1_w_gate': 'new_m', 'new_m_ffn1_w_up': 'new_m', 'new_m_ffn1_w_down': 'new_m', 'new_m_mix_norm': 'new_m', 'new_m_w_in': 'new_m', 'new_m_rwkv_mu': 'new_m', 'new_m_rwkv_w0': 'new_m', 'new_m_rwkv_w_up': 'new_m', 'new_m_rwkv_a0': 'new_m', 'new_m_rwkv_a_up': 'new_m', 'new_m_rwkv_g_up': 'new_m', 'new_m_rwkv_k_k': 'new_m', 'new_m_rwkv_k_a': 'new_m', 'new_m_rwkv_r_k': 'new_m', 'new_m_rwkv_lnx_w': 'new_m', 'new_m_rwkv_lnx_b': 'new_m', 'new_m_w_out': 'new_m', 'new_m_ffn2_norm': 'new_m', 'new_m_ffn2_w_gate': 'new_m', 'new_m_ffn2_w_up': 'new_m', 'new_m_ffn2_w_down': 'new_m', 'new_m_final_norm': 'new_m', 'new_v_meta_tokens': 'new_v', 'new_v_ffn1_norm': 'new_v', 'new_v_ffn1_w_gate': 'new_v', 'new_v_ffn1_w_up': 'new_v', 'new_v_ffn1_w_down': 'new_v', 'new_v_mix_norm': 'new_v', 'new_v_w_in': 'new_v', 'new_v_rwkv_mu': 'new_v', 'new_v_rwkv_w0': 'new_v', 'new_v_rwkv_w_up': 'new_v', 'new_v_rwkv_a0': 'new_v', 'new_v_rwkv_a_up': 'new_v', 'new_v_rwkv_g_up': 'new_v', 'new_v_rwkv_k_k': 'new_v', 'new_v_rwkv_k_a': 'new_v', 'new_v_rwkv_r_k': 'new_v', 'new_v_rwkv_lnx_w': 'new_v', 'new_v_rwkv_lnx_b': 'new_v', 'new_v_w_out': 'new_v', 'new_v_ffn2_norm': 'new_v', 'new_v_ffn2_w_gate': 'new_v', 'new_v_ffn2_w_up': 'new_v', 'new_v_ffn2_w_down': 'new_v', 'new_v_final_norm': 'new_v'}


def _forward(args):
    return _fwd_reference(*[args[k] for k in FWD_PARAMS])


def _output_shape():
    out = _jax.eval_shape(lambda: _forward(_fwd_setup_inputs(0)))
    return out.shape, out.dtype

N_MICROBATCH = 1
ADAM_LR = 0.001
ADAM_B1 = 0.9
ADAM_B2 = 0.999
ADAM_EPS = 1e-08
ADAM_WD = 0.01
ADAM_STEP = 10
PER_EXAMPLE_BATCH_AXIS = {'x': 0, 'loss_target': 0}
SHARED_INPUTS = []
_WEIGHT_DTYPES = {'meta_tokens': _jnp.float32, 'ffn1_norm': _jnp.float32, 'ffn1_w_gate': _jnp.float32, 'ffn1_w_up': _jnp.float32, 'ffn1_w_down': _jnp.float32, 'mix_norm': _jnp.float32, 'w_in': _jnp.float32, 'rwkv_mu': _jnp.float32, 'rwkv_w0': _jnp.float32, 'rwkv_w_up': _jnp.float32, 'rwkv_a0': _jnp.float32, 'rwkv_a_up': _jnp.float32, 'rwkv_g_up': _jnp.float32, 'rwkv_k_k': _jnp.float32, 'rwkv_k_a': _jnp.float32, 'rwkv_r_k': _jnp.float32, 'rwkv_lnx_w': _jnp.float32, 'rwkv_lnx_b': _jnp.float32, 'w_out': _jnp.float32, 'ffn2_norm': _jnp.float32, 'ffn2_w_gate': _jnp.float32, 'ffn2_w_up': _jnp.float32, 'ffn2_w_down': _jnp.float32, 'final_norm': _jnp.float32}
MOMENT_SCALE = {'meta_tokens': 4.667876e-03, 'ffn1_norm': 8.725791e-02, 'ffn1_w_gate': 3.704131e-02, 'ffn1_w_up': 3.595836e-02, 'ffn1_w_down': 5.935591e-02, 'mix_norm': 1.356888e-01, 'w_in': 7.537910e-02, 'rwkv_mu': 1.333735e-01, 'rwkv_w0': 4.130090e-02, 'rwkv_w_up': 6.797473e-03, 'rwkv_a0': 3.515112e-02, 'rwkv_a_up': 3.142221e-02, 'rwkv_g_up': 8.428594e-02, 'rwkv_k_k': 4.807100e-02, 'rwkv_k_a': 8.969590e-02, 'rwkv_r_k': 1.767105e-01, 'rwkv_lnx_w': 8.230886e-02, 'rwkv_lnx_b': 9.313394e-02, 'w_out': 8.963909e-02, 'ffn2_norm': 6.534229e-02, 'ffn2_w_gate': 2.672934e-02, 'ffn2_w_up': 2.584102e-02, 'ffn2_w_down': 4.287455e-02, 'final_norm': 3.194769e+01}


def _to_microbatches(a, axis):
    t = _jnp.moveaxis(a, axis, 0)
    t = t.reshape((N_MICROBATCH, t.shape[0] // N_MICROBATCH) + t.shape[1:])
    return _jnp.moveaxis(t, 1, axis + 1)


def setup_inputs(seed: int = 0) -> dict:
    inp = _fwd_setup_inputs(seed)
    key = _jax.random.fold_in(_jax.random.key(seed), 7919)
    shape, _ = _output_shape()
    out = dict(inp)
    out["loss_target"] = _jax.random.normal(_jax.random.fold_in(key, 0), shape, _jnp.float32)
    for i, name in enumerate(TWIN_WEIGHTS):
        w = inp[name].astype(_jnp.float32)
        if MOMENT_SCALE is None:
            s = _jnp.sqrt(_jnp.mean(_jnp.square(w)) + 1e-30)
        else:
            s = MOMENT_SCALE[name]
        km, kv = _jax.random.split(_jax.random.fold_in(key, i + 1))
        out[name] = w
        out["m_" + name] = s * _jax.random.normal(km, w.shape, _jnp.float32)
        out["v_" + name] = (s * s) * _jax.random.uniform(kv, w.shape, _jnp.float32, 0.5, 1.5)
    if N_MICROBATCH > 1:
        for name, axis in PER_EXAMPLE_BATCH_AXIS.items():
            out[name] = _to_microbatches(out[name], axis)
    return {'x': out['x'], 'meta_tokens': out['meta_tokens'], 'ffn1_norm': out['ffn1_norm'], 'ffn1_w_gate': out['ffn1_w_gate'], 'ffn1_w_up': out['ffn1_w_up'], 'ffn1_w_down': out['ffn1_w_down'], 'mix_norm': out['mix_norm'], 'w_in': out['w_in'], 'rwkv_mu': out['rwkv_mu'], 'rwkv_w0': out['rwkv_w0'], 'rwkv_w_up': out['rwkv_w_up'], 'rwkv_a0': out['rwkv_a0'], 'rwkv_a_up': out['rwkv_a_up'], 'rwkv_g_up': out['rwkv_g_up'], 'rwkv_k_k': out['rwkv_k_k'], 'rwkv_k_a': out['rwkv_k_a'], 'rwkv_r_k': out['rwkv_r_k'], 'rwkv_lnx_w': out['rwkv_lnx_w'], 'rwkv_lnx_b': out['rwkv_lnx_b'], 'w_out': out['w_out'], 'ffn2_norm': out['ffn2_norm'], 'ffn2_w_gate': out['ffn2_w_gate'], 'ffn2_w_up': out['ffn2_w_up'], 'ffn2_w_down': out['ffn2_w_down'], 'final_norm': out['final_norm'], 'loss_target': out['loss_target'], 'm_meta_tokens': out['m_meta_tokens'], 'm_ffn1_norm': out['m_ffn1_norm'], 'm_ffn1_w_gate': out['m_ffn1_w_gate'], 'm_ffn1_w_up': out['m_ffn1_w_up'], 'm_ffn1_w_down': out['m_ffn1_w_down'], 'm_mix_norm': out['m_mix_norm'], 'm_w_in': out['m_w_in'], 'm_rwkv_mu': out['m_rwkv_mu'], 'm_rwkv_w0': out['m_rwkv_w0'], 'm_rwkv_w_up': out['m_rwkv_w_up'], 'm_rwkv_a0': out['m_rwkv_a0'], 'm_rwkv_a_up': out['m_rwkv_a_up'], 'm_rwkv_g_up': out['m_rwkv_g_up'], 'm_rwkv_k_k': out['m_rwkv_k_k'], 'm_rwkv_k_a': out['m_rwkv_k_a'], 'm_rwkv_r_k': out['m_rwkv_r_k'], 'm_rwkv_lnx_w': out['m_rwkv_lnx_w'], 'm_rwkv_lnx_b': out['m_rwkv_lnx_b'], 'm_w_out': out['m_w_out'], 'm_ffn2_norm': out['m_ffn2_norm'], 'm_ffn2_w_gate': out['m_ffn2_w_gate'], 'm_ffn2_w_up': out['m_ffn2_w_up'], 'm_ffn2_w_down': out['m_ffn2_w_down'], 'm_final_norm': out['m_final_norm'], 'v_meta_tokens': out['v_meta_tokens'], 'v_ffn1_norm': out['v_ffn1_norm'], 'v_ffn1_w_gate': out['v_ffn1_w_gate'], 'v_ffn1_w_up': out['v_ffn1_w_up'], 'v_ffn1_w_down': out['v_ffn1_w_down'], 'v_mix_norm': out['v_mix_norm'], 'v_w_in': out['v_w_in'], 'v_rwkv_mu': out['v_rwkv_mu'], 'v_rwkv_w0': out['v_rwkv_w0'], 'v_rwkv_w_up': out['v_rwkv_w_up'], 'v_rwkv_a0': out['v_rwkv_a0'], 'v_rwkv_a_up': out['v_rwkv_a_up'], 'v_rwkv_g_up': out['v_rwkv_g_up'], 'v_rwkv_k_k': out['v_rwkv_k_k'], 'v_rwkv_k_a': out['v_rwkv_k_a'], 'v_rwkv_r_k': out['v_rwkv_r_k'], 'v_rwkv_lnx_w': out['v_rwkv_lnx_w'], 'v_rwkv_lnx_b': out['v_rwkv_lnx_b'], 'v_w_out': out['v_w_out'], 'v_ffn2_norm': out['v_ffn2_norm'], 'v_ffn2_w_gate': out['v_ffn2_w_gate'], 'v_ffn2_w_up': out['v_ffn2_w_up'], 'v_ffn2_w_down': out['v_ffn2_w_down'], 'v_final_norm': out['v_final_norm']}


def _loss(weights, diff, rest, loss_target):
    with _jax.named_scope("forward"):
        args = {**rest, TWIN_DIFF_INPUT: diff, **{k: w.astype(_WEIGHT_DTYPES[k]) for k, w in weights.items()}}
        y = _forward(args)
    with _jax.named_scope("loss_head"):
        err = _jnp.square(y.astype(_jnp.float32) - loss_target)
        return 0.5 * _jnp.sum(_jnp.mean(err, axis=-1)) if err.ndim else 0.5 * err


def _adamw(w, g, m, v):
    m = ADAM_B1 * m + (1.0 - ADAM_B1) * g
    v = ADAM_B2 * v + (1.0 - ADAM_B2) * _jnp.square(g)
    m_hat = m / (1.0 - ADAM_B1 ** ADAM_STEP)
    v_hat = v / (1.0 - ADAM_B2 ** ADAM_STEP)
    delta = -ADAM_LR * (m_hat / (_jnp.sqrt(v_hat) + ADAM_EPS) + ADAM_WD * w)
    return delta, m, v


def reference(x, meta_tokens, ffn1_norm, ffn1_w_gate, ffn1_w_up, ffn1_w_down, mix_norm, w_in, rwkv_mu, rwkv_w0, rwkv_w_up, rwkv_a0, rwkv_a_up, rwkv_g_up, rwkv_k_k, rwkv_k_a, rwkv_r_k, rwkv_lnx_w, rwkv_lnx_b, w_out, ffn2_norm, ffn2_w_gate, ffn2_w_up, ffn2_w_down, final_norm, loss_target, m_meta_tokens, m_ffn1_norm, m_ffn1_w_gate, m_ffn1_w_up, m_ffn1_w_down, m_mix_norm, m_w_in, m_rwkv_mu, m_rwkv_w0, m_rwkv_w_up, m_rwkv_a0, m_rwkv_a_up, m_rwkv_g_up, m_rwkv_k_k, m_rwkv_k_a, m_rwkv_r_k, m_rwkv_lnx_w, m_rwkv_lnx_b, m_w_out, m_ffn2_norm, m_ffn2_w_gate, m_ffn2_w_up, m_ffn2_w_down, m_final_norm, v_meta_tokens, v_ffn1_norm, v_ffn1_w_gate, v_ffn1_w_up, v_ffn1_w_down, v_mix_norm, v_w_in, v_rwkv_mu, v_rwkv_w0, v_rwkv_w_up, v_rwkv_a0, v_rwkv_a_up, v_rwkv_g_up, v_rwkv_k_k, v_rwkv_k_a, v_rwkv_r_k, v_rwkv_lnx_w, v_rwkv_lnx_b, v_w_out, v_ffn2_norm, v_ffn2_w_gate, v_ffn2_w_up, v_ffn2_w_down, v_final_norm):
    given = dict(x=x, meta_tokens=meta_tokens, ffn1_norm=ffn1_norm, ffn1_w_gate=ffn1_w_gate, ffn1_w_up=ffn1_w_up, ffn1_w_down=ffn1_w_down, mix_norm=mix_norm, w_in=w_in, rwkv_mu=rwkv_mu, rwkv_w0=rwkv_w0, rwkv_w_up=rwkv_w_up, rwkv_a0=rwkv_a0, rwkv_a_up=rwkv_a_up, rwkv_g_up=rwkv_g_up, rwkv_k_k=rwkv_k_k, rwkv_k_a=rwkv_k_a, rwkv_r_k=rwkv_r_k, rwkv_lnx_w=rwkv_lnx_w, rwkv_lnx_b=rwkv_lnx_b, w_out=w_out, ffn2_norm=ffn2_norm, ffn2_w_gate=ffn2_w_gate, ffn2_w_up=ffn2_w_up, ffn2_w_down=ffn2_w_down, final_norm=final_norm, loss_target=loss_target, m_meta_tokens=m_meta_tokens, m_ffn1_norm=m_ffn1_norm, m_ffn1_w_gate=m_ffn1_w_gate, m_ffn1_w_up=m_ffn1_w_up, m_ffn1_w_down=m_ffn1_w_down, m_mix_norm=m_mix_norm, m_w_in=m_w_in, m_rwkv_mu=m_rwkv_mu, m_rwkv_w0=m_rwkv_w0, m_rwkv_w_up=m_rwkv_w_up, m_rwkv_a0=m_rwkv_a0, m_rwkv_a_up=m_rwkv_a_up, m_rwkv_g_up=m_rwkv_g_up, m_rwkv_k_k=m_rwkv_k_k, m_rwkv_k_a=m_rwkv_k_a, m_rwkv_r_k=m_rwkv_r_k, m_rwkv_lnx_w=m_rwkv_lnx_w, m_rwkv_lnx_b=m_rwkv_lnx_b, m_w_out=m_w_out, m_ffn2_norm=m_ffn2_norm, m_ffn2_w_gate=m_ffn2_w_gate, m_ffn2_w_up=m_ffn2_w_up, m_ffn2_w_down=m_ffn2_w_down, m_final_norm=m_final_norm, v_meta_tokens=v_meta_tokens, v_ffn1_norm=v_ffn1_norm, v_ffn1_w_gate=v_ffn1_w_gate, v_ffn1_w_up=v_ffn1_w_up, v_ffn1_w_down=v_ffn1_w_down, v_mix_norm=v_mix_norm, v_w_in=v_w_in, v_rwkv_mu=v_rwkv_mu, v_rwkv_w0=v_rwkv_w0, v_rwkv_w_up=v_rwkv_w_up, v_rwkv_a0=v_rwkv_a0, v_rwkv_a_up=v_rwkv_a_up, v_rwkv_g_up=v_rwkv_g_up, v_rwkv_k_k=v_rwkv_k_k, v_rwkv_k_a=v_rwkv_k_a, v_rwkv_r_k=v_rwkv_r_k, v_rwkv_lnx_w=v_rwkv_lnx_w, v_rwkv_lnx_b=v_rwkv_lnx_b, v_w_out=v_w_out, v_ffn2_norm=v_ffn2_norm, v_ffn2_w_gate=v_ffn2_w_gate, v_ffn2_w_up=v_ffn2_w_up, v_ffn2_w_down=v_ffn2_w_down, v_final_norm=v_final_norm)
    weights = {n: given[n] for n in TWIN_WEIGHTS}
    shared = {n: given[n] for n in SHARED_INPUTS}
    per_example = {n: given[n] for n in ['x']}
    grad_fn = _jax.value_and_grad(_loss, argnums=(0, 1))

    def one_microbatch(ex, loss_target):
        ex = dict(ex)
        diff = ex.pop(TWIN_DIFF_INPUT)
        return grad_fn(weights, diff, {**shared, **ex}, loss_target)

    if N_MICROBATCH == 1:
        loss, (grad_w, grad_x) = one_microbatch(per_example, given["loss_target"])
    else:
        def body(carry, xs):
            loss_sum, grad_sum = carry
            l_k, (gw_k, gx_k) = one_microbatch(xs[0], xs[1])
            with _jax.named_scope("update"):
                return (loss_sum + l_k, _jax.tree.map(_jnp.add, grad_sum, gw_k)), gx_k

        init = (_jnp.zeros((), _jnp.float32), _jax.tree.map(_jnp.zeros_like, weights))
        (loss, grad_w), grad_x = _jax.lax.scan(body, init, (per_example, given["loss_target"]))
    with _jax.named_scope("update"):
        delta_w, new_m, new_v = {}, {}, {}
        for n in TWIN_WEIGHTS:
            delta_w[n], new_m[n], new_v[n] = _adamw(weights[n], grad_w[n], given["m_" + n], given["v_" + n])
    return (loss, grad_x, *[grad_w[n] for n in TWIN_WEIGHTS], *[delta_w[n] for n in TWIN_WEIGHTS],
            *[new_m[n] for n in TWIN_WEIGHTS], *[new_v[n] for n in TWIN_WEIGHTS])
```

```python
import functools

import jax
import jax.numpy as jnp
from jax import lax
from jax.experimental import pallas as pl
from jax.experimental.pallas import tpu as pltpu

F32 = jnp.float32
BF16 = jnp.bfloat16

RMS_EPS = 1e-6
LNX_EPS = 64e-5
N_META = 16
ROW0 = 128
META_PAD = ROW0 - N_META
HEAD = 64
N_HEADS = 8
GROUP = N_HEADS * HEAD
LORA_W, LORA_A, LORA_G = 32, 32, 96
LORA_PAD = 256
RW_COLS = 3 * GROUP + LORA_PAD
IN_COLS_PAD = 3 * GROUP + RW_COLS
ATT_BLOCK = 128
CHUNK = 64
VMEM_LIMIT = 56 * 1024 * 1024

ADAM_LR, ADAM_B1, ADAM_B2, ADAM_EPS, ADAM_WD, ADAM_STEP = 0.001, 0.9, 0.999, 1e-08, 0.01, 10

MESH = pl.DeviceIdType.MESH


def _params(*sem):
    return pltpu.CompilerParams(dimension_semantics=tuple(sem), vmem_limit_bytes=VMEM_LIMIT)


def _dot(a, b):
    return lax.dot_general(a, b, (((1,), (0,)), ((), ())), preferred_element_type=F32)


def _dot_nt(a, b):
    return lax.dot_general(a, b, (((1,), (1,)), ((), ())), preferred_element_type=F32)


def _dot_tn(a, b):
    return lax.dot_general(a, b, (((0,), (0,)), ((), ())), preferred_element_type=F32)


def _split2(x):
    hi = x.astype(BF16)
    return hi, (x - hi.astype(F32)).astype(BF16)


def _sigmoid(x):
    return 1.0 / (1.0 + jnp.exp(-x))


def _rms_fwd(x, g):
    rstd = lax.rsqrt(jnp.mean(x * x, axis=-1, keepdims=True) + RMS_EPS)
    xhat = x * rstd
    return xhat * g, xhat, rstd


def _rms_bwd(dn, xhat, rstd, g):
    dxhat = dn * g
    dx = rstd * (dxhat - xhat * jnp.mean(dxhat * xhat, axis=-1, keepdims=True))
    return dx, jnp.sum(dn * xhat, axis=0, keepdims=True)


def _row_tile(rows):
    return 384 if rows % 384 == 0 else 128


def _half_tile(cols):
    return cols // 2 if cols % 256 == 0 else cols


def _ffn_fwd(name, h, g, wg, wu, wd):
    rows, d = h.shape
    f = wg.shape[1]
    tm, tf = _row_tile(rows), _half_tile(f)
    nj = f // tf

    def body(h_ref, g_ref, wg_ref, wu_ref, wd_ref, ho_ref, a_ref, b_ref, n_sc, acc_sc):
        j = pl.program_id(1)

        @pl.when(j == 0)
        def _():
            n, _, _ = _rms_fwd(h_ref[...], g_ref[...])
            n_sc[...] = n.astype(BF16)
            acc_sc[...] = jnp.zeros_like(acc_sc)

        n = n_sc[...]
        a = _dot(n, wg_ref[...])
        b = _dot(n, wu_ref[...])
        a_ref[...] = a
        b_ref[...] = b
        s = a * _sigmoid(a) * b
        acc_sc[...] += _dot(s.astype(BF16), wd_ref[...])

        @pl.when(j == nj - 1)
        def _():
            ho_ref[...] = h_ref[...] + 0.5 * acc_sc[...]

    return pl.pallas_call(
        body, name=name, grid=(rows // tm, nj),
        in_specs=[pl.BlockSpec((tm, d), lambda i, j: (i, 0)),
                  pl.BlockSpec((1, d), lambda i, j: (0, 0)),
                  pl.BlockSpec((d, tf), lambda i, j: (0, j)),
                  pl.BlockSpec((d, tf), lambda i, j: (0, j)),
                  pl.BlockSpec((tf, d), lambda i, j: (j, 0))],
        out_specs=[pl.BlockSpec((tm, d), lambda i, j: (i, 0)),
                   pl.BlockSpec((tm, tf), lambda i, j: (i, j)),
                   pl.BlockSpec((tm, tf), lambda i, j: (i, j))],
        out_shape=[jax.ShapeDtypeStruct((rows, d), F32),
                   jax.ShapeDtypeStruct((rows, f), F32),
                   jax.ShapeDtypeStruct((rows, f), F32)],
        scratch_shapes=[pltpu.VMEM((tm, d), BF16), pltpu.VMEM((tm, d), F32)],
        compiler_params=_params("arbitrary", "arbitrary"),
    )(h, g, wg, wu, wd)


def _ffn_bwd(name, dh, h, g, a, b, wg, wu, wd):
    rows, d = h.shape
    f = wg.shape[1]
    tm, tf = _row_tile(rows), _half_tile(f)
    ni, nj = rows // tm, f // tf

    def body(dh_ref, h_ref, g_ref, a_ref, b_ref, wg_ref, wu_ref, wd_ref,
             dhi_ref, da_ref, db_ref, s_ref, n_ref, dhh_ref, dg_ref, dn_sc):
        i, j = pl.program_id(0), pl.program_id(1)

        @pl.when(j == 0)
        def _():
            n, _, _ = _rms_fwd(h_ref[...], g_ref[...])
            n_ref[...] = n.astype(BF16)
            dhh_ref[...] = (0.5 * dh_ref[...]).astype(BF16)
            dn_sc[...] = jnp.zeros_like(dn_sc)

        @pl.when((i == 0) & (j == 0))
        def _():
            dg_ref[...] = jnp.zeros_like(dg_ref)

        ds = _dot_nt(dhh_ref[...], wd_ref[...])
        av, bv = a_ref[...], b_ref[...]
        sig = _sigmoid(av)
        silu = av * sig
        s_ref[...] = (silu * bv).astype(BF16)
        db = (ds * silu).astype(BF16)
        da = (ds * bv * (sig * (1.0 + av * (1.0 - sig)))).astype(BF16)
        da_ref[...] = da
        db_ref[...] = db
        dn_sc[...] += _dot_nt(da, wg_ref[...]) + _dot_nt(db, wu_ref[...])

        @pl.when(j == nj - 1)
        def _():
            gv = g_ref[...]
            _, xhat, rstd = _rms_fwd(h_ref[...], gv)
            dx, dg = _rms_bwd(dn_sc[...], xhat, rstd, gv)
            dhi_ref[...] = dh_ref[...] + dx
            dg_ref[...] += dg

    return pl.pallas_call(
        body, name=name, grid=(ni, nj),
        in_specs=[pl.BlockSpec((tm, d), lambda i, j: (i, 0)),
                  pl.BlockSpec((tm, d), lambda i, j: (i, 0)),
                  pl.BlockSpec((1, d), lambda i, j: (0, 0)),
                  pl.BlockSpec((tm, tf), lambda i, j: (i, j)),
                  pl.BlockSpec((tm, tf), lambda i, j: (i, j)),
                  pl.BlockSpec((d, tf), lambda i, j: (0, j)),
                  pl.BlockSpec((d, tf), lambda i, j: (0, j)),
                  pl.BlockSpec((tf, d), lambda i, j: (j, 0))],
        out_specs=[pl.BlockSpec((tm, d), lambda i, j: (i, 0)),
                   pl.BlockSpec((tm, tf), lambda i, j: (i, j)),
                   pl.BlockSpec((tm, tf), lambda i, j: (i, j)),
                   pl.BlockSpec((tm, tf), lambda i, j: (i, j)),
                   pl.BlockSpec((tm, d), lambda i, j: (i, 0)),
                   pl.BlockSpec((tm, d), lambda i, j: (i, 0)),
                   pl.BlockSpec((1, d), lambda i, j: (0, 0))],
        out_shape=[jax.ShapeDtypeStruct((rows, d), F32),
                   jax.ShapeDtypeStruct((rows, f), BF16),
                   jax.ShapeDtypeStruct((rows, f), BF16),
                   jax.ShapeDtypeStruct((rows, f), BF16),
                   jax.ShapeDtypeStruct((rows, d), BF16),
                   jax.ShapeDtypeStruct((rows, d), BF16),
                   jax.ShapeDtypeStruct((1, d), F32)],
        scratch_shapes=[pltpu.VMEM((tm, d), F32)],
        compiler_params=_params("arbitrary", "arbitrary"),
    )(dh, h, g, a, b, wg, wu, wd)


def _mm_tn(name, a, b):
    k, m = a.shape
    n = b.shape[1]
    tk = _row_tile(k)
    tm = _half_tile(m) if m > 1024 else m
    tn = _half_tile(n) if n > 1024 else n

    def body(a_ref, b_ref, o_ref):
        @pl.when(pl.program_id(2) == 0)
        def _():
            o_ref[...] = jnp.zeros_like(o_ref)

        o_ref[...] += _dot_tn(a_ref[...], b_ref[...])

    return pl.pallas_call(
        body, name=name, grid=(m // tm, n // tn, k // tk),
        in_specs=[pl.BlockSpec((tk, tm), lambda i, j, kk: (kk, i)),
                  pl.BlockSpec((tk, tn), lambda i, j, kk: (kk, j))],
        out_specs=pl.BlockSpec((tm, tn), lambda i, j, kk: (i, j)),
        out_shape=jax.ShapeDtypeStruct((m, n), F32),
        compiler_params=_params("arbitrary", "arbitrary", "arbitrary"),
    )(a, b)


def _norm_proj(name, h, g, w):
    rows, d = h.shape
    n = w.shape[1]
    tm, tn = _row_tile(rows), _half_tile(n)

    def body(h_ref, g_ref, w_ref, o_ref, n_ref):
        @pl.when(pl.program_id(1) == 0)
        def _():
            nv, _, _ = _rms_fwd(h_ref[...], g_ref[...])
            n_ref[...] = nv.astype(BF16)

        o_ref[...] = _dot(n_ref[...], w_ref[...])

    return pl.pallas_call(
        body, name=name, grid=(rows // tm, n // tn),
        in_specs=[pl.BlockSpec((tm, d), lambda i, j: (i, 0)),
                  pl.BlockSpec((1, d), lambda i, j: (0, 0)),
                  pl.BlockSpec((d, tn), lambda i, j: (0, j))],
        out_specs=[pl.BlockSpec((tm, tn), lambda i, j: (i, j)),
                   pl.BlockSpec((tm, d), lambda i, j: (i, 0))],
        out_shape=[jax.ShapeDtypeStruct((rows, n), F32), jax.ShapeDtypeStruct((rows, d), BF16)],
        compiler_params=_params("arbitrary", "arbitrary"),
    )(h, g, w)


def _out_proj(name, h, sb, rw, w):
    rows, d = h.shape
    gw = sb.shape[1]
    tm = _row_tile(rows)

    def body(h_ref, sb_ref, rw_ref, w_ref, o_ref, mix_ref):
        mix_ref[:, :gw] = sb_ref[...].astype(BF16)
        mix_ref[:, gw:] = rw_ref[...].astype(BF16)
        o_ref[...] = h_ref[...] + _dot(mix_ref[...], w_ref[...])

    return pl.pallas_call(
        body, name=name, grid=(rows // tm,),
        in_specs=[pl.BlockSpec((tm, d), lambda i: (i, 0)),
                  pl.BlockSpec((tm, gw), lambda i: (i, 0)),
                  pl.BlockSpec((tm, gw), lambda i: (i, 0)),
                  pl.BlockSpec((2 * gw, d), lambda i: (0, 0))],
        out_specs=[pl.BlockSpec((tm, d), lambda i: (i, 0)),
                   pl.BlockSpec((tm, 2 * gw), lambda i: (i, 0))],
        out_shape=[jax.ShapeDtypeStruct((rows, d), F32), jax.ShapeDtypeStruct((rows, 2 * gw), BF16)],
        compiler_params=_params("arbitrary"),
    )(h, sb, rw, w)


def _out_proj_bwd(name, dh, w):
    rows, d = dh.shape
    k = w.shape[0]
    tm = _row_tile(rows)

    def body(dh_ref, w_ref, o_ref, dhb_ref):
        dhb = dh_ref[...].astype(BF16)
        dhb_ref[...] = dhb
        o_ref[...] = _dot_nt(dhb, w_ref[...])

    return pl.pallas_call(
        body, name=name, grid=(rows // tm,),
        in_specs=[pl.BlockSpec((tm, d), lambda i: (i, 0)),
                  pl.BlockSpec((k, d), lambda i: (0, 0))],
        out_specs=[pl.BlockSpec((tm, k), lambda i: (i, 0)),
                   pl.BlockSpec((tm, d), lambda i: (i, 0))],
        out_shape=[jax.ShapeDtypeStruct((rows, k), F32), jax.ShapeDtypeStruct((rows, d), BF16)],
        compiler_params=_params("arbitrary"),
    )(dh, w)


def _norm_proj_bwd(name, dproj, w, h, g, dh):
    rows, n = dproj.shape
    d = w.shape[0]
    tm = _row_tile(rows)

    def body(dp_ref, w_ref, h_ref, g_ref, dh_ref, o_ref, dg_ref):
        @pl.when(pl.program_id(0) == 0)
        def _():
            dg_ref[...] = jnp.zeros_like(dg_ref)

        dn = _dot_nt(dp_ref[...], w_ref[...])
        gv = g_ref[...]
        _, xhat, rstd = _rms_fwd(h_ref[...], gv)
        dx, dg = _rms_bwd(dn, xhat, rstd, gv)
        o_ref[...] = dh_ref[...] + dx
        dg_ref[...] += dg

    return pl.pallas_call(
        body, name=name, grid=(rows // tm,),
        in_specs=[pl.BlockSpec((tm, n), lambda i: (i, 0)),
                  pl.BlockSpec((d, n), lambda i: (0, 0)),
                  pl.BlockSpec((tm, d), lambda i: (i, 0)),
                  pl.BlockSpec((1, d), lambda i: (0, 0)),
                  pl.BlockSpec((tm, d), lambda i: (i, 0))],
        out_specs=[pl.BlockSpec((tm, d), lambda i: (i, 0)),
                   pl.BlockSpec((1, d), lambda i: (0, 0))],
        out_shape=[jax.ShapeDtypeStruct((rows, d), F32), jax.ShapeDtypeStruct((1, d), F32)],
        compiler_params=_params("arbitrary"),
    )(dproj, w, h, g, dh)


def _loss_head(name, h, g, tgt):
    rows, d = h.shape
    tm = _row_tile(rows)

    def body(h_ref, g_ref, t_ref, loss_ref, dh_ref, dg_ref):
        i = pl.program_id(0)

        @pl.when(i == 0)
        def _():
            loss_ref[...] = jnp.zeros_like(loss_ref)
            dg_ref[...] = jnp.zeros_like(dg_ref)

        gv = g_ref[...]
        y, xhat, rstd = _rms_fwd(h_ref[...], gv)
        row = i * tm + lax.broadcasted_iota(jnp.int32, (tm, 1), 0)
        diff = jnp.where(row >= ROW0, y - t_ref[...], 0.0)
        part = 0.5 * jnp.sum(jnp.sum(diff * diff, axis=-1, keepdims=True), axis=0, keepdims=True) / d
        loss_ref[...] += jnp.broadcast_to(part, loss_ref.shape)
        dx, dg = _rms_bwd(diff / d, xhat, rstd, gv)
        dh_ref[...] = dx
        dg_ref[...] += dg

    return pl.pallas_call(
        body, name=name, grid=(rows // tm,),
        in_specs=[pl.BlockSpec((tm, d), lambda i: (i, 0)),
                  pl.BlockSpec((1, d), lambda i: (0, 0)),
                  pl.BlockSpec((tm, d), lambda i: (i, 0))],
        out_specs=[pl.BlockSpec((8, 128), lambda i: (0, 0)),
                   pl.BlockSpec((tm, d), lambda i: (i, 0)),
                   pl.BlockSpec((1, d), lambda i: (0, 0))],
        out_shape=[jax.ShapeDtypeStruct((8, 128), F32),
                   jax.ShapeDtypeStruct((rows, d), F32),
                   jax.ShapeDtypeStruct((1, d), F32)],
        compiler_params=_params("arbitrary"),
    )(h, g, tgt)


def _sb_block(qb, kb, i, jb, scale):
    bq, bk = qb.shape[0], kb.shape[0]
    z = _dot_nt(qb, kb) * scale
    qpos = i * bq + lax.broadcasted_iota(jnp.int32, (bq, bk), 0)
    kpos = jb * bk + lax.broadcasted_iota(jnp.int32, (bq, bk), 1)
    valid = (kpos < qpos) & (kpos >= META_PAD)
    e = jnp.exp(-jnp.abs(z))
    log_keep = jnp.where(valid, -(jnp.maximum(z, 0.0) + jnp.log(1.0 + e)), 0.0)
    return z, valid, e, log_keep


def _tri(n, cmp):
    r = lax.broadcasted_iota(jnp.int32, (n, n), 0)
    c = lax.broadcasted_iota(jnp.int32, (n, n), 1)
    return cmp(r, c).astype(BF16)


def _dot_split(x, t):
    hi, lo = _split2(x)
    return _dot(hi, t) + _dot(lo, t)


def _sb_fwd(name, q, k, v):
    nh, rows, dh = q.shape
    blk = ATT_BLOCK
    scale = dh ** -0.5

    def body(q_ref, k_ref, v_ref, o_ref, rt_ref):
        i = pl.program_id(1)
        qb = q_ref[0].astype(BF16)
        after = _tri(blk, lambda r, c: r > c)

        def step(n, carry):
            rest, acc = carry
            jb = i - n
            off = pl.multiple_of(jb * blk, blk)
            kb = k_ref[0, pl.ds(off, blk), :].astype(BF16)
            vb = v_ref[0, pl.ds(off, blk), :].astype(BF16)
            z, valid, _, log_keep = _sb_block(qb, kb, i, jb, scale)
            log_rest = rest + _dot_split(log_keep, after)
            attn = jnp.where(valid, jnp.exp(z + log_keep + log_rest), 0.0)
            acc = acc + _dot(attn.astype(BF16), vb)
            return rest + jnp.sum(log_keep, axis=-1, keepdims=True), acc

        rest, acc = lax.fori_loop(0, i + 1, step, (jnp.zeros((blk, 1), F32), jnp.zeros((blk, dh), F32)))
        o_ref[0] = acc
        rt_ref[0] = rest

    return pl.pallas_call(
        body, name=name, grid=(nh, rows // blk),
        in_specs=[pl.BlockSpec((1, blk, dh), lambda h, i: (h, i, 0)),
                  pl.BlockSpec((1, rows, dh), lambda h, i: (h, 0, 0)),
                  pl.BlockSpec((1, rows, dh), lambda h, i: (h, 0, 0))],
        out_specs=[pl.BlockSpec((1, blk, dh), lambda h, i: (h, i, 0)),
                   pl.BlockSpec((1, blk, 1), lambda h, i: (h, i, 0))],
        out_shape=[jax.ShapeDtypeStruct((nh, rows, dh), F32), jax.ShapeDtypeStruct((nh, rows, 1), F32)],
        compiler_params=_params("arbitrary", "arbitrary"),
    )(q, k, v)


def _sb_bwd(name, q, k, v, rt, do):
    nh, rows, dh = q.shape
    blk = ATT_BLOCK
    scale = dh ** -0.5

    def body(q_ref, k_ref, v_ref, rt_ref, do_ref, dq_ref, dk_ref, dv_ref):
        i = pl.program_id(1)

        @pl.when(i == 0)
        def _():
            dk_ref[...] = jnp.zeros_like(dk_ref)
            dv_ref[...] = jnp.zeros_like(dv_ref)

        qb = q_ref[0].astype(BF16)
        dob = do_ref[0].astype(BF16)
        total = rt_ref[0]
        upto = _tri(blk, lambda r, c: r <= c)
        before = _tri(blk, lambda r, c: r < c)

        def step(jb, carry):
            keep_sum, g_sum, dq = carry
            off = pl.multiple_of(jb * blk, blk)
            kb = k_ref[0, pl.ds(off, blk), :].astype(BF16)
            vb = v_ref[0, pl.ds(off, blk), :].astype(BF16)
            z, valid, e, log_keep = _sb_block(qb, kb, i, jb, scale)
            log_rest = total - keep_sum - _dot_split(log_keep, upto)
            attn = jnp.where(valid, jnp.exp(z + log_keep + log_rest), 0.0)
            g = attn * _dot_nt(dob, vb)
            g_before = g_sum + _dot_split(g, before)
            inv = 1.0 / (1.0 + e)
            sig = jnp.where(z >= 0, inv, e * inv)
            dz = (jnp.where(valid, g * (1.0 - sig) - g_before * sig, 0.0) * scale).astype(BF16)
            dk_ref[0, pl.ds(off, blk), :] += _dot_tn(dz, qb)
            dv_ref[0, pl.ds(off, blk), :] += _dot_tn(attn.astype(BF16), dob)
            return (keep_sum + jnp.sum(log_keep, axis=-1, keepdims=True),
                    g_sum + jnp.sum(g, axis=-1, keepdims=True),
                    dq + _dot(dz, kb))

        zero = jnp.zeros((blk, 1), F32)
        _, _, dq = lax.fori_loop(0, i + 1, step, (zero, zero, jnp.zeros((blk, dh), F32)))
        dq_ref[0] = dq

    return pl.pallas_call(
        body, name=name, grid=(nh, rows // blk),
        in_specs=[pl.BlockSpec((1, blk, dh), lambda h, i: (h, i, 0)),
                  pl.BlockSpec((1, rows, dh), lambda h, i: (h, 0, 0)),
                  pl.BlockSpec((1, rows, dh), lambda h, i: (h, 0, 0)),
                  pl.BlockSpec((1, blk, 1), lambda h, i: (h, i, 0)),
                  pl.BlockSpec((1, blk, dh), lambda h, i: (h, i, 0))],
        out_specs=[pl.BlockSpec((1, blk, dh), lambda h, i: (h, i, 0)),
                   pl.BlockSpec((1, rows, dh), lambda h, i: (h, 0, 0)),
                   pl.BlockSpec((1, rows, dh), lambda h, i: (h, 0, 0))],
        out_shape=[jax.ShapeDtypeStruct((nh, rows, dh), F32)] * 3,
        compiler_params=_params("arbitrary", "arbitrary"),
    )(q, k, v, rt, do)


def _head_sum(x, ones_bd):
    return _dot_split(x, ones_bd)


def _rwkv_pre(p, p_prev, mu, w0, a0, k_k, k_a, w_up, a_up, g_up, ones_bd):
    xs = p + (p_prev - p) * mu
    r = xs[:, :GROUP]
    k0 = xs[:, GROUP:2 * GROUP]
    v = xs[:, 2 * GROUP:3 * GROUP]
    lo = xs[:, 3 * GROUP:]
    wa = w0 + _dot(jnp.tanh(lo).astype(BF16), w_up.astype(BF16))
    w = -(jnp.maximum(-wa, 0.0) + jnp.log(1.0 + jnp.exp(-jnp.abs(wa)))) - 0.5
    log_decay = -jnp.exp(w)
    alpha = _sigmoid(a0 + _dot(lo.astype(BF16), a_up.astype(BF16)))
    gate = _dot(_sigmoid(lo).astype(BF16), g_up.astype(BF16))
    kk = k0 * k_k
    kk = kk * lax.rsqrt(jnp.maximum(_head_sum(kk * kk, ones_bd), 1e-24))
    k = k0 * (1.0 + (alpha - 1.0) * k_a)
    return r, log_decay, k, v, -kk, kk * alpha, gate


def _rwkv_post(y, r, k, v, gate, lnx_w, lnx_b, r_k, ones_bd):
    mean = _head_sum(y, ones_bd) * (1.0 / HEAD)
    yc = y - mean
    var = _head_sum(yc * yc, ones_bd) * (1.0 / HEAD)
    yn = yc * lax.rsqrt(var + LNX_EPS) * lnx_w + lnx_b
    bonus = _head_sum(r * k * r_k, ones_bd) * v
    return (yn + bonus) * gate


_PRE_VEC = 5
_PRE_MAT = 3


def _rwkv_pre_fwd(name, p, p_prev, vecs, mats, ones_bd):
    rows = p.shape[0]
    tm = _row_tile(rows)
    row_spec = lambda w: pl.BlockSpec((tm, w), lambda i: (i, 0))
    full = lambda a: pl.BlockSpec(a.shape, lambda i: (0,) * a.ndim)

    def body(p_ref, pp_ref, *refs):
        ins = [r[...] for r in refs[:_PRE_VEC + _PRE_MAT + 1]]
        outs = refs[_PRE_VEC + _PRE_MAT + 1:]
        for o_ref, val in zip(outs, _rwkv_pre(p_ref[...], pp_ref[...], *ins)):
            o_ref[...] = val

    return pl.pallas_call(
        body, name=name, grid=(rows // tm,),
        in_specs=[row_spec(RW_COLS), row_spec(RW_COLS)] + [full(a) for a in (*vecs, *mats, ones_bd)],
        out_specs=[row_spec(GROUP)] * 7,
        out_shape=[jax.ShapeDtypeStruct((rows, GROUP), F32)] * 7,
        compiler_params=_params("arbitrary"),
    )(p, p_prev, *vecs, *mats, ones_bd)


def _rwkv_pre_bwd(name, p, p_prev, vecs, mats, ones_bd, cts_a, cts_b):
    rows = p.shape[0]
    tm = _row_tile(rows)
    n_par = _PRE_VEC + _PRE_MAT
    row_spec = lambda w: pl.BlockSpec((tm, w), lambda i: (i, 0))
    full = lambda a: pl.BlockSpec(a.shape, lambda i: (0,) * a.ndim)

    def body(*refs):
        p_ref, pp_ref = refs[0], refs[1]
        par = [r[...] for r in refs[2:2 + n_par]]
        ones = refs[2 + n_par][...]
        cta = [r[...] for r in refs[3 + n_par:10 + n_par]]
        ctb = [r[...] for r in refs[10 + n_par:13 + n_par]]
        outs = refs[13 + n_par:]
        ct = (cta[0] + ctb[0], cta[1], cta[2] + ctb[1], cta[3] + ctb[2], cta[4], cta[5], cta[6])
        _, vjp = jax.vjp(lambda pv, ppv, *pr: _rwkv_pre(pv, ppv, *pr, ones), p_ref[...], pp_ref[...], *par)
        grads = vjp(ct)
        outs[0][...] = grads[0]
        outs[1][...] = grads[1]

        @pl.when(pl.program_id(0) == 0)
        def _():
            for o_ref in outs[2:]:
                o_ref[...] = jnp.zeros_like(o_ref)

        for o_ref, gval in zip(outs[2:], grads[2:]):
            o_ref[...] += gval

    par_arrays = (*vecs, *mats)
    return pl.pallas_call(
        body, name=name, grid=(rows // tm,),
        in_specs=([row_spec(RW_COLS)] * 2 + [full(a) for a in (*par_arrays, ones_bd)]
                  + [row_spec(GROUP)] * 10),
        out_specs=[row_spec(RW_COLS)] * 2 + [full(a) for a in par_arrays],
        out_shape=([jax.ShapeDtypeStruct((rows, RW_COLS), F32)] * 2
                   + [jax.ShapeDtypeStruct(a.shape, F32) for a in par_arrays]),
        compiler_params=_params("arbitrary"),
    )(p, p_prev, *par_arrays, ones_bd, *cts_a, *cts_b)


def _rwkv_post_fwd(name, y, r, k, v, gate, vecs, ones_bd):
    rows = y.shape[0]
    tm = _row_tile(rows)
    row_spec = pl.BlockSpec((tm, GROUP), lambda i: (i, 0))
    full = lambda a: pl.BlockSpec(a.shape, lambda i: (0,) * a.ndim)

    def body(*refs):
        vals = [r_[...] for r_ in refs[:-1]]
        refs[-1][...] = _rwkv_post(*vals)

    return pl.pallas_call(
        body, name=name, grid=(rows // tm,),
        in_specs=[row_spec] * 5 + [full(a) for a in (*vecs, ones_bd)],
        out_specs=row_spec,
        out_shape=jax.ShapeDtypeStruct((rows, GROUP), F32),
        compiler_params=_params("arbitrary"),
    )(y, r, k, v, gate, *vecs, ones_bd)


def _rwkv_post_bwd(name, y, r, k, v, gate, vecs, ones_bd, dout):
    rows = y.shape[0]
    tm = _row_tile(rows)
    row_spec = pl.BlockSpec((tm, GROUP), lambda i: (i, 0))
    full = lambda a: pl.BlockSpec(a.shape, lambda i: (0,) * a.ndim)

    def body(*refs):
        vals = [r_[...] for r_ in refs[:8]]
        ones = refs[8][...]
        dout_v = refs[9][...]
        outs = refs[10:]
        _, vjp = jax.vjp(lambda *a: _rwkv_post(*a, ones), *vals)
        grads = vjp(dout_v)
        for o_ref, gval in zip(outs[:5], grads[:5]):
            o_ref[...] = gval

        @pl.when(pl.program_id(0) == 0)
        def _():
            for o_ref in outs[5:]:
                o_ref[...] = jnp.zeros_like(o_ref)

        for o_ref, gval in zip(outs[5:], grads[5:]):
            o_ref[...] += gval

    return pl.pallas_call(
        body, name=name, grid=(rows // tm,),
        in_specs=[row_spec] * 5 + [full(a) for a in (*vecs, ones_bd)] + [row_spec],
        out_specs=[row_spec] * 5 + [full(a) for a in vecs],
        out_shape=[jax.ShapeDtypeStruct((rows, GROUP), F32)] * 5 + [jax.ShapeDtypeStruct(a.shape, F32) for a in vecs],
        compiler_params=_params("arbitrary"),
    )(y, r, k, v, gate, *vecs, ones_bd, dout)


_NN = (((2,), (1,)), ((0,), (0,)))
_NT = (((2,), (2,)), ((0,), (0,)))
_TN = (((1,), (1,)), ((0,), (0,)))


def _bmm(a, b, dims):
    ah, al = _split2(a)
    bh, bl = _split2(b)
    dot = lambda x, y: lax.dot_general(x, y, dims, preferred_element_type=F32)
    return dot(ah, bh) + (dot(ah, bl) + dot(al, bh))


def _chunk(state, r, log_w, k, v, a, b):
    nh, c, _ = r.shape
    row = lax.broadcasted_iota(jnp.int32, (c, c), 0)
    col = lax.broadcasted_iota(jnp.int32, (c, c), 1)
    cum = _bmm(jnp.broadcast_to((row >= col).astype(F32)[None], (nh, c, c)), log_w, _NN)
    mid = cum[:, c // 2 - 1:c // 2, :]
    a_t = a * jnp.exp(cum - log_w - mid)
    r_t = r * jnp.exp(cum - mid)
    back = jnp.exp(mid - cum)
    b_t = b * back
    k_t = k * back
    strict, incl = (row > col)[None], (row >= col)[None]
    n_mat = jnp.where(strict, _bmm(a_t, b_t, _NT), 0.0)
    m_mat = jnp.where(strict, _bmm(a_t, k_t, _NT), 0.0)
    p_mat = jnp.where(incl, _bmm(r_t, b_t, _NT), 0.0)
    q_mat = jnp.where(incl, _bmm(r_t, k_t, _NT), 0.0)
    inv, power, span = n_mat, n_mat, 1
    while 2 * span < c:
        power = _bmm(power, power, _NN)
        inv = inv + power + _bmm(inv, power, _NN)
        span *= 2
    s_mid = state * jnp.swapaxes(jnp.exp(mid), 1, 2)
    x = _bmm(a_t, s_mid, _NN) + _bmm(m_mat, v, _NN)
    u = x + _bmm(inv, x, _NN)
    y = _bmm(r_t, s_mid, _NN) + _bmm(p_mat, u, _NN) + _bmm(q_mat, v, _NN)
    s_new = (s_mid + _bmm(b_t, u, _TN) + _bmm(k_t, v, _TN)) * jnp.swapaxes(jnp.exp(cum[:, c - 1:c, :] - mid), 1, 2)
    return y, s_new


def _scan_fwd(name, ops):
    nh, rows, dh = ops[0].shape
    nc = rows // CHUNK
    spec = pl.BlockSpec((nh, CHUNK, dh), lambda c: (0, c, 0))

    def body(r_ref, w_ref, k_ref, v_ref, a_ref, b_ref, y_ref, st_ref, state):
        @pl.when(pl.program_id(0) == 0)
        def _():
            state[...] = jnp.zeros_like(state)

        st_ref[0] = state[...]
        y, s_new = _chunk(state[...], r_ref[...], w_ref[...], k_ref[...], v_ref[...], a_ref[...], b_ref[...])
        y_ref[...] = y
        state[...] = s_new

    return pl.pallas_call(
        body, name=name, grid=(nc,),
        in_specs=[spec] * 6,
        out_specs=[spec, pl.BlockSpec((1, nh, dh, dh), lambda c: (c, 0, 0, 0))],
        out_shape=[jax.ShapeDtypeStruct((nh, rows, dh), F32), jax.ShapeDtypeStruct((nc, nh, dh, dh), F32)],
        scratch_shapes=[pltpu.VMEM((nh, dh, dh), F32)],
        compiler_params=_params("arbitrary"),
    )(*ops)


def _scan_bwd(name, ops, states, dy):
    nh, rows, dh = ops[0].shape
    nc = rows // CHUNK
    spec = pl.BlockSpec((nh, CHUNK, dh), lambda c: (0, nc - 1 - c, 0))

    def body(r_ref, w_ref, k_ref, v_ref, a_ref, b_ref, st_ref, dy_ref, *rest):
        outs, dstate = rest[:6], rest[6]

        @pl.when(pl.program_id(0) == 0)
        def _():
            dstate[...] = jnp.zeros_like(dstate)

        _, vjp = jax.vjp(_chunk, st_ref[0], r_ref[...], w_ref[...], k_ref[...], v_ref[...], a_ref[...], b_ref[...])
        grads = vjp((dy_ref[...], dstate[...]))
        dstate[...] = grads[0]
        for o_ref, gval in zip(outs, grads[1:]):
            o_ref[...] = gval

    return pl.pallas_call(
        body, name=name, grid=(nc,),
        in_specs=[spec] * 6 + [pl.BlockSpec((1, nh, dh, dh), lambda c: (nc - 1 - c, 0, 0, 0)), spec],
        out_specs=[spec] * 6,
        out_shape=[jax.ShapeDtypeStruct((nh, rows, dh), F32)] * 6,
        scratch_shapes=[pltpu.VMEM((nh, dh, dh), F32)],
        compiler_params=_params("arbitrary"),
    )(*ops, states, dy)


def _heads(x):
    return x.reshape(x.shape[0], N_HEADS, HEAD).transpose(1, 0, 2)


def _unheads(x):
    return x.transpose(1, 0, 2).reshape(x.shape[1], GROUP)


def _shift_down(x):
    return jnp.concatenate([jnp.zeros((1, x.shape[1]), x.dtype), x[:-1]], axis=0)


def _shift_up(x):
    return jnp.concatenate([x[1:], jnp.zeros((1, x.shape[1]), x.dtype)], axis=0)


def _pad_rows(x, rows):
    return jnp.concatenate([x, jnp.zeros((rows - x.shape[0],) + x.shape[1:], x.dtype)], axis=0)


def _pad_cols(x, cols):
    return jnp.concatenate([x, jnp.zeros(x.shape[:-1] + (cols - x.shape[-1],), x.dtype)], axis=-1)


def _lora_pad(w_up, a_up, g_up):
    z = lambda n: jnp.zeros((n, GROUP), F32)
    return (jnp.concatenate([w_up, z(LORA_PAD - LORA_W)], 0),
            jnp.concatenate([z(LORA_W), a_up, z(LORA_PAD - LORA_W - LORA_A)], 0),
            jnp.concatenate([z(LORA_W + LORA_A), g_up, z(LORA_PAD - LORA_W - LORA_A - LORA_G)], 0))


def _local_step(x, tgt, w):
    d = x.shape[1]
    zeros = jnp.zeros((META_PAD, d), F32)
    h0 = jnp.concatenate([zeros, w["meta_tokens"], x], axis=0)
    tgt_p = jnp.concatenate([jnp.zeros((ROW0, d), F32), tgt], axis=0)
    ones_bd = (lax.broadcasted_iota(jnp.int32, (GROUP, GROUP), 0) // HEAD
               == lax.broadcasted_iota(jnp.int32, (GROUP, GROUP), 1) // HEAD).astype(BF16)
    w_in = _pad_cols(w["w_in"], IN_COLS_PAD)
    pre_vecs = (_pad_cols(w["rwkv_mu"], RW_COLS), w["rwkv_w0"], w["rwkv_a0"], w["rwkv_k_k"], w["rwkv_k_a"])
    pre_mats = _lora_pad(w["rwkv_w_up"], w["rwkv_a_up"], w["rwkv_g_up"])
    post_vecs = (w["rwkv_lnx_w"], w["rwkv_lnx_b"], w["rwkv_r_k"].reshape(1, GROUP))

    h1, a1, b1 = _ffn_fwd("ffn1_fwd", h0, w["ffn1_norm"], w["ffn1_w_gate"], w["ffn1_w_up"], w["ffn1_w_down"])
    proj, n2 = _norm_proj("in_proj", h1, w["mix_norm"], w_in)
    q, k, v = (_heads(proj[:, j * GROUP:(j + 1) * GROUP]) for j in range(3))
    sb, rest_total = _sb_fwd("sb_fwd", q, k, v)
    p = proj[:, 3 * GROUP:]
    p_prev = _shift_down(p)
    pre = _rwkv_pre_fwd("rwkv_pre_fwd", p, p_prev, pre_vecs, pre_mats, ones_bd)
    scan_ops = tuple(_heads(t) for t in pre[:6])
    y_h, states = _scan_fwd("rwkv_scan_fwd", scan_ops)
    y = _unheads(y_h)
    rw = _rwkv_post_fwd("rwkv_post_fwd", y, pre[0], pre[2], pre[3], pre[6], post_vecs, ones_bd)
    h2, mix = _out_proj("out_proj", h1, _unheads(sb), rw, w["w_out"])
    h3, a2, b2 = _ffn_fwd("ffn2_fwd", h2, w["ffn2_norm"], w["ffn2_w_gate"], w["ffn2_w_up"], w["ffn2_w_down"])
    loss8, dh3, g_final = _loss_head("loss_head", h3, w["final_norm"].reshape(1, d), tgt_p)

    g = {"final_norm": g_final.reshape(d)}
    dh2, da2, db2, s2, n3, dhh3, g["ffn2_norm"] = _ffn_bwd(
        "ffn2_bwd", dh3, h2, w["ffn2_norm"], a2, b2, w["ffn2_w_gate"], w["ffn2_w_up"], w["ffn2_w_down"])
    g["ffn2_w_gate"] = _mm_tn("ffn2_dgate", n3, da2)
    g["ffn2_w_up"] = _mm_tn("ffn2_dup", n3, db2)
    g["ffn2_w_down"] = _mm_tn("ffn2_ddown", s2, dhh3)
    dmix, dh2b = _out_proj_bwd("out_proj_bwd", dh2, w["w_out"])
    g["w_out"] = _mm_tn("out_proj_dw", mix, dh2b)
    dq, dk, dv = _sb_bwd("sb_bwd", q, k, v, rest_total, _heads(dmix[:, :GROUP]))
    post_g = _rwkv_post_bwd("rwkv_post_bwd", y, pre[0], pre[2], pre[3], pre[6], post_vecs, ones_bd, dmix[:, GROUP:])
    g["rwkv_lnx_w"], g["rwkv_lnx_b"] = post_g[5], post_g[6]
    g["rwkv_r_k"] = post_g[7].reshape(1, N_HEADS, HEAD)
    scan_g = _scan_bwd("rwkv_scan_bwd", scan_ops, states, _heads(post_g[0]))
    cts_a = tuple(_unheads(t) for t in scan_g) + (post_g[4],)
    pre_g = _rwkv_pre_bwd("rwkv_pre_bwd", p, p_prev, pre_vecs, pre_mats, ones_bd, cts_a, post_g[1:4])
    g["rwkv_mu"] = pre_g[2][:, :w["rwkv_mu"].shape[1]]
    g["rwkv_w0"], g["rwkv_a0"], g["rwkv_k_k"], g["rwkv_k_a"] = pre_g[3:7]
    g["rwkv_w_up"] = pre_g[7][:LORA_W]
    g["rwkv_a_up"] = pre_g[8][LORA_W:LORA_W + LORA_A]
    g["rwkv_g_up"] = pre_g[9][LORA_W + LORA_A:LORA_W + LORA_A + LORA_G]
    dp = pre_g[0] + _shift_up(pre_g[1])
    live = (jnp.arange(h0.shape[0]) >= META_PAD)[:, None]
    dproj = jnp.where(live, jnp.concatenate([_unheads(dq), _unheads(dk), _unheads(dv), dp], axis=1), 0.0).astype(BF16)
    g["w_in"] = _mm_tn("in_proj_dw", n2, dproj)[:, :w["w_in"].shape[1]]
    dh1, g["mix_norm"] = _norm_proj_bwd("in_proj_bwd", dproj, w_in, h1, w["mix_norm"], dh2)
    dh0, da1, db1, s1, n1, dhh1, g["ffn1_norm"] = _ffn_bwd(
        "ffn1_bwd", dh1, h0, w["ffn1_norm"], a1, b1, w["ffn1_w_gate"], w["ffn1_w_up"], w["ffn1_w_down"])
    g["ffn1_w_gate"] = _mm_tn("ffn1_dgate", n1, da1)
    g["ffn1_w_up"] = _mm_tn("ffn1_dup", n1, db1)
    g["ffn1_w_down"] = _mm_tn("ffn1_ddown", s1, dhh1)
    g["meta_tokens"] = dh0[META_PAD:ROW0]
    return loss8[0, 0], dh0[ROW0:], g


N_CHIPS = 4
N_DEV = 8
HBM = pl.BlockSpec(memory_space=pltpu.HBM)


def _place():
    return lax.axis_index("x"), lax.axis_index("y"), lax.axis_index("c")


def _other_chips(x, y):
    return [(1 - x, y), (x, 1 - y), (1 - x, 1 - y)]


def _gather_shards(name, shards):
    n = len(shards)

    def body(*refs):
        ins, outs = refs[:n], refs[n:2 * n]
        send, recv, local = refs[2 * n:]
        x, y, c = _place()
        me = 2 * x + y
        chips = _other_chips(x, y)
        own = [pltpu.make_async_copy(ins[k], outs[k].at[me], local.at[k]) for k in range(n)]
        for cp in own:
            cp.start()

        def copy(j, k, slot):
            return pltpu.make_async_remote_copy(
                src_ref=ins[k], dst_ref=outs[k].at[slot], send_sem=send.at[j * n + k], recv_sem=recv.at[j * n + k],
                device_id=(chips[j][0], chips[j][1], c), device_id_type=MESH)

        sent = [copy(j, k, me) for j in range(3) for k in range(n)]
        for cp in sent:
            cp.start()
        for j in range(3):
            for k in range(n):
                copy(j, k, 2 * chips[j][0] + chips[j][1]).wait_recv()
        for cp in sent:
            cp.wait_send()
        for cp in own:
            cp.wait()

    return pl.pallas_call(
        body, name=name,
        in_specs=[HBM] * n, out_specs=[HBM] * n,
        out_shape=[jax.ShapeDtypeStruct((N_CHIPS,) + s.shape, s.dtype) for s in shards],
        scratch_shapes=[pltpu.SemaphoreType.DMA((3 * n,)), pltpu.SemaphoreType.DMA((3 * n,)),
                        pltpu.SemaphoreType.DMA((n,))],
    )(*shards)


def _reduce_shards(name, parts):
    n = len(parts)

    def body(*refs):
        ins, got, sib = refs[:n], refs[n:2 * n], refs[2 * n:3 * n]
        send, recv, local, d2d_send, d2d_recv = refs[3 * n:]
        x, y, c = _place()
        me = 2 * x + y
        chips = _other_chips(x, y)
        own = [pltpu.make_async_copy(ins[k].at[me], got[k].at[me], local.at[k]) for k in range(n)]
        for cp in own:
            cp.start()

        def copy(j, k, shard, slot):
            return pltpu.make_async_remote_copy(
                src_ref=ins[k].at[shard], dst_ref=got[k].at[slot], send_sem=send.at[j * n + k],
                recv_sem=recv.at[j * n + k], device_id=(chips[j][0], chips[j][1], c), device_id_type=MESH)

        sent = [copy(j, k, 2 * chips[j][0] + chips[j][1], me) for j in range(3) for k in range(n)]
        for cp in sent:
            cp.start()

        def swap(k):
            return pltpu.make_async_remote_copy(
                src_ref=got[k], dst_ref=sib[k], send_sem=d2d_send.at[k], recv_sem=d2d_recv.at[k],
                device_id=(x, y, 1 - c), device_id_type=MESH)

        for k in range(n):
            own[k].wait()
            for j in range(3):
                copy(j, k, me, 2 * chips[j][0] + chips[j][1]).wait_recv()
            swap(k).start()
        for k in range(n):
            swap(k).wait_recv()
        for cp in sent:
            cp.wait_send()
        for k in range(n):
            swap(k).wait_send()

    return pl.pallas_call(
        body, name=name,
        in_specs=[HBM] * n, out_specs=[HBM] * (2 * n),
        out_shape=[jax.ShapeDtypeStruct(s.shape, s.dtype) for s in parts] * 2,
        scratch_shapes=[pltpu.SemaphoreType.DMA((3 * n,)), pltpu.SemaphoreType.DMA((3 * n,)),
                        pltpu.SemaphoreType.DMA((n,)), pltpu.SemaphoreType.DMA((n,)), pltpu.SemaphoreType.DMA((n,))],
    )(*parts)


def _all_reduce_small(name, vec):
    rows = vec.shape[0]

    def body(v_ref, o_ref, buf, send, recv):
        x, y, c = _place()
        me = 4 * x + 2 * y + c
        peers = [(x ^ (r >> 2), y ^ ((r >> 1) & 1), c ^ (r & 1)) for r in range(1, N_DEV)]

        def copy(r, slot):
            px, py, pc = peers[r]
            return pltpu.make_async_remote_copy(
                src_ref=v_ref, dst_ref=buf.at[slot], send_sem=send.at[r], recv_sem=recv.at[r],
                device_id=(px, py, pc), device_id_type=MESH)

        sent = [copy(r, me) for r in range(N_DEV - 1)]
        for cp in sent:
            cp.start()
        buf[me] = v_ref[...]
        for r in range(N_DEV - 1):
            px, py, pc = peers[r]
            copy(r, 4 * px + 2 * py + pc).wait_recv()
        total = buf[0]
        for dev in range(1, N_DEV):
            total = total + buf[dev]
        o_ref[...] = total
        for cp in sent:
            cp.wait_send()

    return pl.pallas_call(
        body, name=name,
        in_specs=[pl.BlockSpec(memory_space=pltpu.VMEM)], out_specs=pl.BlockSpec(memory_space=pltpu.VMEM),
        out_shape=jax.ShapeDtypeStruct(vec.shape, F32),
        scratch_shapes=[pltpu.VMEM((N_DEV, rows, 128), F32),
                        pltpu.SemaphoreType.DMA((N_DEV - 1,)), pltpu.SemaphoreType.DMA((N_DEV - 1,))],
        compiler_params=pltpu.CompilerParams(vmem_limit_bytes=VMEM_LIMIT),
    )(vec)


def _adamw(w, g, m, v):
    m = ADAM_B1 * m + (1.0 - ADAM_B1) * g
    v = ADAM_B2 * v + (1.0 - ADAM_B2) * (g * g)
    m_hat = m / (1.0 - ADAM_B1 ** ADAM_STEP)
    v_hat = v / (1.0 - ADAM_B2 ** ADAM_STEP)
    return -ADAM_LR * (m_hat / (jnp.sqrt(v_hat) + ADAM_EPS) + ADAM_WD * w), m, v


def _adamw_shard(name, w, m, v, got, sib):
    rows, cols = w.shape
    tr = rows // 4
    spec = pl.BlockSpec((tr, cols), lambda i: (i, 0))
    spec4 = pl.BlockSpec((N_CHIPS, tr, cols), lambda i: (0, i, 0))

    def body(w_ref, m_ref, v_ref, got_ref, sib_ref, g_ref, d_ref, mo_ref, vo_ref):
        def four(ref):
            return ((ref[0].astype(F32) + ref[1].astype(F32)) + ref[2].astype(F32)) + ref[3].astype(F32)

        g = four(got_ref) + four(sib_ref)
        g_ref[...] = g
        d_ref[...], mo_ref[...], vo_ref[...] = _adamw(w_ref[...], g, m_ref[...], v_ref[...])

    return pl.pallas_call(
        body, name=name, grid=(4,),
        in_specs=[spec, spec, spec, spec4, spec4], out_specs=[spec] * 4,
        out_shape=[jax.ShapeDtypeStruct((rows, cols), F32)] * 4,
        compiler_params=_params("arbitrary"),
    )(w, m, v, got, sib)


def _adamw_small(name, w, m, v, g):
    def body(w_ref, m_ref, v_ref, g_ref, d_ref, mo_ref, vo_ref):
        d_ref[...], mo_ref[...], vo_ref[...] = _adamw(w_ref[...], g_ref[...], m_ref[...], v_ref[...])

    return pl.pallas_call(body, name=name, out_shape=[jax.ShapeDtypeStruct(w.shape, F32)] * 3)(w, m, v, g)


def _pack(arrays, rows):
    flat = jnp.concatenate([a.reshape(-1) for a in arrays])
    return jnp.concatenate([flat, jnp.zeros((rows * 128 - flat.shape[0],), F32)]).reshape(rows, 128)


def _unpack(packed, shapes):
    flat, out, at = packed.reshape(-1), [], 0
    for s in shapes:
        size = 1
        for dim in s:
            size *= dim
        out.append(flat[at:at + size].reshape(s))
        at += size
    return out


def _rows_for(shapes):
    total = 0
    for s in shapes:
        size = 1
        for dim in s:
            size *= dim
        total += size
    return -(-total // 1024) * 8


WEIGHTS = ['meta_tokens', 'ffn1_norm', 'ffn1_w_gate', 'ffn1_w_up', 'ffn1_w_down', 'mix_norm', 'w_in', 'rwkv_mu',
           'rwkv_w0', 'rwkv_w_up', 'rwkv_a0', 'rwkv_a_up', 'rwkv_g_up', 'rwkv_k_k', 'rwkv_k_a', 'rwkv_r_k',
           'rwkv_lnx_w', 'rwkv_lnx_b', 'w_out', 'ffn2_norm', 'ffn2_w_gate', 'ffn2_w_up', 'ffn2_w_down', 'final_norm']
COL_CUT = ['ffn1_w_gate', 'ffn1_w_up', 'w_in', 'ffn2_w_gate', 'ffn2_w_up']
ROW_CUT = ['ffn1_w_down', 'w_out', 'ffn2_w_down']
SMALL_CUT = ['meta_tokens', 'rwkv_w_up', 'rwkv_a_up', 'rwkv_g_up']
BIG = COL_CUT + ROW_CUT
REPLICATED = [n for n in WEIGHTS if n not in BIG + SMALL_CUT]


def _join_cols(a):
    return a.transpose(1, 0, 2).reshape(a.shape[1], N_CHIPS * a.shape[2])


def _cut_cols(a):
    return a.reshape(a.shape[0], N_CHIPS, a.shape[1] // N_CHIPS).transpose(1, 0, 2)


def _step(x, loss_target, w, m, v):
    two = lambda a: a.reshape(a.shape[-2], a.shape[-1])

    names = BIG + SMALL_CUT
    shards = [two(w[n]).astype(BF16) for n in BIG] + [two(w[n]) for n in SMALL_CUT]
    gathered = dict(zip(names, _gather_shards("gather_weights", shards)))
    full = {n: (two(w[n]) if w[n].ndim == 3 else w[n]) for n in REPLICATED}
    for n in COL_CUT + SMALL_CUT:
        full[n] = _join_cols(gathered[n])
    for n in ROW_CUT:
        full[n] = gathered[n].reshape(-1, gathered[n].shape[-1])
    full["rwkv_r_k"] = w["rwkv_r_k"]
    full["final_norm"] = w["final_norm"]

    loss, dx, g = _local_step(x[0], loss_target[0], full)
    loss = lax.psum(loss, ("x", "y", "c"))

    parts = [_cut_cols(g[n]).astype(BF16) for n in COL_CUT]
    parts += [g[n].reshape(N_CHIPS, -1, g[n].shape[-1]).astype(BF16) for n in ROW_CUT]
    reduced = _reduce_shards("reduce_gradients", parts)
    got, sib = dict(zip(BIG, reduced[:len(BIG)])), dict(zip(BIG, reduced[len(BIG):]))

    small_names = REPLICATED + SMALL_CUT
    small_shapes = [g[n].shape for n in small_names]
    small = _all_reduce_small("reduce_small", _pack([g[n] for n in small_names], _rows_for(small_shapes)))
    g_small = dict(zip(small_names, _unpack(small, small_shapes)))
    chip = 2 * lax.axis_index("x") + lax.axis_index("y")
    for n in SMALL_CUT:
        width = g_small[n].shape[1] // N_CHIPS
        g_small[n] = lax.dynamic_slice_in_dim(g_small[n], chip * width, width, axis=1)

    grad, delta, new_m, new_v = {}, {}, {}, {}
    for n in BIG:
        outs = _adamw_shard("adamw_" + n, two(w[n]), two(m[n]), two(v[n]), got[n], sib[n])
        grad[n], delta[n], new_m[n], new_v[n] = (o.reshape(w[n].shape) for o in outs)
    shapes = [w[n].shape for n in small_names]
    rows = _rows_for(shapes)
    packed = [_pack([t[n] for n in small_names], rows) for t in (w, m, v)]
    g_packed = _pack([g_small[n] for n in small_names], rows)
    outs = [_unpack(o, shapes) for o in _adamw_small("adamw_small", *packed, g_packed)]
    for i, n in enumerate(small_names):
        grad[n] = g_small[n].reshape(w[n].shape)
        delta[n], new_m[n], new_v[n] = outs[0][i], outs[1][i], outs[2][i]
    return loss, dx[None], grad, delta, new_m, new_v


def kernel(x, meta_tokens, ffn1_norm, ffn1_w_gate, ffn1_w_up, ffn1_w_down, mix_norm, w_in, rwkv_mu, rwkv_w0, rwkv_w_up, rwkv_a0, rwkv_a_up, rwkv_g_up, rwkv_k_k, rwkv_k_a, rwkv_r_k, rwkv_lnx_w, rwkv_lnx_b, w_out, ffn2_norm, ffn2_w_gate, ffn2_w_up, ffn2_w_down, final_norm, loss_target, m_meta_tokens, m_ffn1_norm, m_ffn1_w_gate, m_ffn1_w_up, m_ffn1_w_down, m_mix_norm, m_w_in, m_rwkv_mu, m_rwkv_w0, m_rwkv_w_up, m_rwkv_a0, m_rwkv_a_up, m_rwkv_g_up, m_rwkv_k_k, m_rwkv_k_a, m_rwkv_r_k, m_rwkv_lnx_w, m_rwkv_lnx_b, m_w_out, m_ffn2_norm, m_ffn2_w_gate, m_ffn2_w_up, m_ffn2_w_down, m_final_norm, v_meta_tokens, v_ffn1_norm, v_ffn1_w_gate, v_ffn1_w_up, v_ffn1_w_down, v_mix_norm, v_w_in, v_rwkv_mu, v_rwkv_w0, v_rwkv_w_up, v_rwkv_a0, v_rwkv_a_up, v_rwkv_g_up, v_rwkv_k_k, v_rwkv_k_a, v_rwkv_r_k, v_rwkv_lnx_w, v_rwkv_lnx_b, v_w_out, v_ffn2_norm, v_ffn2_w_gate, v_ffn2_w_up, v_ffn2_w_down, v_final_norm):
    w = dict(zip(WEIGHTS, (meta_tokens, ffn1_norm, ffn1_w_gate, ffn1_w_up, ffn1_w_down, mix_norm, w_in, rwkv_mu, rwkv_w0, rwkv_w_up, rwkv_a0, rwkv_a_up, rwkv_g_up, rwkv_k_k, rwkv_k_a, rwkv_r_k, rwkv_lnx_w, rwkv_lnx_b, w_out, ffn2_norm, ffn2_w_gate, ffn2_w_up, ffn2_w_down, final_norm)))
    m = dict(zip(WEIGHTS, (m_meta_tokens, m_ffn1_norm, m_ffn1_w_gate, m_ffn1_w_up, m_ffn1_w_down, m_mix_norm, m_w_in, m_rwkv_mu, m_rwkv_w0, m_rwkv_w_up, m_rwkv_a0, m_rwkv_a_up, m_rwkv_g_up, m_rwkv_k_k, m_rwkv_k_a, m_rwkv_r_k, m_rwkv_lnx_w, m_rwkv_lnx_b, m_w_out, m_ffn2_norm, m_ffn2_w_gate, m_ffn2_w_up, m_ffn2_w_down, m_final_norm)))
    v = dict(zip(WEIGHTS, (v_meta_tokens, v_ffn1_norm, v_ffn1_w_gate, v_ffn1_w_up, v_ffn1_w_down, v_mix_norm, v_w_in, v_rwkv_mu, v_rwkv_w0, v_rwkv_w_up, v_rwkv_a0, v_rwkv_a_up, v_rwkv_g_up, v_rwkv_k_k, v_rwkv_k_a, v_rwkv_r_k, v_rwkv_lnx_w, v_rwkv_lnx_b, v_w_out, v_ffn2_norm, v_ffn2_w_gate, v_ffn2_w_up, v_ffn2_w_down, v_final_norm)))
    loss, grad_x, grad, delta, new_m, new_v = _step(x, loss_target, w, m, v)
    return (loss, grad_x, *[grad[n] for n in WEIGHTS], *[delta[n] for n in WEIGHTS],
            *[new_m[n] for n in WEIGHTS], *[new_v[n] for n in WEIGHTS])
```

```python
import functools

import jax
import jax.numpy as jnp
from jax import lax
from jax.experimental import pallas as pl
from jax.experimental.pallas import tpu as pltpu

F32 = jnp.float32
BF16 = jnp.bfloat16

RMS_EPS = 1e-6
LNX_EPS = 64e-5
N_META = 16
ROW0 = 128
META_PAD = ROW0 - N_META
HEAD = 64
N_HEADS = 8
GROUP = N_HEADS * HEAD
LORA_W, LORA_A, LORA_G = 32, 32, 96
LORA_PAD = 256
RW_COLS = 3 * GROUP + LORA_PAD
IN_COLS_PAD = 3 * GROUP + RW_COLS
ATT_BLOCK = 128
CHUNK = 64
VMEM_LIMIT = 56 * 1024 * 1024

ADAM_LR, ADAM_B1, ADAM_B2, ADAM_EPS, ADAM_WD, ADAM_STEP = 0.001, 0.9, 0.999, 1e-08, 0.01, 10

MESH = pl.DeviceIdType.MESH


def _params(*sem):
    return pltpu.CompilerParams(dimension_semantics=tuple(sem), vmem_limit_bytes=VMEM_LIMIT)


def _dot(a, b):
    return lax.dot_general(a, b, (((1,), (0,)), ((), ())), preferred_element_type=F32)


def _dot_nt(a, b):
    return lax.dot_general(a, b, (((1,), (1,)), ((), ())), preferred_element_type=F32)


def _dot_tn(a, b):
    return lax.dot_general(a, b, (((0,), (0,)), ((), ())), preferred_element_type=F32)


def _split2(x):
    hi = x.astype(BF16)
    return hi, (x - hi.astype(F32)).astype(BF16)


def _sigmoid(x):
    return 1.0 / (1.0 + jnp.exp(-x))


def _rms_fwd(x, g):
    rstd = lax.rsqrt(jnp.mean(x * x, axis=-1, keepdims=True) + RMS_EPS)
    xhat = x * rstd
    return xhat * g, xhat, rstd


def _rms_bwd(dn, xhat, rstd, g):
    dxhat = dn * g
    dx = rstd * (dxhat - xhat * jnp.mean(dxhat * xhat, axis=-1, keepdims=True))
    return dx, jnp.sum(dn * xhat, axis=0, keepdims=True)


def _row_tile(rows):
    return 384 if rows % 384 == 0 else 128


def _half_tile(cols):
    return cols // 2 if cols % 256 == 0 else cols


def _ffn_fwd(name, h, g, wg, wu, wd):
    rows, d = h.shape
    f = wg.shape[1]
    tm, tf = _row_tile(rows), _half_tile(f)
    nj = f // tf

    def body(h_ref, g_ref, wg_ref, wu_ref, wd_ref, ho_ref, a_ref, b_ref, n_sc, acc_sc):
        j = pl.program_id(1)

        @pl.when(j == 0)
        def _():
            n, _, _ = _rms_fwd(h_ref[...], g_ref[...])
            n_sc[...] = n.astype(BF16)
            acc_sc[...] = jnp.zeros_like(acc_sc)

        n = n_sc[...]
        a = _dot(n, wg_ref[...])
        b = _dot(n, wu_ref[...])
        a_ref[...] = a
        b_ref[...] = b
        s = a * _sigmoid(a) * b
        acc_sc[...] += _dot(s.astype(BF16), wd_ref[...])

        @pl.when(j == nj - 1)
        def _():
            ho_ref[...] = h_ref[...] + 0.5 * acc_sc[...]

    return pl.pallas_call(
        body, name=name, grid=(rows // tm, nj),
        in_specs=[pl.BlockSpec((tm, d), lambda i, j: (i, 0)),
                  pl.BlockSpec((1, d), lambda i, j: (0, 0)),
                  pl.BlockSpec((d, tf), lambda i, j: (0, j)),
                  pl.BlockSpec((d, tf), lambda i, j: (0, j)),
                  pl.BlockSpec((tf, d), lambda i, j: (j, 0))],
        out_specs=[pl.BlockSpec((tm, d), lambda i, j: (i, 0)),
                   pl.BlockSpec((tm, tf), lambda i, j: (i, j)),
                   pl.BlockSpec((tm, tf), lambda i, j: (i, j))],
        out_shape=[jax.ShapeDtypeStruct((rows, d), F32),
                   jax.ShapeDtypeStruct((rows, f), F32),
                   jax.ShapeDtypeStruct((rows, f), F32)],
        scratch_shapes=[pltpu.VMEM((tm, d), BF16), pltpu.VMEM((tm, d), F32)],
        compiler_params=_params("arbitrary", "arbitrary"),
    )(h, g, wg, wu, wd)


def _ffn_bwd(name, dh, h, g, a, b, wg, wu, wd):
    rows, d = h.shape
    f = wg.shape[1]
    tm, tf = _row_tile(rows), _half_tile(f)
    ni, nj = rows // tm, f // tf

    def body(dh_ref, h_ref, g_ref, a_ref, b_ref, wg_ref, wu_ref, wd_ref,
             dhi_ref, da_ref, db_ref, s_ref, n_ref, dhh_ref, dg_ref, dn_sc):
        i, j = pl.program_id(0), pl.program_id(1)

        @pl.when(j == 0)
        def _():
            n, _, _ = _rms_fwd(h_ref[...], g_ref[...])
            n_ref[...] = n.astype(BF16)
            dhh_ref[...] = (0.5 * dh_ref[...]).astype(BF16)
            dn_sc[...] = jnp.zeros_like(dn_sc)

        @pl.when((i == 0) & (j == 0))
        def _():
            dg_ref[...] = jnp.zeros_like(dg_ref)

        ds = _dot_nt(dhh_ref[...], wd_ref[...])
        av, bv = a_ref[...], b_ref[...]
        sig = _sigmoid(av)
        silu = av * sig
        s_ref[...] = (silu * bv).astype(BF16)
        db = (ds * silu).astype(BF16)
        da = (ds * bv * (sig * (1.0 + av * (1.0 - sig)))).astype(BF16)
        da_ref[...] = da
        db_ref[...] = db
        dn_sc[...] += _dot_nt(da, wg_ref[...]) + _dot_nt(db, wu_ref[...])

        @pl.when(j == nj - 1)
        def _():
            gv = g_ref[...]
            _, xhat, rstd = _rms_fwd(h_ref[...], gv)
            dx, dg = _rms_bwd(dn_sc[...], xhat, rstd, gv)
            dhi_ref[...] = dh_ref[...] + dx
            dg_ref[...] += dg

    return pl.pallas_call(
        body, name=name, grid=(ni, nj),
        in_specs=[pl.BlockSpec((tm, d), lambda i, j: (i, 0)),
                  pl.BlockSpec((tm, d), lambda i, j: (i, 0)),
                  pl.BlockSpec((1, d), lambda i, j: (0, 0)),
                  pl.BlockSpec((tm, tf), lambda i, j: (i, j)),
                  pl.BlockSpec((tm, tf), lambda i, j: (i, j)),
                  pl.BlockSpec((d, tf), lambda i, j: (0, j)),
                  pl.BlockSpec((d, tf), lambda i, j: (0, j)),
                  pl.BlockSpec((tf, d), lambda i, j: (j, 0))],
        out_specs=[pl.BlockSpec((tm, d), lambda i, j: (i, 0)),
                   pl.BlockSpec((tm, tf), lambda i, j: (i, j)),
                   pl.BlockSpec((tm, tf), lambda i, j: (i, j)),
                   pl.BlockSpec((tm, tf), lambda i, j: (i, j)),
                   pl.BlockSpec((tm, d), lambda i, j: (i, 0)),
                   pl.BlockSpec((tm, d), lambda i, j: (i, 0)),
                   pl.BlockSpec((1, d), lambda i, j: (0, 0))],
        out_shape=[jax.ShapeDtypeStruct((rows, d), F32),
                   jax.ShapeDtypeStruct((rows, f), BF16),
                   jax.ShapeDtypeStruct((rows, f), BF16),
                   jax.ShapeDtypeStruct((rows, f), BF16),
                   jax.ShapeDtypeStruct((rows, d), BF16),
                   jax.ShapeDtypeStruct((rows, d), BF16),
                   jax.ShapeDtypeStruct((1, d), F32)],
        scratch_shapes=[pltpu.VMEM((tm, d), F32)],
        compiler_params=_params("arbitrary", "arbitrary"),
    )(dh, h, g, a, b, wg, wu, wd)


def _mm_tn(name, a, b):
    k, m = a.shape
    n = b.shape[1]
    tk = _row_tile(k)
    tm = _half_tile(m) if m > 1024 else m
    tn = _half_tile(n) if n > 1024 else n

    def body(a_ref, b_ref, o_ref):
        @pl.when(pl.program_id(2) == 0)
        def _():
            o_ref[...] = jnp.zeros_like(o_ref)

        o_ref[...] += _dot_tn(a_ref[...], b_ref[...])

    return pl.pallas_call(
        body, name=name, grid=(m // tm, n // tn, k // tk),
        in_specs=[pl.BlockSpec((tk, tm), lambda i, j, kk: (kk, i)),
                  pl.BlockSpec((tk, tn), lambda i, j, kk: (kk, j))],
        out_specs=pl.BlockSpec((tm, tn), lambda i, j, kk: (i, j)),
        out_shape=jax.ShapeDtypeStruct((m, n), F32),
        compiler_params=_params("arbitrary", "arbitrary", "arbitrary"),
    )(a, b)


def _norm_proj(name, h, g, w):
    rows, d = h.shape
    n = w.shape[1]
    tm, tn = _row_tile(rows), _half_tile(n)

    def body(h_ref, g_ref, w_ref, o_ref, n_ref):
        @pl.when(pl.program_id(1) == 0)
        def _():
            nv, _, _ = _rms_fwd(h_ref[...], g_ref[...])
            n_ref[...] = nv.astype(BF16)

        o_ref[...] = _dot(n_ref[...], w_ref[...])

    return pl.pallas_call(
        body, name=name, grid=(rows // tm, n // tn),
        in_specs=[pl.BlockSpec((tm, d), lambda i, j: (i, 0)),
                  pl.BlockSpec((1, d), lambda i, j: (0, 0)),
                  pl.BlockSpec((d, tn), lambda i, j: (0, j))],
        out_specs=[pl.BlockSpec((tm, tn), lambda i, j: (i, j)),
                   pl.BlockSpec((tm, d), lambda i, j: (i, 0))],
        out_shape=[jax.ShapeDtypeStruct((rows, n), F32), jax.ShapeDtypeStruct((rows, d), BF16)],
        compiler_params=_params("arbitrary", "arbitrary"),
    )(h, g, w)


def _out_proj(name, h, sb, rw, w):
    rows, d = h.shape
    gw = sb.shape[1]
    tm = _row_tile(rows)

    def body(h_ref, sb_ref, rw_ref, w_ref, o_ref, mix_ref):
        mix_ref[:, :gw] = sb_ref[...].astype(BF16)
        mix_ref[:, gw:] = rw_ref[...].astype(BF16)
        o_ref[...] = h_ref[...] + _dot(mix_ref[...], w_ref[...])

    return pl.pallas_call(
        body, name=name, grid=(rows // tm,),
        in_specs=[pl.BlockSpec((tm, d), lambda i: (i, 0)),
                  pl.BlockSpec((tm, gw), lambda i: (i, 0)),
                  pl.BlockSpec((tm, gw), lambda i: (i, 0)),
                  pl.BlockSpec((2 * gw, d), lambda i: (0, 0))],
        out_specs=[pl.BlockSpec((tm, d), lambda i: (i, 0)),
                   pl.BlockSpec((tm, 2 * gw), lambda i: (i, 0))],
        out_shape=[jax.ShapeDtypeStruct((rows, d), F32), jax.ShapeDtypeStruct((rows, 2 * gw), BF16)],
        compiler_params=_params("arbitrary"),
    )(h, sb, rw, w)


def _out_proj_bwd(name, dh, w):
    rows, d = dh.shape
    k = w.shape[0]
    tm = _row_tile(rows)

    def body(dh_ref, w_ref, o_ref, dhb_ref):
        dhb = dh_ref[...].astype(BF16)
        dhb_ref[...] = dhb
        o_ref[...] = _dot_nt(dhb, w_ref[...])

    return pl.pallas_call(
        body, name=name, grid=(rows // tm,),
        in_specs=[pl.BlockSpec((tm, d), lambda i: (i, 0)),
                  pl.BlockSpec((k, d), lambda i: (0, 0))],
        out_specs=[pl.BlockSpec((tm, k), lambda i: (i, 0)),
                   pl.BlockSpec((tm, d), lambda i: (i, 0))],
        out_shape=[jax.ShapeDtypeStruct((rows, k), F32), jax.ShapeDtypeStruct((rows, d), BF16)],
        compiler_params=_params("arbitrary"),
    )(dh, w)


def _norm_proj_bwd(name, dproj, w, h, g, dh):
    rows, n = dproj.shape
    d = w.shape[0]
    tm = _row_tile(rows)

    def body(dp_ref, w_ref, h_ref, g_ref, dh_ref, o_ref, dg_ref):
        @pl.when(pl.program_id(0) == 0)
        def _():
            dg_ref[...] = jnp.zeros_like(dg_ref)

        dn = _dot_nt(dp_ref[...], w_ref[...])
        gv = g_ref[...]
        _, xhat, rstd = _rms_fwd(h_ref[...], gv)
        dx, dg = _rms_bwd(dn, xhat, rstd, gv)
        o_ref[...] = dh_ref[...] + dx
        dg_ref[...] += dg

    return pl.pallas_call(
        body, name=name, grid=(rows // tm,),
        in_specs=[pl.BlockSpec((tm, n), lambda i: (i, 0)),
                  pl.BlockSpec((d, n), lambda i: (0, 0)),
                  pl.BlockSpec((tm, d), lambda i: (i, 0)),
                  pl.BlockSpec((1, d), lambda i: (0, 0)),
                  pl.BlockSpec((tm, d), lambda i: (i, 0))],
        out_specs=[pl.BlockSpec((tm, d), lambda i: (i, 0)),
                   pl.BlockSpec((1, d), lambda i: (0, 0))],
        out_shape=[jax.ShapeDtypeStruct((rows, d), F32), jax.ShapeDtypeStruct((1, d), F32)],
        compiler_params=_params("arbitrary"),
    )(dproj, w, h, g, dh)


def _loss_head(name, h, g, tgt):
    rows, d = h.shape
    tm = _row_tile(rows)

    def body(h_ref, g_ref, t_ref, loss_ref, dh_ref, dg_ref):
        i = pl.program_id(0)

        @pl.when(i == 0)
        def _():
            loss_ref[...] = jnp.zeros_like(loss_ref)
            dg_ref[...] = jnp.zeros_like(dg_ref)

        gv = g_ref[...]
        y, xhat, rstd = _rms_fwd(h_ref[...], gv)
        row = i * tm + lax.broadcasted_iota(jnp.int32, (tm, 1), 0)
        diff = jnp.where(row >= ROW0, y - t_ref[...], 0.0)
        part = 0.5 * jnp.sum(jnp.sum(diff * diff, axis=-1, keepdims=True), axis=0, keepdims=True) / d
        loss_ref[...] += jnp.broadcast_to(part, loss_ref.shape)
        dx, dg = _rms_bwd(diff / d, xhat, rstd, gv)
        dh_ref[...] = dx
        dg_ref[...] += dg

    return pl.pallas_call(
        body, name=name, grid=(rows // tm,),
        in_specs=[pl.BlockSpec((tm, d), lambda i: (i, 0)),
                  pl.BlockSpec((1, d), lambda i: (0, 0)),
                  pl.BlockSpec((tm, d), lambda i: (i, 0))],
        out_specs=[pl.BlockSpec((8, 128), lambda i: (0, 0)),
                   pl.BlockSpec((tm, d), lambda i: (i, 0)),
                   pl.BlockSpec((1, d), lambda i: (0, 0))],
        out_shape=[jax.ShapeDtypeStruct((8, 128), F32),
                   jax.ShapeDtypeStruct((rows, d), F32),
                   jax.ShapeDtypeStruct((1, d), F32)],
        compiler_params=_params("arbitrary"),
    )(h, g, tgt)


def _sb_block(qb, kb, i, jb, scale):
    bq, bk = qb.shape[0], kb.shape[0]
    z = _dot_nt(qb, kb) * scale
    qpos = i * bq + lax.broadcasted_iota(jnp.int32, (bq, bk), 0)
    kpos = jb * bk + lax.broadcasted_iota(jnp.int32, (bq, bk), 1)
    valid = (kpos < qpos) & (kpos >= META_PAD)
    e = jnp.exp(-jnp.abs(z))
    log_keep = jnp.where(valid, -(jnp.maximum(z, 0.0) + jnp.log(1.0 + e)), 0.0)
    return z, valid, e, log_keep


def _tri2(n, cmp):
    r = lax.broadcasted_iota(jnp.int32, (2 * n, n), 0) % n
    c = lax.broadcasted_iota(jnp.int32, (2 * n, n), 1)
    return cmp(r, c).astype(BF16)


def _dot_split(x, t2):
    hi, lo = _split2(x)
    return _dot(jnp.concatenate([hi, lo], axis=1), t2)


ATT_HEADS = 2
ATT_CUT = -104.0


def _sb_fwd(name, q, k, v):
    nh, rows, dh = q.shape
    bq, bk, hg = _row_tile(rows), ATT_BLOCK, ATT_HEADS
    per = bq // bk
    scale = dh ** -0.5

    def body(q_ref, k_ref, v_ref, o_ref, rt_ref, cnt_ref):
        i = pl.program_id(1)
        after = _tri2(bk, lambda r, c: r > c)
        nkb = (i + 1) * per

        def live(state):
            n, carry = state
            top = jnp.max(carry[0][0])
            for hh in range(1, hg):
                top = jnp.maximum(top, jnp.max(carry[hh][0]))
            return (n < nkb) & (top >= ATT_CUT)

        def step(state):
            n, carry = state
            jb = nkb - 1 - n
            off = pl.multiple_of(jb * bk, bk)
            out = []
            for hh in range(hg):
                rest, acc = carry[hh]
                kb = k_ref[hh, pl.ds(off, bk), :]
                vb = v_ref[hh, pl.ds(off, bk), :]
                z, valid, _, log_keep = _sb_block(q_ref[hh], kb, i, jb, scale)
                log_rest = rest + _dot_split(log_keep, after)
                attn = jnp.where(valid, jnp.exp(z + log_keep + log_rest), 0.0)
                out.append((rest + jnp.sum(log_keep, axis=-1, keepdims=True), acc + _dot(attn.astype(BF16), vb)))
            return n + 1, tuple(out)

        init = tuple((jnp.zeros((bq, 1), F32), jnp.zeros((bq, dh), F32)) for _ in range(hg))
        n, res = lax.while_loop(live, step, (jnp.int32(0), init))
        for hh in range(hg):
            rt_ref[hh] = res[hh][0]
            o_ref[hh] = res[hh][1]
            cnt_ref[hh] = jnp.full((bq, 1), n, F32)

    return pl.pallas_call(
        body, name=name, grid=(nh // hg, rows // bq),
        in_specs=[pl.BlockSpec((hg, bq, dh), lambda h, i: (h, i, 0)),
                  pl.BlockSpec((hg, rows, dh), lambda h, i: (h, 0, 0)),
                  pl.BlockSpec((hg, rows, dh), lambda h, i: (h, 0, 0))],
        out_specs=[pl.BlockSpec((hg, bq, dh), lambda h, i: (h, i, 0)),
                   pl.BlockSpec((hg, bq, 1), lambda h, i: (h, i, 0)),
                   pl.BlockSpec((hg, bq, 1), lambda h, i: (h, i, 0))],
        out_shape=[jax.ShapeDtypeStruct((nh, rows, dh), F32), jax.ShapeDtypeStruct((nh, rows, 1), F32),
                   jax.ShapeDtypeStruct((nh, rows, 1), F32)],
        compiler_params=_params("arbitrary", "arbitrary"),
    )(q, k, v)


def _sb_bwd(name, q, k, v, rt, cnt, do):
    nh, rows, dh = q.shape
    bq, bk, hg = _row_tile(rows), ATT_BLOCK, ATT_HEADS
    per = bq // bk
    scale = dh ** -0.5

    def body(q_ref, k_ref, v_ref, rt_ref, cnt_ref, do_ref, dq_ref, dk_ref, dv_ref):
        i = pl.program_id(1)

        @pl.when(i == 0)
        def _():
            dk_ref[...] = jnp.zeros_like(dk_ref)
            dv_ref[...] = jnp.zeros_like(dv_ref)

        upto = _tri2(bk, lambda r, c: r <= c)
        before = _tri2(bk, lambda r, c: r < c)
        nkb = (i + 1) * per
        first = nkb - jnp.max(cnt_ref[0]).astype(jnp.int32)

        def step(jb, carry):
            off = pl.multiple_of(jb * bk, bk)
            out = []
            for hh in range(hg):
                keep_sum, g_sum, dq = carry[hh]
                qb, dob = q_ref[hh], do_ref[hh]
                kb = k_ref[hh, pl.ds(off, bk), :]
                vb = v_ref[hh, pl.ds(off, bk), :]
                z, valid, e, log_keep = _sb_block(qb, kb, i, jb, scale)
                log_rest = rt_ref[hh] - keep_sum - _dot_split(log_keep, upto)
                attn = jnp.where(valid, jnp.exp(z + log_keep + log_rest), 0.0)
                g = attn * _dot_nt(dob, vb)
                g_before = g_sum + _dot_split(g, before)
                inv = 1.0 / (1.0 + e)
                sig = jnp.where(z >= 0, inv, e * inv)
                dz = (jnp.where(valid, g * (1.0 - sig) - g_before * sig, 0.0) * scale).astype(BF16)
                dk_ref[hh, pl.ds(off, bk), :] += _dot_tn(dz, qb)
                dv_ref[hh, pl.ds(off, bk), :] += _dot_tn(attn.astype(BF16), dob)
                out.append((keep_sum + jnp.sum(log_keep, axis=-1, keepdims=True),
                            g_sum + jnp.sum(g, axis=-1, keepdims=True),
                            dq + _dot(dz, kb)))
            return tuple(out)

        zero = jnp.zeros((bq, 1), F32)
        res = lax.fori_loop(first, nkb, step, tuple((zero, zero, jnp.zeros((bq, dh), F32)) for _ in range(hg)))
        for hh in range(hg):
            dq_ref[hh] = res[hh][2]

    return pl.pallas_call(
        body, name=name, grid=(nh // hg, rows // bq),
        in_specs=[pl.BlockSpec((hg, bq, dh), lambda h, i: (h, i, 0)),
                  pl.BlockSpec((hg, rows, dh), lambda h, i: (h, 0, 0)),
                  pl.BlockSpec((hg, rows, dh), lambda h, i: (h, 0, 0)),
                  pl.BlockSpec((hg, bq, 1), lambda h, i: (h, i, 0)),
                  pl.BlockSpec((hg, bq, 1), lambda h, i: (h, i, 0)),
                  pl.BlockSpec((hg, bq, dh), lambda h, i: (h, i, 0))],
        out_specs=[pl.BlockSpec((hg, bq, dh), lambda h, i: (h, i, 0)),
                   pl.BlockSpec((hg, rows, dh), lambda h, i: (h, 0, 0)),
                   pl.BlockSpec((hg, rows, dh), lambda h, i: (h, 0, 0))],
        out_shape=[jax.ShapeDtypeStruct((nh, rows, dh), F32)] * 3,
        compiler_params=_params("arbitrary", "arbitrary"),
    )(q, k, v, rt, cnt, do)


def _head_sum(x, ones_bd):
    return _dot_split(x, ones_bd)


def _rwkv_pre(p, p_prev, mu, w0, a0, k_k, k_a, w_up, a_up, g_up, ones_bd):
    xs = p + (p_prev - p) * mu
    r = xs[:, :GROUP]
    k0 = xs[:, GROUP:2 * GROUP]
    v = xs[:, 2 * GROUP:3 * GROUP]
    lo = xs[:, 3 * GROUP:]
    wa = w0 + _dot(jnp.tanh(lo).astype(BF16), w_up.astype(BF16))
    w = -(jnp.maximum(-wa, 0.0) + jnp.log(1.0 + jnp.exp(-jnp.abs(wa)))) - 0.5
    log_decay = -jnp.exp(w)
    alpha = _sigmoid(a0 + _dot(lo.astype(BF16), a_up.astype(BF16)))
    gate = _dot(_sigmoid(lo).astype(BF16), g_up.astype(BF16))
    kk = k0 * k_k
    kk = kk * lax.rsqrt(jnp.maximum(_head_sum(kk * kk, ones_bd), 1e-24))
    k = k0 * (1.0 + (alpha - 1.0) * k_a)
    return r, log_decay, k, v, -kk, kk * alpha, gate


def _rwkv_post(y, r, k, v, gate, lnx_w, lnx_b, r_k, ones_bd):
    mean = _head_sum(y, ones_bd) * (1.0 / HEAD)
    yc = y - mean
    var = _head_sum(yc * yc, ones_bd) * (1.0 / HEAD)
    yn = yc * lax.rsqrt(var + LNX_EPS) * lnx_w + lnx_b
    bonus = _head_sum(r * k * r_k, ones_bd) * v
    return (yn + bonus) * gate


_PRE_VEC = 5
_PRE_MAT = 3


def _rwkv_pre_fwd(name, p, p_prev, vecs, mats, ones_bd):
    rows = p.shape[0]
    tm = _row_tile(rows)
    row_spec = lambda w: pl.BlockSpec((tm, w), lambda i: (i, 0))
    full = lambda a: pl.BlockSpec(a.shape, lambda i: (0,) * a.ndim)

    def body(p_ref, pp_ref, *refs):
        ins = [r[...] for r in refs[:_PRE_VEC + _PRE_MAT + 1]]
        outs = refs[_PRE_VEC + _PRE_MAT + 1:]
        for o_ref, val in zip(outs, _rwkv_pre(p_ref[...], pp_ref[...], *ins)):
            o_ref[...] = val

    return pl.pallas_call(
        body, name=name, grid=(rows // tm,),
        in_specs=[row_spec(RW_COLS), row_spec(RW_COLS)] + [full(a) for a in (*vecs, *mats, ones_bd)],
        out_specs=[row_spec(GROUP)] * 7,
        out_shape=[jax.ShapeDtypeStruct((rows, GROUP), F32)] * 7,
        compiler_params=_params("arbitrary"),
    )(p, p_prev, *vecs, *mats, ones_bd)


def _rwkv_pre_bwd(name, p, p_prev, vecs, mats, ones_bd, cts_a, cts_b):
    rows = p.shape[0]
    tm = _row_tile(rows)
    n_par = _PRE_VEC + _PRE_MAT
    row_spec = lambda w: pl.BlockSpec((tm, w), lambda i: (i, 0))
    full = lambda a: pl.BlockSpec(a.shape, lambda i: (0,) * a.ndim)

    def body(*refs):
        p_ref, pp_ref = refs[0], refs[1]
        par = [r[...] for r in refs[2:2 + n_par]]
        ones = refs[2 + n_par][...]
        cta = [r[...] for r in refs[3 + n_par:10 + n_par]]
        ctb = [r[...] for r in refs[10 + n_par:13 + n_par]]
        outs = refs[13 + n_par:]
        ct = (cta[0] + ctb[0], cta[1], cta[2] + ctb[1], cta[3] + ctb[2], cta[4], cta[5], cta[6])
        _, vjp = jax.vjp(lambda pv, ppv, *pr: _rwkv_pre(pv, ppv, *pr, ones), p_ref[...], pp_ref[...], *par)
        grads = vjp(ct)
        outs[0][...] = grads[0]
        outs[1][...] = grads[1]

        @pl.when(pl.program_id(0) == 0)
        def _():
            for o_ref in outs[2:]:
                o_ref[...] = jnp.zeros_like(o_ref)

        for o_ref, gval in zip(outs[2:], grads[2:]):
            o_ref[...] += gval

    par_arrays = (*vecs, *mats)
    return pl.pallas_call(
        body, name=name, grid=(rows // tm,),
        in_specs=([row_spec(RW_COLS)] * 2 + [full(a) for a in (*par_arrays, ones_bd)]
                  + [row_spec(GROUP)] * 10),
        out_specs=[row_spec(RW_COLS)] * 2 + [full(a) for a in par_arrays],
        out_shape=([jax.ShapeDtypeStruct((rows, RW_COLS), F32)] * 2
                   + [jax.ShapeDtypeStruct(a.shape, F32) for a in par_arrays]),
        compiler_params=_params("arbitrary"),
    )(p, p_prev, *par_arrays, ones_bd, *cts_a, *cts_b)


def _rwkv_post_fwd(name, y, r, k, v, gate, vecs, ones_bd):
    rows = y.shape[0]
    tm = _row_tile(rows)
    row_spec = pl.BlockSpec((tm, GROUP), lambda i: (i, 0))
    full = lambda a: pl.BlockSpec(a.shape, lambda i: (0,) * a.ndim)

    def body(*refs):
        vals = [r_[...] for r_ in refs[:-1]]
        refs[-1][...] = _rwkv_post(*vals)

    return pl.pallas_call(
        body, name=name, grid=(rows // tm,),
        in_specs=[row_spec] * 5 + [full(a) for a in (*vecs, ones_bd)],
        out_specs=row_spec,
        out_shape=jax.ShapeDtypeStruct((rows, GROUP), F32),
        compiler_params=_params("arbitrary"),
    )(y, r, k, v, gate, *vecs, ones_bd)


def _rwkv_post_bwd(name, y, r, k, v, gate, vecs, ones_bd, dout):
    rows = y.shape[0]
    tm = _row_tile(rows)
    row_spec = pl.BlockSpec((tm, GROUP), lambda i: (i, 0))
    full = lambda a: pl.BlockSpec(a.shape, lambda i: (0,) * a.ndim)

    def body(*refs):
        vals = [r_[...] for r_ in refs[:8]]
        ones = refs[8][...]
        dout_v = refs[9][...]
        outs = refs[10:]
        _, vjp = jax.vjp(lambda *a: _rwkv_post(*a, ones), *vals)
        grads = vjp(dout_v)
        for o_ref, gval in zip(outs[:5], grads[:5]):
            o_ref[...] = gval

        @pl.when(pl.program_id(0) == 0)
        def _():
            for o_ref in outs[5:]:
                o_ref[...] = jnp.zeros_like(o_ref)

        for o_ref, gval in zip(outs[5:], grads[5:]):
            o_ref[...] += gval

    return pl.pallas_call(
        body, name=name, grid=(rows // tm,),
        in_specs=[row_spec] * 5 + [full(a) for a in (*vecs, ones_bd)] + [row_spec],
        out_specs=[row_spec] * 5 + [full(a) for a in vecs],
        out_shape=[jax.ShapeDtypeStruct((rows, GROUP), F32)] * 5 + [jax.ShapeDtypeStruct(a.shape, F32) for a in vecs],
        compiler_params=_params("arbitrary"),
    )(y, r, k, v, gate, *vecs, ones_bd, dout)


_NN = (((2,), (1,)), ((0,), (0,)))
_NT = (((2,), (2,)), ((0,), (0,)))
_TN = (((1,), (1,)), ((0,), (0,)))


def _bmm(a, b, dims):
    ah, al = _split2(a)
    bh, bl = _split2(b)
    dot = lambda x, y: lax.dot_general(x, y, dims, preferred_element_type=F32)
    return dot(ah, bh) + (dot(ah, bl) + dot(al, bh))


def _chunk(state, r, log_w, k, v, a, b):
    nh, c, _ = r.shape
    row = lax.broadcasted_iota(jnp.int32, (c, c), 0)
    col = lax.broadcasted_iota(jnp.int32, (c, c), 1)
    cum = _bmm(jnp.broadcast_to((row >= col).astype(F32)[None], (nh, c, c)), log_w, _NN)
    mid = cum[:, c // 2 - 1:c // 2, :]
    a_t = a * jnp.exp(cum - log_w - mid)
    r_t = r * jnp.exp(cum - mid)
    back = jnp.exp(mid - cum)
    b_t = b * back
    k_t = k * back
    strict, incl = (row > col)[None], (row >= col)[None]
    n_mat = jnp.where(strict, _bmm(a_t, b_t, _NT), 0.0)
    m_mat = jnp.where(strict, _bmm(a_t, k_t, _NT), 0.0)
    p_mat = jnp.where(incl, _bmm(r_t, b_t, _NT), 0.0)
    q_mat = jnp.where(incl, _bmm(r_t, k_t, _NT), 0.0)
    inv, power, span = n_mat, n_mat, 1
    while 2 * span < c:
        power = _bmm(power, power, _NN)
        inv = inv + power + _bmm(inv, power, _NN)
        span *= 2
    s_mid = state * jnp.swapaxes(jnp.exp(mid), 1, 2)
    x = _bmm(a_t, s_mid, _NN) + _bmm(m_mat, v, _NN)
    u = x + _bmm(inv, x, _NN)
    y = _bmm(r_t, s_mid, _NN) + _bmm(p_mat, u, _NN) + _bmm(q_mat, v, _NN)
    s_new = (s_mid + _bmm(b_t, u, _TN) + _bmm(k_t, v, _TN)) * jnp.swapaxes(jnp.exp(cum[:, c - 1:c, :] - mid), 1, 2)
    return y, s_new


def _scan_fwd(name, ops):
    nh, rows, dh = ops[0].shape
    nc = rows // CHUNK
    spec = pl.BlockSpec((nh, CHUNK, dh), lambda c: (0, c, 0))

    def body(r_ref, w_ref, k_ref, v_ref, a_ref, b_ref, y_ref, st_ref, state):
        @pl.when(pl.program_id(0) == 0)
        def _():
            state[...] = jnp.zeros_like(state)

        st_ref[0] = state[...]
        y, s_new = _chunk(state[...], r_ref[...], w_ref[...], k_ref[...], v_ref[...], a_ref[...], b_ref[...])
        y_ref[...] = y
        state[...] = s_new

    return pl.pallas_call(
        body, name=name, grid=(nc,),
        in_specs=[spec] * 6,
        out_specs=[spec, pl.BlockSpec((1, nh, dh, dh), lambda c: (c, 0, 0, 0))],
        out_shape=[jax.ShapeDtypeStruct((nh, rows, dh), F32), jax.ShapeDtypeStruct((nc, nh, dh, dh), F32)],
        scratch_shapes=[pltpu.VMEM((nh, dh, dh), F32)],
        compiler_params=_params("arbitrary"),
    )(*ops)


def _scan_bwd(name, ops, states, dy):
    nh, rows, dh = ops[0].shape
    nc = rows // CHUNK
    spec = pl.BlockSpec((nh, CHUNK, dh), lambda c: (0, nc - 1 - c, 0))

    def body(r_ref, w_ref, k_ref, v_ref, a_ref, b_ref, st_ref, dy_ref, *rest):
        outs, dstate = rest[:6], rest[6]

        @pl.when(pl.program_id(0) == 0)
        def _():
            dstate[...] = jnp.zeros_like(dstate)

        _, vjp = jax.vjp(_chunk, st_ref[0], r_ref[...], w_ref[...], k_ref[...], v_ref[...], a_ref[...], b_ref[...])
        grads = vjp((dy_ref[...], dstate[...]))
        dstate[...] = grads[0]
        for o_ref, gval in zip(outs, grads[1:]):
            o_ref[...] = gval

    return pl.pallas_call(
        body, name=name, grid=(nc,),
        in_specs=[spec] * 6 + [pl.BlockSpec((1, nh, dh, dh), lambda c: (nc - 1 - c, 0, 0, 0)), spec],
        out_specs=[spec] * 6,
        out_shape=[jax.ShapeDtypeStruct((nh, rows, dh), F32)] * 6,
        scratch_shapes=[pltpu.VMEM((nh, dh, dh), F32)],
        compiler_params=_params("arbitrary"),
    )(*ops, states, dy)


def _heads(x):
    return x.reshape(x.shape[0], N_HEADS, HEAD).transpose(1, 0, 2)


def _unheads(x):
    return x.transpose(1, 0, 2).reshape(x.shape[1], GROUP)


def _shift_down(x):
    return jnp.concatenate([jnp.zeros((1, x.shape[1]), x.dtype), x[:-1]], axis=0)


def _shift_up(x):
    return jnp.concatenate([x[1:], jnp.zeros((1, x.shape[1]), x.dtype)], axis=0)


def _pad_rows(x, rows):
    return jnp.concatenate([x, jnp.zeros((rows - x.shape[0],) + x.shape[1:], x.dtype)], axis=0)


def _pad_cols(x, cols):
    return jnp.concatenate([x, jnp.zeros(x.shape[:-1] + (cols - x.shape[-1],), x.dtype)], axis=-1)


def _lora_pad(w_up, a_up, g_up):
    z = lambda n: jnp.zeros((n, GROUP), F32)
    return (jnp.concatenate([w_up, z(LORA_PAD - LORA_W)], 0),
            jnp.concatenate([z(LORA_W), a_up, z(LORA_PAD - LORA_W - LORA_A)], 0),
            jnp.concatenate([z(LORA_W + LORA_A), g_up, z(LORA_PAD - LORA_W - LORA_A - LORA_G)], 0))


def _local_step(x, tgt, w):
    d = x.shape[1]
    zeros = jnp.zeros((META_PAD, d), F32)
    h0 = jnp.concatenate([zeros, w["meta_tokens"], x], axis=0)
    tgt_p = jnp.concatenate([jnp.zeros((ROW0, d), F32), tgt], axis=0)
    ones_bd = ((lax.broadcasted_iota(jnp.int32, (2 * GROUP, GROUP), 0) % GROUP) // HEAD
               == lax.broadcasted_iota(jnp.int32, (2 * GROUP, GROUP), 1) // HEAD).astype(BF16)
    w_in = _pad_cols(w["w_in"], IN_COLS_PAD)
    pre_vecs = (_pad_cols(w["rwkv_mu"], RW_COLS), w["rwkv_w0"], w["rwkv_a0"], w["rwkv_k_k"], w["rwkv_k_a"])
    pre_mats = _lora_pad(w["rwkv_w_up"], w["rwkv_a_up"], w["rwkv_g_up"])
    post_vecs = (w["rwkv_lnx_w"], w["rwkv_lnx_b"], w["rwkv_r_k"].reshape(1, GROUP))

    h1, a1, b1 = _ffn_fwd("ffn1_fwd", h0, w["ffn1_norm"], w["ffn1_w_gate"], w["ffn1_w_up"], w["ffn1_w_down"])
    proj, n2 = _norm_proj("in_proj", h1, w["mix_norm"], w_in)
    q, k, v = (_heads(proj[:, j * GROUP:(j + 1) * GROUP]).astype(BF16) for j in range(3))
    sb, rest_total, visited = _sb_fwd("sb_fwd", q, k, v)
    p = proj[:, 3 * GROUP:]
    p_prev = _shift_down(p)
    pre = _rwkv_pre_fwd("rwkv_pre_fwd", p, p_prev, pre_vecs, pre_mats, ones_bd)
    scan_ops = tuple(_heads(t) for t in pre[:6])
    y_h, states = _scan_fwd("rwkv_scan_fwd", scan_ops)
    y = _unheads(y_h)
    rw = _rwkv_post_fwd("rwkv_post_fwd", y, pre[0], pre[2], pre[3], pre[6], post_vecs, ones_bd)
    h2, mix = _out_proj("out_proj", h1, _unheads(sb), rw, w["w_out"])
    h3, a2, b2 = _ffn_fwd("ffn2_fwd", h2, w["ffn2_norm"], w["ffn2_w_gate"], w["ffn2_w_up"], w["ffn2_w_down"])
    loss8, dh3, g_final = _loss_head("loss_head", h3, w["final_norm"].reshape(1, d), tgt_p)

    g = {"final_norm": g_final.reshape(d)}
    dh2, da2, db2, s2, n3, dhh3, g["ffn2_norm"] = _ffn_bwd(
        "ffn2_bwd", dh3, h2, w["ffn2_norm"], a2, b2, w["ffn2_w_gate"], w["ffn2_w_up"], w["ffn2_w_down"])
    g["ffn2_w_gate"] = _mm_tn("ffn2_dgate", n3, da2)
    g["ffn2_w_up"] = _mm_tn("ffn2_dup", n3, db2)
    g["ffn2_w_down"] = _mm_tn("ffn2_ddown", s2, dhh3)
    dmix, dh2b = _out_proj_bwd("out_proj_bwd", dh2, w["w_out"])
    g["w_out"] = _mm_tn("out_proj_dw", mix, dh2b)
    dq, dk, dv = _sb_bwd("sb_bwd", q, k, v, rest_total, visited, _heads(dmix[:, :GROUP]).astype(BF16))
    post_g = _rwkv_post_bwd("rwkv_post_bwd", y, pre[0], pre[2], pre[3], pre[6], post_vecs, ones_bd, dmix[:, GROUP:])
    g["rwkv_lnx_w"], g["rwkv_lnx_b"] = post_g[5], post_g[6]
    g["rwkv_r_k"] = post_g[7].reshape(1, N_HEADS, HEAD)
    scan_g = _scan_bwd("rwkv_scan_bwd", scan_ops, states, _heads(post_g[0]))
    cts_a = tuple(_unheads(t) for t in scan_g) + (post_g[4],)
    pre_g = _rwkv_pre_bwd("rwkv_pre_bwd", p, p_prev, pre_vecs, pre_mats, ones_bd, cts_a, post_g[1:4])
    g["rwkv_mu"] = pre_g[2][:, :w["rwkv_mu"].shape[1]]
    g["rwkv_w0"], g["rwkv_a0"], g["rwkv_k_k"], g["rwkv_k_a"] = pre_g[3:7]
    g["rwkv_w_up"] = pre_g[7][:LORA_W]
    g["rwkv_a_up"] = pre_g[8][LORA_W:LORA_W + LORA_A]
    g["rwkv_g_up"] = pre_g[9][LORA_W + LORA_A:LORA_W + LORA_A + LORA_G]
    dp = pre_g[0] + _shift_up(pre_g[1])
    live = (jnp.arange(h0.shape[0]) >= META_PAD)[:, None]
    dproj = jnp.where(live, jnp.concatenate([_unheads(dq), _unheads(dk), _unheads(dv), dp], axis=1), 0.0).astype(BF16)
    g["w_in"] = _mm_tn("in_proj_dw", n2, dproj)[:, :w["w_in"].shape[1]]
    dh1, g["mix_norm"] = _norm_proj_bwd("in_proj_bwd", dproj, w_in, h1, w["mix_norm"], dh2)
    dh0, da1, db1, s1, n1, dhh1, g["ffn1_norm"] = _ffn_bwd(
        "ffn1_bwd", dh1, h0, w["ffn1_norm"], a1, b1, w["ffn1_w_gate"], w["ffn1_w_up"], w["ffn1_w_down"])
    g["ffn1_w_gate"] = _mm_tn("ffn1_dgate", n1, da1)
    g["ffn1_w_up"] = _mm_tn("ffn1_dup", n1, db1)
    g["ffn1_w_down"] = _mm_tn("ffn1_ddown", s1, dhh1)
    g["meta_tokens"] = dh0[META_PAD:ROW0]
    return loss8[0, 0], dh0[ROW0:], g


N_CHIPS = 4
N_DEV = 8
HBM = pl.BlockSpec(memory_space=pltpu.HBM)


def _place():
    return lax.axis_index("x"), lax.axis_index("y"), lax.axis_index("c")


def _other_chips(x, y):
    return [(1 - x, y), (x, 1 - y), (1 - x, 1 - y)]


def _gather_shards(name, shards):
    n = len(shards)

    def body(*refs):
        ins, outs = refs[:n], refs[n:2 * n]
        send, recv, local = refs[2 * n:]
        x, y, c = _place()
        me = 2 * x + y
        chips = _other_chips(x, y)
        own = [pltpu.make_async_copy(ins[k], outs[k].at[me], local.at[k]) for k in range(n)]
        for cp in own:
            cp.start()

        def copy(j, k, slot):
            return pltpu.make_async_remote_copy(
                src_ref=ins[k], dst_ref=outs[k].at[slot], send_sem=send.at[j * n + k], recv_sem=recv.at[j * n + k],
                device_id=(chips[j][0], chips[j][1], c), device_id_type=MESH)

        sent = [copy(j, k, me) for j in range(3) for k in range(n)]
        for cp in sent:
            cp.start()
        for j in range(3):
            for k in range(n):
                copy(j, k, 2 * chips[j][0] + chips[j][1]).wait_recv()
        for cp in sent:
            cp.wait_send()
        for cp in own:
            cp.wait()

    return pl.pallas_call(
        body, name=name,
        in_specs=[HBM] * n, out_specs=[HBM] * n,
        out_shape=[jax.ShapeDtypeStruct((N_CHIPS,) + s.shape, s.dtype) for s in shards],
        scratch_shapes=[pltpu.SemaphoreType.DMA((3 * n,)), pltpu.SemaphoreType.DMA((3 * n,)),
                        pltpu.SemaphoreType.DMA((n,))],
    )(*shards)


def _reduce_shards(name, parts):
    n = len(parts)

    def body(*refs):
        ins, got, sib = refs[:n], refs[n:2 * n], refs[2 * n:3 * n]
        send, recv, local, d2d_send, d2d_recv = refs[3 * n:]
        x, y, c = _place()
        me = 2 * x + y
        chips = _other_chips(x, y)
        own = [pltpu.make_async_copy(ins[k].at[me], got[k].at[me], local.at[k]) for k in range(n)]
        for cp in own:
            cp.start()

        def copy(j, k, shard, slot):
            return pltpu.make_async_remote_copy(
                src_ref=ins[k].at[shard], dst_ref=got[k].at[slot], send_sem=send.at[j * n + k],
                recv_sem=recv.at[j * n + k], device_id=(chips[j][0], chips[j][1], c), device_id_type=MESH)

        sent = [copy(j, k, 2 * chips[j][0] + chips[j][1], me) for j in range(3) for k in range(n)]
        for cp in sent:
            cp.start()

        def swap(k):
            return pltpu.make_async_remote_copy(
                src_ref=got[k], dst_ref=sib[k], send_sem=d2d_send.at[k], recv_sem=d2d_recv.at[k],
                device_id=(x, y, 1 - c), device_id_type=MESH)

        for k in range(n):
            own[k].wait()
            for j in range(3):
                copy(j, k, me, 2 * chips[j][0] + chips[j][1]).wait_recv()
            swap(k).start()
        for k in range(n):
            swap(k).wait_recv()
        for cp in sent:
            cp.wait_send()
        for k in range(n):
            swap(k).wait_send()

    return pl.pallas_call(
        body, name=name,
        in_specs=[HBM] * n, out_specs=[HBM] * (2 * n),
        out_shape=[jax.ShapeDtypeStruct(s.shape, s.dtype) for s in parts] * 2,
        scratch_shapes=[pltpu.SemaphoreType.DMA((3 * n,)), pltpu.SemaphoreType.DMA((3 * n,)),
                        pltpu.SemaphoreType.DMA((n,)), pltpu.SemaphoreType.DMA((n,)), pltpu.SemaphoreType.DMA((n,))],
    )(*parts)


def _all_reduce_small(name, vec):
    rows = vec.shape[0]

    def body(v_ref, o_ref, buf, send, recv):
        x, y, c = _place()
        me = 4 * x + 2 * y + c
        peers = [(x ^ (r >> 2), y ^ ((r >> 1) & 1), c ^ (r & 1)) for r in range(1, N_DEV)]

        def copy(r, slot):
            px, py, pc = peers[r]
            return pltpu.make_async_remote_copy(
                src_ref=v_ref, dst_ref=buf.at[slot], send_sem=send.at[r], recv_sem=recv.at[r],
                device_id=(px, py, pc), device_id_type=MESH)

        sent = [copy(r, me) for r in range(N_DEV - 1)]
        for cp in sent:
            cp.start()
        buf[me] = v_ref[...]
        for r in range(N_DEV - 1):
            px, py, pc = peers[r]
            copy(r, 4 * px + 2 * py + pc).wait_recv()
        total = buf[0]
        for dev in range(1, N_DEV):
            total = total + buf[dev]
        o_ref[...] = total
        for cp in sent:
            cp.wait_send()

    return pl.pallas_call(
        body, name=name,
        in_specs=[pl.BlockSpec(memory_space=pltpu.VMEM)], out_specs=pl.BlockSpec(memory_space=pltpu.VMEM),
        out_shape=jax.ShapeDtypeStruct(vec.shape, F32),
        scratch_shapes=[pltpu.VMEM((N_DEV, rows, 128), F32),
                        pltpu.SemaphoreType.DMA((N_DEV - 1,)), pltpu.SemaphoreType.DMA((N_DEV - 1,))],
        compiler_params=pltpu.CompilerParams(vmem_limit_bytes=VMEM_LIMIT),
    )(vec)


def _adamw(w, g, m, v):
    m = ADAM_B1 * m + (1.0 - ADAM_B1) * g
    v = ADAM_B2 * v + (1.0 - ADAM_B2) * (g * g)
    m_hat = m / (1.0 - ADAM_B1 ** ADAM_STEP)
    v_hat = v / (1.0 - ADAM_B2 ** ADAM_STEP)
    return -ADAM_LR * (m_hat / (jnp.sqrt(v_hat) + ADAM_EPS) + ADAM_WD * w), m, v


def _adamw_shard(name, w, m, v, got, sib):
    rows, cols = w.shape
    tr = rows // 4
    spec = pl.BlockSpec((tr, cols), lambda i: (i, 0))
    spec4 = pl.BlockSpec((N_CHIPS, tr, cols), lambda i: (0, i, 0))

    def body(w_ref, m_ref, v_ref, got_ref, sib_ref, g_ref, d_ref, mo_ref, vo_ref):
        def four(ref):
            return ((ref[0].astype(F32) + ref[1].astype(F32)) + ref[2].astype(F32)) + ref[3].astype(F32)

        g = four(got_ref) + four(sib_ref)
        g_ref[...] = g
        d_ref[...], mo_ref[...], vo_ref[...] = _adamw(w_ref[...], g, m_ref[...], v_ref[...])

    return pl.pallas_call(
        body, name=name, grid=(4,),
        in_specs=[spec, spec, spec, spec4, spec4], out_specs=[spec] * 4,
        out_shape=[jax.ShapeDtypeStruct((rows, cols), F32)] * 4,
        compiler_params=_params("arbitrary"),
    )(w, m, v, got, sib)


def _adamw_small(name, w, m, v, g):
    def body(w_ref, m_ref, v_ref, g_ref, d_ref, mo_ref, vo_ref):
        d_ref[...], mo_ref[...], vo_ref[...] = _adamw(w_ref[...], g_ref[...], m_ref[...], v_ref[...])

    return pl.pallas_call(body, name=name, out_shape=[jax.ShapeDtypeStruct(w.shape, F32)] * 3)(w, m, v, g)


def _pack(arrays, rows):
    flat = jnp.concatenate([a.reshape(-1) for a in arrays])
    return jnp.concatenate([flat, jnp.zeros((rows * 128 - flat.shape[0],), F32)]).reshape(rows, 128)


def _unpack(packed, shapes):
    flat, out, at = packed.reshape(-1), [], 0
    for s in shapes:
        size = 1
        for dim in s:
            size *= dim
        out.append(flat[at:at + size].reshape(s))
        at += size
    return out


def _rows_for(shapes):
    total = 0
    for s in shapes:
        size = 1
        for dim in s:
            size *= dim
        total += size
    return -(-total // 1024) * 8


WEIGHTS = ['meta_tokens', 'ffn1_norm', 'ffn1_w_gate', 'ffn1_w_up', 'ffn1_w_down', 'mix_norm', 'w_in', 'rwkv_mu',
           'rwkv_w0', 'rwkv_w_up', 'rwkv_a0', 'rwkv_a_up', 'rwkv_g_up', 'rwkv_k_k', 'rwkv_k_a', 'rwkv_r_k',
           'rwkv_lnx_w', 'rwkv_lnx_b', 'w_out', 'ffn2_norm', 'ffn2_w_gate', 'ffn2_w_up', 'ffn2_w_down', 'final_norm']
COL_CUT = ['ffn1_w_gate', 'ffn1_w_up', 'w_in', 'ffn2_w_gate', 'ffn2_w_up']
ROW_CUT = ['ffn1_w_down', 'w_out', 'ffn2_w_down']
SMALL_CUT = ['meta_tokens', 'rwkv_w_up', 'rwkv_a_up', 'rwkv_g_up']
BIG = COL_CUT + ROW_CUT
REPLICATED = [n for n in WEIGHTS if n not in BIG + SMALL_CUT]


def _join_cols(a):
    return a.transpose(1, 0, 2).reshape(a.shape[1], N_CHIPS * a.shape[2])


def _cut_cols(a):
    return a.reshape(a.shape[0], N_CHIPS, a.shape[1] // N_CHIPS).transpose(1, 0, 2)


def _step(x, loss_target, w, m, v):
    two = lambda a: a.reshape(a.shape[-2], a.shape[-1])

    names = BIG + SMALL_CUT
    shards = [two(w[n]).astype(BF16) for n in BIG] + [two(w[n]) for n in SMALL_CUT]
    gathered = dict(zip(names, _gather_shards("gather_weights", shards)))
    full = {n: (two(w[n]) if w[n].ndim == 3 else w[n]) for n in REPLICATED}
    for n in COL_CUT + SMALL_CUT:
        full[n] = _join_cols(gathered[n])
    for n in ROW_CUT:
        full[n] = gathered[n].reshape(-1, gathered[n].shape[-1])
    full["rwkv_r_k"] = w["rwkv_r_k"]
    full["final_norm"] = w["final_norm"]

    loss, dx, g = _local_step(x[0], loss_target[0], full)
    loss = lax.psum(loss, ("x", "y", "c"))

    parts = [_cut_cols(g[n]).astype(BF16) for n in COL_CUT]
    parts += [g[n].reshape(N_CHIPS, -1, g[n].shape[-1]).astype(BF16) for n in ROW_CUT]
    reduced = _reduce_shards("reduce_gradients", parts)
    got, sib = dict(zip(BIG, reduced[:len(BIG)])), dict(zip(BIG, reduced[len(BIG):]))

    small_names = REPLICATED + SMALL_CUT
    small_shapes = [g[n].shape for n in small_names]
    small = _all_reduce_small("reduce_small", _pack([g[n] for n in small_names], _rows_for(small_shapes)))
    g_small = dict(zip(small_names, _unpack(small, small_shapes)))
    chip = 2 * lax.axis_index("x") + lax.axis_index("y")
    for n in SMALL_CUT:
        width = g_small[n].shape[1] // N_CHIPS
        g_small[n] = lax.dynamic_slice_in_dim(g_small[n], chip * width, width, axis=1)

    grad, delta, new_m, new_v = {}, {}, {}, {}
    for n in BIG:
        outs = _adamw_shard("adamw_" + n, two(w[n]), two(m[n]), two(v[n]), got[n], sib[n])
        grad[n], delta[n], new_m[n], new_v[n] = (o.reshape(w[n].shape) for o in outs)
    shapes = [w[n].shape for n in small_names]
    rows = _rows_for(shapes)
    packed = [_pack([t[n] for n in small_names], rows) for t in (w, m, v)]
    g_packed = _pack([g_small[n] for n in small_names], rows)
    outs = [_unpack(o, shapes) for o in _adamw_small("adamw_small", *packed, g_packed)]
    for i, n in enumerate(small_names):
        grad[n] = g_small[n].reshape(w[n].shape)
        delta[n], new_m[n], new_v[n] = outs[0][i], outs[1][i], outs[2][i]
    return loss, dx[None], grad, delta, new_m, new_v


def kernel(x, meta_tokens, ffn1_norm, ffn1_w_gate, ffn1_w_up, ffn1_w_down, mix_norm, w_in, rwkv_mu, rwkv_w0, rwkv_w_up, rwkv_a0, rwkv_a_up, rwkv_g_up, rwkv_k_k, rwkv_k_a, rwkv_r_k, rwkv_lnx_w, rwkv_lnx_b, w_out, ffn2_norm, ffn2_w_gate, ffn2_w_up, ffn2_w_down, final_norm, loss_target, m_meta_tokens, m_ffn1_norm, m_ffn1_w_gate, m_ffn1_w_up, m_ffn1_w_down, m_mix_norm, m_w_in, m_rwkv_mu, m_rwkv_w0, m_rwkv_w_up, m_rwkv_a0, m_rwkv_a_up, m_rwkv_g_up, m_rwkv_k_k, m_rwkv_k_a, m_rwkv_r_k, m_rwkv_lnx_w, m_rwkv_lnx_b, m_w_out, m_ffn2_norm, m_ffn2_w_gate, m_ffn2_w_up, m_ffn2_w_down, m_final_norm, v_meta_tokens, v_ffn1_norm, v_ffn1_w_gate, v_ffn1_w_up, v_ffn1_w_down, v_mix_norm, v_w_in, v_rwkv_mu, v_rwkv_w0, v_rwkv_w_up, v_rwkv_a0, v_rwkv_a_up, v_rwkv_g_up, v_rwkv_k_k, v_rwkv_k_a, v_rwkv_r_k, v_rwkv_lnx_w, v_rwkv_lnx_b, v_w_out, v_ffn2_norm, v_ffn2_w_gate, v_ffn2_w_up, v_ffn2_w_down, v_final_norm):
    w = dict(zip(WEIGHTS, (meta_tokens, ffn1_norm, ffn1_w_gate, ffn1_w_up, ffn1_w_down, mix_norm, w_in, rwkv_mu, rwkv_w0, rwkv_w_up, rwkv_a0, rwkv_a_up, rwkv_g_up, rwkv_k_k, rwkv_k_a, rwkv_r_k, rwkv_lnx_w, rwkv_lnx_b, w_out, ffn2_norm, ffn2_w_gate, ffn2_w_up, ffn2_w_down, final_norm)))
    m = dict(zip(WEIGHTS, (m_meta_tokens, m_ffn1_norm, m_ffn1_w_gate, m_ffn1_w_up, m_ffn1_w_down, m_mix_norm, m_w_in, m_rwkv_mu, m_rwkv_w0, m_rwkv_w_up, m_rwkv_a0, m_rwkv_a_up, m_rwkv_g_up, m_rwkv_k_k, m_rwkv_k_a, m_rwkv_r_k, m_rwkv_lnx_w, m_rwkv_lnx_b, m_w_out, m_ffn2_norm, m_ffn2_w_gate, m_ffn2_w_up, m_ffn2_w_down, m_final_norm)))
    v = dict(zip(WEIGHTS, (v_meta_tokens, v_ffn1_norm, v_ffn1_w_gate, v_ffn1_w_up, v_ffn1_w_down, v_mix_norm, v_w_in, v_rwkv_mu, v_rwkv_w0, v_rwkv_w_up, v_rwkv_a0, v_rwkv_a_up, v_rwkv_g_up, v_rwkv_k_k, v_rwkv_k_a, v_rwkv_r_k, v_rwkv_lnx_w, v_rwkv_lnx_b, v_w_out, v_ffn2_norm, v_ffn2_w_gate, v_ffn2_w_up, v_ffn2_w_down, v_final_norm)))
    loss, grad_x, grad, delta, new_m, new_v = _step(x, loss_target, w, m, v)
    return (loss, grad_x, *[grad[n] for n in WEIGHTS], *[delta[n] for n in WEIGHTS],
            *[new_m[n] for n in WEIGHTS], *[new_v[n] for n in WEIGHTS])
```

```python
import functools

import jax
import jax.numpy as jnp
from jax import lax
from jax.experimental import pallas as pl
from jax.experimental.pallas import tpu as pltpu

F32 = jnp.float32
BF16 = jnp.bfloat16

RMS_EPS = 1e-6
LNX_EPS = 64e-5
N_META = 16
ROW0 = 128
META_PAD = ROW0 - N_META
HEAD = 64
N_HEADS = 8
GROUP = N_HEADS * HEAD
LORA_W, LORA_A, LORA_G = 32, 32, 96
LORA_PAD = 256
RW_COLS = 3 * GROUP + LORA_PAD
IN_COLS_PAD = 3 * GROUP + RW_COLS
ATT_BLOCK = 128
CHUNK = 64
VMEM_LIMIT = 56 * 1024 * 1024

ADAM_LR, ADAM_B1, ADAM_B2, ADAM_EPS, ADAM_WD, ADAM_STEP = 0.001, 0.9, 0.999, 1e-08, 0.01, 10

MESH = pl.DeviceIdType.MESH


def _params(*sem):
    return pltpu.CompilerParams(dimension_semantics=tuple(sem), vmem_limit_bytes=VMEM_LIMIT)


def _dot(a, b):
    return lax.dot_general(a, b, (((1,), (0,)), ((), ())), preferred_element_type=F32)


def _dot_nt(a, b):
    return lax.dot_general(a, b, (((1,), (1,)), ((), ())), preferred_element_type=F32)


def _dot_tn(a, b):
    return lax.dot_general(a, b, (((0,), (0,)), ((), ())), preferred_element_type=F32)


def _split2(x):
    hi = x.astype(BF16)
    return hi, (x - hi.astype(F32)).astype(BF16)


def _sigmoid(x):
    return 1.0 / (1.0 + jnp.exp(-x))


def _rms_fwd(x, g):
    rstd = lax.rsqrt(jnp.mean(x * x, axis=-1, keepdims=True) + RMS_EPS)
    xhat = x * rstd
    return xhat * g, xhat, rstd


def _rms_bwd(dn, xhat, rstd, g):
    dxhat = dn * g
    dx = rstd * (dxhat - xhat * jnp.mean(dxhat * xhat, axis=-1, keepdims=True))
    return dx, jnp.sum(dn * xhat, axis=0, keepdims=True)


def _row_tile(rows):
    return 384 if rows % 384 == 0 else 128


def _half_tile(cols):
    return cols // 2 if cols % 256 == 0 else cols


def _ffn_fwd(name, h, g, wg, wu, wd):
    rows, d = h.shape
    f = wg.shape[1]
    tm, tf = _row_tile(rows), _half_tile(f)
    nj = f // tf

    def body(h_ref, g_ref, wg_ref, wu_ref, wd_ref, ho_ref, a_ref, b_ref, n_sc, acc_sc):
        j = pl.program_id(1)

        @pl.when(j == 0)
        def _():
            n, _, _ = _rms_fwd(h_ref[...], g_ref[...])
            n_sc[...] = n.astype(BF16)
            acc_sc[...] = jnp.zeros_like(acc_sc)

        n = n_sc[...]
        a = _dot(n, wg_ref[...])
        b = _dot(n, wu_ref[...])
        a_ref[...] = a
        b_ref[...] = b
        s = a * _sigmoid(a) * b
        acc_sc[...] += _dot(s.astype(BF16), wd_ref[...])

        @pl.when(j == nj - 1)
        def _():
            ho_ref[...] = h_ref[...] + 0.5 * acc_sc[...]

    return pl.pallas_call(
        body, name=name, grid=(rows // tm, nj),
        in_specs=[pl.BlockSpec((tm, d), lambda i, j: (i, 0)),
                  pl.BlockSpec((1, d), lambda i, j: (0, 0)),
                  pl.BlockSpec((d, tf), lambda i, j: (0, j)),
                  pl.BlockSpec((d, tf), lambda i, j: (0, j)),
                  pl.BlockSpec((tf, d), lambda i, j: (j, 0))],
        out_specs=[pl.BlockSpec((tm, d), lambda i, j: (i, 0)),
                   pl.BlockSpec((tm, tf), lambda i, j: (i, j)),
                   pl.BlockSpec((tm, tf), lambda i, j: (i, j))],
        out_shape=[jax.ShapeDtypeStruct((rows, d), F32),
                   jax.ShapeDtypeStruct((rows, f), F32),
                   jax.ShapeDtypeStruct((rows, f), F32)],
        scratch_shapes=[pltpu.VMEM((tm, d), BF16), pltpu.VMEM((tm, d), F32)],
        compiler_params=_params("arbitrary", "arbitrary"),
    )(h, g, wg, wu, wd)


def _ffn_bwd(name, dh, h, g, a, b, wg, wu, wd):
    rows, d = h.shape
    f = wg.shape[1]
    tm, tf = _row_tile(rows), _half_tile(f)
    ni, nj = rows // tm, f // tf

    def body(dh_ref, h_ref, g_ref, a_ref, b_ref, wg_ref, wu_ref, wd_ref,
             dhi_ref, da_ref, db_ref, s_ref, n_ref, dhh_ref, dg_ref, dn_sc):
        i, j = pl.program_id(0), pl.program_id(1)

        @pl.when(j == 0)
        def _():
            n, _, _ = _rms_fwd(h_ref[...], g_ref[...])
            n_ref[...] = n.astype(BF16)
            dhh_ref[...] = (0.5 * dh_ref[...]).astype(BF16)
            dn_sc[...] = jnp.zeros_like(dn_sc)

        @pl.when((i == 0) & (j == 0))
        def _():
            dg_ref[...] = jnp.zeros_like(dg_ref)

        ds = _dot_nt(dhh_ref[...], wd_ref[...])
        av, bv = a_ref[...], b_ref[...]
        sig = _sigmoid(av)
        silu = av * sig
        s_ref[...] = (silu * bv).astype(BF16)
        db = (ds * silu).astype(BF16)
        da = (ds * bv * (sig * (1.0 + av * (1.0 - sig)))).astype(BF16)
        da_ref[...] = da
        db_ref[...] = db
        dn_sc[...] += _dot_nt(da, wg_ref[...]) + _dot_nt(db, wu_ref[...])

        @pl.when(j == nj - 1)
        def _():
            gv = g_ref[...]
            _, xhat, rstd = _rms_fwd(h_ref[...], gv)
            dx, dg = _rms_bwd(dn_sc[...], xhat, rstd, gv)
            dhi_ref[...] = dh_ref[...] + dx
            dg_ref[...] += dg

    return pl.pallas_call(
        body, name=name, grid=(ni, nj),
        in_specs=[pl.BlockSpec((tm, d), lambda i, j: (i, 0)),
                  pl.BlockSpec((tm, d), lambda i, j: (i, 0)),
                  pl.BlockSpec((1, d), lambda i, j: (0, 0)),
                  pl.BlockSpec((tm, tf), lambda i, j: (i, j)),
                  pl.BlockSpec((tm, tf), lambda i, j: (i, j)),
                  pl.BlockSpec((d, tf), lambda i, j: (0, j)),
                  pl.BlockSpec((d, tf), lambda i, j: (0, j)),
                  pl.BlockSpec((tf, d), lambda i, j: (j, 0))],
        out_specs=[pl.BlockSpec((tm, d), lambda i, j: (i, 0)),
                   pl.BlockSpec((tm, tf), lambda i, j: (i, j)),
                   pl.BlockSpec((tm, tf), lambda i, j: (i, j)),
                   pl.BlockSpec((tm, tf), lambda i, j: (i, j)),
                   pl.BlockSpec((tm, d), lambda i, j: (i, 0)),
                   pl.BlockSpec((tm, d), lambda i, j: (i, 0)),
                   pl.BlockSpec((1, d), lambda i, j: (0, 0))],
        out_shape=[jax.ShapeDtypeStruct((rows, d), F32),
                   jax.ShapeDtypeStruct((rows, f), BF16),
                   jax.ShapeDtypeStruct((rows, f), BF16),
                   jax.ShapeDtypeStruct((rows, f), BF16),
                   jax.ShapeDtypeStruct((rows, d), BF16),
                   jax.ShapeDtypeStruct((rows, d), BF16),
                   jax.ShapeDtypeStruct((1, d), F32)],
        scratch_shapes=[pltpu.VMEM((tm, d), F32)],
        compiler_params=_params("arbitrary", "arbitrary"),
    )(dh, h, g, a, b, wg, wu, wd)


def _mm_tn(name, a, b):
    k, m = a.shape
    n = b.shape[1]
    tk = _row_tile(k)
    tm = _half_tile(m) if m > 1024 else m
    tn = _half_tile(n) if n > 1024 else n

    def body(a_ref, b_ref, o_ref):
        @pl.when(pl.program_id(2) == 0)
        def _():
            o_ref[...] = jnp.zeros_like(o_ref)

        o_ref[...] += _dot_tn(a_ref[...], b_ref[...])

    return pl.pallas_call(
        body, name=name, grid=(m // tm, n // tn, k // tk),
        in_specs=[pl.BlockSpec((tk, tm), lambda i, j, kk: (kk, i)),
                  pl.BlockSpec((tk, tn), lambda i, j, kk: (kk, j))],
        out_specs=pl.BlockSpec((tm, tn), lambda i, j, kk: (i, j)),
        out_shape=jax.ShapeDtypeStruct((m, n), F32),
        compiler_params=_params("arbitrary", "arbitrary", "arbitrary"),
    )(a, b)


def _norm_proj(name, h, g, w):
    rows, d = h.shape
    n = w.shape[1]
    tm, tn = _row_tile(rows), _half_tile(n)

    def body(h_ref, g_ref, w_ref, o_ref, n_ref):
        @pl.when(pl.program_id(1) == 0)
        def _():
            nv, _, _ = _rms_fwd(h_ref[...], g_ref[...])
            n_ref[...] = nv.astype(BF16)

        o_ref[...] = _dot(n_ref[...], w_ref[...])

    return pl.pallas_call(
        body, name=name, grid=(rows // tm, n // tn),
        in_specs=[pl.BlockSpec((tm, d), lambda i, j: (i, 0)),
                  pl.BlockSpec((1, d), lambda i, j: (0, 0)),
                  pl.BlockSpec((d, tn), lambda i, j: (0, j))],
        out_specs=[pl.BlockSpec((tm, tn), lambda i, j: (i, j)),
                   pl.BlockSpec((tm, d), lambda i, j: (i, 0))],
        out_shape=[jax.ShapeDtypeStruct((rows, n), F32), jax.ShapeDtypeStruct((rows, d), BF16)],
        compiler_params=_params("arbitrary", "arbitrary"),
    )(h, g, w)


def _out_proj(name, h, sb, rw, w):
    rows, d = h.shape
    gw = sb.shape[1]
    tm = _row_tile(rows)

    def body(h_ref, sb_ref, rw_ref, w_ref, o_ref, mix_ref):
        mix_ref[:, :gw] = sb_ref[...].astype(BF16)
        mix_ref[:, gw:] = rw_ref[...].astype(BF16)
        o_ref[...] = h_ref[...] + _dot(mix_ref[...], w_ref[...])

    return pl.pallas_call(
        body, name=name, grid=(rows // tm,),
        in_specs=[pl.BlockSpec((tm, d), lambda i: (i, 0)),
                  pl.BlockSpec((tm, gw), lambda i: (i, 0)),
                  pl.BlockSpec((tm, gw), lambda i: (i, 0)),
                  pl.BlockSpec((2 * gw, d), lambda i: (0, 0))],
        out_specs=[pl.BlockSpec((tm, d), lambda i: (i, 0)),
                   pl.BlockSpec((tm, 2 * gw), lambda i: (i, 0))],
        out_shape=[jax.ShapeDtypeStruct((rows, d), F32), jax.ShapeDtypeStruct((rows, 2 * gw), BF16)],
        compiler_params=_params("arbitrary"),
    )(h, sb, rw, w)


def _out_proj_bwd(name, dh, w):
    rows, d = dh.shape
    k = w.shape[0]
    tm = _row_tile(rows)

    def body(dh_ref, w_ref, o_ref, dhb_ref):
        dhb = dh_ref[...].astype(BF16)
        dhb_ref[...] = dhb
        o_ref[...] = _dot_nt(dhb, w_ref[...])

    return pl.pallas_call(
        body, name=name, grid=(rows // tm,),
        in_specs=[pl.BlockSpec((tm, d), lambda i: (i, 0)),
                  pl.BlockSpec((k, d), lambda i: (0, 0))],
        out_specs=[pl.BlockSpec((tm, k), lambda i: (i, 0)),
                   pl.BlockSpec((tm, d), lambda i: (i, 0))],
        out_shape=[jax.ShapeDtypeStruct((rows, k), F32), jax.ShapeDtypeStruct((rows, d), BF16)],
        compiler_params=_params("arbitrary"),
    )(dh, w)


def _norm_proj_bwd(name, dproj, w, h, g, dh):
    rows, n = dproj.shape
    d = w.shape[0]
    tm = _row_tile(rows)

    def body(dp_ref, w_ref, h_ref, g_ref, dh_ref, o_ref, dg_ref):
        @pl.when(pl.program_id(0) == 0)
        def _():
            dg_ref[...] = jnp.zeros_like(dg_ref)

        dn = _dot_nt(dp_ref[...], w_ref[...])
        gv = g_ref[...]
        _, xhat, rstd = _rms_fwd(h_ref[...], gv)
        dx, dg = _rms_bwd(dn, xhat, rstd, gv)
        o_ref[...] = dh_ref[...] + dx
        dg_ref[...] += dg

    return pl.pallas_call(
        body, name=name, grid=(rows // tm,),
        in_specs=[pl.BlockSpec((tm, n), lambda i: (i, 0)),
                  pl.BlockSpec((d, n), lambda i: (0, 0)),
                  pl.BlockSpec((tm, d), lambda i: (i, 0)),
                  pl.BlockSpec((1, d), lambda i: (0, 0)),
                  pl.BlockSpec((tm, d), lambda i: (i, 0))],
        out_specs=[pl.BlockSpec((tm, d), lambda i: (i, 0)),
                   pl.BlockSpec((1, d), lambda i: (0, 0))],
        out_shape=[jax.ShapeDtypeStruct((rows, d), F32), jax.ShapeDtypeStruct((1, d), F32)],
        compiler_params=_params("arbitrary"),
    )(dproj, w, h, g, dh)


def _loss_head(name, h, g, tgt):
    rows, d = h.shape
    tm = _row_tile(rows)

    def body(h_ref, g_ref, t_ref, loss_ref, dh_ref, dg_ref):
        i = pl.program_id(0)

        @pl.when(i == 0)
        def _():
            loss_ref[...] = jnp.zeros_like(loss_ref)
            dg_ref[...] = jnp.zeros_like(dg_ref)

        gv = g_ref[...]
        y, xhat, rstd = _rms_fwd(h_ref[...], gv)
        row = i * tm + lax.broadcasted_iota(jnp.int32, (tm, 1), 0)
        diff = jnp.where(row >= ROW0, y - t_ref[...], 0.0)
        part = 0.5 * jnp.sum(jnp.sum(diff * diff, axis=-1, keepdims=True), axis=0, keepdims=True) / d
        loss_ref[...] += jnp.broadcast_to(part, loss_ref.shape)
        dx, dg = _rms_bwd(diff / d, xhat, rstd, gv)
        dh_ref[...] = dx
        dg_ref[...] += dg

    return pl.pallas_call(
        body, name=name, grid=(rows // tm,),
        in_specs=[pl.BlockSpec((tm, d), lambda i: (i, 0)),
                  pl.BlockSpec((1, d), lambda i: (0, 0)),
                  pl.BlockSpec((tm, d), lambda i: (i, 0))],
        out_specs=[pl.BlockSpec((8, 128), lambda i: (0, 0)),
                   pl.BlockSpec((tm, d), lambda i: (i, 0)),
                   pl.BlockSpec((1, d), lambda i: (0, 0))],
        out_shape=[jax.ShapeDtypeStruct((8, 128), F32),
                   jax.ShapeDtypeStruct((rows, d), F32),
                   jax.ShapeDtypeStruct((1, d), F32)],
        compiler_params=_params("arbitrary"),
    )(h, g, tgt)


def _sb_block(qb, kb, i, jb, scale):
    bq, bk = qb.shape[0], kb.shape[0]
    z = _dot_nt(qb, kb) * scale
    qpos = i * bq + lax.broadcasted_iota(jnp.int32, (bq, bk), 0)
    kpos = jb * bk + lax.broadcasted_iota(jnp.int32, (bq, bk), 1)
    valid = (kpos < qpos) & (kpos >= META_PAD)
    e = jnp.exp(-jnp.abs(z))
    log_keep = jnp.where(valid, -(jnp.maximum(z, 0.0) + jnp.log(1.0 + e)), 0.0)
    return z, valid, e, log_keep


def _tri2(n, cmp):
    r = lax.broadcasted_iota(jnp.int32, (2 * n, n), 0) % n
    c = lax.broadcasted_iota(jnp.int32, (2 * n, n), 1)
    return cmp(r, c).astype(BF16)


def _dot_split(x, t2):
    hi, lo = _split2(x)
    return _dot(jnp.concatenate([hi, lo], axis=1), t2)


ATT_HEADS = 2
ATT_CUT = -104.0


def _sb_fwd(name, q, k, v):
    nh, rows, dh = q.shape
    bq, bk, hg = _row_tile(rows), ATT_BLOCK, ATT_HEADS
    per = bq // bk
    scale = dh ** -0.5

    def body(q_ref, k_ref, v_ref, o_ref, rt_ref, cnt_ref):
        i = pl.program_id(1)
        after = _tri2(bk, lambda r, c: r > c)
        nkb = (i + 1) * per

        def live(state):
            n, carry = state
            top = jnp.max(carry[0][0])
            for hh in range(1, hg):
                top = jnp.maximum(top, jnp.max(carry[hh][0]))
            return (n < nkb) & (top >= ATT_CUT)

        def step(state):
            n, carry = state
            jb = nkb - 1 - n
            off = pl.multiple_of(jb * bk, bk)
            out = []
            for hh in range(hg):
                rest, acc = carry[hh]
                kb = k_ref[hh, pl.ds(off, bk), :]
                vb = v_ref[hh, pl.ds(off, bk), :]
                z, valid, _, log_keep = _sb_block(q_ref[hh], kb, i, jb, scale)
                log_rest = rest + _dot_split(log_keep, after)
                attn = jnp.where(valid, jnp.exp(z + log_keep + log_rest), 0.0)
                out.append((rest + jnp.sum(log_keep, axis=-1, keepdims=True), acc + _dot(attn.astype(BF16), vb)))
            return n + 1, tuple(out)

        init = tuple((jnp.zeros((bq, 1), F32), jnp.zeros((bq, dh), F32)) for _ in range(hg))
        n, res = lax.while_loop(live, step, (jnp.int32(0), init))
        for hh in range(hg):
            rt_ref[hh] = res[hh][0]
            o_ref[hh] = res[hh][1]
            cnt_ref[hh] = jnp.full((bq, 1), n, F32)

    return pl.pallas_call(
        body, name=name, grid=(nh // hg, rows // bq),
        in_specs=[pl.BlockSpec((hg, bq, dh), lambda h, i: (h, i, 0)),
                  pl.BlockSpec((hg, rows, dh), lambda h, i: (h, 0, 0)),
                  pl.BlockSpec((hg, rows, dh), lambda h, i: (h, 0, 0))],
        out_specs=[pl.BlockSpec((hg, bq, dh), lambda h, i: (h, i, 0)),
                   pl.BlockSpec((hg, bq, 1), lambda h, i: (h, i, 0)),
                   pl.BlockSpec((hg, bq, 1), lambda h, i: (h, i, 0))],
        out_shape=[jax.ShapeDtypeStruct((nh, rows, dh), F32), jax.ShapeDtypeStruct((nh, rows, 1), F32),
                   jax.ShapeDtypeStruct((nh, rows, 1), F32)],
        compiler_params=_params("arbitrary", "arbitrary"),
    )(q, k, v)


def _sb_bwd(name, q, k, v, rt, cnt, do):
    nh, rows, dh = q.shape
    bq, bk, hg = _row_tile(rows), ATT_BLOCK, ATT_HEADS
    per = bq // bk
    scale = dh ** -0.5

    def body(q_ref, k_ref, v_ref, rt_ref, cnt_ref, do_ref, dq_ref, dk_ref, dv_ref):
        i = pl.program_id(1)

        @pl.when(i == 0)
        def _():
            dk_ref[...] = jnp.zeros_like(dk_ref)
            dv_ref[...] = jnp.zeros_like(dv_ref)

        upto = _tri2(bk, lambda r, c: r <= c)
        before = _tri2(bk, lambda r, c: r < c)
        nkb = (i + 1) * per
        first = nkb - jnp.max(cnt_ref[0]).astype(jnp.int32)

        def step(jb, carry):
            off = pl.multiple_of(jb * bk, bk)
            out = []
            for hh in range(hg):
                keep_sum, g_sum, dq = carry[hh]
                qb, dob = q_ref[hh], do_ref[hh]
                kb = k_ref[hh, pl.ds(off, bk), :]
                vb = v_ref[hh, pl.ds(off, bk), :]
                z, valid, e, log_keep = _sb_block(qb, kb, i, jb, scale)
                log_rest = rt_ref[hh] - keep_sum - _dot_split(log_keep, upto)
                attn = jnp.where(valid, jnp.exp(z + log_keep + log_rest), 0.0)
                g = attn * _dot_nt(dob, vb)
                g_before = g_sum + _dot_split(g, before)
                inv = 1.0 / (1.0 + e)
                sig = jnp.where(z >= 0, inv, e * inv)
                dz = (jnp.where(valid, g * (1.0 - sig) - g_before * sig, 0.0) * scale).astype(BF16)
                dk_ref[hh, pl.ds(off, bk), :] += _dot_tn(dz, qb)
                dv_ref[hh, pl.ds(off, bk), :] += _dot_tn(attn.astype(BF16), dob)
                out.append((keep_sum + jnp.sum(log_keep, axis=-1, keepdims=True),
                            g_sum + jnp.sum(g, axis=-1, keepdims=True),
                            dq + _dot(dz, kb)))
            return tuple(out)

        zero = jnp.zeros((bq, 1), F32)
        res = lax.fori_loop(first, nkb, step, tuple((zero, zero, jnp.zeros((bq, dh), F32)) for _ in range(hg)))
        for hh in range(hg):
            dq_ref[hh] = res[hh][2]

    return pl.pallas_call(
        body, name=name, grid=(nh // hg, rows // bq),
        in_specs=[pl.BlockSpec((hg, bq, dh), lambda h, i: (h, i, 0)),
                  pl.BlockSpec((hg, rows, dh), lambda h, i: (h, 0, 0)),
                  pl.BlockSpec((hg, rows, dh), lambda h, i: (h, 0, 0)),
                  pl.BlockSpec((hg, bq, 1), lambda h, i: (h, i, 0)),
                  pl.BlockSpec((hg, bq, 1), lambda h, i: (h, i, 0)),
                  pl.BlockSpec((hg, bq, dh), lambda h, i: (h, i, 0))],
        out_specs=[pl.BlockSpec((hg, bq, dh), lambda h, i: (h, i, 0)),
                   pl.BlockSpec((hg, rows, dh), lambda h, i: (h, 0, 0)),
                   pl.BlockSpec((hg, rows, dh), lambda h, i: (h, 0, 0))],
        out_shape=[jax.ShapeDtypeStruct((nh, rows, dh), F32)] * 3,
        compiler_params=_params("arbitrary", "arbitrary"),
    )(q, k, v, rt, cnt, do)


def _head_sum(x, ones_bd):
    return _dot_split(x, ones_bd)


def _rwkv_pre(p, p_prev, mu, w0, a0, k_k, k_a, w_up, a_up, g_up, ones_bd):
    xs = p + (p_prev - p) * mu
    r = xs[:, :GROUP]
    k0 = xs[:, GROUP:2 * GROUP]
    v = xs[:, 2 * GROUP:3 * GROUP]
    lo = xs[:, 3 * GROUP:]
    wa = w0 + _dot(jnp.tanh(lo).astype(BF16), w_up.astype(BF16))
    w = -(jnp.maximum(-wa, 0.0) + jnp.log(1.0 + jnp.exp(-jnp.abs(wa)))) - 0.5
    log_decay = -jnp.exp(w)
    alpha = _sigmoid(a0 + _dot(lo.astype(BF16), a_up.astype(BF16)))
    gate = _dot(_sigmoid(lo).astype(BF16), g_up.astype(BF16))
    kk = k0 * k_k
    kk = kk * lax.rsqrt(jnp.maximum(_head_sum(kk * kk, ones_bd), 1e-24))
    k = k0 * (1.0 + (alpha - 1.0) * k_a)
    return r, log_decay, k, v, -kk, kk * alpha, gate


def _rwkv_post(y, r, k, v, gate, lnx_w, lnx_b, r_k, ones_bd):
    mean = _head_sum(y, ones_bd) * (1.0 / HEAD)
    yc = y - mean
    var = _head_sum(yc * yc, ones_bd) * (1.0 / HEAD)
    yn = yc * lax.rsqrt(var + LNX_EPS) * lnx_w + lnx_b
    bonus = _head_sum(r * k * r_k, ones_bd) * v
    return (yn + bonus) * gate


_PRE_VEC = 5
_PRE_MAT = 3


def _rwkv_pre_fwd(name, p, p_prev, vecs, mats, ones_bd):
    rows = p.shape[0]
    tm = _row_tile(rows)
    row_spec = lambda w: pl.BlockSpec((tm, w), lambda i: (i, 0))
    full = lambda a: pl.BlockSpec(a.shape, lambda i: (0,) * a.ndim)

    def body(p_ref, pp_ref, *refs):
        ins = [r[...] for r in refs[:_PRE_VEC + _PRE_MAT + 1]]
        outs = refs[_PRE_VEC + _PRE_MAT + 1:]
        for o_ref, val in zip(outs, _rwkv_pre(p_ref[...], pp_ref[...], *ins)):
            o_ref[...] = val

    return pl.pallas_call(
        body, name=name, grid=(rows // tm,),
        in_specs=[row_spec(RW_COLS), row_spec(RW_COLS)] + [full(a) for a in (*vecs, *mats, ones_bd)],
        out_specs=[row_spec(GROUP)] * 7,
        out_shape=[jax.ShapeDtypeStruct((rows, GROUP), F32)] * 7,
        compiler_params=_params("arbitrary"),
    )(p, p_prev, *vecs, *mats, ones_bd)


def _rwkv_pre_bwd(name, p, p_prev, vecs, mats, ones_bd, cts_a, cts_b):
    rows = p.shape[0]
    tm = _row_tile(rows)
    n_par = _PRE_VEC + _PRE_MAT
    row_spec = lambda w: pl.BlockSpec((tm, w), lambda i: (i, 0))
    full = lambda a: pl.BlockSpec(a.shape, lambda i: (0,) * a.ndim)

    def body(*refs):
        p_ref, pp_ref = refs[0], refs[1]
        par = [r[...] for r in refs[2:2 + n_par]]
        ones = refs[2 + n_par][...]
        cta = [r[...] for r in refs[3 + n_par:10 + n_par]]
        ctb = [r[...] for r in refs[10 + n_par:13 + n_par]]
        outs = refs[13 + n_par:]
        ct = (cta[0] + ctb[0], cta[1], cta[2] + ctb[1], cta[3] + ctb[2], cta[4], cta[5], cta[6])
        _, vjp = jax.vjp(lambda pv, ppv, *pr: _rwkv_pre(pv, ppv, *pr, ones), p_ref[...], pp_ref[...], *par)
        grads = vjp(ct)
        outs[0][...] = grads[0]
        outs[1][...] = grads[1]

        @pl.when(pl.program_id(0) == 0)
        def _():
            for o_ref in outs[2:]:
                o_ref[...] = jnp.zeros_like(o_ref)

        for o_ref, gval in zip(outs[2:], grads[2:]):
            o_ref[...] += gval

    par_arrays = (*vecs, *mats)
    return pl.pallas_call(
        body, name=name, grid=(rows // tm,),
        in_specs=([row_spec(RW_COLS)] * 2 + [full(a) for a in (*par_arrays, ones_bd)]
                  + [row_spec(GROUP)] * 10),
        out_specs=[row_spec(RW_COLS)] * 2 + [full(a) for a in par_arrays],
        out_shape=([jax.ShapeDtypeStruct((rows, RW_COLS), F32)] * 2
                   + [jax.ShapeDtypeStruct(a.shape, F32) for a in par_arrays]),
        compiler_params=_params("arbitrary"),
    )(p, p_prev, *par_arrays, ones_bd, *cts_a, *cts_b)


def _rwkv_post_fwd(name, y, r, k, v, gate, vecs, ones_bd):
    rows = y.shape[0]
    tm = _row_tile(rows)
    row_spec = pl.BlockSpec((tm, GROUP), lambda i: (i, 0))
    full = lambda a: pl.BlockSpec(a.shape, lambda i: (0,) * a.ndim)

    def body(*refs):
        vals = [r_[...] for r_ in refs[:-1]]
        refs[-1][...] = _rwkv_post(*vals)

    return pl.pallas_call(
        body, name=name, grid=(rows // tm,),
        in_specs=[row_spec] * 5 + [full(a) for a in (*vecs, ones_bd)],
        out_specs=row_spec,
        out_shape=jax.ShapeDtypeStruct((rows, GROUP), F32),
        compiler_params=_params("arbitrary"),
    )(y, r, k, v, gate, *vecs, ones_bd)


def _rwkv_post_bwd(name, y, r, k, v, gate, vecs, ones_bd, dout):
    rows = y.shape[0]
    tm = _row_tile(rows)
    row_spec = pl.BlockSpec((tm, GROUP), lambda i: (i, 0))
    full = lambda a: pl.BlockSpec(a.shape, lambda i: (0,) * a.ndim)

    def body(*refs):
        vals = [r_[...] for r_ in refs[:8]]
        ones = refs[8][...]
        dout_v = refs[9][...]
        outs = refs[10:]
        _, vjp = jax.vjp(lambda *a: _rwkv_post(*a, ones), *vals)
        grads = vjp(dout_v)
        for o_ref, gval in zip(outs[:5], grads[:5]):
            o_ref[...] = gval

        @pl.when(pl.program_id(0) == 0)
        def _():
            for o_ref in outs[5:]:
                o_ref[...] = jnp.zeros_like(o_ref)

        for o_ref, gval in zip(outs[5:], grads[5:]):
            o_ref[...] += gval

    return pl.pallas_call(
        body, name=name, grid=(rows // tm,),
        in_specs=[row_spec] * 5 + [full(a) for a in (*vecs, ones_bd)] + [row_spec],
        out_specs=[row_spec] * 5 + [full(a) for a in vecs],
        out_shape=[jax.ShapeDtypeStruct((rows, GROUP), F32)] * 5 + [jax.ShapeDtypeStruct(a.shape, F32) for a in vecs],
        compiler_params=_params("arbitrary"),
    )(y, r, k, v, gate, *vecs, ones_bd, dout)


_NN = (((2,), (1,)), ((0,), (0,)))
_NT = (((2,), (2,)), ((0,), (0,)))
_TN = (((1,), (1,)), ((0,), (0,)))


_BWD_FORMS = {"nn": (("nt", False), ("tn", False)),
              "nt": (("nn", False), ("tn", True)),
              "tn": (("nt", True), ("nn", False))}
_DIMS = {"nn": _NN, "nt": _NT, "tn": _TN}


def _bdot(a, b, form):
    return lax.dot_general(a.astype(BF16), b.astype(BF16), _DIMS[form], preferred_element_type=F32)


@functools.partial(jax.custom_vjp, nondiff_argnums=(2,))
def _bmm(a, b, form):
    return _bdot(a, b, form)


def _bmm_fwd(a, b, form):
    return _bdot(a, b, form), (a.astype(BF16), b.astype(BF16))


def _bmm_bwd(form, res, dc):
    a, b = res
    (fa, swap_a), (fb, swap_b) = _BWD_FORMS[form]
    da = _bdot(b, dc, fa) if swap_a else _bdot(dc, b, fa)
    db = _bdot(dc, a, fb) if swap_b else _bdot(a, dc, fb)
    return da, db


_bmm.defvjp(_bmm_fwd, _bmm_bwd)


@jax.custom_vjp
def _cumsum_steps(x):
    return _tri_apply(x, lambda r, c: r >= c)


def _tri_apply(x, cmp):
    nh, c, _ = x.shape
    tri = cmp(lax.broadcasted_iota(jnp.int32, (c, c), 0), lax.broadcasted_iota(jnp.int32, (c, c), 1))
    tri = jnp.broadcast_to(tri.astype(BF16)[None], (nh, c, c))
    hi, lo = _split2(x)
    return (lax.dot_general(tri, hi, _NN, preferred_element_type=F32)
            + lax.dot_general(tri, lo, _NN, preferred_element_type=F32))


_cumsum_steps.defvjp(lambda x: (_cumsum_steps(x), None), lambda _, d: (_tri_apply(d, lambda r, c: r <= c),))


def _chunk(state, r, log_w, k, v, a, b):
    nh, c, _ = r.shape
    row = lax.broadcasted_iota(jnp.int32, (c, c), 0)
    col = lax.broadcasted_iota(jnp.int32, (c, c), 1)
    cum = _cumsum_steps(log_w)
    mid = cum[:, c // 2 - 1:c // 2, :]
    a_t = a * jnp.exp(cum - log_w - mid)
    r_t = r * jnp.exp(cum - mid)
    back = jnp.exp(mid - cum)
    b_t = b * back
    k_t = k * back
    strict, incl = (row > col)[None], (row >= col)[None]
    ar = jnp.concatenate([a_t, r_t], axis=1)
    on_b = _bmm(ar, b_t, "nt")
    on_k = _bmm(ar, k_t, "nt")
    n_mat = jnp.where(strict, on_b[:, :c], 0.0)
    p_mat = jnp.where(incl, on_b[:, c:], 0.0)
    m_mat = jnp.where(strict, on_k[:, :c], 0.0)
    q_mat = jnp.where(incl, on_k[:, c:], 0.0)
    inv, power, span = n_mat, _bmm(n_mat, n_mat, "nn"), 2
    while span < c:
        both = _bmm(jnp.concatenate([power, inv], axis=1), power, "nn")
        inv = inv + power + both[:, c:]
        power = both[:, :c]
        span *= 2
    s_mid = state * jnp.swapaxes(jnp.exp(mid), 1, 2)
    x = _bmm(jnp.concatenate([a_t, m_mat], axis=2), jnp.concatenate([s_mid, v], axis=1), "nn")
    u = x + _bmm(inv, x, "nn")
    y = _bmm(jnp.concatenate([r_t, p_mat, q_mat], axis=2), jnp.concatenate([s_mid, u, v], axis=1), "nn")
    grown = _bmm(jnp.concatenate([b_t, k_t], axis=1), jnp.concatenate([u, v], axis=1), "tn")
    s_new = (s_mid + grown) * jnp.swapaxes(jnp.exp(cum[:, c - 1:c, :] - mid), 1, 2)
    return y, s_new


def _scan_fwd(name, ops):
    nh, rows, dh = ops[0].shape
    nc = rows // CHUNK
    spec = pl.BlockSpec((nh, CHUNK, dh), lambda c: (0, c, 0))

    def body(r_ref, w_ref, k_ref, v_ref, a_ref, b_ref, y_ref, st_ref, state):
        @pl.when(pl.program_id(0) == 0)
        def _():
            state[...] = jnp.zeros_like(state)

        st_ref[0] = state[...]
        y, s_new = _chunk(state[...], r_ref[...], w_ref[...], k_ref[...], v_ref[...], a_ref[...], b_ref[...])
        y_ref[...] = y
        state[...] = s_new

    return pl.pallas_call(
        body, name=name, grid=(nc,),
        in_specs=[spec] * 6,
        out_specs=[spec, pl.BlockSpec((1, nh, dh, dh), lambda c: (c, 0, 0, 0))],
        out_shape=[jax.ShapeDtypeStruct((nh, rows, dh), F32), jax.ShapeDtypeStruct((nc, nh, dh, dh), F32)],
        scratch_shapes=[pltpu.VMEM((nh, dh, dh), F32)],
        compiler_params=_params("arbitrary"),
    )(*ops)


def _scan_bwd(name, ops, states, dy):
    nh, rows, dh = ops[0].shape
    nc = rows // CHUNK
    spec = pl.BlockSpec((nh, CHUNK, dh), lambda c: (0, nc - 1 - c, 0))

    def body(r_ref, w_ref, k_ref, v_ref, a_ref, b_ref, st_ref, dy_ref, *rest):
        outs, dstate = rest[:6], rest[6]

        @pl.when(pl.program_id(0) == 0)
        def _():
            dstate[...] = jnp.zeros_like(dstate)

        _, vjp = jax.vjp(_chunk, st_ref[0], r_ref[...], w_ref[...], k_ref[...], v_ref[...], a_ref[...], b_ref[...])
        grads = vjp((dy_ref[...], dstate[...]))
        dstate[...] = grads[0]
        for o_ref, gval in zip(outs, grads[1:]):
            o_ref[...] = gval

    return pl.pallas_call(
        body, name=name, grid=(nc,),
        in_specs=[spec] * 6 + [pl.BlockSpec((1, nh, dh, dh), lambda c: (nc - 1 - c, 0, 0, 0)), spec],
        out_specs=[spec] * 6,
        out_shape=[jax.ShapeDtypeStruct((nh, rows, dh), F32)] * 6,
        scratch_shapes=[pltpu.VMEM((nh, dh, dh), F32)],
        compiler_params=_params("arbitrary"),
    )(*ops, states, dy)


def _heads(x):
    return x.reshape(x.shape[0], N_HEADS, HEAD).transpose(1, 0, 2)


def _unheads(x):
    return x.transpose(1, 0, 2).reshape(x.shape[1], GROUP)


def _shift_down(x):
    return jnp.concatenate([jnp.zeros((1, x.shape[1]), x.dtype), x[:-1]], axis=0)


def _shift_up(x):
    return jnp.concatenate([x[1:], jnp.zeros((1, x.shape[1]), x.dtype)], axis=0)


def _pad_rows(x, rows):
    return jnp.concatenate([x, jnp.zeros((rows - x.shape[0],) + x.shape[1:], x.dtype)], axis=0)


def _pad_cols(x, cols):
    return jnp.concatenate([x, jnp.zeros(x.shape[:-1] + (cols - x.shape[-1],), x.dtype)], axis=-1)


def _lora_pad(w_up, a_up, g_up):
    z = lambda n: jnp.zeros((n, GROUP), F32)
    return (jnp.concatenate([w_up, z(LORA_PAD - LORA_W)], 0),
            jnp.concatenate([z(LORA_W), a_up, z(LORA_PAD - LORA_W - LORA_A)], 0),
            jnp.concatenate([z(LORA_W + LORA_A), g_up, z(LORA_PAD - LORA_W - LORA_A - LORA_G)], 0))


def _local_step(x, tgt, w):
    d = x.shape[1]
    zeros = jnp.zeros((META_PAD, d), F32)
    h0 = jnp.concatenate([zeros, w["meta_tokens"], x], axis=0)
    tgt_p = jnp.concatenate([jnp.zeros((ROW0, d), F32), tgt], axis=0)
    ones_bd = ((lax.broadcasted_iota(jnp.int32, (2 * GROUP, GROUP), 0) % GROUP) // HEAD
               == lax.broadcasted_iota(jnp.int32, (2 * GROUP, GROUP), 1) // HEAD).astype(BF16)
    w_in = _pad_cols(w["w_in"], IN_COLS_PAD)
    pre_vecs = (_pad_cols(w["rwkv_mu"], RW_COLS), w["rwkv_w0"], w["rwkv_a0"], w["rwkv_k_k"], w["rwkv_k_a"])
    pre_mats = _lora_pad(w["rwkv_w_up"], w["rwkv_a_up"], w["rwkv_g_up"])
    post_vecs = (w["rwkv_lnx_w"], w["rwkv_lnx_b"], w["rwkv_r_k"].reshape(1, GROUP))

    h1, a1, b1 = _ffn_fwd("ffn1_fwd", h0, w["ffn1_norm"], w["ffn1_w_gate"], w["ffn1_w_up"], w["ffn1_w_down"])
    proj, n2 = _norm_proj("in_proj", h1, w["mix_norm"], w_in)
    q, k, v = (_heads(proj[:, j * GROUP:(j + 1) * GROUP]).astype(BF16) for j in range(3))
    sb, rest_total, visited = _sb_fwd("sb_fwd", q, k, v)
    p = proj[:, 3 * GROUP:]
    p_prev = _shift_down(p)
    pre = _rwkv_pre_fwd("rwkv_pre_fwd", p, p_prev, pre_vecs, pre_mats, ones_bd)
    scan_ops = tuple(_heads(t) for t in pre[:6])
    y_h, states = _scan_fwd("rwkv_scan_fwd", scan_ops)
    y = _unheads(y_h)
    rw = _rwkv_post_fwd("rwkv_post_fwd", y, pre[0], pre[2], pre[3], pre[6], post_vecs, ones_bd)
    h2, mix = _out_proj("out_proj", h1, _unheads(sb), rw, w["w_out"])
    h3, a2, b2 = _ffn_fwd("ffn2_fwd", h2, w["ffn2_norm"], w["ffn2_w_gate"], w["ffn2_w_up"], w["ffn2_w_down"])
    loss8, dh3, g_final = _loss_head("loss_head", h3, w["final_norm"].reshape(1, d), tgt_p)

    g = {"final_norm": g_final.reshape(d)}
    dh2, da2, db2, s2, n3, dhh3, g["ffn2_norm"] = _ffn_bwd(
        "ffn2_bwd", dh3, h2, w["ffn2_norm"], a2, b2, w["ffn2_w_gate"], w["ffn2_w_up"], w["ffn2_w_down"])
    g["ffn2_w_gate"] = _mm_tn("ffn2_dgate", n3, da2)
    g["ffn2_w_up"] = _mm_tn("ffn2_dup", n3, db2)
    g["ffn2_w_down"] = _mm_tn("ffn2_ddown", s2, dhh3)
    dmix, dh2b = _out_proj_bwd("out_proj_bwd", dh2, w["w_out"])
    g["w_out"] = _mm_tn("out_proj_dw", mix, dh2b)
    dq, dk, dv = _sb_bwd("sb_bwd", q, k, v, rest_total, visited, _heads(dmix[:, :GROUP]).astype(BF16))
    post_g = _rwkv_post_bwd("rwkv_post_bwd", y, pre[0], pre[2], pre[3], pre[6], post_vecs, ones_bd, dmix[:, GROUP:])
    g["rwkv_lnx_w"], g["rwkv_lnx_b"] = post_g[5], post_g[6]
    g["rwkv_r_k"] = post_g[7].reshape(1, N_HEADS, HEAD)
    scan_g = _scan_bwd("rwkv_scan_bwd", scan_ops, states, _heads(post_g[0]))
    cts_a = tuple(_unheads(t) for t in scan_g) + (post_g[4],)
    pre_g = _rwkv_pre_bwd("rwkv_pre_bwd", p, p_prev, pre_vecs, pre_mats, ones_bd, cts_a, post_g[1:4])
    g["rwkv_mu"] = pre_g[2][:, :w["rwkv_mu"].shape[1]]
    g["rwkv_w0"], g["rwkv_a0"], g["rwkv_k_k"], g["rwkv_k_a"] = pre_g[3:7]
    g["rwkv_w_up"] = pre_g[7][:LORA_W]
    g["rwkv_a_up"] = pre_g[8][LORA_W:LORA_W + LORA_A]
    g["rwkv_g_up"] = pre_g[9][LORA_W + LORA_A:LORA_W + LORA_A + LORA_G]
    dp = pre_g[0] + _shift_up(pre_g[1])
    live = (jnp.arange(h0.shape[0]) >= META_PAD)[:, None]
    dproj = jnp.where(live, jnp.concatenate([_unheads(dq), _unheads(dk), _unheads(dv), dp], axis=1), 0.0).astype(BF16)
    g["w_in"] = _mm_tn("in_proj_dw", n2, dproj)[:, :w["w_in"].shape[1]]
    dh1, g["mix_norm"] = _norm_proj_bwd("in_proj_bwd", dproj, w_in, h1, w["mix_norm"], dh2)
    dh0, da1, db1, s1, n1, dhh1, g["ffn1_norm"] = _ffn_bwd(
        "ffn1_bwd", dh1, h0, w["ffn1_norm"], a1, b1, w["ffn1_w_gate"], w["ffn1_w_up"], w["ffn1_w_down"])
    g["ffn1_w_gate"] = _mm_tn("ffn1_dgate", n1, da1)
    g["ffn1_w_up"] = _mm_tn("ffn1_dup", n1, db1)
    g["ffn1_w_down"] = _mm_tn("ffn1_ddown", s1, dhh1)
    g["meta_tokens"] = dh0[META_PAD:ROW0]
    return loss8[0, 0], dh0[ROW0:], g


N_CHIPS = 4
N_DEV = 8
HBM = pl.BlockSpec(memory_space=pltpu.HBM)


def _place():
    return lax.axis_index("x"), lax.axis_index("y"), lax.axis_index("c")


def _other_chips(x, y):
    return [(1 - x, y), (x, 1 - y), (1 - x, 1 - y)]


def _gather_shards(name, shards):
    n = len(shards)
    half = [s.shape[0] // 2 for s in shards]

    def body(*refs):
        ins, outs = refs[:n], refs[n:2 * n]
        send, recv, local, d2d_send, d2d_recv = refs[2 * n:]
        x, y, c = _place()
        me = 2 * x + y
        chips = _other_chips(x, y)
        own = [pltpu.make_async_copy(ins[k], outs[k].at[me], local.at[k]) for k in range(n)]
        for cp in own:
            cp.start()

        def rows_of(k, h):
            return pl.ds(pl.multiple_of(h * half[k], 8), half[k])

        def copy(j, k, slot):
            return pltpu.make_async_remote_copy(
                src_ref=ins[k].at[rows_of(k, c)], dst_ref=outs[k].at[slot, rows_of(k, c)],
                send_sem=send.at[j * n + k], recv_sem=recv.at[j * n + k],
                device_id=(chips[j][0], chips[j][1], c), device_id_type=MESH)

        def passed(j, k, h):
            slot = 2 * chips[j][0] + chips[j][1]
            return pltpu.make_async_remote_copy(
                src_ref=outs[k].at[slot, rows_of(k, h)], dst_ref=outs[k].at[slot, rows_of(k, h)],
                send_sem=d2d_send.at[j * n + k], recv_sem=d2d_recv.at[j * n + k],
                device_id=(x, y, 1 - c), device_id_type=MESH)

        sent = [copy(j, k, me) for j in range(3) for k in range(n)]
        for cp in sent:
            cp.start()
        for j in range(3):
            for k in range(n):
                copy(j, k, 2 * chips[j][0] + chips[j][1]).wait_recv()
                passed(j, k, c).start()
        for j in range(3):
            for k in range(n):
                passed(j, k, 1 - c).wait_recv()
        for cp in sent:
            cp.wait_send()
        for j in range(3):
            for k in range(n):
                passed(j, k, c).wait_send()
        for cp in own:
            cp.wait()

    return pl.pallas_call(
        body, name=name,
        in_specs=[HBM] * n, out_specs=[HBM] * n,
        out_shape=[jax.ShapeDtypeStruct((N_CHIPS,) + s.shape, s.dtype) for s in shards],
        scratch_shapes=[pltpu.SemaphoreType.DMA((3 * n,)), pltpu.SemaphoreType.DMA((3 * n,)),
                        pltpu.SemaphoreType.DMA((n,)), pltpu.SemaphoreType.DMA((3 * n,)),
                        pltpu.SemaphoreType.DMA((3 * n,))],
    )(*shards)


def _pair_exchange(name, parts):
    n = len(parts)
    half = [s.shape[1] // 2 for s in parts]

    def body(*refs):
        ins, outs = refs[:n], refs[n:2 * n]
        send, recv = refs[2 * n:]
        x, y, c = _place()

        def copy(k):
            rows = pl.ds(pl.multiple_of((1 - c) * half[k], 8), half[k])
            return pltpu.make_async_remote_copy(
                src_ref=ins[k].at[:, rows], dst_ref=outs[k], send_sem=send.at[k], recv_sem=recv.at[k],
                device_id=(x, y, 1 - c), device_id_type=MESH)

        for k in range(n):
            copy(k).start()
        for k in range(n):
            copy(k).wait_recv()
        for k in range(n):
            copy(k).wait_send()

    return pl.pallas_call(
        body, name=name,
        in_specs=[HBM] * n, out_specs=[HBM] * n,
        out_shape=[jax.ShapeDtypeStruct((s.shape[0], s.shape[1] // 2, s.shape[2]), s.dtype) for s in parts],
        scratch_shapes=[pltpu.SemaphoreType.DMA((n,)), pltpu.SemaphoreType.DMA((n,))],
    )(*parts)


def _pair_add(name, part, other):
    nch, rows, cols = part.shape
    half = rows // 2

    def body(p_ref, o_ref, out_ref):
        c = lax.axis_index("c")
        mine = p_ref[0, pl.ds(pl.multiple_of(c * half, 16), half), :]
        out_ref[0] = (mine.astype(F32) + o_ref[0].astype(F32)).astype(out_ref.dtype)

    return pl.pallas_call(
        body, name=name, grid=(nch,),
        in_specs=[pl.BlockSpec((1, rows, cols), lambda j: (j, 0, 0)),
                  pl.BlockSpec((1, half, cols), lambda j: (j, 0, 0))],
        out_specs=pl.BlockSpec((1, half, cols), lambda j: (j, 0, 0)),
        out_shape=jax.ShapeDtypeStruct((nch, half, cols), part.dtype),
        compiler_params=_params("arbitrary"),
    )(part, other)


def _reduce_shards(name, parts):
    n = len(parts)

    def body(*refs):
        ins, got, sib = refs[:n], refs[n:2 * n], refs[2 * n:3 * n]
        send, recv, local, d2d_send, d2d_recv = refs[3 * n:]
        x, y, c = _place()
        me = 2 * x + y
        chips = _other_chips(x, y)
        own = [pltpu.make_async_copy(ins[k].at[me], got[k].at[me], local.at[k]) for k in range(n)]
        for cp in own:
            cp.start()

        def copy(j, k, shard, slot):
            return pltpu.make_async_remote_copy(
                src_ref=ins[k].at[shard], dst_ref=got[k].at[slot], send_sem=send.at[j * n + k],
                recv_sem=recv.at[j * n + k], device_id=(chips[j][0], chips[j][1], c), device_id_type=MESH)

        sent = [copy(j, k, 2 * chips[j][0] + chips[j][1], me) for j in range(3) for k in range(n)]
        for cp in sent:
            cp.start()

        def swap(k):
            return pltpu.make_async_remote_copy(
                src_ref=got[k], dst_ref=sib[k], send_sem=d2d_send.at[k], recv_sem=d2d_recv.at[k],
                device_id=(x, y, 1 - c), device_id_type=MESH)

        for k in range(n):
            own[k].wait()
            for j in range(3):
                copy(j, k, me, 2 * chips[j][0] + chips[j][1]).wait_recv()
            swap(k).start()
        for k in range(n):
            swap(k).wait_recv()
        for cp in sent:
            cp.wait_send()
        for k in range(n):
            swap(k).wait_send()

    return pl.pallas_call(
        body, name=name,
        in_specs=[HBM] * n, out_specs=[HBM] * (2 * n),
        out_shape=[jax.ShapeDtypeStruct(s.shape, s.dtype) for s in parts] * 2,
        scratch_shapes=[pltpu.SemaphoreType.DMA((3 * n,)), pltpu.SemaphoreType.DMA((3 * n,)),
                        pltpu.SemaphoreType.DMA((n,)), pltpu.SemaphoreType.DMA((n,)), pltpu.SemaphoreType.DMA((n,))],
    )(*parts)


def _all_reduce_small(name, vec):
    rows = vec.shape[0]

    def body(v_ref, o_ref, buf, send, recv):
        x, y, c = _place()
        me = 4 * x + 2 * y + c
        peers = [(x ^ (r >> 2), y ^ ((r >> 1) & 1), c ^ (r & 1)) for r in range(1, N_DEV)]

        def copy(r, slot):
            px, py, pc = peers[r]
            return pltpu.make_async_remote_copy(
                src_ref=v_ref, dst_ref=buf.at[slot], send_sem=send.at[r], recv_sem=recv.at[r],
                device_id=(px, py, pc), device_id_type=MESH)

        sent = [copy(r, me) for r in range(N_DEV - 1)]
        for cp in sent:
            cp.start()
        buf[me] = v_ref[...]
        for r in range(N_DEV - 1):
            px, py, pc = peers[r]
            copy(r, 4 * px + 2 * py + pc).wait_recv()
        total = buf[0]
        for dev in range(1, N_DEV):
            total = total + buf[dev]
        o_ref[...] = total
        for cp in sent:
            cp.wait_send()

    return pl.pallas_call(
        body, name=name,
        in_specs=[pl.BlockSpec(memory_space=pltpu.VMEM)], out_specs=pl.BlockSpec(memory_space=pltpu.VMEM),
        out_shape=jax.ShapeDtypeStruct(vec.shape, F32),
        scratch_shapes=[pltpu.VMEM((N_DEV, rows, 128), F32),
                        pltpu.SemaphoreType.DMA((N_DEV - 1,)), pltpu.SemaphoreType.DMA((N_DEV - 1,))],
        compiler_params=pltpu.CompilerParams(vmem_limit_bytes=VMEM_LIMIT),
    )(vec)


def _adamw(w, g, m, v):
    m = ADAM_B1 * m + (1.0 - ADAM_B1) * g
    v = ADAM_B2 * v + (1.0 - ADAM_B2) * (g * g)
    m_hat = m / (1.0 - ADAM_B1 ** ADAM_STEP)
    v_hat = v / (1.0 - ADAM_B2 ** ADAM_STEP)
    return -ADAM_LR * (m_hat / (jnp.sqrt(v_hat) + ADAM_EPS) + ADAM_WD * w), m, v


def _adamw_shard(name, w, m, v, got, sib):
    rows, cols = w.shape
    tr = rows // 4
    spec = pl.BlockSpec((tr, cols), lambda i: (i, 0))
    spec4 = pl.BlockSpec((N_CHIPS, tr, cols), lambda i: (0, i % 2, 0))

    def body(w_ref, m_ref, v_ref, got_ref, sib_ref, g_ref, d_ref, mo_ref, vo_ref):
        def four(ref):
            return ((ref[0].astype(F32) + ref[1].astype(F32)) + ref[2].astype(F32)) + ref[3].astype(F32)

        g = jnp.where(pl.program_id(0) // 2 == lax.axis_index("c"), four(got_ref), four(sib_ref))
        g_ref[...] = g
        d_ref[...], mo_ref[...], vo_ref[...] = _adamw(w_ref[...], g, m_ref[...], v_ref[...])

    return pl.pallas_call(
        body, name=name, grid=(4,),
        in_specs=[spec, spec, spec, spec4, spec4], out_specs=[spec] * 4,
        out_shape=[jax.ShapeDtypeStruct((rows, cols), F32)] * 4,
        compiler_params=_params("arbitrary"),
    )(w, m, v, got, sib)


def _adamw_small(name, w, m, v, g):
    def body(w_ref, m_ref, v_ref, g_ref, d_ref, mo_ref, vo_ref):
        d_ref[...], mo_ref[...], vo_ref[...] = _adamw(w_ref[...], g_ref[...], m_ref[...], v_ref[...])

    return pl.pallas_call(body, name=name, out_shape=[jax.ShapeDtypeStruct(w.shape, F32)] * 3)(w, m, v, g)


def _pack(arrays, rows):
    flat = jnp.concatenate([a.reshape(-1) for a in arrays])
    return jnp.concatenate([flat, jnp.zeros((rows * 128 - flat.shape[0],), F32)]).reshape(rows, 128)


def _unpack(packed, shapes):
    flat, out, at = packed.reshape(-1), [], 0
    for s in shapes:
        size = 1
        for dim in s:
            size *= dim
        out.append(flat[at:at + size].reshape(s))
        at += size
    return out


def _rows_for(shapes):
    total = 0
    for s in shapes:
        size = 1
        for dim in s:
            size *= dim
        total += size
    return -(-total // 1024) * 8


WEIGHTS = ['meta_tokens', 'ffn1_norm', 'ffn1_w_gate', 'ffn1_w_up', 'ffn1_w_down', 'mix_norm', 'w_in', 'rwkv_mu',
           'rwkv_w0', 'rwkv_w_up', 'rwkv_a0', 'rwkv_a_up', 'rwkv_g_up', 'rwkv_k_k', 'rwkv_k_a', 'rwkv_r_k',
           'rwkv_lnx_w', 'rwkv_lnx_b', 'w_out', 'ffn2_norm', 'ffn2_w_gate', 'ffn2_w_up', 'ffn2_w_down', 'final_norm']
COL_CUT = ['ffn1_w_gate', 'ffn1_w_up', 'w_in', 'ffn2_w_gate', 'ffn2_w_up']
ROW_CUT = ['ffn1_w_down', 'w_out', 'ffn2_w_down']
SMALL_CUT = ['meta_tokens', 'rwkv_w_up', 'rwkv_a_up', 'rwkv_g_up']
BIG = COL_CUT + ROW_CUT
REPLICATED = [n for n in WEIGHTS if n not in BIG + SMALL_CUT]


def _join_cols(a):
    return a.transpose(1, 0, 2).reshape(a.shape[1], N_CHIPS * a.shape[2])


def _cut_cols(a):
    return a.reshape(a.shape[0], N_CHIPS, a.shape[1] // N_CHIPS).transpose(1, 0, 2)


def _step(x, loss_target, w, m, v):
    two = lambda a: a.reshape(a.shape[-2], a.shape[-1])

    names = BIG + SMALL_CUT
    shards = [two(w[n]).astype(BF16) for n in BIG] + [two(w[n]) for n in SMALL_CUT]
    gathered = dict(zip(names, _gather_shards("gather_weights", shards)))
    full = {n: (two(w[n]) if w[n].ndim == 3 else w[n]) for n in REPLICATED}
    for n in COL_CUT + SMALL_CUT:
        full[n] = _join_cols(gathered[n])
    for n in ROW_CUT:
        full[n] = gathered[n].reshape(-1, gathered[n].shape[-1])
    full["rwkv_r_k"] = w["rwkv_r_k"]
    full["final_norm"] = w["final_norm"]

    loss, dx, g = _local_step(x[0], loss_target[0], full)
    loss = lax.psum(loss, ("x", "y", "c"))

    parts = [_cut_cols(g[n]).astype(BF16) for n in COL_CUT]
    parts += [g[n].reshape(N_CHIPS, -1, g[n].shape[-1]).astype(BF16) for n in ROW_CUT]
    arrived = _pair_exchange("pair_exchange", parts)
    parts = [_pair_add("pair_add_" + n, p, o) for n, p, o in zip(BIG, parts, arrived)]
    reduced = _reduce_shards("reduce_gradients", parts)
    got, sib = dict(zip(BIG, reduced[:len(BIG)])), dict(zip(BIG, reduced[len(BIG):]))

    small_names = REPLICATED + SMALL_CUT
    small_shapes = [g[n].shape for n in small_names]
    small = _all_reduce_small("reduce_small", _pack([g[n] for n in small_names], _rows_for(small_shapes)))
    g_small = dict(zip(small_names, _unpack(small, small_shapes)))
    chip = 2 * lax.axis_index("x") + lax.axis_index("y")
    for n in SMALL_CUT:
        width = g_small[n].shape[1] // N_CHIPS
        g_small[n] = lax.dynamic_slice_in_dim(g_small[n], chip * width, width, axis=1)

    grad, delta, new_m, new_v = {}, {}, {}, {}
    for n in BIG:
        outs = _adamw_shard("adamw_" + n, two(w[n]), two(m[n]), two(v[n]), got[n], sib[n])
        grad[n], delta[n], new_m[n], new_v[n] = (o.reshape(w[n].shape) for o in outs)
    shapes = [w[n].shape for n in small_names]
    rows = _rows_for(shapes)
    packed = [_pack([t[n] for n in small_names], rows) for t in (w, m, v)]
    g_packed = _pack([g_small[n] for n in small_names], rows)
    outs = [_unpack(o, shapes) for o in _adamw_small("adamw_small", *packed, g_packed)]
    for i, n in enumerate(small_names):
        grad[n] = g_small[n].reshape(w[n].shape)
        delta[n], new_m[n], new_v[n] = outs[0][i], outs[1][i], outs[2][i]
    return loss, dx[None], grad, delta, new_m, new_v


def kernel(x, meta_tokens, ffn1_norm, ffn1_w_gate, ffn1_w_up, ffn1_w_down, mix_norm, w_in, rwkv_mu, rwkv_w0, rwkv_w_up, rwkv_a0, rwkv_a_up, rwkv_g_up, rwkv_k_k, rwkv_k_a, rwkv_r_k, rwkv_lnx_w, rwkv_lnx_b, w_out, ffn2_norm, ffn2_w_gate, ffn2_w_up, ffn2_w_down, final_norm, loss_target, m_meta_tokens, m_ffn1_norm, m_ffn1_w_gate, m_ffn1_w_up, m_ffn1_w_down, m_mix_norm, m_w_in, m_rwkv_mu, m_rwkv_w0, m_rwkv_w_up, m_rwkv_a0, m_rwkv_a_up, m_rwkv_g_up, m_rwkv_k_k, m_rwkv_k_a, m_rwkv_r_k, m_rwkv_lnx_w, m_rwkv_lnx_b, m_w_out, m_ffn2_norm, m_ffn2_w_gate, m_ffn2_w_up, m_ffn2_w_down, m_final_norm, v_meta_tokens, v_ffn1_norm, v_ffn1_w_gate, v_ffn1_w_up, v_ffn1_w_down, v_mix_norm, v_w_in, v_rwkv_mu, v_rwkv_w0, v_rwkv_w_up, v_rwkv_a0, v_rwkv_a_up, v_rwkv_g_up, v_rwkv_k_k, v_rwkv_k_a, v_rwkv_r_k, v_rwkv_lnx_w, v_rwkv_lnx_b, v_w_out, v_ffn2_norm, v_ffn2_w_gate, v_ffn2_w_up, v_ffn2_w_down, v_final_norm):
    w = dict(zip(WEIGHTS, (meta_tokens, ffn1_norm, ffn1_w_gate, ffn1_w_up, ffn1_w_down, mix_norm, w_in, rwkv_mu, rwkv_w0, rwkv_w_up, rwkv_a0, rwkv_a_up, rwkv_g_up, rwkv_k_k, rwkv_k_a, rwkv_r_k, rwkv_lnx_w, rwkv_lnx_b, w_out, ffn2_norm, ffn2_w_gate, ffn2_w_up, ffn2_w_down, final_norm)))
    m = dict(zip(WEIGHTS, (m_meta_tokens, m_ffn1_norm, m_ffn1_w_gate, m_ffn1_w_up, m_ffn1_w_down, m_mix_norm, m_w_in, m_rwkv_mu, m_rwkv_w0, m_rwkv_w_up, m_rwkv_a0, m_rwkv_a_up, m_rwkv_g_up, m_rwkv_k_k, m_rwkv_k_a, m_rwkv_r_k, m_rwkv_lnx_w, m_rwkv_lnx_b, m_w_out, m_ffn2_norm, m_ffn2_w_gate, m_ffn2_w_up, m_ffn2_w_down, m_final_norm)))
    v = dict(zip(WEIGHTS, (v_meta_tokens, v_ffn1_norm, v_ffn1_w_gate, v_ffn1_w_up, v_ffn1_w_down, v_mix_norm, v_w_in, v_rwkv_mu, v_rwkv_w0, v_rwkv_w_up, v_rwkv_a0, v_rwkv_a_up, v_rwkv_g_up, v_rwkv_k_k, v_rwkv_k_a, v_rwkv_r_k, v_rwkv_lnx_w, v_rwkv_lnx_b, v_w_out, v_ffn2_norm, v_ffn2_w_gate, v_ffn2_w_up, v_ffn2_w_down, v_final_norm)))
    loss, grad_x, grad, delta, new_m, new_v = _step(x, loss_target, w, m, v)
    return (loss, grad_x, *[grad[n] for n in WEIGHTS], *[delta[n] for n in WEIGHTS],
            *[new_m[n] for n in WEIGHTS], *[new_v[n] for n in WEIGHTS])
```

```python
import functools

import jax
import jax.numpy as jnp
from jax import lax
from jax.experimental import pallas as pl
from jax.experimental.pallas import tpu as pltpu

F32 = jnp.float32
BF16 = jnp.bfloat16

RMS_EPS = 1e-6
LNX_EPS = 64e-5
N_META = 16
ROW0 = 128
META_PAD = ROW0 - N_META
HEAD = 64
N_HEADS = 8
GROUP = N_HEADS * HEAD
LORA_W, LORA_A, LORA_G = 32, 32, 96
LORA_PAD = 256
RW_COLS = 3 * GROUP + LORA_PAD
IN_COLS_PAD = 3 * GROUP + RW_COLS
ATT_BLOCK = 128
CHUNK = 64
VMEM_LIMIT = 56 * 1024 * 1024

ADAM_LR, ADAM_B1, ADAM_B2, ADAM_EPS, ADAM_WD, ADAM_STEP = 0.001, 0.9, 0.999, 1e-08, 0.01, 10

MESH = pl.DeviceIdType.MESH


def _params(*sem):
    return pltpu.CompilerParams(dimension_semantics=tuple(sem), vmem_limit_bytes=VMEM_LIMIT)


def _dot(a, b):
    return lax.dot_general(a, b, (((1,), (0,)), ((), ())), preferred_element_type=F32)


def _dot_nt(a, b):
    return lax.dot_general(a, b, (((1,), (1,)), ((), ())), preferred_element_type=F32)


def _dot_tn(a, b):
    return lax.dot_general(a, b, (((0,), (0,)), ((), ())), preferred_element_type=F32)


def _split2(x):
    hi = x.astype(BF16)
    return hi, (x - hi.astype(F32)).astype(BF16)


def _sigmoid(x):
    return 1.0 / (1.0 + jnp.exp(-x))


def _rms_fwd(x, g):
    rstd = lax.rsqrt(jnp.mean(x * x, axis=-1, keepdims=True) + RMS_EPS)
    xhat = x * rstd
    return xhat * g, xhat, rstd


def _rms_bwd(dn, xhat, rstd, g):
    dxhat = dn * g
    dx = rstd * (dxhat - xhat * jnp.mean(dxhat * xhat, axis=-1, keepdims=True))
    return dx, jnp.sum(dn * xhat, axis=0, keepdims=True)


def _row_tile(rows):
    return 384 if rows % 384 == 0 else 128


def _half_tile(cols):
    return cols // 2 if cols % 256 == 0 else cols


def _ffn_fwd(name, h, g, wg, wu, wd):
    rows, d = h.shape
    f = wg.shape[1]
    tm, tf = _row_tile(rows), _half_tile(f)
    nj = f // tf

    def body(h_ref, g_ref, wg_ref, wu_ref, wd_ref, ho_ref, a_ref, b_ref, n_sc, acc_sc):
        j = pl.program_id(1)

        @pl.when(j == 0)
        def _():
            n, _, _ = _rms_fwd(h_ref[...], g_ref[...])
            n_sc[...] = n.astype(BF16)
            acc_sc[...] = jnp.zeros_like(acc_sc)

        n = n_sc[...]
        a = _dot(n, wg_ref[...])
        b = _dot(n, wu_ref[...])
        a_ref[...] = a
        b_ref[...] = b
        s = a * _sigmoid(a) * b
        acc_sc[...] += _dot(s.astype(BF16), wd_ref[...])

        @pl.when(j == nj - 1)
        def _():
            ho_ref[...] = h_ref[...] + 0.5 * acc_sc[...]

    return pl.pallas_call(
        body, name=name, grid=(rows // tm, nj),
        in_specs=[pl.BlockSpec((tm, d), lambda i, j: (i, 0)),
                  pl.BlockSpec((1, d), lambda i, j: (0, 0)),
                  pl.BlockSpec((d, tf), lambda i, j: (0, j)),
                  pl.BlockSpec((d, tf), lambda i, j: (0, j)),
                  pl.BlockSpec((tf, d), lambda i, j: (j, 0))],
        out_specs=[pl.BlockSpec((tm, d), lambda i, j: (i, 0)),
                   pl.BlockSpec((tm, tf), lambda i, j: (i, j)),
                   pl.BlockSpec((tm, tf), lambda i, j: (i, j))],
        out_shape=[jax.ShapeDtypeStruct((rows, d), F32),
                   jax.ShapeDtypeStruct((rows, f), F32),
                   jax.ShapeDtypeStruct((rows, f), F32)],
        scratch_shapes=[pltpu.VMEM((tm, d), BF16), pltpu.VMEM((tm, d), F32)],
        compiler_params=_params("arbitrary", "arbitrary"),
    )(h, g, wg, wu, wd)


def _ffn_bwd(name, dh, h, g, a, b, wg, wu, wd):
    rows, d = h.shape
    f = wg.shape[1]
    tm, tf = _row_tile(rows), _half_tile(f)
    ni, nj = rows // tm, f // tf

    def body(dh_ref, h_ref, g_ref, a_ref, b_ref, wg_ref, wu_ref, wd_ref,
             dhi_ref, da_ref, db_ref, s_ref, n_ref, dhh_ref, dg_ref, dn_sc):
        i, j = pl.program_id(0), pl.program_id(1)

        @pl.when(j == 0)
        def _():
            n, _, _ = _rms_fwd(h_ref[...], g_ref[...])
            n_ref[...] = n.astype(BF16)
            dhh_ref[...] = (0.5 * dh_ref[...]).astype(BF16)
            dn_sc[...] = jnp.zeros_like(dn_sc)

        @pl.when((i == 0) & (j == 0))
        def _():
            dg_ref[...] = jnp.zeros_like(dg_ref)

        ds = _dot_nt(dhh_ref[...], wd_ref[...])
        av, bv = a_ref[...], b_ref[...]
        sig = _sigmoid(av)
        silu = av * sig
        s_ref[...] = (silu * bv).astype(BF16)
        db = (ds * silu).astype(BF16)
        da = (ds * bv * (sig * (1.0 + av * (1.0 - sig)))).astype(BF16)
        da_ref[...] = da
        db_ref[...] = db
        dn_sc[...] += _dot_nt(da, wg_ref[...]) + _dot_nt(db, wu_ref[...])

        @pl.when(j == nj - 1)
        def _():
            gv = g_ref[...]
            _, xhat, rstd = _rms_fwd(h_ref[...], gv)
            dx, dg = _rms_bwd(dn_sc[...], xhat, rstd, gv)
            dhi_ref[...] = dh_ref[...] + dx
            dg_ref[...] += dg

    return pl.pallas_call(
        body, name=name, grid=(ni, nj),
        in_specs=[pl.BlockSpec((tm, d), lambda i, j: (i, 0)),
                  pl.BlockSpec((tm, d), lambda i, j: (i, 0)),
                  pl.BlockSpec((1, d), lambda i, j: (0, 0)),
                  pl.BlockSpec((tm, tf), lambda i, j: (i, j)),
                  pl.BlockSpec((tm, tf), lambda i, j: (i, j)),
                  pl.BlockSpec((d, tf), lambda i, j: (0, j)),
                  pl.BlockSpec((d, tf), lambda i, j: (0, j)),
                  pl.BlockSpec((tf, d), lambda i, j: (j, 0))],
        out_specs=[pl.BlockSpec((tm, d), lambda i, j: (i, 0)),
                   pl.BlockSpec((tm, tf), lambda i, j: (i, j)),
                   pl.BlockSpec((tm, tf), lambda i, j: (i, j)),
                   pl.BlockSpec((tm, tf), lambda i, j: (i, j)),
                   pl.BlockSpec((tm, d), lambda i, j: (i, 0)),
                   pl.BlockSpec((tm, d), lambda i, j: (i, 0)),
                   pl.BlockSpec((1, d), lambda i, j: (0, 0))],
        out_shape=[jax.ShapeDtypeStruct((rows, d), F32),
                   jax.ShapeDtypeStruct((rows, f), BF16),
                   jax.ShapeDtypeStruct((rows, f), BF16),
                   jax.ShapeDtypeStruct((rows, f), BF16),
                   jax.ShapeDtypeStruct((rows, d), BF16),
                   jax.ShapeDtypeStruct((rows, d), BF16),
                   jax.ShapeDtypeStruct((1, d), F32)],
        scratch_shapes=[pltpu.VMEM((tm, d), F32)],
        compiler_params=_params("arbitrary", "arbitrary"),
    )(dh, h, g, a, b, wg, wu, wd)


def _mm_tn(name, a, b):
    k, m = a.shape
    n = b.shape[1]
    tk = _row_tile(k)
    tm = _half_tile(m) if m > 1024 else m
    tn = _half_tile(n) if n > 1024 else n

    def body(a_ref, b_ref, o_ref):
        @pl.when(pl.program_id(2) == 0)
        def _():
            o_ref[...] = jnp.zeros_like(o_ref)

        o_ref[...] += _dot_tn(a_ref[...], b_ref[...])

    return pl.pallas_call(
        body, name=name, grid=(m // tm, n // tn, k // tk),
        in_specs=[pl.BlockSpec((tk, tm), lambda i, j, kk: (kk, i)),
                  pl.BlockSpec((tk, tn), lambda i, j, kk: (kk, j))],
        out_specs=pl.BlockSpec((tm, tn), lambda i, j, kk: (i, j)),
        out_shape=jax.ShapeDtypeStruct((m, n), F32),
        compiler_params=_params("arbitrary", "arbitrary", "arbitrary"),
    )(a, b)


def _norm_proj(name, h, g, w):
    rows, d = h.shape
    n = w.shape[1]
    split = 3 * GROUP
    tm = _row_tile(rows)

    def body(h_ref, g_ref, w_ref, qkv_ref, p_ref, n_ref):
        nv, _, _ = _rms_fwd(h_ref[...], g_ref[...])
        nb = nv.astype(BF16)
        n_ref[...] = nb
        qkv_ref[...] = _dot(nb, w_ref[:, :split]).astype(BF16)
        p_ref[...] = _dot(nb, w_ref[:, split:])

    return pl.pallas_call(
        body, name=name, grid=(rows // tm,),
        in_specs=[pl.BlockSpec((tm, d), lambda i: (i, 0)),
                  pl.BlockSpec((1, d), lambda i: (0, 0)),
                  pl.BlockSpec((d, n), lambda i: (0, 0))],
        out_specs=[pl.BlockSpec((tm, split), lambda i: (i, 0)),
                   pl.BlockSpec((tm, n - split), lambda i: (i, 0)),
                   pl.BlockSpec((tm, d), lambda i: (i, 0))],
        out_shape=[jax.ShapeDtypeStruct((rows, split), BF16), jax.ShapeDtypeStruct((rows, n - split), F32),
                   jax.ShapeDtypeStruct((rows, d), BF16)],
        compiler_params=_params("arbitrary"),
    )(h, g, w)


def _out_proj(name, h, sb, rw, w):
    rows, d = h.shape
    gw = sb.shape[1]
    tm = _row_tile(rows)

    def body(h_ref, sb_ref, rw_ref, w_ref, o_ref, mix_ref):
        mix_ref[:, :gw] = sb_ref[...].astype(BF16)
        mix_ref[:, gw:] = rw_ref[...].astype(BF16)
        o_ref[...] = h_ref[...] + _dot(mix_ref[...], w_ref[...])

    return pl.pallas_call(
        body, name=name, grid=(rows // tm,),
        in_specs=[pl.BlockSpec((tm, d), lambda i: (i, 0)),
                  pl.BlockSpec((tm, gw), lambda i: (i, 0)),
                  pl.BlockSpec((tm, gw), lambda i: (i, 0)),
                  pl.BlockSpec((2 * gw, d), lambda i: (0, 0))],
        out_specs=[pl.BlockSpec((tm, d), lambda i: (i, 0)),
                   pl.BlockSpec((tm, 2 * gw), lambda i: (i, 0))],
        out_shape=[jax.ShapeDtypeStruct((rows, d), F32), jax.ShapeDtypeStruct((rows, 2 * gw), BF16)],
        compiler_params=_params("arbitrary"),
    )(h, sb, rw, w)


def _out_proj_bwd(name, dh, w):
    rows, d = dh.shape
    k = w.shape[0]
    tm = _row_tile(rows)

    def body(dh_ref, w_ref, dsb_ref, drw_ref, dhb_ref):
        dhb = dh_ref[...].astype(BF16)
        dhb_ref[...] = dhb
        dsb_ref[...] = _dot_nt(dhb, w_ref[:GROUP, :]).astype(BF16)
        drw_ref[...] = _dot_nt(dhb, w_ref[GROUP:, :])

    return pl.pallas_call(
        body, name=name, grid=(rows // tm,),
        in_specs=[pl.BlockSpec((tm, d), lambda i: (i, 0)),
                  pl.BlockSpec((k, d), lambda i: (0, 0))],
        out_specs=[pl.BlockSpec((tm, GROUP), lambda i: (i, 0)),
                   pl.BlockSpec((tm, GROUP), lambda i: (i, 0)),
                   pl.BlockSpec((tm, d), lambda i: (i, 0))],
        out_shape=[jax.ShapeDtypeStruct((rows, GROUP), BF16), jax.ShapeDtypeStruct((rows, GROUP), F32),
                   jax.ShapeDtypeStruct((rows, d), BF16)],
        compiler_params=_params("arbitrary"),
    )(dh, w)


def _norm_proj_bwd(name, dproj, w, h, g, dh):
    rows, n = dproj.shape
    d = w.shape[0]
    tm = _row_tile(rows)

    def body(dp_ref, w_ref, h_ref, g_ref, dh_ref, o_ref, dg_ref):
        @pl.when(pl.program_id(0) == 0)
        def _():
            dg_ref[...] = jnp.zeros_like(dg_ref)

        dn = _dot_nt(dp_ref[...], w_ref[...])
        gv = g_ref[...]
        _, xhat, rstd = _rms_fwd(h_ref[...], gv)
        dx, dg = _rms_bwd(dn, xhat, rstd, gv)
        o_ref[...] = dh_ref[...] + dx
        dg_ref[...] += dg

    return pl.pallas_call(
        body, name=name, grid=(rows // tm,),
        in_specs=[pl.BlockSpec((tm, n), lambda i: (i, 0)),
                  pl.BlockSpec((d, n), lambda i: (0, 0)),
                  pl.BlockSpec((tm, d), lambda i: (i, 0)),
                  pl.BlockSpec((1, d), lambda i: (0, 0)),
                  pl.BlockSpec((tm, d), lambda i: (i, 0))],
        out_specs=[pl.BlockSpec((tm, d), lambda i: (i, 0)),
                   pl.BlockSpec((1, d), lambda i: (0, 0))],
        out_shape=[jax.ShapeDtypeStruct((rows, d), F32), jax.ShapeDtypeStruct((1, d), F32)],
        compiler_params=_params("arbitrary"),
    )(dproj, w, h, g, dh)


def _loss_head(name, h, g, tgt):
    rows, d = h.shape
    tm = _row_tile(rows)

    def body(h_ref, g_ref, t_ref, loss_ref, dh_ref, dg_ref):
        i = pl.program_id(0)

        @pl.when(i == 0)
        def _():
            loss_ref[...] = jnp.zeros_like(loss_ref)
            dg_ref[...] = jnp.zeros_like(dg_ref)

        gv = g_ref[...]
        y, xhat, rstd = _rms_fwd(h_ref[...], gv)
        row = i * tm + lax.broadcasted_iota(jnp.int32, (tm, 1), 0)
        diff = jnp.where(row >= ROW0, y - t_ref[...], 0.0)
        part = 0.5 * jnp.sum(jnp.sum(diff * diff, axis=-1, keepdims=True), axis=0, keepdims=True) / d
        loss_ref[...] += jnp.broadcast_to(part, loss_ref.shape)
        dx, dg = _rms_bwd(diff / d, xhat, rstd, gv)
        dh_ref[...] = dx
        dg_ref[...] += dg

    return pl.pallas_call(
        body, name=name, grid=(rows // tm,),
        in_specs=[pl.BlockSpec((tm, d), lambda i: (i, 0)),
                  pl.BlockSpec((1, d), lambda i: (0, 0)),
                  pl.BlockSpec((tm, d), lambda i: (i, 0))],
        out_specs=[pl.BlockSpec((8, 128), lambda i: (0, 0)),
                   pl.BlockSpec((tm, d), lambda i: (i, 0)),
                   pl.BlockSpec((1, d), lambda i: (0, 0))],
        out_shape=[jax.ShapeDtypeStruct((8, 128), F32),
                   jax.ShapeDtypeStruct((rows, d), F32),
                   jax.ShapeDtypeStruct((1, d), F32)],
        compiler_params=_params("arbitrary"),
    )(h, g, tgt)


def _sb_block(qb, kb, i, jb, scale):
    bq, bk = qb.shape[0], kb.shape[0]
    z = _dot_nt(qb, kb) * scale
    qpos = i * bq + lax.broadcasted_iota(jnp.int32, (bq, bk), 0)
    kpos = jb * bk + lax.broadcasted_iota(jnp.int32, (bq, bk), 1)
    valid = (kpos < qpos) & (kpos >= META_PAD)
    e = jnp.exp(-jnp.abs(z))
    log_keep = jnp.where(valid, -(jnp.maximum(z, 0.0) + jnp.log(1.0 + e)), 0.0)
    return z, valid, e, log_keep


def _tri2(n, cmp):
    r = lax.broadcasted_iota(jnp.int32, (2 * n, n), 0) % n
    c = lax.broadcasted_iota(jnp.int32, (2 * n, n), 1)
    return cmp(r, c).astype(BF16)


def _dot_split(x, t2):
    hi, lo = _split2(x)
    return _dot(jnp.concatenate([hi, lo], axis=1), t2)


ATT_HEADS = 128 // HEAD
ATT_CUT = -104.0
ATT_TILES = GROUP // 128


def _lanes(hh):
    return slice(hh * HEAD, (hh + 1) * HEAD)


def _sb_fwd(name, qkv):
    rows = qkv.shape[0]
    nh, dh = N_HEADS, HEAD
    bq, bk, hg = _row_tile(rows), ATT_BLOCK, ATT_HEADS
    per = bq // bk
    scale = dh ** -0.5

    def body(q_ref, k_ref, v_ref, o_ref, rt_ref, cnt_ref):
        i = pl.program_id(1)
        after = _tri2(bk, lambda r, c: r > c)
        nkb = (i + 1) * per

        def live(state):
            n, carry = state
            top = jnp.max(carry[0][0])
            for hh in range(1, hg):
                top = jnp.maximum(top, jnp.max(carry[hh][0]))
            return (n < nkb) & (top >= ATT_CUT)

        def step(state):
            n, carry = state
            jb = nkb - 1 - n
            off = pl.multiple_of(jb * bk, bk)
            out = []
            for hh in range(hg):
                rest, acc = carry[hh]
                kb = k_ref[pl.ds(off, bk), _lanes(hh)]
                vb = v_ref[pl.ds(off, bk), _lanes(hh)]
                z, valid, _, log_keep = _sb_block(q_ref[:, _lanes(hh)], kb, i, jb, scale)
                log_rest = rest + _dot_split(log_keep, after)
                attn = jnp.where(valid, jnp.exp(z + log_keep + log_rest), 0.0)
                out.append((rest + jnp.sum(log_keep, axis=-1, keepdims=True), acc + _dot(attn.astype(BF16), vb)))
            return n + 1, tuple(out)

        init = tuple((jnp.zeros((bq, 1), F32), jnp.zeros((bq, dh), F32)) for _ in range(hg))
        n, res = lax.while_loop(live, step, (jnp.int32(0), init))
        for hh in range(hg):
            rt_ref[hh] = res[hh][0]
            o_ref[:, _lanes(hh)] = res[hh][1]
            cnt_ref[hh] = jnp.full((bq, 1), n, F32)

    return pl.pallas_call(
        body, name=name, grid=(nh // hg, rows // bq),
        in_specs=[pl.BlockSpec((bq, 128), lambda h, i: (i, h)),
                  pl.BlockSpec((rows, 128), lambda h, i: (0, ATT_TILES + h)),
                  pl.BlockSpec((rows, 128), lambda h, i: (0, 2 * ATT_TILES + h))],
        out_specs=[pl.BlockSpec((bq, 128), lambda h, i: (i, h)),
                   pl.BlockSpec((hg, bq, 1), lambda h, i: (h, i, 0)),
                   pl.BlockSpec((hg, bq, 1), lambda h, i: (h, i, 0))],
        out_shape=[jax.ShapeDtypeStruct((rows, GROUP), F32), jax.ShapeDtypeStruct((nh, rows, 1), F32),
                   jax.ShapeDtypeStruct((nh, rows, 1), F32)],
        compiler_params=_params("arbitrary", "arbitrary"),
    )(qkv, qkv, qkv)


def _sb_bwd(name, qkv, rt, cnt, do):
    rows = qkv.shape[0]
    nh, dh = N_HEADS, HEAD
    bq, bk, hg = _row_tile(rows), ATT_BLOCK, ATT_HEADS
    per = bq // bk
    scale = dh ** -0.5

    def body(q_ref, k_ref, v_ref, rt_ref, cnt_ref, do_ref, dq_ref, dk_ref, dv_ref):
        i = pl.program_id(1)

        @pl.when(i == 0)
        def _():
            dk_ref[...] = jnp.zeros_like(dk_ref)
            dv_ref[...] = jnp.zeros_like(dv_ref)

        upto = _tri2(bk, lambda r, c: r <= c)
        before = _tri2(bk, lambda r, c: r < c)
        nkb = (i + 1) * per
        first = nkb - jnp.max(cnt_ref[0]).astype(jnp.int32)

        def step(jb, carry):
            off = pl.multiple_of(jb * bk, bk)
            out = []
            for hh in range(hg):
                keep_sum, g_sum, dq = carry[hh]
                qb, dob = q_ref[:, _lanes(hh)], do_ref[:, _lanes(hh)]
                kb = k_ref[pl.ds(off, bk), _lanes(hh)]
                vb = v_ref[pl.ds(off, bk), _lanes(hh)]
                z, valid, e, log_keep = _sb_block(qb, kb, i, jb, scale)
                log_rest = rt_ref[hh] - keep_sum - _dot_split(log_keep, upto)
                attn = jnp.where(valid, jnp.exp(z + log_keep + log_rest), 0.0)
                g = attn * _dot_nt(dob, vb)
                g_before = g_sum + _dot_split(g, before)
                inv = 1.0 / (1.0 + e)
                sig = jnp.where(z >= 0, inv, e * inv)
                dz = (jnp.where(valid, g * (1.0 - sig) - g_before * sig, 0.0) * scale).astype(BF16)
                dk_ref[pl.ds(off, bk), _lanes(hh)] += _dot_tn(dz, qb)
                dv_ref[pl.ds(off, bk), _lanes(hh)] += _dot_tn(attn.astype(BF16), dob)
                out.append((keep_sum + jnp.sum(log_keep, axis=-1, keepdims=True),
                            g_sum + jnp.sum(g, axis=-1, keepdims=True),
                            dq + _dot(dz, kb)))
            return tuple(out)

        zero = jnp.zeros((bq, 1), F32)
        res = lax.fori_loop(first, nkb, step, tuple((zero, zero, jnp.zeros((bq, dh), F32)) for _ in range(hg)))
        for hh in range(hg):
            dq_ref[:, _lanes(hh)] = res[hh][2]

    return pl.pallas_call(
        body, name=name, grid=(nh // hg, rows // bq),
        in_specs=[pl.BlockSpec((bq, 128), lambda h, i: (i, h)),
                  pl.BlockSpec((rows, 128), lambda h, i: (0, ATT_TILES + h)),
                  pl.BlockSpec((rows, 128), lambda h, i: (0, 2 * ATT_TILES + h)),
                  pl.BlockSpec((hg, bq, 1), lambda h, i: (h, i, 0)),
                  pl.BlockSpec((hg, bq, 1), lambda h, i: (h, i, 0)),
                  pl.BlockSpec((bq, 128), lambda h, i: (i, h))],
        out_specs=[pl.BlockSpec((bq, 128), lambda h, i: (i, h)),
                   pl.BlockSpec((rows, 128), lambda h, i: (0, h)),
                   pl.BlockSpec((rows, 128), lambda h, i: (0, h))],
        out_shape=[jax.ShapeDtypeStruct((rows, GROUP), F32)] * 3,
        compiler_params=_params("arbitrary", "arbitrary"),
    )(qkv, qkv, qkv, rt, cnt, do)


def _head_sum(x, ones_bd):
    return _dot_split(x, ones_bd)


def _rwkv_pre(p, p_prev, mu, w0, a0, k_k, k_a, w_up, a_up, g_up, ones_bd):
    xs = p + (p_prev - p) * mu
    r = xs[:, :GROUP]
    k0 = xs[:, GROUP:2 * GROUP]
    v = xs[:, 2 * GROUP:3 * GROUP]
    lo = xs[:, 3 * GROUP:]
    wa = w0 + _dot(jnp.tanh(lo).astype(BF16), w_up.astype(BF16))
    w = -(jnp.maximum(-wa, 0.0) + jnp.log(1.0 + jnp.exp(-jnp.abs(wa)))) - 0.5
    log_decay = -jnp.exp(w)
    alpha = _sigmoid(a0 + _dot(lo.astype(BF16), a_up.astype(BF16)))
    gate = _dot(_sigmoid(lo).astype(BF16), g_up.astype(BF16))
    kk = k0 * k_k
    kk = kk * lax.rsqrt(jnp.maximum(_head_sum(kk * kk, ones_bd), 1e-24))
    k = k0 * (1.0 + (alpha - 1.0) * k_a)
    return r, log_decay, k, v, -kk, kk * alpha, gate


def _rwkv_post(y, r, k, v, gate, lnx_w, lnx_b, r_k, ones_bd):
    mean = _head_sum(y, ones_bd) * (1.0 / HEAD)
    yc = y - mean
    var = _head_sum(yc * yc, ones_bd) * (1.0 / HEAD)
    yn = yc * lax.rsqrt(var + LNX_EPS) * lnx_w + lnx_b
    bonus = _head_sum(r * k * r_k, ones_bd) * v
    return (yn + bonus) * gate


TOKEN_TILE = 128
_PRE_VEC = 5
_PRE_MAT = 3


def _split_heads(o_ref, val):
    for h in range(N_HEADS):
        o_ref[h] = val[:, _lanes(h)]


def _merge_heads(ref):
    return jnp.concatenate([ref[h] for h in range(N_HEADS)], axis=1)


def _head_spec(tm):
    return pl.BlockSpec((N_HEADS, tm, HEAD), lambda i: (0, i, 0))


def _rwkv_pre_fwd(name, p, p_prev, vecs, mats, ones_bd):
    rows = p.shape[0]
    tm = TOKEN_TILE
    row_spec = lambda w: pl.BlockSpec((tm, w), lambda i: (i, 0))
    full = lambda a: pl.BlockSpec(a.shape, lambda i: (0,) * a.ndim)

    def body(p_ref, pp_ref, *refs):
        ins = [r[...] for r in refs[:_PRE_VEC + _PRE_MAT + 1]]
        outs = refs[_PRE_VEC + _PRE_MAT + 1:]
        vals = _rwkv_pre(p_ref[...], pp_ref[...], *ins)
        for o_ref, val in zip(outs[:6], vals[:6]):
            _split_heads(o_ref, val)
        for o_ref, val in zip(outs[6:], (vals[0], vals[2], vals[3], vals[6])):
            o_ref[...] = val

    return pl.pallas_call(
        body, name=name, grid=(rows // tm,),
        in_specs=[row_spec(RW_COLS), row_spec(RW_COLS)] + [full(a) for a in (*vecs, *mats, ones_bd)],
        out_specs=[_head_spec(tm)] * 6 + [row_spec(GROUP)] * 4,
        out_shape=([jax.ShapeDtypeStruct((N_HEADS, rows, HEAD), F32)] * 6
                   + [jax.ShapeDtypeStruct((rows, GROUP), F32)] * 4),
        compiler_params=_params("arbitrary"),
    )(p, p_prev, *vecs, *mats, ones_bd)


def _rwkv_pre_bwd(name, p, p_prev, vecs, mats, ones_bd, cts_scan, ct_gate, cts_b):
    rows = p.shape[0]
    tm = TOKEN_TILE
    n_par = _PRE_VEC + _PRE_MAT
    row_spec = lambda w: pl.BlockSpec((tm, w), lambda i: (i, 0))
    full = lambda a: pl.BlockSpec(a.shape, lambda i: (0,) * a.ndim)

    def body(*refs):
        p_ref, pp_ref = refs[0], refs[1]
        par = [r[...] for r in refs[2:2 + n_par]]
        ones = refs[2 + n_par][...]
        cta = [_merge_heads(r) for r in refs[3 + n_par:9 + n_par]] + [refs[9 + n_par][...]]
        ctb = [r[...] for r in refs[10 + n_par:13 + n_par]]
        outs = refs[13 + n_par:]
        ct = (cta[0] + ctb[0], cta[1], cta[2] + ctb[1], cta[3] + ctb[2], cta[4], cta[5], cta[6])
        _, vjp = jax.vjp(lambda pv, ppv, *pr: _rwkv_pre(pv, ppv, *pr, ones), p_ref[...], pp_ref[...], *par)
        grads = vjp(ct)
        outs[0][...] = grads[0]
        outs[1][...] = grads[1]

        @pl.when(pl.program_id(0) == 0)
        def _():
            for o_ref in outs[2:]:
                o_ref[...] = jnp.zeros_like(o_ref)

        for o_ref, gval in zip(outs[2:], grads[2:]):
            o_ref[...] += gval

    par_arrays = (*vecs, *mats)
    return pl.pallas_call(
        body, name=name, grid=(rows // tm,),
        in_specs=([row_spec(RW_COLS)] * 2 + [full(a) for a in (*par_arrays, ones_bd)]
                  + [_head_spec(tm)] * 6 + [row_spec(GROUP)] * 4),
        out_specs=[row_spec(RW_COLS)] * 2 + [full(a) for a in par_arrays],
        out_shape=([jax.ShapeDtypeStruct((rows, RW_COLS), F32)] * 2
                   + [jax.ShapeDtypeStruct(a.shape, F32) for a in par_arrays]),
        compiler_params=_params("arbitrary"),
    )(p, p_prev, *par_arrays, ones_bd, *cts_scan, ct_gate, *cts_b)


def _rwkv_post_fwd(name, y, r, k, v, gate, vecs, ones_bd):
    rows = r.shape[0]
    tm = TOKEN_TILE
    row_spec = pl.BlockSpec((tm, GROUP), lambda i: (i, 0))
    full = lambda a: pl.BlockSpec(a.shape, lambda i: (0,) * a.ndim)

    def body(y_ref, *refs):
        vals = [r_[...] for r_ in refs[:-1]]
        refs[-1][...] = _rwkv_post(_merge_heads(y_ref), *vals)

    return pl.pallas_call(
        body, name=name, grid=(rows // tm,),
        in_specs=[_head_spec(tm)] + [row_spec] * 4 + [full(a) for a in (*vecs, ones_bd)],
        out_specs=row_spec,
        out_shape=jax.ShapeDtypeStruct((rows, GROUP), F32),
        compiler_params=_params("arbitrary"),
    )(y, r, k, v, gate, *vecs, ones_bd)


def _rwkv_post_bwd(name, y, r, k, v, gate, vecs, ones_bd, dout):
    rows = r.shape[0]
    tm = TOKEN_TILE
    row_spec = pl.BlockSpec((tm, GROUP), lambda i: (i, 0))
    full = lambda a: pl.BlockSpec(a.shape, lambda i: (0,) * a.ndim)

    def body(y_ref, *refs):
        vals = [_merge_heads(y_ref)] + [r_[...] for r_ in refs[:7]]
        ones = refs[7][...]
        dout_v = refs[8][...]
        outs = refs[9:]
        _, vjp = jax.vjp(lambda *a: _rwkv_post(*a, ones), *vals)
        grads = vjp(dout_v)
        _split_heads(outs[0], grads[0])
        for o_ref, gval in zip(outs[1:5], grads[1:5]):
            o_ref[...] = gval

        @pl.when(pl.program_id(0) == 0)
        def _():
            for o_ref in outs[5:]:
                o_ref[...] = jnp.zeros_like(o_ref)

        for o_ref, gval in zip(outs[5:], grads[5:]):
            o_ref[...] += gval

    return pl.pallas_call(
        body, name=name, grid=(rows // tm,),
        in_specs=[_head_spec(tm)] + [row_spec] * 4 + [full(a) for a in (*vecs, ones_bd)] + [row_spec],
        out_specs=[_head_spec(tm)] + [row_spec] * 4 + [full(a) for a in vecs],
        out_shape=([jax.ShapeDtypeStruct((N_HEADS, rows, HEAD), F32)] + [jax.ShapeDtypeStruct((rows, GROUP), F32)] * 4
                   + [jax.ShapeDtypeStruct(a.shape, F32) for a in vecs]),
        compiler_params=_params("arbitrary"),
    )(y, r, k, v, gate, *vecs, ones_bd, dout)


_NN = (((2,), (1,)), ((0,), (0,)))
_NT = (((2,), (2,)), ((0,), (0,)))
_TN = (((1,), (1,)), ((0,), (0,)))


_BWD_FORMS = {"nn": (("nt", False), ("tn", False)),
              "nt": (("nn", False), ("tn", True)),
              "tn": (("nt", True), ("nn", False))}
_DIMS = {"nn": _NN, "nt": _NT, "tn": _TN}


def _bdot(a, b, form):
    return lax.dot_general(a.astype(BF16), b.astype(BF16), _DIMS[form], preferred_element_type=F32)


@functools.partial(jax.custom_vjp, nondiff_argnums=(2,))
def _bmm(a, b, form):
    return _bdot(a, b, form)


def _bmm_fwd(a, b, form):
    return _bdot(a, b, form), (a.astype(BF16), b.astype(BF16))


def _bmm_bwd(form, res, dc):
    a, b = res
    (fa, swap_a), (fb, swap_b) = _BWD_FORMS[form]
    da = _bdot(b, dc, fa) if swap_a else _bdot(dc, b, fa)
    db = _bdot(dc, a, fb) if swap_b else _bdot(a, dc, fb)
    return da, db


_bmm.defvjp(_bmm_fwd, _bmm_bwd)


@jax.custom_vjp
def _cumsum_steps(x):
    return _tri_apply(x, lambda r, c: r >= c)


def _tri_apply(x, cmp):
    nh, c, _ = x.shape
    tri = cmp(lax.broadcasted_iota(jnp.int32, (c, c), 0), lax.broadcasted_iota(jnp.int32, (c, c), 1))
    tri = jnp.broadcast_to(tri.astype(BF16)[None], (nh, c, c))
    hi, lo = _split2(x)
    return (lax.dot_general(tri, hi, _NN, preferred_element_type=F32)
            + lax.dot_general(tri, lo, _NN, preferred_element_type=F32))


_cumsum_steps.defvjp(lambda x: (_cumsum_steps(x), None), lambda _, d: (_tri_apply(d, lambda r, c: r <= c),))


def _chunk(state, r, log_w, k, v, a, b):
    nh, c, _ = r.shape
    row = lax.broadcasted_iota(jnp.int32, (c, c), 0)
    col = lax.broadcasted_iota(jnp.int32, (c, c), 1)
    cum = _cumsum_steps(log_w)
    mid = cum[:, c // 2 - 1:c // 2, :]
    a_t = a * jnp.exp(cum - log_w - mid)
    r_t = r * jnp.exp(cum - mid)
    back = jnp.exp(mid - cum)
    b_t = b * back
    k_t = k * back
    strict, incl = (row > col)[None], (row >= col)[None]
    ar = jnp.concatenate([a_t, r_t], axis=1)
    on_b = _bmm(ar, b_t, "nt")
    on_k = _bmm(ar, k_t, "nt")
    n_mat = jnp.where(strict, on_b[:, :c], 0.0)
    p_mat = jnp.where(incl, on_b[:, c:], 0.0)
    m_mat = jnp.where(strict, on_k[:, :c], 0.0)
    q_mat = jnp.where(incl, on_k[:, c:], 0.0)
    inv, power, span = n_mat, _bmm(n_mat, n_mat, "nn"), 2
    while span < c:
        both = _bmm(jnp.concatenate([power, inv], axis=1), power, "nn")
        inv = inv + power + both[:, c:]
        power = both[:, :c]
        span *= 2
    s_mid = state * jnp.swapaxes(jnp.exp(mid), 1, 2)
    x = _bmm(jnp.concatenate([a_t, m_mat], axis=2), jnp.concatenate([s_mid, v], axis=1), "nn")
    u = x + _bmm(inv, x, "nn")
    y = _bmm(jnp.concatenate([r_t, p_mat, q_mat], axis=2), jnp.concatenate([s_mid, u, v], axis=1), "nn")
    grown = _bmm(jnp.concatenate([b_t, k_t], axis=1), jnp.concatenate([u, v], axis=1), "tn")
    s_new = (s_mid + grown) * jnp.swapaxes(jnp.exp(cum[:, c - 1:c, :] - mid), 1, 2)
    return y, s_new


def _scan_fwd(name, ops):
    nh, rows, dh = ops[0].shape
    nc = rows // CHUNK
    spec = pl.BlockSpec((nh, CHUNK, dh), lambda c: (0, c, 0))

    def body(r_ref, w_ref, k_ref, v_ref, a_ref, b_ref, y_ref, st_ref, state):
        @pl.when(pl.program_id(0) == 0)
        def _():
            state[...] = jnp.zeros_like(state)

        st_ref[0] = state[...]
        y, s_new = _chunk(state[...], r_ref[...], w_ref[...], k_ref[...], v_ref[...], a_ref[...], b_ref[...])
        y_ref[...] = y
        state[...] = s_new

    return pl.pallas_call(
        body, name=name, grid=(nc,),
        in_specs=[spec] * 6,
        out_specs=[spec, pl.BlockSpec((1, nh, dh, dh), lambda c: (c, 0, 0, 0))],
        out_shape=[jax.ShapeDtypeStruct((nh, rows, dh), F32), jax.ShapeDtypeStruct((nc, nh, dh, dh), F32)],
        scratch_shapes=[pltpu.VMEM((nh, dh, dh), F32)],
        compiler_params=_params("arbitrary"),
    )(*ops)


def _scan_bwd(name, ops, states, dy):
    nh, rows, dh = ops[0].shape
    nc = rows // CHUNK
    spec = pl.BlockSpec((nh, CHUNK, dh), lambda c: (0, nc - 1 - c, 0))

    def body(r_ref, w_ref, k_ref, v_ref, a_ref, b_ref, st_ref, dy_ref, *rest):
        outs, dstate = rest[:6], rest[6]

        @pl.when(pl.program_id(0) == 0)
        def _():
            dstate[...] = jnp.zeros_like(dstate)

        _, vjp = jax.vjp(_chunk, st_ref[0], r_ref[...], w_ref[...], k_ref[...], v_ref[...], a_ref[...], b_ref[...])
        grads = vjp((dy_ref[...], dstate[...]))
        dstate[...] = grads[0]
        for o_ref, gval in zip(outs, grads[1:]):
            o_ref[...] = gval

    return pl.pallas_call(
        body, name=name, grid=(nc,),
        in_specs=[spec] * 6 + [pl.BlockSpec((1, nh, dh, dh), lambda c: (nc - 1 - c, 0, 0, 0)), spec],
        out_specs=[spec] * 6,
        out_shape=[jax.ShapeDtypeStruct((nh, rows, dh), F32)] * 6,
        scratch_shapes=[pltpu.VMEM((nh, dh, dh), F32)],
        compiler_params=_params("arbitrary"),
    )(*ops, states, dy)


def _shift_down(x):
    return jnp.concatenate([jnp.zeros((1, x.shape[1]), x.dtype), x[:-1]], axis=0)


def _shift_up(x):
    return jnp.concatenate([x[1:], jnp.zeros((1, x.shape[1]), x.dtype)], axis=0)


def _pad_rows(x, rows):
    return jnp.concatenate([x, jnp.zeros((rows - x.shape[0],) + x.shape[1:], x.dtype)], axis=0)


def _pad_cols(x, cols):
    return jnp.concatenate([x, jnp.zeros(x.shape[:-1] + (cols - x.shape[-1],), x.dtype)], axis=-1)


def _lora_pad(w_up, a_up, g_up):
    z = lambda n: jnp.zeros((n, GROUP), F32)
    return (jnp.concatenate([w_up, z(LORA_PAD - LORA_W)], 0),
            jnp.concatenate([z(LORA_W), a_up, z(LORA_PAD - LORA_W - LORA_A)], 0),
            jnp.concatenate([z(LORA_W + LORA_A), g_up, z(LORA_PAD - LORA_W - LORA_A - LORA_G)], 0))


def _local_step(x, tgt, w):
    d = x.shape[1]
    zeros = jnp.zeros((META_PAD, d), F32)
    h0 = jnp.concatenate([zeros, w["meta_tokens"], x], axis=0)
    tgt_p = jnp.concatenate([jnp.zeros((ROW0, d), F32), tgt], axis=0)
    ones_bd = ((lax.broadcasted_iota(jnp.int32, (2 * GROUP, GROUP), 0) % GROUP) // HEAD
               == lax.broadcasted_iota(jnp.int32, (2 * GROUP, GROUP), 1) // HEAD).astype(BF16)
    w_in = _pad_cols(w["w_in"], IN_COLS_PAD)
    pre_vecs = (_pad_cols(w["rwkv_mu"], RW_COLS), w["rwkv_w0"], w["rwkv_a0"], w["rwkv_k_k"], w["rwkv_k_a"])
    pre_mats = _lora_pad(w["rwkv_w_up"], w["rwkv_a_up"], w["rwkv_g_up"])
    post_vecs = (w["rwkv_lnx_w"], w["rwkv_lnx_b"], w["rwkv_r_k"].reshape(1, GROUP))

    h1, a1, b1 = _ffn_fwd("ffn1_fwd", h0, w["ffn1_norm"], w["ffn1_w_gate"], w["ffn1_w_up"], w["ffn1_w_down"])
    qkv, p, n2 = _norm_proj("in_proj", h1, w["mix_norm"], w_in)
    sb, rest_total, visited = _sb_fwd("sb_fwd", qkv)
    p_prev = _shift_down(p)
    pre = _rwkv_pre_fwd("rwkv_pre_fwd", p, p_prev, pre_vecs, pre_mats, ones_bd)
    scan_ops, token_ops = pre[:6], pre[6:]
    y, states = _scan_fwd("rwkv_scan_fwd", scan_ops)
    rw = _rwkv_post_fwd("rwkv_post_fwd", y, *token_ops, post_vecs, ones_bd)
    h2, mix = _out_proj("out_proj", h1, sb, rw, w["w_out"])
    h3, a2, b2 = _ffn_fwd("ffn2_fwd", h2, w["ffn2_norm"], w["ffn2_w_gate"], w["ffn2_w_up"], w["ffn2_w_down"])
    loss8, dh3, g_final = _loss_head("loss_head", h3, w["final_norm"].reshape(1, d), tgt_p)

    g = {"final_norm": g_final.reshape(d)}
    dh2, da2, db2, s2, n3, dhh3, g["ffn2_norm"] = _ffn_bwd(
        "ffn2_bwd", dh3, h2, w["ffn2_norm"], a2, b2, w["ffn2_w_gate"], w["ffn2_w_up"], w["ffn2_w_down"])
    g["ffn2_w_gate"] = _mm_tn("ffn2_dgate", n3, da2)
    g["ffn2_w_up"] = _mm_tn("ffn2_dup", n3, db2)
    g["ffn2_w_down"] = _mm_tn("ffn2_ddown", s2, dhh3)
    dsb, drw, dh2b = _out_proj_bwd("out_proj_bwd", dh2, w["w_out"])
    g["w_out"] = _mm_tn("out_proj_dw", mix, dh2b)
    dq, dk, dv = _sb_bwd("sb_bwd", qkv, rest_total, visited, dsb)
    post_g = _rwkv_post_bwd("rwkv_post_bwd", y, *token_ops, post_vecs, ones_bd, drw)
    g["rwkv_lnx_w"], g["rwkv_lnx_b"] = post_g[5], post_g[6]
    g["rwkv_r_k"] = post_g[7].reshape(1, N_HEADS, HEAD)
    scan_g = _scan_bwd("rwkv_scan_bwd", scan_ops, states, post_g[0])
    pre_g = _rwkv_pre_bwd("rwkv_pre_bwd", p, p_prev, pre_vecs, pre_mats, ones_bd, scan_g, post_g[4], post_g[1:4])
    g["rwkv_mu"] = pre_g[2][:, :w["rwkv_mu"].shape[1]]
    g["rwkv_w0"], g["rwkv_a0"], g["rwkv_k_k"], g["rwkv_k_a"] = pre_g[3:7]
    g["rwkv_w_up"] = pre_g[7][:LORA_W]
    g["rwkv_a_up"] = pre_g[8][LORA_W:LORA_W + LORA_A]
    g["rwkv_g_up"] = pre_g[9][LORA_W + LORA_A:LORA_W + LORA_A + LORA_G]
    dp = pre_g[0] + _shift_up(pre_g[1])
    live = (jnp.arange(h0.shape[0]) >= META_PAD)[:, None]
    dproj = jnp.where(live, jnp.concatenate([dq, dk, dv, dp], axis=1), 0.0).astype(BF16)
    g["w_in"] = _mm_tn("in_proj_dw", n2, dproj)[:, :w["w_in"].shape[1]]
    dh1, g["mix_norm"] = _norm_proj_bwd("in_proj_bwd", dproj, w_in, h1, w["mix_norm"], dh2)
    dh0, da1, db1, s1, n1, dhh1, g["ffn1_norm"] = _ffn_bwd(
        "ffn1_bwd", dh1, h0, w["ffn1_norm"], a1, b1, w["ffn1_w_gate"], w["ffn1_w_up"], w["ffn1_w_down"])
    g["ffn1_w_gate"] = _mm_tn("ffn1_dgate", n1, da1)
    g["ffn1_w_up"] = _mm_tn("ffn1_dup", n1, db1)
    g["ffn1_w_down"] = _mm_tn("ffn1_ddown", s1, dhh1)
    g["meta_tokens"] = dh0[META_PAD:ROW0]
    return loss8[0, 0], dh0[ROW0:], g


N_CHIPS = 4
N_DEV = 8
HBM = pl.BlockSpec(memory_space=pltpu.HBM)


def _place():
    return lax.axis_index("x"), lax.axis_index("y"), lax.axis_index("c")


def _other_chips(x, y):
    return [(1 - x, y), (x, 1 - y), (1 - x, 1 - y)]


def _gather_shards(name, shards):
    n = len(shards)
    half = [s.shape[0] // 2 for s in shards]

    def body(*refs):
        ins, outs = refs[:n], refs[n:2 * n]
        send, recv, local, d2d_send, d2d_recv = refs[2 * n:]
        x, y, c = _place()
        me = 2 * x + y
        chips = _other_chips(x, y)
        own = [pltpu.make_async_copy(ins[k], outs[k].at[me], local.at[k]) for k in range(n)]
        for cp in own:
            cp.start()

        def rows_of(k, h):
            return pl.ds(pl.multiple_of(h * half[k], 8), half[k])

        def copy(j, k, slot):
            return pltpu.make_async_remote_copy(
                src_ref=ins[k].at[rows_of(k, c)], dst_ref=outs[k].at[slot, rows_of(k, c)],
                send_sem=send.at[j * n + k], recv_sem=recv.at[j * n + k],
                device_id=(chips[j][0], chips[j][1], c), device_id_type=MESH)

        def passed(j, k, h):
            slot = 2 * chips[j][0] + chips[j][1]
            return pltpu.make_async_remote_copy(
                src_ref=outs[k].at[slot, rows_of(k, h)], dst_ref=outs[k].at[slot, rows_of(k, h)],
                send_sem=d2d_send.at[j * n + k], recv_sem=d2d_recv.at[j * n + k],
                device_id=(x, y, 1 - c), device_id_type=MESH)

        sent = [copy(j, k, me) for j in range(3) for k in range(n)]
        for cp in sent:
            cp.start()
        for j in range(3):
            for k in range(n):
                copy(j, k, 2 * chips[j][0] + chips[j][1]).wait_recv()
                passed(j, k, c).start()
        for j in range(3):
            for k in range(n):
                passed(j, k, 1 - c).wait_recv()
        for cp in sent:
            cp.wait_send()
        for j in range(3):
            for k in range(n):
                passed(j, k, c).wait_send()
        for cp in own:
            cp.wait()

    return pl.pallas_call(
        body, name=name,
        in_specs=[HBM] * n, out_specs=[HBM] * n,
        out_shape=[jax.ShapeDtypeStruct((N_CHIPS,) + s.shape, s.dtype) for s in shards],
        scratch_shapes=[pltpu.SemaphoreType.DMA((3 * n,)), pltpu.SemaphoreType.DMA((3 * n,)),
                        pltpu.SemaphoreType.DMA((n,)), pltpu.SemaphoreType.DMA((3 * n,)),
                        pltpu.SemaphoreType.DMA((3 * n,))],
    )(*shards)


def _pair_exchange(name, parts):
    n = len(parts)
    half = [s.shape[1] // 2 for s in parts]

    def body(*refs):
        ins, outs = refs[:n], refs[n:2 * n]
        send, recv = refs[2 * n:]
        x, y, c = _place()

        def copy(k):
            rows = pl.ds(pl.multiple_of((1 - c) * half[k], 8), half[k])
            return pltpu.make_async_remote_copy(
                src_ref=ins[k].at[:, rows], dst_ref=outs[k], send_sem=send.at[k], recv_sem=recv.at[k],
                device_id=(x, y, 1 - c), device_id_type=MESH)

        for k in range(n):
            copy(k).start()
        for k in range(n):
            copy(k).wait_recv()
        for k in range(n):
            copy(k).wait_send()

    return pl.pallas_call(
        body, name=name,
        in_specs=[HBM] * n, out_specs=[HBM] * n,
        out_shape=[jax.ShapeDtypeStruct((s.shape[0], s.shape[1] // 2, s.shape[2]), s.dtype) for s in parts],
        scratch_shapes=[pltpu.SemaphoreType.DMA((n,)), pltpu.SemaphoreType.DMA((n,))],
    )(*parts)


def _pair_add(name, part, other):
    nch, rows, cols = part.shape
    half = rows // 2

    def body(p_ref, o_ref, out_ref):
        c = lax.axis_index("c")
        mine = p_ref[0, pl.ds(pl.multiple_of(c * half, 16), half), :]
        out_ref[0] = (mine.astype(F32) + o_ref[0].astype(F32)).astype(out_ref.dtype)

    return pl.pallas_call(
        body, name=name, grid=(nch,),
        in_specs=[pl.BlockSpec((1, rows, cols), lambda j: (j, 0, 0)),
                  pl.BlockSpec((1, half, cols), lambda j: (j, 0, 0))],
        out_specs=pl.BlockSpec((1, half, cols), lambda j: (j, 0, 0)),
        out_shape=jax.ShapeDtypeStruct((nch, half, cols), part.dtype),
        compiler_params=_params("arbitrary"),
    )(part, other)


def _reduce_shards(name, parts):
    n = len(parts)

    def body(*refs):
        ins, got, sib = refs[:n], refs[n:2 * n], refs[2 * n:3 * n]
        send, recv, local, d2d_send, d2d_recv = refs[3 * n:]
        x, y, c = _place()
        me = 2 * x + y
        chips = _other_chips(x, y)
        own = [pltpu.make_async_copy(ins[k].at[me], got[k].at[me], local.at[k]) for k in range(n)]
        for cp in own:
            cp.start()

        def copy(j, k, shard, slot):
            return pltpu.make_async_remote_copy(
                src_ref=ins[k].at[shard], dst_ref=got[k].at[slot], send_sem=send.at[j * n + k],
                recv_sem=recv.at[j * n + k], device_id=(chips[j][0], chips[j][1], c), device_id_type=MESH)

        sent = [copy(j, k, 2 * chips[j][0] + chips[j][1], me) for j in range(3) for k in range(n)]
        for cp in sent:
            cp.start()

        def swap(k):
            return pltpu.make_async_remote_copy(
                src_ref=got[k], dst_ref=sib[k], send_sem=d2d_send.at[k], recv_sem=d2d_recv.at[k],
                device_id=(x, y, 1 - c), device_id_type=MESH)

        for k in range(n):
            own[k].wait()
            for j in range(3):
                copy(j, k, me, 2 * chips[j][0] + chips[j][1]).wait_recv()
            swap(k).start()
        for k in range(n):
            swap(k).wait_recv()
        for cp in sent:
            cp.wait_send()
        for k in range(n):
            swap(k).wait_send()

    return pl.pallas_call(
        body, name=name,
        in_specs=[HBM] * n, out_specs=[HBM] * (2 * n),
        out_shape=[jax.ShapeDtypeStruct(s.shape, s.dtype) for s in parts] * 2,
        scratch_shapes=[pltpu.SemaphoreType.DMA((3 * n,)), pltpu.SemaphoreType.DMA((3 * n,)),
                        pltpu.SemaphoreType.DMA((n,)), pltpu.SemaphoreType.DMA((n,)), pltpu.SemaphoreType.DMA((n,))],
    )(*parts)


def _all_reduce_small(name, vec):
    rows = vec.shape[0]

    def body(v_ref, o_ref, buf, send, recv):
        x, y, c = _place()
        me = 4 * x + 2 * y + c
        peers = [(x ^ (r >> 2), y ^ ((r >> 1) & 1), c ^ (r & 1)) for r in range(1, N_DEV)]

        def copy(r, slot):
            px, py, pc = peers[r]
            return pltpu.make_async_remote_copy(
                src_ref=v_ref, dst_ref=buf.at[slot], send_sem=send.at[r], recv_sem=recv.at[r],
                device_id=(px, py, pc), device_id_type=MESH)

        sent = [copy(r, me) for r in range(N_DEV - 1)]
        for cp in sent:
            cp.start()
        buf[me] = v_ref[...]
        for r in range(N_DEV - 1):
            px, py, pc = peers[r]
            copy(r, 4 * px + 2 * py + pc).wait_recv()
        total = buf[0]
        for dev in range(1, N_DEV):
            total = total + buf[dev]
        o_ref[...] = total
        for cp in sent:
            cp.wait_send()

    return pl.pallas_call(
        body, name=name,
        in_specs=[pl.BlockSpec(memory_space=pltpu.VMEM)], out_specs=pl.BlockSpec(memory_space=pltpu.VMEM),
        out_shape=jax.ShapeDtypeStruct(vec.shape, F32),
        scratch_shapes=[pltpu.VMEM((N_DEV, rows, 128), F32),
                        pltpu.SemaphoreType.DMA((N_DEV - 1,)), pltpu.SemaphoreType.DMA((N_DEV - 1,))],
        compiler_params=pltpu.CompilerParams(vmem_limit_bytes=VMEM_LIMIT),
    )(vec)


def _adamw(w, g, m, v):
    m = ADAM_B1 * m + (1.0 - ADAM_B1) * g
    v = ADAM_B2 * v + (1.0 - ADAM_B2) * (g * g)
    m_hat = m / (1.0 - ADAM_B1 ** ADAM_STEP)
    v_hat = v / (1.0 - ADAM_B2 ** ADAM_STEP)
    return -ADAM_LR * (m_hat / (jnp.sqrt(v_hat) + ADAM_EPS) + ADAM_WD * w), m, v


def _adamw_shard(name, w, m, v, got, sib):
    rows, cols = w.shape
    tr = rows // 4
    spec = pl.BlockSpec((tr, cols), lambda i: (i, 0))
    spec4 = pl.BlockSpec((N_CHIPS, tr, cols), lambda i: (0, i % 2, 0))

    def body(w_ref, m_ref, v_ref, got_ref, sib_ref, g_ref, d_ref, mo_ref, vo_ref):
        def four(ref):
            return ((ref[0].astype(F32) + ref[1].astype(F32)) + ref[2].astype(F32)) + ref[3].astype(F32)

        g = jnp.where(pl.program_id(0) // 2 == lax.axis_index("c"), four(got_ref), four(sib_ref))
        g_ref[...] = g
        d_ref[...], mo_ref[...], vo_ref[...] = _adamw(w_ref[...], g, m_ref[...], v_ref[...])

    return pl.pallas_call(
        body, name=name, grid=(4,),
        in_specs=[spec, spec, spec, spec4, spec4], out_specs=[spec] * 4,
        out_shape=[jax.ShapeDtypeStruct((rows, cols), F32)] * 4,
        compiler_params=_params("arbitrary"),
    )(w, m, v, got, sib)


def _adamw_small(name, w, m, v, g):
    def body(w_ref, m_ref, v_ref, g_ref, d_ref, mo_ref, vo_ref):
        d_ref[...], mo_ref[...], vo_ref[...] = _adamw(w_ref[...], g_ref[...], m_ref[...], v_ref[...])

    return pl.pallas_call(body, name=name, out_shape=[jax.ShapeDtypeStruct(w.shape, F32)] * 3)(w, m, v, g)


def _cast_bf16(name, arrays):
    n = len(arrays)

    def body(*refs):
        for i_ref, o_ref in zip(refs[:n], refs[n:]):
            o_ref[...] = i_ref[...].astype(BF16)

    return pl.pallas_call(
        body, name=name, out_shape=[jax.ShapeDtypeStruct(a.shape, BF16) for a in arrays],
        compiler_params=pltpu.CompilerParams(vmem_limit_bytes=VMEM_LIMIT),
    )(*arrays)


def _pack(arrays, rows):
    flat = jnp.concatenate([a.reshape(-1) for a in arrays])
    return jnp.concatenate([flat, jnp.zeros((rows * 128 - flat.shape[0],), F32)]).reshape(rows, 128)


def _unpack(packed, shapes):
    flat, out, at = packed.reshape(-1), [], 0
    for s in shapes:
        size = 1
        for dim in s:
            size *= dim
        out.append(flat[at:at + size].reshape(s))
        at += size
    return out


def _rows_for(shapes):
    total = 0
    for s in shapes:
        size = 1
        for dim in s:
            size *= dim
        total += size
    return -(-total // 1024) * 8


WEIGHTS = ['meta_tokens', 'ffn1_norm', 'ffn1_w_gate', 'ffn1_w_up', 'ffn1_w_down', 'mix_norm', 'w_in', 'rwkv_mu',
           'rwkv_w0', 'rwkv_w_up', 'rwkv_a0', 'rwkv_a_up', 'rwkv_g_up', 'rwkv_k_k', 'rwkv_k_a', 'rwkv_r_k',
           'rwkv_lnx_w', 'rwkv_lnx_b', 'w_out', 'ffn2_norm', 'ffn2_w_gate', 'ffn2_w_up', 'ffn2_w_down', 'final_norm']
COL_CUT = ['ffn1_w_gate', 'ffn1_w_up', 'w_in', 'ffn2_w_gate', 'ffn2_w_up']
ROW_CUT = ['ffn1_w_down', 'w_out', 'ffn2_w_down']
SMALL_CUT = ['meta_tokens', 'rwkv_w_up', 'rwkv_a_up', 'rwkv_g_up']
BIG = COL_CUT + ROW_CUT
REPLICATED = [n for n in WEIGHTS if n not in BIG + SMALL_CUT]


def _join_cols(a):
    return a.transpose(1, 0, 2).reshape(a.shape[1], N_CHIPS * a.shape[2])


def _cut_cols(a):
    return a.reshape(a.shape[0], N_CHIPS, a.shape[1] // N_CHIPS).transpose(1, 0, 2)


def _step(x, loss_target, w, m, v):
    two = lambda a: a.reshape(a.shape[-2], a.shape[-1])

    names = BIG + SMALL_CUT
    shards = list(_cast_bf16("cast_weights", [two(w[n]) for n in BIG])) + [two(w[n]) for n in SMALL_CUT]
    gathered = dict(zip(names, _gather_shards("gather_weights", shards)))
    full = {n: (two(w[n]) if w[n].ndim == 3 else w[n]) for n in REPLICATED}
    for n in COL_CUT + SMALL_CUT:
        full[n] = _join_cols(gathered[n])
    for n in ROW_CUT:
        full[n] = gathered[n].reshape(-1, gathered[n].shape[-1])
    full["rwkv_r_k"] = w["rwkv_r_k"]
    full["final_norm"] = w["final_norm"]

    loss, dx, g = _local_step(x[0], loss_target[0], full)
    loss = lax.psum(loss, ("x", "y", "c"))

    parts = [_cut_cols(g[n]).astype(BF16) for n in COL_CUT]
    parts += [g[n].reshape(N_CHIPS, -1, g[n].shape[-1]).astype(BF16) for n in ROW_CUT]
    arrived = _pair_exchange("pair_exchange", parts)
    parts = [_pair_add("pair_add_" + n, p, o) for n, p, o in zip(BIG, parts, arrived)]
    reduced = _reduce_shards("reduce_gradients", parts)
    got, sib = dict(zip(BIG, reduced[:len(BIG)])), dict(zip(BIG, reduced[len(BIG):]))

    small_names = REPLICATED + SMALL_CUT
    small_shapes = [g[n].shape for n in small_names]
    small = _all_reduce_small("reduce_small", _pack([g[n] for n in small_names], _rows_for(small_shapes)))
    g_small = dict(zip(small_names, _unpack(small, small_shapes)))
    chip = 2 * lax.axis_index("x") + lax.axis_index("y")
    for n in SMALL_CUT:
        width = g_small[n].shape[1] // N_CHIPS
        g_small[n] = lax.dynamic_slice_in_dim(g_small[n], chip * width, width, axis=1)

    grad, delta, new_m, new_v = {}, {}, {}, {}
    for n in BIG:
        outs = _adamw_shard("adamw_" + n, two(w[n]), two(m[n]), two(v[n]), got[n], sib[n])
        grad[n], delta[n], new_m[n], new_v[n] = (o.reshape(w[n].shape) for o in outs)
    shapes = [w[n].shape for n in small_names]
    rows = _rows_for(shapes)
    packed = [_pack([t[n] for n in small_names], rows) for t in (w, m, v)]
    g_packed = _pack([g_small[n] for n in small_names], rows)
    outs = [_unpack(o, shapes) for o in _adamw_small("adamw_small", *packed, g_packed)]
    for i, n in enumerate(small_names):
        grad[n] = g_small[n].reshape(w[n].shape)
        delta[n], new_m[n], new_v[n] = outs[0][i], outs[1][i], outs[2][i]
    return loss, dx[None], grad, delta, new_m, new_v


def kernel(x, meta_tokens, ffn1_norm, ffn1_w_gate, ffn1_w_up, ffn1_w_down, mix_norm, w_in, rwkv_mu, rwkv_w0, rwkv_w_up, rwkv_a0, rwkv_a_up, rwkv_g_up, rwkv_k_k, rwkv_k_a, rwkv_r_k, rwkv_lnx_w, rwkv_lnx_b, w_out, ffn2_norm, ffn2_w_gate, ffn2_w_up, ffn2_w_down, final_norm, loss_target, m_meta_tokens, m_ffn1_norm, m_ffn1_w_gate, m_ffn1_w_up, m_ffn1_w_down, m_mix_norm, m_w_in, m_rwkv_mu, m_rwkv_w0, m_rwkv_w_up, m_rwkv_a0, m_rwkv_a_up, m_rwkv_g_up, m_rwkv_k_k, m_rwkv_k_a, m_rwkv_r_k, m_rwkv_lnx_w, m_rwkv_lnx_b, m_w_out, m_ffn2_norm, m_ffn2_w_gate, m_ffn2_w_up, m_ffn2_w_down, m_final_norm, v_meta_tokens, v_ffn1_norm, v_ffn1_w_gate, v_ffn1_w_up, v_ffn1_w_down, v_mix_norm, v_w_in, v_rwkv_mu, v_rwkv_w0, v_rwkv_w_up, v_rwkv_a0, v_rwkv_a_up, v_rwkv_g_up, v_rwkv_k_k, v_rwkv_k_a, v_rwkv_r_k, v_rwkv_lnx_w, v_rwkv_lnx_b, v_w_out, v_ffn2_norm, v_ffn2_w_gate, v_ffn2_w_up, v_ffn2_w_down, v_final_norm):
    w = dict(zip(WEIGHTS, (meta_tokens, ffn1_norm, ffn1_w_gate, ffn1_w_up, ffn1_w_down, mix_norm, w_in, rwkv_mu, rwkv_w0, rwkv_w_up, rwkv_a0, rwkv_a_up, rwkv_g_up, rwkv_k_k, rwkv_k_a, rwkv_r_k, rwkv_lnx_w, rwkv_lnx_b, w_out, ffn2_norm, ffn2_w_gate, ffn2_w_up, ffn2_w_down, final_norm)))
    m = dict(zip(WEIGHTS, (m_meta_tokens, m_ffn1_norm, m_ffn1_w_gate, m_ffn1_w_up, m_ffn1_w_down, m_mix_norm, m_w_in, m_rwkv_mu, m_rwkv_w0, m_rwkv_w_up, m_rwkv_a0, m_rwkv_a_up, m_rwkv_g_up, m_rwkv_k_k, m_rwkv_k_a, m_rwkv_r_k, m_rwkv_lnx_w, m_rwkv_lnx_b, m_w_out, m_ffn2_norm, m_ffn2_w_gate, m_ffn2_w_up, m_ffn2_w_down, m_final_norm)))
    v = dict(zip(WEIGHTS, (v_meta_tokens, v_ffn1_norm, v_ffn1_w_gate, v_ffn1_w_up, v_ffn1_w_down, v_mix_norm, v_w_in, v_rwkv_mu, v_rwkv_w0, v_rwkv_w_up, v_rwkv_a0, v_rwkv_a_up, v_rwkv_g_up, v_rwkv_k_k, v_rwkv_k_a, v_rwkv_r_k, v_rwkv_lnx_w, v_rwkv_lnx_b, v_w_out, v_ffn2_norm, v_ffn2_w_gate, v_ffn2_w_up, v_ffn2_w_down, v_final_norm)))
    loss, grad_x, grad, delta, new_m, new_v = _step(x, loss_target, w, m, v)
    return (loss, grad_x, *[grad[n] for n in WEIGHTS], *[delta[n] for n in WEIGHTS],
            *[new_m[n] for n in WEIGHTS], *[new_v[n] for n in WEIGHTS])
```

```python
import functools

import jax
import jax.numpy as jnp
from jax import lax
from jax.experimental import pallas as pl
from jax.experimental.pallas import tpu as pltpu

F32 = jnp.float32
BF16 = jnp.bfloat16

RMS_EPS = 1e-6
LNX_EPS = 64e-5
N_META = 16
ROW0 = 128
META_PAD = ROW0 - N_META
HEAD = 64
N_HEADS = 8
GROUP = N_HEADS * HEAD
LORA_W, LORA_A, LORA_G = 32, 32, 96
LORA_PAD = 256
RW_COLS = 3 * GROUP + LORA_PAD
IN_COLS_PAD = 3 * GROUP + RW_COLS
ATT_BLOCK = 128
CHUNK = 64
VMEM_LIMIT = 56 * 1024 * 1024

ADAM_LR, ADAM_B1, ADAM_B2, ADAM_EPS, ADAM_WD, ADAM_STEP = 0.001, 0.9, 0.999, 1e-08, 0.01, 10

MESH = pl.DeviceIdType.MESH


def _params(*sem):
    return pltpu.CompilerParams(dimension_semantics=tuple(sem), vmem_limit_bytes=VMEM_LIMIT)


def _dot(a, b):
    return lax.dot_general(a, b, (((1,), (0,)), ((), ())), preferred_element_type=F32)


def _dot_nt(a, b):
    return lax.dot_general(a, b, (((1,), (1,)), ((), ())), preferred_element_type=F32)


def _dot_tn(a, b):
    return lax.dot_general(a, b, (((0,), (0,)), ((), ())), preferred_element_type=F32)


def _split2(x):
    hi = x.astype(BF16)
    return hi, (x - hi.astype(F32)).astype(BF16)


def _sigmoid(x):
    return 1.0 / (1.0 + jnp.exp(-x))


def _rms_fwd(x, g):
    rstd = lax.rsqrt(jnp.mean(x * x, axis=-1, keepdims=True) + RMS_EPS)
    xhat = x * rstd
    return xhat * g, xhat, rstd


def _rms_bwd(dn, xhat, rstd, g):
    dxhat = dn * g
    dx = rstd * (dxhat - xhat * jnp.mean(dxhat * xhat, axis=-1, keepdims=True))
    return dx, jnp.sum(dn * xhat, axis=0, keepdims=True)


def _row_tile(rows):
    return 384 if rows % 384 == 0 else 128


def _half_tile(cols):
    return cols // 2 if cols % 256 == 0 else cols


def _tall_tile(rows, parts):
    return rows // parts if rows % (16 * parts) == 0 else _row_tile(rows)


def _ffn_fwd(name, h, g, wg, wu, wd):
    rows, d = h.shape
    f = wg.shape[1]
    tm, tf = _row_tile(rows), _half_tile(f)
    nj = f // tf

    def body(h_ref, g_ref, wg_ref, wu_ref, wd_ref, ho_ref, a_ref, b_ref, n_sc, acc_sc):
        j = pl.program_id(1)

        @pl.when(j == 0)
        def _():
            n, _, _ = _rms_fwd(h_ref[...], g_ref[...])
            n_sc[...] = n.astype(BF16)
            acc_sc[...] = jnp.zeros_like(acc_sc)

        n = n_sc[...]
        a = _dot(n, wg_ref[...])
        b = _dot(n, wu_ref[...])
        a_ref[...] = a
        b_ref[...] = b
        s = a * _sigmoid(a) * b
        acc_sc[...] += _dot(s.astype(BF16), wd_ref[...])

        @pl.when(j == nj - 1)
        def _():
            ho_ref[...] = h_ref[...] + 0.5 * acc_sc[...]

    return pl.pallas_call(
        body, name=name, grid=(rows // tm, nj),
        in_specs=[pl.BlockSpec((tm, d), lambda i, j: (i, 0)),
                  pl.BlockSpec((1, d), lambda i, j: (0, 0)),
                  pl.BlockSpec((d, tf), lambda i, j: (0, j)),
                  pl.BlockSpec((d, tf), lambda i, j: (0, j)),
                  pl.BlockSpec((tf, d), lambda i, j: (j, 0))],
        out_specs=[pl.BlockSpec((tm, d), lambda i, j: (i, 0)),
                   pl.BlockSpec((tm, tf), lambda i, j: (i, j)),
                   pl.BlockSpec((tm, tf), lambda i, j: (i, j))],
        out_shape=[jax.ShapeDtypeStruct((rows, d), F32),
                   jax.ShapeDtypeStruct((rows, f), F32),
                   jax.ShapeDtypeStruct((rows, f), F32)],
        scratch_shapes=[pltpu.VMEM((tm, d), BF16), pltpu.VMEM((tm, d), F32)],
        compiler_params=_params("arbitrary", "arbitrary"),
    )(h, g, wg, wu, wd)


def _ffn_bwd(name, dh, h, g, a, b, wg, wu, wd):
    rows, d = h.shape
    f = wg.shape[1]
    tm, tf = _row_tile(rows), _half_tile(f)
    ni, nj = rows // tm, f // tf

    def body(dh_ref, h_ref, g_ref, a_ref, b_ref, wg_ref, wu_ref, wd_ref,
             dhi_ref, da_ref, db_ref, s_ref, n_ref, dhh_ref, dg_ref, dn_sc):
        i, j = pl.program_id(0), pl.program_id(1)

        @pl.when(j == 0)
        def _():
            n, _, _ = _rms_fwd(h_ref[...], g_ref[...])
            n_ref[...] = n.astype(BF16)
            dhh_ref[...] = (0.5 * dh_ref[...]).astype(BF16)
            dn_sc[...] = jnp.zeros_like(dn_sc)

        @pl.when((i == 0) & (j == 0))
        def _():
            dg_ref[...] = jnp.zeros_like(dg_ref)

        ds = _dot_nt(dhh_ref[...], wd_ref[...])
        av, bv = a_ref[...], b_ref[...]
        sig = _sigmoid(av)
        silu = av * sig
        s_ref[...] = (silu * bv).astype(BF16)
        db = (ds * silu).astype(BF16)
        da = (ds * bv * (sig * (1.0 + av * (1.0 - sig)))).astype(BF16)
        da_ref[...] = da
        db_ref[...] = db
        dn_sc[...] += _dot_nt(da, wg_ref[...]) + _dot_nt(db, wu_ref[...])

        @pl.when(j == nj - 1)
        def _():
            gv = g_ref[...]
            _, xhat, rstd = _rms_fwd(h_ref[...], gv)
            dx, dg = _rms_bwd(dn_sc[...], xhat, rstd, gv)
            dhi_ref[...] = dh_ref[...] + dx
            dg_ref[...] += dg

    return pl.pallas_call(
        body, name=name, grid=(ni, nj),
        in_specs=[pl.BlockSpec((tm, d), lambda i, j: (i, 0)),
                  pl.BlockSpec((tm, d), lambda i, j: (i, 0)),
                  pl.BlockSpec((1, d), lambda i, j: (0, 0)),
                  pl.BlockSpec((tm, tf), lambda i, j: (i, j)),
                  pl.BlockSpec((tm, tf), lambda i, j: (i, j)),
                  pl.BlockSpec((d, tf), lambda i, j: (0, j)),
                  pl.BlockSpec((d, tf), lambda i, j: (0, j)),
                  pl.BlockSpec((tf, d), lambda i, j: (j, 0))],
        out_specs=[pl.BlockSpec((tm, d), lambda i, j: (i, 0)),
                   pl.BlockSpec((tm, tf), lambda i, j: (i, j)),
                   pl.BlockSpec((tm, tf), lambda i, j: (i, j)),
                   pl.BlockSpec((tm, tf), lambda i, j: (i, j)),
                   pl.BlockSpec((tm, d), lambda i, j: (i, 0)),
                   pl.BlockSpec((tm, d), lambda i, j: (i, 0)),
                   pl.BlockSpec((1, d), lambda i, j: (0, 0))],
        out_shape=[jax.ShapeDtypeStruct((rows, d), F32),
                   jax.ShapeDtypeStruct((rows, f), BF16),
                   jax.ShapeDtypeStruct((rows, f), BF16),
                   jax.ShapeDtypeStruct((rows, f), BF16),
                   jax.ShapeDtypeStruct((rows, d), BF16),
                   jax.ShapeDtypeStruct((rows, d), BF16),
                   jax.ShapeDtypeStruct((1, d), F32)],
        scratch_shapes=[pltpu.VMEM((tm, d), F32)],
        compiler_params=_params("arbitrary", "arbitrary"),
    )(dh, h, g, a, b, wg, wu, wd)


def _mm_tn(name, a, b):
    k, m = a.shape
    n = b.shape[1]
    tk = _tall_tile(k, 3)
    tm = _half_tile(m) if m > 1024 else m
    tn = _half_tile(n) if n > 1024 else n
    nk = k // tk

    def body(a_ref, b_ref, o_ref, acc):
        kk = pl.program_id(2)

        @pl.when(kk == 0)
        def _():
            acc[...] = jnp.zeros_like(acc)

        acc[...] += _dot_tn(a_ref[...], b_ref[...])

        @pl.when(kk == nk - 1)
        def _():
            o_ref[...] = acc[...].astype(BF16)

    return pl.pallas_call(
        body, name=name, grid=(m // tm, n // tn, nk),
        in_specs=[pl.BlockSpec((tk, tm), lambda i, j, kk: (kk, i)),
                  pl.BlockSpec((tk, tn), lambda i, j, kk: (kk, j))],
        out_specs=pl.BlockSpec((tm, tn), lambda i, j, kk: (i, j)),
        out_shape=jax.ShapeDtypeStruct((m, n), BF16),
        scratch_shapes=[pltpu.VMEM((tm, tn), F32)],
        compiler_params=_params("arbitrary", "arbitrary", "arbitrary"),
    )(a, b)


def _norm_proj(name, h, g, w):
    rows, d = h.shape
    n = w.shape[1]
    split = 3 * GROUP
    tm = _row_tile(rows)

    def body(h_ref, g_ref, w_ref, qkv_ref, p_ref, n_ref):
        nv, _, _ = _rms_fwd(h_ref[...], g_ref[...])
        nb = nv.astype(BF16)
        n_ref[...] = nb
        qkv_ref[...] = _dot(nb, w_ref[:, :split]).astype(BF16)
        p_ref[...] = _dot(nb, w_ref[:, split:])

    return pl.pallas_call(
        body, name=name, grid=(rows // tm,),
        in_specs=[pl.BlockSpec((tm, d), lambda i: (i, 0)),
                  pl.BlockSpec((1, d), lambda i: (0, 0)),
                  pl.BlockSpec((d, n), lambda i: (0, 0))],
        out_specs=[pl.BlockSpec((tm, split), lambda i: (i, 0)),
                   pl.BlockSpec((tm, n - split), lambda i: (i, 0)),
                   pl.BlockSpec((tm, d), lambda i: (i, 0))],
        out_shape=[jax.ShapeDtypeStruct((rows, split), BF16), jax.ShapeDtypeStruct((rows, n - split), F32),
                   jax.ShapeDtypeStruct((rows, d), BF16)],
        compiler_params=_params("arbitrary"),
    )(h, g, w)


def _out_proj(name, h, sb, rw, w):
    rows, d = h.shape
    gw = sb.shape[1]
    tm = _row_tile(rows)

    def body(h_ref, sb_ref, rw_ref, w_ref, o_ref, mix_ref):
        mix_ref[:, :gw] = sb_ref[...].astype(BF16)
        mix_ref[:, gw:] = rw_ref[...].astype(BF16)
        o_ref[...] = h_ref[...] + _dot(mix_ref[...], w_ref[...])

    return pl.pallas_call(
        body, name=name, grid=(rows // tm,),
        in_specs=[pl.BlockSpec((tm, d), lambda i: (i, 0)),
                  pl.BlockSpec((tm, gw), lambda i: (i, 0)),
                  pl.BlockSpec((tm, gw), lambda i: (i, 0)),
                  pl.BlockSpec((2 * gw, d), lambda i: (0, 0))],
        out_specs=[pl.BlockSpec((tm, d), lambda i: (i, 0)),
                   pl.BlockSpec((tm, 2 * gw), lambda i: (i, 0))],
        out_shape=[jax.ShapeDtypeStruct((rows, d), F32), jax.ShapeDtypeStruct((rows, 2 * gw), BF16)],
        compiler_params=_params("arbitrary"),
    )(h, sb, rw, w)


def _out_proj_bwd(name, dh, w):
    rows, d = dh.shape
    k = w.shape[0]
    tm = _row_tile(rows)

    def body(dh_ref, w_ref, dsb_ref, drw_ref, dhb_ref):
        dhb = dh_ref[...].astype(BF16)
        dhb_ref[...] = dhb
        dsb_ref[...] = _dot_nt(dhb, w_ref[:GROUP, :]).astype(BF16)
        drw_ref[...] = _dot_nt(dhb, w_ref[GROUP:, :])

    return pl.pallas_call(
        body, name=name, grid=(rows // tm,),
        in_specs=[pl.BlockSpec((tm, d), lambda i: (i, 0)),
                  pl.BlockSpec((k, d), lambda i: (0, 0))],
        out_specs=[pl.BlockSpec((tm, GROUP), lambda i: (i, 0)),
                   pl.BlockSpec((tm, GROUP), lambda i: (i, 0)),
                   pl.BlockSpec((tm, d), lambda i: (i, 0))],
        out_shape=[jax.ShapeDtypeStruct((rows, GROUP), BF16), jax.ShapeDtypeStruct((rows, GROUP), F32),
                   jax.ShapeDtypeStruct((rows, d), BF16)],
        compiler_params=_params("arbitrary"),
    )(dh, w)


def _norm_proj_bwd(name, dproj, w, h, g, dh):
    rows, n = dproj.shape
    d = w.shape[0]
    tm = _row_tile(rows)

    def body(dp_ref, w_ref, h_ref, g_ref, dh_ref, o_ref, dg_ref):
        @pl.when(pl.program_id(0) == 0)
        def _():
            dg_ref[...] = jnp.zeros_like(dg_ref)

        dn = _dot_nt(dp_ref[...], w_ref[...])
        gv = g_ref[...]
        _, xhat, rstd = _rms_fwd(h_ref[...], gv)
        dx, dg = _rms_bwd(dn, xhat, rstd, gv)
        o_ref[...] = dh_ref[...] + dx
        dg_ref[...] += dg

    return pl.pallas_call(
        body, name=name, grid=(rows // tm,),
        in_specs=[pl.BlockSpec((tm, n), lambda i: (i, 0)),
                  pl.BlockSpec((d, n), lambda i: (0, 0)),
                  pl.BlockSpec((tm, d), lambda i: (i, 0)),
                  pl.BlockSpec((1, d), lambda i: (0, 0)),
                  pl.BlockSpec((tm, d), lambda i: (i, 0))],
        out_specs=[pl.BlockSpec((tm, d), lambda i: (i, 0)),
                   pl.BlockSpec((1, d), lambda i: (0, 0))],
        out_shape=[jax.ShapeDtypeStruct((rows, d), F32), jax.ShapeDtypeStruct((1, d), F32)],
        compiler_params=_params("arbitrary"),
    )(dproj, w, h, g, dh)


def _loss_head(name, h, g, tgt):
    rows, d = h.shape
    tm = _row_tile(rows)

    def body(h_ref, g_ref, t_ref, loss_ref, dh_ref, dg_ref):
        i = pl.program_id(0)

        @pl.when(i == 0)
        def _():
            loss_ref[...] = jnp.zeros_like(loss_ref)
            dg_ref[...] = jnp.zeros_like(dg_ref)

        gv = g_ref[...]
        y, xhat, rstd = _rms_fwd(h_ref[...], gv)
        row = i * tm + lax.broadcasted_iota(jnp.int32, (tm, 1), 0)
        diff = jnp.where(row >= ROW0, y - t_ref[...], 0.0)
        part = 0.5 * jnp.sum(jnp.sum(diff * diff, axis=-1, keepdims=True), axis=0, keepdims=True) / d
        loss_ref[...] += jnp.broadcast_to(part, loss_ref.shape)
        dx, dg = _rms_bwd(diff / d, xhat, rstd, gv)
        dh_ref[...] = dx
        dg_ref[...] += dg

    return pl.pallas_call(
        body, name=name, grid=(rows // tm,),
        in_specs=[pl.BlockSpec((tm, d), lambda i: (i, 0)),
                  pl.BlockSpec((1, d), lambda i: (0, 0)),
                  pl.BlockSpec((tm, d), lambda i: (i, 0))],
        out_specs=[pl.BlockSpec((8, 128), lambda i: (0, 0)),
                   pl.BlockSpec((tm, d), lambda i: (i, 0)),
                   pl.BlockSpec((1, d), lambda i: (0, 0))],
        out_shape=[jax.ShapeDtypeStruct((8, 128), F32),
                   jax.ShapeDtypeStruct((rows, d), F32),
                   jax.ShapeDtypeStruct((1, d), F32)],
        compiler_params=_params("arbitrary"),
    )(h, g, tgt)


def _sb_block(qb, kb, i, jb, scale):
    bq, bk = qb.shape[0], kb.shape[0]
    z = _dot_nt(qb, kb) * scale
    qpos = i * bq + lax.broadcasted_iota(jnp.int32, (bq, bk), 0)
    kpos = jb * bk + lax.broadcasted_iota(jnp.int32, (bq, bk), 1)
    valid = (kpos < qpos) & (kpos >= META_PAD)
    e = jnp.exp(-jnp.abs(z))
    log_keep = jnp.where(valid, -(jnp.maximum(z, 0.0) + jnp.log(1.0 + e)), 0.0)
    return z, valid, e, log_keep


def _tri2(n, cmp):
    r = lax.broadcasted_iota(jnp.int32, (2 * n, n), 0) % n
    c = lax.broadcasted_iota(jnp.int32, (2 * n, n), 1)
    return cmp(r, c).astype(BF16)


def _dot_split(x, t2):
    hi, lo = _split2(x)
    return _dot(jnp.concatenate([hi, lo], axis=1), t2)


ATT_HEADS = 128 // HEAD
ATT_CUT = -104.0
ATT_TILES = GROUP // 128


def _lanes(hh):
    return slice(hh * HEAD, (hh + 1) * HEAD)


def _sb_fwd(name, qkv):
    rows = qkv.shape[0]
    nh, dh = N_HEADS, HEAD
    bq, bk, hg = _row_tile(rows), ATT_BLOCK, ATT_HEADS
    per = bq // bk
    scale = dh ** -0.5

    def body(q_ref, k_ref, v_ref, o_ref, rt_ref, cnt_ref):
        i = pl.program_id(1)
        after = _tri2(bk, lambda r, c: r > c)
        nkb = (i + 1) * per

        def live(state):
            n, carry = state
            top = jnp.max(carry[0][0])
            for hh in range(1, hg):
                top = jnp.maximum(top, jnp.max(carry[hh][0]))
            return (n < nkb) & (top >= ATT_CUT)

        def step(state):
            n, carry = state
            jb = nkb - 1 - n
            off = pl.multiple_of(jb * bk, bk)
            out = []
            for hh in range(hg):
                rest, acc = carry[hh]
                kb = k_ref[pl.ds(off, bk), _lanes(hh)]
                vb = v_ref[pl.ds(off, bk), _lanes(hh)]
                z, valid, _, log_keep = _sb_block(q_ref[:, _lanes(hh)], kb, i, jb, scale)
                log_rest = rest + _dot_split(log_keep, after)
                attn = jnp.where(valid, jnp.exp(z + log_keep + log_rest), 0.0)
                out.append((rest + jnp.sum(log_keep, axis=-1, keepdims=True), acc + _dot(attn.astype(BF16), vb)))
            return n + 1, tuple(out)

        init = tuple((jnp.zeros((bq, 1), F32), jnp.zeros((bq, dh), F32)) for _ in range(hg))
        n, res = lax.while_loop(live, step, (jnp.int32(0), init))
        for hh in range(hg):
            rt_ref[hh] = res[hh][0]
            o_ref[:, _lanes(hh)] = res[hh][1]
            cnt_ref[hh] = jnp.full((bq, 1), n, F32)

    return pl.pallas_call(
        body, name=name, grid=(nh // hg, rows // bq),
        in_specs=[pl.BlockSpec((bq, 128), lambda h, i: (i, h)),
                  pl.BlockSpec((rows, 128), lambda h, i: (0, ATT_TILES + h)),
                  pl.BlockSpec((rows, 128), lambda h, i: (0, 2 * ATT_TILES + h))],
        out_specs=[pl.BlockSpec((bq, 128), lambda h, i: (i, h)),
                   pl.BlockSpec((hg, bq, 1), lambda h, i: (h, i, 0)),
                   pl.BlockSpec((hg, bq, 1), lambda h, i: (h, i, 0))],
        out_shape=[jax.ShapeDtypeStruct((rows, GROUP), F32), jax.ShapeDtypeStruct((nh, rows, 1), F32),
                   jax.ShapeDtypeStruct((nh, rows, 1), F32)],
        compiler_params=_params("arbitrary", "arbitrary"),
    )(qkv, qkv, qkv)


def _sb_bwd(name, qkv, rt, cnt, do):
    rows = qkv.shape[0]
    nh, dh = N_HEADS, HEAD
    bq, bk, hg = _row_tile(rows), ATT_BLOCK, ATT_HEADS
    per = bq // bk
    scale = dh ** -0.5

    def body(q_ref, k_ref, v_ref, rt_ref, cnt_ref, do_ref, dq_ref, dk_ref, dv_ref):
        i = pl.program_id(1)

        @pl.when(i == 0)
        def _():
            dk_ref[...] = jnp.zeros_like(dk_ref)
            dv_ref[...] = jnp.zeros_like(dv_ref)

        upto = _tri2(bk, lambda r, c: r <= c)
        before = _tri2(bk, lambda r, c: r < c)
        nkb = (i + 1) * per
        first = nkb - jnp.max(cnt_ref[0]).astype(jnp.int32)

        def step(jb, carry):
            off = pl.multiple_of(jb * bk, bk)
            out = []
            for hh in range(hg):
                keep_sum, g_sum, dq = carry[hh]
                qb, dob = q_ref[:, _lanes(hh)], do_ref[:, _lanes(hh)]
                kb = k_ref[pl.ds(off, bk), _lanes(hh)]
                vb = v_ref[pl.ds(off, bk), _lanes(hh)]
                z, valid, e, log_keep = _sb_block(qb, kb, i, jb, scale)
                log_rest = rt_ref[hh] - keep_sum - _dot_split(log_keep, upto)
                attn = jnp.where(valid, jnp.exp(z + log_keep + log_rest), 0.0)
                g = attn * _dot_nt(dob, vb)
                g_before = g_sum + _dot_split(g, before)
                inv = 1.0 / (1.0 + e)
                sig = jnp.where(z >= 0, inv, e * inv)
                dz = (jnp.where(valid, g * (1.0 - sig) - g_before * sig, 0.0) * scale).astype(BF16)
                dk_ref[pl.ds(off, bk), _lanes(hh)] += _dot_tn(dz, qb)
                dv_ref[pl.ds(off, bk), _lanes(hh)] += _dot_tn(attn.astype(BF16), dob)
                out.append((keep_sum + jnp.sum(log_keep, axis=-1, keepdims=True),
                            g_sum + jnp.sum(g, axis=-1, keepdims=True),
                            dq + _dot(dz, kb)))
            return tuple(out)

        zero = jnp.zeros((bq, 1), F32)
        res = lax.fori_loop(first, nkb, step, tuple((zero, zero, jnp.zeros((bq, dh), F32)) for _ in range(hg)))
        for hh in range(hg):
            dq_ref[:, _lanes(hh)] = res[hh][2]

    return pl.pallas_call(
        body, name=name, grid=(nh // hg, rows // bq),
        in_specs=[pl.BlockSpec((bq, 128), lambda h, i: (i, h)),
                  pl.BlockSpec((rows, 128), lambda h, i: (0, ATT_TILES + h)),
                  pl.BlockSpec((rows, 128), lambda h, i: (0, 2 * ATT_TILES + h)),
                  pl.BlockSpec((hg, bq, 1), lambda h, i: (h, i, 0)),
                  pl.BlockSpec((hg, bq, 1), lambda h, i: (h, i, 0)),
                  pl.BlockSpec((bq, 128), lambda h, i: (i, h))],
        out_specs=[pl.BlockSpec((bq, 128), lambda h, i: (i, h)),
                   pl.BlockSpec((rows, 128), lambda h, i: (0, h)),
                   pl.BlockSpec((rows, 128), lambda h, i: (0, h))],
        out_shape=[jax.ShapeDtypeStruct((rows, GROUP), F32)] * 3,
        compiler_params=_params("arbitrary", "arbitrary"),
    )(qkv, qkv, qkv, rt, cnt, do)


def _head_sum(x, ones_bd):
    return _dot_split(x, ones_bd)


def _rwkv_pre(p, p_prev, mu, w0, a0, k_k, k_a, w_up, a_up, g_up, ones_bd):
    xs = p + (p_prev - p) * mu
    r = xs[:, :GROUP]
    k0 = xs[:, GROUP:2 * GROUP]
    v = xs[:, 2 * GROUP:3 * GROUP]
    lo = xs[:, 3 * GROUP:]
    wa = w0 + _dot(jnp.tanh(lo).astype(BF16), w_up.astype(BF16))
    w = -(jnp.maximum(-wa, 0.0) + jnp.log(1.0 + jnp.exp(-jnp.abs(wa)))) - 0.5
    log_decay = -jnp.exp(w)
    alpha = _sigmoid(a0 + _dot(lo.astype(BF16), a_up.astype(BF16)))
    gate = _dot(_sigmoid(lo).astype(BF16), g_up.astype(BF16))
    kk = k0 * k_k
    kk = kk * lax.rsqrt(jnp.maximum(_head_sum(kk * kk, ones_bd), 1e-24))
    k = k0 * (1.0 + (alpha - 1.0) * k_a)
    return r, log_decay, k, v, -kk, kk * alpha, gate


def _rwkv_post(y, r, k, v, gate, lnx_w, lnx_b, r_k, ones_bd):
    mean = _head_sum(y, ones_bd) * (1.0 / HEAD)
    yc = y - mean
    var = _head_sum(yc * yc, ones_bd) * (1.0 / HEAD)
    yn = yc * lax.rsqrt(var + LNX_EPS) * lnx_w + lnx_b
    bonus = _head_sum(r * k * r_k, ones_bd) * v
    return (yn + bonus) * gate


TOKEN_TILE = 128
_PRE_VEC = 5
_PRE_MAT = 3


def _split_heads(o_ref, val):
    for h in range(N_HEADS):
        o_ref[h] = val[:, _lanes(h)]


def _merge_heads(ref):
    return jnp.concatenate([ref[h] for h in range(N_HEADS)], axis=1)


def _head_spec(tm):
    return pl.BlockSpec((N_HEADS, tm, HEAD), lambda i: (0, i, 0))


def _rwkv_pre_fwd(name, p, p_prev, vecs, mats, ones_bd):
    rows = p.shape[0]
    tm = TOKEN_TILE
    row_spec = lambda w: pl.BlockSpec((tm, w), lambda i: (i, 0))
    full = lambda a: pl.BlockSpec(a.shape, lambda i: (0,) * a.ndim)

    def body(p_ref, pp_ref, *refs):
        ins = [r[...] for r in refs[:_PRE_VEC + _PRE_MAT + 1]]
        outs = refs[_PRE_VEC + _PRE_MAT + 1:]
        vals = _rwkv_pre(p_ref[...], pp_ref[...], *ins)
        for o_ref, val in zip(outs[:6], vals[:6]):
            _split_heads(o_ref, val)
        for o_ref, val in zip(outs[6:], (vals[0], vals[2], vals[3], vals[6])):
            o_ref[...] = val

    return pl.pallas_call(
        body, name=name, grid=(rows // tm,),
        in_specs=[row_spec(RW_COLS), row_spec(RW_COLS)] + [full(a) for a in (*vecs, *mats, ones_bd)],
        out_specs=[_head_spec(tm)] * 6 + [row_spec(GROUP)] * 4,
        out_shape=([jax.ShapeDtypeStruct((N_HEADS, rows, HEAD), F32)] * 6
                   + [jax.ShapeDtypeStruct((rows, GROUP), F32)] * 4),
        compiler_params=_params("arbitrary"),
    )(p, p_prev, *vecs, *mats, ones_bd)


def _rwkv_pre_bwd(name, p, p_prev, vecs, mats, ones_bd, cts_scan, ct_gate, cts_b):
    rows = p.shape[0]
    tm = TOKEN_TILE
    n_par = _PRE_VEC + _PRE_MAT
    row_spec = lambda w: pl.BlockSpec((tm, w), lambda i: (i, 0))
    full = lambda a: pl.BlockSpec(a.shape, lambda i: (0,) * a.ndim)

    def body(*refs):
        p_ref, pp_ref = refs[0], refs[1]
        par = [r[...] for r in refs[2:2 + n_par]]
        ones = refs[2 + n_par][...]
        cta = [_merge_heads(r) for r in refs[3 + n_par:9 + n_par]] + [refs[9 + n_par][...]]
        ctb = [r[...] for r in refs[10 + n_par:13 + n_par]]
        outs = refs[13 + n_par:]
        ct = (cta[0] + ctb[0], cta[1], cta[2] + ctb[1], cta[3] + ctb[2], cta[4], cta[5], cta[6])
        _, vjp = jax.vjp(lambda pv, ppv, *pr: _rwkv_pre(pv, ppv, *pr, ones), p_ref[...], pp_ref[...], *par)
        grads = vjp(ct)
        outs[0][...] = grads[0]
        outs[1][...] = grads[1]

        @pl.when(pl.program_id(0) == 0)
        def _():
            for o_ref in outs[2:]:
                o_ref[...] = jnp.zeros_like(o_ref)

        for o_ref, gval in zip(outs[2:], grads[2:]):
            o_ref[...] += gval

    par_arrays = (*vecs, *mats)
    return pl.pallas_call(
        body, name=name, grid=(rows // tm,),
        in_specs=([row_spec(RW_COLS)] * 2 + [full(a) for a in (*par_arrays, ones_bd)]
                  + [_head_spec(tm)] * 6 + [row_spec(GROUP)] * 4),
        out_specs=[row_spec(RW_COLS)] * 2 + [full(a) for a in par_arrays],
        out_shape=([jax.ShapeDtypeStruct((rows, RW_COLS), F32)] * 2
                   + [jax.ShapeDtypeStruct(a.shape, F32) for a in par_arrays]),
        compiler_params=_params("arbitrary"),
    )(p, p_prev, *par_arrays, ones_bd, *cts_scan, ct_gate, *cts_b)


def _rwkv_post_fwd(name, y, r, k, v, gate, vecs, ones_bd):
    rows = r.shape[0]
    tm = TOKEN_TILE
    row_spec = pl.BlockSpec((tm, GROUP), lambda i: (i, 0))
    full = lambda a: pl.BlockSpec(a.shape, lambda i: (0,) * a.ndim)

    def body(y_ref, *refs):
        vals = [r_[...] for r_ in refs[:-1]]
        refs[-1][...] = _rwkv_post(_merge_heads(y_ref), *vals)

    return pl.pallas_call(
        body, name=name, grid=(rows // tm,),
        in_specs=[_head_spec(tm)] + [row_spec] * 4 + [full(a) for a in (*vecs, ones_bd)],
        out_specs=row_spec,
        out_shape=jax.ShapeDtypeStruct((rows, GROUP), F32),
        compiler_params=_params("arbitrary"),
    )(y, r, k, v, gate, *vecs, ones_bd)


def _rwkv_post_bwd(name, y, r, k, v, gate, vecs, ones_bd, dout):
    rows = r.shape[0]
    tm = TOKEN_TILE
    row_spec = pl.BlockSpec((tm, GROUP), lambda i: (i, 0))
    full = lambda a: pl.BlockSpec(a.shape, lambda i: (0,) * a.ndim)

    def body(y_ref, *refs):
        vals = [_merge_heads(y_ref)] + [r_[...] for r_ in refs[:7]]
        ones = refs[7][...]
        dout_v = refs[8][...]
        outs = refs[9:]
        _, vjp = jax.vjp(lambda *a: _rwkv_post(*a, ones), *vals)
        grads = vjp(dout_v)
        _split_heads(outs[0], grads[0])
        for o_ref, gval in zip(outs[1:5], grads[1:5]):
            o_ref[...] = gval

        @pl.when(pl.program_id(0) == 0)
        def _():
            for o_ref in outs[5:]:
                o_ref[...] = jnp.zeros_like(o_ref)

        for o_ref, gval in zip(outs[5:], grads[5:]):
            o_ref[...] += gval

    return pl.pallas_call(
        body, name=name, grid=(rows // tm,),
        in_specs=[_head_spec(tm)] + [row_spec] * 4 + [full(a) for a in (*vecs, ones_bd)] + [row_spec],
        out_specs=[_head_spec(tm)] + [row_spec] * 4 + [full(a) for a in vecs],
        out_shape=([jax.ShapeDtypeStruct((N_HEADS, rows, HEAD), F32)] + [jax.ShapeDtypeStruct((rows, GROUP), F32)] * 4
                   + [jax.ShapeDtypeStruct(a.shape, F32) for a in vecs]),
        compiler_params=_params("arbitrary"),
    )(y, r, k, v, gate, *vecs, ones_bd, dout)


_NN = (((2,), (1,)), ((0,), (0,)))
_NT = (((2,), (2,)), ((0,), (0,)))
_TN = (((1,), (1,)), ((0,), (0,)))


_BWD_FORMS = {"nn": (("nt", False), ("tn", False)),
              "nt": (("nn", False), ("tn", True)),
              "tn": (("nt", True), ("nn", False))}
_DIMS = {"nn": _NN, "nt": _NT, "tn": _TN}


def _bdot(a, b, form):
    return lax.dot_general(a.astype(BF16), b.astype(BF16), _DIMS[form], preferred_element_type=F32)


@functools.partial(jax.custom_vjp, nondiff_argnums=(2,))
def _bmm(a, b, form):
    return _bdot(a, b, form)


def _bmm_fwd(a, b, form):
    return _bdot(a, b, form), (a.astype(BF16), b.astype(BF16))


def _bmm_bwd(form, res, dc):
    a, b = res
    (fa, swap_a), (fb, swap_b) = _BWD_FORMS[form]
    da = _bdot(b, dc, fa) if swap_a else _bdot(dc, b, fa)
    db = _bdot(dc, a, fb) if swap_b else _bdot(a, dc, fb)
    return da, db


_bmm.defvjp(_bmm_fwd, _bmm_bwd)


@jax.custom_vjp
def _cumsum_steps(x):
    return _tri_apply(x, lambda r, c: r >= c)


def _tri_apply(x, cmp):
    nh, c, _ = x.shape
    tri = cmp(lax.broadcasted_iota(jnp.int32, (c, c), 0), lax.broadcasted_iota(jnp.int32, (c, c), 1))
    tri = jnp.broadcast_to(tri.astype(BF16)[None], (nh, c, c))
    hi, lo = _split2(x)
    return (lax.dot_general(tri, hi, _NN, preferred_element_type=F32)
            + lax.dot_general(tri, lo, _NN, preferred_element_type=F32))


_cumsum_steps.defvjp(lambda x: (_cumsum_steps(x), None), lambda _, d: (_tri_apply(d, lambda r, c: r <= c),))


def _chunk(state, r, log_w, k, v, a, b):
    nh, c, _ = r.shape
    row = lax.broadcasted_iota(jnp.int32, (c, c), 0)
    col = lax.broadcasted_iota(jnp.int32, (c, c), 1)
    cum = _cumsum_steps(log_w)
    mid = cum[:, c // 2 - 1:c // 2, :]
    a_t = a * jnp.exp(cum - log_w - mid)
    r_t = r * jnp.exp(cum - mid)
    back = jnp.exp(mid - cum)
    b_t = b * back
    k_t = k * back
    strict, incl = (row > col)[None], (row >= col)[None]
    ar = jnp.concatenate([a_t, r_t], axis=1)
    on_b = _bmm(ar, b_t, "nt")
    on_k = _bmm(ar, k_t, "nt")
    n_mat = jnp.where(strict, on_b[:, :c], 0.0)
    p_mat = jnp.where(incl, on_b[:, c:], 0.0)
    m_mat = jnp.where(strict, on_k[:, :c], 0.0)
    q_mat = jnp.where(incl, on_k[:, c:], 0.0)
    inv, power, span = n_mat, _bmm(n_mat, n_mat, "nn"), 2
    while span < c:
        both = _bmm(jnp.concatenate([power, inv], axis=1), power, "nn")
        inv = inv + power + both[:, c:]
        power = both[:, :c]
        span *= 2
    s_mid = state * jnp.swapaxes(jnp.exp(mid), 1, 2)
    x = _bmm(jnp.concatenate([a_t, m_mat], axis=2), jnp.concatenate([s_mid, v], axis=1), "nn")
    u = x + _bmm(inv, x, "nn")
    y = _bmm(jnp.concatenate([r_t, p_mat, q_mat], axis=2), jnp.concatenate([s_mid, u, v], axis=1), "nn")
    grown = _bmm(jnp.concatenate([b_t, k_t], axis=1), jnp.concatenate([u, v], axis=1), "tn")
    s_new = (s_mid + grown) * jnp.swapaxes(jnp.exp(cum[:, c - 1:c, :] - mid), 1, 2)
    return y, s_new


def _scan_fwd(name, ops):
    nh, rows, dh = ops[0].shape
    nc = rows // CHUNK
    spec = pl.BlockSpec((nh, CHUNK, dh), lambda c: (0, c, 0))

    def body(r_ref, w_ref, k_ref, v_ref, a_ref, b_ref, y_ref, st_ref, state):
        @pl.when(pl.program_id(0) == 0)
        def _():
            state[...] = jnp.zeros_like(state)

        st_ref[0] = state[...]
        y, s_new = _chunk(state[...], r_ref[...], w_ref[...], k_ref[...], v_ref[...], a_ref[...], b_ref[...])
        y_ref[...] = y
        state[...] = s_new

    return pl.pallas_call(
        body, name=name, grid=(nc,),
        in_specs=[spec] * 6,
        out_specs=[spec, pl.BlockSpec((1, nh, dh, dh), lambda c: (c, 0, 0, 0))],
        out_shape=[jax.ShapeDtypeStruct((nh, rows, dh), F32), jax.ShapeDtypeStruct((nc, nh, dh, dh), F32)],
        scratch_shapes=[pltpu.VMEM((nh, dh, dh), F32)],
        compiler_params=_params("arbitrary"),
    )(*ops)


def _scan_bwd(name, ops, states, dy):
    nh, rows, dh = ops[0].shape
    nc = rows // CHUNK
    spec = pl.BlockSpec((nh, CHUNK, dh), lambda c: (0, nc - 1 - c, 0))

    def body(r_ref, w_ref, k_ref, v_ref, a_ref, b_ref, st_ref, dy_ref, *rest):
        outs, dstate = rest[:6], rest[6]

        @pl.when(pl.program_id(0) == 0)
        def _():
            dstate[...] = jnp.zeros_like(dstate)

        _, vjp = jax.vjp(_chunk, st_ref[0], r_ref[...], w_ref[...], k_ref[...], v_ref[...], a_ref[...], b_ref[...])
        grads = vjp((dy_ref[...], dstate[...]))
        dstate[...] = grads[0]
        for o_ref, gval in zip(outs, grads[1:]):
            o_ref[...] = gval

    return pl.pallas_call(
        body, name=name, grid=(nc,),
        in_specs=[spec] * 6 + [pl.BlockSpec((1, nh, dh, dh), lambda c: (nc - 1 - c, 0, 0, 0)), spec],
        out_specs=[spec] * 6,
        out_shape=[jax.ShapeDtypeStruct((nh, rows, dh), F32)] * 6,
        scratch_shapes=[pltpu.VMEM((nh, dh, dh), F32)],
        compiler_params=_params("arbitrary"),
    )(*ops, states, dy)


def _shift_down(x):
    return jnp.concatenate([jnp.zeros((1, x.shape[1]), x.dtype), x[:-1]], axis=0)


def _shift_up(x):
    return jnp.concatenate([x[1:], jnp.zeros((1, x.shape[1]), x.dtype)], axis=0)


def _pad_rows(x, rows):
    return jnp.concatenate([x, jnp.zeros((rows - x.shape[0],) + x.shape[1:], x.dtype)], axis=0)


def _pad_cols(x, cols):
    return jnp.concatenate([x, jnp.zeros(x.shape[:-1] + (cols - x.shape[-1],), x.dtype)], axis=-1)


def _lora_pad(w_up, a_up, g_up):
    z = lambda n: jnp.zeros((n, GROUP), F32)
    return (jnp.concatenate([w_up, z(LORA_PAD - LORA_W)], 0),
            jnp.concatenate([z(LORA_W), a_up, z(LORA_PAD - LORA_W - LORA_A)], 0),
            jnp.concatenate([z(LORA_W + LORA_A), g_up, z(LORA_PAD - LORA_W - LORA_A - LORA_G)], 0))


def _local_step(x, tgt, w):
    d = x.shape[1]
    zeros = jnp.zeros((META_PAD, d), F32)
    h0 = jnp.concatenate([zeros, w["meta_tokens"], x], axis=0)
    tgt_p = jnp.concatenate([jnp.zeros((ROW0, d), F32), tgt], axis=0)
    ones_bd = ((lax.broadcasted_iota(jnp.int32, (2 * GROUP, GROUP), 0) % GROUP) // HEAD
               == lax.broadcasted_iota(jnp.int32, (2 * GROUP, GROUP), 1) // HEAD).astype(BF16)
    w_in = _pad_cols(w["w_in"], IN_COLS_PAD)
    pre_vecs = (_pad_cols(w["rwkv_mu"], RW_COLS), w["rwkv_w0"], w["rwkv_a0"], w["rwkv_k_k"], w["rwkv_k_a"])
    pre_mats = _lora_pad(w["rwkv_w_up"], w["rwkv_a_up"], w["rwkv_g_up"])
    post_vecs = (w["rwkv_lnx_w"], w["rwkv_lnx_b"], w["rwkv_r_k"].reshape(1, GROUP))

    h1, a1, b1 = _ffn_fwd("ffn1_fwd", h0, w["ffn1_norm"], w["ffn1_w_gate"], w["ffn1_w_up"], w["ffn1_w_down"])
    qkv, p, n2 = _norm_proj("in_proj", h1, w["mix_norm"], w_in)
    sb, rest_total, visited = _sb_fwd("sb_fwd", qkv)
    p_prev = _shift_down(p)
    pre = _rwkv_pre_fwd("rwkv_pre_fwd", p, p_prev, pre_vecs, pre_mats, ones_bd)
    scan_ops, token_ops = pre[:6], pre[6:]
    y, states = _scan_fwd("rwkv_scan_fwd", scan_ops)
    rw = _rwkv_post_fwd("rwkv_post_fwd", y, *token_ops, post_vecs, ones_bd)
    h2, mix = _out_proj("out_proj", h1, sb, rw, w["w_out"])
    h3, a2, b2 = _ffn_fwd("ffn2_fwd", h2, w["ffn2_norm"], w["ffn2_w_gate"], w["ffn2_w_up"], w["ffn2_w_down"])
    loss8, dh3, g_final = _loss_head("loss_head", h3, w["final_norm"].reshape(1, d), tgt_p)

    g = {"final_norm": g_final.reshape(d)}
    dh2, da2, db2, s2, n3, dhh3, g["ffn2_norm"] = _ffn_bwd(
        "ffn2_bwd", dh3, h2, w["ffn2_norm"], a2, b2, w["ffn2_w_gate"], w["ffn2_w_up"], w["ffn2_w_down"])
    g["ffn2_w_gate"] = _mm_tn("ffn2_dgate", n3, da2)
    g["ffn2_w_up"] = _mm_tn("ffn2_dup", n3, db2)
    g["ffn2_w_down"] = _mm_tn("ffn2_ddown", s2, dhh3)
    dsb, drw, dh2b = _out_proj_bwd("out_proj_bwd", dh2, w["w_out"])
    g["w_out"] = _mm_tn("out_proj_dw", mix, dh2b)
    dq, dk, dv = _sb_bwd("sb_bwd", qkv, rest_total, visited, dsb)
    post_g = _rwkv_post_bwd("rwkv_post_bwd", y, *token_ops, post_vecs, ones_bd, drw)
    g["rwkv_lnx_w"], g["rwkv_lnx_b"] = post_g[5], post_g[6]
    g["rwkv_r_k"] = post_g[7].reshape(1, N_HEADS, HEAD)
    scan_g = _scan_bwd("rwkv_scan_bwd", scan_ops, states, post_g[0])
    pre_g = _rwkv_pre_bwd("rwkv_pre_bwd", p, p_prev, pre_vecs, pre_mats, ones_bd, scan_g, post_g[4], post_g[1:4])
    g["rwkv_mu"] = pre_g[2][:, :w["rwkv_mu"].shape[1]]
    g["rwkv_w0"], g["rwkv_a0"], g["rwkv_k_k"], g["rwkv_k_a"] = pre_g[3:7]
    g["rwkv_w_up"] = pre_g[7][:LORA_W]
    g["rwkv_a_up"] = pre_g[8][LORA_W:LORA_W + LORA_A]
    g["rwkv_g_up"] = pre_g[9][LORA_W + LORA_A:LORA_W + LORA_A + LORA_G]
    dp = pre_g[0] + _shift_up(pre_g[1])
    live = (jnp.arange(h0.shape[0]) >= META_PAD)[:, None]
    dproj = jnp.where(live, jnp.concatenate([dq, dk, dv, dp], axis=1), 0.0).astype(BF16)
    g["w_in"] = _mm_tn("in_proj_dw", n2, dproj)[:, :w["w_in"].shape[1]]
    dh1, g["mix_norm"] = _norm_proj_bwd("in_proj_bwd", dproj, w_in, h1, w["mix_norm"], dh2)
    dh0, da1, db1, s1, n1, dhh1, g["ffn1_norm"] = _ffn_bwd(
        "ffn1_bwd", dh1, h0, w["ffn1_norm"], a1, b1, w["ffn1_w_gate"], w["ffn1_w_up"], w["ffn1_w_down"])
    g["ffn1_w_gate"] = _mm_tn("ffn1_dgate", n1, da1)
    g["ffn1_w_up"] = _mm_tn("ffn1_dup", n1, db1)
    g["ffn1_w_down"] = _mm_tn("ffn1_ddown", s1, dhh1)
    g["meta_tokens"] = dh0[META_PAD:ROW0]
    return loss8[0, 0], dh0[ROW0:], g


N_CHIPS = 4
N_DEV = 8
HBM = pl.BlockSpec(memory_space=pltpu.HBM)


def _place():
    return lax.axis_index("x"), lax.axis_index("y"), lax.axis_index("c")


def _other_chips(x, y):
    return [(1 - x, y), (x, 1 - y), (1 - x, 1 - y)]


def _gather_shards(name, shards):
    n = len(shards)
    half = [s.shape[0] // 2 for s in shards]

    def body(*refs):
        ins, outs = refs[:n], refs[n:2 * n]
        send, recv, local, d2d_send, d2d_recv = refs[2 * n:]
        x, y, c = _place()
        me = 2 * x + y
        chips = _other_chips(x, y)
        own = [pltpu.make_async_copy(ins[k], outs[k].at[me], local.at[k]) for k in range(n)]
        for cp in own:
            cp.start()

        def rows_of(k, h):
            return pl.ds(pl.multiple_of(h * half[k], 8), half[k])

        def copy(j, k, slot):
            return pltpu.make_async_remote_copy(
                src_ref=ins[k].at[rows_of(k, c)], dst_ref=outs[k].at[slot, rows_of(k, c)],
                send_sem=send.at[j * n + k], recv_sem=recv.at[j * n + k],
                device_id=(chips[j][0], chips[j][1], c), device_id_type=MESH)

        def passed(j, k, h):
            slot = 2 * chips[j][0] + chips[j][1]
            return pltpu.make_async_remote_copy(
                src_ref=outs[k].at[slot, rows_of(k, h)], dst_ref=outs[k].at[slot, rows_of(k, h)],
                send_sem=d2d_send.at[j * n + k], recv_sem=d2d_recv.at[j * n + k],
                device_id=(x, y, 1 - c), device_id_type=MESH)

        sent = [copy(j, k, me) for j in range(3) for k in range(n)]
        for cp in sent:
            cp.start()
        for j in range(3):
            for k in range(n):
                copy(j, k, 2 * chips[j][0] + chips[j][1]).wait_recv()
                passed(j, k, c).start()
        for j in range(3):
            for k in range(n):
                passed(j, k, 1 - c).wait_recv()
        for cp in sent:
            cp.wait_send()
        for j in range(3):
            for k in range(n):
                passed(j, k, c).wait_send()
        for cp in own:
            cp.wait()

    return pl.pallas_call(
        body, name=name,
        in_specs=[HBM] * n, out_specs=[HBM] * n,
        out_shape=[jax.ShapeDtypeStruct((N_CHIPS,) + s.shape, s.dtype) for s in shards],
        scratch_shapes=[pltpu.SemaphoreType.DMA((3 * n,)), pltpu.SemaphoreType.DMA((3 * n,)),
                        pltpu.SemaphoreType.DMA((n,)), pltpu.SemaphoreType.DMA((3 * n,)),
                        pltpu.SemaphoreType.DMA((3 * n,))],
    )(*shards)


def _pair_exchange(name, parts):
    n = len(parts)
    half = [s.shape[1] // 2 for s in parts]

    def body(*refs):
        ins, outs = refs[:n], refs[n:2 * n]
        send, recv = refs[2 * n:]
        x, y, c = _place()

        def copy(k):
            rows = pl.ds(pl.multiple_of((1 - c) * half[k], 8), half[k])
            return pltpu.make_async_remote_copy(
                src_ref=ins[k].at[:, rows], dst_ref=outs[k], send_sem=send.at[k], recv_sem=recv.at[k],
                device_id=(x, y, 1 - c), device_id_type=MESH)

        for k in range(n):
            copy(k).start()
        for k in range(n):
            copy(k).wait_recv()
        for k in range(n):
            copy(k).wait_send()

    return pl.pallas_call(
        body, name=name,
        in_specs=[HBM] * n, out_specs=[HBM] * n,
        out_shape=[jax.ShapeDtypeStruct((s.shape[0], s.shape[1] // 2, s.shape[2]), s.dtype) for s in parts],
        scratch_shapes=[pltpu.SemaphoreType.DMA((n,)), pltpu.SemaphoreType.DMA((n,))],
    )(*parts)


def _pair_add(name, part, other):
    nch, rows, cols = part.shape
    half = rows // 2

    def body(p_ref, o_ref, out_ref):
        c = lax.axis_index("c")
        mine = p_ref[0, pl.ds(pl.multiple_of(c * half, 16), half), :]
        out_ref[0] = (mine.astype(F32) + o_ref[0].astype(F32)).astype(out_ref.dtype)

    return pl.pallas_call(
        body, name=name, grid=(nch,),
        in_specs=[pl.BlockSpec((1, rows, cols), lambda j: (j, 0, 0)),
                  pl.BlockSpec((1, half, cols), lambda j: (j, 0, 0))],
        out_specs=pl.BlockSpec((1, half, cols), lambda j: (j, 0, 0)),
        out_shape=jax.ShapeDtypeStruct((nch, half, cols), part.dtype),
        compiler_params=_params("arbitrary"),
    )(part, other)


def _reduce_shards(name, parts):
    n = len(parts)

    def body(*refs):
        ins, got, sib = refs[:n], refs[n:2 * n], refs[2 * n:3 * n]
        send, recv, local, d2d_send, d2d_recv = refs[3 * n:]
        x, y, c = _place()
        me = 2 * x + y
        chips = _other_chips(x, y)
        own = [pltpu.make_async_copy(ins[k].at[me], got[k].at[me], local.at[k]) for k in range(n)]
        for cp in own:
            cp.start()

        def copy(j, k, shard, slot):
            return pltpu.make_async_remote_copy(
                src_ref=ins[k].at[shard], dst_ref=got[k].at[slot], send_sem=send.at[j * n + k],
                recv_sem=recv.at[j * n + k], device_id=(chips[j][0], chips[j][1], c), device_id_type=MESH)

        sent = [copy(j, k, 2 * chips[j][0] + chips[j][1], me) for j in range(3) for k in range(n)]
        for cp in sent:
            cp.start()

        def swap(k):
            return pltpu.make_async_remote_copy(
                src_ref=got[k], dst_ref=sib[k], send_sem=d2d_send.at[k], recv_sem=d2d_recv.at[k],
                device_id=(x, y, 1 - c), device_id_type=MESH)

        for k in range(n):
            own[k].wait()
            for j in range(3):
                copy(j, k, me, 2 * chips[j][0] + chips[j][1]).wait_recv()
            swap(k).start()
        for k in range(n):
            swap(k).wait_recv()
        for cp in sent:
            cp.wait_send()
        for k in range(n):
            swap(k).wait_send()

    return pl.pallas_call(
        body, name=name,
        in_specs=[HBM] * n, out_specs=[HBM] * (2 * n),
        out_shape=[jax.ShapeDtypeStruct(s.shape, s.dtype) for s in parts] * 2,
        scratch_shapes=[pltpu.SemaphoreType.DMA((3 * n,)), pltpu.SemaphoreType.DMA((3 * n,)),
                        pltpu.SemaphoreType.DMA((n,)), pltpu.SemaphoreType.DMA((n,)), pltpu.SemaphoreType.DMA((n,))],
    )(*parts)


def _all_reduce_small(name, vec):
    rows = vec.shape[0]

    def body(v_ref, o_ref, buf, send, recv):
        x, y, c = _place()
        me = 4 * x + 2 * y + c
        peers = [(x ^ (r >> 2), y ^ ((r >> 1) & 1), c ^ (r & 1)) for r in range(1, N_DEV)]

        def copy(r, slot):
            px, py, pc = peers[r]
            return pltpu.make_async_remote_copy(
                src_ref=v_ref, dst_ref=buf.at[slot], send_sem=send.at[r], recv_sem=recv.at[r],
                device_id=(px, py, pc), device_id_type=MESH)

        sent = [copy(r, me) for r in range(N_DEV - 1)]
        for cp in sent:
            cp.start()
        buf[me] = v_ref[...]
        for r in range(N_DEV - 1):
            px, py, pc = peers[r]
            copy(r, 4 * px + 2 * py + pc).wait_recv()
        total = buf[0]
        for dev in range(1, N_DEV):
            total = total + buf[dev]
        o_ref[...] = total
        for cp in sent:
            cp.wait_send()

    return pl.pallas_call(
        body, name=name,
        in_specs=[pl.BlockSpec(memory_space=pltpu.VMEM)], out_specs=pl.BlockSpec(memory_space=pltpu.VMEM),
        out_shape=jax.ShapeDtypeStruct(vec.shape, F32),
        scratch_shapes=[pltpu.VMEM((N_DEV, rows, 128), F32),
                        pltpu.SemaphoreType.DMA((N_DEV - 1,)), pltpu.SemaphoreType.DMA((N_DEV - 1,))],
        compiler_params=pltpu.CompilerParams(vmem_limit_bytes=VMEM_LIMIT),
    )(vec)


def _adamw(w, g, m, v):
    m = ADAM_B1 * m + (1.0 - ADAM_B1) * g
    v = ADAM_B2 * v + (1.0 - ADAM_B2) * (g * g)
    m_hat = m / (1.0 - ADAM_B1 ** ADAM_STEP)
    v_hat = v / (1.0 - ADAM_B2 ** ADAM_STEP)
    return -ADAM_LR * (m_hat / (jnp.sqrt(v_hat) + ADAM_EPS) + ADAM_WD * w), m, v


def _adamw_shard(name, core, w, m, v, got, sib):
    rows, cols = w.shape
    tr = rows // 4
    spec = pl.BlockSpec((tr, cols), lambda i, c_ref: (i, 0))
    spec4 = pl.BlockSpec((N_CHIPS, tr, cols), lambda i, c_ref: (0, i % 2, 0))

    def body(c_ref, w_ref, m_ref, v_ref, got_ref, sib_ref, g_ref, d_ref, mo_ref, vo_ref):
        def four(ref):
            return ((ref[0].astype(F32) + ref[1].astype(F32)) + ref[2].astype(F32)) + ref[3].astype(F32)

        g = jnp.where(pl.program_id(0) // 2 == c_ref[0], four(got_ref), four(sib_ref))
        g_ref[...] = g
        d_ref[...], mo_ref[...], vo_ref[...] = _adamw(w_ref[...], g, m_ref[...], v_ref[...])

    return pl.pallas_call(
        body, name=name,
        grid_spec=pltpu.PrefetchScalarGridSpec(
            num_scalar_prefetch=1, grid=(4,),
            in_specs=[spec, spec, spec, spec4, spec4], out_specs=[spec] * 4),
        out_shape=[jax.ShapeDtypeStruct((rows, cols), F32)] * 4,
        compiler_params=_params("arbitrary"),
    )(core, w, m, v, got, sib)


def _adamw_small(name, w, m, v, g):
    def body(w_ref, m_ref, v_ref, g_ref, d_ref, mo_ref, vo_ref):
        d_ref[...], mo_ref[...], vo_ref[...] = _adamw(w_ref[...], g_ref[...], m_ref[...], v_ref[...])

    return pl.pallas_call(body, name=name, out_shape=[jax.ShapeDtypeStruct(w.shape, F32)] * 3)(w, m, v, g)


def _cast_bf16(name, arrays):
    n = len(arrays)

    def body(*refs):
        for i_ref, o_ref in zip(refs[:n], refs[n:]):
            o_ref[...] = i_ref[...].astype(BF16)

    return pl.pallas_call(
        body, name=name, out_shape=[jax.ShapeDtypeStruct(a.shape, BF16) for a in arrays],
        compiler_params=pltpu.CompilerParams(vmem_limit_bytes=VMEM_LIMIT),
    )(*arrays)


def _pack(arrays, rows):
    flat = jnp.concatenate([a.reshape(-1) for a in arrays])
    return jnp.concatenate([flat, jnp.zeros((rows * 128 - flat.shape[0],), F32)]).reshape(rows, 128)


def _unpack(packed, shapes):
    flat, out, at = packed.reshape(-1), [], 0
    for s in shapes:
        size = 1
        for dim in s:
            size *= dim
        out.append(flat[at:at + size].reshape(s))
        at += size
    return out


def _rows_for(shapes):
    total = 0
    for s in shapes:
        size = 1
        for dim in s:
            size *= dim
        total += size
    return -(-total // 1024) * 8


WEIGHTS = ['meta_tokens', 'ffn1_norm', 'ffn1_w_gate', 'ffn1_w_up', 'ffn1_w_down', 'mix_norm', 'w_in', 'rwkv_mu',
           'rwkv_w0', 'rwkv_w_up', 'rwkv_a0', 'rwkv_a_up', 'rwkv_g_up', 'rwkv_k_k', 'rwkv_k_a', 'rwkv_r_k',
           'rwkv_lnx_w', 'rwkv_lnx_b', 'w_out', 'ffn2_norm', 'ffn2_w_gate', 'ffn2_w_up', 'ffn2_w_down', 'final_norm']
COL_CUT = ['ffn1_w_gate', 'ffn1_w_up', 'w_in', 'ffn2_w_gate', 'ffn2_w_up']
ROW_CUT = ['ffn1_w_down', 'w_out', 'ffn2_w_down']
SMALL_CUT = ['meta_tokens', 'rwkv_w_up', 'rwkv_a_up', 'rwkv_g_up']
BIG = COL_CUT + ROW_CUT
REPLICATED = [n for n in WEIGHTS if n not in BIG + SMALL_CUT]


def _join_cols(a):
    return a.transpose(1, 0, 2).reshape(a.shape[1], N_CHIPS * a.shape[2])


def _cut_cols(a):
    return a.reshape(a.shape[0], N_CHIPS, a.shape[1] // N_CHIPS).transpose(1, 0, 2)


def _step(x, loss_target, w, m, v):
    two = lambda a: a.reshape(a.shape[-2], a.shape[-1])

    names = BIG + SMALL_CUT
    shards = list(_cast_bf16("cast_weights", [two(w[n]) for n in BIG])) + [two(w[n]) for n in SMALL_CUT]
    gathered = dict(zip(names, _gather_shards("gather_weights", shards)))
    full = {n: (two(w[n]) if w[n].ndim == 3 else w[n]) for n in REPLICATED}
    for n in COL_CUT + SMALL_CUT:
        full[n] = _join_cols(gathered[n])
    for n in ROW_CUT:
        full[n] = gathered[n].reshape(-1, gathered[n].shape[-1])
    full["rwkv_r_k"] = w["rwkv_r_k"]
    full["final_norm"] = w["final_norm"]

    loss, dx, g = _local_step(x[0], loss_target[0], full)
    loss = lax.psum(loss, ("x", "y", "c"))

    parts = [_cut_cols(g[n]) for n in COL_CUT]
    parts += [g[n].reshape(N_CHIPS, -1, g[n].shape[-1]) for n in ROW_CUT]
    arrived = _pair_exchange("pair_exchange", parts)
    parts = [_pair_add("pair_add_" + n, p, o) for n, p, o in zip(BIG, parts, arrived)]
    reduced = _reduce_shards("reduce_gradients", parts)
    got, sib = dict(zip(BIG, reduced[:len(BIG)])), dict(zip(BIG, reduced[len(BIG):]))

    small_names = REPLICATED + SMALL_CUT
    small_shapes = [g[n].shape for n in small_names]
    small = _all_reduce_small("reduce_small", _pack([g[n] for n in small_names], _rows_for(small_shapes)))
    g_small = dict(zip(small_names, _unpack(small, small_shapes)))
    chip = 2 * lax.axis_index("x") + lax.axis_index("y")
    for n in SMALL_CUT:
        width = g_small[n].shape[1] // N_CHIPS
        g_small[n] = lax.dynamic_slice_in_dim(g_small[n], chip * width, width, axis=1)

    grad, delta, new_m, new_v = {}, {}, {}, {}
    core = lax.axis_index("c").astype(jnp.int32).reshape(1)
    for n in BIG:
        outs = _adamw_shard("adamw_" + n, core, two(w[n]), two(m[n]), two(v[n]), got[n], sib[n])
        grad[n], delta[n], new_m[n], new_v[n] = (o.reshape(w[n].shape) for o in outs)
    shapes = [w[n].shape for n in small_names]
    rows = _rows_for(shapes)
    packed = [_pack([t[n] for n in small_names], rows) for t in (w, m, v)]
    g_packed = _pack([g_small[n] for n in small_names], rows)
    outs = [_unpack(o, shapes) for o in _adamw_small("adamw_small", *packed, g_packed)]
    for i, n in enumerate(small_names):
        grad[n] = g_small[n].reshape(w[n].shape)
        delta[n], new_m[n], new_v[n] = outs[0][i], outs[1][i], outs[2][i]
    return loss, dx[None], grad, delta, new_m, new_v


def kernel(x, meta_tokens, ffn1_norm, ffn1_w_gate, ffn1_w_up, ffn1_w_down, mix_norm, w_in, rwkv_mu, rwkv_w0, rwkv_w_up, rwkv_a0, rwkv_a_up, rwkv_g_up, rwkv_k_k, rwkv_k_a, rwkv_r_k, rwkv_lnx_w, rwkv_lnx_b, w_out, ffn2_norm, ffn2_w_gate, ffn2_w_up, ffn2_w_down, final_norm, loss_target, m_meta_tokens, m_ffn1_norm, m_ffn1_w_gate, m_ffn1_w_up, m_ffn1_w_down, m_mix_norm, m_w_in, m_rwkv_mu, m_rwkv_w0, m_rwkv_w_up, m_rwkv_a0, m_rwkv_a_up, m_rwkv_g_up, m_rwkv_k_k, m_rwkv_k_a, m_rwkv_r_k, m_rwkv_lnx_w, m_rwkv_lnx_b, m_w_out, m_ffn2_norm, m_ffn2_w_gate, m_ffn2_w_up, m_ffn2_w_down, m_final_norm, v_meta_tokens, v_ffn1_norm, v_ffn1_w_gate, v_ffn1_w_up, v_ffn1_w_down, v_mix_norm, v_w_in, v_rwkv_mu, v_rwkv_w0, v_rwkv_w_up, v_rwkv_a0, v_rwkv_a_up, v_rwkv_g_up, v_rwkv_k_k, v_rwkv_k_a, v_rwkv_r_k, v_rwkv_lnx_w, v_rwkv_lnx_b, v_w_out, v_ffn2_norm, v_ffn2_w_gate, v_ffn2_w_up, v_ffn2_w_down, v_final_norm):
    w = dict(zip(WEIGHTS, (meta_tokens, ffn1_norm, ffn1_w_gate, ffn1_w_up, ffn1_w_down, mix_norm, w_in, rwkv_mu, rwkv_w0, rwkv_w_up, rwkv_a0, rwkv_a_up, rwkv_g_up, rwkv_k_k, rwkv_k_a, rwkv_r_k, rwkv_lnx_w, rwkv_lnx_b, w_out, ffn2_norm, ffn2_w_gate, ffn2_w_up, ffn2_w_down, final_norm)))
    m = dict(zip(WEIGHTS, (m_meta_tokens, m_ffn1_norm, m_ffn1_w_gate, m_ffn1_w_up, m_ffn1_w_down, m_mix_norm, m_w_in, m_rwkv_mu, m_rwkv_w0, m_rwkv_w_up, m_rwkv_a0, m_rwkv_a_up, m_rwkv_g_up, m_rwkv_k_k, m_rwkv_k_a, m_rwkv_r_k, m_rwkv_lnx_w, m_rwkv_lnx_b, m_w_out, m_ffn2_norm, m_ffn2_w_gate, m_ffn2_w_up, m_ffn2_w_down, m_final_norm)))
    v = dict(zip(WEIGHTS, (v_meta_tokens, v_ffn1_norm, v_ffn1_w_gate, v_ffn1_w_up, v_ffn1_w_down, v_mix_norm, v_w_in, v_rwkv_mu, v_rwkv_w0, v_rwkv_w_up, v_rwkv_a0, v_rwkv_a_up, v_rwkv_g_up, v_rwkv_k_k, v_rwkv_k_a, v_rwkv_r_k, v_rwkv_lnx_w, v_rwkv_lnx_b, v_w_out, v_ffn2_norm, v_ffn2_w_gate, v_ffn2_w_up, v_ffn2_w_down, v_final_norm)))
    loss, grad_x, grad, delta, new_m, new_v = _step(x, loss_target, w, m, v)
    return (loss, grad_x, *[grad[n] for n in WEIGHTS], *[delta[n] for n in WEIGHTS],
            *[new_m[n] for n in WEIGHTS], *[new_v[n] for n in WEIGHTS])
```

```python
import functools
import types

import jax
import jax.numpy as jnp
from jax import lax
from jax.experimental import pallas as pl
from jax.experimental.pallas import tpu as pltpu

F32 = jnp.float32
BF16 = jnp.bfloat16

RMS_EPS = 1e-6
LNX_EPS = 64e-5
N_META = 16
ROW0 = 128
META_PAD = ROW0 - N_META
HEAD = 64
N_HEADS = 8
GROUP = N_HEADS * HEAD
LORA_W, LORA_A, LORA_G = 32, 32, 96
LORA_PAD = 256
RW_COLS = 3 * GROUP + LORA_PAD
IN_COLS_PAD = 3 * GROUP + RW_COLS
ATT_BLOCK = 128
CHUNK = 64
VMEM_LIMIT = 56 * 1024 * 1024

ADAM_LR, ADAM_B1, ADAM_B2, ADAM_EPS, ADAM_WD, ADAM_STEP = 0.001, 0.9, 0.999, 1e-08, 0.01, 10

MESH = pl.DeviceIdType.MESH


def _params(*sem):
    return pltpu.CompilerParams(dimension_semantics=tuple(sem), vmem_limit_bytes=VMEM_LIMIT)


def _dot(a, b):
    return lax.dot_general(a, b, (((1,), (0,)), ((), ())), preferred_element_type=F32)


def _dot_nt(a, b):
    return lax.dot_general(a, b, (((1,), (1,)), ((), ())), preferred_element_type=F32)


def _dot_tn(a, b):
    return lax.dot_general(a, b, (((0,), (0,)), ((), ())), preferred_element_type=F32)


def _split2(x):
    hi = x.astype(BF16)
    return hi, (x - hi.astype(F32)).astype(BF16)


def _sigmoid(x):
    return 1.0 / (1.0 + jnp.exp(-x))


def _rms_fwd(x, g):
    rstd = lax.rsqrt(jnp.mean(x * x, axis=-1, keepdims=True) + RMS_EPS)
    xhat = x * rstd
    return xhat * g, xhat, rstd


def _rms_bwd(dn, xhat, rstd, g):
    dxhat = dn * g
    dx = rstd * (dxhat - xhat * jnp.mean(dxhat * xhat, axis=-1, keepdims=True))
    return dx, jnp.sum(dn * xhat, axis=0, keepdims=True)


def _row_tile(rows):
    return 384 if rows % 384 == 0 else 128


def _half_tile(cols):
    return cols // 2 if cols % 256 == 0 else cols


def _tall_tile(rows, parts):
    return rows // parts if rows % (16 * parts) == 0 else _row_tile(rows)


def _ffn_fwd(name, h, g, wg, wu, wd):
    rows, d = h.shape
    f = wg.shape[1]
    tm, tf = _row_tile(rows), _half_tile(f)
    nj = f // tf

    def body(h_ref, g_ref, wg_ref, wu_ref, wd_ref, ho_ref, a_ref, b_ref, n_sc, acc_sc):
        j = pl.program_id(1)

        @pl.when(j == 0)
        def _():
            n, _, _ = _rms_fwd(h_ref[...], g_ref[...])
            n_sc[...] = n.astype(BF16)
            acc_sc[...] = jnp.zeros_like(acc_sc)

        n = n_sc[...]
        a = _dot(n, wg_ref[...])
        b = _dot(n, wu_ref[...])
        a_ref[...] = a
        b_ref[...] = b
        s = a * _sigmoid(a) * b
        acc_sc[...] += _dot(s.astype(BF16), wd_ref[...])

        @pl.when(j == nj - 1)
        def _():
            ho_ref[...] = h_ref[...] + 0.5 * acc_sc[...]

    return pl.pallas_call(
        body, name=name, grid=(rows // tm, nj),
        in_specs=[pl.BlockSpec((tm, d), lambda i, j: (i, 0)),
                  pl.BlockSpec((1, d), lambda i, j: (0, 0)),
                  pl.BlockSpec((d, tf), lambda i, j: (0, j)),
                  pl.BlockSpec((d, tf), lambda i, j: (0, j)),
                  pl.BlockSpec((tf, d), lambda i, j: (j, 0))],
        out_specs=[pl.BlockSpec((tm, d), lambda i, j: (i, 0)),
                   pl.BlockSpec((tm, tf), lambda i, j: (i, j)),
                   pl.BlockSpec((tm, tf), lambda i, j: (i, j))],
        out_shape=[jax.ShapeDtypeStruct((rows, d), F32),
                   jax.ShapeDtypeStruct((rows, f), F32),
                   jax.ShapeDtypeStruct((rows, f), F32)],
        scratch_shapes=[pltpu.VMEM((tm, d), BF16), pltpu.VMEM((tm, d), F32)],
        compiler_params=_params("arbitrary", "arbitrary"),
    )(h, g, wg, wu, wd)


def _ffn_bwd(name, dh, h, g, a, b, wg, wu, wd):
    rows, d = h.shape
    f = wg.shape[1]
    tm, tf = _row_tile(rows), _half_tile(f)
    ni, nj = rows // tm, f // tf

    def body(dh_ref, h_ref, g_ref, a_ref, b_ref, wg_ref, wu_ref, wd_ref,
             dhi_ref, da_ref, db_ref, s_ref, n_ref, dhh_ref, dg_ref, dn_sc):
        i, j = pl.program_id(0), pl.program_id(1)

        @pl.when(j == 0)
        def _():
            n, _, _ = _rms_fwd(h_ref[...], g_ref[...])
            n_ref[...] = n.astype(BF16)
            dhh_ref[...] = (0.5 * dh_ref[...]).astype(BF16)
            dn_sc[...] = jnp.zeros_like(dn_sc)

        @pl.when((i == 0) & (j == 0))
        def _():
            dg_ref[...] = jnp.zeros_like(dg_ref)

        ds = _dot_nt(dhh_ref[...], wd_ref[...])
        av, bv = a_ref[...], b_ref[...]
        sig = _sigmoid(av)
        silu = av * sig
        s_ref[...] = (silu * bv).astype(BF16)
        db = (ds * silu).astype(BF16)
        da = (ds * bv * (sig * (1.0 + av * (1.0 - sig)))).astype(BF16)
        da_ref[...] = da
        db_ref[...] = db
        dn_sc[...] += _dot_nt(da, wg_ref[...]) + _dot_nt(db, wu_ref[...])

        @pl.when(j == nj - 1)
        def _():
            gv = g_ref[...]
            _, xhat, rstd = _rms_fwd(h_ref[...], gv)
            dx, dg = _rms_bwd(dn_sc[...], xhat, rstd, gv)
            dhi_ref[...] = dh_ref[...] + dx
            dg_ref[...] += dg

    return pl.pallas_call(
        body, name=name, grid=(ni, nj),
        in_specs=[pl.BlockSpec((tm, d), lambda i, j: (i, 0)),
                  pl.BlockSpec((tm, d), lambda i, j: (i, 0)),
                  pl.BlockSpec((1, d), lambda i, j: (0, 0)),
                  pl.BlockSpec((tm, tf), lambda i, j: (i, j)),
                  pl.BlockSpec((tm, tf), lambda i, j: (i, j)),
                  pl.BlockSpec((d, tf), lambda i, j: (0, j)),
                  pl.BlockSpec((d, tf), lambda i, j: (0, j)),
                  pl.BlockSpec((tf, d), lambda i, j: (j, 0))],
        out_specs=[pl.BlockSpec((tm, d), lambda i, j: (i, 0)),
                   pl.BlockSpec((tm, tf), lambda i, j: (i, j)),
                   pl.BlockSpec((tm, tf), lambda i, j: (i, j)),
                   pl.BlockSpec((tm, tf), lambda i, j: (i, j)),
                   pl.BlockSpec((tm, d), lambda i, j: (i, 0)),
                   pl.BlockSpec((tm, d), lambda i, j: (i, 0)),
                   pl.BlockSpec((1, d), lambda i, j: (0, 0))],
        out_shape=[jax.ShapeDtypeStruct((rows, d), F32),
                   jax.ShapeDtypeStruct((rows, f), BF16),
                   jax.ShapeDtypeStruct((rows, f), BF16),
                   jax.ShapeDtypeStruct((rows, f), BF16),
                   jax.ShapeDtypeStruct((rows, d), BF16),
                   jax.ShapeDtypeStruct((rows, d), BF16),
                   jax.ShapeDtypeStruct((1, d), F32)],
        scratch_shapes=[pltpu.VMEM((tm, d), F32)],
        compiler_params=_params("arbitrary", "arbitrary"),
    )(dh, h, g, a, b, wg, wu, wd)


def _mm_tn(name, a, b):
    k, m = a.shape
    n = b.shape[1]
    tk = _tall_tile(k, 3)
    tm = _half_tile(m) if m > 1024 else m
    tn = _half_tile(n) if n > 1024 else n
    nk = k // tk

    def body(a_ref, b_ref, o_ref, acc):
        kk = pl.program_id(2)

        @pl.when(kk == 0)
        def _():
            acc[...] = jnp.zeros_like(acc)

        acc[...] += _dot_tn(a_ref[...], b_ref[...])

        @pl.when(kk == nk - 1)
        def _():
            o_ref[...] = acc[...].astype(BF16)

    return pl.pallas_call(
        body, name=name, grid=(m // tm, n // tn, nk),
        in_specs=[pl.BlockSpec((tk, tm), lambda i, j, kk: (kk, i)),
                  pl.BlockSpec((tk, tn), lambda i, j, kk: (kk, j))],
        out_specs=pl.BlockSpec((tm, tn), lambda i, j, kk: (i, j)),
        out_shape=jax.ShapeDtypeStruct((m, n), BF16),
        scratch_shapes=[pltpu.VMEM((tm, tn), F32)],
        compiler_params=_params("arbitrary", "arbitrary", "arbitrary"),
    )(a, b)


def _norm_proj(name, h, g, w):
    rows, d = h.shape
    n = w.shape[1]
    split = 3 * GROUP
    tm = _row_tile(rows)

    def body(h_ref, g_ref, w_ref, qkv_ref, p_ref, n_ref):
        nv, _, _ = _rms_fwd(h_ref[...], g_ref[...])
        nb = nv.astype(BF16)
        n_ref[...] = nb
        qkv_ref[...] = _dot(nb, w_ref[:, :split]).astype(BF16)
        p_ref[...] = _dot(nb, w_ref[:, split:])

    return pl.pallas_call(
        body, name=name, grid=(rows // tm,),
        in_specs=[pl.BlockSpec((tm, d), lambda i: (i, 0)),
                  pl.BlockSpec((1, d), lambda i: (0, 0)),
                  pl.BlockSpec((d, n), lambda i: (0, 0))],
        out_specs=[pl.BlockSpec((tm, split), lambda i: (i, 0)),
                   pl.BlockSpec((tm, n - split), lambda i: (i, 0)),
                   pl.BlockSpec((tm, d), lambda i: (i, 0))],
        out_shape=[jax.ShapeDtypeStruct((rows, split), BF16), jax.ShapeDtypeStruct((rows, n - split), F32),
                   jax.ShapeDtypeStruct((rows, d), BF16)],
        compiler_params=_params("arbitrary"),
    )(h, g, w)


def _out_proj(name, h, sb, rw, w):
    rows, d = h.shape
    gw = sb.shape[1]
    tm = _row_tile(rows)

    def body(h_ref, sb_ref, rw_ref, w_ref, o_ref, mix_ref):
        mix_ref[:, :gw] = sb_ref[...].astype(BF16)
        mix_ref[:, gw:] = rw_ref[...].astype(BF16)
        o_ref[...] = h_ref[...] + _dot(mix_ref[...], w_ref[...])

    return pl.pallas_call(
        body, name=name, grid=(rows // tm,),
        in_specs=[pl.BlockSpec((tm, d), lambda i: (i, 0)),
                  pl.BlockSpec((tm, gw), lambda i: (i, 0)),
                  pl.BlockSpec((tm, gw), lambda i: (i, 0)),
                  pl.BlockSpec((2 * gw, d), lambda i: (0, 0))],
        out_specs=[pl.BlockSpec((tm, d), lambda i: (i, 0)),
                   pl.BlockSpec((tm, 2 * gw), lambda i: (i, 0))],
        out_shape=[jax.ShapeDtypeStruct((rows, d), F32), jax.ShapeDtypeStruct((rows, 2 * gw), BF16)],
        compiler_params=_params("arbitrary"),
    )(h, sb, rw, w)


def _out_proj_bwd(name, dh, w):
    rows, d = dh.shape
    k = w.shape[0]
    tm = _row_tile(rows)

    def body(dh_ref, w_ref, dsb_ref, drw_ref, dhb_ref):
        dhb = dh_ref[...].astype(BF16)
        dhb_ref[...] = dhb
        dsb_ref[...] = _dot_nt(dhb, w_ref[:GROUP, :]).astype(BF16)
        drw_ref[...] = _dot_nt(dhb, w_ref[GROUP:, :])

    return pl.pallas_call(
        body, name=name, grid=(rows // tm,),
        in_specs=[pl.BlockSpec((tm, d), lambda i: (i, 0)),
                  pl.BlockSpec((k, d), lambda i: (0, 0))],
        out_specs=[pl.BlockSpec((tm, GROUP), lambda i: (i, 0)),
                   pl.BlockSpec((tm, GROUP), lambda i: (i, 0)),
                   pl.BlockSpec((tm, d), lambda i: (i, 0))],
        out_shape=[jax.ShapeDtypeStruct((rows, GROUP), BF16), jax.ShapeDtypeStruct((rows, GROUP), F32),
                   jax.ShapeDtypeStruct((rows, d), BF16)],
        compiler_params=_params("arbitrary"),
    )(dh, w)


def _norm_proj_bwd(name, dproj, w, h, g, dh):
    rows, n = dproj.shape
    d = w.shape[0]
    tm = _row_tile(rows)

    def body(dp_ref, w_ref, h_ref, g_ref, dh_ref, o_ref, dg_ref):
        @pl.when(pl.program_id(0) == 0)
        def _():
            dg_ref[...] = jnp.zeros_like(dg_ref)

        dn = _dot_nt(dp_ref[...], w_ref[...])
        gv = g_ref[...]
        _, xhat, rstd = _rms_fwd(h_ref[...], gv)
        dx, dg = _rms_bwd(dn, xhat, rstd, gv)
        o_ref[...] = dh_ref[...] + dx
        dg_ref[...] += dg

    return pl.pallas_call(
        body, name=name, grid=(rows // tm,),
        in_specs=[pl.BlockSpec((tm, n), lambda i: (i, 0)),
                  pl.BlockSpec((d, n), lambda i: (0, 0)),
                  pl.BlockSpec((tm, d), lambda i: (i, 0)),
                  pl.BlockSpec((1, d), lambda i: (0, 0)),
                  pl.BlockSpec((tm, d), lambda i: (i, 0))],
        out_specs=[pl.BlockSpec((tm, d), lambda i: (i, 0)),
                   pl.BlockSpec((1, d), lambda i: (0, 0))],
        out_shape=[jax.ShapeDtypeStruct((rows, d), F32), jax.ShapeDtypeStruct((1, d), F32)],
        compiler_params=_params("arbitrary"),
    )(dproj, w, h, g, dh)


def _loss_head(name, h, g, tgt):
    rows, d = h.shape
    tm = _row_tile(rows)

    def body(h_ref, g_ref, t_ref, loss_ref, dh_ref, dg_ref):
        i = pl.program_id(0)

        @pl.when(i == 0)
        def _():
            loss_ref[...] = jnp.zeros_like(loss_ref)
            dg_ref[...] = jnp.zeros_like(dg_ref)

        gv = g_ref[...]
        y, xhat, rstd = _rms_fwd(h_ref[...], gv)
        row = i * tm + lax.broadcasted_iota(jnp.int32, (tm, 1), 0)
        diff = jnp.where(row >= ROW0, y - t_ref[...], 0.0)
        part = 0.5 * jnp.sum(jnp.sum(diff * diff, axis=-1, keepdims=True), axis=0, keepdims=True) / d
        loss_ref[...] += jnp.broadcast_to(part, loss_ref.shape)
        dx, dg = _rms_bwd(diff / d, xhat, rstd, gv)
        dh_ref[...] = dx
        dg_ref[...] += dg

    return pl.pallas_call(
        body, name=name, grid=(rows // tm,),
        in_specs=[pl.BlockSpec((tm, d), lambda i: (i, 0)),
                  pl.BlockSpec((1, d), lambda i: (0, 0)),
                  pl.BlockSpec((tm, d), lambda i: (i, 0))],
        out_specs=[pl.BlockSpec((8, 128), lambda i: (0, 0)),
                   pl.BlockSpec((tm, d), lambda i: (i, 0)),
                   pl.BlockSpec((1, d), lambda i: (0, 0))],
        out_shape=[jax.ShapeDtypeStruct((8, 128), F32),
                   jax.ShapeDtypeStruct((rows, d), F32),
                   jax.ShapeDtypeStruct((1, d), F32)],
        compiler_params=_params("arbitrary"),
    )(h, g, tgt)


def _sb_block(qb, kb, i, jb, scale):
    bq, bk = qb.shape[0], kb.shape[0]
    z = _dot_nt(qb, kb) * scale
    qpos = i * bq + lax.broadcasted_iota(jnp.int32, (bq, bk), 0)
    kpos = jb * bk + lax.broadcasted_iota(jnp.int32, (bq, bk), 1)
    valid = (kpos < qpos) & (kpos >= META_PAD)
    e = jnp.exp(-jnp.abs(z))
    log_keep = jnp.where(valid, -(jnp.maximum(z, 0.0) + jnp.log(1.0 + e)), 0.0)
    return z, valid, e, log_keep


def _tri2(n, cmp):
    r = lax.broadcasted_iota(jnp.int32, (2 * n, n), 0) % n
    c = lax.broadcasted_iota(jnp.int32, (2 * n, n), 1)
    return cmp(r, c).astype(BF16)


def _dot_split(x, t2):
    hi, lo = _split2(x)
    return _dot(jnp.concatenate([hi, lo], axis=1), t2)


ATT_HEADS = 128 // HEAD
ATT_CUT = -104.0
ATT_TILES = GROUP // 128


def _lanes(hh):
    return slice(hh * HEAD, (hh + 1) * HEAD)


def _first_and_last_step(grid):
    here = [pl.program_id(a) for a in range(len(grid))]
    first, last = here[0] == 0, here[0] == grid[0] - 1
    for a in range(1, len(grid)):
        first, last = first & (here[a] == 0), last & (here[a] == grid[a] - 1)
    return first, last


def _sb_fwd(name, qkv, shards=()):
    rows = qkv.shape[0]
    nh, dh = N_HEADS, HEAD
    bq, bk, hg = _row_tile(rows), ATT_BLOCK, ATT_HEADS
    per = bq // bk
    scale = dh ** -0.5
    ns = len(shards)
    grid = (nh // hg, rows // bq)

    def body(q_ref, k_ref, v_ref, *rest):
        o_ref, rt_ref, cnt_ref = rest[ns:ns + 3]
        if ns:
            first, last = _first_and_last_step(grid)
            start, finish = _gather_exchange(rest[:ns], rest[ns + 3:2 * ns + 3], rest[2 * ns + 3:])
            pl.when(first)(start)
        i = pl.program_id(1)
        after = _tri2(bk, lambda r, c: r > c)
        nkb = (i + 1) * per

        def live(state):
            n, carry = state
            top = jnp.max(carry[0][0])
            for hh in range(1, hg):
                top = jnp.maximum(top, jnp.max(carry[hh][0]))
            return (n < nkb) & (top >= ATT_CUT)

        def step(state):
            n, carry = state
            jb = nkb - 1 - n
            off = pl.multiple_of(jb * bk, bk)
            out = []
            for hh in range(hg):
                rest, acc = carry[hh]
                kb = k_ref[pl.ds(off, bk), _lanes(hh)]
                vb = v_ref[pl.ds(off, bk), _lanes(hh)]
                z, valid, _, log_keep = _sb_block(q_ref[:, _lanes(hh)], kb, i, jb, scale)
                log_rest = rest + _dot_split(log_keep, after)
                attn = jnp.where(valid, jnp.exp(z + log_keep + log_rest), 0.0)
                out.append((rest + jnp.sum(log_keep, axis=-1, keepdims=True), acc + _dot(attn.astype(BF16), vb)))
            return n + 1, tuple(out)

        init = tuple((jnp.zeros((bq, 1), F32), jnp.zeros((bq, dh), F32)) for _ in range(hg))
        n, res = lax.while_loop(live, step, (jnp.int32(0), init))
        for hh in range(hg):
            rt_ref[hh] = res[hh][0]
            o_ref[:, _lanes(hh)] = res[hh][1]
            cnt_ref[hh] = jnp.full((bq, 1), n, F32)
        if ns:
            pl.when(last)(finish)

    return pl.pallas_call(
        body, name=name, grid=grid,
        in_specs=[pl.BlockSpec((bq, 128), lambda h, i: (i, h)),
                  pl.BlockSpec((rows, 128), lambda h, i: (0, ATT_TILES + h)),
                  pl.BlockSpec((rows, 128), lambda h, i: (0, 2 * ATT_TILES + h))] + [HBM] * ns,
        out_specs=[pl.BlockSpec((bq, 128), lambda h, i: (i, h)),
                   pl.BlockSpec((hg, bq, 1), lambda h, i: (h, i, 0)),
                   pl.BlockSpec((hg, bq, 1), lambda h, i: (h, i, 0))] + [HBM] * ns,
        out_shape=[jax.ShapeDtypeStruct((rows, GROUP), F32), jax.ShapeDtypeStruct((nh, rows, 1), F32),
                   jax.ShapeDtypeStruct((nh, rows, 1), F32)]
        + [jax.ShapeDtypeStruct((N_CHIPS,) + s.shape, s.dtype) for s in shards],
        scratch_shapes=_gather_sems(ns) if ns else [],
        compiler_params=_params("arbitrary", "arbitrary"),
    )(qkv, qkv, qkv, *shards)


def _sb_bwd(name, qkv, rt, cnt, do, parts=()):
    rows = qkv.shape[0]
    nh, dh = N_HEADS, HEAD
    bq, bk, hg = _row_tile(rows), ATT_BLOCK, ATT_HEADS
    per = bq // bk
    scale = dh ** -0.5
    ns = len(parts)
    grid = (nh // hg, rows // bq)

    def body(q_ref, k_ref, v_ref, rt_ref, cnt_ref, do_ref, *rest):
        dq_ref, dk_ref, dv_ref = rest[ns:ns + 3]
        if ns:
            at_first, at_last = _first_and_last_step(grid)
            start, finish = _reduce_exchange(rest[:ns], rest[ns + 3:2 * ns + 3], rest[2 * ns + 3:3 * ns + 3],
                                             rest[3 * ns + 3:])
            pl.when(at_first)(start)
        i = pl.program_id(1)

        @pl.when(i == 0)
        def _():
            dk_ref[...] = jnp.zeros_like(dk_ref)
            dv_ref[...] = jnp.zeros_like(dv_ref)

        upto = _tri2(bk, lambda r, c: r <= c)
        before = _tri2(bk, lambda r, c: r < c)
        nkb = (i + 1) * per
        first = nkb - jnp.max(cnt_ref[0]).astype(jnp.int32)

        def step(jb, carry):
            off = pl.multiple_of(jb * bk, bk)
            out = []
            for hh in range(hg):
                keep_sum, g_sum, dq = carry[hh]
                qb, dob = q_ref[:, _lanes(hh)], do_ref[:, _lanes(hh)]
                kb = k_ref[pl.ds(off, bk), _lanes(hh)]
                vb = v_ref[pl.ds(off, bk), _lanes(hh)]
                z, valid, e, log_keep = _sb_block(qb, kb, i, jb, scale)
                log_rest = rt_ref[hh] - keep_sum - _dot_split(log_keep, upto)
                attn = jnp.where(valid, jnp.exp(z + log_keep + log_rest), 0.0)
                g = attn * _dot_nt(dob, vb)
                g_before = g_sum + _dot_split(g, before)
                inv = 1.0 / (1.0 + e)
                sig = jnp.where(z >= 0, inv, e * inv)
                dz = (jnp.where(valid, g * (1.0 - sig) - g_before * sig, 0.0) * scale).astype(BF16)
                dk_ref[pl.ds(off, bk), _lanes(hh)] += _dot_tn(dz, qb)
                dv_ref[pl.ds(off, bk), _lanes(hh)] += _dot_tn(attn.astype(BF16), dob)
                out.append((keep_sum + jnp.sum(log_keep, axis=-1, keepdims=True),
                            g_sum + jnp.sum(g, axis=-1, keepdims=True),
                            dq + _dot(dz, kb)))
            return tuple(out)

        zero = jnp.zeros((bq, 1), F32)
        res = lax.fori_loop(first, nkb, step, tuple((zero, zero, jnp.zeros((bq, dh), F32)) for _ in range(hg)))
        for hh in range(hg):
            dq_ref[:, _lanes(hh)] = res[hh][2]
        if ns:
            pl.when(at_last)(finish)

    return pl.pallas_call(
        body, name=name, grid=grid,
        in_specs=[pl.BlockSpec((bq, 128), lambda h, i: (i, h)),
                  pl.BlockSpec((rows, 128), lambda h, i: (0, ATT_TILES + h)),
                  pl.BlockSpec((rows, 128), lambda h, i: (0, 2 * ATT_TILES + h)),
                  pl.BlockSpec((hg, bq, 1), lambda h, i: (h, i, 0)),
                  pl.BlockSpec((hg, bq, 1), lambda h, i: (h, i, 0)),
                  pl.BlockSpec((bq, 128), lambda h, i: (i, h))] + [HBM] * ns,
        out_specs=[pl.BlockSpec((bq, 128), lambda h, i: (i, h)),
                   pl.BlockSpec((rows, 128), lambda h, i: (0, h)),
                   pl.BlockSpec((rows, 128), lambda h, i: (0, h))] + [HBM] * (2 * ns),
        out_shape=[jax.ShapeDtypeStruct((rows, GROUP), F32)] * 3
        + [jax.ShapeDtypeStruct(s.shape, s.dtype) for s in parts] * 2,
        scratch_shapes=_reduce_sems(ns) if ns else [],
        compiler_params=_params("arbitrary", "arbitrary"),
    )(qkv, qkv, qkv, rt, cnt, do, *parts)


def _head_sum(x, ones_bd):
    return _dot_split(x, ones_bd)


def _rwkv_pre(p, p_prev, mu, w0, a0, k_k, k_a, w_up, a_up, g_up, ones_bd):
    xs = p + (p_prev - p) * mu
    r = xs[:, :GROUP]
    k0 = xs[:, GROUP:2 * GROUP]
    v = xs[:, 2 * GROUP:3 * GROUP]
    lo = xs[:, 3 * GROUP:]
    wa = w0 + _dot(jnp.tanh(lo).astype(BF16), w_up.astype(BF16))
    w = -(jnp.maximum(-wa, 0.0) + jnp.log(1.0 + jnp.exp(-jnp.abs(wa)))) - 0.5
    log_decay = -jnp.exp(w)
    alpha = _sigmoid(a0 + _dot(lo.astype(BF16), a_up.astype(BF16)))
    gate = _dot(_sigmoid(lo).astype(BF16), g_up.astype(BF16))
    kk = k0 * k_k
    kk = kk * lax.rsqrt(jnp.maximum(_head_sum(kk * kk, ones_bd), 1e-24))
    k = k0 * (1.0 + (alpha - 1.0) * k_a)
    return r, log_decay, k, v, -kk, kk * alpha, gate


def _rwkv_post(y, r, k, v, gate, lnx_w, lnx_b, r_k, ones_bd):
    mean = _head_sum(y, ones_bd) * (1.0 / HEAD)
    yc = y - mean
    var = _head_sum(yc * yc, ones_bd) * (1.0 / HEAD)
    yn = yc * lax.rsqrt(var + LNX_EPS) * lnx_w + lnx_b
    bonus = _head_sum(r * k * r_k, ones_bd) * v
    return (yn + bonus) * gate


TOKEN_TILE = 128
_PRE_VEC = 5
_PRE_MAT = 3


def _split_heads(o_ref, val):
    for h in range(N_HEADS):
        o_ref[h] = val[:, _lanes(h)]


def _merge_heads(ref):
    return jnp.concatenate([ref[h] for h in range(N_HEADS)], axis=1)


def _head_spec(tm):
    return pl.BlockSpec((N_HEADS, tm, HEAD), lambda i: (0, i, 0))


def _rwkv_pre_fwd(name, p, p_prev, vecs, mats, ones_bd):
    rows = p.shape[0]
    tm = TOKEN_TILE
    row_spec = lambda w: pl.BlockSpec((tm, w), lambda i: (i, 0))
    full = lambda a: pl.BlockSpec(a.shape, lambda i: (0,) * a.ndim)

    def body(p_ref, pp_ref, *refs):
        ins = [r[...] for r in refs[:_PRE_VEC + _PRE_MAT + 1]]
        outs = refs[_PRE_VEC + _PRE_MAT + 1:]
        vals = _rwkv_pre(p_ref[...], pp_ref[...], *ins)
        for o_ref, val in zip(outs[:6], vals[:6]):
            _split_heads(o_ref, val)
        for o_ref, val in zip(outs[6:], (vals[0], vals[2], vals[3], vals[6])):
            o_ref[...] = val

    return pl.pallas_call(
        body, name=name, grid=(rows // tm,),
        in_specs=[row_spec(RW_COLS), row_spec(RW_COLS)] + [full(a) for a in (*vecs, *mats, ones_bd)],
        out_specs=[_head_spec(tm)] * 6 + [row_spec(GROUP)] * 4,
        out_shape=([jax.ShapeDtypeStruct((N_HEADS, rows, HEAD), F32)] * 6
                   + [jax.ShapeDtypeStruct((rows, GROUP), F32)] * 4),
        compiler_params=_params("arbitrary"),
    )(p, p_prev, *vecs, *mats, ones_bd)


def _rwkv_pre_bwd(name, p, p_prev, vecs, mats, ones_bd, cts_scan, ct_gate, cts_b):
    rows = p.shape[0]
    tm = TOKEN_TILE
    n_par = _PRE_VEC + _PRE_MAT
    row_spec = lambda w: pl.BlockSpec((tm, w), lambda i: (i, 0))
    full = lambda a: pl.BlockSpec(a.shape, lambda i: (0,) * a.ndim)

    def body(*refs):
        p_ref, pp_ref = refs[0], refs[1]
        par = [r[...] for r in refs[2:2 + n_par]]
        ones = refs[2 + n_par][...]
        cta = [_merge_heads(r) for r in refs[3 + n_par:9 + n_par]] + [refs[9 + n_par][...]]
        ctb = [r[...] for r in refs[10 + n_par:13 + n_par]]
        outs = refs[13 + n_par:]
        ct = (cta[0] + ctb[0], cta[1], cta[2] + ctb[1], cta[3] + ctb[2], cta[4], cta[5], cta[6])
        _, vjp = jax.vjp(lambda pv, ppv, *pr: _rwkv_pre(pv, ppv, *pr, ones), p_ref[...], pp_ref[...], *par)
        grads = vjp(ct)
        outs[0][...] = grads[0]
        outs[1][...] = grads[1]

        @pl.when(pl.program_id(0) == 0)
        def _():
            for o_ref in outs[2:]:
                o_ref[...] = jnp.zeros_like(o_ref)

        for o_ref, gval in zip(outs[2:], grads[2:]):
            o_ref[...] += gval

    par_arrays = (*vecs, *mats)
    return pl.pallas_call(
        body, name=name, grid=(rows // tm,),
        in_specs=([row_spec(RW_COLS)] * 2 + [full(a) for a in (*par_arrays, ones_bd)]
                  + [_head_spec(tm)] * 6 + [row_spec(GROUP)] * 4),
        out_specs=[row_spec(RW_COLS)] * 2 + [full(a) for a in par_arrays],
        out_shape=([jax.ShapeDtypeStruct((rows, RW_COLS), F32)] * 2
                   + [jax.ShapeDtypeStruct(a.shape, F32) for a in par_arrays]),
        compiler_params=_params("arbitrary"),
    )(p, p_prev, *par_arrays, ones_bd, *cts_scan, ct_gate, *cts_b)


def _rwkv_post_fwd(name, y, r, k, v, gate, vecs, ones_bd):
    rows = r.shape[0]
    tm = TOKEN_TILE
    row_spec = pl.BlockSpec((tm, GROUP), lambda i: (i, 0))
    full = lambda a: pl.BlockSpec(a.shape, lambda i: (0,) * a.ndim)

    def body(y_ref, *refs):
        vals = [r_[...] for r_ in refs[:-1]]
        refs[-1][...] = _rwkv_post(_merge_heads(y_ref), *vals)

    return pl.pallas_call(
        body, name=name, grid=(rows // tm,),
        in_specs=[_head_spec(tm)] + [row_spec] * 4 + [full(a) for a in (*vecs, ones_bd)],
        out_specs=row_spec,
        out_shape=jax.ShapeDtypeStruct((rows, GROUP), F32),
        compiler_params=_params("arbitrary"),
    )(y, r, k, v, gate, *vecs, ones_bd)


def _rwkv_post_bwd(name, y, r, k, v, gate, vecs, ones_bd, dout):
    rows = r.shape[0]
    tm = TOKEN_TILE
    row_spec = pl.BlockSpec((tm, GROUP), lambda i: (i, 0))
    full = lambda a: pl.BlockSpec(a.shape, lambda i: (0,) * a.ndim)

    def body(y_ref, *refs):
        vals = [_merge_heads(y_ref)] + [r_[...] for r_ in refs[:7]]
        ones = refs[7][...]
        dout_v = refs[8][...]
        outs = refs[9:]
        _, vjp = jax.vjp(lambda *a: _rwkv_post(*a, ones), *vals)
        grads = vjp(dout_v)
        _split_heads(outs[0], grads[0])
        for o_ref, gval in zip(outs[1:5], grads[1:5]):
            o_ref[...] = gval

        @pl.when(pl.program_id(0) == 0)
        def _():
            for o_ref in outs[5:]:
                o_ref[...] = jnp.zeros_like(o_ref)

        for o_ref, gval in zip(outs[5:], grads[5:]):
            o_ref[...] += gval

    return pl.pallas_call(
        body, name=name, grid=(rows // tm,),
        in_specs=[_head_spec(tm)] + [row_spec] * 4 + [full(a) for a in (*vecs, ones_bd)] + [row_spec],
        out_specs=[_head_spec(tm)] + [row_spec] * 4 + [full(a) for a in vecs],
        out_shape=([jax.ShapeDtypeStruct((N_HEADS, rows, HEAD), F32)] + [jax.ShapeDtypeStruct((rows, GROUP), F32)] * 4
                   + [jax.ShapeDtypeStruct(a.shape, F32) for a in vecs]),
        compiler_params=_params("arbitrary"),
    )(y, r, k, v, gate, *vecs, ones_bd, dout)


_NN = (((2,), (1,)), ((0,), (0,)))
_NT = (((2,), (2,)), ((0,), (0,)))
_TN = (((1,), (1,)), ((0,), (0,)))


_BWD_FORMS = {"nn": (("nt", False), ("tn", False)),
              "nt": (("nn", False), ("tn", True)),
              "tn": (("nt", True), ("nn", False))}
_DIMS = {"nn": _NN, "nt": _NT, "tn": _TN}


def _bdot(a, b, form):
    return lax.dot_general(a.astype(BF16), b.astype(BF16), _DIMS[form], preferred_element_type=F32)


@functools.partial(jax.custom_vjp, nondiff_argnums=(2,))
def _bmm(a, b, form):
    return _bdot(a, b, form)


def _bmm_fwd(a, b, form):
    return _bdot(a, b, form), (a.astype(BF16), b.astype(BF16))


def _bmm_bwd(form, res, dc):
    a, b = res
    (fa, swap_a), (fb, swap_b) = _BWD_FORMS[form]
    da = _bdot(b, dc, fa) if swap_a else _bdot(dc, b, fa)
    db = _bdot(dc, a, fb) if swap_b else _bdot(a, dc, fb)
    return da, db


_bmm.defvjp(_bmm_fwd, _bmm_bwd)


@jax.custom_vjp
def _cumsum_steps(x):
    return _tri_apply(x, lambda r, c: r >= c)


def _tri_apply(x, cmp):
    nh, c, _ = x.shape
    tri = cmp(lax.broadcasted_iota(jnp.int32, (c, c), 0), lax.broadcasted_iota(jnp.int32, (c, c), 1))
    tri = jnp.broadcast_to(tri.astype(BF16)[None], (nh, c, c))
    hi, lo = _split2(x)
    return (lax.dot_general(tri, hi, _NN, preferred_element_type=F32)
            + lax.dot_general(tri, lo, _NN, preferred_element_type=F32))


_cumsum_steps.defvjp(lambda x: (_cumsum_steps(x), None), lambda _, d: (_tri_apply(d, lambda r, c: r <= c),))


def _chunk(state, r, log_w, k, v, a, b):
    nh, c, _ = r.shape
    row = lax.broadcasted_iota(jnp.int32, (c, c), 0)
    col = lax.broadcasted_iota(jnp.int32, (c, c), 1)
    cum = _cumsum_steps(log_w)
    mid = cum[:, c // 2 - 1:c // 2, :]
    a_t = a * jnp.exp(cum - log_w - mid)
    r_t = r * jnp.exp(cum - mid)
    back = jnp.exp(mid - cum)
    b_t = b * back
    k_t = k * back
    strict, incl = (row > col)[None], (row >= col)[None]
    ar = jnp.concatenate([a_t, r_t], axis=1)
    on_b = _bmm(ar, b_t, "nt")
    on_k = _bmm(ar, k_t, "nt")
    n_mat = jnp.where(strict, on_b[:, :c], 0.0)
    p_mat = jnp.where(incl, on_b[:, c:], 0.0)
    m_mat = jnp.where(strict, on_k[:, :c], 0.0)
    q_mat = jnp.where(incl, on_k[:, c:], 0.0)
    inv, power, span = n_mat, _bmm(n_mat, n_mat, "nn"), 2
    while span < c:
        both = _bmm(jnp.concatenate([power, inv], axis=1), power, "nn")
        inv = inv + power + both[:, c:]
        power = both[:, :c]
        span *= 2
    s_mid = state * jnp.swapaxes(jnp.exp(mid), 1, 2)
    x = _bmm(jnp.concatenate([a_t, m_mat], axis=2), jnp.concatenate([s_mid, v], axis=1), "nn")
    u = x + _bmm(inv, x, "nn")
    y = _bmm(jnp.concatenate([r_t, p_mat, q_mat], axis=2), jnp.concatenate([s_mid, u, v], axis=1), "nn")
    grown = _bmm(jnp.concatenate([b_t, k_t], axis=1), jnp.concatenate([u, v], axis=1), "tn")
    s_new = (s_mid + grown) * jnp.swapaxes(jnp.exp(cum[:, c - 1:c, :] - mid), 1, 2)
    return y, s_new


def _scan_fwd(name, ops):
    nh, rows, dh = ops[0].shape
    nc = rows // CHUNK
    spec = pl.BlockSpec((nh, CHUNK, dh), lambda c: (0, c, 0))

    def body(r_ref, w_ref, k_ref, v_ref, a_ref, b_ref, y_ref, st_ref, state):
        @pl.when(pl.program_id(0) == 0)
        def _():
            state[...] = jnp.zeros_like(state)

        st_ref[0] = state[...]
        y, s_new = _chunk(state[...], r_ref[...], w_ref[...], k_ref[...], v_ref[...], a_ref[...], b_ref[...])
        y_ref[...] = y
        state[...] = s_new

    return pl.pallas_call(
        body, name=name, grid=(nc,),
        in_specs=[spec] * 6,
        out_specs=[spec, pl.BlockSpec((1, nh, dh, dh), lambda c: (c, 0, 0, 0))],
        out_shape=[jax.ShapeDtypeStruct((nh, rows, dh), F32), jax.ShapeDtypeStruct((nc, nh, dh, dh), F32)],
        scratch_shapes=[pltpu.VMEM((nh, dh, dh), F32)],
        compiler_params=_params("arbitrary"),
    )(*ops)


def _scan_bwd(name, ops, states, dy):
    nh, rows, dh = ops[0].shape
    nc = rows // CHUNK
    spec = pl.BlockSpec((nh, CHUNK, dh), lambda c: (0, nc - 1 - c, 0))

    def body(r_ref, w_ref, k_ref, v_ref, a_ref, b_ref, st_ref, dy_ref, *rest):
        outs, dstate = rest[:6], rest[6]

        @pl.when(pl.program_id(0) == 0)
        def _():
            dstate[...] = jnp.zeros_like(dstate)

        _, vjp = jax.vjp(_chunk, st_ref[0], r_ref[...], w_ref[...], k_ref[...], v_ref[...], a_ref[...], b_ref[...])
        grads = vjp((dy_ref[...], dstate[...]))
        dstate[...] = grads[0]
        for o_ref, gval in zip(outs, grads[1:]):
            o_ref[...] = gval

    return pl.pallas_call(
        body, name=name, grid=(nc,),
        in_specs=[spec] * 6 + [pl.BlockSpec((1, nh, dh, dh), lambda c: (nc - 1 - c, 0, 0, 0)), spec],
        out_specs=[spec] * 6,
        out_shape=[jax.ShapeDtypeStruct((nh, rows, dh), F32)] * 6,
        scratch_shapes=[pltpu.VMEM((nh, dh, dh), F32)],
        compiler_params=_params("arbitrary"),
    )(*ops, states, dy)


def _shift_down(x):
    return jnp.concatenate([jnp.zeros((1, x.shape[1]), x.dtype), x[:-1]], axis=0)


def _shift_up(x):
    return jnp.concatenate([x[1:], jnp.zeros((1, x.shape[1]), x.dtype)], axis=0)


def _pad_rows(x, rows):
    return jnp.concatenate([x, jnp.zeros((rows - x.shape[0],) + x.shape[1:], x.dtype)], axis=0)


def _pad_cols(x, cols):
    return jnp.concatenate([x, jnp.zeros(x.shape[:-1] + (cols - x.shape[-1],), x.dtype)], axis=-1)


def _lora_pad(w_up, a_up, g_up):
    z = lambda n: jnp.zeros((n, GROUP), F32)
    return (jnp.concatenate([w_up, z(LORA_PAD - LORA_W)], 0),
            jnp.concatenate([z(LORA_W), a_up, z(LORA_PAD - LORA_W - LORA_A)], 0),
            jnp.concatenate([z(LORA_W + LORA_A), g_up, z(LORA_PAD - LORA_W - LORA_A - LORA_G)], 0))


LATE = ['ffn2_w_gate', 'ffn2_w_up', 'ffn2_w_down', 'w_out']


def _local_step(x, tgt, w, late=None):
    d = x.shape[1]
    zeros = jnp.zeros((META_PAD, d), F32)
    h0 = jnp.concatenate([zeros, w["meta_tokens"], x], axis=0)
    tgt_p = jnp.concatenate([jnp.zeros((ROW0, d), F32), tgt], axis=0)
    ones_bd = ((lax.broadcasted_iota(jnp.int32, (2 * GROUP, GROUP), 0) % GROUP) // HEAD
               == lax.broadcasted_iota(jnp.int32, (2 * GROUP, GROUP), 1) // HEAD).astype(BF16)
    w_in = _pad_cols(w["w_in"], IN_COLS_PAD)
    pre_vecs = (_pad_cols(w["rwkv_mu"], RW_COLS), w["rwkv_w0"], w["rwkv_a0"], w["rwkv_k_k"], w["rwkv_k_a"])
    pre_mats = _lora_pad(w["rwkv_w_up"], w["rwkv_a_up"], w["rwkv_g_up"])
    post_vecs = (w["rwkv_lnx_w"], w["rwkv_lnx_b"], w["rwkv_r_k"].reshape(1, GROUP))

    h1, a1, b1 = _ffn_fwd("ffn1_fwd", h0, w["ffn1_norm"], w["ffn1_w_gate"], w["ffn1_w_up"], w["ffn1_w_down"])
    qkv, p, n2 = _norm_proj("in_proj", h1, w["mix_norm"], w_in)
    if late is None:
        sb, rest_total, visited = _sb_fwd("sb_fwd", qkv)
    else:
        sb, rest_total, visited, *gathered = _sb_fwd("sb_fwd", qkv, late.shards)
        w = {**w, **late.join(gathered)}
    p_prev = _shift_down(p)
    pre = _rwkv_pre_fwd("rwkv_pre_fwd", p, p_prev, pre_vecs, pre_mats, ones_bd)
    scan_ops, token_ops = pre[:6], pre[6:]
    y, states = _scan_fwd("rwkv_scan_fwd", scan_ops)
    rw = _rwkv_post_fwd("rwkv_post_fwd", y, *token_ops, post_vecs, ones_bd)
    h2, mix = _out_proj("out_proj", h1, sb, rw, w["w_out"])
    h3, a2, b2 = _ffn_fwd("ffn2_fwd", h2, w["ffn2_norm"], w["ffn2_w_gate"], w["ffn2_w_up"], w["ffn2_w_down"])
    loss8, dh3, g_final = _loss_head("loss_head", h3, w["final_norm"].reshape(1, d), tgt_p)

    g = {"final_norm": g_final.reshape(d)}
    dh2, da2, db2, s2, n3, dhh3, g["ffn2_norm"] = _ffn_bwd(
        "ffn2_bwd", dh3, h2, w["ffn2_norm"], a2, b2, w["ffn2_w_gate"], w["ffn2_w_up"], w["ffn2_w_down"])
    g["ffn2_w_gate"] = _mm_tn("ffn2_dgate", n3, da2)
    g["ffn2_w_up"] = _mm_tn("ffn2_dup", n3, db2)
    g["ffn2_w_down"] = _mm_tn("ffn2_ddown", s2, dhh3)
    dsb, drw, dh2b = _out_proj_bwd("out_proj_bwd", dh2, w["w_out"])
    g["w_out"] = _mm_tn("out_proj_dw", mix, dh2b)
    early = () if late is None else late.parts(g)
    dq, dk, dv, *reduced_early = _sb_bwd("sb_bwd", qkv, rest_total, visited, dsb, early)
    post_g = _rwkv_post_bwd("rwkv_post_bwd", y, *token_ops, post_vecs, ones_bd, drw)
    g["rwkv_lnx_w"], g["rwkv_lnx_b"] = post_g[5], post_g[6]
    g["rwkv_r_k"] = post_g[7].reshape(1, N_HEADS, HEAD)
    scan_g = _scan_bwd("rwkv_scan_bwd", scan_ops, states, post_g[0])
    pre_g = _rwkv_pre_bwd("rwkv_pre_bwd", p, p_prev, pre_vecs, pre_mats, ones_bd, scan_g, post_g[4], post_g[1:4])
    g["rwkv_mu"] = pre_g[2][:, :w["rwkv_mu"].shape[1]]
    g["rwkv_w0"], g["rwkv_a0"], g["rwkv_k_k"], g["rwkv_k_a"] = pre_g[3:7]
    g["rwkv_w_up"] = pre_g[7][:LORA_W]
    g["rwkv_a_up"] = pre_g[8][LORA_W:LORA_W + LORA_A]
    g["rwkv_g_up"] = pre_g[9][LORA_W + LORA_A:LORA_W + LORA_A + LORA_G]
    dp = pre_g[0] + _shift_up(pre_g[1])
    live = (jnp.arange(h0.shape[0]) >= META_PAD)[:, None]
    dproj = jnp.where(live, jnp.concatenate([dq, dk, dv, dp], axis=1), 0.0).astype(BF16)
    g["w_in"] = _mm_tn("in_proj_dw", n2, dproj)[:, :w["w_in"].shape[1]]
    dh1, g["mix_norm"] = _norm_proj_bwd("in_proj_bwd", dproj, w_in, h1, w["mix_norm"], dh2)
    dh0, da1, db1, s1, n1, dhh1, g["ffn1_norm"] = _ffn_bwd(
        "ffn1_bwd", dh1, h0, w["ffn1_norm"], a1, b1, w["ffn1_w_gate"], w["ffn1_w_up"], w["ffn1_w_down"])
    g["ffn1_w_gate"] = _mm_tn("ffn1_dgate", n1, da1)
    g["ffn1_w_up"] = _mm_tn("ffn1_dup", n1, db1)
    g["ffn1_w_down"] = _mm_tn("ffn1_ddown", s1, dhh1)
    g["meta_tokens"] = dh0[META_PAD:ROW0]
    return loss8[0, 0], dh0[ROW0:], g, reduced_early


N_CHIPS = 4
N_DEV = 8
HBM = pl.BlockSpec(memory_space=pltpu.HBM)


def _place():
    return lax.axis_index("x"), lax.axis_index("y"), lax.axis_index("c")


def _other_chips(x, y):
    return [(1 - x, y), (x, 1 - y), (1 - x, 1 - y)]


def _gather_sems(n):
    return [pltpu.SemaphoreType.DMA((3 * n,)), pltpu.SemaphoreType.DMA((3 * n,)), pltpu.SemaphoreType.DMA((n,)),
            pltpu.SemaphoreType.DMA((3 * n,)), pltpu.SemaphoreType.DMA((3 * n,))]


def _gather_exchange(ins, outs, sems):
    n = len(ins)
    half = [r.shape[0] // 2 for r in ins]
    send, recv, local, d2d_send, d2d_recv = sems
    x, y, c = _place()
    me = 2 * x + y
    chips = _other_chips(x, y)

    def rows_of(k, h):
        return pl.ds(pl.multiple_of(h * half[k], 8), half[k])

    def own(k):
        return pltpu.make_async_copy(ins[k], outs[k].at[me], local.at[k])

    def copy(j, k, slot):
        return pltpu.make_async_remote_copy(
            src_ref=ins[k].at[rows_of(k, c)], dst_ref=outs[k].at[slot, rows_of(k, c)],
            send_sem=send.at[j * n + k], recv_sem=recv.at[j * n + k],
            device_id=(chips[j][0], chips[j][1], c), device_id_type=MESH)

    def passed(j, k, h):
        slot = 2 * chips[j][0] + chips[j][1]
        return pltpu.make_async_remote_copy(
            src_ref=outs[k].at[slot, rows_of(k, h)], dst_ref=outs[k].at[slot, rows_of(k, h)],
            send_sem=d2d_send.at[j * n + k], recv_sem=d2d_recv.at[j * n + k],
            device_id=(x, y, 1 - c), device_id_type=MESH)

    def start():
        for k in range(n):
            own(k).start()
        for j in range(3):
            for k in range(n):
                copy(j, k, me).start()

    def finish():
        for j in range(3):
            for k in range(n):
                copy(j, k, 2 * chips[j][0] + chips[j][1]).wait_recv()
                passed(j, k, c).start()
        for j in range(3):
            for k in range(n):
                passed(j, k, 1 - c).wait_recv()
        for j in range(3):
            for k in range(n):
                copy(j, k, me).wait_send()
                passed(j, k, c).wait_send()
        for k in range(n):
            own(k).wait()

    return start, finish


def _gather_shards(name, shards):
    n = len(shards)

    def body(*refs):
        start, finish = _gather_exchange(refs[:n], refs[n:2 * n], refs[2 * n:])
        start()
        finish()

    return pl.pallas_call(
        body, name=name,
        in_specs=[HBM] * n, out_specs=[HBM] * n,
        out_shape=[jax.ShapeDtypeStruct((N_CHIPS,) + s.shape, s.dtype) for s in shards],
        scratch_shapes=_gather_sems(n),
    )(*shards)


def _pair_exchange(name, parts):
    n = len(parts)
    half = [s.shape[1] // 2 for s in parts]

    def body(*refs):
        ins, outs = refs[:n], refs[n:2 * n]
        send, recv = refs[2 * n:]
        x, y, c = _place()

        def copy(k):
            rows = pl.ds(pl.multiple_of((1 - c) * half[k], 8), half[k])
            return pltpu.make_async_remote_copy(
                src_ref=ins[k].at[:, rows], dst_ref=outs[k], send_sem=send.at[k], recv_sem=recv.at[k],
                device_id=(x, y, 1 - c), device_id_type=MESH)

        for k in range(n):
            copy(k).start()
        for k in range(n):
            copy(k).wait_recv()
        for k in range(n):
            copy(k).wait_send()

    return pl.pallas_call(
        body, name=name,
        in_specs=[HBM] * n, out_specs=[HBM] * n,
        out_shape=[jax.ShapeDtypeStruct((s.shape[0], s.shape[1] // 2, s.shape[2]), s.dtype) for s in parts],
        scratch_shapes=[pltpu.SemaphoreType.DMA((n,)), pltpu.SemaphoreType.DMA((n,))],
    )(*parts)


def _pair_add(name, part, other):
    nch, rows, cols = part.shape
    half = rows // 2

    def body(p_ref, o_ref, out_ref):
        c = lax.axis_index("c")
        mine = p_ref[0, pl.ds(pl.multiple_of(c * half, 16), half), :]
        out_ref[0] = (mine.astype(F32) + o_ref[0].astype(F32)).astype(out_ref.dtype)

    return pl.pallas_call(
        body, name=name, grid=(nch,),
        in_specs=[pl.BlockSpec((1, rows, cols), lambda j: (j, 0, 0)),
                  pl.BlockSpec((1, half, cols), lambda j: (j, 0, 0))],
        out_specs=pl.BlockSpec((1, half, cols), lambda j: (j, 0, 0)),
        out_shape=jax.ShapeDtypeStruct((nch, half, cols), part.dtype),
        compiler_params=_params("arbitrary"),
    )(part, other)


def _reduce_sems(n):
    return [pltpu.SemaphoreType.DMA((3 * n,)), pltpu.SemaphoreType.DMA((3 * n,)), pltpu.SemaphoreType.DMA((n,)),
            pltpu.SemaphoreType.DMA((n,)), pltpu.SemaphoreType.DMA((n,))]


def _reduce_exchange(ins, got, sib, sems):
    n = len(ins)
    send, recv, local, d2d_send, d2d_recv = sems
    x, y, c = _place()
    me = 2 * x + y
    chips = _other_chips(x, y)

    def own(k):
        return pltpu.make_async_copy(ins[k].at[me], got[k].at[me], local.at[k])

    def copy(j, k, shard, slot):
        return pltpu.make_async_remote_copy(
            src_ref=ins[k].at[shard], dst_ref=got[k].at[slot], send_sem=send.at[j * n + k],
            recv_sem=recv.at[j * n + k], device_id=(chips[j][0], chips[j][1], c), device_id_type=MESH)

    def swap(k):
        return pltpu.make_async_remote_copy(
            src_ref=got[k], dst_ref=sib[k], send_sem=d2d_send.at[k], recv_sem=d2d_recv.at[k],
            device_id=(x, y, 1 - c), device_id_type=MESH)

    def start():
        for k in range(n):
            own(k).start()
        for j in range(3):
            for k in range(n):
                copy(j, k, 2 * chips[j][0] + chips[j][1], me).start()

    def finish():
        for k in range(n):
            own(k).wait()
            for j in range(3):
                copy(j, k, me, 2 * chips[j][0] + chips[j][1]).wait_recv()
            swap(k).start()
        for k in range(n):
            swap(k).wait_recv()
        for j in range(3):
            for k in range(n):
                copy(j, k, me, me).wait_send()
        for k in range(n):
            swap(k).wait_send()

    return start, finish


def _reduce_shards(name, parts):
    n = len(parts)

    def body(*refs):
        start, finish = _reduce_exchange(refs[:n], refs[n:2 * n], refs[2 * n:3 * n], refs[3 * n:])
        start()
        finish()

    return pl.pallas_call(
        body, name=name,
        in_specs=[HBM] * n, out_specs=[HBM] * (2 * n),
        out_shape=[jax.ShapeDtypeStruct(s.shape, s.dtype) for s in parts] * 2,
        scratch_shapes=_reduce_sems(n),
    )(*parts)


def _all_reduce_small(name, vec):
    rows = vec.shape[0]

    def body(v_ref, o_ref, buf, send, recv):
        x, y, c = _place()
        me = 4 * x + 2 * y + c
        peers = [(x ^ (r >> 2), y ^ ((r >> 1) & 1), c ^ (r & 1)) for r in range(1, N_DEV)]

        def copy(r, slot):
            px, py, pc = peers[r]
            return pltpu.make_async_remote_copy(
                src_ref=v_ref, dst_ref=buf.at[slot], send_sem=send.at[r], recv_sem=recv.at[r],
                device_id=(px, py, pc), device_id_type=MESH)

        sent = [copy(r, me) for r in range(N_DEV - 1)]
        for cp in sent:
            cp.start()
        buf[me] = v_ref[...]
        for r in range(N_DEV - 1):
            px, py, pc = peers[r]
            copy(r, 4 * px + 2 * py + pc).wait_recv()
        total = buf[0]
        for dev in range(1, N_DEV):
            total = total + buf[dev]
        o_ref[...] = total
        for cp in sent:
            cp.wait_send()

    return pl.pallas_call(
        body, name=name,
        in_specs=[pl.BlockSpec(memory_space=pltpu.VMEM)], out_specs=pl.BlockSpec(memory_space=pltpu.VMEM),
        out_shape=jax.ShapeDtypeStruct(vec.shape, F32),
        scratch_shapes=[pltpu.VMEM((N_DEV, rows, 128), F32),
                        pltpu.SemaphoreType.DMA((N_DEV - 1,)), pltpu.SemaphoreType.DMA((N_DEV - 1,))],
        compiler_params=pltpu.CompilerParams(vmem_limit_bytes=VMEM_LIMIT),
    )(vec)


def _adamw(w, g, m, v):
    m = ADAM_B1 * m + (1.0 - ADAM_B1) * g
    v = ADAM_B2 * v + (1.0 - ADAM_B2) * (g * g)
    m_hat = m / (1.0 - ADAM_B1 ** ADAM_STEP)
    v_hat = v / (1.0 - ADAM_B2 ** ADAM_STEP)
    return -ADAM_LR * (m_hat / (jnp.sqrt(v_hat) + ADAM_EPS) + ADAM_WD * w), m, v


def _adamw_shard(name, core, w, m, v, got, sib):
    rows, cols = w.shape
    tr = rows // 4
    spec = pl.BlockSpec((tr, cols), lambda i, c_ref: (i, 0))
    spec4 = pl.BlockSpec((N_CHIPS, tr, cols), lambda i, c_ref: (0, i % 2, 0))

    def body(c_ref, w_ref, m_ref, v_ref, got_ref, sib_ref, g_ref, d_ref, mo_ref, vo_ref):
        def four(ref):
            return ((ref[0].astype(F32) + ref[1].astype(F32)) + ref[2].astype(F32)) + ref[3].astype(F32)

        g = jnp.where(pl.program_id(0) // 2 == c_ref[0], four(got_ref), four(sib_ref))
        g_ref[...] = g
        d_ref[...], mo_ref[...], vo_ref[...] = _adamw(w_ref[...], g, m_ref[...], v_ref[...])

    return pl.pallas_call(
        body, name=name,
        grid_spec=pltpu.PrefetchScalarGridSpec(
            num_scalar_prefetch=1, grid=(4,),
            in_specs=[spec, spec, spec, spec4, spec4], out_specs=[spec] * 4),
        out_shape=[jax.ShapeDtypeStruct((rows, cols), F32)] * 4,
        compiler_params=_params("arbitrary"),
    )(core, w, m, v, got, sib)


def _adamw_small(name, w, m, v, g):
    def body(w_ref, m_ref, v_ref, g_ref, d_ref, mo_ref, vo_ref):
        d_ref[...], mo_ref[...], vo_ref[...] = _adamw(w_ref[...], g_ref[...], m_ref[...], v_ref[...])

    return pl.pallas_call(body, name=name, out_shape=[jax.ShapeDtypeStruct(w.shape, F32)] * 3)(w, m, v, g)


def _cast_bf16(name, arrays):
    n = len(arrays)

    def body(*refs):
        for i_ref, o_ref in zip(refs[:n], refs[n:]):
            o_ref[...] = i_ref[...].astype(BF16)

    return pl.pallas_call(
        body, name=name, out_shape=[jax.ShapeDtypeStruct(a.shape, BF16) for a in arrays],
        compiler_params=pltpu.CompilerParams(vmem_limit_bytes=VMEM_LIMIT),
    )(*arrays)


def _pack(arrays, rows):
    flat = jnp.concatenate([a.reshape(-1) for a in arrays])
    return jnp.concatenate([flat, jnp.zeros((rows * 128 - flat.shape[0],), F32)]).reshape(rows, 128)


def _unpack(packed, shapes):
    flat, out, at = packed.reshape(-1), [], 0
    for s in shapes:
        size = 1
        for dim in s:
            size *= dim
        out.append(flat[at:at + size].reshape(s))
        at += size
    return out


def _rows_for(shapes):
    total = 0
    for s in shapes:
        size = 1
        for dim in s:
            size *= dim
        total += size
    return -(-total // 1024) * 8


WEIGHTS = ['meta_tokens', 'ffn1_norm', 'ffn1_w_gate', 'ffn1_w_up', 'ffn1_w_down', 'mix_norm', 'w_in', 'rwkv_mu',
           'rwkv_w0', 'rwkv_w_up', 'rwkv_a0', 'rwkv_a_up', 'rwkv_g_up', 'rwkv_k_k', 'rwkv_k_a', 'rwkv_r_k',
           'rwkv_lnx_w', 'rwkv_lnx_b', 'w_out', 'ffn2_norm', 'ffn2_w_gate', 'ffn2_w_up', 'ffn2_w_down', 'final_norm']
COL_CUT = ['ffn1_w_gate', 'ffn1_w_up', 'w_in', 'ffn2_w_gate', 'ffn2_w_up']
ROW_CUT = ['ffn1_w_down', 'w_out', 'ffn2_w_down']
SMALL_CUT = ['meta_tokens', 'rwkv_w_up', 'rwkv_a_up', 'rwkv_g_up']
BIG = COL_CUT + ROW_CUT
REPLICATED = [n for n in WEIGHTS if n not in BIG + SMALL_CUT]


def _join_cols(a):
    return a.transpose(1, 0, 2).reshape(a.shape[1], N_CHIPS * a.shape[2])


def _cut_cols(a):
    return a.reshape(a.shape[0], N_CHIPS, a.shape[1] // N_CHIPS).transpose(1, 0, 2)


def _step(x, loss_target, w, m, v):
    two = lambda a: a.reshape(a.shape[-2], a.shape[-1])

    def join(names, gathered):
        return {n: (_join_cols(a) if n in COL_CUT + SMALL_CUT else a.reshape(-1, a.shape[-1]))
                for n, a in zip(names, gathered)}

    def pair_sums(tag, names, g):
        parts = [_cut_cols(g[n]) if n in COL_CUT else g[n].reshape(N_CHIPS, -1, g[n].shape[-1]) for n in names]
        arrived = _pair_exchange("pair_exchange_" + tag, parts)
        return [_pair_add("pair_add_" + n, p, o) for n, p, o in zip(names, parts, arrived)]

    first = [n for n in BIG if n not in LATE]
    cast = dict(zip(BIG, _cast_bf16("cast_weights", [two(w[n]) for n in BIG])))
    names = first + SMALL_CUT
    shards = [cast[n] for n in first] + [two(w[n]) for n in SMALL_CUT]
    full = {n: (two(w[n]) if w[n].ndim == 3 else w[n]) for n in REPLICATED}
    full.update(join(names, _gather_shards("gather_weights", shards)))
    full["rwkv_r_k"] = w["rwkv_r_k"]
    full["final_norm"] = w["final_norm"]

    late = types.SimpleNamespace(shards=[cast[n] for n in LATE],
                                 join=lambda gathered: join(LATE, gathered),
                                 parts=lambda g: pair_sums("early", LATE, g))

    loss, dx, g, reduced_early = _local_step(x[0], loss_target[0], full, late)
    loss = lax.psum(loss, ("x", "y", "c"))

    reduced = list(_reduce_shards("reduce_gradients", pair_sums("late", first, g)))
    got = dict(zip(first + LATE, reduced[:len(first)] + reduced_early[:len(LATE)]))
    sib = dict(zip(first + LATE, reduced[len(first):] + reduced_early[len(LATE):]))

    small_names = REPLICATED + SMALL_CUT
    small_shapes = [g[n].shape for n in small_names]
    small = _all_reduce_small("reduce_small", _pack([g[n] for n in small_names], _rows_for(small_shapes)))
    g_small = dict(zip(small_names, _unpack(small, small_shapes)))
    chip = 2 * lax.axis_index("x") + lax.axis_index("y")
    for n in SMALL_CUT:
        width = g_small[n].shape[1] // N_CHIPS
        g_small[n] = lax.dynamic_slice_in_dim(g_small[n], chip * width, width, axis=1)

    grad, delta, new_m, new_v = {}, {}, {}, {}
    core = lax.axis_index("c").astype(jnp.int32).reshape(1)
    for n in BIG:
        outs = _adamw_shard("adamw_" + n, core, two(w[n]), two(m[n]), two(v[n]), got[n], sib[n])
        grad[n], delta[n], new_m[n], new_v[n] = (o.reshape(w[n].shape) for o in outs)
    shapes = [w[n].shape for n in small_names]
    rows = _rows_for(shapes)
    packed = [_pack([t[n] for n in small_names], rows) for t in (w, m, v)]
    g_packed = _pack([g_small[n] for n in small_names], rows)
    outs = [_unpack(o, shapes) for o in _adamw_small("adamw_small", *packed, g_packed)]
    for i, n in enumerate(small_names):
        grad[n] = g_small[n].reshape(w[n].shape)
        delta[n], new_m[n], new_v[n] = outs[0][i], outs[1][i], outs[2][i]
    return loss, dx[None], grad, delta, new_m, new_v


def kernel(x, meta_tokens, ffn1_norm, ffn1_w_gate, ffn1_w_up, ffn1_w_down, mix_norm, w_in, rwkv_mu, rwkv_w0, rwkv_w_up, rwkv_a0, rwkv_a_up, rwkv_g_up, rwkv_k_k, rwkv_k_a, rwkv_r_k, rwkv_lnx_w, rwkv_lnx_b, w_out, ffn2_norm, ffn2_w_gate, ffn2_w_up, ffn2_w_down, final_norm, loss_target, m_meta_tokens, m_ffn1_norm, m_ffn1_w_gate, m_ffn1_w_up, m_ffn1_w_down, m_mix_norm, m_w_in, m_rwkv_mu, m_rwkv_w0, m_rwkv_w_up, m_rwkv_a0, m_rwkv_a_up, m_rwkv_g_up, m_rwkv_k_k, m_rwkv_k_a, m_rwkv_r_k, m_rwkv_lnx_w, m_rwkv_lnx_b, m_w_out, m_ffn2_norm, m_ffn2_w_gate, m_ffn2_w_up, m_ffn2_w_down, m_final_norm, v_meta_tokens, v_ffn1_norm, v_ffn1_w_gate, v_ffn1_w_up, v_ffn1_w_down, v_mix_norm, v_w_in, v_rwkv_mu, v_rwkv_w0, v_rwkv_w_up, v_rwkv_a0, v_rwkv_a_up, v_rwkv_g_up, v_rwkv_k_k, v_rwkv_k_a, v_rwkv_r_k, v_rwkv_lnx_w, v_rwkv_lnx_b, v_w_out, v_ffn2_norm, v_ffn2_w_gate, v_ffn2_w_up, v_ffn2_w_down, v_final_norm):
    w = dict(zip(WEIGHTS, (meta_tokens, ffn1_norm, ffn1_w_gate, ffn1_w_up, ffn1_w_down, mix_norm, w_in, rwkv_mu, rwkv_w0, rwkv_w_up, rwkv_a0, rwkv_a_up, rwkv_g_up, rwkv_k_k, rwkv_k_a, rwkv_r_k, rwkv_lnx_w, rwkv_lnx_b, w_out, ffn2_norm, ffn2_w_gate, ffn2_w_up, ffn2_w_down, final_norm)))
    m = dict(zip(WEIGHTS, (m_meta_tokens, m_ffn1_norm, m_ffn1_w_gate, m_ffn1_w_up, m_ffn1_w_down, m_mix_norm, m_w_in, m_rwkv_mu, m_rwkv_w0, m_rwkv_w_up, m_rwkv_a0, m_rwkv_a_up, m_rwkv_g_up, m_rwkv_k_k, m_rwkv_k_a, m_rwkv_r_k, m_rwkv_lnx_w, m_rwkv_lnx_b, m_w_out, m_ffn2_norm, m_ffn2_w_gate, m_ffn2_w_up, m_ffn2_w_down, m_final_norm)))
    v = dict(zip(WEIGHTS, (v_meta_tokens, v_ffn1_norm, v_ffn1_w_gate, v_ffn1_w_up, v_ffn1_w_down, v_mix_norm, v_w_in, v_rwkv_mu, v_rwkv_w0, v_rwkv_w_up, v_rwkv_a0, v_rwkv_a_up, v_rwkv_g_up, v_rwkv_k_k, v_rwkv_k_a, v_rwkv_r_k, v_rwkv_lnx_w, v_rwkv_lnx_b, v_w_out, v_ffn2_norm, v_ffn2_w_gate, v_ffn2_w_up, v_ffn2_w_down, v_final_norm)))
    loss, grad_x, grad, delta, new_m, new_v = _step(x, loss_target, w, m, v)
    return (loss, grad_x, *[grad[n] for n in WEIGHTS], *[delta[n] for n in WEIGHTS],
            *[new_m[n] for n in WEIGHTS], *[new_v[n] for n in WEIGHTS])
```

```python
import functools
import types

import jax
import jax.numpy as jnp
from jax import lax
from jax.experimental import pallas as pl
from jax.experimental.pallas import tpu as pltpu

F32 = jnp.float32
BF16 = jnp.bfloat16

RMS_EPS = 1e-6
LNX_EPS = 64e-5
N_META = 16
ROW0 = 128
META_PAD = ROW0 - N_META
HEAD = 64
N_HEADS = 8
GROUP = N_HEADS * HEAD
LORA_W, LORA_A, LORA_G = 32, 32, 96
LORA_PAD = 256
RW_COLS = 3 * GROUP + LORA_PAD
IN_COLS_PAD = 3 * GROUP + RW_COLS
ATT_BLOCK = 128
CHUNK = 64
VMEM_LIMIT = 56 * 1024 * 1024

ADAM_LR, ADAM_B1, ADAM_B2, ADAM_EPS, ADAM_WD, ADAM_STEP = 0.001, 0.9, 0.999, 1e-08, 0.01, 10

MESH = pl.DeviceIdType.MESH


def _params(*sem):
    return pltpu.CompilerParams(dimension_semantics=tuple(sem), vmem_limit_bytes=VMEM_LIMIT)


def _dot(a, b):
    return lax.dot_general(a, b, (((1,), (0,)), ((), ())), preferred_element_type=F32)


def _dot_nt(a, b):
    return lax.dot_general(a, b, (((1,), (1,)), ((), ())), preferred_element_type=F32)


def _dot_tn(a, b):
    return lax.dot_general(a, b, (((0,), (0,)), ((), ())), preferred_element_type=F32)


def _split2(x):
    hi = x.astype(BF16)
    return hi, (x - hi.astype(F32)).astype(BF16)


def _sigmoid(x):
    return 1.0 / (1.0 + jnp.exp(-x))


def _rms_fwd(x, g):
    rstd = lax.rsqrt(jnp.mean(x * x, axis=-1, keepdims=True) + RMS_EPS)
    xhat = x * rstd
    return xhat * g, xhat, rstd


def _rms_bwd(dn, xhat, rstd, g):
    dxhat = dn * g
    dx = rstd * (dxhat - xhat * jnp.mean(dxhat * xhat, axis=-1, keepdims=True))
    return dx, jnp.sum(dn * xhat, axis=0, keepdims=True)


def _row_tile(rows):
    return 384 if rows % 384 == 0 else 128


def _half_tile(cols):
    return cols // 2 if cols % 256 == 0 else cols


def _tall_tile(rows, parts):
    return rows // parts if rows % (16 * parts) == 0 else _row_tile(rows)


def _call_with_exchange(name, body, grid, in_specs, out_specs, out_shape, scratch, operands, params, exchange):
    if exchange is None or not exchange[1]:
        return pl.pallas_call(body, name=name, grid=grid, in_specs=in_specs, out_specs=out_specs,
                              out_shape=out_shape, scratch_shapes=scratch, compiler_params=params)(*operands)
    kind, arrays = exchange
    ns, n_in, n_out, n_scr = len(arrays), len(in_specs), len(out_specs), len(scratch)
    if kind == "gather":
        results = [jax.ShapeDtypeStruct((N_CHIPS,) + s.shape, s.dtype) for s in arrays]
        sems = _gather_sems(ns)
    else:
        results = [jax.ShapeDtypeStruct(s.shape, s.dtype) for s in arrays] * 2
        sems = _reduce_sems(ns)
    n_res = len(results)

    def carried(*refs):
        at = n_in + ns + n_out
        sent, landed = refs[n_in:n_in + ns], refs[at:at + n_res]
        own_scratch, sem_refs = refs[at + n_res:at + n_res + n_scr], refs[at + n_res + n_scr:]
        first, last = _first_and_last_step(grid)
        if kind == "gather":
            start, finish = _gather_exchange(sent, landed, sem_refs)
        else:
            start, finish = _reduce_exchange(sent, landed[:ns], landed[ns:], sem_refs)
        pl.when(first)(start)
        body(*refs[:n_in], *refs[n_in + ns:at], *own_scratch)
        pl.when(last)(finish)

    return pl.pallas_call(
        carried, name=name, grid=grid, in_specs=list(in_specs) + [HBM] * ns, out_specs=list(out_specs) + [HBM] * n_res,
        out_shape=list(out_shape) + results, scratch_shapes=list(scratch) + sems, compiler_params=params,
    )(*operands, *arrays)


def _ffn_fwd(name, h, g, wg, wu, wd, exchange=None):
    rows, d = h.shape
    f = wg.shape[1]
    tm, tf = _row_tile(rows), _half_tile(f)
    nj = f // tf

    def body(h_ref, g_ref, wg_ref, wu_ref, wd_ref, ho_ref, a_ref, b_ref, n_sc, acc_sc):
        j = pl.program_id(1)

        @pl.when(j == 0)
        def _():
            n, _, _ = _rms_fwd(h_ref[...], g_ref[...])
            n_sc[...] = n.astype(BF16)
            acc_sc[...] = jnp.zeros_like(acc_sc)

        n = n_sc[...]
        a = _dot(n, wg_ref[...])
        b = _dot(n, wu_ref[...])
        a_ref[...] = a
        b_ref[...] = b
        s = a * _sigmoid(a) * b
        acc_sc[...] += _dot(s.astype(BF16), wd_ref[...])

        @pl.when(j == nj - 1)
        def _():
            ho_ref[...] = h_ref[...] + 0.5 * acc_sc[...]

    return _call_with_exchange(
        name, body, (rows // tm, nj),
        [pl.BlockSpec((tm, d), lambda i, j: (i, 0)),
         pl.BlockSpec((1, d), lambda i, j: (0, 0)),
         pl.BlockSpec((d, tf), lambda i, j: (0, j)),
         pl.BlockSpec((d, tf), lambda i, j: (0, j)),
         pl.BlockSpec((tf, d), lambda i, j: (j, 0))],
        [pl.BlockSpec((tm, d), lambda i, j: (i, 0)),
         pl.BlockSpec((tm, tf), lambda i, j: (i, j)),
         pl.BlockSpec((tm, tf), lambda i, j: (i, j))],
        [jax.ShapeDtypeStruct((rows, d), F32),
         jax.ShapeDtypeStruct((rows, f), F32),
         jax.ShapeDtypeStruct((rows, f), F32)],
        [pltpu.VMEM((tm, d), BF16), pltpu.VMEM((tm, d), F32)],
        (h, g, wg, wu, wd), _params("arbitrary", "arbitrary"), exchange)


def _ffn_bwd(name, dh, h, g, a, b, wg, wu, wd, exchange=None):
    rows, d = h.shape
    f = wg.shape[1]
    tm, tf = _row_tile(rows), _half_tile(f)
    ni, nj = rows // tm, f // tf

    def body(dh_ref, h_ref, g_ref, a_ref, b_ref, wg_ref, wu_ref, wd_ref,
             dhi_ref, da_ref, db_ref, s_ref, n_ref, dhh_ref, dg_ref, dn_sc):
        i, j = pl.program_id(0), pl.program_id(1)

        @pl.when(j == 0)
        def _():
            n, _, _ = _rms_fwd(h_ref[...], g_ref[...])
            n_ref[...] = n.astype(BF16)
            dhh_ref[...] = (0.5 * dh_ref[...]).astype(BF16)
            dn_sc[...] = jnp.zeros_like(dn_sc)

        @pl.when((i == 0) & (j == 0))
        def _():
            dg_ref[...] = jnp.zeros_like(dg_ref)

        ds = _dot_nt(dhh_ref[...], wd_ref[...])
        av, bv = a_ref[...], b_ref[...]
        sig = _sigmoid(av)
        silu = av * sig
        s_ref[...] = (silu * bv).astype(BF16)
        db = (ds * silu).astype(BF16)
        da = (ds * bv * (sig * (1.0 + av * (1.0 - sig)))).astype(BF16)
        da_ref[...] = da
        db_ref[...] = db
        dn_sc[...] += _dot_nt(da, wg_ref[...]) + _dot_nt(db, wu_ref[...])

        @pl.when(j == nj - 1)
        def _():
            gv = g_ref[...]
            _, xhat, rstd = _rms_fwd(h_ref[...], gv)
            dx, dg = _rms_bwd(dn_sc[...], xhat, rstd, gv)
            dhi_ref[...] = dh_ref[...] + dx
            dg_ref[...] += dg

    return _call_with_exchange(
        name, body, (ni, nj),
        [pl.BlockSpec((tm, d), lambda i, j: (i, 0)),
         pl.BlockSpec((tm, d), lambda i, j: (i, 0)),
         pl.BlockSpec((1, d), lambda i, j: (0, 0)),
         pl.BlockSpec((tm, tf), lambda i, j: (i, j)),
         pl.BlockSpec((tm, tf), lambda i, j: (i, j)),
         pl.BlockSpec((d, tf), lambda i, j: (0, j)),
         pl.BlockSpec((d, tf), lambda i, j: (0, j)),
         pl.BlockSpec((tf, d), lambda i, j: (j, 0))],
        [pl.BlockSpec((tm, d), lambda i, j: (i, 0)),
         pl.BlockSpec((tm, tf), lambda i, j: (i, j)),
         pl.BlockSpec((tm, tf), lambda i, j: (i, j)),
         pl.BlockSpec((tm, tf), lambda i, j: (i, j)),
         pl.BlockSpec((tm, d), lambda i, j: (i, 0)),
         pl.BlockSpec((tm, d), lambda i, j: (i, 0)),
         pl.BlockSpec((1, d), lambda i, j: (0, 0))],
        [jax.ShapeDtypeStruct((rows, d), F32),
         jax.ShapeDtypeStruct((rows, f), BF16),
         jax.ShapeDtypeStruct((rows, f), BF16),
         jax.ShapeDtypeStruct((rows, f), BF16),
         jax.ShapeDtypeStruct((rows, d), BF16),
         jax.ShapeDtypeStruct((rows, d), BF16),
         jax.ShapeDtypeStruct((1, d), F32)],
        [pltpu.VMEM((tm, d), F32)],
        (dh, h, g, a, b, wg, wu, wd), _params("arbitrary", "arbitrary"), exchange)


def _mm_tn(name, a, b):
    k, m = a.shape
    n = b.shape[1]
    tk = _tall_tile(k, 3)
    tm = _half_tile(m) if m > 1024 else m
    tn = _half_tile(n) if n > 1024 else n
    nk = k // tk

    def body(a_ref, b_ref, o_ref, acc):
        kk = pl.program_id(2)

        @pl.when(kk == 0)
        def _():
            acc[...] = jnp.zeros_like(acc)

        acc[...] += _dot_tn(a_ref[...], b_ref[...])

        @pl.when(kk == nk - 1)
        def _():
            o_ref[...] = acc[...].astype(BF16)

    return pl.pallas_call(
        body, name=name, grid=(m // tm, n // tn, nk),
        in_specs=[pl.BlockSpec((tk, tm), lambda i, j, kk: (kk, i)),
                  pl.BlockSpec((tk, tn), lambda i, j, kk: (kk, j))],
        out_specs=pl.BlockSpec((tm, tn), lambda i, j, kk: (i, j)),
        out_shape=jax.ShapeDtypeStruct((m, n), BF16),
        scratch_shapes=[pltpu.VMEM((tm, tn), F32)],
        compiler_params=_params("arbitrary", "arbitrary", "arbitrary"),
    )(a, b)


def _norm_proj(name, h, g, w):
    rows, d = h.shape
    n = w.shape[1]
    split = 3 * GROUP
    tm = _row_tile(rows)

    def body(h_ref, g_ref, w_ref, qkv_ref, p_ref, n_ref):
        nv, _, _ = _rms_fwd(h_ref[...], g_ref[...])
        nb = nv.astype(BF16)
        n_ref[...] = nb
        qkv_ref[...] = _dot(nb, w_ref[:, :split]).astype(BF16)
        p_ref[...] = _dot(nb, w_ref[:, split:])

    return pl.pallas_call(
        body, name=name, grid=(rows // tm,),
        in_specs=[pl.BlockSpec((tm, d), lambda i: (i, 0)),
                  pl.BlockSpec((1, d), lambda i: (0, 0)),
                  pl.BlockSpec((d, n), lambda i: (0, 0))],
        out_specs=[pl.BlockSpec((tm, split), lambda i: (i, 0)),
                   pl.BlockSpec((tm, n - split), lambda i: (i, 0)),
                   pl.BlockSpec((tm, d), lambda i: (i, 0))],
        out_shape=[jax.ShapeDtypeStruct((rows, split), BF16), jax.ShapeDtypeStruct((rows, n - split), F32),
                   jax.ShapeDtypeStruct((rows, d), BF16)],
        compiler_params=_params("arbitrary"),
    )(h, g, w)


def _out_proj(name, h, sb, rw, w):
    rows, d = h.shape
    gw = sb.shape[1]
    tm = _row_tile(rows)

    def body(h_ref, sb_ref, rw_ref, w_ref, o_ref, mix_ref):
        mix_ref[:, :gw] = sb_ref[...].astype(BF16)
        mix_ref[:, gw:] = rw_ref[...].astype(BF16)
        o_ref[...] = h_ref[...] + _dot(mix_ref[...], w_ref[...])

    return pl.pallas_call(
        body, name=name, grid=(rows // tm,),
        in_specs=[pl.BlockSpec((tm, d), lambda i: (i, 0)),
                  pl.BlockSpec((tm, gw), lambda i: (i, 0)),
                  pl.BlockSpec((tm, gw), lambda i: (i, 0)),
                  pl.BlockSpec((2 * gw, d), lambda i: (0, 0))],
        out_specs=[pl.BlockSpec((tm, d), lambda i: (i, 0)),
                   pl.BlockSpec((tm, 2 * gw), lambda i: (i, 0))],
        out_shape=[jax.ShapeDtypeStruct((rows, d), F32), jax.ShapeDtypeStruct((rows, 2 * gw), BF16)],
        compiler_params=_params("arbitrary"),
    )(h, sb, rw, w)


def _out_proj_bwd(name, dh, w):
    rows, d = dh.shape
    k = w.shape[0]
    tm = _row_tile(rows)

    def body(dh_ref, w_ref, dsb_ref, drw_ref, dhb_ref):
        dhb = dh_ref[...].astype(BF16)
        dhb_ref[...] = dhb
        dsb_ref[...] = _dot_nt(dhb, w_ref[:GROUP, :]).astype(BF16)
        drw_ref[...] = _dot_nt(dhb, w_ref[GROUP:, :])

    return pl.pallas_call(
        body, name=name, grid=(rows // tm,),
        in_specs=[pl.BlockSpec((tm, d), lambda i: (i, 0)),
                  pl.BlockSpec((k, d), lambda i: (0, 0))],
        out_specs=[pl.BlockSpec((tm, GROUP), lambda i: (i, 0)),
                   pl.BlockSpec((tm, GROUP), lambda i: (i, 0)),
                   pl.BlockSpec((tm, d), lambda i: (i, 0))],
        out_shape=[jax.ShapeDtypeStruct((rows, GROUP), BF16), jax.ShapeDtypeStruct((rows, GROUP), F32),
                   jax.ShapeDtypeStruct((rows, d), BF16)],
        compiler_params=_params("arbitrary"),
    )(dh, w)


def _norm_proj_bwd(name, dproj, w, h, g, dh):
    rows, n = dproj.shape
    d = w.shape[0]
    tm = _row_tile(rows)

    def body(dp_ref, w_ref, h_ref, g_ref, dh_ref, o_ref, dg_ref):
        @pl.when(pl.program_id(0) == 0)
        def _():
            dg_ref[...] = jnp.zeros_like(dg_ref)

        dn = _dot_nt(dp_ref[...], w_ref[...])
        gv = g_ref[...]
        _, xhat, rstd = _rms_fwd(h_ref[...], gv)
        dx, dg = _rms_bwd(dn, xhat, rstd, gv)
        o_ref[...] = dh_ref[...] + dx
        dg_ref[...] += dg

    return pl.pallas_call(
        body, name=name, grid=(rows // tm,),
        in_specs=[pl.BlockSpec((tm, n), lambda i: (i, 0)),
                  pl.BlockSpec((d, n), lambda i: (0, 0)),
                  pl.BlockSpec((tm, d), lambda i: (i, 0)),
                  pl.BlockSpec((1, d), lambda i: (0, 0)),
                  pl.BlockSpec((tm, d), lambda i: (i, 0))],
        out_specs=[pl.BlockSpec((tm, d), lambda i: (i, 0)),
                   pl.BlockSpec((1, d), lambda i: (0, 0))],
        out_shape=[jax.ShapeDtypeStruct((rows, d), F32), jax.ShapeDtypeStruct((1, d), F32)],
        compiler_params=_params("arbitrary"),
    )(dproj, w, h, g, dh)


def _loss_head(name, h, g, tgt):
    rows, d = h.shape
    tm = _row_tile(rows)

    def body(h_ref, g_ref, t_ref, loss_ref, dh_ref, dg_ref):
        i = pl.program_id(0)

        @pl.when(i == 0)
        def _():
            loss_ref[...] = jnp.zeros_like(loss_ref)
            dg_ref[...] = jnp.zeros_like(dg_ref)

        gv = g_ref[...]
        y, xhat, rstd = _rms_fwd(h_ref[...], gv)
        row = i * tm + lax.broadcasted_iota(jnp.int32, (tm, 1), 0)
        diff = jnp.where(row >= ROW0, y - t_ref[...], 0.0)
        part = 0.5 * jnp.sum(jnp.sum(diff * diff, axis=-1, keepdims=True), axis=0, keepdims=True) / d
        loss_ref[...] += jnp.broadcast_to(part, loss_ref.shape)
        dx, dg = _rms_bwd(diff / d, xhat, rstd, gv)
        dh_ref[...] = dx
        dg_ref[...] += dg

    return pl.pallas_call(
        body, name=name, grid=(rows // tm,),
        in_specs=[pl.BlockSpec((tm, d), lambda i: (i, 0)),
                  pl.BlockSpec((1, d), lambda i: (0, 0)),
                  pl.BlockSpec((tm, d), lambda i: (i, 0))],
        out_specs=[pl.BlockSpec((8, 128), lambda i: (0, 0)),
                   pl.BlockSpec((tm, d), lambda i: (i, 0)),
                   pl.BlockSpec((1, d), lambda i: (0, 0))],
        out_shape=[jax.ShapeDtypeStruct((8, 128), F32),
                   jax.ShapeDtypeStruct((rows, d), F32),
                   jax.ShapeDtypeStruct((1, d), F32)],
        compiler_params=_params("arbitrary"),
    )(h, g, tgt)


def _sb_block(qb, kb, i, jb, scale):
    bq, bk = qb.shape[0], kb.shape[0]
    z = _dot_nt(qb, kb) * scale
    qpos = i * bq + lax.broadcasted_iota(jnp.int32, (bq, bk), 0)
    kpos = jb * bk + lax.broadcasted_iota(jnp.int32, (bq, bk), 1)
    valid = (kpos < qpos) & (kpos >= META_PAD)
    e = jnp.exp(-jnp.abs(z))
    log_keep = jnp.where(valid, -(jnp.maximum(z, 0.0) + jnp.log(1.0 + e)), 0.0)
    return z, valid, e, log_keep


def _tri2(n, cmp):
    r = lax.broadcasted_iota(jnp.int32, (2 * n, n), 0) % n
    c = lax.broadcasted_iota(jnp.int32, (2 * n, n), 1)
    return cmp(r, c).astype(BF16)


def _dot_split(x, t2):
    hi, lo = _split2(x)
    return _dot(jnp.concatenate([hi, lo], axis=1), t2)


ATT_HEADS = 128 // HEAD
ATT_CUT = -104.0
ATT_TILES = GROUP // 128


def _lanes(hh):
    return slice(hh * HEAD, (hh + 1) * HEAD)


def _first_and_last_step(grid):
    here = [pl.program_id(a) for a in range(len(grid))]
    first, last = here[0] == 0, here[0] == grid[0] - 1
    for a in range(1, len(grid)):
        first, last = first & (here[a] == 0), last & (here[a] == grid[a] - 1)
    return first, last


def _sb_fwd(name, qkv, shards=()):
    rows = qkv.shape[0]
    nh, dh = N_HEADS, HEAD
    bq, bk, hg = _row_tile(rows), ATT_BLOCK, ATT_HEADS
    per = bq // bk
    scale = dh ** -0.5
    ns = len(shards)
    grid = (nh // hg, rows // bq)

    def body(q_ref, k_ref, v_ref, *rest):
        o_ref, rt_ref, cnt_ref = rest[ns:ns + 3]
        if ns:
            first, last = _first_and_last_step(grid)
            start, finish = _gather_exchange(rest[:ns], rest[ns + 3:2 * ns + 3], rest[2 * ns + 3:])
            pl.when(first)(start)
        i = pl.program_id(1)
        after = _tri2(bk, lambda r, c: r > c)
        nkb = (i + 1) * per

        def live(state):
            n, carry = state
            top = jnp.max(carry[0][0])
            for hh in range(1, hg):
                top = jnp.maximum(top, jnp.max(carry[hh][0]))
            return (n < nkb) & (top >= ATT_CUT)

        def step(state):
            n, carry = state
            jb = nkb - 1 - n
            off = pl.multiple_of(jb * bk, bk)
            out = []
            for hh in range(hg):
                rest, acc = carry[hh]
                kb = k_ref[pl.ds(off, bk), _lanes(hh)]
                vb = v_ref[pl.ds(off, bk), _lanes(hh)]
                z, valid, _, log_keep = _sb_block(q_ref[:, _lanes(hh)], kb, i, jb, scale)
                log_rest = rest + _dot_split(log_keep, after)
                attn = jnp.where(valid, jnp.exp(z + log_keep + log_rest), 0.0)
                out.append((rest + jnp.sum(log_keep, axis=-1, keepdims=True), acc + _dot(attn.astype(BF16), vb)))
            return n + 1, tuple(out)

        init = tuple((jnp.zeros((bq, 1), F32), jnp.zeros((bq, dh), F32)) for _ in range(hg))
        n, res = lax.while_loop(live, step, (jnp.int32(0), init))
        for hh in range(hg):
            rt_ref[hh] = res[hh][0]
            o_ref[:, _lanes(hh)] = res[hh][1]
            cnt_ref[hh] = jnp.full((bq, 1), n, F32)
        if ns:
            pl.when(last)(finish)

    return pl.pallas_call(
        body, name=name, grid=grid,
        in_specs=[pl.BlockSpec((bq, 128), lambda h, i: (i, h)),
                  pl.BlockSpec((rows, 128), lambda h, i: (0, ATT_TILES + h)),
                  pl.BlockSpec((rows, 128), lambda h, i: (0, 2 * ATT_TILES + h))] + [HBM] * ns,
        out_specs=[pl.BlockSpec((bq, 128), lambda h, i: (i, h)),
                   pl.BlockSpec((hg, bq, 1), lambda h, i: (h, i, 0)),
                   pl.BlockSpec((hg, bq, 1), lambda h, i: (h, i, 0))] + [HBM] * ns,
        out_shape=[jax.ShapeDtypeStruct((rows, GROUP), F32), jax.ShapeDtypeStruct((nh, rows, 1), F32),
                   jax.ShapeDtypeStruct((nh, rows, 1), F32)]
        + [jax.ShapeDtypeStruct((N_CHIPS,) + s.shape, s.dtype) for s in shards],
        scratch_shapes=_gather_sems(ns) if ns else [],
        compiler_params=_params("arbitrary", "arbitrary"),
    )(qkv, qkv, qkv, *shards)


def _sb_bwd(name, qkv, rt, cnt, do, parts=()):
    rows = qkv.shape[0]
    nh, dh = N_HEADS, HEAD
    bq, bk, hg = _row_tile(rows), ATT_BLOCK, ATT_HEADS
    per = bq // bk
    scale = dh ** -0.5
    ns = len(parts)
    grid = (nh // hg, rows // bq)

    def body(q_ref, k_ref, v_ref, rt_ref, cnt_ref, do_ref, *rest):
        dq_ref, dk_ref, dv_ref = rest[ns:ns + 3]
        if ns:
            at_first, at_last = _first_and_last_step(grid)
            start, finish = _reduce_exchange(rest[:ns], rest[ns + 3:2 * ns + 3], rest[2 * ns + 3:3 * ns + 3],
                                             rest[3 * ns + 3:])
            pl.when(at_first)(start)
        i = pl.program_id(1)

        @pl.when(i == 0)
        def _():
            dk_ref[...] = jnp.zeros_like(dk_ref)
            dv_ref[...] = jnp.zeros_like(dv_ref)

        upto = _tri2(bk, lambda r, c: r <= c)
        before = _tri2(bk, lambda r, c: r < c)
        nkb = (i + 1) * per
        first = nkb - jnp.max(cnt_ref[0]).astype(jnp.int32)

        def step(jb, carry):
            off = pl.multiple_of(jb * bk, bk)
            out = []
            for hh in range(hg):
                keep_sum, g_sum, dq = carry[hh]
                qb, dob = q_ref[:, _lanes(hh)], do_ref[:, _lanes(hh)]
                kb = k_ref[pl.ds(off, bk), _lanes(hh)]
                vb = v_ref[pl.ds(off, bk), _lanes(hh)]
                z, valid, e, log_keep = _sb_block(qb, kb, i, jb, scale)
                log_rest = rt_ref[hh] - keep_sum - _dot_split(log_keep, upto)
                attn = jnp.where(valid, jnp.exp(z + log_keep + log_rest), 0.0)
                g = attn * _dot_nt(dob, vb)
                g_before = g_sum + _dot_split(g, before)
                inv = 1.0 / (1.0 + e)
                sig = jnp.where(z >= 0, inv, e * inv)
                dz = (jnp.where(valid, g * (1.0 - sig) - g_before * sig, 0.0) * scale).astype(BF16)
                dk_ref[pl.ds(off, bk), _lanes(hh)] += _dot_tn(dz, qb)
                dv_ref[pl.ds(off, bk), _lanes(hh)] += _dot_tn(attn.astype(BF16), dob)
                out.append((keep_sum + jnp.sum(log_keep, axis=-1, keepdims=True),
                            g_sum + jnp.sum(g, axis=-1, keepdims=True),
                            dq + _dot(dz, kb)))
            return tuple(out)

        zero = jnp.zeros((bq, 1), F32)
        res = lax.fori_loop(first, nkb, step, tuple((zero, zero, jnp.zeros((bq, dh), F32)) for _ in range(hg)))
        for hh in range(hg):
            dq_ref[:, _lanes(hh)] = res[hh][2]
        if ns:
            pl.when(at_last)(finish)

    return pl.pallas_call(
        body, name=name, grid=grid,
        in_specs=[pl.BlockSpec((bq, 128), lambda h, i: (i, h)),
                  pl.BlockSpec((rows, 128), lambda h, i: (0, ATT_TILES + h)),
                  pl.BlockSpec((rows, 128), lambda h, i: (0, 2 * ATT_TILES + h)),
                  pl.BlockSpec((hg, bq, 1), lambda h, i: (h, i, 0)),
                  pl.BlockSpec((hg, bq, 1), lambda h, i: (h, i, 0)),
                  pl.BlockSpec((bq, 128), lambda h, i: (i, h))] + [HBM] * ns,
        out_specs=[pl.BlockSpec((bq, 128), lambda h, i: (i, h)),
                   pl.BlockSpec((rows, 128), lambda h, i: (0, h)),
                   pl.BlockSpec((rows, 128), lambda h, i: (0, h))] + [HBM] * (2 * ns),
        out_shape=[jax.ShapeDtypeStruct((rows, GROUP), F32)] * 3
        + [jax.ShapeDtypeStruct(s.shape, s.dtype) for s in parts] * 2,
        scratch_shapes=_reduce_sems(ns) if ns else [],
        compiler_params=_params("arbitrary", "arbitrary"),
    )(qkv, qkv, qkv, rt, cnt, do, *parts)


def _head_sum(x, ones_bd):
    return _dot_split(x, ones_bd)


def _rwkv_pre(p, p_prev, mu, w0, a0, k_k, k_a, w_up, a_up, g_up, ones_bd):
    xs = p + (p_prev - p) * mu
    r = xs[:, :GROUP]
    k0 = xs[:, GROUP:2 * GROUP]
    v = xs[:, 2 * GROUP:3 * GROUP]
    lo = xs[:, 3 * GROUP:]
    wa = w0 + _dot(jnp.tanh(lo).astype(BF16), w_up.astype(BF16))
    w = -(jnp.maximum(-wa, 0.0) + jnp.log(1.0 + jnp.exp(-jnp.abs(wa)))) - 0.5
    log_decay = -jnp.exp(w)
    alpha = _sigmoid(a0 + _dot(lo.astype(BF16), a_up.astype(BF16)))
    gate = _dot(_sigmoid(lo).astype(BF16), g_up.astype(BF16))
    kk = k0 * k_k
    kk = kk * lax.rsqrt(jnp.maximum(_head_sum(kk * kk, ones_bd), 1e-24))
    k = k0 * (1.0 + (alpha - 1.0) * k_a)
    return r, log_decay, k, v, -kk, kk * alpha, gate


def _rwkv_post(y, r, k, v, gate, lnx_w, lnx_b, r_k, ones_bd):
    mean = _head_sum(y, ones_bd) * (1.0 / HEAD)
    yc = y - mean
    var = _head_sum(yc * yc, ones_bd) * (1.0 / HEAD)
    yn = yc * lax.rsqrt(var + LNX_EPS) * lnx_w + lnx_b
    bonus = _head_sum(r * k * r_k, ones_bd) * v
    return (yn + bonus) * gate


TOKEN_TILE = 128
_PRE_VEC = 5
_PRE_MAT = 3


def _split_heads(o_ref, val):
    for h in range(N_HEADS):
        o_ref[h] = val[:, _lanes(h)]


def _merge_heads(ref):
    return jnp.concatenate([ref[h] for h in range(N_HEADS)], axis=1)


def _head_spec(tm):
    return pl.BlockSpec((N_HEADS, tm, HEAD), lambda i: (0, i, 0))


def _rwkv_pre_fwd(name, p, p_prev, vecs, mats, ones_bd):
    rows = p.shape[0]
    tm = TOKEN_TILE
    row_spec = lambda w: pl.BlockSpec((tm, w), lambda i: (i, 0))
    full = lambda a: pl.BlockSpec(a.shape, lambda i: (0,) * a.ndim)

    def body(p_ref, pp_ref, *refs):
        ins = [r[...] for r in refs[:_PRE_VEC + _PRE_MAT + 1]]
        outs = refs[_PRE_VEC + _PRE_MAT + 1:]
        vals = _rwkv_pre(p_ref[...], pp_ref[...], *ins)
        for o_ref, val in zip(outs[:6], vals[:6]):
            _split_heads(o_ref, val)
        for o_ref, val in zip(outs[6:], (vals[0], vals[2], vals[3], vals[6])):
            o_ref[...] = val

    return pl.pallas_call(
        body, name=name, grid=(rows // tm,),
        in_specs=[row_spec(RW_COLS), row_spec(RW_COLS)] + [full(a) for a in (*vecs, *mats, ones_bd)],
        out_specs=[_head_spec(tm)] * 6 + [row_spec(GROUP)] * 4,
        out_shape=([jax.ShapeDtypeStruct((N_HEADS, rows, HEAD), F32)] * 6
                   + [jax.ShapeDtypeStruct((rows, GROUP), F32)] * 4),
        compiler_params=_params("arbitrary"),
    )(p, p_prev, *vecs, *mats, ones_bd)


def _rwkv_pre_bwd(name, p, p_prev, vecs, mats, ones_bd, cts_scan, ct_gate, cts_b):
    rows = p.shape[0]
    tm = TOKEN_TILE
    n_par = _PRE_VEC + _PRE_MAT
    row_spec = lambda w: pl.BlockSpec((tm, w), lambda i: (i, 0))
    full = lambda a: pl.BlockSpec(a.shape, lambda i: (0,) * a.ndim)

    def body(*refs):
        p_ref, pp_ref = refs[0], refs[1]
        par = [r[...] for r in refs[2:2 + n_par]]
        ones = refs[2 + n_par][...]
        cta = [_merge_heads(r) for r in refs[3 + n_par:9 + n_par]] + [refs[9 + n_par][...]]
        ctb = [r[...] for r in refs[10 + n_par:13 + n_par]]
        outs = refs[13 + n_par:]
        ct = (cta[0] + ctb[0], cta[1], cta[2] + ctb[1], cta[3] + ctb[2], cta[4], cta[5], cta[6])
        _, vjp = jax.vjp(lambda pv, ppv, *pr: _rwkv_pre(pv, ppv, *pr, ones), p_ref[...], pp_ref[...], *par)
        grads = vjp(ct)
        outs[0][...] = grads[0]
        outs[1][...] = grads[1]

        @pl.when(pl.program_id(0) == 0)
        def _():
            for o_ref in outs[2:]:
                o_ref[...] = jnp.zeros_like(o_ref)

        for o_ref, gval in zip(outs[2:], grads[2:]):
            o_ref[...] += gval

    par_arrays = (*vecs, *mats)
    return pl.pallas_call(
        body, name=name, grid=(rows // tm,),
        in_specs=([row_spec(RW_COLS)] * 2 + [full(a) for a in (*par_arrays, ones_bd)]
                  + [_head_spec(tm)] * 6 + [row_spec(GROUP)] * 4),
        out_specs=[row_spec(RW_COLS)] * 2 + [full(a) for a in par_arrays],
        out_shape=([jax.ShapeDtypeStruct((rows, RW_COLS), F32)] * 2
                   + [jax.ShapeDtypeStruct(a.shape, F32) for a in par_arrays]),
        compiler_params=_params("arbitrary"),
    )(p, p_prev, *par_arrays, ones_bd, *cts_scan, ct_gate, *cts_b)


def _rwkv_post_fwd(name, y, r, k, v, gate, vecs, ones_bd):
    rows = r.shape[0]
    tm = TOKEN_TILE
    row_spec = pl.BlockSpec((tm, GROUP), lambda i: (i, 0))
    full = lambda a: pl.BlockSpec(a.shape, lambda i: (0,) * a.ndim)

    def body(y_ref, *refs):
        vals = [r_[...] for r_ in refs[:-1]]
        refs[-1][...] = _rwkv_post(_merge_heads(y_ref), *vals)

    return pl.pallas_call(
        body, name=name, grid=(rows // tm,),
        in_specs=[_head_spec(tm)] + [row_spec] * 4 + [full(a) for a in (*vecs, ones_bd)],
        out_specs=row_spec,
        out_shape=jax.ShapeDtypeStruct((rows, GROUP), F32),
        compiler_params=_params("arbitrary"),
    )(y, r, k, v, gate, *vecs, ones_bd)


def _rwkv_post_bwd(name, y, r, k, v, gate, vecs, ones_bd, dout):
    rows = r.shape[0]
    tm = TOKEN_TILE
    row_spec = pl.BlockSpec((tm, GROUP), lambda i: (i, 0))
    full = lambda a: pl.BlockSpec(a.shape, lambda i: (0,) * a.ndim)

    def body(y_ref, *refs):
        vals = [_merge_heads(y_ref)] + [r_[...] for r_ in refs[:7]]
        ones = refs[7][...]
        dout_v = refs[8][...]
        outs = refs[9:]
        _, vjp = jax.vjp(lambda *a: _rwkv_post(*a, ones), *vals)
        grads = vjp(dout_v)
        _split_heads(outs[0], grads[0])
        for o_ref, gval in zip(outs[1:5], grads[1:5]):
            o_ref[...] = gval

        @pl.when(pl.program_id(0) == 0)
        def _():
            for o_ref in outs[5:]:
                o_ref[...] = jnp.zeros_like(o_ref)

        for o_ref, gval in zip(outs[5:], grads[5:]):
            o_ref[...] += gval

    return pl.pallas_call(
        body, name=name, grid=(rows // tm,),
        in_specs=[_head_spec(tm)] + [row_spec] * 4 + [full(a) for a in (*vecs, ones_bd)] + [row_spec],
        out_specs=[_head_spec(tm)] + [row_spec] * 4 + [full(a) for a in vecs],
        out_shape=([jax.ShapeDtypeStruct((N_HEADS, rows, HEAD), F32)] + [jax.ShapeDtypeStruct((rows, GROUP), F32)] * 4
                   + [jax.ShapeDtypeStruct(a.shape, F32) for a in vecs]),
        compiler_params=_params("arbitrary"),
    )(y, r, k, v, gate, *vecs, ones_bd, dout)


_NN = (((2,), (1,)), ((0,), (0,)))
_NT = (((2,), (2,)), ((0,), (0,)))
_TN = (((1,), (1,)), ((0,), (0,)))


_BWD_FORMS = {"nn": (("nt", False), ("tn", False)),
              "nt": (("nn", False), ("tn", True)),
              "tn": (("nt", True), ("nn", False))}
_DIMS = {"nn": _NN, "nt": _NT, "tn": _TN}


def _bdot(a, b, form):
    return lax.dot_general(a.astype(BF16), b.astype(BF16), _DIMS[form], preferred_element_type=F32)


@functools.partial(jax.custom_vjp, nondiff_argnums=(2,))
def _bmm(a, b, form):
    return _bdot(a, b, form)


def _bmm_fwd(a, b, form):
    return _bdot(a, b, form), (a.astype(BF16), b.astype(BF16))


def _bmm_bwd(form, res, dc):
    a, b = res
    (fa, swap_a), (fb, swap_b) = _BWD_FORMS[form]
    da = _bdot(b, dc, fa) if swap_a else _bdot(dc, b, fa)
    db = _bdot(dc, a, fb) if swap_b else _bdot(a, dc, fb)
    return da, db


_bmm.defvjp(_bmm_fwd, _bmm_bwd)


@jax.custom_vjp
def _cumsum_steps(x):
    return _tri_apply(x, lambda r, c: r >= c)


def _tri_apply(x, cmp):
    nh, c, _ = x.shape
    tri = cmp(lax.broadcasted_iota(jnp.int32, (c, c), 0), lax.broadcasted_iota(jnp.int32, (c, c), 1))
    tri = jnp.broadcast_to(tri.astype(BF16)[None], (nh, c, c))
    hi, lo = _split2(x)
    return (lax.dot_general(tri, hi, _NN, preferred_element_type=F32)
            + lax.dot_general(tri, lo, _NN, preferred_element_type=F32))


_cumsum_steps.defvjp(lambda x: (_cumsum_steps(x), None), lambda _, d: (_tri_apply(d, lambda r, c: r <= c),))


def _chunk(state, r, log_w, k, v, a, b):
    nh, c, _ = r.shape
    row = lax.broadcasted_iota(jnp.int32, (c, c), 0)
    col = lax.broadcasted_iota(jnp.int32, (c, c), 1)
    cum = _cumsum_steps(log_w)
    mid = cum[:, c // 2 - 1:c // 2, :]
    a_t = a * jnp.exp(cum - log_w - mid)
    r_t = r * jnp.exp(cum - mid)
    back = jnp.exp(mid - cum)
    b_t = b * back
    k_t = k * back
    strict, incl = (row > col)[None], (row >= col)[None]
    ar = jnp.concatenate([a_t, r_t], axis=1)
    on_b = _bmm(ar, b_t, "nt")
    on_k = _bmm(ar, k_t, "nt")
    n_mat = jnp.where(strict, on_b[:, :c], 0.0)
    p_mat = jnp.where(incl, on_b[:, c:], 0.0)
    m_mat = jnp.where(strict, on_k[:, :c], 0.0)
    q_mat = jnp.where(incl, on_k[:, c:], 0.0)
    inv, power, span = n_mat, _bmm(n_mat, n_mat, "nn"), 2
    while span < c:
        both = _bmm(jnp.concatenate([power, inv], axis=1), power, "nn")
        inv = inv + power + both[:, c:]
        power = both[:, :c]
        span *= 2
    s_mid = state * jnp.swapaxes(jnp.exp(mid), 1, 2)
    x = _bmm(jnp.concatenate([a_t, m_mat], axis=2), jnp.concatenate([s_mid, v], axis=1), "nn")
    u = x + _bmm(inv, x, "nn")
    y = _bmm(jnp.concatenate([r_t, p_mat, q_mat], axis=2), jnp.concatenate([s_mid, u, v], axis=1), "nn")
    grown = _bmm(jnp.concatenate([b_t, k_t], axis=1), jnp.concatenate([u, v], axis=1), "tn")
    s_new = (s_mid + grown) * jnp.swapaxes(jnp.exp(cum[:, c - 1:c, :] - mid), 1, 2)
    return y, s_new


def _scan_fwd(name, ops):
    nh, rows, dh = ops[0].shape
    nc = rows // CHUNK
    spec = pl.BlockSpec((nh, CHUNK, dh), lambda c: (0, c, 0))

    def body(r_ref, w_ref, k_ref, v_ref, a_ref, b_ref, y_ref, st_ref, state):
        @pl.when(pl.program_id(0) == 0)
        def _():
            state[...] = jnp.zeros_like(state)

        st_ref[0] = state[...]
        y, s_new = _chunk(state[...], r_ref[...], w_ref[...], k_ref[...], v_ref[...], a_ref[...], b_ref[...])
        y_ref[...] = y
        state[...] = s_new

    return pl.pallas_call(
        body, name=name, grid=(nc,),
        in_specs=[spec] * 6,
        out_specs=[spec, pl.BlockSpec((1, nh, dh, dh), lambda c: (c, 0, 0, 0))],
        out_shape=[jax.ShapeDtypeStruct((nh, rows, dh), F32), jax.ShapeDtypeStruct((nc, nh, dh, dh), F32)],
        scratch_shapes=[pltpu.VMEM((nh, dh, dh), F32)],
        compiler_params=_params("arbitrary"),
    )(*ops)


def _scan_bwd(name, ops, states, dy):
    nh, rows, dh = ops[0].shape
    nc = rows // CHUNK
    spec = pl.BlockSpec((nh, CHUNK, dh), lambda c: (0, nc - 1 - c, 0))

    def body(r_ref, w_ref, k_ref, v_ref, a_ref, b_ref, st_ref, dy_ref, *rest):
        outs, dstate = rest[:6], rest[6]

        @pl.when(pl.program_id(0) == 0)
        def _():
            dstate[...] = jnp.zeros_like(dstate)

        _, vjp = jax.vjp(_chunk, st_ref[0], r_ref[...], w_ref[...], k_ref[...], v_ref[...], a_ref[...], b_ref[...])
        grads = vjp((dy_ref[...], dstate[...]))
        dstate[...] = grads[0]
        for o_ref, gval in zip(outs, grads[1:]):
            o_ref[...] = gval

    return pl.pallas_call(
        body, name=name, grid=(nc,),
        in_specs=[spec] * 6 + [pl.BlockSpec((1, nh, dh, dh), lambda c: (nc - 1 - c, 0, 0, 0)), spec],
        out_specs=[spec] * 6,
        out_shape=[jax.ShapeDtypeStruct((nh, rows, dh), F32)] * 6,
        scratch_shapes=[pltpu.VMEM((nh, dh, dh), F32)],
        compiler_params=_params("arbitrary"),
    )(*ops, states, dy)


def _shift_down(x):
    return jnp.concatenate([jnp.zeros((1, x.shape[1]), x.dtype), x[:-1]], axis=0)


def _shift_up(x):
    return jnp.concatenate([x[1:], jnp.zeros((1, x.shape[1]), x.dtype)], axis=0)


def _pad_rows(x, rows):
    return jnp.concatenate([x, jnp.zeros((rows - x.shape[0],) + x.shape[1:], x.dtype)], axis=0)


def _pad_cols(x, cols):
    return jnp.concatenate([x, jnp.zeros(x.shape[:-1] + (cols - x.shape[-1],), x.dtype)], axis=-1)


def _lora_pad(w_up, a_up, g_up):
    z = lambda n: jnp.zeros((n, GROUP), F32)
    return (jnp.concatenate([w_up, z(LORA_PAD - LORA_W)], 0),
            jnp.concatenate([z(LORA_W), a_up, z(LORA_PAD - LORA_W - LORA_A)], 0),
            jnp.concatenate([z(LORA_W + LORA_A), g_up, z(LORA_PAD - LORA_W - LORA_A - LORA_G)], 0))


MID = ['w_in']
LATE = ['ffn2_w_gate', 'ffn2_w_up', 'ffn2_w_down', 'w_out']


def _local_step(x, tgt, w, late=None):
    d = x.shape[1]
    zeros = jnp.zeros((META_PAD, d), F32)
    h0 = jnp.concatenate([zeros, w["meta_tokens"], x], axis=0)
    tgt_p = jnp.concatenate([jnp.zeros((ROW0, d), F32), tgt], axis=0)
    ones_bd = ((lax.broadcasted_iota(jnp.int32, (2 * GROUP, GROUP), 0) % GROUP) // HEAD
               == lax.broadcasted_iota(jnp.int32, (2 * GROUP, GROUP), 1) // HEAD).astype(BF16)
    pre_vecs = (_pad_cols(w["rwkv_mu"], RW_COLS), w["rwkv_w0"], w["rwkv_a0"], w["rwkv_k_k"], w["rwkv_k_a"])
    pre_mats = _lora_pad(w["rwkv_w_up"], w["rwkv_a_up"], w["rwkv_g_up"])
    post_vecs = (w["rwkv_lnx_w"], w["rwkv_lnx_b"], w["rwkv_r_k"].reshape(1, GROUP))

    h1, a1, b1, *gathered = _ffn_fwd("ffn1_fwd", h0, w["ffn1_norm"], w["ffn1_w_gate"], w["ffn1_w_up"],
                                     w["ffn1_w_down"], late and ("gather", late.shards["mid"]))
    if late is not None:
        w = {**w, **late.join("mid", gathered)}
    w_in = _pad_cols(w["w_in"], IN_COLS_PAD)
    qkv, p, n2 = _norm_proj("in_proj", h1, w["mix_norm"], w_in)
    sb, rest_total, visited, *gathered = _sb_fwd("sb_fwd", qkv, late.shards["late"] if late else ())
    if late is not None:
        w = {**w, **late.join("late", gathered)}
    p_prev = _shift_down(p)
    pre = _rwkv_pre_fwd("rwkv_pre_fwd", p, p_prev, pre_vecs, pre_mats, ones_bd)
    scan_ops, token_ops = pre[:6], pre[6:]
    y, states = _scan_fwd("rwkv_scan_fwd", scan_ops)
    rw = _rwkv_post_fwd("rwkv_post_fwd", y, *token_ops, post_vecs, ones_bd)
    h2, mix = _out_proj("out_proj", h1, sb, rw, w["w_out"])
    h3, a2, b2 = _ffn_fwd("ffn2_fwd", h2, w["ffn2_norm"], w["ffn2_w_gate"], w["ffn2_w_up"], w["ffn2_w_down"])
    loss8, dh3, g_final = _loss_head("loss_head", h3, w["final_norm"].reshape(1, d), tgt_p)

    g = {"final_norm": g_final.reshape(d)}
    dh2, da2, db2, s2, n3, dhh3, g["ffn2_norm"] = _ffn_bwd(
        "ffn2_bwd", dh3, h2, w["ffn2_norm"], a2, b2, w["ffn2_w_gate"], w["ffn2_w_up"], w["ffn2_w_down"])
    g["ffn2_w_gate"] = _mm_tn("ffn2_dgate", n3, da2)
    g["ffn2_w_up"] = _mm_tn("ffn2_dup", n3, db2)
    g["ffn2_w_down"] = _mm_tn("ffn2_ddown", s2, dhh3)
    dsb, drw, dh2b = _out_proj_bwd("out_proj_bwd", dh2, w["w_out"])
    g["w_out"] = _mm_tn("out_proj_dw", mix, dh2b)
    dq, dk, dv, *reduced_late = _sb_bwd("sb_bwd", qkv, rest_total, visited, dsb, late.parts("late", g) if late else ())
    post_g = _rwkv_post_bwd("rwkv_post_bwd", y, *token_ops, post_vecs, ones_bd, drw)
    g["rwkv_lnx_w"], g["rwkv_lnx_b"] = post_g[5], post_g[6]
    g["rwkv_r_k"] = post_g[7].reshape(1, N_HEADS, HEAD)
    scan_g = _scan_bwd("rwkv_scan_bwd", scan_ops, states, post_g[0])
    pre_g = _rwkv_pre_bwd("rwkv_pre_bwd", p, p_prev, pre_vecs, pre_mats, ones_bd, scan_g, post_g[4], post_g[1:4])
    g["rwkv_mu"] = pre_g[2][:, :w["rwkv_mu"].shape[1]]
    g["rwkv_w0"], g["rwkv_a0"], g["rwkv_k_k"], g["rwkv_k_a"] = pre_g[3:7]
    g["rwkv_w_up"] = pre_g[7][:LORA_W]
    g["rwkv_a_up"] = pre_g[8][LORA_W:LORA_W + LORA_A]
    g["rwkv_g_up"] = pre_g[9][LORA_W + LORA_A:LORA_W + LORA_A + LORA_G]
    dp = pre_g[0] + _shift_up(pre_g[1])
    live = (jnp.arange(h0.shape[0]) >= META_PAD)[:, None]
    dproj = jnp.where(live, jnp.concatenate([dq, dk, dv, dp], axis=1), 0.0).astype(BF16)
    g["w_in"] = _mm_tn("in_proj_dw", n2, dproj)[:, :w["w_in"].shape[1]]
    dh1, g["mix_norm"] = _norm_proj_bwd("in_proj_bwd", dproj, w_in, h1, w["mix_norm"], dh2)
    dh0, da1, db1, s1, n1, dhh1, g["ffn1_norm"], *reduced_mid = _ffn_bwd(
        "ffn1_bwd", dh1, h0, w["ffn1_norm"], a1, b1, w["ffn1_w_gate"], w["ffn1_w_up"], w["ffn1_w_down"],
        late and ("reduce", late.parts("mid", g)))
    g["ffn1_w_gate"] = _mm_tn("ffn1_dgate", n1, da1)
    g["ffn1_w_up"] = _mm_tn("ffn1_dup", n1, db1)
    g["ffn1_w_down"] = _mm_tn("ffn1_ddown", s1, dhh1)
    g["meta_tokens"] = dh0[META_PAD:ROW0]
    return loss8[0, 0], dh0[ROW0:], g, {"mid": reduced_mid, "late": reduced_late}


N_CHIPS = 4
N_DEV = 8
HBM = pl.BlockSpec(memory_space=pltpu.HBM)


def _place():
    return lax.axis_index("x"), lax.axis_index("y"), lax.axis_index("c")


def _other_chips(x, y):
    return [(1 - x, y), (x, 1 - y), (1 - x, 1 - y)]


def _gather_sems(n):
    return [pltpu.SemaphoreType.DMA((3 * n,)), pltpu.SemaphoreType.DMA((3 * n,)), pltpu.SemaphoreType.DMA((n,)),
            pltpu.SemaphoreType.DMA((3 * n,)), pltpu.SemaphoreType.DMA((3 * n,))]


def _gather_exchange(ins, outs, sems):
    n = len(ins)
    half = [r.shape[0] // 2 for r in ins]
    send, recv, local, d2d_send, d2d_recv = sems
    x, y, c = _place()
    me = 2 * x + y
    chips = _other_chips(x, y)

    def rows_of(k, h):
        return pl.ds(pl.multiple_of(h * half[k], 8), half[k])

    def own(k):
        return pltpu.make_async_copy(ins[k], outs[k].at[me], local.at[k])

    def copy(j, k, slot):
        return pltpu.make_async_remote_copy(
            src_ref=ins[k].at[rows_of(k, c)], dst_ref=outs[k].at[slot, rows_of(k, c)],
            send_sem=send.at[j * n + k], recv_sem=recv.at[j * n + k],
            device_id=(chips[j][0], chips[j][1], c), device_id_type=MESH)

    def passed(j, k, h):
        slot = 2 * chips[j][0] + chips[j][1]
        return pltpu.make_async_remote_copy(
            src_ref=outs[k].at[slot, rows_of(k, h)], dst_ref=outs[k].at[slot, rows_of(k, h)],
            send_sem=d2d_send.at[j * n + k], recv_sem=d2d_recv.at[j * n + k],
            device_id=(x, y, 1 - c), device_id_type=MESH)

    def start():
        for k in range(n):
            own(k).start()
        for j in range(3):
            for k in range(n):
                copy(j, k, me).start()

    def finish():
        for j in range(3):
            for k in range(n):
                copy(j, k, 2 * chips[j][0] + chips[j][1]).wait_recv()
                passed(j, k, c).start()
        for j in range(3):
            for k in range(n):
                passed(j, k, 1 - c).wait_recv()
        for j in range(3):
            for k in range(n):
                copy(j, k, me).wait_send()
                passed(j, k, c).wait_send()
        for k in range(n):
            own(k).wait()

    return start, finish


def _gather_shards(name, shards):
    n = len(shards)

    def body(*refs):
        start, finish = _gather_exchange(refs[:n], refs[n:2 * n], refs[2 * n:])
        start()
        finish()

    return pl.pallas_call(
        body, name=name,
        in_specs=[HBM] * n, out_specs=[HBM] * n,
        out_shape=[jax.ShapeDtypeStruct((N_CHIPS,) + s.shape, s.dtype) for s in shards],
        scratch_shapes=_gather_sems(n),
    )(*shards)


def _pair_exchange(name, parts):
    n = len(parts)
    half = [s.shape[1] // 2 for s in parts]

    def body(*refs):
        ins, outs = refs[:n], refs[n:2 * n]
        send, recv = refs[2 * n:]
        x, y, c = _place()

        def copy(k):
            rows = pl.ds(pl.multiple_of((1 - c) * half[k], 8), half[k])
            return pltpu.make_async_remote_copy(
                src_ref=ins[k].at[:, rows], dst_ref=outs[k], send_sem=send.at[k], recv_sem=recv.at[k],
                device_id=(x, y, 1 - c), device_id_type=MESH)

        for k in range(n):
            copy(k).start()
        for k in range(n):
            copy(k).wait_recv()
        for k in range(n):
            copy(k).wait_send()

    return pl.pallas_call(
        body, name=name,
        in_specs=[HBM] * n, out_specs=[HBM] * n,
        out_shape=[jax.ShapeDtypeStruct((s.shape[0], s.shape[1] // 2, s.shape[2]), s.dtype) for s in parts],
        scratch_shapes=[pltpu.SemaphoreType.DMA((n,)), pltpu.SemaphoreType.DMA((n,))],
    )(*parts)


def _pair_add(name, part, other):
    nch, rows, cols = part.shape
    half = rows // 2

    def body(p_ref, o_ref, out_ref):
        c = lax.axis_index("c")
        mine = p_ref[0, pl.ds(pl.multiple_of(c * half, 16), half), :]
        out_ref[0] = (mine.astype(F32) + o_ref[0].astype(F32)).astype(out_ref.dtype)

    return pl.pallas_call(
        body, name=name, grid=(nch,),
        in_specs=[pl.BlockSpec((1, rows, cols), lambda j: (j, 0, 0)),
                  pl.BlockSpec((1, half, cols), lambda j: (j, 0, 0))],
        out_specs=pl.BlockSpec((1, half, cols), lambda j: (j, 0, 0)),
        out_shape=jax.ShapeDtypeStruct((nch, half, cols), part.dtype),
        compiler_params=_params("arbitrary"),
    )(part, other)


def _reduce_sems(n):
    return [pltpu.SemaphoreType.DMA((3 * n,)), pltpu.SemaphoreType.DMA((3 * n,)), pltpu.SemaphoreType.DMA((n,)),
            pltpu.SemaphoreType.DMA((n,)), pltpu.SemaphoreType.DMA((n,))]


def _reduce_exchange(ins, got, sib, sems):
    n = len(ins)
    send, recv, local, d2d_send, d2d_recv = sems
    x, y, c = _place()
    me = 2 * x + y
    chips = _other_chips(x, y)

    def own(k):
        return pltpu.make_async_copy(ins[k].at[me], got[k].at[me], local.at[k])

    def copy(j, k, shard, slot):
        return pltpu.make_async_remote_copy(
            src_ref=ins[k].at[shard], dst_ref=got[k].at[slot], send_sem=send.at[j * n + k],
            recv_sem=recv.at[j * n + k], device_id=(chips[j][0], chips[j][1], c), device_id_type=MESH)

    def swap(k):
        return pltpu.make_async_remote_copy(
            src_ref=got[k], dst_ref=sib[k], send_sem=d2d_send.at[k], recv_sem=d2d_recv.at[k],
            device_id=(x, y, 1 - c), device_id_type=MESH)

    def start():
        for k in range(n):
            own(k).start()
        for j in range(3):
            for k in range(n):
                copy(j, k, 2 * chips[j][0] + chips[j][1], me).start()

    def finish():
        for k in range(n):
            own(k).wait()
            for j in range(3):
                copy(j, k, me, 2 * chips[j][0] + chips[j][1]).wait_recv()
            swap(k).start()
        for k in range(n):
            swap(k).wait_recv()
        for j in range(3):
            for k in range(n):
                copy(j, k, me, me).wait_send()
        for k in range(n):
            swap(k).wait_send()

    return start, finish


def _reduce_shards(name, parts):
    n = len(parts)

    def body(*refs):
        start, finish = _reduce_exchange(refs[:n], refs[n:2 * n], refs[2 * n:3 * n], refs[3 * n:])
        start()
        finish()

    return pl.pallas_call(
        body, name=name,
        in_specs=[HBM] * n, out_specs=[HBM] * (2 * n),
        out_shape=[jax.ShapeDtypeStruct(s.shape, s.dtype) for s in parts] * 2,
        scratch_shapes=_reduce_sems(n),
    )(*parts)


def _all_reduce_small(name, vec):
    rows = vec.shape[0]

    def body(v_ref, o_ref, buf, send, recv):
        x, y, c = _place()
        me = 4 * x + 2 * y + c
        peers = [(x ^ (r >> 2), y ^ ((r >> 1) & 1), c ^ (r & 1)) for r in range(1, N_DEV)]

        def copy(r, slot):
            px, py, pc = peers[r]
            return pltpu.make_async_remote_copy(
                src_ref=v_ref, dst_ref=buf.at[slot], send_sem=send.at[r], recv_sem=recv.at[r],
                device_id=(px, py, pc), device_id_type=MESH)

        sent = [copy(r, me) for r in range(N_DEV - 1)]
        for cp in sent:
            cp.start()
        buf[me] = v_ref[...]
        for r in range(N_DEV - 1):
            px, py, pc = peers[r]
            copy(r, 4 * px + 2 * py + pc).wait_recv()
        total = buf[0]
        for dev in range(1, N_DEV):
            total = total + buf[dev]
        o_ref[...] = total
        for cp in sent:
            cp.wait_send()

    return pl.pallas_call(
        body, name=name,
        in_specs=[pl.BlockSpec(memory_space=pltpu.VMEM)], out_specs=pl.BlockSpec(memory_space=pltpu.VMEM),
        out_shape=jax.ShapeDtypeStruct(vec.shape, F32),
        scratch_shapes=[pltpu.VMEM((N_DEV, rows, 128), F32),
                        pltpu.SemaphoreType.DMA((N_DEV - 1,)), pltpu.SemaphoreType.DMA((N_DEV - 1,))],
        compiler_params=pltpu.CompilerParams(vmem_limit_bytes=VMEM_LIMIT),
    )(vec)


def _adamw(w, g, m, v):
    m = ADAM_B1 * m + (1.0 - ADAM_B1) * g
    v = ADAM_B2 * v + (1.0 - ADAM_B2) * (g * g)
    m_hat = m / (1.0 - ADAM_B1 ** ADAM_STEP)
    v_hat = v / (1.0 - ADAM_B2 ** ADAM_STEP)
    return -ADAM_LR * (m_hat / (jnp.sqrt(v_hat) + ADAM_EPS) + ADAM_WD * w), m, v


def _adamw_shard(name, core, w, m, v, got, sib):
    rows, cols = w.shape
    tr = rows // 4
    spec = pl.BlockSpec((tr, cols), lambda i, c_ref: (i, 0))
    spec4 = pl.BlockSpec((N_CHIPS, tr, cols), lambda i, c_ref: (0, i % 2, 0))

    def body(c_ref, w_ref, m_ref, v_ref, got_ref, sib_ref, g_ref, d_ref, mo_ref, vo_ref):
        def four(ref):
            return ((ref[0].astype(F32) + ref[1].astype(F32)) + ref[2].astype(F32)) + ref[3].astype(F32)

        g = jnp.where(pl.program_id(0) // 2 == c_ref[0], four(got_ref), four(sib_ref))
        g_ref[...] = g
        d_ref[...], mo_ref[...], vo_ref[...] = _adamw(w_ref[...], g, m_ref[...], v_ref[...])

    return pl.pallas_call(
        body, name=name,
        grid_spec=pltpu.PrefetchScalarGridSpec(
            num_scalar_prefetch=1, grid=(4,),
            in_specs=[spec, spec, spec, spec4, spec4], out_specs=[spec] * 4),
        out_shape=[jax.ShapeDtypeStruct((rows, cols), F32)] * 4,
        compiler_params=_params("arbitrary"),
    )(core, w, m, v, got, sib)


def _adamw_small(name, w, m, v, g):
    def body(w_ref, m_ref, v_ref, g_ref, d_ref, mo_ref, vo_ref):
        d_ref[...], mo_ref[...], vo_ref[...] = _adamw(w_ref[...], g_ref[...], m_ref[...], v_ref[...])

    return pl.pallas_call(body, name=name, out_shape=[jax.ShapeDtypeStruct(w.shape, F32)] * 3)(w, m, v, g)


def _cast_bf16(name, arrays):
    n = len(arrays)

    def body(*refs):
        for i_ref, o_ref in zip(refs[:n], refs[n:]):
            o_ref[...] = i_ref[...].astype(BF16)

    return pl.pallas_call(
        body, name=name, out_shape=[jax.ShapeDtypeStruct(a.shape, BF16) for a in arrays],
        compiler_params=pltpu.CompilerParams(vmem_limit_bytes=VMEM_LIMIT),
    )(*arrays)


def _pack(arrays, rows):
    flat = jnp.concatenate([a.reshape(-1) for a in arrays])
    return jnp.concatenate([flat, jnp.zeros((rows * 128 - flat.shape[0],), F32)]).reshape(rows, 128)


def _unpack(packed, shapes):
    flat, out, at = packed.reshape(-1), [], 0
    for s in shapes:
        size = 1
        for dim in s:
            size *= dim
        out.append(flat[at:at + size].reshape(s))
        at += size
    return out


def _rows_for(shapes):
    total = 0
    for s in shapes:
        size = 1
        for dim in s:
            size *= dim
        total += size
    return -(-total // 1024) * 8


WEIGHTS = ['meta_tokens', 'ffn1_norm', 'ffn1_w_gate', 'ffn1_w_up', 'ffn1_w_down', 'mix_norm', 'w_in', 'rwkv_mu',
           'rwkv_w0', 'rwkv_w_up', 'rwkv_a0', 'rwkv_a_up', 'rwkv_g_up', 'rwkv_k_k', 'rwkv_k_a', 'rwkv_r_k',
           'rwkv_lnx_w', 'rwkv_lnx_b', 'w_out', 'ffn2_norm', 'ffn2_w_gate', 'ffn2_w_up', 'ffn2_w_down', 'final_norm']
COL_CUT = ['ffn1_w_gate', 'ffn1_w_up', 'w_in', 'ffn2_w_gate', 'ffn2_w_up']
ROW_CUT = ['ffn1_w_down', 'w_out', 'ffn2_w_down']
SMALL_CUT = ['meta_tokens', 'rwkv_w_up', 'rwkv_a_up', 'rwkv_g_up']
BIG = COL_CUT + ROW_CUT
REPLICATED = [n for n in WEIGHTS if n not in BIG + SMALL_CUT]


def _join_cols(a):
    return a.transpose(1, 0, 2).reshape(a.shape[1], N_CHIPS * a.shape[2])


def _cut_cols(a):
    return a.reshape(a.shape[0], N_CHIPS, a.shape[1] // N_CHIPS).transpose(1, 0, 2)


def _step(x, loss_target, w, m, v):
    two = lambda a: a.reshape(a.shape[-2], a.shape[-1])

    def join(names, gathered):
        return {n: (_join_cols(a) if n in COL_CUT + SMALL_CUT else a.reshape(-1, a.shape[-1]))
                for n, a in zip(names, gathered)}

    def pair_sums(tag, names, g):
        parts = [_cut_cols(g[n]) if n in COL_CUT else g[n].reshape(N_CHIPS, -1, g[n].shape[-1]) for n in names]
        arrived = _pair_exchange("pair_exchange_" + tag, parts)
        return [_pair_add("pair_add_" + n, p, o) for n, p, o in zip(names, parts, arrived)]

    first = [n for n in BIG if n not in MID + LATE]
    groups = {"mid": MID, "late": LATE}
    cast = dict(zip(BIG, _cast_bf16("cast_weights", [two(w[n]) for n in BIG])))
    names = first + SMALL_CUT
    shards = [cast[n] for n in first] + [two(w[n]) for n in SMALL_CUT]
    full = {n: (two(w[n]) if w[n].ndim == 3 else w[n]) for n in REPLICATED}
    full.update(join(names, _gather_shards("gather_weights", shards)))
    full["rwkv_r_k"] = w["rwkv_r_k"]
    full["final_norm"] = w["final_norm"]

    late = types.SimpleNamespace(shards={k: [cast[n] for n in names] for k, names in groups.items()},
                                 join=lambda k, gathered: join(groups[k], gathered),
                                 parts=lambda k, g: pair_sums(k, groups[k], g))

    loss, dx, g, reduced = _local_step(x[0], loss_target[0], full, late)
    loss = lax.psum(loss, ("x", "y", "c"))

    groups["first"] = first
    reduced["first"] = list(_reduce_shards("reduce_gradients", pair_sums("first", first, g)))
    got, sib = {}, {}
    for k, names in groups.items():
        got.update(zip(names, reduced[k][:len(names)]))
        sib.update(zip(names, reduced[k][len(names):]))

    small_names = REPLICATED + SMALL_CUT
    small_shapes = [g[n].shape for n in small_names]
    small = _all_reduce_small("reduce_small", _pack([g[n] for n in small_names], _rows_for(small_shapes)))
    g_small = dict(zip(small_names, _unpack(small, small_shapes)))
    chip = 2 * lax.axis_index("x") + lax.axis_index("y")
    for n in SMALL_CUT:
        width = g_small[n].shape[1] // N_CHIPS
        g_small[n] = lax.dynamic_slice_in_dim(g_small[n], chip * width, width, axis=1)

    grad, delta, new_m, new_v = {}, {}, {}, {}
    core = lax.axis_index("c").astype(jnp.int32).reshape(1)
    for n in BIG:
        outs = _adamw_shard("adamw_" + n, core, two(w[n]), two(m[n]), two(v[n]), got[n], sib[n])
        grad[n], delta[n], new_m[n], new_v[n] = (o.reshape(w[n].shape) for o in outs)
    shapes = [w[n].shape for n in small_names]
    rows = _rows_for(shapes)
    packed = [_pack([t[n] for n in small_names], rows) for t in (w, m, v)]
    g_packed = _pack([g_small[n] for n in small_names], rows)
    outs = [_unpack(o, shapes) for o in _adamw_small("adamw_small", *packed, g_packed)]
    for i, n in enumerate(small_names):
        grad[n] = g_small[n].reshape(w[n].shape)
        delta[n], new_m[n], new_v[n] = outs[0][i], outs[1][i], outs[2][i]
    return loss, dx[None], grad, delta, new_m, new_v


def kernel(x, meta_tokens, ffn1_norm, ffn1_w_gate, ffn1_w_up, ffn1_w_down, mix_norm, w_in, rwkv_mu, rwkv_w0, rwkv_w_up, rwkv_a0, rwkv_a_up, rwkv_g_up, rwkv_k_k, rwkv_k_a, rwkv_r_k, rwkv_lnx_w, rwkv_lnx_b, w_out, ffn2_norm, ffn2_w_gate, ffn2_w_up, ffn2_w_down, final_norm, loss_target, m_meta_tokens, m_ffn1_norm, m_ffn1_w_gate, m_ffn1_w_up, m_ffn1_w_down, m_mix_norm, m_w_in, m_rwkv_mu, m_rwkv_w0, m_rwkv_w_up, m_rwkv_a0, m_rwkv_a_up, m_rwkv_g_up, m_rwkv_k_k, m_rwkv_k_a, m_rwkv_r_k, m_rwkv_lnx_w, m_rwkv_lnx_b, m_w_out, m_ffn2_norm, m_ffn2_w_gate, m_ffn2_w_up, m_ffn2_w_down, m_final_norm, v_meta_tokens, v_ffn1_norm, v_ffn1_w_gate, v_ffn1_w_up, v_ffn1_w_down, v_mix_norm, v_w_in, v_rwkv_mu, v_rwkv_w0, v_rwkv_w_up, v_rwkv_a0, v_rwkv_a_up, v_rwkv_g_up, v_rwkv_k_k, v_rwkv_k_a, v_rwkv_r_k, v_rwkv_lnx_w, v_rwkv_lnx_b, v_w_out, v_ffn2_norm, v_ffn2_w_gate, v_ffn2_w_up, v_ffn2_w_down, v_final_norm):
    w = dict(zip(WEIGHTS, (meta_tokens, ffn1_norm, ffn1_w_gate, ffn1_w_up, ffn1_w_down, mix_norm, w_in, rwkv_mu, rwkv_w0, rwkv_w_up, rwkv_a0, rwkv_a_up, rwkv_g_up, rwkv_k_k, rwkv_k_a, rwkv_r_k, rwkv_lnx_w, rwkv_lnx_b, w_out, ffn2_norm, ffn2_w_gate, ffn2_w_up, ffn2_w_down, final_norm)))
    m = dict(zip(WEIGHTS, (m_meta_tokens, m_ffn1_norm, m_ffn1_w_gate, m_ffn1_w_up, m_ffn1_w_down, m_mix_norm, m_w_in, m_rwkv_mu, m_rwkv_w0, m_rwkv_w_up, m_rwkv_a0, m_rwkv_a_up, m_rwkv_g_up, m_rwkv_k_k, m_rwkv_k_a, m_rwkv_r_k, m_rwkv_lnx_w, m_rwkv_lnx_b, m_w_out, m_ffn2_norm, m_ffn2_w_gate, m_ffn2_w_up, m_ffn2_w_down, m_final_norm)))
    v = dict(zip(WEIGHTS, (v_meta_tokens, v_ffn1_norm, v_ffn1_w_gate, v_ffn1_w_up, v_ffn1_w_down, v_mix_norm, v_w_in, v_rwkv_mu, v_rwkv_w0, v_rwkv_w_up, v_rwkv_a0, v_rwkv_a_up, v_rwkv_g_up, v_rwkv_k_k, v_rwkv_k_a, v_rwkv_r_k, v_rwkv_lnx_w, v_rwkv_lnx_b, v_w_out, v_ffn2_norm, v_ffn2_w_gate, v_ffn2_w_up, v_ffn2_w_down, v_final_norm)))
    loss, grad_x, grad, delta, new_m, new_v = _step(x, loss_target, w, m, v)
    return (loss, grad_x, *[grad[n] for n in WEIGHTS], *[delta[n] for n in WEIGHTS],
            *[new_m[n] for n in WEIGHTS], *[new_v[n] for n in WEIGHTS])
```

```python
import functools
import types

import jax
import jax.numpy as jnp
from jax import lax
from jax.experimental import pallas as pl
from jax.experimental.pallas import tpu as pltpu

F32 = jnp.float32
BF16 = jnp.bfloat16

RMS_EPS = 1e-6
LNX_EPS = 64e-5
N_META = 16
ROW0 = 128
META_PAD = ROW0 - N_META
HEAD = 64
N_HEADS = 8
GROUP = N_HEADS * HEAD
LORA_W, LORA_A, LORA_G = 32, 32, 96
LORA_PAD = 256
RW_COLS = 3 * GROUP + LORA_PAD
IN_COLS_PAD = 3 * GROUP + RW_COLS
ATT_BLOCK = 128
CHUNK = 64
VMEM_LIMIT = 56 * 1024 * 1024

ADAM_LR, ADAM_B1, ADAM_B2, ADAM_EPS, ADAM_WD, ADAM_STEP = 0.001, 0.9, 0.999, 1e-08, 0.01, 10

MESH = pl.DeviceIdType.MESH


def _params(*sem):
    return pltpu.CompilerParams(dimension_semantics=tuple(sem), vmem_limit_bytes=VMEM_LIMIT)


def _dot(a, b):
    return lax.dot_general(a, b, (((1,), (0,)), ((), ())), preferred_element_type=F32)


def _dot_nt(a, b):
    return lax.dot_general(a, b, (((1,), (1,)), ((), ())), preferred_element_type=F32)


def _dot_tn(a, b):
    return lax.dot_general(a, b, (((0,), (0,)), ((), ())), preferred_element_type=F32)


def _split2(x):
    hi = x.astype(BF16)
    return hi, (x - hi.astype(F32)).astype(BF16)


def _sigmoid(x):
    return 1.0 / (1.0 + jnp.exp(-x))


def _rms_fwd(x, g):
    rstd = lax.rsqrt(jnp.mean(x * x, axis=-1, keepdims=True) + RMS_EPS)
    xhat = x * rstd
    return xhat * g, xhat, rstd


def _rms_bwd(dn, xhat, rstd, g):
    dxhat = dn * g
    dx = rstd * (dxhat - xhat * jnp.mean(dxhat * xhat, axis=-1, keepdims=True))
    return dx, jnp.sum(dn * xhat, axis=0, keepdims=True)


def _row_tile(rows):
    return 384 if rows % 384 == 0 else 128


def _half_tile(cols):
    return cols // 2 if cols % 256 == 0 else cols


def _tall_tile(rows, parts):
    return rows // parts if rows % (16 * parts) == 0 else _row_tile(rows)


def _call_with_exchange(name, body, grid, in_specs, out_specs, out_shape, scratch, operands, params, exchange):
    if exchange is None or not exchange[1]:
        return pl.pallas_call(body, name=name, grid=grid, in_specs=in_specs, out_specs=out_specs,
                              out_shape=out_shape, scratch_shapes=scratch, compiler_params=params)(*operands)
    kind, arrays = exchange
    ns, n_in, n_out, n_scr = len(arrays), len(in_specs), len(out_specs), len(scratch)
    if kind == "gather":
        results = [jax.ShapeDtypeStruct((N_CHIPS,) + s.shape, s.dtype) for s in arrays]
        sems = _gather_sems(ns)
    else:
        results = [jax.ShapeDtypeStruct(s.shape, s.dtype) for s in arrays] * 2
        sems = _reduce_sems(ns)
    n_res = len(results)

    def carried(*refs):
        at = n_in + ns + n_out
        sent, landed = refs[n_in:n_in + ns], refs[at:at + n_res]
        own_scratch, sem_refs = refs[at + n_res:at + n_res + n_scr], refs[at + n_res + n_scr:]
        first, last = _first_and_last_step(grid)
        if kind == "gather":
            start, finish = _gather_exchange(sent, landed, sem_refs)
        else:
            start, finish = _reduce_exchange(sent, landed[:ns], landed[ns:], sem_refs)
        pl.when(first)(start)
        body(*refs[:n_in], *refs[n_in + ns:at], *own_scratch)
        pl.when(last)(finish)

    return pl.pallas_call(
        carried, name=name, grid=grid, in_specs=list(in_specs) + [HBM] * ns, out_specs=list(out_specs) + [HBM] * n_res,
        out_shape=list(out_shape) + results, scratch_shapes=list(scratch) + sems, compiler_params=params,
    )(*operands, *arrays)


def _ffn_fwd(name, h, g, wg, wu, wd, exchange=None):
    rows, d = h.shape
    f = wg.shape[0]
    tm, tf = _row_tile(rows), _half_tile(f)
    nj = f // tf

    def body(h_ref, g_ref, wg_ref, wu_ref, wd_ref, ho_ref, a_ref, b_ref, n_sc, acc_sc):
        j = pl.program_id(1)

        @pl.when(j == 0)
        def _():
            n, _, _ = _rms_fwd(h_ref[...], g_ref[...])
            n_sc[...] = n.astype(BF16)
            acc_sc[...] = jnp.zeros_like(acc_sc)

        n = n_sc[...]
        a = _dot_nt(n, wg_ref[...])
        b = _dot_nt(n, wu_ref[...])
        a_ref[...] = a
        b_ref[...] = b
        s = a * _sigmoid(a) * b
        acc_sc[...] += _dot(s.astype(BF16), wd_ref[...])

        @pl.when(j == nj - 1)
        def _():
            ho_ref[...] = h_ref[...] + 0.5 * acc_sc[...]

    return _call_with_exchange(
        name, body, (rows // tm, nj),
        [pl.BlockSpec((tm, d), lambda i, j: (i, 0)),
         pl.BlockSpec((1, d), lambda i, j: (0, 0)),
         pl.BlockSpec((tf, d), lambda i, j: (j, 0)),
         pl.BlockSpec((tf, d), lambda i, j: (j, 0)),
         pl.BlockSpec((tf, d), lambda i, j: (j, 0))],
        [pl.BlockSpec((tm, d), lambda i, j: (i, 0)),
         pl.BlockSpec((tm, tf), lambda i, j: (i, j)),
         pl.BlockSpec((tm, tf), lambda i, j: (i, j))],
        [jax.ShapeDtypeStruct((rows, d), F32),
         jax.ShapeDtypeStruct((rows, f), F32),
         jax.ShapeDtypeStruct((rows, f), F32)],
        [pltpu.VMEM((tm, d), BF16), pltpu.VMEM((tm, d), F32)],
        (h, g, wg, wu, wd), _params("arbitrary", "arbitrary"), exchange)


def _ffn_bwd(name, dh, h, g, a, b, wg, wu, wd, exchange=None):
    rows, d = h.shape
    f = wg.shape[0]
    tm, tf = _row_tile(rows), _half_tile(f)
    ni, nj = rows // tm, f // tf

    def body(dh_ref, h_ref, g_ref, a_ref, b_ref, wg_ref, wu_ref, wd_ref,
             dhi_ref, da_ref, db_ref, s_ref, n_ref, dhh_ref, dg_ref, dn_sc):
        i, j = pl.program_id(0), pl.program_id(1)

        @pl.when(j == 0)
        def _():
            n, _, _ = _rms_fwd(h_ref[...], g_ref[...])
            n_ref[...] = n.astype(BF16)
            dhh_ref[...] = (0.5 * dh_ref[...]).astype(BF16)
            dn_sc[...] = jnp.zeros_like(dn_sc)

        @pl.when((i == 0) & (j == 0))
        def _():
            dg_ref[...] = jnp.zeros_like(dg_ref)

        ds = _dot_nt(dhh_ref[...], wd_ref[...])
        av, bv = a_ref[...], b_ref[...]
        sig = _sigmoid(av)
        silu = av * sig
        s_ref[...] = (silu * bv).astype(BF16)
        db = (ds * silu).astype(BF16)
        da = (ds * bv * (sig * (1.0 + av * (1.0 - sig)))).astype(BF16)
        da_ref[...] = da
        db_ref[...] = db
        dn_sc[...] += _dot(da, wg_ref[...]) + _dot(db, wu_ref[...])

        @pl.when(j == nj - 1)
        def _():
            gv = g_ref[...]
            _, xhat, rstd = _rms_fwd(h_ref[...], gv)
            dx, dg = _rms_bwd(dn_sc[...], xhat, rstd, gv)
            dhi_ref[...] = dh_ref[...] + dx
            dg_ref[...] += dg

    return _call_with_exchange(
        name, body, (ni, nj),
        [pl.BlockSpec((tm, d), lambda i, j: (i, 0)),
         pl.BlockSpec((tm, d), lambda i, j: (i, 0)),
         pl.BlockSpec((1, d), lambda i, j: (0, 0)),
         pl.BlockSpec((tm, tf), lambda i, j: (i, j)),
         pl.BlockSpec((tm, tf), lambda i, j: (i, j)),
         pl.BlockSpec((tf, d), lambda i, j: (j, 0)),
         pl.BlockSpec((tf, d), lambda i, j: (j, 0)),
         pl.BlockSpec((tf, d), lambda i, j: (j, 0))],
        [pl.BlockSpec((tm, d), lambda i, j: (i, 0)),
         pl.BlockSpec((tm, tf), lambda i, j: (i, j)),
         pl.BlockSpec((tm, tf), lambda i, j: (i, j)),
         pl.BlockSpec((tm, tf), lambda i, j: (i, j)),
         pl.BlockSpec((tm, d), lambda i, j: (i, 0)),
         pl.BlockSpec((tm, d), lambda i, j: (i, 0)),
         pl.BlockSpec((1, d), lambda i, j: (0, 0))],
        [jax.ShapeDtypeStruct((rows, d), F32),
         jax.ShapeDtypeStruct((rows, f), BF16),
         jax.ShapeDtypeStruct((rows, f), BF16),
         jax.ShapeDtypeStruct((rows, f), BF16),
         jax.ShapeDtypeStruct((rows, d), BF16),
         jax.ShapeDtypeStruct((rows, d), BF16),
         jax.ShapeDtypeStruct((1, d), F32)],
        [pltpu.VMEM((tm, d), F32)],
        (dh, h, g, a, b, wg, wu, wd), _params("arbitrary", "arbitrary"), exchange)


def _mm_tn(name, a, b):
    k, m = a.shape
    n = b.shape[1]
    tk = _tall_tile(k, 3)
    tm = _half_tile(m) if m > 1024 else m
    tn = _half_tile(n) if n > 1024 else n
    nk = k // tk

    def body(a_ref, b_ref, o_ref, acc):
        kk = pl.program_id(2)

        @pl.when(kk == 0)
        def _():
            acc[...] = jnp.zeros_like(acc)

        acc[...] += _dot_tn(a_ref[...], b_ref[...])

        @pl.when(kk == nk - 1)
        def _():
            o_ref[...] = acc[...].astype(BF16)

    return pl.pallas_call(
        body, name=name, grid=(m // tm, n // tn, nk),
        in_specs=[pl.BlockSpec((tk, tm), lambda i, j, kk: (kk, i)),
                  pl.BlockSpec((tk, tn), lambda i, j, kk: (kk, j))],
        out_specs=pl.BlockSpec((tm, tn), lambda i, j, kk: (i, j)),
        out_shape=jax.ShapeDtypeStruct((m, n), BF16),
        scratch_shapes=[pltpu.VMEM((tm, tn), F32)],
        compiler_params=_params("arbitrary", "arbitrary", "arbitrary"),
    )(a, b)


def _norm_proj(name, h, g, w):
    rows, d = h.shape
    n = w.shape[1]
    split = 3 * GROUP
    tm = _row_tile(rows)

    def body(h_ref, g_ref, w_ref, qkv_ref, p_ref, n_ref):
        nv, _, _ = _rms_fwd(h_ref[...], g_ref[...])
        nb = nv.astype(BF16)
        n_ref[...] = nb
        qkv_ref[...] = _dot(nb, w_ref[:, :split]).astype(BF16)
        p_ref[...] = _dot(nb, w_ref[:, split:])

    return pl.pallas_call(
        body, name=name, grid=(rows // tm,),
        in_specs=[pl.BlockSpec((tm, d), lambda i: (i, 0)),
                  pl.BlockSpec((1, d), lambda i: (0, 0)),
                  pl.BlockSpec((d, n), lambda i: (0, 0))],
        out_specs=[pl.BlockSpec((tm, split), lambda i: (i, 0)),
                   pl.BlockSpec((tm, n - split), lambda i: (i, 0)),
                   pl.BlockSpec((tm, d), lambda i: (i, 0))],
        out_shape=[jax.ShapeDtypeStruct((rows, split), BF16), jax.ShapeDtypeStruct((rows, n - split), F32),
                   jax.ShapeDtypeStruct((rows, d), BF16)],
        compiler_params=_params("arbitrary"),
    )(h, g, w)


def _out_proj(name, h, sb, rw, w):
    rows, d = h.shape
    gw = sb.shape[1]
    tm = _row_tile(rows)

    def body(h_ref, sb_ref, rw_ref, w_ref, o_ref, mix_ref):
        mix_ref[:, :gw] = sb_ref[...].astype(BF16)
        mix_ref[:, gw:] = rw_ref[...].astype(BF16)
        o_ref[...] = h_ref[...] + _dot(mix_ref[...], w_ref[...])

    return pl.pallas_call(
        body, name=name, grid=(rows // tm,),
        in_specs=[pl.BlockSpec((tm, d), lambda i: (i, 0)),
                  pl.BlockSpec((tm, gw), lambda i: (i, 0)),
                  pl.BlockSpec((tm, gw), lambda i: (i, 0)),
                  pl.BlockSpec((2 * gw, d), lambda i: (0, 0))],
        out_specs=[pl.BlockSpec((tm, d), lambda i: (i, 0)),
                   pl.BlockSpec((tm, 2 * gw), lambda i: (i, 0))],
        out_shape=[jax.ShapeDtypeStruct((rows, d), F32), jax.ShapeDtypeStruct((rows, 2 * gw), BF16)],
        compiler_params=_params("arbitrary"),
    )(h, sb, rw, w)


def _out_proj_bwd(name, dh, w):
    rows, d = dh.shape
    k = w.shape[0]
    tm = _row_tile(rows)

    def body(dh_ref, w_ref, dsb_ref, drw_ref, dhb_ref):
        dhb = dh_ref[...].astype(BF16)
        dhb_ref[...] = dhb
        dsb_ref[...] = _dot_nt(dhb, w_ref[:GROUP, :]).astype(BF16)
        drw_ref[...] = _dot_nt(dhb, w_ref[GROUP:, :])

    return pl.pallas_call(
        body, name=name, grid=(rows // tm,),
        in_specs=[pl.BlockSpec((tm, d), lambda i: (i, 0)),
                  pl.BlockSpec((k, d), lambda i: (0, 0))],
        out_specs=[pl.BlockSpec((tm, GROUP), lambda i: (i, 0)),
                   pl.BlockSpec((tm, GROUP), lambda i: (i, 0)),
                   pl.BlockSpec((tm, d), lambda i: (i, 0))],
        out_shape=[jax.ShapeDtypeStruct((rows, GROUP), BF16), jax.ShapeDtypeStruct((rows, GROUP), F32),
                   jax.ShapeDtypeStruct((rows, d), BF16)],
        compiler_params=_params("arbitrary"),
    )(dh, w)


def _norm_proj_bwd(name, dproj, w, h, g, dh):
    rows, n = dproj.shape
    d = w.shape[0]
    tm = _row_tile(rows)

    def body(dp_ref, w_ref, h_ref, g_ref, dh_ref, o_ref, dg_ref):
        @pl.when(pl.program_id(0) == 0)
        def _():
            dg_ref[...] = jnp.zeros_like(dg_ref)

        dn = _dot_nt(dp_ref[...], w_ref[...])
        gv = g_ref[...]
        _, xhat, rstd = _rms_fwd(h_ref[...], gv)
        dx, dg = _rms_bwd(dn, xhat, rstd, gv)
        o_ref[...] = dh_ref[...] + dx
        dg_ref[...] += dg

    return pl.pallas_call(
        body, name=name, grid=(rows // tm,),
        in_specs=[pl.BlockSpec((tm, n), lambda i: (i, 0)),
                  pl.BlockSpec((d, n), lambda i: (0, 0)),
                  pl.BlockSpec((tm, d), lambda i: (i, 0)),
                  pl.BlockSpec((1, d), lambda i: (0, 0)),
                  pl.BlockSpec((tm, d), lambda i: (i, 0))],
        out_specs=[pl.BlockSpec((tm, d), lambda i: (i, 0)),
                   pl.BlockSpec((1, d), lambda i: (0, 0))],
        out_shape=[jax.ShapeDtypeStruct((rows, d), F32), jax.ShapeDtypeStruct((1, d), F32)],
        compiler_params=_params("arbitrary"),
    )(dproj, w, h, g, dh)


def _loss_head(name, h, g, tgt):
    rows, d = h.shape
    tm = _row_tile(rows)

    def body(h_ref, g_ref, t_ref, loss_ref, dh_ref, dg_ref):
        i = pl.program_id(0)

        @pl.when(i == 0)
        def _():
            loss_ref[...] = jnp.zeros_like(loss_ref)
            dg_ref[...] = jnp.zeros_like(dg_ref)

        gv = g_ref[...]
        y, xhat, rstd = _rms_fwd(h_ref[...], gv)
        row = i * tm + lax.broadcasted_iota(jnp.int32, (tm, 1), 0)
        diff = jnp.where(row >= ROW0, y - t_ref[...], 0.0)
        part = 0.5 * jnp.sum(jnp.sum(diff * diff, axis=-1, keepdims=True), axis=0, keepdims=True) / d
        loss_ref[...] += jnp.broadcast_to(part, loss_ref.shape)
        dx, dg = _rms_bwd(diff / d, xhat, rstd, gv)
        dh_ref[...] = dx
        dg_ref[...] += dg

    return pl.pallas_call(
        body, name=name, grid=(rows // tm,),
        in_specs=[pl.BlockSpec((tm, d), lambda i: (i, 0)),
                  pl.BlockSpec((1, d), lambda i: (0, 0)),
                  pl.BlockSpec((tm, d), lambda i: (i, 0))],
        out_specs=[pl.BlockSpec((8, 128), lambda i: (0, 0)),
                   pl.BlockSpec((tm, d), lambda i: (i, 0)),
                   pl.BlockSpec((1, d), lambda i: (0, 0))],
        out_shape=[jax.ShapeDtypeStruct((8, 128), F32),
                   jax.ShapeDtypeStruct((rows, d), F32),
                   jax.ShapeDtypeStruct((1, d), F32)],
        compiler_params=_params("arbitrary"),
    )(h, g, tgt)


def _sb_block(qb, kb, i, jb, scale):
    bq, bk = qb.shape[0], kb.shape[0]
    z = _dot_nt(qb, kb) * scale
    qpos = i * bq + lax.broadcasted_iota(jnp.int32, (bq, bk), 0)
    kpos = jb * bk + lax.broadcasted_iota(jnp.int32, (bq, bk), 1)
    valid = (kpos < qpos) & (kpos >= META_PAD)
    e = jnp.exp(-jnp.abs(z))
    log_keep = jnp.where(valid, -(jnp.maximum(z, 0.0) + jnp.log(1.0 + e)), 0.0)
    return z, valid, e, log_keep


def _tri2(n, cmp):
    r = lax.broadcasted_iota(jnp.int32, (2 * n, n), 0) % n
    c = lax.broadcasted_iota(jnp.int32, (2 * n, n), 1)
    return cmp(r, c).astype(BF16)


def _dot_split(x, t2):
    hi, lo = _split2(x)
    return _dot(jnp.concatenate([hi, lo], axis=1), t2)


ATT_HEADS = 128 // HEAD
ATT_CUT = -104.0
ATT_TILES = GROUP // 128


def _lanes(hh):
    return slice(hh * HEAD, (hh + 1) * HEAD)


def _first_and_last_step(grid):
    here = [pl.program_id(a) for a in range(len(grid))]
    first, last = here[0] == 0, here[0] == grid[0] - 1
    for a in range(1, len(grid)):
        first, last = first & (here[a] == 0), last & (here[a] == grid[a] - 1)
    return first, last


def _sb_fwd(name, qkv, shards=()):
    rows = qkv.shape[0]
    nh, dh = N_HEADS, HEAD
    bq, bk, hg = _row_tile(rows), ATT_BLOCK, ATT_HEADS
    per = bq // bk
    scale = dh ** -0.5
    ns = len(shards)
    grid = (nh // hg, rows // bq)

    def body(q_ref, k_ref, v_ref, *rest):
        o_ref, rt_ref, cnt_ref = rest[ns:ns + 3]
        if ns:
            first, last = _first_and_last_step(grid)
            start, finish = _gather_exchange(rest[:ns], rest[ns + 3:2 * ns + 3], rest[2 * ns + 3:])
            pl.when(first)(start)
        i = pl.program_id(1)
        after = _tri2(bk, lambda r, c: r > c)
        nkb = (i + 1) * per

        def live(state):
            n, carry = state
            top = jnp.max(carry[0][0])
            for hh in range(1, hg):
                top = jnp.maximum(top, jnp.max(carry[hh][0]))
            return (n < nkb) & (top >= ATT_CUT)

        def step(state):
            n, carry = state
            jb = nkb - 1 - n
            off = pl.multiple_of(jb * bk, bk)
            out = []
            for hh in range(hg):
                rest, acc = carry[hh]
                kb = k_ref[pl.ds(off, bk), _lanes(hh)]
                vb = v_ref[pl.ds(off, bk), _lanes(hh)]
                z, valid, _, log_keep = _sb_block(q_ref[:, _lanes(hh)], kb, i, jb, scale)
                log_rest = rest + _dot_split(log_keep, after)
                attn = jnp.where(valid, jnp.exp(z + log_keep + log_rest), 0.0)
                out.append((rest + jnp.sum(log_keep, axis=-1, keepdims=True), acc + _dot(attn.astype(BF16), vb)))
            return n + 1, tuple(out)

        init = tuple((jnp.zeros((bq, 1), F32), jnp.zeros((bq, dh), F32)) for _ in range(hg))
        n, res = lax.while_loop(live, step, (jnp.int32(0), init))
        for hh in range(hg):
            rt_ref[hh] = res[hh][0]
            o_ref[:, _lanes(hh)] = res[hh][1]
            cnt_ref[hh] = jnp.full((bq, 1), n, F32)
        if ns:
            pl.when(last)(finish)

    return pl.pallas_call(
        body, name=name, grid=grid,
        in_specs=[pl.BlockSpec((bq, 128), lambda h, i: (i, h)),
                  pl.BlockSpec((rows, 128), lambda h, i: (0, ATT_TILES + h)),
                  pl.BlockSpec((rows, 128), lambda h, i: (0, 2 * ATT_TILES + h))] + [HBM] * ns,
        out_specs=[pl.BlockSpec((bq, 128), lambda h, i: (i, h)),
                   pl.BlockSpec((hg, bq, 1), lambda h, i: (h, i, 0)),
                   pl.BlockSpec((hg, bq, 1), lambda h, i: (h, i, 0))] + [HBM] * ns,
        out_shape=[jax.ShapeDtypeStruct((rows, GROUP), F32), jax.ShapeDtypeStruct((nh, rows, 1), F32),
                   jax.ShapeDtypeStruct((nh, rows, 1), F32)]
        + [jax.ShapeDtypeStruct((N_CHIPS,) + s.shape, s.dtype) for s in shards],
        scratch_shapes=_gather_sems(ns) if ns else [],
        compiler_params=_params("arbitrary", "arbitrary"),
    )(qkv, qkv, qkv, *shards)


def _sb_bwd(name, qkv, rt, cnt, do, parts=()):
    rows = qkv.shape[0]
    nh, dh = N_HEADS, HEAD
    bq, bk, hg = _row_tile(rows), ATT_BLOCK, ATT_HEADS
    per = bq // bk
    scale = dh ** -0.5
    ns = len(parts)
    grid = (nh // hg, rows // bq)

    def body(q_ref, k_ref, v_ref, rt_ref, cnt_ref, do_ref, *rest):
        dq_ref, dk_ref, dv_ref = rest[ns:ns + 3]
        if ns:
            at_first, at_last = _first_and_last_step(grid)
            start, finish = _reduce_exchange(rest[:ns], rest[ns + 3:2 * ns + 3], rest[2 * ns + 3:3 * ns + 3],
                                             rest[3 * ns + 3:])
            pl.when(at_first)(start)
        i = pl.program_id(1)

        @pl.when(i == 0)
        def _():
            dk_ref[...] = jnp.zeros_like(dk_ref)
            dv_ref[...] = jnp.zeros_like(dv_ref)

        upto = _tri2(bk, lambda r, c: r <= c)
        before = _tri2(bk, lambda r, c: r < c)
        nkb = (i + 1) * per
        first = nkb - jnp.max(cnt_ref[0]).astype(jnp.int32)

        def step(jb, carry):
            off = pl.multiple_of(jb * bk, bk)
            out = []
            for hh in range(hg):
                keep_sum, g_sum, dq = carry[hh]
                qb, dob = q_ref[:, _lanes(hh)], do_ref[:, _lanes(hh)]
                kb = k_ref[pl.ds(off, bk), _lanes(hh)]
                vb = v_ref[pl.ds(off, bk), _lanes(hh)]
                z, valid, e, log_keep = _sb_block(qb, kb, i, jb, scale)
                log_rest = rt_ref[hh] - keep_sum - _dot_split(log_keep, upto)
                attn = jnp.where(valid, jnp.exp(z + log_keep + log_rest), 0.0)
                g = attn * _dot_nt(dob, vb)
                g_before = g_sum + _dot_split(g, before)
                inv = 1.0 / (1.0 + e)
                sig = jnp.where(z >= 0, inv, e * inv)
                dz = (jnp.where(valid, g * (1.0 - sig) - g_before * sig, 0.0) * scale).astype(BF16)
                dk_ref[pl.ds(off, bk), _lanes(hh)] += _dot_tn(dz, qb)
                dv_ref[pl.ds(off, bk), _lanes(hh)] += _dot_tn(attn.astype(BF16), dob)
                out.append((keep_sum + jnp.sum(log_keep, axis=-1, keepdims=True),
                            g_sum + jnp.sum(g, axis=-1, keepdims=True),
                            dq + _dot(dz, kb)))
            return tuple(out)

        zero = jnp.zeros((bq, 1), F32)
        res = lax.fori_loop(first, nkb, step, tuple((zero, zero, jnp.zeros((bq, dh), F32)) for _ in range(hg)))
        for hh in range(hg):
            dq_ref[:, _lanes(hh)] = res[hh][2]
        if ns:
            pl.when(at_last)(finish)

    return pl.pallas_call(
        body, name=name, grid=grid,
        in_specs=[pl.BlockSpec((bq, 128), lambda h, i: (i, h)),
                  pl.BlockSpec((rows, 128), lambda h, i: (0, ATT_TILES + h)),
                  pl.BlockSpec((rows, 128), lambda h, i: (0, 2 * ATT_TILES + h)),
                  pl.BlockSpec((hg, bq, 1), lambda h, i: (h, i, 0)),
                  pl.BlockSpec((hg, bq, 1), lambda h, i: (h, i, 0)),
                  pl.BlockSpec((bq, 128), lambda h, i: (i, h))] + [HBM] * ns,
        out_specs=[pl.BlockSpec((bq, 128), lambda h, i: (i, h)),
                   pl.BlockSpec((rows, 128), lambda h, i: (0, h)),
                   pl.BlockSpec((rows, 128), lambda h, i: (0, h))] + [HBM] * (2 * ns),
        out_shape=[jax.ShapeDtypeStruct((rows, GROUP), F32)] * 3
        + [jax.ShapeDtypeStruct(s.shape, s.dtype) for s in parts] * 2,
        scratch_shapes=_reduce_sems(ns) if ns else [],
        compiler_params=_params("arbitrary", "arbitrary"),
    )(qkv, qkv, qkv, rt, cnt, do, *parts)


def _head_sum(x, ones_bd):
    return _dot_split(x, ones_bd)


def _rwkv_pre(p, p_prev, mu, w0, a0, k_k, k_a, w_up, a_up, g_up, ones_bd):
    xs = p + (p_prev - p) * mu
    r = xs[:, :GROUP]
    k0 = xs[:, GROUP:2 * GROUP]
    v = xs[:, 2 * GROUP:3 * GROUP]
    lo = xs[:, 3 * GROUP:]
    wa = w0 + _dot(jnp.tanh(lo).astype(BF16), w_up.astype(BF16))
    w = -(jnp.maximum(-wa, 0.0) + jnp.log(1.0 + jnp.exp(-jnp.abs(wa)))) - 0.5
    log_decay = -jnp.exp(w)
    alpha = _sigmoid(a0 + _dot(lo.astype(BF16), a_up.astype(BF16)))
    gate = _dot(_sigmoid(lo).astype(BF16), g_up.astype(BF16))
    kk = k0 * k_k
    kk = kk * lax.rsqrt(jnp.maximum(_head_sum(kk * kk, ones_bd), 1e-24))
    k = k0 * (1.0 + (alpha - 1.0) * k_a)
    return r, log_decay, k, v, -kk, kk * alpha, gate


def _rwkv_post(y, r, k, v, gate, lnx_w, lnx_b, r_k, ones_bd):
    mean = _head_sum(y, ones_bd) * (1.0 / HEAD)
    yc = y - mean
    var = _head_sum(yc * yc, ones_bd) * (1.0 / HEAD)
    yn = yc * lax.rsqrt(var + LNX_EPS) * lnx_w + lnx_b
    bonus = _head_sum(r * k * r_k, ones_bd) * v
    return (yn + bonus) * gate


TOKEN_TILE = 128
_PRE_VEC = 5
_PRE_MAT = 3


def _split_heads(o_ref, val):
    for h in range(N_HEADS):
        o_ref[h] = val[:, _lanes(h)]


def _merge_heads(ref):
    return jnp.concatenate([ref[h] for h in range(N_HEADS)], axis=1)


def _head_spec(tm):
    return pl.BlockSpec((N_HEADS, tm, HEAD), lambda i: (0, i, 0))


def _rwkv_pre_fwd(name, p, p_prev, vecs, mats, ones_bd):
    rows = p.shape[0]
    tm = TOKEN_TILE
    row_spec = lambda w: pl.BlockSpec((tm, w), lambda i: (i, 0))
    full = lambda a: pl.BlockSpec(a.shape, lambda i: (0,) * a.ndim)

    def body(p_ref, pp_ref, *refs):
        ins = [r[...] for r in refs[:_PRE_VEC + _PRE_MAT + 1]]
        outs = refs[_PRE_VEC + _PRE_MAT + 1:]
        vals = _rwkv_pre(p_ref[...], pp_ref[...], *ins)
        for o_ref, val in zip(outs[:6], vals[:6]):
            _split_heads(o_ref, val)
        for o_ref, val in zip(outs[6:], (vals[0], vals[2], vals[3], vals[6])):
            o_ref[...] = val

    return pl.pallas_call(
        body, name=name, grid=(rows // tm,),
        in_specs=[row_spec(RW_COLS), row_spec(RW_COLS)] + [full(a) for a in (*vecs, *mats, ones_bd)],
        out_specs=[_head_spec(tm)] * 6 + [row_spec(GROUP)] * 4,
        out_shape=([jax.ShapeDtypeStruct((N_HEADS, rows, HEAD), F32)] * 6
                   + [jax.ShapeDtypeStruct((rows, GROUP), F32)] * 4),
        compiler_params=_params("arbitrary"),
    )(p, p_prev, *vecs, *mats, ones_bd)


def _rwkv_pre_bwd(name, p, p_prev, vecs, mats, ones_bd, cts_scan, ct_gate, cts_b):
    rows = p.shape[0]
    tm = TOKEN_TILE
    n_par = _PRE_VEC + _PRE_MAT
    row_spec = lambda w: pl.BlockSpec((tm, w), lambda i: (i, 0))
    full = lambda a: pl.BlockSpec(a.shape, lambda i: (0,) * a.ndim)

    def body(*refs):
        p_ref, pp_ref = refs[0], refs[1]
        par = [r[...] for r in refs[2:2 + n_par]]
        ones = refs[2 + n_par][...]
        cta = [_merge_heads(r) for r in refs[3 + n_par:9 + n_par]] + [refs[9 + n_par][...]]
        ctb = [r[...] for r in refs[10 + n_par:13 + n_par]]
        outs = refs[13 + n_par:]
        ct = (cta[0] + ctb[0], cta[1], cta[2] + ctb[1], cta[3] + ctb[2], cta[4], cta[5], cta[6])
        _, vjp = jax.vjp(lambda pv, ppv, *pr: _rwkv_pre(pv, ppv, *pr, ones), p_ref[...], pp_ref[...], *par)
        grads = vjp(ct)
        outs[0][...] = grads[0]
        outs[1][...] = grads[1]

        @pl.when(pl.program_id(0) == 0)
        def _():
            for o_ref in outs[2:]:
                o_ref[...] = jnp.zeros_like(o_ref)

        for o_ref, gval in zip(outs[2:], grads[2:]):
            o_ref[...] += gval

    par_arrays = (*vecs, *mats)
    return pl.pallas_call(
        body, name=name, grid=(rows // tm,),
        in_specs=([row_spec(RW_COLS)] * 2 + [full(a) for a in (*par_arrays, ones_bd)]
                  + [_head_spec(tm)] * 6 + [row_spec(GROUP)] * 4),
        out_specs=[row_spec(RW_COLS)] * 2 + [full(a) for a in par_arrays],
        out_shape=([jax.ShapeDtypeStruct((rows, RW_COLS), F32)] * 2
                   + [jax.ShapeDtypeStruct(a.shape, F32) for a in par_arrays]),
        compiler_params=_params("arbitrary"),
    )(p, p_prev, *par_arrays, ones_bd, *cts_scan, ct_gate, *cts_b)


def _rwkv_post_fwd(name, y, r, k, v, gate, vecs, ones_bd):
    rows = r.shape[0]
    tm = TOKEN_TILE
    row_spec = pl.BlockSpec((tm, GROUP), lambda i: (i, 0))
    full = lambda a: pl.BlockSpec(a.shape, lambda i: (0,) * a.ndim)

    def body(y_ref, *refs):
        vals = [r_[...] for r_ in refs[:-1]]
        refs[-1][...] = _rwkv_post(_merge_heads(y_ref), *vals)

    return pl.pallas_call(
        body, name=name, grid=(rows // tm,),
        in_specs=[_head_spec(tm)] + [row_spec] * 4 + [full(a) for a in (*vecs, ones_bd)],
        out_specs=row_spec,
        out_shape=jax.ShapeDtypeStruct((rows, GROUP), F32),
        compiler_params=_params("arbitrary"),
    )(y, r, k, v, gate, *vecs, ones_bd)


def _rwkv_post_bwd(name, y, r, k, v, gate, vecs, ones_bd, dout):
    rows = r.shape[0]
    tm = TOKEN_TILE
    row_spec = pl.BlockSpec((tm, GROUP), lambda i: (i, 0))
    full = lambda a: pl.BlockSpec(a.shape, lambda i: (0,) * a.ndim)

    def body(y_ref, *refs):
        vals = [_merge_heads(y_ref)] + [r_[...] for r_ in refs[:7]]
        ones = refs[7][...]
        dout_v = refs[8][...]
        outs = refs[9:]
        _, vjp = jax.vjp(lambda *a: _rwkv_post(*a, ones), *vals)
        grads = vjp(dout_v)
        _split_heads(outs[0], grads[0])
        for o_ref, gval in zip(outs[1:5], grads[1:5]):
            o_ref[...] = gval

        @pl.when(pl.program_id(0) == 0)
        def _():
            for o_ref in outs[5:]:
                o_ref[...] = jnp.zeros_like(o_ref)

        for o_ref, gval in zip(outs[5:], grads[5:]):
            o_ref[...] += gval

    return pl.pallas_call(
        body, name=name, grid=(rows // tm,),
        in_specs=[_head_spec(tm)] + [row_spec] * 4 + [full(a) for a in (*vecs, ones_bd)] + [row_spec],
        out_specs=[_head_spec(tm)] + [row_spec] * 4 + [full(a) for a in vecs],
        out_shape=([jax.ShapeDtypeStruct((N_HEADS, rows, HEAD), F32)] + [jax.ShapeDtypeStruct((rows, GROUP), F32)] * 4
                   + [jax.ShapeDtypeStruct(a.shape, F32) for a in vecs]),
        compiler_params=_params("arbitrary"),
    )(y, r, k, v, gate, *vecs, ones_bd, dout)


_NN = (((2,), (1,)), ((0,), (0,)))
_NT = (((2,), (2,)), ((0,), (0,)))
_TN = (((1,), (1,)), ((0,), (0,)))


_BWD_FORMS = {"nn": (("nt", False), ("tn", False)),
              "nt": (("nn", False), ("tn", True)),
              "tn": (("nt", True), ("nn", False))}
_DIMS = {"nn": _NN, "nt": _NT, "tn": _TN}


def _bdot(a, b, form):
    return lax.dot_general(a.astype(BF16), b.astype(BF16), _DIMS[form], preferred_element_type=F32)


@functools.partial(jax.custom_vjp, nondiff_argnums=(2,))
def _bmm(a, b, form):
    return _bdot(a, b, form)


def _bmm_fwd(a, b, form):
    return _bdot(a, b, form), (a.astype(BF16), b.astype(BF16))


def _bmm_bwd(form, res, dc):
    a, b = res
    (fa, swap_a), (fb, swap_b) = _BWD_FORMS[form]
    da = _bdot(b, dc, fa) if swap_a else _bdot(dc, b, fa)
    db = _bdot(dc, a, fb) if swap_b else _bdot(a, dc, fb)
    return da, db


_bmm.defvjp(_bmm_fwd, _bmm_bwd)


@jax.custom_vjp
def _cumsum_steps(x):
    return _tri_apply(x, lambda r, c: r >= c)


def _tri_apply(x, cmp):
    nh, c, _ = x.shape
    tri = cmp(lax.broadcasted_iota(jnp.int32, (c, c), 0), lax.broadcasted_iota(jnp.int32, (c, c), 1))
    tri = jnp.broadcast_to(tri.astype(BF16)[None], (nh, c, c))
    hi, lo = _split2(x)
    return (lax.dot_general(tri, hi, _NN, preferred_element_type=F32)
            + lax.dot_general(tri, lo, _NN, preferred_element_type=F32))


_cumsum_steps.defvjp(lambda x: (_cumsum_steps(x), None), lambda _, d: (_tri_apply(d, lambda r, c: r <= c),))


def _chunk(state, r, log_w, k, v, a, b):
    nh, c, _ = r.shape
    row = lax.broadcasted_iota(jnp.int32, (c, c), 0)
    col = lax.broadcasted_iota(jnp.int32, (c, c), 1)
    cum = _cumsum_steps(log_w)
    mid = cum[:, c // 2 - 1:c // 2, :]
    a_t = a * jnp.exp(cum - log_w - mid)
    r_t = r * jnp.exp(cum - mid)
    back = jnp.exp(mid - cum)
    b_t = b * back
    k_t = k * back
    strict, incl = (row > col)[None], (row >= col)[None]
    ar = jnp.concatenate([a_t, r_t], axis=1)
    on_b = _bmm(ar, b_t, "nt")
    on_k = _bmm(ar, k_t, "nt")
    n_mat = jnp.where(strict, on_b[:, :c], 0.0)
    p_mat = jnp.where(incl, on_b[:, c:], 0.0)
    m_mat = jnp.where(strict, on_k[:, :c], 0.0)
    q_mat = jnp.where(incl, on_k[:, c:], 0.0)
    inv, power, span = n_mat, _bmm(n_mat, n_mat, "nn"), 2
    while span < c:
        both = _bmm(jnp.concatenate([power, inv], axis=1), power, "nn")
        inv = inv + power + both[:, c:]
        power = both[:, :c]
        span *= 2
    s_mid = state * jnp.swapaxes(jnp.exp(mid), 1, 2)
    x = _bmm(jnp.concatenate([a_t, m_mat], axis=2), jnp.concatenate([s_mid, v], axis=1), "nn")
    u = x + _bmm(inv, x, "nn")
    y = _bmm(jnp.concatenate([r_t, p_mat, q_mat], axis=2), jnp.concatenate([s_mid, u, v], axis=1), "nn")
    grown = _bmm(jnp.concatenate([b_t, k_t], axis=1), jnp.concatenate([u, v], axis=1), "tn")
    s_new = (s_mid + grown) * jnp.swapaxes(jnp.exp(cum[:, c - 1:c, :] - mid), 1, 2)
    return y, s_new


def _scan_fwd(name, ops):
    nh, rows, dh = ops[0].shape
    nc = rows // CHUNK
    spec = pl.BlockSpec((nh, CHUNK, dh), lambda c: (0, c, 0))

    def body(r_ref, w_ref, k_ref, v_ref, a_ref, b_ref, y_ref, st_ref, state):
        @pl.when(pl.program_id(0) == 0)
        def _():
            state[...] = jnp.zeros_like(state)

        st_ref[0] = state[...]
        y, s_new = _chunk(state[...], r_ref[...], w_ref[...], k_ref[...], v_ref[...], a_ref[...], b_ref[...])
        y_ref[...] = y
        state[...] = s_new

    return pl.pallas_call(
        body, name=name, grid=(nc,),
        in_specs=[spec] * 6,
        out_specs=[spec, pl.BlockSpec((1, nh, dh, dh), lambda c: (c, 0, 0, 0))],
        out_shape=[jax.ShapeDtypeStruct((nh, rows, dh), F32), jax.ShapeDtypeStruct((nc, nh, dh, dh), F32)],
        scratch_shapes=[pltpu.VMEM((nh, dh, dh), F32)],
        compiler_params=_params("arbitrary"),
    )(*ops)


def _scan_bwd(name, ops, states, dy):
    nh, rows, dh = ops[0].shape
    nc = rows // CHUNK
    spec = pl.BlockSpec((nh, CHUNK, dh), lambda c: (0, nc - 1 - c, 0))

    def body(r_ref, w_ref, k_ref, v_ref, a_ref, b_ref, st_ref, dy_ref, *rest):
        outs, dstate = rest[:6], rest[6]

        @pl.when(pl.program_id(0) == 0)
        def _():
            dstate[...] = jnp.zeros_like(dstate)

        _, vjp = jax.vjp(_chunk, st_ref[0], r_ref[...], w_ref[...], k_ref[...], v_ref[...], a_ref[...], b_ref[...])
        grads = vjp((dy_ref[...], dstate[...]))
        dstate[...] = grads[0]
        for o_ref, gval in zip(outs, grads[1:]):
            o_ref[...] = gval

    return pl.pallas_call(
        body, name=name, grid=(nc,),
        in_specs=[spec] * 6 + [pl.BlockSpec((1, nh, dh, dh), lambda c: (nc - 1 - c, 0, 0, 0)), spec],
        out_specs=[spec] * 6,
        out_shape=[jax.ShapeDtypeStruct((nh, rows, dh), F32)] * 6,
        scratch_shapes=[pltpu.VMEM((nh, dh, dh), F32)],
        compiler_params=_params("arbitrary"),
    )(*ops, states, dy)


def _shift_down(x):
    return jnp.concatenate([jnp.zeros((1, x.shape[1]), x.dtype), x[:-1]], axis=0)


def _shift_up(x):
    return jnp.concatenate([x[1:], jnp.zeros((1, x.shape[1]), x.dtype)], axis=0)


def _pad_rows(x, rows):
    return jnp.concatenate([x, jnp.zeros((rows - x.shape[0],) + x.shape[1:], x.dtype)], axis=0)


def _pad_cols(x, cols):
    return jnp.concatenate([x, jnp.zeros(x.shape[:-1] + (cols - x.shape[-1],), x.dtype)], axis=-1)


def _lora_pad(w_up, a_up, g_up):
    z = lambda n: jnp.zeros((n, GROUP), F32)
    return (jnp.concatenate([w_up, z(LORA_PAD - LORA_W)], 0),
            jnp.concatenate([z(LORA_W), a_up, z(LORA_PAD - LORA_W - LORA_A)], 0),
            jnp.concatenate([z(LORA_W + LORA_A), g_up, z(LORA_PAD - LORA_W - LORA_A - LORA_G)], 0))


MID = ['w_in']
LATE = ['ffn2_w_gate', 'ffn2_w_up', 'ffn2_w_down', 'w_out']


def _local_step(x, tgt, w, late=None):
    d = x.shape[1]
    zeros = jnp.zeros((META_PAD, d), F32)
    h0 = jnp.concatenate([zeros, w["meta_tokens"], x], axis=0)
    tgt_p = jnp.concatenate([jnp.zeros((ROW0, d), F32), tgt], axis=0)
    ones_bd = ((lax.broadcasted_iota(jnp.int32, (2 * GROUP, GROUP), 0) % GROUP) // HEAD
               == lax.broadcasted_iota(jnp.int32, (2 * GROUP, GROUP), 1) // HEAD).astype(BF16)
    pre_vecs = (_pad_cols(w["rwkv_mu"], RW_COLS), w["rwkv_w0"], w["rwkv_a0"], w["rwkv_k_k"], w["rwkv_k_a"])
    pre_mats = _lora_pad(w["rwkv_w_up"], w["rwkv_a_up"], w["rwkv_g_up"])
    post_vecs = (w["rwkv_lnx_w"], w["rwkv_lnx_b"], w["rwkv_r_k"].reshape(1, GROUP))

    h1, a1, b1, *gathered = _ffn_fwd("ffn1_fwd", h0, w["ffn1_norm"], w["ffn1_w_gate"], w["ffn1_w_up"],
                                     w["ffn1_w_down"], late and ("gather", late.shards["mid"]))
    if late is not None:
        w = {**w, **late.join("mid", gathered)}
    w_in = _pad_cols(w["w_in"], IN_COLS_PAD)
    qkv, p, n2 = _norm_proj("in_proj", h1, w["mix_norm"], w_in)
    sb, rest_total, visited, *gathered = _sb_fwd("sb_fwd", qkv, late.shards["late"] if late else ())
    if late is not None:
        w = {**w, **late.join("late", gathered)}
    p_prev = _shift_down(p)
    pre = _rwkv_pre_fwd("rwkv_pre_fwd", p, p_prev, pre_vecs, pre_mats, ones_bd)
    scan_ops, token_ops = pre[:6], pre[6:]
    y, states = _scan_fwd("rwkv_scan_fwd", scan_ops)
    rw = _rwkv_post_fwd("rwkv_post_fwd", y, *token_ops, post_vecs, ones_bd)
    h2, mix = _out_proj("out_proj", h1, sb, rw, w["w_out"])
    h3, a2, b2 = _ffn_fwd("ffn2_fwd", h2, w["ffn2_norm"], w["ffn2_w_gate"], w["ffn2_w_up"], w["ffn2_w_down"])
    loss8, dh3, g_final = _loss_head("loss_head", h3, w["final_norm"].reshape(1, d), tgt_p)

    g = {"final_norm": g_final.reshape(d)}
    dh2, da2, db2, s2, n3, dhh3, g["ffn2_norm"] = _ffn_bwd(
        "ffn2_bwd", dh3, h2, w["ffn2_norm"], a2, b2, w["ffn2_w_gate"], w["ffn2_w_up"], w["ffn2_w_down"])
    g["ffn2_w_gate"] = _mm_tn("ffn2_dgate", da2, n3)
    g["ffn2_w_up"] = _mm_tn("ffn2_dup", db2, n3)
    g["ffn2_w_down"] = _mm_tn("ffn2_ddown", s2, dhh3)
    dsb, drw, dh2b = _out_proj_bwd("out_proj_bwd", dh2, w["w_out"])
    g["w_out"] = _mm_tn("out_proj_dw", mix, dh2b)
    dq, dk, dv, *reduced_late = _sb_bwd("sb_bwd", qkv, rest_total, visited, dsb, late.parts("late", g) if late else ())
    post_g = _rwkv_post_bwd("rwkv_post_bwd", y, *token_ops, post_vecs, ones_bd, drw)
    g["rwkv_lnx_w"], g["rwkv_lnx_b"] = post_g[5], post_g[6]
    g["rwkv_r_k"] = post_g[7].reshape(1, N_HEADS, HEAD)
    scan_g = _scan_bwd("rwkv_scan_bwd", scan_ops, states, post_g[0])
    pre_g = _rwkv_pre_bwd("rwkv_pre_bwd", p, p_prev, pre_vecs, pre_mats, ones_bd, scan_g, post_g[4], post_g[1:4])
    g["rwkv_mu"] = pre_g[2][:, :w["rwkv_mu"].shape[1]]
    g["rwkv_w0"], g["rwkv_a0"], g["rwkv_k_k"], g["rwkv_k_a"] = pre_g[3:7]
    g["rwkv_w_up"] = pre_g[7][:LORA_W]
    g["rwkv_a_up"] = pre_g[8][LORA_W:LORA_W + LORA_A]
    g["rwkv_g_up"] = pre_g[9][LORA_W + LORA_A:LORA_W + LORA_A + LORA_G]
    dp = pre_g[0] + _shift_up(pre_g[1])
    live = (jnp.arange(h0.shape[0]) >= META_PAD)[:, None]
    dproj = jnp.where(live, jnp.concatenate([dq, dk, dv, dp], axis=1), 0.0).astype(BF16)
    g["w_in"] = _mm_tn("in_proj_dw", n2, dproj)[:, :w["w_in"].shape[1]]
    dh1, g["mix_norm"] = _norm_proj_bwd("in_proj_bwd", dproj, w_in, h1, w["mix_norm"], dh2)
    dh0, da1, db1, s1, n1, dhh1, g["ffn1_norm"], *reduced_mid = _ffn_bwd(
        "ffn1_bwd", dh1, h0, w["ffn1_norm"], a1, b1, w["ffn1_w_gate"], w["ffn1_w_up"], w["ffn1_w_down"],
        late and ("reduce", late.parts("mid", g)))
    g["ffn1_w_gate"] = _mm_tn("ffn1_dgate", da1, n1)
    g["ffn1_w_up"] = _mm_tn("ffn1_dup", db1, n1)
    g["ffn1_w_down"] = _mm_tn("ffn1_ddown", s1, dhh1)
    g["meta_tokens"] = dh0[META_PAD:ROW0]
    return loss8[0, 0], dh0[ROW0:], g, {"mid": reduced_mid, "late": reduced_late}


N_CHIPS = 4
N_DEV = 8
HBM = pl.BlockSpec(memory_space=pltpu.HBM)


def _place():
    return lax.axis_index("x"), lax.axis_index("y"), lax.axis_index("c")


def _other_chips(x, y):
    return [(1 - x, y), (x, 1 - y), (1 - x, 1 - y)]


def _gather_sems(n):
    return [pltpu.SemaphoreType.DMA((3 * n,)), pltpu.SemaphoreType.DMA((3 * n,)), pltpu.SemaphoreType.DMA((n,)),
            pltpu.SemaphoreType.DMA((3 * n,)), pltpu.SemaphoreType.DMA((3 * n,))]


def _gather_exchange(ins, outs, sems):
    n = len(ins)
    half = [r.shape[0] // 2 for r in ins]
    send, recv, local, d2d_send, d2d_recv = sems
    x, y, c = _place()
    me = 2 * x + y
    chips = _other_chips(x, y)

    def rows_of(k, h):
        return pl.ds(pl.multiple_of(h * half[k], 8), half[k])

    def own(k):
        return pltpu.make_async_copy(ins[k], outs[k].at[me], local.at[k])

    def copy(j, k, slot):
        return pltpu.make_async_remote_copy(
            src_ref=ins[k].at[rows_of(k, c)], dst_ref=outs[k].at[slot, rows_of(k, c)],
            send_sem=send.at[j * n + k], recv_sem=recv.at[j * n + k],
            device_id=(chips[j][0], chips[j][1], c), device_id_type=MESH)

    def passed(j, k, h):
        slot = 2 * chips[j][0] + chips[j][1]
        return pltpu.make_async_remote_copy(
            src_ref=outs[k].at[slot, rows_of(k, h)], dst_ref=outs[k].at[slot, rows_of(k, h)],
            send_sem=d2d_send.at[j * n + k], recv_sem=d2d_recv.at[j * n + k],
            device_id=(x, y, 1 - c), device_id_type=MESH)

    def start():
        for k in range(n):
            own(k).start()
        for j in range(3):
            for k in range(n):
                copy(j, k, me).start()

    def finish():
        for j in range(3):
            for k in range(n):
                copy(j, k, 2 * chips[j][0] + chips[j][1]).wait_recv()
                passed(j, k, c).start()
        for j in range(3):
            for k in range(n):
                passed(j, k, 1 - c).wait_recv()
        for j in range(3):
            for k in range(n):
                copy(j, k, me).wait_send()
                passed(j, k, c).wait_send()
        for k in range(n):
            own(k).wait()

    return start, finish


def _gather_shards(name, shards):
    n = len(shards)

    def body(*refs):
        start, finish = _gather_exchange(refs[:n], refs[n:2 * n], refs[2 * n:])
        start()
        finish()

    return pl.pallas_call(
        body, name=name,
        in_specs=[HBM] * n, out_specs=[HBM] * n,
        out_shape=[jax.ShapeDtypeStruct((N_CHIPS,) + s.shape, s.dtype) for s in shards],
        scratch_shapes=_gather_sems(n),
    )(*shards)


def _pair_exchange(name, parts):
    n = len(parts)
    half = [s.shape[1] // 2 for s in parts]

    def body(*refs):
        ins, outs = refs[:n], refs[n:2 * n]
        send, recv = refs[2 * n:]
        x, y, c = _place()

        def copy(k):
            rows = pl.ds(pl.multiple_of((1 - c) * half[k], 8), half[k])
            return pltpu.make_async_remote_copy(
                src_ref=ins[k].at[:, rows], dst_ref=outs[k], send_sem=send.at[k], recv_sem=recv.at[k],
                device_id=(x, y, 1 - c), device_id_type=MESH)

        for k in range(n):
            copy(k).start()
        for k in range(n):
            copy(k).wait_recv()
        for k in range(n):
            copy(k).wait_send()

    return pl.pallas_call(
        body, name=name,
        in_specs=[HBM] * n, out_specs=[HBM] * n,
        out_shape=[jax.ShapeDtypeStruct((s.shape[0], s.shape[1] // 2, s.shape[2]), s.dtype) for s in parts],
        scratch_shapes=[pltpu.SemaphoreType.DMA((n,)), pltpu.SemaphoreType.DMA((n,))],
    )(*parts)


def _pair_add(name, part, other):
    nch, rows, cols = part.shape
    half = rows // 2

    def body(p_ref, o_ref, out_ref):
        c = lax.axis_index("c")
        mine = p_ref[0, pl.ds(pl.multiple_of(c * half, 16), half), :]
        out_ref[0] = (mine.astype(F32) + o_ref[0].astype(F32)).astype(out_ref.dtype)

    return pl.pallas_call(
        body, name=name, grid=(nch,),
        in_specs=[pl.BlockSpec((1, rows, cols), lambda j: (j, 0, 0)),
                  pl.BlockSpec((1, half, cols), lambda j: (j, 0, 0))],
        out_specs=pl.BlockSpec((1, half, cols), lambda j: (j, 0, 0)),
        out_shape=jax.ShapeDtypeStruct((nch, half, cols), part.dtype),
        compiler_params=_params("arbitrary"),
    )(part, other)


def _reduce_sems(n):
    return [pltpu.SemaphoreType.DMA((3 * n,)), pltpu.SemaphoreType.DMA((3 * n,)), pltpu.SemaphoreType.DMA((n,)),
            pltpu.SemaphoreType.DMA((n,)), pltpu.SemaphoreType.DMA((n,))]


def _reduce_exchange(ins, got, sib, sems):
    n = len(ins)
    send, recv, local, d2d_send, d2d_recv = sems
    x, y, c = _place()
    me = 2 * x + y
    chips = _other_chips(x, y)

    def own(k):
        return pltpu.make_async_copy(ins[k].at[me], got[k].at[me], local.at[k])

    def copy(j, k, shard, slot):
        return pltpu.make_async_remote_copy(
            src_ref=ins[k].at[shard], dst_ref=got[k].at[slot], send_sem=send.at[j * n + k],
            recv_sem=recv.at[j * n + k], device_id=(chips[j][0], chips[j][1], c), device_id_type=MESH)

    def swap(k):
        return pltpu.make_async_remote_copy(
            src_ref=got[k], dst_ref=sib[k], send_sem=d2d_send.at[k], recv_sem=d2d_recv.at[k],
            device_id=(x, y, 1 - c), device_id_type=MESH)

    def start():
        for k in range(n):
            own(k).start()
        for j in range(3):
            for k in range(n):
                copy(j, k, 2 * chips[j][0] + chips[j][1], me).start()

    def finish():
        for k in range(n):
            own(k).wait()
            for j in range(3):
                copy(j, k, me, 2 * chips[j][0] + chips[j][1]).wait_recv()
            swap(k).start()
        for k in range(n):
            swap(k).wait_recv()
        for j in range(3):
            for k in range(n):
                copy(j, k, me, me).wait_send()
        for k in range(n):
            swap(k).wait_send()

    return start, finish


def _reduce_shards(name, parts):
    n = len(parts)

    def body(*refs):
        start, finish = _reduce_exchange(refs[:n], refs[n:2 * n], refs[2 * n:3 * n], refs[3 * n:])
        start()
        finish()

    return pl.pallas_call(
        body, name=name,
        in_specs=[HBM] * n, out_specs=[HBM] * (2 * n),
        out_shape=[jax.ShapeDtypeStruct(s.shape, s.dtype) for s in parts] * 2,
        scratch_shapes=_reduce_sems(n),
    )(*parts)


def _all_reduce_small(name, vec):
    rows = vec.shape[0]

    def body(v_ref, o_ref, buf, send, recv):
        x, y, c = _place()
        me = 4 * x + 2 * y + c
        peers = [(x ^ (r >> 2), y ^ ((r >> 1) & 1), c ^ (r & 1)) for r in range(1, N_DEV)]

        def copy(r, slot):
            px, py, pc = peers[r]
            return pltpu.make_async_remote_copy(
                src_ref=v_ref, dst_ref=buf.at[slot], send_sem=send.at[r], recv_sem=recv.at[r],
                device_id=(px, py, pc), device_id_type=MESH)

        sent = [copy(r, me) for r in range(N_DEV - 1)]
        for cp in sent:
            cp.start()
        buf[me] = v_ref[...]
        for r in range(N_DEV - 1):
            px, py, pc = peers[r]
            copy(r, 4 * px + 2 * py + pc).wait_recv()
        total = buf[0]
        for dev in range(1, N_DEV):
            total = total + buf[dev]
        o_ref[...] = total
        for cp in sent:
            cp.wait_send()

    return pl.pallas_call(
        body, name=name,
        in_specs=[pl.BlockSpec(memory_space=pltpu.VMEM)], out_specs=pl.BlockSpec(memory_space=pltpu.VMEM),
        out_shape=jax.ShapeDtypeStruct(vec.shape, F32),
        scratch_shapes=[pltpu.VMEM((N_DEV, rows, 128), F32),
                        pltpu.SemaphoreType.DMA((N_DEV - 1,)), pltpu.SemaphoreType.DMA((N_DEV - 1,))],
        compiler_params=pltpu.CompilerParams(vmem_limit_bytes=VMEM_LIMIT),
    )(vec)


def _adamw(w, g, m, v):
    m = ADAM_B1 * m + (1.0 - ADAM_B1) * g
    v = ADAM_B2 * v + (1.0 - ADAM_B2) * (g * g)
    m_hat = m / (1.0 - ADAM_B1 ** ADAM_STEP)
    v_hat = v / (1.0 - ADAM_B2 ** ADAM_STEP)
    return -ADAM_LR * (m_hat / (jnp.sqrt(v_hat) + ADAM_EPS) + ADAM_WD * w), m, v


def _adamw_shard(name, core, w, m, v, got, sib):
    rows, cols = w.shape
    tr = rows // 4
    spec = pl.BlockSpec((tr, cols), lambda i, c_ref: (i, 0))
    spec4 = pl.BlockSpec((N_CHIPS, tr, cols), lambda i, c_ref: (0, i % 2, 0))

    def body(c_ref, w_ref, m_ref, v_ref, got_ref, sib_ref, g_ref, d_ref, mo_ref, vo_ref):
        def four(ref):
            return ((ref[0].astype(F32) + ref[1].astype(F32)) + ref[2].astype(F32)) + ref[3].astype(F32)

        g = jnp.where(pl.program_id(0) // 2 == c_ref[0], four(got_ref), four(sib_ref))
        g_ref[...] = g
        d_ref[...], mo_ref[...], vo_ref[...] = _adamw(w_ref[...], g, m_ref[...], v_ref[...])

    return pl.pallas_call(
        body, name=name,
        grid_spec=pltpu.PrefetchScalarGridSpec(
            num_scalar_prefetch=1, grid=(4,),
            in_specs=[spec, spec, spec, spec4, spec4], out_specs=[spec] * 4),
        out_shape=[jax.ShapeDtypeStruct((rows, cols), F32)] * 4,
        compiler_params=_params("arbitrary"),
    )(core, w, m, v, got, sib)


def _adamw_small(name, w, m, v, g):
    def body(w_ref, m_ref, v_ref, g_ref, d_ref, mo_ref, vo_ref):
        d_ref[...], mo_ref[...], vo_ref[...] = _adamw(w_ref[...], g_ref[...], m_ref[...], v_ref[...])

    return pl.pallas_call(body, name=name, out_shape=[jax.ShapeDtypeStruct(w.shape, F32)] * 3)(w, m, v, g)


def _cast_bf16(name, arrays):
    n = len(arrays)

    def body(*refs):
        for i_ref, o_ref in zip(refs[:n], refs[n:]):
            o_ref[...] = i_ref[...].astype(BF16)

    return pl.pallas_call(
        body, name=name, out_shape=[jax.ShapeDtypeStruct(a.shape, BF16) for a in arrays],
        compiler_params=pltpu.CompilerParams(vmem_limit_bytes=VMEM_LIMIT),
    )(*arrays)


def _pack(arrays, rows):
    flat = jnp.concatenate([a.reshape(-1) for a in arrays])
    return jnp.concatenate([flat, jnp.zeros((rows * 128 - flat.shape[0],), F32)]).reshape(rows, 128)


def _unpack(packed, shapes):
    flat, out, at = packed.reshape(-1), [], 0
    for s in shapes:
        size = 1
        for dim in s:
            size *= dim
        out.append(flat[at:at + size].reshape(s))
        at += size
    return out


def _rows_for(shapes):
    total = 0
    for s in shapes:
        size = 1
        for dim in s:
            size *= dim
        total += size
    return -(-total // 1024) * 8


WEIGHTS = ['meta_tokens', 'ffn1_norm', 'ffn1_w_gate', 'ffn1_w_up', 'ffn1_w_down', 'mix_norm', 'w_in', 'rwkv_mu',
           'rwkv_w0', 'rwkv_w_up', 'rwkv_a0', 'rwkv_a_up', 'rwkv_g_up', 'rwkv_k_k', 'rwkv_k_a', 'rwkv_r_k',
           'rwkv_lnx_w', 'rwkv_lnx_b', 'w_out', 'ffn2_norm', 'ffn2_w_gate', 'ffn2_w_up', 'ffn2_w_down', 'final_norm']
COL_CUT = ['ffn1_w_gate', 'ffn1_w_up', 'w_in', 'ffn2_w_gate', 'ffn2_w_up']
ROW_CUT = ['ffn1_w_down', 'w_out', 'ffn2_w_down']
SMALL_CUT = ['meta_tokens', 'rwkv_w_up', 'rwkv_a_up', 'rwkv_g_up']
TRANSPOSED = ['ffn1_w_gate', 'ffn1_w_up', 'ffn2_w_gate', 'ffn2_w_up']
BIG = COL_CUT + ROW_CUT
REPLICATED = [n for n in WEIGHTS if n not in BIG + SMALL_CUT]


def _join_cols(a):
    return a.transpose(1, 0, 2).reshape(a.shape[1], N_CHIPS * a.shape[2])


def _cut_cols(a):
    return a.reshape(a.shape[0], N_CHIPS, a.shape[1] // N_CHIPS).transpose(1, 0, 2)


def _step(x, loss_target, w, m, v):
    two = lambda a: a.reshape(a.shape[-2], a.shape[-1])

    def rows_cut(n, a):
        return jnp.swapaxes(two(a), 0, 1) if n in TRANSPOSED else two(a)

    def as_given(n, a, like):
        return (jnp.swapaxes(a, 0, 1) if n in TRANSPOSED else a).reshape(like.shape)

    col_cut = [n for n in COL_CUT + SMALL_CUT if n not in TRANSPOSED]

    def join(names, gathered):
        return {n: (_join_cols(a) if n in col_cut else a.reshape(-1, a.shape[-1])) for n, a in zip(names, gathered)}

    def pair_sums(tag, names, g):
        parts = [_cut_cols(g[n]) if n in col_cut else g[n].reshape(N_CHIPS, -1, g[n].shape[-1]) for n in names]
        arrived = _pair_exchange("pair_exchange_" + tag, parts)
        return [_pair_add("pair_add_" + n, p, o) for n, p, o in zip(names, parts, arrived)]

    first = [n for n in BIG if n not in MID + LATE]
    groups = {"mid": MID, "late": LATE}
    cast = dict(zip(BIG, _cast_bf16("cast_weights", [rows_cut(n, w[n]) for n in BIG])))
    names = first + SMALL_CUT
    shards = [cast[n] for n in first] + [two(w[n]) for n in SMALL_CUT]
    full = {n: (two(w[n]) if w[n].ndim == 3 else w[n]) for n in REPLICATED}
    full.update(join(names, _gather_shards("gather_weights", shards)))
    full["rwkv_r_k"] = w["rwkv_r_k"]
    full["final_norm"] = w["final_norm"]

    late = types.SimpleNamespace(shards={k: [cast[n] for n in names] for k, names in groups.items()},
                                 join=lambda k, gathered: join(groups[k], gathered),
                                 parts=lambda k, g: pair_sums(k, groups[k], g))

    loss, dx, g, reduced = _local_step(x[0], loss_target[0], full, late)
    loss = lax.psum(loss, ("x", "y", "c"))

    groups["first"] = first
    reduced["first"] = list(_reduce_shards("reduce_gradients", pair_sums("first", first, g)))
    got, sib = {}, {}
    for k, names in groups.items():
        got.update(zip(names, reduced[k][:len(names)]))
        sib.update(zip(names, reduced[k][len(names):]))

    small_names = REPLICATED + SMALL_CUT
    small_shapes = [g[n].shape for n in small_names]
    small = _all_reduce_small("reduce_small", _pack([g[n] for n in small_names], _rows_for(small_shapes)))
    g_small = dict(zip(small_names, _unpack(small, small_shapes)))
    chip = 2 * lax.axis_index("x") + lax.axis_index("y")
    for n in SMALL_CUT:
        width = g_small[n].shape[1] // N_CHIPS
        g_small[n] = lax.dynamic_slice_in_dim(g_small[n], chip * width, width, axis=1)

    grad, delta, new_m, new_v = {}, {}, {}, {}
    core = lax.axis_index("c").astype(jnp.int32).reshape(1)
    for n in BIG:
        outs = _adamw_shard("adamw_" + n, core, rows_cut(n, w[n]), rows_cut(n, m[n]), rows_cut(n, v[n]), got[n], sib[n])
        grad[n], delta[n], new_m[n], new_v[n] = (as_given(n, o, w[n]) for o in outs)
    shapes = [w[n].shape for n in small_names]
    rows = _rows_for(shapes)
    packed = [_pack([t[n] for n in small_names], rows) for t in (w, m, v)]
    g_packed = _pack([g_small[n] for n in small_names], rows)
    outs = [_unpack(o, shapes) for o in _adamw_small("adamw_small", *packed, g_packed)]
    for i, n in enumerate(small_names):
        grad[n] = g_small[n].reshape(w[n].shape)
        delta[n], new_m[n], new_v[n] = outs[0][i], outs[1][i], outs[2][i]
    return loss, dx[None], grad, delta, new_m, new_v


def kernel(x, meta_tokens, ffn1_norm, ffn1_w_gate, ffn1_w_up, ffn1_w_down, mix_norm, w_in, rwkv_mu, rwkv_w0, rwkv_w_up, rwkv_a0, rwkv_a_up, rwkv_g_up, rwkv_k_k, rwkv_k_a, rwkv_r_k, rwkv_lnx_w, rwkv_lnx_b, w_out, ffn2_norm, ffn2_w_gate, ffn2_w_up, ffn2_w_down, final_norm, loss_target, m_meta_tokens, m_ffn1_norm, m_ffn1_w_gate, m_ffn1_w_up, m_ffn1_w_down, m_mix_norm, m_w_in, m_rwkv_mu, m_rwkv_w0, m_rwkv_w_up, m_rwkv_a0, m_rwkv_a_up, m_rwkv_g_up, m_rwkv_k_k, m_rwkv_k_a, m_rwkv_r_k, m_rwkv_lnx_w, m_rwkv_lnx_b, m_w_out, m_ffn2_norm, m_ffn2_w_gate, m_ffn2_w_up, m_ffn2_w_down, m_final_norm, v_meta_tokens, v_ffn1_norm, v_ffn1_w_gate, v_ffn1_w_up, v_ffn1_w_down, v_mix_norm, v_w_in, v_rwkv_mu, v_rwkv_w0, v_rwkv_w_up, v_rwkv_a0, v_rwkv_a_up, v_rwkv_g_up, v_rwkv_k_k, v_rwkv_k_a, v_rwkv_r_k, v_rwkv_lnx_w, v_rwkv_lnx_b, v_w_out, v_ffn2_norm, v_ffn2_w_gate, v_ffn2_w_up, v_ffn2_w_down, v_final_norm):
    w = dict(zip(WEIGHTS, (meta_tokens, ffn1_norm, ffn1_w_gate, ffn1_w_up, ffn1_w_down, mix_norm, w_in, rwkv_mu, rwkv_w0, rwkv_w_up, rwkv_a0, rwkv_a_up, rwkv_g_up, rwkv_k_k, rwkv_k_a, rwkv_r_k, rwkv_lnx_w, rwkv_lnx_b, w_out, ffn2_norm, ffn2_w_gate, ffn2_w_up, ffn2_w_down, final_norm)))
    m = dict(zip(WEIGHTS, (m_meta_tokens, m_ffn1_norm, m_ffn1_w_gate, m_ffn1_w_up, m_ffn1_w_down, m_mix_norm, m_w_in, m_rwkv_mu, m_rwkv_w0, m_rwkv_w_up, m_rwkv_a0, m_rwkv_a_up, m_rwkv_g_up, m_rwkv_k_k, m_rwkv_k_a, m_rwkv_r_k, m_rwkv_lnx_w, m_rwkv_lnx_b, m_w_out, m_ffn2_norm, m_ffn2_w_gate, m_ffn2_w_up, m_ffn2_w_down, m_final_norm)))
    v = dict(zip(WEIGHTS, (v_meta_tokens, v_ffn1_norm, v_ffn1_w_gate, v_ffn1_w_up, v_ffn1_w_down, v_mix_norm, v_w_in, v_rwkv_mu, v_rwkv_w0, v_rwkv_w_up, v_rwkv_a0, v_rwkv_a_up, v_rwkv_g_up, v_rwkv_k_k, v_rwkv_k_a, v_rwkv_r_k, v_rwkv_lnx_w, v_rwkv_lnx_b, v_w_out, v_ffn2_norm, v_ffn2_w_gate, v_ffn2_w_up, v_ffn2_w_down, v_final_norm)))
    loss, grad_x, grad, delta, new_m, new_v = _step(x, loss_target, w, m, v)
    return (loss, grad_x, *[grad[n] for n in WEIGHTS], *[delta[n] for n in WEIGHTS],
            *[new_m[n] for n in WEIGHTS], *[new_v[n] for n in WEIGHTS])
```

```python
import functools
import types

import jax
import jax.numpy as jnp
from jax import lax
from jax.experimental import pallas as pl
from jax.experimental.pallas import tpu as pltpu

F32 = jnp.float32
BF16 = jnp.bfloat16

RMS_EPS = 1e-6
LNX_EPS = 64e-5
N_META = 16
ROW0 = 128
META_PAD = ROW0 - N_META
HEAD = 64
N_HEADS = 8
GROUP = N_HEADS * HEAD
LORA_W, LORA_A, LORA_G = 32, 32, 96
LORA_PAD = 256
RW_COLS = 3 * GROUP + LORA_PAD
IN_COLS_PAD = 3 * GROUP + RW_COLS
ATT_BLOCK = 128
CHUNK = 64
VMEM_LIMIT = 56 * 1024 * 1024

ADAM_LR, ADAM_B1, ADAM_B2, ADAM_EPS, ADAM_WD, ADAM_STEP = 0.001, 0.9, 0.999, 1e-08, 0.01, 10

MESH = pl.DeviceIdType.MESH


def _params(*sem):
    return pltpu.CompilerParams(dimension_semantics=tuple(sem), vmem_limit_bytes=VMEM_LIMIT)


def _dot(a, b):
    return lax.dot_general(a, b, (((1,), (0,)), ((), ())), preferred_element_type=F32)


def _dot_nt(a, b):
    return lax.dot_general(a, b, (((1,), (1,)), ((), ())), preferred_element_type=F32)


def _dot_tn(a, b):
    return lax.dot_general(a, b, (((0,), (0,)), ((), ())), preferred_element_type=F32)


def _split2(x):
    hi = x.astype(BF16)
    return hi, (x - hi.astype(F32)).astype(BF16)


def _sigmoid(x):
    return 1.0 / (1.0 + jnp.exp(-x))


def _rms_fwd(x, g):
    rstd = lax.rsqrt(jnp.mean(x * x, axis=-1, keepdims=True) + RMS_EPS)
    xhat = x * rstd
    return xhat * g, xhat, rstd


def _rms_bwd(dn, xhat, rstd, g):
    dxhat = dn * g
    dx = rstd * (dxhat - xhat * jnp.mean(dxhat * xhat, axis=-1, keepdims=True))
    return dx, jnp.sum(dn * xhat, axis=0, keepdims=True)


def _row_tile(rows):
    return 384 if rows % 384 == 0 else 128


def _half_tile(cols):
    return cols // 2 if cols % 256 == 0 else cols


def _tall_tile(rows, parts):
    return rows // parts if rows % (16 * parts) == 0 else _row_tile(rows)


def _call_with_exchange(name, body, grid, in_specs, out_specs, out_shape, scratch, operands, params, exchange):
    if exchange is None or not exchange[1]:
        return pl.pallas_call(body, name=name, grid=grid, in_specs=in_specs, out_specs=out_specs,
                              out_shape=out_shape, scratch_shapes=scratch, compiler_params=params)(*operands)
    kind, arrays = exchange
    ns, n_in, n_out, n_scr = len(arrays), len(in_specs), len(out_specs), len(scratch)
    if kind == "gather":
        results = [jax.ShapeDtypeStruct((N_CHIPS,) + s.shape, s.dtype) for s in arrays]
        sems = _gather_sems(ns)
    else:
        results = [jax.ShapeDtypeStruct(s.shape, s.dtype) for s in arrays] * 2
        sems = _reduce_sems(ns)
    n_res = len(results)

    def carried(*refs):
        at = n_in + ns + n_out
        sent, landed = refs[n_in:n_in + ns], refs[at:at + n_res]
        own_scratch, sem_refs = refs[at + n_res:at + n_res + n_scr], refs[at + n_res + n_scr:]
        first, last = _first_and_last_step(grid)
        if kind == "gather":
            start, finish = _gather_exchange(sent, landed, sem_refs)
        else:
            start, finish = _reduce_exchange(sent, landed[:ns], landed[ns:], sem_refs)
        pl.when(first)(start)
        body(*refs[:n_in], *refs[n_in + ns:at], *own_scratch)
        pl.when(last)(finish)

    return pl.pallas_call(
        carried, name=name, grid=grid, in_specs=list(in_specs) + [HBM] * ns, out_specs=list(out_specs) + [HBM] * n_res,
        out_shape=list(out_shape) + results, scratch_shapes=list(scratch) + sems, compiler_params=params,
    )(*operands, *arrays)


def _ffn_fwd(name, h, g, wg, wu, wd, exchange=None):
    rows, d = h.shape
    f = wg.shape[0]
    tm, tf = _row_tile(rows), _half_tile(f)
    nj = f // tf

    def body(h_ref, g_ref, wg_ref, wu_ref, wd_ref, ho_ref, a_ref, b_ref, n_sc, acc_sc):
        j = pl.program_id(1)

        @pl.when(j == 0)
        def _():
            n, _, _ = _rms_fwd(h_ref[...], g_ref[...])
            n_sc[...] = n.astype(BF16)
            acc_sc[...] = jnp.zeros_like(acc_sc)

        n = n_sc[...]
        a = _dot_nt(n, wg_ref[...])
        b = _dot_nt(n, wu_ref[...])
        a_ref[...] = a
        b_ref[...] = b
        s = a * _sigmoid(a) * b
        acc_sc[...] += _dot(s.astype(BF16), wd_ref[...])

        @pl.when(j == nj - 1)
        def _():
            ho_ref[...] = h_ref[...] + 0.5 * acc_sc[...]

    return _call_with_exchange(
        name, body, (rows // tm, nj),
        [pl.BlockSpec((tm, d), lambda i, j: (i, 0)),
         pl.BlockSpec((1, d), lambda i, j: (0, 0)),
         pl.BlockSpec((tf, d), lambda i, j: (j, 0)),
         pl.BlockSpec((tf, d), lambda i, j: (j, 0)),
         pl.BlockSpec((tf, d), lambda i, j: (j, 0))],
        [pl.BlockSpec((tm, d), lambda i, j: (i, 0)),
         pl.BlockSpec((tm, tf), lambda i, j: (i, j)),
         pl.BlockSpec((tm, tf), lambda i, j: (i, j))],
        [jax.ShapeDtypeStruct((rows, d), F32),
         jax.ShapeDtypeStruct((rows, f), F32),
         jax.ShapeDtypeStruct((rows, f), F32)],
        [pltpu.VMEM((tm, d), BF16), pltpu.VMEM((tm, d), F32)],
        (h, g, wg, wu, wd), _params("arbitrary", "arbitrary"), exchange)


def _ffn_bwd(name, dh, h, g, a, b, wg, wu, wd, exchange=None):
    rows, d = h.shape
    f = wg.shape[0]
    tm, tf = _row_tile(rows), _half_tile(f)
    ni, nj = rows // tm, f // tf

    def body(dh_ref, h_ref, g_ref, a_ref, b_ref, wg_ref, wu_ref, wd_ref,
             dhi_ref, da_ref, db_ref, s_ref, n_ref, dhh_ref, dg_ref, dn_sc):
        i, j = pl.program_id(0), pl.program_id(1)

        @pl.when(j == 0)
        def _():
            n, _, _ = _rms_fwd(h_ref[...], g_ref[...])
            n_ref[...] = n.astype(BF16)
            dhh_ref[...] = (0.5 * dh_ref[...]).astype(BF16)
            dn_sc[...] = jnp.zeros_like(dn_sc)

        @pl.when((i == 0) & (j == 0))
        def _():
            dg_ref[...] = jnp.zeros_like(dg_ref)

        ds = _dot_nt(dhh_ref[...], wd_ref[...])
        av, bv = a_ref[...], b_ref[...]
        sig = _sigmoid(av)
        silu = av * sig
        s_ref[...] = (silu * bv).astype(BF16)
        db = (ds * silu).astype(BF16)
        da = (ds * bv * (sig * (1.0 + av * (1.0 - sig)))).astype(BF16)
        da_ref[...] = da
        db_ref[...] = db
        dn_sc[...] += _dot(da, wg_ref[...]) + _dot(db, wu_ref[...])

        @pl.when(j == nj - 1)
        def _():
            gv = g_ref[...]
            _, xhat, rstd = _rms_fwd(h_ref[...], gv)
            dx, dg = _rms_bwd(dn_sc[...], xhat, rstd, gv)
            dhi_ref[...] = dh_ref[...] + dx
            dg_ref[...] += dg

    return _call_with_exchange(
        name, body, (ni, nj),
        [pl.BlockSpec((tm, d), lambda i, j: (i, 0)),
         pl.BlockSpec((tm, d), lambda i, j: (i, 0)),
         pl.BlockSpec((1, d), lambda i, j: (0, 0)),
         pl.BlockSpec((tm, tf), lambda i, j: (i, j)),
         pl.BlockSpec((tm, tf), lambda i, j: (i, j)),
         pl.BlockSpec((tf, d), lambda i, j: (j, 0)),
         pl.BlockSpec((tf, d), lambda i, j: (j, 0)),
         pl.BlockSpec((tf, d), lambda i, j: (j, 0))],
        [pl.BlockSpec((tm, d), lambda i, j: (i, 0)),
         pl.BlockSpec((tm, tf), lambda i, j: (i, j)),
         pl.BlockSpec((tm, tf), lambda i, j: (i, j)),
         pl.BlockSpec((tm, tf), lambda i, j: (i, j)),
         pl.BlockSpec((tm, d), lambda i, j: (i, 0)),
         pl.BlockSpec((tm, d), lambda i, j: (i, 0)),
         pl.BlockSpec((1, d), lambda i, j: (0, 0))],
        [jax.ShapeDtypeStruct((rows, d), F32),
         jax.ShapeDtypeStruct((rows, f), BF16),
         jax.ShapeDtypeStruct((rows, f), BF16),
         jax.ShapeDtypeStruct((rows, f), BF16),
         jax.ShapeDtypeStruct((rows, d), BF16),
         jax.ShapeDtypeStruct((rows, d), BF16),
         jax.ShapeDtypeStruct((1, d), F32)],
        [pltpu.VMEM((tm, d), F32)],
        (dh, h, g, a, b, wg, wu, wd), _params("arbitrary", "arbitrary"), exchange)


def _mm_tn(name, a, b):
    k, m = a.shape
    n = b.shape[1]
    tk = _tall_tile(k, 3)
    tm = _half_tile(m) if m > 1024 else m
    tn = _half_tile(n) if n > 1024 else n
    nk = k // tk

    def body(a_ref, b_ref, o_ref, acc):
        kk = pl.program_id(2)

        @pl.when(kk == 0)
        def _():
            acc[...] = jnp.zeros_like(acc)

        acc[...] += _dot_tn(a_ref[...], b_ref[...])

        @pl.when(kk == nk - 1)
        def _():
            o_ref[...] = acc[...].astype(BF16)

    return pl.pallas_call(
        body, name=name, grid=(m // tm, n // tn, nk),
        in_specs=[pl.BlockSpec((tk, tm), lambda i, j, kk: (kk, i)),
                  pl.BlockSpec((tk, tn), lambda i, j, kk: (kk, j))],
        out_specs=pl.BlockSpec((tm, tn), lambda i, j, kk: (i, j)),
        out_shape=jax.ShapeDtypeStruct((m, n), BF16),
        scratch_shapes=[pltpu.VMEM((tm, tn), F32)],
        compiler_params=_params("arbitrary", "arbitrary", "arbitrary"),
    )(a, b)


def _norm_proj(name, h, g, w):
    rows, d = h.shape
    n = w.shape[1]
    split = 3 * GROUP
    tm = _row_tile(rows)

    def body(h_ref, g_ref, w_ref, qkv_ref, p_ref, n_ref):
        nv, _, _ = _rms_fwd(h_ref[...], g_ref[...])
        nb = nv.astype(BF16)
        n_ref[...] = nb
        qkv_ref[...] = _dot(nb, w_ref[:, :split]).astype(BF16)
        p_ref[...] = _dot(nb, w_ref[:, split:])

    return pl.pallas_call(
        body, name=name, grid=(rows // tm,),
        in_specs=[pl.BlockSpec((tm, d), lambda i: (i, 0)),
                  pl.BlockSpec((1, d), lambda i: (0, 0)),
                  pl.BlockSpec((d, n), lambda i: (0, 0))],
        out_specs=[pl.BlockSpec((tm, split), lambda i: (i, 0)),
                   pl.BlockSpec((tm, n - split), lambda i: (i, 0)),
                   pl.BlockSpec((tm, d), lambda i: (i, 0))],
        out_shape=[jax.ShapeDtypeStruct((rows, split), BF16), jax.ShapeDtypeStruct((rows, n - split), F32),
                   jax.ShapeDtypeStruct((rows, d), BF16)],
        compiler_params=_params("arbitrary"),
    )(h, g, w)


def _out_proj(name, h, sb, rw, w):
    rows, d = h.shape
    gw = sb.shape[1]
    tm = _row_tile(rows)

    def body(h_ref, sb_ref, rw_ref, w_ref, o_ref, mix_ref):
        mix_ref[:, :gw] = sb_ref[...].astype(BF16)
        mix_ref[:, gw:] = rw_ref[...].astype(BF16)
        o_ref[...] = h_ref[...] + _dot(mix_ref[...], w_ref[...])

    return pl.pallas_call(
        body, name=name, grid=(rows // tm,),
        in_specs=[pl.BlockSpec((tm, d), lambda i: (i, 0)),
                  pl.BlockSpec((tm, gw), lambda i: (i, 0)),
                  pl.BlockSpec((tm, gw), lambda i: (i, 0)),
                  pl.BlockSpec((2 * gw, d), lambda i: (0, 0))],
        out_specs=[pl.BlockSpec((tm, d), lambda i: (i, 0)),
                   pl.BlockSpec((tm, 2 * gw), lambda i: (i, 0))],
        out_shape=[jax.ShapeDtypeStruct((rows, d), F32), jax.ShapeDtypeStruct((rows, 2 * gw), BF16)],
        compiler_params=_params("arbitrary"),
    )(h, sb, rw, w)


def _out_proj_bwd(name, dh, w):
    rows, d = dh.shape
    k = w.shape[0]
    tm = _row_tile(rows)

    def body(dh_ref, w_ref, dsb_ref, drw_ref, dhb_ref):
        dhb = dh_ref[...].astype(BF16)
        dhb_ref[...] = dhb
        dsb_ref[...] = _dot_nt(dhb, w_ref[:GROUP, :]).astype(BF16)
        drw_ref[...] = _dot_nt(dhb, w_ref[GROUP:, :])

    return pl.pallas_call(
        body, name=name, grid=(rows // tm,),
        in_specs=[pl.BlockSpec((tm, d), lambda i: (i, 0)),
                  pl.BlockSpec((k, d), lambda i: (0, 0))],
        out_specs=[pl.BlockSpec((tm, GROUP), lambda i: (i, 0)),
                   pl.BlockSpec((tm, GROUP), lambda i: (i, 0)),
                   pl.BlockSpec((tm, d), lambda i: (i, 0))],
        out_shape=[jax.ShapeDtypeStruct((rows, GROUP), BF16), jax.ShapeDtypeStruct((rows, GROUP), F32),
                   jax.ShapeDtypeStruct((rows, d), BF16)],
        compiler_params=_params("arbitrary"),
    )(dh, w)


def _norm_proj_bwd(name, dproj, w, h, g, dh):
    rows, n = dproj.shape
    d = w.shape[0]
    tm = _row_tile(rows)

    def body(dp_ref, w_ref, h_ref, g_ref, dh_ref, o_ref, dg_ref):
        @pl.when(pl.program_id(0) == 0)
        def _():
            dg_ref[...] = jnp.zeros_like(dg_ref)

        dn = _dot_nt(dp_ref[...], w_ref[...])
        gv = g_ref[...]
        _, xhat, rstd = _rms_fwd(h_ref[...], gv)
        dx, dg = _rms_bwd(dn, xhat, rstd, gv)
        o_ref[...] = dh_ref[...] + dx
        dg_ref[...] += dg

    return pl.pallas_call(
        body, name=name, grid=(rows // tm,),
        in_specs=[pl.BlockSpec((tm, n), lambda i: (i, 0)),
                  pl.BlockSpec((d, n), lambda i: (0, 0)),
                  pl.BlockSpec((tm, d), lambda i: (i, 0)),
                  pl.BlockSpec((1, d), lambda i: (0, 0)),
                  pl.BlockSpec((tm, d), lambda i: (i, 0))],
        out_specs=[pl.BlockSpec((tm, d), lambda i: (i, 0)),
                   pl.BlockSpec((1, d), lambda i: (0, 0))],
        out_shape=[jax.ShapeDtypeStruct((rows, d), F32), jax.ShapeDtypeStruct((1, d), F32)],
        compiler_params=_params("arbitrary"),
    )(dproj, w, h, g, dh)


def _loss_head(name, h, g, tgt):
    rows, d = h.shape
    tm = _row_tile(rows)

    def body(h_ref, g_ref, t_ref, loss_ref, dh_ref, dg_ref):
        i = pl.program_id(0)

        @pl.when(i == 0)
        def _():
            loss_ref[...] = jnp.zeros_like(loss_ref)
            dg_ref[...] = jnp.zeros_like(dg_ref)

        gv = g_ref[...]
        y, xhat, rstd = _rms_fwd(h_ref[...], gv)
        row = i * tm + lax.broadcasted_iota(jnp.int32, (tm, 1), 0)
        diff = jnp.where(row >= ROW0, y - t_ref[...], 0.0)
        part = 0.5 * jnp.sum(jnp.sum(diff * diff, axis=-1, keepdims=True), axis=0, keepdims=True) / d
        loss_ref[...] += jnp.broadcast_to(part, loss_ref.shape)
        dx, dg = _rms_bwd(diff / d, xhat, rstd, gv)
        dh_ref[...] = dx
        dg_ref[...] += dg

    return pl.pallas_call(
        body, name=name, grid=(rows // tm,),
        in_specs=[pl.BlockSpec((tm, d), lambda i: (i, 0)),
                  pl.BlockSpec((1, d), lambda i: (0, 0)),
                  pl.BlockSpec((tm, d), lambda i: (i, 0))],
        out_specs=[pl.BlockSpec((8, 128), lambda i: (0, 0)),
                   pl.BlockSpec((tm, d), lambda i: (i, 0)),
                   pl.BlockSpec((1, d), lambda i: (0, 0))],
        out_shape=[jax.ShapeDtypeStruct((8, 128), F32),
                   jax.ShapeDtypeStruct((rows, d), F32),
                   jax.ShapeDtypeStruct((1, d), F32)],
        compiler_params=_params("arbitrary"),
    )(h, g, tgt)


def _sb_block(qb, kb, q0, jb, scale):
    bq, bk = qb.shape[0], kb.shape[0]
    z = _dot_nt(qb, kb) * scale
    qpos = q0 + lax.broadcasted_iota(jnp.int32, (bq, bk), 0)
    kpos = jb * bk + lax.broadcasted_iota(jnp.int32, (bq, bk), 1)
    valid = (kpos < qpos) & (kpos >= META_PAD)
    e = jnp.exp(-jnp.abs(z))
    log_keep = jnp.where(valid, -(jnp.maximum(z, 0.0) + jnp.log(1.0 + e)), 0.0)
    return z, valid, e, log_keep


def _tri2(n, cmp):
    r = lax.broadcasted_iota(jnp.int32, (2 * n, n), 0) % n
    c = lax.broadcasted_iota(jnp.int32, (2 * n, n), 1)
    return cmp(r, c).astype(BF16)


def _dot_split(x, t2):
    hi, lo = _split2(x)
    return _dot(jnp.concatenate([hi, lo], axis=1), t2)


ATT_HEADS = 128 // HEAD
ATT_CUT = -104.0
ATT_TILES = GROUP // 128


def _lanes(hh):
    return slice(hh * HEAD, (hh + 1) * HEAD)


def _first_and_last_step(grid):
    here = [pl.program_id(a) for a in range(len(grid))]
    first, last = here[0] == 0, here[0] == grid[0] - 1
    for a in range(1, len(grid)):
        first, last = first & (here[a] == 0), last & (here[a] == grid[a] - 1)
    return first, last


def _sb_fwd(name, qkv, shards=()):
    rows = qkv.shape[0]
    nh, dh = N_HEADS, HEAD
    bq, bk, hg = _row_tile(rows), ATT_BLOCK, ATT_HEADS
    per = bq // bk
    scale = dh ** -0.5
    ns = len(shards)
    grid = (nh // hg, rows // bq)

    def body(q_ref, k_ref, v_ref, *rest):
        o_ref, rt_ref, cnt_ref = rest[ns:ns + 3]
        if ns:
            first, last = _first_and_last_step(grid)
            start, finish = _gather_exchange(rest[:ns], rest[ns + 3:2 * ns + 3], rest[2 * ns + 3:])
            pl.when(first)(start)
        i = pl.program_id(1)
        after = _tri2(bk, lambda r, c: r > c)
        nkb = (i + 1) * per

        def live(state):
            n, carry = state
            top = jnp.max(carry[0][0])
            for hh in range(1, hg):
                top = jnp.maximum(top, jnp.max(carry[hh][0]))
            return (n < nkb) & (top >= ATT_CUT)

        def visit(carry, jb, r0):
            off = pl.multiple_of(jb * bk, bk)
            out = []
            for hh in range(hg):
                rest, acc = carry[hh]
                kb = k_ref[pl.ds(off, bk), _lanes(hh)]
                vb = v_ref[pl.ds(off, bk), _lanes(hh)]
                z, valid, _, log_keep = _sb_block(q_ref[r0:, _lanes(hh)], kb, i * bq + r0, jb, scale)
                log_rest = rest[r0:] + _dot_split(log_keep, after)
                attn = jnp.where(valid, jnp.exp(z + log_keep + log_rest), 0.0)
                new_rest = rest[r0:] + jnp.sum(log_keep, axis=-1, keepdims=True)
                new_acc = acc[r0:] + _dot(attn.astype(BF16), vb)
                if r0:
                    new_rest = jnp.concatenate([rest[:r0], new_rest], axis=0)
                    new_acc = jnp.concatenate([acc[:r0], new_acc], axis=0)
                out.append((new_rest, new_acc))
            return tuple(out)

        carry = tuple((jnp.zeros((bq, 1), F32), jnp.zeros((bq, dh), F32)) for _ in range(hg))
        for dgl in reversed(range(per)):
            carry = visit(carry, i * per + dgl, dgl * bk)
        n, res = lax.while_loop(live, lambda s: (s[0] + 1, visit(s[1], nkb - 1 - s[0], 0)), (jnp.int32(per), carry))
        for hh in range(hg):
            rt_ref[hh] = res[hh][0]
            o_ref[:, _lanes(hh)] = res[hh][1]
            cnt_ref[hh] = jnp.full((bq, 1), n, F32)
        if ns:
            pl.when(last)(finish)

    return pl.pallas_call(
        body, name=name, grid=grid,
        in_specs=[pl.BlockSpec((bq, 128), lambda h, i: (i, h)),
                  pl.BlockSpec((rows, 128), lambda h, i: (0, ATT_TILES + h)),
                  pl.BlockSpec((rows, 128), lambda h, i: (0, 2 * ATT_TILES + h))] + [HBM] * ns,
        out_specs=[pl.BlockSpec((bq, 128), lambda h, i: (i, h)),
                   pl.BlockSpec((hg, bq, 1), lambda h, i: (h, i, 0)),
                   pl.BlockSpec((hg, bq, 1), lambda h, i: (h, i, 0))] + [HBM] * ns,
        out_shape=[jax.ShapeDtypeStruct((rows, GROUP), F32), jax.ShapeDtypeStruct((nh, rows, 1), F32),
                   jax.ShapeDtypeStruct((nh, rows, 1), F32)]
        + [jax.ShapeDtypeStruct((N_CHIPS,) + s.shape, s.dtype) for s in shards],
        scratch_shapes=_gather_sems(ns) if ns else [],
        compiler_params=_params("arbitrary", "arbitrary"),
    )(qkv, qkv, qkv, *shards)


def _sb_bwd(name, qkv, rt, cnt, do, parts=()):
    rows = qkv.shape[0]
    nh, dh = N_HEADS, HEAD
    bq, bk, hg = _row_tile(rows), ATT_BLOCK, ATT_HEADS
    per = bq // bk
    scale = dh ** -0.5
    ns = len(parts)
    grid = (nh // hg, rows // bq)

    def body(q_ref, k_ref, v_ref, rt_ref, cnt_ref, do_ref, *rest):
        dq_ref, dk_ref, dv_ref = rest[ns:ns + 3]
        if ns:
            at_first, at_last = _first_and_last_step(grid)
            start, finish = _reduce_exchange(rest[:ns], rest[ns + 3:2 * ns + 3], rest[2 * ns + 3:3 * ns + 3],
                                             rest[3 * ns + 3:])
            pl.when(at_first)(start)
        i = pl.program_id(1)

        @pl.when(i == 0)
        def _():
            dk_ref[...] = jnp.zeros_like(dk_ref)
            dv_ref[...] = jnp.zeros_like(dv_ref)

        upto = _tri2(bk, lambda r, c: r <= c)
        before = _tri2(bk, lambda r, c: r < c)
        nkb = (i + 1) * per
        first = nkb - jnp.max(cnt_ref[0]).astype(jnp.int32)

        def visit(carry, jb, r0):
            off = pl.multiple_of(jb * bk, bk)
            out = []
            for hh in range(hg):
                keep_sum, g_sum, dq = carry[hh]
                qb, dob = q_ref[r0:, _lanes(hh)], do_ref[r0:, _lanes(hh)]
                kb = k_ref[pl.ds(off, bk), _lanes(hh)]
                vb = v_ref[pl.ds(off, bk), _lanes(hh)]
                z, valid, e, log_keep = _sb_block(qb, kb, i * bq + r0, jb, scale)
                log_rest = rt_ref[hh, r0:, :] - keep_sum[r0:] - _dot_split(log_keep, upto)
                attn = jnp.where(valid, jnp.exp(z + log_keep + log_rest), 0.0)
                g = attn * _dot_nt(dob, vb)
                g_before = g_sum[r0:] + _dot_split(g, before)
                inv = 1.0 / (1.0 + e)
                sig = jnp.where(z >= 0, inv, e * inv)
                dz = (jnp.where(valid, g * (1.0 - sig) - g_before * sig, 0.0) * scale).astype(BF16)
                dk_ref[pl.ds(off, bk), _lanes(hh)] += _dot_tn(dz, qb)
                dv_ref[pl.ds(off, bk), _lanes(hh)] += _dot_tn(attn.astype(BF16), dob)
                new = (keep_sum[r0:] + jnp.sum(log_keep, axis=-1, keepdims=True),
                       g_sum[r0:] + jnp.sum(g, axis=-1, keepdims=True),
                       dq[r0:] + _dot(dz, kb))
                if r0:
                    new = tuple(jnp.concatenate([old[:r0], x], axis=0) for old, x in zip(carry[hh], new))
                out.append(new)
            return tuple(out)

        zero = jnp.zeros((bq, 1), F32)
        res = lax.fori_loop(first, nkb - per, lambda jb, c: visit(c, jb, 0),
                            tuple((zero, zero, jnp.zeros((bq, dh), F32)) for _ in range(hg)))
        for dgl in range(per):
            res = visit(res, i * per + dgl, dgl * bk)
        for hh in range(hg):
            dq_ref[:, _lanes(hh)] = res[hh][2]
        if ns:
            pl.when(at_last)(finish)

    return pl.pallas_call(
        body, name=name, grid=grid,
        in_specs=[pl.BlockSpec((bq, 128), lambda h, i: (i, h)),
                  pl.BlockSpec((rows, 128), lambda h, i: (0, ATT_TILES + h)),
                  pl.BlockSpec((rows, 128), lambda h, i: (0, 2 * ATT_TILES + h)),
                  pl.BlockSpec((hg, bq, 1), lambda h, i: (h, i, 0)),
                  pl.BlockSpec((hg, bq, 1), lambda h, i: (h, i, 0)),
                  pl.BlockSpec((bq, 128), lambda h, i: (i, h))] + [HBM] * ns,
        out_specs=[pl.BlockSpec((bq, 128), lambda h, i: (i, h)),
                   pl.BlockSpec((rows, 128), lambda h, i: (0, h)),
                   pl.BlockSpec((rows, 128), lambda h, i: (0, h))] + [HBM] * (2 * ns),
        out_shape=[jax.ShapeDtypeStruct((rows, GROUP), F32)] * 3
        + [jax.ShapeDtypeStruct(s.shape, s.dtype) for s in parts] * 2,
        scratch_shapes=_reduce_sems(ns) if ns else [],
        compiler_params=_params("arbitrary", "arbitrary"),
    )(qkv, qkv, qkv, rt, cnt, do, *parts)


def _head_sum(x, ones_bd):
    return _dot_split(x, ones_bd)


def _rwkv_pre(p, p_prev, mu, w0, a0, k_k, k_a, w_up, a_up, g_up, ones_bd):
    xs = p + (p_prev - p) * mu
    r = xs[:, :GROUP]
    k0 = xs[:, GROUP:2 * GROUP]
    v = xs[:, 2 * GROUP:3 * GROUP]
    lo = xs[:, 3 * GROUP:]
    wa = w0 + _dot(jnp.tanh(lo).astype(BF16), w_up.astype(BF16))
    w = -(jnp.maximum(-wa, 0.0) + jnp.log(1.0 + jnp.exp(-jnp.abs(wa)))) - 0.5
    log_decay = -jnp.exp(w)
    alpha = _sigmoid(a0 + _dot(lo.astype(BF16), a_up.astype(BF16)))
    gate = _dot(_sigmoid(lo).astype(BF16), g_up.astype(BF16))
    kk = k0 * k_k
    kk = kk * lax.rsqrt(jnp.maximum(_head_sum(kk * kk, ones_bd), 1e-24))
    k = k0 * (1.0 + (alpha - 1.0) * k_a)
    return r, log_decay, k, v, -kk, kk * alpha, gate


def _rwkv_post(y, r, k, v, gate, lnx_w, lnx_b, r_k, ones_bd):
    mean = _head_sum(y, ones_bd) * (1.0 / HEAD)
    yc = y - mean
    var = _head_sum(yc * yc, ones_bd) * (1.0 / HEAD)
    yn = yc * lax.rsqrt(var + LNX_EPS) * lnx_w + lnx_b
    bonus = _head_sum(r * k * r_k, ones_bd) * v
    return (yn + bonus) * gate


TOKEN_TILE = 128
_PRE_VEC = 5
_PRE_MAT = 3


def _split_heads(o_ref, val):
    for h in range(N_HEADS):
        o_ref[h] = val[:, _lanes(h)]


def _merge_heads(ref):
    return jnp.concatenate([ref[h] for h in range(N_HEADS)], axis=1)


def _head_spec(tm):
    return pl.BlockSpec((N_HEADS, tm, HEAD), lambda i: (0, i, 0))


def _rwkv_pre_fwd(name, p, p_prev, vecs, mats, ones_bd):
    rows = p.shape[0]
    tm = TOKEN_TILE
    row_spec = lambda w: pl.BlockSpec((tm, w), lambda i: (i, 0))
    full = lambda a: pl.BlockSpec(a.shape, lambda i: (0,) * a.ndim)

    def body(p_ref, pp_ref, *refs):
        ins = [r[...] for r in refs[:_PRE_VEC + _PRE_MAT + 1]]
        outs = refs[_PRE_VEC + _PRE_MAT + 1:]
        vals = _rwkv_pre(p_ref[...], pp_ref[...], *ins)
        for o_ref, val in zip(outs[:6], vals[:6]):
            _split_heads(o_ref, val)
        for o_ref, val in zip(outs[6:], (vals[0], vals[2], vals[3], vals[6])):
            o_ref[...] = val

    return pl.pallas_call(
        body, name=name, grid=(rows // tm,),
        in_specs=[row_spec(RW_COLS), row_spec(RW_COLS)] + [full(a) for a in (*vecs, *mats, ones_bd)],
        out_specs=[_head_spec(tm)] * 6 + [row_spec(GROUP)] * 4,
        out_shape=([jax.ShapeDtypeStruct((N_HEADS, rows, HEAD), F32)] * 6
                   + [jax.ShapeDtypeStruct((rows, GROUP), F32)] * 4),
        compiler_params=_params("arbitrary"),
    )(p, p_prev, *vecs, *mats, ones_bd)


def _rwkv_pre_bwd(name, p, p_prev, vecs, mats, ones_bd, cts_scan, ct_gate, cts_b):
    rows = p.shape[0]
    tm = TOKEN_TILE
    n_par = _PRE_VEC + _PRE_MAT
    row_spec = lambda w: pl.BlockSpec((tm, w), lambda i: (i, 0))
    full = lambda a: pl.BlockSpec(a.shape, lambda i: (0,) * a.ndim)

    def body(*refs):
        p_ref, pp_ref = refs[0], refs[1]
        par = [r[...] for r in refs[2:2 + n_par]]
        ones = refs[2 + n_par][...]
        cta = [_merge_heads(r) for r in refs[3 + n_par:9 + n_par]] + [refs[9 + n_par][...]]
        ctb = [r[...] for r in refs[10 + n_par:13 + n_par]]
        outs = refs[13 + n_par:]
        ct = (cta[0] + ctb[0], cta[1], cta[2] + ctb[1], cta[3] + ctb[2], cta[4], cta[5], cta[6])
        _, vjp = jax.vjp(lambda pv, ppv, *pr: _rwkv_pre(pv, ppv, *pr, ones), p_ref[...], pp_ref[...], *par)
        grads = vjp(ct)
        outs[0][...] = grads[0]
        outs[1][...] = grads[1]

        @pl.when(pl.program_id(0) == 0)
        def _():
            for o_ref in outs[2:]:
                o_ref[...] = jnp.zeros_like(o_ref)

        for o_ref, gval in zip(outs[2:], grads[2:]):
            o_ref[...] += gval

    par_arrays = (*vecs, *mats)
    return pl.pallas_call(
        body, name=name, grid=(rows // tm,),
        in_specs=([row_spec(RW_COLS)] * 2 + [full(a) for a in (*par_arrays, ones_bd)]
                  + [_head_spec(tm)] * 6 + [row_spec(GROUP)] * 4),
        out_specs=[row_spec(RW_COLS)] * 2 + [full(a) for a in par_arrays],
        out_shape=([jax.ShapeDtypeStruct((rows, RW_COLS), F32)] * 2
                   + [jax.ShapeDtypeStruct(a.shape, F32) for a in par_arrays]),
        compiler_params=_params("arbitrary"),
    )(p, p_prev, *par_arrays, ones_bd, *cts_scan, ct_gate, *cts_b)


def _rwkv_post_fwd(name, y, r, k, v, gate, vecs, ones_bd):
    rows = r.shape[0]
    tm = TOKEN_TILE
    row_spec = pl.BlockSpec((tm, GROUP), lambda i: (i, 0))
    full = lambda a: pl.BlockSpec(a.shape, lambda i: (0,) * a.ndim)

    def body(y_ref, *refs):
        vals = [r_[...] for r_ in refs[:-1]]
        refs[-1][...] = _rwkv_post(_merge_heads(y_ref), *vals)

    return pl.pallas_call(
        body, name=name, grid=(rows // tm,),
        in_specs=[_head_spec(tm)] + [row_spec] * 4 + [full(a) for a in (*vecs, ones_bd)],
        out_specs=row_spec,
        out_shape=jax.ShapeDtypeStruct((rows, GROUP), F32),
        compiler_params=_params("arbitrary"),
    )(y, r, k, v, gate, *vecs, ones_bd)


def _rwkv_post_bwd(name, y, r, k, v, gate, vecs, ones_bd, dout):
    rows = r.shape[0]
    tm = TOKEN_TILE
    row_spec = pl.BlockSpec((tm, GROUP), lambda i: (i, 0))
    full = lambda a: pl.BlockSpec(a.shape, lambda i: (0,) * a.ndim)

    def body(y_ref, *refs):
        vals = [_merge_heads(y_ref)] + [r_[...] for r_ in refs[:7]]
        ones = refs[7][...]
        dout_v = refs[8][...]
        outs = refs[9:]
        _, vjp = jax.vjp(lambda *a: _rwkv_post(*a, ones), *vals)
        grads = vjp(dout_v)
        _split_heads(outs[0], grads[0])
        for o_ref, gval in zip(outs[1:5], grads[1:5]):
            o_ref[...] = gval

        @pl.when(pl.program_id(0) == 0)
        def _():
            for o_ref in outs[5:]:
                o_ref[...] = jnp.zeros_like(o_ref)

        for o_ref, gval in zip(outs[5:], grads[5:]):
            o_ref[...] += gval

    return pl.pallas_call(
        body, name=name, grid=(rows // tm,),
        in_specs=[_head_spec(tm)] + [row_spec] * 4 + [full(a) for a in (*vecs, ones_bd)] + [row_spec],
        out_specs=[_head_spec(tm)] + [row_spec] * 4 + [full(a) for a in vecs],
        out_shape=([jax.ShapeDtypeStruct((N_HEADS, rows, HEAD), F32)] + [jax.ShapeDtypeStruct((rows, GROUP), F32)] * 4
                   + [jax.ShapeDtypeStruct(a.shape, F32) for a in vecs]),
        compiler_params=_params("arbitrary"),
    )(y, r, k, v, gate, *vecs, ones_bd, dout)


_NN = (((2,), (1,)), ((0,), (0,)))
_NT = (((2,), (2,)), ((0,), (0,)))
_TN = (((1,), (1,)), ((0,), (0,)))


_BWD_FORMS = {"nn": (("nt", False), ("tn", False)),
              "nt": (("nn", False), ("tn", True)),
              "tn": (("nt", True), ("nn", False))}
_DIMS = {"nn": _NN, "nt": _NT, "tn": _TN}


def _bdot(a, b, form):
    return lax.dot_general(a.astype(BF16), b.astype(BF16), _DIMS[form], preferred_element_type=F32)


@functools.partial(jax.custom_vjp, nondiff_argnums=(2,))
def _bmm(a, b, form):
    return _bdot(a, b, form)


def _bmm_fwd(a, b, form):
    return _bdot(a, b, form), (a.astype(BF16), b.astype(BF16))


def _bmm_bwd(form, res, dc):
    a, b = res
    (fa, swap_a), (fb, swap_b) = _BWD_FORMS[form]
    da = _bdot(b, dc, fa) if swap_a else _bdot(dc, b, fa)
    db = _bdot(dc, a, fb) if swap_b else _bdot(a, dc, fb)
    return da, db


_bmm.defvjp(_bmm_fwd, _bmm_bwd)


@jax.custom_vjp
def _cumsum_steps(x):
    return _tri_apply(x, lambda r, c: r >= c)


def _tri_apply(x, cmp):
    nh, c, _ = x.shape
    tri = cmp(lax.broadcasted_iota(jnp.int32, (c, c), 0), lax.broadcasted_iota(jnp.int32, (c, c), 1))
    tri = jnp.broadcast_to(tri.astype(BF16)[None], (nh, c, c))
    hi, lo = _split2(x)
    return (lax.dot_general(tri, hi, _NN, preferred_element_type=F32)
            + lax.dot_general(tri, lo, _NN, preferred_element_type=F32))


_cumsum_steps.defvjp(lambda x: (_cumsum_steps(x), None), lambda _, d: (_tri_apply(d, lambda r, c: r <= c),))


def _chunk(state, r, log_w, k, v, a, b):
    nh, c, _ = r.shape
    row = lax.broadcasted_iota(jnp.int32, (c, c), 0)
    col = lax.broadcasted_iota(jnp.int32, (c, c), 1)
    cum = _cumsum_steps(log_w)
    mid = cum[:, c // 2 - 1:c // 2, :]
    a_t = a * jnp.exp(cum - log_w - mid)
    r_t = r * jnp.exp(cum - mid)
    back = jnp.exp(mid - cum)
    b_t = b * back
    k_t = k * back
    strict, incl = (row > col)[None], (row >= col)[None]
    ar = jnp.concatenate([a_t, r_t], axis=1)
    on_b = _bmm(ar, b_t, "nt")
    on_k = _bmm(ar, k_t, "nt")
    n_mat = jnp.where(strict, on_b[:, :c], 0.0)
    p_mat = jnp.where(incl, on_b[:, c:], 0.0)
    m_mat = jnp.where(strict, on_k[:, :c], 0.0)
    q_mat = jnp.where(incl, on_k[:, c:], 0.0)
    inv, power, span = n_mat, _bmm(n_mat, n_mat, "nn"), 2
    while span < c:
        both = _bmm(jnp.concatenate([power, inv], axis=1), power, "nn")
        inv = inv + power + both[:, c:]
        power = both[:, :c]
        span *= 2
    s_mid = state * jnp.swapaxes(jnp.exp(mid), 1, 2)
    x = _bmm(jnp.concatenate([a_t, m_mat], axis=2), jnp.concatenate([s_mid, v], axis=1), "nn")
    u = x + _bmm(inv, x, "nn")
    y = _bmm(jnp.concatenate([r_t, p_mat, q_mat], axis=2), jnp.concatenate([s_mid, u, v], axis=1), "nn")
    grown = _bmm(jnp.concatenate([b_t, k_t], axis=1), jnp.concatenate([u, v], axis=1), "tn")
    s_new = (s_mid + grown) * jnp.swapaxes(jnp.exp(cum[:, c - 1:c, :] - mid), 1, 2)
    return y, s_new


def _scan_fwd(name, ops):
    nh, rows, dh = ops[0].shape
    nc = rows // CHUNK
    spec = pl.BlockSpec((nh, CHUNK, dh), lambda c: (0, c, 0))

    def body(r_ref, w_ref, k_ref, v_ref, a_ref, b_ref, y_ref, st_ref, state):
        @pl.when(pl.program_id(0) == 0)
        def _():
            state[...] = jnp.zeros_like(state)

        st_ref[0] = state[...]
        y, s_new = _chunk(state[...], r_ref[...], w_ref[...], k_ref[...], v_ref[...], a_ref[...], b_ref[...])
        y_ref[...] = y
        state[...] = s_new

    return pl.pallas_call(
        body, name=name, grid=(nc,),
        in_specs=[spec] * 6,
        out_specs=[spec, pl.BlockSpec((1, nh, dh, dh), lambda c: (c, 0, 0, 0))],
        out_shape=[jax.ShapeDtypeStruct((nh, rows, dh), F32), jax.ShapeDtypeStruct((nc, nh, dh, dh), F32)],
        scratch_shapes=[pltpu.VMEM((nh, dh, dh), F32)],
        compiler_params=_params("arbitrary"),
    )(*ops)


def _scan_bwd(name, ops, states, dy):
    nh, rows, dh = ops[0].shape
    nc = rows // CHUNK
    spec = pl.BlockSpec((nh, CHUNK, dh), lambda c: (0, nc - 1 - c, 0))

    def body(r_ref, w_ref, k_ref, v_ref, a_ref, b_ref, st_ref, dy_ref, *rest):
        outs, dstate = rest[:6], rest[6]

        @pl.when(pl.program_id(0) == 0)
        def _():
            dstate[...] = jnp.zeros_like(dstate)

        _, vjp = jax.vjp(_chunk, st_ref[0], r_ref[...], w_ref[...], k_ref[...], v_ref[...], a_ref[...], b_ref[...])
        grads = vjp((dy_ref[...], dstate[...]))
        dstate[...] = grads[0]
        for o_ref, gval in zip(outs, grads[1:]):
            o_ref[...] = gval

    return pl.pallas_call(
        body, name=name, grid=(nc,),
        in_specs=[spec] * 6 + [pl.BlockSpec((1, nh, dh, dh), lambda c: (nc - 1 - c, 0, 0, 0)), spec],
        out_specs=[spec] * 6,
        out_shape=[jax.ShapeDtypeStruct((nh, rows, dh), F32)] * 6,
        scratch_shapes=[pltpu.VMEM((nh, dh, dh), F32)],
        compiler_params=_params("arbitrary"),
    )(*ops, states, dy)


def _shift_down(x):
    return jnp.concatenate([jnp.zeros((1, x.shape[1]), x.dtype), x[:-1]], axis=0)


def _shift_up(x):
    return jnp.concatenate([x[1:], jnp.zeros((1, x.shape[1]), x.dtype)], axis=0)


def _pad_rows(x, rows):
    return jnp.concatenate([x, jnp.zeros((rows - x.shape[0],) + x.shape[1:], x.dtype)], axis=0)


def _pad_cols(x, cols):
    return jnp.concatenate([x, jnp.zeros(x.shape[:-1] + (cols - x.shape[-1],), x.dtype)], axis=-1)


def _lora_pad(w_up, a_up, g_up):
    z = lambda n: jnp.zeros((n, GROUP), F32)
    return (jnp.concatenate([w_up, z(LORA_PAD - LORA_W)], 0),
            jnp.concatenate([z(LORA_W), a_up, z(LORA_PAD - LORA_W - LORA_A)], 0),
            jnp.concatenate([z(LORA_W + LORA_A), g_up, z(LORA_PAD - LORA_W - LORA_A - LORA_G)], 0))


MID = ['w_in']
LATE = ['ffn2_w_gate', 'ffn2_w_up', 'ffn2_w_down', 'w_out']


def _local_step(x, tgt, w, late=None):
    d = x.shape[1]
    zeros = jnp.zeros((META_PAD, d), F32)
    h0 = jnp.concatenate([zeros, w["meta_tokens"], x], axis=0)
    tgt_p = jnp.concatenate([jnp.zeros((ROW0, d), F32), tgt], axis=0)
    ones_bd = ((lax.broadcasted_iota(jnp.int32, (2 * GROUP, GROUP), 0) % GROUP) // HEAD
               == lax.broadcasted_iota(jnp.int32, (2 * GROUP, GROUP), 1) // HEAD).astype(BF16)
    pre_vecs = (_pad_cols(w["rwkv_mu"], RW_COLS), w["rwkv_w0"], w["rwkv_a0"], w["rwkv_k_k"], w["rwkv_k_a"])
    pre_mats = _lora_pad(w["rwkv_w_up"], w["rwkv_a_up"], w["rwkv_g_up"])
    post_vecs = (w["rwkv_lnx_w"], w["rwkv_lnx_b"], w["rwkv_r_k"].reshape(1, GROUP))

    h1, a1, b1, *gathered = _ffn_fwd("ffn1_fwd", h0, w["ffn1_norm"], w["ffn1_w_gate"], w["ffn1_w_up"],
                                     w["ffn1_w_down"], late and ("gather", late.shards["mid"]))
    if late is not None:
        w = {**w, **late.join("mid", gathered)}
    w_in = _pad_cols(w["w_in"], IN_COLS_PAD)
    qkv, p, n2 = _norm_proj("in_proj", h1, w["mix_norm"], w_in)
    sb, rest_total, visited, *gathered = _sb_fwd("sb_fwd", qkv, late.shards["late"] if late else ())
    if late is not None:
        w = {**w, **late.join("late", gathered)}
    p_prev = _shift_down(p)
    pre = _rwkv_pre_fwd("rwkv_pre_fwd", p, p_prev, pre_vecs, pre_mats, ones_bd)
    scan_ops, token_ops = pre[:6], pre[6:]
    y, states = _scan_fwd("rwkv_scan_fwd", scan_ops)
    rw = _rwkv_post_fwd("rwkv_post_fwd", y, *token_ops, post_vecs, ones_bd)
    h2, mix = _out_proj("out_proj", h1, sb, rw, w["w_out"])
    h3, a2, b2 = _ffn_fwd("ffn2_fwd", h2, w["ffn2_norm"], w["ffn2_w_gate"], w["ffn2_w_up"], w["ffn2_w_down"])
    loss8, dh3, g_final = _loss_head("loss_head", h3, w["final_norm"].reshape(1, d), tgt_p)

    g = {"final_norm": g_final.reshape(d)}
    dh2, da2, db2, s2, n3, dhh3, g["ffn2_norm"] = _ffn_bwd(
        "ffn2_bwd", dh3, h2, w["ffn2_norm"], a2, b2, w["ffn2_w_gate"], w["ffn2_w_up"], w["ffn2_w_down"])
    g["ffn2_w_gate"] = _mm_tn("ffn2_dgate", da2, n3)
    g["ffn2_w_up"] = _mm_tn("ffn2_dup", db2, n3)
    g["ffn2_w_down"] = _mm_tn("ffn2_ddown", s2, dhh3)
    dsb, drw, dh2b = _out_proj_bwd("out_proj_bwd", dh2, w["w_out"])
    g["w_out"] = _mm_tn("out_proj_dw", mix, dh2b)
    dq, dk, dv, *reduced_late = _sb_bwd("sb_bwd", qkv, rest_total, visited, dsb, late.parts("late", g) if late else ())
    post_g = _rwkv_post_bwd("rwkv_post_bwd", y, *token_ops, post_vecs, ones_bd, drw)
    g["rwkv_lnx_w"], g["rwkv_lnx_b"] = post_g[5], post_g[6]
    g["rwkv_r_k"] = post_g[7].reshape(1, N_HEADS, HEAD)
    scan_g = _scan_bwd("rwkv_scan_bwd", scan_ops, states, post_g[0])
    pre_g = _rwkv_pre_bwd("rwkv_pre_bwd", p, p_prev, pre_vecs, pre_mats, ones_bd, scan_g, post_g[4], post_g[1:4])
    g["rwkv_mu"] = pre_g[2][:, :w["rwkv_mu"].shape[1]]
    g["rwkv_w0"], g["rwkv_a0"], g["rwkv_k_k"], g["rwkv_k_a"] = pre_g[3:7]
    g["rwkv_w_up"] = pre_g[7][:LORA_W]
    g["rwkv_a_up"] = pre_g[8][LORA_W:LORA_W + LORA_A]
    g["rwkv_g_up"] = pre_g[9][LORA_W + LORA_A:LORA_W + LORA_A + LORA_G]
    dp = pre_g[0] + _shift_up(pre_g[1])
    live = (jnp.arange(h0.shape[0]) >= META_PAD)[:, None]
    dproj = jnp.where(live, jnp.concatenate([dq, dk, dv, dp], axis=1), 0.0).astype(BF16)
    g["w_in"] = _mm_tn("in_proj_dw", n2, dproj)[:, :w["w_in"].shape[1]]
    dh1, g["mix_norm"] = _norm_proj_bwd("in_proj_bwd", dproj, w_in, h1, w["mix_norm"], dh2)
    dh0, da1, db1, s1, n1, dhh1, g["ffn1_norm"], *reduced_mid = _ffn_bwd(
        "ffn1_bwd", dh1, h0, w["ffn1_norm"], a1, b1, w["ffn1_w_gate"], w["ffn1_w_up"], w["ffn1_w_down"],
        late and ("reduce", late.parts("mid", g)))
    g["ffn1_w_gate"] = _mm_tn("ffn1_dgate", da1, n1)
    g["ffn1_w_up"] = _mm_tn("ffn1_dup", db1, n1)
    g["ffn1_w_down"] = _mm_tn("ffn1_ddown", s1, dhh1)
    g["meta_tokens"] = dh0[META_PAD:ROW0]
    return loss8[0, 0], dh0[ROW0:], g, {"mid": reduced_mid, "late": reduced_late}


N_CHIPS = 4
N_DEV = 8
HBM = pl.BlockSpec(memory_space=pltpu.HBM)


def _place():
    return lax.axis_index("x"), lax.axis_index("y"), lax.axis_index("c")


def _other_chips(x, y):
    return [(1 - x, y), (x, 1 - y), (1 - x, 1 - y)]


def _gather_sems(n):
    return [pltpu.SemaphoreType.DMA((3 * n,)), pltpu.SemaphoreType.DMA((3 * n,)), pltpu.SemaphoreType.DMA((n,)),
            pltpu.SemaphoreType.DMA((3 * n,)), pltpu.SemaphoreType.DMA((3 * n,))]


def _gather_exchange(ins, outs, sems):
    n = len(ins)
    half = [r.shape[0] // 2 for r in ins]
    send, recv, local, d2d_send, d2d_recv = sems
    x, y, c = _place()
    me = 2 * x + y
    chips = _other_chips(x, y)

    def rows_of(k, h):
        return pl.ds(pl.multiple_of(h * half[k], 8), half[k])

    def own(k):
        return pltpu.make_async_copy(ins[k], outs[k].at[me], local.at[k])

    def copy(j, k, slot):
        return pltpu.make_async_remote_copy(
            src_ref=ins[k].at[rows_of(k, c)], dst_ref=outs[k].at[slot, rows_of(k, c)],
            send_sem=send.at[j * n + k], recv_sem=recv.at[j * n + k],
            device_id=(chips[j][0], chips[j][1], c), device_id_type=MESH)

    def passed(j, k, h):
        slot = 2 * chips[j][0] + chips[j][1]
        return pltpu.make_async_remote_copy(
            src_ref=outs[k].at[slot, rows_of(k, h)], dst_ref=outs[k].at[slot, rows_of(k, h)],
            send_sem=d2d_send.at[j * n + k], recv_sem=d2d_recv.at[j * n + k],
            device_id=(x, y, 1 - c), device_id_type=MESH)

    def start():
        for k in range(n):
            own(k).start()
        for j in range(3):
            for k in range(n):
                copy(j, k, me).start()

    def finish():
        for j in range(3):
            for k in range(n):
                copy(j, k, 2 * chips[j][0] + chips[j][1]).wait_recv()
                passed(j, k, c).start()
        for j in range(3):
            for k in range(n):
                passed(j, k, 1 - c).wait_recv()
        for j in range(3):
            for k in range(n):
                copy(j, k, me).wait_send()
                passed(j, k, c).wait_send()
        for k in range(n):
            own(k).wait()

    return start, finish


def _gather_shards(name, shards):
    n = len(shards)

    def body(*refs):
        start, finish = _gather_exchange(refs[:n], refs[n:2 * n], refs[2 * n:])
        start()
        finish()

    return pl.pallas_call(
        body, name=name,
        in_specs=[HBM] * n, out_specs=[HBM] * n,
        out_shape=[jax.ShapeDtypeStruct((N_CHIPS,) + s.shape, s.dtype) for s in shards],
        scratch_shapes=_gather_sems(n),
    )(*shards)


def _pair_exchange(name, parts):
    n = len(parts)
    half = [s.shape[1] // 2 for s in parts]

    def body(*refs):
        ins, outs = refs[:n], refs[n:2 * n]
        send, recv = refs[2 * n:]
        x, y, c = _place()

        def copy(k):
            rows = pl.ds(pl.multiple_of((1 - c) * half[k], 8), half[k])
            return pltpu.make_async_remote_copy(
                src_ref=ins[k].at[:, rows], dst_ref=outs[k], send_sem=send.at[k], recv_sem=recv.at[k],
                device_id=(x, y, 1 - c), device_id_type=MESH)

        for k in range(n):
            copy(k).start()
        for k in range(n):
            copy(k).wait_recv()
        for k in range(n):
            copy(k).wait_send()

    return pl.pallas_call(
        body, name=name,
        in_specs=[HBM] * n, out_specs=[HBM] * n,
        out_shape=[jax.ShapeDtypeStruct((s.shape[0], s.shape[1] // 2, s.shape[2]), s.dtype) for s in parts],
        scratch_shapes=[pltpu.SemaphoreType.DMA((n,)), pltpu.SemaphoreType.DMA((n,))],
    )(*parts)


def _pair_add(name, part, other):
    nch, rows, cols = part.shape
    half = rows // 2

    def body(p_ref, o_ref, out_ref):
        c = lax.axis_index("c")
        mine = p_ref[0, pl.ds(pl.multiple_of(c * half, 16), half), :]
        out_ref[0] = (mine.astype(F32) + o_ref[0].astype(F32)).astype(out_ref.dtype)

    return pl.pallas_call(
        body, name=name, grid=(nch,),
        in_specs=[pl.BlockSpec((1, rows, cols), lambda j: (j, 0, 0)),
                  pl.BlockSpec((1, half, cols), lambda j: (j, 0, 0))],
        out_specs=pl.BlockSpec((1, half, cols), lambda j: (j, 0, 0)),
        out_shape=jax.ShapeDtypeStruct((nch, half, cols), part.dtype),
        compiler_params=_params("arbitrary"),
    )(part, other)


def _reduce_sems(n):
    return [pltpu.SemaphoreType.DMA((3 * n,)), pltpu.SemaphoreType.DMA((3 * n,)), pltpu.SemaphoreType.DMA((n,)),
            pltpu.SemaphoreType.DMA((n,)), pltpu.SemaphoreType.DMA((n,))]


def _reduce_exchange(ins, got, sib, sems):
    n = len(ins)
    send, recv, local, d2d_send, d2d_recv = sems
    x, y, c = _place()
    me = 2 * x + y
    chips = _other_chips(x, y)

    def own(k):
        return pltpu.make_async_copy(ins[k].at[me], got[k].at[me], local.at[k])

    def copy(j, k, shard, slot):
        return pltpu.make_async_remote_copy(
            src_ref=ins[k].at[shard], dst_ref=got[k].at[slot], send_sem=send.at[j * n + k],
            recv_sem=recv.at[j * n + k], device_id=(chips[j][0], chips[j][1], c), device_id_type=MESH)

    def swap(k):
        return pltpu.make_async_remote_copy(
            src_ref=got[k], dst_ref=sib[k], send_sem=d2d_send.at[k], recv_sem=d2d_recv.at[k],
            device_id=(x, y, 1 - c), device_id_type=MESH)

    def start():
        for k in range(n):
            own(k).start()
        for j in range(3):
            for k in range(n):
                copy(j, k, 2 * chips[j][0] + chips[j][1], me).start()

    def finish():
        for k in range(n):
            own(k).wait()
            for j in range(3):
                copy(j, k, me, 2 * chips[j][0] + chips[j][1]).wait_recv()
            swap(k).start()
        for k in range(n):
            swap(k).wait_recv()
        for j in range(3):
            for k in range(n):
                copy(j, k, me, me).wait_send()
        for k in range(n):
            swap(k).wait_send()

    return start, finish


def _reduce_shards(name, parts):
    n = len(parts)

    def body(*refs):
        start, finish = _reduce_exchange(refs[:n], refs[n:2 * n], refs[2 * n:3 * n], refs[3 * n:])
        start()
        finish()

    return pl.pallas_call(
        body, name=name,
        in_specs=[HBM] * n, out_specs=[HBM] * (2 * n),
        out_shape=[jax.ShapeDtypeStruct(s.shape, s.dtype) for s in parts] * 2,
        scratch_shapes=_reduce_sems(n),
    )(*parts)


def _all_reduce_small(name, vec):
    rows = vec.shape[0]

    def body(v_ref, o_ref, buf, send, recv):
        x, y, c = _place()
        me = 4 * x + 2 * y + c
        peers = [(x ^ (r >> 2), y ^ ((r >> 1) & 1), c ^ (r & 1)) for r in range(1, N_DEV)]

        def copy(r, slot):
            px, py, pc = peers[r]
            return pltpu.make_async_remote_copy(
                src_ref=v_ref, dst_ref=buf.at[slot], send_sem=send.at[r], recv_sem=recv.at[r],
                device_id=(px, py, pc), device_id_type=MESH)

        sent = [copy(r, me) for r in range(N_DEV - 1)]
        for cp in sent:
            cp.start()
        buf[me] = v_ref[...]
        for r in range(N_DEV - 1):
            px, py, pc = peers[r]
            copy(r, 4 * px + 2 * py + pc).wait_recv()
        total = buf[0]
        for dev in range(1, N_DEV):
            total = total + buf[dev]
        o_ref[...] = total
        for cp in sent:
            cp.wait_send()

    return pl.pallas_call(
        body, name=name,
        in_specs=[pl.BlockSpec(memory_space=pltpu.VMEM)], out_specs=pl.BlockSpec(memory_space=pltpu.VMEM),
        out_shape=jax.ShapeDtypeStruct(vec.shape, F32),
        scratch_shapes=[pltpu.VMEM((N_DEV, rows, 128), F32),
                        pltpu.SemaphoreType.DMA((N_DEV - 1,)), pltpu.SemaphoreType.DMA((N_DEV - 1,))],
        compiler_params=pltpu.CompilerParams(vmem_limit_bytes=VMEM_LIMIT),
    )(vec)


def _adamw(w, g, m, v):
    m = ADAM_B1 * m + (1.0 - ADAM_B1) * g
    v = ADAM_B2 * v + (1.0 - ADAM_B2) * (g * g)
    m_hat = m / (1.0 - ADAM_B1 ** ADAM_STEP)
    v_hat = v / (1.0 - ADAM_B2 ** ADAM_STEP)
    return -ADAM_LR * (m_hat / (jnp.sqrt(v_hat) + ADAM_EPS) + ADAM_WD * w), m, v


def _adamw_shard(name, core, w, m, v, got, sib):
    rows, cols = w.shape
    tr = rows // 4
    spec = pl.BlockSpec((tr, cols), lambda i, c_ref: (i, 0))
    spec4 = pl.BlockSpec((N_CHIPS, tr, cols), lambda i, c_ref: (0, i % 2, 0))

    def body(c_ref, w_ref, m_ref, v_ref, got_ref, sib_ref, g_ref, d_ref, mo_ref, vo_ref):
        def four(ref):
            return ((ref[0].astype(F32) + ref[1].astype(F32)) + ref[2].astype(F32)) + ref[3].astype(F32)

        g = jnp.where(pl.program_id(0) // 2 == c_ref[0], four(got_ref), four(sib_ref))
        g_ref[...] = g
        d_ref[...], mo_ref[...], vo_ref[...] = _adamw(w_ref[...], g, m_ref[...], v_ref[...])

    return pl.pallas_call(
        body, name=name,
        grid_spec=pltpu.PrefetchScalarGridSpec(
            num_scalar_prefetch=1, grid=(4,),
            in_specs=[spec, spec, spec, spec4, spec4], out_specs=[spec] * 4),
        out_shape=[jax.ShapeDtypeStruct((rows, cols), F32)] * 4,
        compiler_params=_params("arbitrary"),
    )(core, w, m, v, got, sib)


def _adamw_small(name, w, m, v, g):
    def body(w_ref, m_ref, v_ref, g_ref, d_ref, mo_ref, vo_ref):
        d_ref[...], mo_ref[...], vo_ref[...] = _adamw(w_ref[...], g_ref[...], m_ref[...], v_ref[...])

    return pl.pallas_call(body, name=name, out_shape=[jax.ShapeDtypeStruct(w.shape, F32)] * 3)(w, m, v, g)


def _cast_bf16(name, arrays):
    n = len(arrays)

    def body(*refs):
        for i_ref, o_ref in zip(refs[:n], refs[n:]):
            o_ref[...] = i_ref[...].astype(BF16)

    return pl.pallas_call(
        body, name=name, out_shape=[jax.ShapeDtypeStruct(a.shape, BF16) for a in arrays],
        compiler_params=pltpu.CompilerParams(vmem_limit_bytes=VMEM_LIMIT),
    )(*arrays)


def _pack(arrays, rows):
    flat = jnp.concatenate([a.reshape(-1) for a in arrays])
    return jnp.concatenate([flat, jnp.zeros((rows * 128 - flat.shape[0],), F32)]).reshape(rows, 128)


def _unpack(packed, shapes):
    flat, out, at = packed.reshape(-1), [], 0
    for s in shapes:
        size = 1
        for dim in s:
            size *= dim
        out.append(flat[at:at + size].reshape(s))
        at += size
    return out


def _rows_for(shapes):
    total = 0
    for s in shapes:
        size = 1
        for dim in s:
            size *= dim
        total += size
    return -(-total // 1024) * 8


WEIGHTS = ['meta_tokens', 'ffn1_norm', 'ffn1_w_gate', 'ffn1_w_up', 'ffn1_w_down', 'mix_norm', 'w_in', 'rwkv_mu',
           'rwkv_w0', 'rwkv_w_up', 'rwkv_a0', 'rwkv_a_up', 'rwkv_g_up', 'rwkv_k_k', 'rwkv_k_a', 'rwkv_r_k',
           'rwkv_lnx_w', 'rwkv_lnx_b', 'w_out', 'ffn2_norm', 'ffn2_w_gate', 'ffn2_w_up', 'ffn2_w_down', 'final_norm']
COL_CUT = ['ffn1_w_gate', 'ffn1_w_up', 'w_in', 'ffn2_w_gate', 'ffn2_w_up']
ROW_CUT = ['ffn1_w_down', 'w_out', 'ffn2_w_down']
SMALL_CUT = ['meta_tokens', 'rwkv_w_up', 'rwkv_a_up', 'rwkv_g_up']
TRANSPOSED = ['ffn1_w_gate', 'ffn1_w_up', 'ffn2_w_gate', 'ffn2_w_up']
BIG = COL_CUT + ROW_CUT
REPLICATED = [n for n in WEIGHTS if n not in BIG + SMALL_CUT]


def _join_cols(a):
    return a.transpose(1, 0, 2).reshape(a.shape[1], N_CHIPS * a.shape[2])


def _cut_cols(a):
    return a.reshape(a.shape[0], N_CHIPS, a.shape[1] // N_CHIPS).transpose(1, 0, 2)


def _step(x, loss_target, w, m, v):
    two = lambda a: a.reshape(a.shape[-2], a.shape[-1])

    def rows_cut(n, a):
        return jnp.swapaxes(two(a), 0, 1) if n in TRANSPOSED else two(a)

    def as_given(n, a, like):
        return (jnp.swapaxes(a, 0, 1) if n in TRANSPOSED else a).reshape(like.shape)

    col_cut = [n for n in COL_CUT + SMALL_CUT if n not in TRANSPOSED]

    def join(names, gathered):
        return {n: (_join_cols(a) if n in col_cut else a.reshape(-1, a.shape[-1])) for n, a in zip(names, gathered)}

    def pair_sums(tag, names, g):
        parts = [_cut_cols(g[n]) if n in col_cut else g[n].reshape(N_CHIPS, -1, g[n].shape[-1]) for n in names]
        arrived = _pair_exchange("pair_exchange_" + tag, parts)
        return [_pair_add("pair_add_" + n, p, o) for n, p, o in zip(names, parts, arrived)]

    first = [n for n in BIG if n not in MID + LATE]
    groups = {"mid": MID, "late": LATE}
    cast = dict(zip(BIG, _cast_bf16("cast_weights", [rows_cut(n, w[n]) for n in BIG])))
    names = first + SMALL_CUT
    shards = [cast[n] for n in first] + [two(w[n]) for n in SMALL_CUT]
    full = {n: (two(w[n]) if w[n].ndim == 3 else w[n]) for n in REPLICATED}
    full.update(join(names, _gather_shards("gather_weights", shards)))
    full["rwkv_r_k"] = w["rwkv_r_k"]
    full["final_norm"] = w["final_norm"]

    late = types.SimpleNamespace(shards={k: [cast[n] for n in names] for k, names in groups.items()},
                                 join=lambda k, gathered: join(groups[k], gathered),
                                 parts=lambda k, g: pair_sums(k, groups[k], g))

    loss, dx, g, reduced = _local_step(x[0], loss_target[0], full, late)
    loss = lax.psum(loss, ("x", "y", "c"))

    groups["first"] = first
    reduced["first"] = list(_reduce_shards("reduce_gradients", pair_sums("first", first, g)))
    got, sib = {}, {}
    for k, names in groups.items():
        got.update(zip(names, reduced[k][:len(names)]))
        sib.update(zip(names, reduced[k][len(names):]))

    small_names = REPLICATED + SMALL_CUT
    small_shapes = [g[n].shape for n in small_names]
    small = _all_reduce_small("reduce_small", _pack([g[n] for n in small_names], _rows_for(small_shapes)))
    g_small = dict(zip(small_names, _unpack(small, small_shapes)))
    chip = 2 * lax.axis_index("x") + lax.axis_index("y")
    for n in SMALL_CUT:
        width = g_small[n].shape[1] // N_CHIPS
        g_small[n] = lax.dynamic_slice_in_dim(g_small[n], chip * width, width, axis=1)

    grad, delta, new_m, new_v = {}, {}, {}, {}
    core = lax.axis_index("c").astype(jnp.int32).reshape(1)
    for n in BIG:
        outs = _adamw_shard("adamw_" + n, core, rows_cut(n, w[n]), rows_cut(n, m[n]), rows_cut(n, v[n]), got[n], sib[n])
        grad[n], delta[n], new_m[n], new_v[n] = (as_given(n, o, w[n]) for o in outs)
    shapes = [w[n].shape for n in small_names]
    rows = _rows_for(shapes)
    packed = [_pack([t[n] for n in small_names], rows) for t in (w, m, v)]
    g_packed = _pack([g_small[n] for n in small_names], rows)
    outs = [_unpack(o, shapes) for o in _adamw_small("adamw_small", *packed, g_packed)]
    for i, n in enumerate(small_names):
        grad[n] = g_small[n].reshape(w[n].shape)
        delta[n], new_m[n], new_v[n] = outs[0][i], outs[1][i], outs[2][i]
    return loss, dx[None], grad, delta, new_m, new_v


def kernel(x, meta_tokens, ffn1_norm, ffn1_w_gate, ffn1_w_up, ffn1_w_down, mix_norm, w_in, rwkv_mu, rwkv_w0, rwkv_w_up, rwkv_a0, rwkv_a_up, rwkv_g_up, rwkv_k_k, rwkv_k_a, rwkv_r_k, rwkv_lnx_w, rwkv_lnx_b, w_out, ffn2_norm, ffn2_w_gate, ffn2_w_up, ffn2_w_down, final_norm, loss_target, m_meta_tokens, m_ffn1_norm, m_ffn1_w_gate, m_ffn1_w_up, m_ffn1_w_down, m_mix_norm, m_w_in, m_rwkv_mu, m_rwkv_w0, m_rwkv_w_up, m_rwkv_a0, m_rwkv_a_up, m_rwkv_g_up, m_rwkv_k_k, m_rwkv_k_a, m_rwkv_r_k, m_rwkv_lnx_w, m_rwkv_lnx_b, m_w_out, m_ffn2_norm, m_ffn2_w_gate, m_ffn2_w_up, m_ffn2_w_down, m_final_norm, v_meta_tokens, v_ffn1_norm, v_ffn1_w_gate, v_ffn1_w_up, v_ffn1_w_down, v_mix_norm, v_w_in, v_rwkv_mu, v_rwkv_w0, v_rwkv_w_up, v_rwkv_a0, v_rwkv_a_up, v_rwkv_g_up, v_rwkv_k_k, v_rwkv_k_a, v_rwkv_r_k, v_rwkv_lnx_w, v_rwkv_lnx_b, v_w_out, v_ffn2_norm, v_ffn2_w_gate, v_ffn2_w_up, v_ffn2_w_down, v_final_norm):
    w = dict(zip(WEIGHTS, (meta_tokens, ffn1_norm, ffn1_w_gate, ffn1_w_up, ffn1_w_down, mix_norm, w_in, rwkv_mu, rwkv_w0, rwkv_w_up, rwkv_a0, rwkv_a_up, rwkv_g_up, rwkv_k_k, rwkv_k_a, rwkv_r_k, rwkv_lnx_w, rwkv_lnx_b, w_out, ffn2_norm, ffn2_w_gate, ffn2_w_up, ffn2_w_down, final_norm)))
    m = dict(zip(WEIGHTS, (m_meta_tokens, m_ffn1_norm, m_ffn1_w_gate, m_ffn1_w_up, m_ffn1_w_down, m_mix_norm, m_w_in, m_rwkv_mu, m_rwkv_w0, m_rwkv_w_up, m_rwkv_a0, m_rwkv_a_up, m_rwkv_g_up, m_rwkv_k_k, m_rwkv_k_a, m_rwkv_r_k, m_rwkv_lnx_w, m_rwkv_lnx_b, m_w_out, m_ffn2_norm, m_ffn2_w_gate, m_ffn2_w_up, m_ffn2_w_down, m_final_norm)))
    v = dict(zip(WEIGHTS, (v_meta_tokens, v_ffn1_norm, v_ffn1_w_gate, v_ffn1_w_up, v_ffn1_w_down, v_mix_norm, v_w_in, v_rwkv_mu, v_rwkv_w0, v_rwkv_w_up, v_rwkv_a0, v_rwkv_a_up, v_rwkv_g_up, v_rwkv_k_k, v_rwkv_k_a, v_rwkv_r_k, v_rwkv_lnx_w, v_rwkv_lnx_b, v_w_out, v_ffn2_norm, v_ffn2_w_gate, v_ffn2_w_up, v_ffn2_w_down, v_final_norm)))
    loss, grad_x, grad, delta, new_m, new_v = _step(x, loss_target, w, m, v)
    return (loss, grad_x, *[grad[n] for n in WEIGHTS], *[delta[n] for n in WEIGHTS],
            *[new_m[n] for n in WEIGHTS], *[new_v[n] for n in WEIGHTS])
```

```python
import functools
import types

import jax
import jax.numpy as jnp
from jax import lax
from jax.experimental import pallas as pl
from jax.experimental.pallas import tpu as pltpu

F32 = jnp.float32
BF16 = jnp.bfloat16

RMS_EPS = 1e-6
LNX_EPS = 64e-5
N_META = 16
ROW0 = 128
META_PAD = ROW0 - N_META
HEAD = 64
N_HEADS = 8
GROUP = N_HEADS * HEAD
LORA_W, LORA_A, LORA_G = 32, 32, 96
LORA_PAD = 256
RW_COLS = 3 * GROUP + LORA_PAD
IN_COLS_PAD = 3 * GROUP + RW_COLS
ATT_BLOCK = 128
CHUNK = 64
SCAN_CHUNKS = 2
VMEM_LIMIT = 56 * 1024 * 1024

ADAM_LR, ADAM_B1, ADAM_B2, ADAM_EPS, ADAM_WD, ADAM_STEP = 0.001, 0.9, 0.999, 1e-08, 0.01, 10

MESH = pl.DeviceIdType.MESH


def _params(*sem):
    return pltpu.CompilerParams(dimension_semantics=tuple(sem), vmem_limit_bytes=VMEM_LIMIT)


def _dot(a, b):
    return lax.dot_general(a, b, (((1,), (0,)), ((), ())), preferred_element_type=F32)


def _dot_nt(a, b):
    return lax.dot_general(a, b, (((1,), (1,)), ((), ())), preferred_element_type=F32)


def _dot_tn(a, b):
    return lax.dot_general(a, b, (((0,), (0,)), ((), ())), preferred_element_type=F32)


def _split2(x):
    hi = x.astype(BF16)
    return hi, (x - hi.astype(F32)).astype(BF16)


def _sigmoid(x):
    return 1.0 / (1.0 + jnp.exp(-x))


def _rms_fwd(x, g):
    rstd = lax.rsqrt(jnp.mean(x * x, axis=-1, keepdims=True) + RMS_EPS)
    xhat = x * rstd
    return xhat * g, xhat, rstd


def _rms_bwd(dn, xhat, rstd, g):
    dxhat = dn * g
    dx = rstd * (dxhat - xhat * jnp.mean(dxhat * xhat, axis=-1, keepdims=True))
    return dx, jnp.sum(dn * xhat, axis=0, keepdims=True)


def _row_tile(rows):
    return 384 if rows % 384 == 0 else 128


def _half_tile(cols):
    return cols // 2 if cols % 256 == 0 else cols


def _tall_tile(rows, parts):
    return rows // parts if rows % (16 * parts) == 0 else _row_tile(rows)


def _call_with_exchange(name, body, grid, in_specs, out_specs, out_shape, scratch, operands, params, exchange):
    if exchange is None or not exchange[1]:
        return pl.pallas_call(body, name=name, grid=grid, in_specs=in_specs, out_specs=out_specs,
                              out_shape=out_shape, scratch_shapes=scratch, compiler_params=params)(*operands)
    kind, arrays = exchange
    ns, n_in, n_out, n_scr = len(arrays), len(in_specs), len(out_specs), len(scratch)
    if kind == "gather":
        results = [jax.ShapeDtypeStruct((N_CHIPS,) + s.shape, s.dtype) for s in arrays]
        sems = _gather_sems(ns)
    else:
        results = [jax.ShapeDtypeStruct(s.shape, s.dtype) for s in arrays] * 2
        sems = _reduce_sems(ns)
    n_res = len(results)

    def carried(*refs):
        at = n_in + ns + n_out
        sent, landed = refs[n_in:n_in + ns], refs[at:at + n_res]
        own_scratch, sem_refs = refs[at + n_res:at + n_res + n_scr], refs[at + n_res + n_scr:]
        first, last = _first_and_last_step(grid)
        if kind == "gather":
            start, finish = _gather_exchange(sent, landed, sem_refs)
        else:
            start, finish = _reduce_exchange(sent, landed[:ns], landed[ns:], sem_refs)
        pl.when(first)(start)
        body(*refs[:n_in], *refs[n_in + ns:at], *own_scratch)
        pl.when(last)(finish)

    return pl.pallas_call(
        carried, name=name, grid=grid, in_specs=list(in_specs) + [HBM] * ns, out_specs=list(out_specs) + [HBM] * n_res,
        out_shape=list(out_shape) + results, scratch_shapes=list(scratch) + sems, compiler_params=params,
    )(*operands, *arrays)


def _ffn_fwd(name, h, g, wg, wu, wd, exchange=None):
    rows, d = h.shape
    f = wg.shape[0]
    tm, tf = _row_tile(rows), _half_tile(f)
    nj = f // tf

    def body(h_ref, g_ref, wg_ref, wu_ref, wd_ref, ho_ref, a_ref, b_ref, n_sc, acc_sc):
        j = pl.program_id(1)

        @pl.when(j == 0)
        def _():
            n, _, _ = _rms_fwd(h_ref[...], g_ref[...])
            n_sc[...] = n.astype(BF16)
            acc_sc[...] = jnp.zeros_like(acc_sc)

        n = n_sc[...]
        a = _dot_nt(n, wg_ref[...])
        b = _dot_nt(n, wu_ref[...])
        a_ref[...] = a
        b_ref[...] = b
        s = a * _sigmoid(a) * b
        acc_sc[...] += _dot(s.astype(BF16), wd_ref[...])

        @pl.when(j == nj - 1)
        def _():
            ho_ref[...] = h_ref[...] + 0.5 * acc_sc[...]

    return _call_with_exchange(
        name, body, (rows // tm, nj),
        [pl.BlockSpec((tm, d), lambda i, j: (i, 0)),
         pl.BlockSpec((1, d), lambda i, j: (0, 0)),
         pl.BlockSpec((tf, d), lambda i, j: (j, 0)),
         pl.BlockSpec((tf, d), lambda i, j: (j, 0)),
         pl.BlockSpec((tf, d), lambda i, j: (j, 0))],
        [pl.BlockSpec((tm, d), lambda i, j: (i, 0)),
         pl.BlockSpec((tm, tf), lambda i, j: (i, j)),
         pl.BlockSpec((tm, tf), lambda i, j: (i, j))],
        [jax.ShapeDtypeStruct((rows, d), F32),
         jax.ShapeDtypeStruct((rows, f), F32),
         jax.ShapeDtypeStruct((rows, f), F32)],
        [pltpu.VMEM((tm, d), BF16), pltpu.VMEM((tm, d), F32)],
        (h, g, wg, wu, wd), _params("arbitrary", "arbitrary"), exchange)


def _ffn_bwd(name, dh, h, g, a, b, wg, wu, wd, exchange=None):
    rows, d = h.shape
    f = wg.shape[0]
    tm, tf = _row_tile(rows), _half_tile(f)
    ni, nj = rows // tm, f // tf

    def body(dh_ref, h_ref, g_ref, a_ref, b_ref, wg_ref, wu_ref, wd_ref,
             dhi_ref, da_ref, db_ref, s_ref, n_ref, dhh_ref, dg_ref, dn_sc):
        i, j = pl.program_id(0), pl.program_id(1)

        @pl.when(j == 0)
        def _():
            n, _, _ = _rms_fwd(h_ref[...], g_ref[...])
            n_ref[...] = n.astype(BF16)
            dhh_ref[...] = (0.5 * dh_ref[...]).astype(BF16)
            dn_sc[...] = jnp.zeros_like(dn_sc)

        @pl.when((i == 0) & (j == 0))
        def _():
            dg_ref[...] = jnp.zeros_like(dg_ref)

        ds = _dot_nt(dhh_ref[...], wd_ref[...])
        av, bv = a_ref[...], b_ref[...]
        sig = _sigmoid(av)
        silu = av * sig
        s_ref[...] = (silu * bv).astype(BF16)
        db = (ds * silu).astype(BF16)
        da = (ds * bv * (sig * (1.0 + av * (1.0 - sig)))).astype(BF16)
        da_ref[...] = da
        db_ref[...] = db
        dn_sc[...] += _dot(da, wg_ref[...]) + _dot(db, wu_ref[...])

        @pl.when(j == nj - 1)
        def _():
            gv = g_ref[...]
            _, xhat, rstd = _rms_fwd(h_ref[...], gv)
            dx, dg = _rms_bwd(dn_sc[...], xhat, rstd, gv)
            dhi_ref[...] = dh_ref[...] + dx
            dg_ref[...] += dg

    return _call_with_exchange(
        name, body, (ni, nj),
        [pl.BlockSpec((tm, d), lambda i, j: (i, 0)),
         pl.BlockSpec((tm, d), lambda i, j: (i, 0)),
         pl.BlockSpec((1, d), lambda i, j: (0, 0)),
         pl.BlockSpec((tm, tf), lambda i, j: (i, j)),
         pl.BlockSpec((tm, tf), lambda i, j: (i, j)),
         pl.BlockSpec((tf, d), lambda i, j: (j, 0)),
         pl.BlockSpec((tf, d), lambda i, j: (j, 0)),
         pl.BlockSpec((tf, d), lambda i, j: (j, 0))],
        [pl.BlockSpec((tm, d), lambda i, j: (i, 0)),
         pl.BlockSpec((tm, tf), lambda i, j: (i, j)),
         pl.BlockSpec((tm, tf), lambda i, j: (i, j)),
         pl.BlockSpec((tm, tf), lambda i, j: (i, j)),
         pl.BlockSpec((tm, d), lambda i, j: (i, 0)),
         pl.BlockSpec((tm, d), lambda i, j: (i, 0)),
         pl.BlockSpec((1, d), lambda i, j: (0, 0))],
        [jax.ShapeDtypeStruct((rows, d), F32),
         jax.ShapeDtypeStruct((rows, f), BF16),
         jax.ShapeDtypeStruct((rows, f), BF16),
         jax.ShapeDtypeStruct((rows, f), BF16),
         jax.ShapeDtypeStruct((rows, d), BF16),
         jax.ShapeDtypeStruct((rows, d), BF16),
         jax.ShapeDtypeStruct((1, d), F32)],
        [pltpu.VMEM((tm, d), F32)],
        (dh, h, g, a, b, wg, wu, wd), _params("arbitrary", "arbitrary"), exchange)


def _mm_tn(name, a, b, exchange=None):
    k, m = a.shape
    n = b.shape[1]
    tk = _tall_tile(k, 3)
    tm = _half_tile(m) if m > 1024 else m
    tn = _half_tile(n) if n > 1024 else n
    nk = k // tk

    def body(a_ref, b_ref, o_ref, acc):
        kk = pl.program_id(2)

        @pl.when(kk == 0)
        def _():
            acc[...] = jnp.zeros_like(acc)

        acc[...] += _dot_tn(a_ref[...], b_ref[...])

        @pl.when(kk == nk - 1)
        def _():
            o_ref[...] = acc[...].astype(BF16)

    outs = _call_with_exchange(
        name, body, (m // tm, n // tn, nk),
        [pl.BlockSpec((tk, tm), lambda i, j, kk: (kk, i)),
         pl.BlockSpec((tk, tn), lambda i, j, kk: (kk, j))],
        [pl.BlockSpec((tm, tn), lambda i, j, kk: (i, j))],
        [jax.ShapeDtypeStruct((m, n), BF16)],
        [pltpu.VMEM((tm, tn), F32)],
        (a, b), _params("arbitrary", "arbitrary", "arbitrary"), exchange)
    return outs if exchange else outs[0]


def _norm_proj(name, h, g, w):
    rows, d = h.shape
    n = w.shape[1]
    split = 3 * GROUP
    tm = _row_tile(rows)

    def body(h_ref, g_ref, w_ref, qkv_ref, p_ref, n_ref):
        nv, _, _ = _rms_fwd(h_ref[...], g_ref[...])
        nb = nv.astype(BF16)
        n_ref[...] = nb
        qkv_ref[...] = _dot(nb, w_ref[:, :split]).astype(BF16)
        p_ref[...] = _dot(nb, w_ref[:, split:])

    return pl.pallas_call(
        body, name=name, grid=(rows // tm,),
        in_specs=[pl.BlockSpec((tm, d), lambda i: (i, 0)),
                  pl.BlockSpec((1, d), lambda i: (0, 0)),
                  pl.BlockSpec((d, n), lambda i: (0, 0))],
        out_specs=[pl.BlockSpec((tm, split), lambda i: (i, 0)),
                   pl.BlockSpec((tm, n - split), lambda i: (i, 0)),
                   pl.BlockSpec((tm, d), lambda i: (i, 0))],
        out_shape=[jax.ShapeDtypeStruct((rows, split), BF16), jax.ShapeDtypeStruct((rows, n - split), F32),
                   jax.ShapeDtypeStruct((rows, d), BF16)],
        compiler_params=_params("arbitrary"),
    )(h, g, w)


def _out_proj(name, h, sb, rw, w):
    rows, d = h.shape
    gw = sb.shape[1]
    tm = _row_tile(rows)

    def body(h_ref, sb_ref, rw_ref, w_ref, o_ref, mix_ref):
        mix_ref[:, :gw] = sb_ref[...].astype(BF16)
        mix_ref[:, gw:] = rw_ref[...].astype(BF16)
        o_ref[...] = h_ref[...] + _dot(mix_ref[...], w_ref[...])

    return pl.pallas_call(
        body, name=name, grid=(rows // tm,),
        in_specs=[pl.BlockSpec((tm, d), lambda i: (i, 0)),
                  pl.BlockSpec((tm, gw), lambda i: (i, 0)),
                  pl.BlockSpec((tm, gw), lambda i: (i, 0)),
                  pl.BlockSpec((2 * gw, d), lambda i: (0, 0))],
        out_specs=[pl.BlockSpec((tm, d), lambda i: (i, 0)),
                   pl.BlockSpec((tm, 2 * gw), lambda i: (i, 0))],
        out_shape=[jax.ShapeDtypeStruct((rows, d), F32), jax.ShapeDtypeStruct((rows, 2 * gw), BF16)],
        compiler_params=_params("arbitrary"),
    )(h, sb, rw, w)


def _out_proj_bwd(name, dh, w):
    rows, d = dh.shape
    k = w.shape[0]
    tm = _row_tile(rows)

    def body(dh_ref, w_ref, dsb_ref, drw_ref, dhb_ref):
        dhb = dh_ref[...].astype(BF16)
        dhb_ref[...] = dhb
        dsb_ref[...] = _dot_nt(dhb, w_ref[:GROUP, :]).astype(BF16)
        drw_ref[...] = _dot_nt(dhb, w_ref[GROUP:, :])

    return pl.pallas_call(
        body, name=name, grid=(rows // tm,),
        in_specs=[pl.BlockSpec((tm, d), lambda i: (i, 0)),
                  pl.BlockSpec((k, d), lambda i: (0, 0))],
        out_specs=[pl.BlockSpec((tm, GROUP), lambda i: (i, 0)),
                   pl.BlockSpec((tm, GROUP), lambda i: (i, 0)),
                   pl.BlockSpec((tm, d), lambda i: (i, 0))],
        out_shape=[jax.ShapeDtypeStruct((rows, GROUP), BF16), jax.ShapeDtypeStruct((rows, GROUP), F32),
                   jax.ShapeDtypeStruct((rows, d), BF16)],
        compiler_params=_params("arbitrary"),
    )(dh, w)


def _norm_proj_bwd(name, dproj, w, h, g, dh):
    rows, n = dproj.shape
    d = w.shape[0]
    tm = _row_tile(rows)

    def body(dp_ref, w_ref, h_ref, g_ref, dh_ref, o_ref, dg_ref):
        @pl.when(pl.program_id(0) == 0)
        def _():
            dg_ref[...] = jnp.zeros_like(dg_ref)

        dn = _dot_nt(dp_ref[...], w_ref[...])
        gv = g_ref[...]
        _, xhat, rstd = _rms_fwd(h_ref[...], gv)
        dx, dg = _rms_bwd(dn, xhat, rstd, gv)
        o_ref[...] = dh_ref[...] + dx
        dg_ref[...] += dg

    return pl.pallas_call(
        body, name=name, grid=(rows // tm,),
        in_specs=[pl.BlockSpec((tm, n), lambda i: (i, 0)),
                  pl.BlockSpec((d, n), lambda i: (0, 0)),
                  pl.BlockSpec((tm, d), lambda i: (i, 0)),
                  pl.BlockSpec((1, d), lambda i: (0, 0)),
                  pl.BlockSpec((tm, d), lambda i: (i, 0))],
        out_specs=[pl.BlockSpec((tm, d), lambda i: (i, 0)),
                   pl.BlockSpec((1, d), lambda i: (0, 0))],
        out_shape=[jax.ShapeDtypeStruct((rows, d), F32), jax.ShapeDtypeStruct((1, d), F32)],
        compiler_params=_params("arbitrary"),
    )(dproj, w, h, g, dh)


def _loss_head(name, h, g, tgt):
    rows, d = h.shape
    tm = _row_tile(rows)

    def body(h_ref, g_ref, t_ref, loss_ref, dh_ref, dg_ref):
        i = pl.program_id(0)

        @pl.when(i == 0)
        def _():
            loss_ref[...] = jnp.zeros_like(loss_ref)
            dg_ref[...] = jnp.zeros_like(dg_ref)

        gv = g_ref[...]
        y, xhat, rstd = _rms_fwd(h_ref[...], gv)
        row = i * tm + lax.broadcasted_iota(jnp.int32, (tm, 1), 0)
        diff = jnp.where(row >= ROW0, y - t_ref[...], 0.0)
        part = 0.5 * jnp.sum(jnp.sum(diff * diff, axis=-1, keepdims=True), axis=0, keepdims=True) / d
        loss_ref[...] += jnp.broadcast_to(part, loss_ref.shape)
        dx, dg = _rms_bwd(diff / d, xhat, rstd, gv)
        dh_ref[...] = dx
        dg_ref[...] += dg

    return pl.pallas_call(
        body, name=name, grid=(rows // tm,),
        in_specs=[pl.BlockSpec((tm, d), lambda i: (i, 0)),
                  pl.BlockSpec((1, d), lambda i: (0, 0)),
                  pl.BlockSpec((tm, d), lambda i: (i, 0))],
        out_specs=[pl.BlockSpec((8, 128), lambda i: (0, 0)),
                   pl.BlockSpec((tm, d), lambda i: (i, 0)),
                   pl.BlockSpec((1, d), lambda i: (0, 0))],
        out_shape=[jax.ShapeDtypeStruct((8, 128), F32),
                   jax.ShapeDtypeStruct((rows, d), F32),
                   jax.ShapeDtypeStruct((1, d), F32)],
        compiler_params=_params("arbitrary"),
    )(h, g, tgt)


def _sb_block(qb, kb, q0, jb, scale):
    bq, bk = qb.shape[0], kb.shape[0]
    z = _dot_nt(qb, kb) * scale
    qpos = q0 + lax.broadcasted_iota(jnp.int32, (bq, bk), 0)
    kpos = jb * bk + lax.broadcasted_iota(jnp.int32, (bq, bk), 1)
    valid = (kpos < qpos) & (kpos >= META_PAD)
    e = jnp.exp(-jnp.abs(z))
    log_keep = jnp.where(valid, -(jnp.maximum(z, 0.0) + jnp.log(1.0 + e)), 0.0)
    return z, valid, e, log_keep


def _tri2(n, cmp):
    r = lax.broadcasted_iota(jnp.int32, (2 * n, n), 0) % n
    c = lax.broadcasted_iota(jnp.int32, (2 * n, n), 1)
    return cmp(r, c).astype(BF16)


def _dot_split(x, t2):
    hi, lo = _split2(x)
    return _dot(jnp.concatenate([hi, lo], axis=1), t2)


ATT_HEADS = 128 // HEAD
ATT_CUT = -104.0
ATT_TILES = GROUP // 128


def _lanes(hh):
    return slice(hh * HEAD, (hh + 1) * HEAD)


def _first_and_last_step(grid):
    here = [pl.program_id(a) for a in range(len(grid))]
    first, last = here[0] == 0, here[0] == grid[0] - 1
    for a in range(1, len(grid)):
        first, last = first & (here[a] == 0), last & (here[a] == grid[a] - 1)
    return first, last


def _sb_fwd(name, qkv, shards=()):
    rows = qkv.shape[0]
    nh, dh = N_HEADS, HEAD
    bq, bk, hg = _row_tile(rows), ATT_BLOCK, ATT_HEADS
    per = bq // bk
    scale = dh ** -0.5
    ns = len(shards)
    grid = (nh // hg, rows // bq)

    def body(q_ref, k_ref, v_ref, *rest):
        o_ref, rt_ref, cnt_ref = rest[ns:ns + 3]
        if ns:
            first, last = _first_and_last_step(grid)
            start, finish = _gather_exchange(rest[:ns], rest[ns + 3:2 * ns + 3], rest[2 * ns + 3:])
            pl.when(first)(start)
        i = pl.program_id(1)
        after = _tri2(bk, lambda r, c: r > c)
        nkb = (i + 1) * per

        def live(state):
            n, carry = state
            top = jnp.max(carry[0][0])
            for hh in range(1, hg):
                top = jnp.maximum(top, jnp.max(carry[hh][0]))
            return (n < nkb) & (top >= ATT_CUT)

        def visit(carry, jb, r0):
            off = pl.multiple_of(jb * bk, bk)
            out = []
            for hh in range(hg):
                rest, acc = carry[hh]
                kb = k_ref[pl.ds(off, bk), _lanes(hh)]
                vb = v_ref[pl.ds(off, bk), _lanes(hh)]
                z, valid, _, log_keep = _sb_block(q_ref[r0:, _lanes(hh)], kb, i * bq + r0, jb, scale)
                log_rest = rest[r0:] + _dot_split(log_keep, after)
                attn = jnp.where(valid, jnp.exp(z + log_keep + log_rest), 0.0)
                new_rest = rest[r0:] + jnp.sum(log_keep, axis=-1, keepdims=True)
                new_acc = acc[r0:] + _dot(attn.astype(BF16), vb)
                if r0:
                    new_rest = jnp.concatenate([rest[:r0], new_rest], axis=0)
                    new_acc = jnp.concatenate([acc[:r0], new_acc], axis=0)
                out.append((new_rest, new_acc))
            return tuple(out)

        carry = tuple((jnp.zeros((bq, 1), F32), jnp.zeros((bq, dh), F32)) for _ in range(hg))
        for dgl in reversed(range(per)):
            carry = visit(carry, i * per + dgl, dgl * bk)
        n, res = lax.while_loop(live, lambda s: (s[0] + 1, visit(s[1], nkb - 1 - s[0], 0)), (jnp.int32(per), carry))
        for hh in range(hg):
            rt_ref[hh] = res[hh][0]
            o_ref[:, _lanes(hh)] = res[hh][1]
            cnt_ref[hh] = jnp.full((bq, 1), n, F32)
        if ns:
            pl.when(last)(finish)

    return pl.pallas_call(
        body, name=name, grid=grid,
        in_specs=[pl.BlockSpec((bq, 128), lambda h, i: (i, h)),
                  pl.BlockSpec((rows, 128), lambda h, i: (0, ATT_TILES + h)),
                  pl.BlockSpec((rows, 128), lambda h, i: (0, 2 * ATT_TILES + h))] + [HBM] * ns,
        out_specs=[pl.BlockSpec((bq, 128), lambda h, i: (i, h)),
                   pl.BlockSpec((hg, bq, 1), lambda h, i: (h, i, 0)),
                   pl.BlockSpec((hg, bq, 1), lambda h, i: (h, i, 0))] + [HBM] * ns,
        out_shape=[jax.ShapeDtypeStruct((rows, GROUP), F32), jax.ShapeDtypeStruct((nh, rows, 1), F32),
                   jax.ShapeDtypeStruct((nh, rows, 1), F32)]
        + [jax.ShapeDtypeStruct((N_CHIPS,) + s.shape, s.dtype) for s in shards],
        scratch_shapes=_gather_sems(ns) if ns else [],
        compiler_params=_params("arbitrary", "arbitrary"),
    )(qkv, qkv, qkv, *shards)


def _sb_bwd(name, qkv, rt, cnt, do, parts=()):
    rows = qkv.shape[0]
    nh, dh = N_HEADS, HEAD
    bq, bk, hg = _row_tile(rows), ATT_BLOCK, ATT_HEADS
    per = bq // bk
    scale = dh ** -0.5
    ns = len(parts)
    grid = (nh // hg, rows // bq)

    def body(q_ref, k_ref, v_ref, rt_ref, cnt_ref, do_ref, *rest):
        dq_ref, dk_ref, dv_ref = rest[ns:ns + 3]
        if ns:
            at_first, at_last = _first_and_last_step(grid)
            start, finish = _reduce_exchange(rest[:ns], rest[ns + 3:2 * ns + 3], rest[2 * ns + 3:3 * ns + 3],
                                             rest[3 * ns + 3:])
            pl.when(at_first)(start)
        i = pl.program_id(1)

        @pl.when(i == 0)
        def _():
            dk_ref[...] = jnp.zeros_like(dk_ref)
            dv_ref[...] = jnp.zeros_like(dv_ref)

        upto = _tri2(bk, lambda r, c: r <= c)
        before = _tri2(bk, lambda r, c: r < c)
        nkb = (i + 1) * per
        first = nkb - jnp.max(cnt_ref[0]).astype(jnp.int32)

        def visit(carry, jb, r0):
            off = pl.multiple_of(jb * bk, bk)
            out = []
            for hh in range(hg):
                keep_sum, g_sum, dq = carry[hh]
                qb, dob = q_ref[r0:, _lanes(hh)], do_ref[r0:, _lanes(hh)]
                kb = k_ref[pl.ds(off, bk), _lanes(hh)]
                vb = v_ref[pl.ds(off, bk), _lanes(hh)]
                z, valid, e, log_keep = _sb_block(qb, kb, i * bq + r0, jb, scale)
                log_rest = rt_ref[hh, r0:, :] - keep_sum[r0:] - _dot_split(log_keep, upto)
                attn = jnp.where(valid, jnp.exp(z + log_keep + log_rest), 0.0)
                g = attn * _dot_nt(dob, vb)
                g_before = g_sum[r0:] + _dot_split(g, before)
                inv = 1.0 / (1.0 + e)
                sig = jnp.where(z >= 0, inv, e * inv)
                dz = (jnp.where(valid, g * (1.0 - sig) - g_before * sig, 0.0) * scale).astype(BF16)
                dk_ref[pl.ds(off, bk), _lanes(hh)] += _dot_tn(dz, qb)
                dv_ref[pl.ds(off, bk), _lanes(hh)] += _dot_tn(attn.astype(BF16), dob)
                new = (keep_sum[r0:] + jnp.sum(log_keep, axis=-1, keepdims=True),
                       g_sum[r0:] + jnp.sum(g, axis=-1, keepdims=True),
                       dq[r0:] + _dot(dz, kb))
                if r0:
                    new = tuple(jnp.concatenate([old[:r0], x], axis=0) for old, x in zip(carry[hh], new))
                out.append(new)
            return tuple(out)

        zero = jnp.zeros((bq, 1), F32)
        res = lax.fori_loop(first, nkb - per, lambda jb, c: visit(c, jb, 0),
                            tuple((zero, zero, jnp.zeros((bq, dh), F32)) for _ in range(hg)))
        for dgl in range(per):
            res = visit(res, i * per + dgl, dgl * bk)
        for hh in range(hg):
            dq_ref[:, _lanes(hh)] = res[hh][2]
        if ns:
            pl.when(at_last)(finish)

    return pl.pallas_call(
        body, name=name, grid=grid,
        in_specs=[pl.BlockSpec((bq, 128), lambda h, i: (i, h)),
                  pl.BlockSpec((rows, 128), lambda h, i: (0, ATT_TILES + h)),
                  pl.BlockSpec((rows, 128), lambda h, i: (0, 2 * ATT_TILES + h)),
                  pl.BlockSpec((hg, bq, 1), lambda h, i: (h, i, 0)),
                  pl.BlockSpec((hg, bq, 1), lambda h, i: (h, i, 0)),
                  pl.BlockSpec((bq, 128), lambda h, i: (i, h))] + [HBM] * ns,
        out_specs=[pl.BlockSpec((bq, 128), lambda h, i: (i, h)),
                   pl.BlockSpec((rows, 128), lambda h, i: (0, h)),
                   pl.BlockSpec((rows, 128), lambda h, i: (0, h))] + [HBM] * (2 * ns),
        out_shape=[jax.ShapeDtypeStruct((rows, GROUP), F32)] * 3
        + [jax.ShapeDtypeStruct(s.shape, s.dtype) for s in parts] * 2,
        scratch_shapes=_reduce_sems(ns) if ns else [],
        compiler_params=_params("arbitrary", "arbitrary"),
    )(qkv, qkv, qkv, rt, cnt, do, *parts)


def _head_sum(x, ones_bd):
    return _dot_split(x, ones_bd)


def _rwkv_pre(p, p_prev, mu, w0, a0, k_k, k_a, w_up, a_up, g_up, ones_bd):
    xs = p + (p_prev - p) * mu
    r = xs[:, :GROUP]
    k0 = xs[:, GROUP:2 * GROUP]
    v = xs[:, 2 * GROUP:3 * GROUP]
    lo = xs[:, 3 * GROUP:]
    wa = w0 + _dot(jnp.tanh(lo).astype(BF16), w_up.astype(BF16))
    w = -(jnp.maximum(-wa, 0.0) + jnp.log(1.0 + jnp.exp(-jnp.abs(wa)))) - 0.5
    log_decay = -jnp.exp(w)
    alpha = _sigmoid(a0 + _dot(lo.astype(BF16), a_up.astype(BF16)))
    gate = _dot(_sigmoid(lo).astype(BF16), g_up.astype(BF16))
    kk = k0 * k_k
    kk = kk * lax.rsqrt(jnp.maximum(_head_sum(kk * kk, ones_bd), 1e-24))
    k = k0 * (1.0 + (alpha - 1.0) * k_a)
    return r, log_decay, k, v, -kk, kk * alpha, gate


def _rwkv_post(y, r, k, v, gate, lnx_w, lnx_b, r_k, ones_bd):
    mean = _head_sum(y, ones_bd) * (1.0 / HEAD)
    yc = y - mean
    var = _head_sum(yc * yc, ones_bd) * (1.0 / HEAD)
    yn = yc * lax.rsqrt(var + LNX_EPS) * lnx_w + lnx_b
    bonus = _head_sum(r * k * r_k, ones_bd) * v
    return (yn + bonus) * gate


TOKEN_TILE = 128
_PRE_VEC = 5
_PRE_MAT = 3


def _split_heads(o_ref, val):
    for h in range(N_HEADS):
        o_ref[h] = val[:, _lanes(h)]


def _merge_heads(ref):
    return jnp.concatenate([ref[h] for h in range(N_HEADS)], axis=1)


def _head_spec(tm):
    return pl.BlockSpec((N_HEADS, tm, HEAD), lambda i: (0, i, 0))


def _rwkv_pre_fwd(name, p, p_prev, vecs, mats, ones_bd):
    rows = p.shape[0]
    tm = TOKEN_TILE
    row_spec = lambda w: pl.BlockSpec((tm, w), lambda i: (i, 0))
    full = lambda a: pl.BlockSpec(a.shape, lambda i: (0,) * a.ndim)

    def body(p_ref, pp_ref, *refs):
        ins = [r[...] for r in refs[:_PRE_VEC + _PRE_MAT + 1]]
        outs = refs[_PRE_VEC + _PRE_MAT + 1:]
        vals = _rwkv_pre(p_ref[...], pp_ref[...], *ins)
        for o_ref, val in zip(outs[:6], vals[:6]):
            _split_heads(o_ref, val)
        for o_ref, val in zip(outs[6:], (vals[0], vals[2], vals[3], vals[6])):
            o_ref[...] = val

    return pl.pallas_call(
        body, name=name, grid=(rows // tm,),
        in_specs=[row_spec(RW_COLS), row_spec(RW_COLS)] + [full(a) for a in (*vecs, *mats, ones_bd)],
        out_specs=[_head_spec(tm)] * 6 + [row_spec(GROUP)] * 4,
        out_shape=([jax.ShapeDtypeStruct((N_HEADS, rows, HEAD), F32)] * 6
                   + [jax.ShapeDtypeStruct((rows, GROUP), F32)] * 4),
        compiler_params=_params("arbitrary"),
    )(p, p_prev, *vecs, *mats, ones_bd)


def _rwkv_pre_bwd(name, p, p_prev, vecs, mats, ones_bd, cts_scan, ct_gate, cts_b):
    rows = p.shape[0]
    tm = TOKEN_TILE
    n_par = _PRE_VEC + _PRE_MAT
    row_spec = lambda w: pl.BlockSpec((tm, w), lambda i: (i, 0))
    full = lambda a: pl.BlockSpec(a.shape, lambda i: (0,) * a.ndim)

    def body(*refs):
        p_ref, pp_ref = refs[0], refs[1]
        par = [r[...] for r in refs[2:2 + n_par]]
        ones = refs[2 + n_par][...]
        cta = [_merge_heads(r) for r in refs[3 + n_par:9 + n_par]] + [refs[9 + n_par][...]]
        ctb = [r[...] for r in refs[10 + n_par:13 + n_par]]
        outs = refs[13 + n_par:]
        ct = (cta[0] + ctb[0], cta[1], cta[2] + ctb[1], cta[3] + ctb[2], cta[4], cta[5], cta[6])
        _, vjp = jax.vjp(lambda pv, ppv, *pr: _rwkv_pre(pv, ppv, *pr, ones), p_ref[...], pp_ref[...], *par)
        grads = vjp(ct)
        outs[0][...] = grads[0]
        outs[1][...] = grads[1]

        @pl.when(pl.program_id(0) == 0)
        def _():
            for o_ref in outs[2:]:
                o_ref[...] = jnp.zeros_like(o_ref)

        for o_ref, gval in zip(outs[2:], grads[2:]):
            o_ref[...] += gval

    par_arrays = (*vecs, *mats)
    return pl.pallas_call(
        body, name=name, grid=(rows // tm,),
        in_specs=([row_spec(RW_COLS)] * 2 + [full(a) for a in (*par_arrays, ones_bd)]
                  + [_head_spec(tm)] * 6 + [row_spec(GROUP)] * 4),
        out_specs=[row_spec(RW_COLS)] * 2 + [full(a) for a in par_arrays],
        out_shape=([jax.ShapeDtypeStruct((rows, RW_COLS), F32)] * 2
                   + [jax.ShapeDtypeStruct(a.shape, F32) for a in par_arrays]),
        compiler_params=_params("arbitrary"),
    )(p, p_prev, *par_arrays, ones_bd, *cts_scan, ct_gate, *cts_b)


def _rwkv_post_fwd(name, y, r, k, v, gate, vecs, ones_bd):
    rows = r.shape[0]
    tm = TOKEN_TILE
    row_spec = pl.BlockSpec((tm, GROUP), lambda i: (i, 0))
    full = lambda a: pl.BlockSpec(a.shape, lambda i: (0,) * a.ndim)

    def body(y_ref, *refs):
        vals = [r_[...] for r_ in refs[:-1]]
        refs[-1][...] = _rwkv_post(_merge_heads(y_ref), *vals)

    return pl.pallas_call(
        body, name=name, grid=(rows // tm,),
        in_specs=[_head_spec(tm)] + [row_spec] * 4 + [full(a) for a in (*vecs, ones_bd)],
        out_specs=row_spec,
        out_shape=jax.ShapeDtypeStruct((rows, GROUP), F32),
        compiler_params=_params("arbitrary"),
    )(y, r, k, v, gate, *vecs, ones_bd)


def _rwkv_post_bwd(name, y, r, k, v, gate, vecs, ones_bd, dout):
    rows = r.shape[0]
    tm = TOKEN_TILE
    row_spec = pl.BlockSpec((tm, GROUP), lambda i: (i, 0))
    full = lambda a: pl.BlockSpec(a.shape, lambda i: (0,) * a.ndim)

    def body(y_ref, *refs):
        vals = [_merge_heads(y_ref)] + [r_[...] for r_ in refs[:7]]
        ones = refs[7][...]
        dout_v = refs[8][...]
        outs = refs[9:]
        _, vjp = jax.vjp(lambda *a: _rwkv_post(*a, ones), *vals)
        grads = vjp(dout_v)
        _split_heads(outs[0], grads[0])
        for o_ref, gval in zip(outs[1:5], grads[1:5]):
            o_ref[...] = gval

        @pl.when(pl.program_id(0) == 0)
        def _():
            for o_ref in outs[5:]:
                o_ref[...] = jnp.zeros_like(o_ref)

        for o_ref, gval in zip(outs[5:], grads[5:]):
            o_ref[...] += gval

    return pl.pallas_call(
        body, name=name, grid=(rows // tm,),
        in_specs=[_head_spec(tm)] + [row_spec] * 4 + [full(a) for a in (*vecs, ones_bd)] + [row_spec],
        out_specs=[_head_spec(tm)] + [row_spec] * 4 + [full(a) for a in vecs],
        out_shape=([jax.ShapeDtypeStruct((N_HEADS, rows, HEAD), F32)] + [jax.ShapeDtypeStruct((rows, GROUP), F32)] * 4
                   + [jax.ShapeDtypeStruct(a.shape, F32) for a in vecs]),
        compiler_params=_params("arbitrary"),
    )(y, r, k, v, gate, *vecs, ones_bd, dout)


_NN = (((2,), (1,)), ((0,), (0,)))
_NT = (((2,), (2,)), ((0,), (0,)))
_TN = (((1,), (1,)), ((0,), (0,)))


_BWD_FORMS = {"nn": (("nt", False), ("tn", False)),
              "nt": (("nn", False), ("tn", True)),
              "tn": (("nt", True), ("nn", False))}
_DIMS = {"nn": _NN, "nt": _NT, "tn": _TN}


def _bdot(a, b, form):
    return lax.dot_general(a.astype(BF16), b.astype(BF16), _DIMS[form], preferred_element_type=F32)


@functools.partial(jax.custom_vjp, nondiff_argnums=(2,))
def _bmm(a, b, form):
    return _bdot(a, b, form)


def _bmm_fwd(a, b, form):
    return _bdot(a, b, form), (a.astype(BF16), b.astype(BF16))


def _bmm_bwd(form, res, dc):
    a, b = res
    (fa, swap_a), (fb, swap_b) = _BWD_FORMS[form]
    da = _bdot(b, dc, fa) if swap_a else _bdot(dc, b, fa)
    db = _bdot(dc, a, fb) if swap_b else _bdot(a, dc, fb)
    return da, db


_bmm.defvjp(_bmm_fwd, _bmm_bwd)


@jax.custom_vjp
def _cumsum_steps(x):
    return _tri_apply(x, lambda r, c: r >= c)


def _tri_apply(x, cmp):
    nh, c, _ = x.shape
    tri = cmp(lax.broadcasted_iota(jnp.int32, (c, c), 0), lax.broadcasted_iota(jnp.int32, (c, c), 1))
    tri = jnp.broadcast_to(tri.astype(BF16)[None], (nh, c, c))
    hi, lo = _split2(x)
    return (lax.dot_general(tri, hi, _NN, preferred_element_type=F32)
            + lax.dot_general(tri, lo, _NN, preferred_element_type=F32))


_cumsum_steps.defvjp(lambda x: (_cumsum_steps(x), None), lambda _, d: (_tri_apply(d, lambda r, c: r <= c),))


def _chunk(state, r, log_w, k, v, a, b):
    nh, c, _ = r.shape
    row = lax.broadcasted_iota(jnp.int32, (c, c), 0)
    col = lax.broadcasted_iota(jnp.int32, (c, c), 1)
    cum = _cumsum_steps(log_w)
    mid = cum[:, c // 2 - 1:c // 2, :]
    a_t = a * jnp.exp(cum - log_w - mid)
    r_t = r * jnp.exp(cum - mid)
    back = jnp.exp(mid - cum)
    b_t = b * back
    k_t = k * back
    strict, incl = (row > col)[None], (row >= col)[None]
    ar = jnp.concatenate([a_t, r_t], axis=1)
    on_b = _bmm(ar, b_t, "nt")
    on_k = _bmm(ar, k_t, "nt")
    n_mat = jnp.where(strict, on_b[:, :c], 0.0)
    p_mat = jnp.where(incl, on_b[:, c:], 0.0)
    m_mat = jnp.where(strict, on_k[:, :c], 0.0)
    q_mat = jnp.where(incl, on_k[:, c:], 0.0)
    inv, power, span = n_mat, _bmm(n_mat, n_mat, "nn"), 2
    while span < c:
        both = _bmm(jnp.concatenate([power, inv], axis=1), power, "nn")
        inv = inv + power + both[:, c:]
        power = both[:, :c]
        span *= 2
    s_mid = state * jnp.swapaxes(jnp.exp(mid), 1, 2)
    x = _bmm(jnp.concatenate([a_t, m_mat], axis=2), jnp.concatenate([s_mid, v], axis=1), "nn")
    u = x + _bmm(inv, x, "nn")
    y = _bmm(jnp.concatenate([r_t, p_mat, q_mat], axis=2), jnp.concatenate([s_mid, u, v], axis=1), "nn")
    grown = _bmm(jnp.concatenate([b_t, k_t], axis=1), jnp.concatenate([u, v], axis=1), "tn")
    s_new = (s_mid + grown) * jnp.swapaxes(jnp.exp(cum[:, c - 1:c, :] - mid), 1, 2)
    return y, s_new


def _scan_fwd(name, ops):
    nh, rows, dh = ops[0].shape
    nc, per = rows // CHUNK, SCAN_CHUNKS
    spec = pl.BlockSpec((nh, per * CHUNK, dh), lambda c: (0, c, 0))

    def body(r_ref, w_ref, k_ref, v_ref, a_ref, b_ref, y_ref, st_ref, state):
        @pl.when(pl.program_id(0) == 0)
        def _():
            state[...] = jnp.zeros_like(state)

        s = state[...]
        for u in range(per):
            at = slice(u * CHUNK, (u + 1) * CHUNK)
            st_ref[u] = s
            y, s = _chunk(s, *(ref[:, at, :] for ref in (r_ref, w_ref, k_ref, v_ref, a_ref, b_ref)))
            y_ref[:, at, :] = y
        state[...] = s

    return pl.pallas_call(
        body, name=name, grid=(nc // per,),
        in_specs=[spec] * 6,
        out_specs=[spec, pl.BlockSpec((per, nh, dh, dh), lambda c: (c, 0, 0, 0))],
        out_shape=[jax.ShapeDtypeStruct((nh, rows, dh), F32), jax.ShapeDtypeStruct((nc, nh, dh, dh), F32)],
        scratch_shapes=[pltpu.VMEM((nh, dh, dh), F32)],
        compiler_params=_params("arbitrary"),
    )(*ops)


def _scan_bwd(name, ops, states, dy):
    nh, rows, dh = ops[0].shape
    nc, per = rows // CHUNK, SCAN_CHUNKS
    steps = nc // per
    spec = pl.BlockSpec((nh, per * CHUNK, dh), lambda c: (0, steps - 1 - c, 0))

    def body(r_ref, w_ref, k_ref, v_ref, a_ref, b_ref, st_ref, dy_ref, *rest):
        outs, dstate = rest[:6], rest[6]

        @pl.when(pl.program_id(0) == 0)
        def _():
            dstate[...] = jnp.zeros_like(dstate)

        ds = dstate[...]
        for u in reversed(range(per)):
            at = slice(u * CHUNK, (u + 1) * CHUNK)
            _, vjp = jax.vjp(_chunk, st_ref[u], *(ref[:, at, :] for ref in (r_ref, w_ref, k_ref, v_ref, a_ref, b_ref)))
            grads = vjp((dy_ref[:, at, :], ds))
            ds = grads[0]
            for o_ref, gval in zip(outs, grads[1:]):
                o_ref[:, at, :] = gval
        dstate[...] = ds

    return pl.pallas_call(
        body, name=name, grid=(steps,),
        in_specs=[spec] * 6 + [pl.BlockSpec((per, nh, dh, dh), lambda c: (steps - 1 - c, 0, 0, 0)), spec],
        out_specs=[spec] * 6,
        out_shape=[jax.ShapeDtypeStruct((nh, rows, dh), F32)] * 6,
        scratch_shapes=[pltpu.VMEM((nh, dh, dh), F32)],
        compiler_params=_params("arbitrary"),
    )(*ops, states, dy)


def _shift_down(x):
    return jnp.concatenate([jnp.zeros((1, x.shape[1]), x.dtype), x[:-1]], axis=0)


def _shift_up(x):
    return jnp.concatenate([x[1:], jnp.zeros((1, x.shape[1]), x.dtype)], axis=0)


def _pad_rows(x, rows):
    return jnp.concatenate([x, jnp.zeros((rows - x.shape[0],) + x.shape[1:], x.dtype)], axis=0)


def _pad_cols(x, cols):
    return jnp.concatenate([x, jnp.zeros(x.shape[:-1] + (cols - x.shape[-1],), x.dtype)], axis=-1)


def _lora_pad(w_up, a_up, g_up):
    z = lambda n: jnp.zeros((n, GROUP), F32)
    return (jnp.concatenate([w_up, z(LORA_PAD - LORA_W)], 0),
            jnp.concatenate([z(LORA_W), a_up, z(LORA_PAD - LORA_W - LORA_A)], 0),
            jnp.concatenate([z(LORA_W + LORA_A), g_up, z(LORA_PAD - LORA_W - LORA_A - LORA_G)], 0))


MID = ['w_in']
LATE = ['ffn2_w_gate', 'ffn2_w_up', 'ffn2_w_down', 'w_out']


def _local_step(x, tgt, w, late=None):
    d = x.shape[1]
    zeros = jnp.zeros((META_PAD, d), F32)
    h0 = jnp.concatenate([zeros, w["meta_tokens"], x], axis=0)
    tgt_p = jnp.concatenate([jnp.zeros((ROW0, d), F32), tgt], axis=0)
    ones_bd = ((lax.broadcasted_iota(jnp.int32, (2 * GROUP, GROUP), 0) % GROUP) // HEAD
               == lax.broadcasted_iota(jnp.int32, (2 * GROUP, GROUP), 1) // HEAD).astype(BF16)
    pre_vecs = (_pad_cols(w["rwkv_mu"], RW_COLS), w["rwkv_w0"], w["rwkv_a0"], w["rwkv_k_k"], w["rwkv_k_a"])
    pre_mats = _lora_pad(w["rwkv_w_up"], w["rwkv_a_up"], w["rwkv_g_up"])
    post_vecs = (w["rwkv_lnx_w"], w["rwkv_lnx_b"], w["rwkv_r_k"].reshape(1, GROUP))

    h1, a1, b1, *gathered = _ffn_fwd("ffn1_fwd", h0, w["ffn1_norm"], w["ffn1_w_gate"], w["ffn1_w_up"],
                                     w["ffn1_w_down"], late and ("gather", late.shards["mid"]))
    if late is not None:
        w = {**w, **late.join("mid", gathered)}
    w_in = _pad_cols(w["w_in"], IN_COLS_PAD)
    qkv, p, n2 = _norm_proj("in_proj", h1, w["mix_norm"], w_in)
    sb, rest_total, visited, *gathered = _sb_fwd("sb_fwd", qkv, late.shards["late"] if late else ())
    if late is not None:
        w = {**w, **late.join("late", gathered)}
    p_prev = _shift_down(p)
    pre = _rwkv_pre_fwd("rwkv_pre_fwd", p, p_prev, pre_vecs, pre_mats, ones_bd)
    scan_ops, token_ops = pre[:6], pre[6:]
    y, states = _scan_fwd("rwkv_scan_fwd", scan_ops)
    rw = _rwkv_post_fwd("rwkv_post_fwd", y, *token_ops, post_vecs, ones_bd)
    h2, mix = _out_proj("out_proj", h1, sb, rw, w["w_out"])
    h3, a2, b2 = _ffn_fwd("ffn2_fwd", h2, w["ffn2_norm"], w["ffn2_w_gate"], w["ffn2_w_up"], w["ffn2_w_down"])
    loss8, dh3, g_final = _loss_head("loss_head", h3, w["final_norm"].reshape(1, d), tgt_p)

    g = {"final_norm": g_final.reshape(d)}
    dh2, da2, db2, s2, n3, dhh3, g["ffn2_norm"] = _ffn_bwd(
        "ffn2_bwd", dh3, h2, w["ffn2_norm"], a2, b2, w["ffn2_w_gate"], w["ffn2_w_up"], w["ffn2_w_down"])
    g["ffn2_w_gate"] = _mm_tn("ffn2_dgate", da2, n3)
    g["ffn2_w_up"] = _mm_tn("ffn2_dup", db2, n3)
    g["ffn2_w_down"] = _mm_tn("ffn2_ddown", s2, dhh3)
    dsb, drw, dh2b = _out_proj_bwd("out_proj_bwd", dh2, w["w_out"])
    g["w_out"] = _mm_tn("out_proj_dw", mix, dh2b)
    dq, dk, dv, *reduced_late = _sb_bwd("sb_bwd", qkv, rest_total, visited, dsb, late.parts("late", g) if late else ())
    post_g = _rwkv_post_bwd("rwkv_post_bwd", y, *token_ops, post_vecs, ones_bd, drw)
    g["rwkv_lnx_w"], g["rwkv_lnx_b"] = post_g[5], post_g[6]
    g["rwkv_r_k"] = post_g[7].reshape(1, N_HEADS, HEAD)
    scan_g = _scan_bwd("rwkv_scan_bwd", scan_ops, states, post_g[0])
    pre_g = _rwkv_pre_bwd("rwkv_pre_bwd", p, p_prev, pre_vecs, pre_mats, ones_bd, scan_g, post_g[4], post_g[1:4])
    g["rwkv_mu"] = pre_g[2][:, :w["rwkv_mu"].shape[1]]
    g["rwkv_w0"], g["rwkv_a0"], g["rwkv_k_k"], g["rwkv_k_a"] = pre_g[3:7]
    g["rwkv_w_up"] = pre_g[7][:LORA_W]
    g["rwkv_a_up"] = pre_g[8][LORA_W:LORA_W + LORA_A]
    g["rwkv_g_up"] = pre_g[9][LORA_W + LORA_A:LORA_W + LORA_A + LORA_G]
    dp = pre_g[0] + _shift_up(pre_g[1])
    live = (jnp.arange(h0.shape[0]) >= META_PAD)[:, None]
    dproj = jnp.where(live, jnp.concatenate([dq, dk, dv, dp], axis=1), 0.0).astype(BF16)
    g["w_in"] = _mm_tn("in_proj_dw", n2, dproj)[:, :w["w_in"].shape[1]]
    dh1, g["mix_norm"] = _norm_proj_bwd("in_proj_bwd", dproj, w_in, h1, w["mix_norm"], dh2)
    dh0, da1, db1, s1, n1, dhh1, g["ffn1_norm"], *reduced_mid = _ffn_bwd(
        "ffn1_bwd", dh1, h0, w["ffn1_norm"], a1, b1, w["ffn1_w_gate"], w["ffn1_w_up"], w["ffn1_w_down"],
        late and ("reduce", late.parts("mid", g)))
    g["ffn1_w_gate"] = _mm_tn("ffn1_dgate", da1, n1)
    reduced_gate, reduced_up = [], []
    if late is None:
        g["ffn1_w_up"] = _mm_tn("ffn1_dup", db1, n1)
        g["ffn1_w_down"] = _mm_tn("ffn1_ddown", s1, dhh1)
    else:
        g["ffn1_w_up"], *reduced_gate = _mm_tn("ffn1_dup", db1, n1, ("reduce", late.parts("gate", g)))
        g["ffn1_w_down"], *reduced_up = _mm_tn("ffn1_ddown", s1, dhh1, ("reduce", late.parts("up", g)))
    g["meta_tokens"] = dh0[META_PAD:ROW0]
    return loss8[0, 0], dh0[ROW0:], g, {"mid": reduced_mid, "late": reduced_late, "gate": reduced_gate,
                                        "up": reduced_up}


N_CHIPS = 4
N_DEV = 8
HBM = pl.BlockSpec(memory_space=pltpu.HBM)


def _place():
    return lax.axis_index("x"), lax.axis_index("y"), lax.axis_index("c")


def _other_chips(x, y):
    return [(1 - x, y), (x, 1 - y), (1 - x, 1 - y)]


def _gather_sems(n):
    return [pltpu.SemaphoreType.DMA((3 * n,)), pltpu.SemaphoreType.DMA((3 * n,)), pltpu.SemaphoreType.DMA((n,)),
            pltpu.SemaphoreType.DMA((3 * n,)), pltpu.SemaphoreType.DMA((3 * n,))]


def _gather_exchange(ins, outs, sems):
    n = len(ins)
    half = [r.shape[0] // 2 for r in ins]
    send, recv, local, d2d_send, d2d_recv = sems
    x, y, c = _place()
    me = 2 * x + y
    chips = _other_chips(x, y)

    def rows_of(k, h):
        return pl.ds(pl.multiple_of(h * half[k], 8), half[k])

    def own(k):
        return pltpu.make_async_copy(ins[k], outs[k].at[me], local.at[k])

    def copy(j, k, slot):
        return pltpu.make_async_remote_copy(
            src_ref=ins[k].at[rows_of(k, c)], dst_ref=outs[k].at[slot, rows_of(k, c)],
            send_sem=send.at[j * n + k], recv_sem=recv.at[j * n + k],
            device_id=(chips[j][0], chips[j][1], c), device_id_type=MESH)

    def passed(j, k, h):
        slot = 2 * chips[j][0] + chips[j][1]
        return pltpu.make_async_remote_copy(
            src_ref=outs[k].at[slot, rows_of(k, h)], dst_ref=outs[k].at[slot, rows_of(k, h)],
            send_sem=d2d_send.at[j * n + k], recv_sem=d2d_recv.at[j * n + k],
            device_id=(x, y, 1 - c), device_id_type=MESH)

    def start():
        for k in range(n):
            own(k).start()
        for j in range(3):
            for k in range(n):
                copy(j, k, me).start()

    def finish():
        for j in range(3):
            for k in range(n):
                copy(j, k, 2 * chips[j][0] + chips[j][1]).wait_recv()
                passed(j, k, c).start()
        for j in range(3):
            for k in range(n):
                passed(j, k, 1 - c).wait_recv()
        for j in range(3):
            for k in range(n):
                copy(j, k, me).wait_send()
                passed(j, k, c).wait_send()
        for k in range(n):
            own(k).wait()

    return start, finish


def _gather_shards(name, shards):
    n = len(shards)

    def body(*refs):
        start, finish = _gather_exchange(refs[:n], refs[n:2 * n], refs[2 * n:])
        start()
        finish()

    return pl.pallas_call(
        body, name=name,
        in_specs=[HBM] * n, out_specs=[HBM] * n,
        out_shape=[jax.ShapeDtypeStruct((N_CHIPS,) + s.shape, s.dtype) for s in shards],
        scratch_shapes=_gather_sems(n),
    )(*shards)


def _pair_exchange(name, parts):
    n = len(parts)
    half = [s.shape[1] // 2 for s in parts]

    def body(*refs):
        ins, outs = refs[:n], refs[n:2 * n]
        send, recv = refs[2 * n:]
        x, y, c = _place()

        def copy(k):
            rows = pl.ds(pl.multiple_of((1 - c) * half[k], 8), half[k])
            return pltpu.make_async_remote_copy(
                src_ref=ins[k].at[:, rows], dst_ref=outs[k], send_sem=send.at[k], recv_sem=recv.at[k],
                device_id=(x, y, 1 - c), device_id_type=MESH)

        for k in range(n):
            copy(k).start()
        for k in range(n):
            copy(k).wait_recv()
        for k in range(n):
            copy(k).wait_send()

    return pl.pallas_call(
        body, name=name,
        in_specs=[HBM] * n, out_specs=[HBM] * n,
        out_shape=[jax.ShapeDtypeStruct((s.shape[0], s.shape[1] // 2, s.shape[2]), s.dtype) for s in parts],
        scratch_shapes=[pltpu.SemaphoreType.DMA((n,)), pltpu.SemaphoreType.DMA((n,))],
    )(*parts)


def _pair_add(name, part, other):
    nch, rows, cols = part.shape
    half = rows // 2

    def body(p_ref, o_ref, out_ref):
        c = lax.axis_index("c")
        mine = p_ref[0, pl.ds(pl.multiple_of(c * half, 16), half), :]
        out_ref[0] = (mine.astype(F32) + o_ref[0].astype(F32)).astype(out_ref.dtype)

    return pl.pallas_call(
        body, name=name, grid=(nch,),
        in_specs=[pl.BlockSpec((1, rows, cols), lambda j: (j, 0, 0)),
                  pl.BlockSpec((1, half, cols), lambda j: (j, 0, 0))],
        out_specs=pl.BlockSpec((1, half, cols), lambda j: (j, 0, 0)),
        out_shape=jax.ShapeDtypeStruct((nch, half, cols), part.dtype),
        compiler_params=_params("arbitrary"),
    )(part, other)


def _reduce_sems(n):
    return [pltpu.SemaphoreType.DMA((3 * n,)), pltpu.SemaphoreType.DMA((3 * n,)), pltpu.SemaphoreType.DMA((n,)),
            pltpu.SemaphoreType.DMA((n,)), pltpu.SemaphoreType.DMA((n,))]


def _reduce_exchange(ins, got, sib, sems):
    n = len(ins)
    send, recv, local, d2d_send, d2d_recv = sems
    x, y, c = _place()
    me = 2 * x + y
    chips = _other_chips(x, y)

    def own(k):
        return pltpu.make_async_copy(ins[k].at[me], got[k].at[me], local.at[k])

    def copy(j, k, shard, slot):
        return pltpu.make_async_remote_copy(
            src_ref=ins[k].at[shard], dst_ref=got[k].at[slot], send_sem=send.at[j * n + k],
            recv_sem=recv.at[j * n + k], device_id=(chips[j][0], chips[j][1], c), device_id_type=MESH)

    def swap(k):
        return pltpu.make_async_remote_copy(
            src_ref=got[k], dst_ref=sib[k], send_sem=d2d_send.at[k], recv_sem=d2d_recv.at[k],
            device_id=(x, y, 1 - c), device_id_type=MESH)

    def start():
        for k in range(n):
            own(k).start()
        for j in range(3):
            for k in range(n):
                copy(j, k, 2 * chips[j][0] + chips[j][1], me).start()

    def finish():
        for k in range(n):
            own(k).wait()
            for j in range(3):
                copy(j, k, me, 2 * chips[j][0] + chips[j][1]).wait_recv()
            swap(k).start()
        for k in range(n):
            swap(k).wait_recv()
        for j in range(3):
            for k in range(n):
                copy(j, k, me, me).wait_send()
        for k in range(n):
            swap(k).wait_send()

    return start, finish


def _reduce_shards(name, parts):
    n = len(parts)

    def body(*refs):
        start, finish = _reduce_exchange(refs[:n], refs[n:2 * n], refs[2 * n:3 * n], refs[3 * n:])
        start()
        finish()

    return pl.pallas_call(
        body, name=name,
        in_specs=[HBM] * n, out_specs=[HBM] * (2 * n),
        out_shape=[jax.ShapeDtypeStruct(s.shape, s.dtype) for s in parts] * 2,
        scratch_shapes=_reduce_sems(n),
    )(*parts)


def _all_reduce_small(name, vec):
    rows = vec.shape[0]

    def body(v_ref, o_ref, buf, send, recv):
        x, y, c = _place()
        me = 4 * x + 2 * y + c
        peers = [(x ^ (r >> 2), y ^ ((r >> 1) & 1), c ^ (r & 1)) for r in range(1, N_DEV)]

        def copy(r, slot):
            px, py, pc = peers[r]
            return pltpu.make_async_remote_copy(
                src_ref=v_ref, dst_ref=buf.at[slot], send_sem=send.at[r], recv_sem=recv.at[r],
                device_id=(px, py, pc), device_id_type=MESH)

        sent = [copy(r, me) for r in range(N_DEV - 1)]
        for cp in sent:
            cp.start()
        buf[me] = v_ref[...]
        for r in range(N_DEV - 1):
            px, py, pc = peers[r]
            copy(r, 4 * px + 2 * py + pc).wait_recv()
        total = buf[0]
        for dev in range(1, N_DEV):
            total = total + buf[dev]
        o_ref[...] = total
        for cp in sent:
            cp.wait_send()

    return pl.pallas_call(
        body, name=name,
        in_specs=[pl.BlockSpec(memory_space=pltpu.VMEM)], out_specs=pl.BlockSpec(memory_space=pltpu.VMEM),
        out_shape=jax.ShapeDtypeStruct(vec.shape, F32),
        scratch_shapes=[pltpu.VMEM((N_DEV, rows, 128), F32),
                        pltpu.SemaphoreType.DMA((N_DEV - 1,)), pltpu.SemaphoreType.DMA((N_DEV - 1,))],
        compiler_params=pltpu.CompilerParams(vmem_limit_bytes=VMEM_LIMIT),
    )(vec)


def _adamw(w, g, m, v):
    m = ADAM_B1 * m + (1.0 - ADAM_B1) * g
    v = ADAM_B2 * v + (1.0 - ADAM_B2) * (g * g)
    m_hat = m / (1.0 - ADAM_B1 ** ADAM_STEP)
    v_hat = v / (1.0 - ADAM_B2 ** ADAM_STEP)
    return -ADAM_LR * (m_hat / (jnp.sqrt(v_hat) + ADAM_EPS) + ADAM_WD * w), m, v


def _adamw_shard(name, core, w, m, v, got, sib):
    rows, cols = w.shape
    tr = rows // 4
    spec = pl.BlockSpec((tr, cols), lambda i, c_ref: (i, 0))
    spec4 = pl.BlockSpec((N_CHIPS, tr, cols), lambda i, c_ref: (0, i % 2, 0))

    def body(c_ref, w_ref, m_ref, v_ref, got_ref, sib_ref, g_ref, d_ref, mo_ref, vo_ref):
        def four(ref):
            return ((ref[0].astype(F32) + ref[1].astype(F32)) + ref[2].astype(F32)) + ref[3].astype(F32)

        g = jnp.where(pl.program_id(0) // 2 == c_ref[0], four(got_ref), four(sib_ref))
        g_ref[...] = g
        d_ref[...], mo_ref[...], vo_ref[...] = _adamw(w_ref[...], g, m_ref[...], v_ref[...])

    return pl.pallas_call(
        body, name=name,
        grid_spec=pltpu.PrefetchScalarGridSpec(
            num_scalar_prefetch=1, grid=(4,),
            in_specs=[spec, spec, spec, spec4, spec4], out_specs=[spec] * 4),
        out_shape=[jax.ShapeDtypeStruct((rows, cols), F32)] * 4,
        compiler_params=_params("arbitrary"),
    )(core, w, m, v, got, sib)


def _adamw_small(name, w, m, v, g):
    def body(w_ref, m_ref, v_ref, g_ref, d_ref, mo_ref, vo_ref):
        d_ref[...], mo_ref[...], vo_ref[...] = _adamw(w_ref[...], g_ref[...], m_ref[...], v_ref[...])

    return pl.pallas_call(body, name=name, out_shape=[jax.ShapeDtypeStruct(w.shape, F32)] * 3)(w, m, v, g)


def _cast_bf16(name, arrays):
    n = len(arrays)

    def body(*refs):
        for i_ref, o_ref in zip(refs[:n], refs[n:]):
            o_ref[...] = i_ref[...].astype(BF16)

    return pl.pallas_call(
        body, name=name, out_shape=[jax.ShapeDtypeStruct(a.shape, BF16) for a in arrays],
        compiler_params=pltpu.CompilerParams(vmem_limit_bytes=VMEM_LIMIT),
    )(*arrays)


def _pack(arrays, rows):
    flat = jnp.concatenate([a.reshape(-1) for a in arrays])
    return jnp.concatenate([flat, jnp.zeros((rows * 128 - flat.shape[0],), F32)]).reshape(rows, 128)


def _unpack(packed, shapes):
    flat, out, at = packed.reshape(-1), [], 0
    for s in shapes:
        size = 1
        for dim in s:
            size *= dim
        out.append(flat[at:at + size].reshape(s))
        at += size
    return out


def _rows_for(shapes):
    total = 0
    for s in shapes:
        size = 1
        for dim in s:
            size *= dim
        total += size
    return -(-total // 1024) * 8


WEIGHTS = ['meta_tokens', 'ffn1_norm', 'ffn1_w_gate', 'ffn1_w_up', 'ffn1_w_down', 'mix_norm', 'w_in', 'rwkv_mu',
           'rwkv_w0', 'rwkv_w_up', 'rwkv_a0', 'rwkv_a_up', 'rwkv_g_up', 'rwkv_k_k', 'rwkv_k_a', 'rwkv_r_k',
           'rwkv_lnx_w', 'rwkv_lnx_b', 'w_out', 'ffn2_norm', 'ffn2_w_gate', 'ffn2_w_up', 'ffn2_w_down', 'final_norm']
COL_CUT = ['ffn1_w_gate', 'ffn1_w_up', 'w_in', 'ffn2_w_gate', 'ffn2_w_up']
ROW_CUT = ['ffn1_w_down', 'w_out', 'ffn2_w_down']
SMALL_CUT = ['meta_tokens', 'rwkv_w_up', 'rwkv_a_up', 'rwkv_g_up']
TRANSPOSED = ['ffn1_w_gate', 'ffn1_w_up', 'ffn2_w_gate', 'ffn2_w_up']
BIG = COL_CUT + ROW_CUT
REPLICATED = [n for n in WEIGHTS if n not in BIG + SMALL_CUT]


def _join_cols(a):
    return a.transpose(1, 0, 2).reshape(a.shape[1], N_CHIPS * a.shape[2])


def _cut_cols(a):
    return a.reshape(a.shape[0], N_CHIPS, a.shape[1] // N_CHIPS).transpose(1, 0, 2)


def _step(x, loss_target, w, m, v):
    two = lambda a: a.reshape(a.shape[-2], a.shape[-1])

    def rows_cut(n, a):
        return jnp.swapaxes(two(a), 0, 1) if n in TRANSPOSED else two(a)

    def as_given(n, a, like):
        return (jnp.swapaxes(a, 0, 1) if n in TRANSPOSED else a).reshape(like.shape)

    col_cut = [n for n in COL_CUT + SMALL_CUT if n not in TRANSPOSED]

    def join(names, gathered):
        return {n: (_join_cols(a) if n in col_cut else a.reshape(-1, a.shape[-1])) for n, a in zip(names, gathered)}

    def pair_sums(tag, names, g):
        parts = [_cut_cols(g[n]) if n in col_cut else g[n].reshape(N_CHIPS, -1, g[n].shape[-1]) for n in names]
        arrived = _pair_exchange("pair_exchange_" + tag, parts)
        return [_pair_add("pair_add_" + n, p, o) for n, p, o in zip(names, parts, arrived)]

    first = [n for n in BIG if n not in MID + LATE]
    gathered_later = {"mid": MID, "late": LATE}
    groups = {**gathered_later, "gate": ["ffn1_w_gate"], "up": ["ffn1_w_up"]}
    cast = dict(zip(BIG, _cast_bf16("cast_weights", [rows_cut(n, w[n]) for n in BIG])))
    names = first + SMALL_CUT
    shards = [cast[n] for n in first] + [two(w[n]) for n in SMALL_CUT]
    full = {n: (two(w[n]) if w[n].ndim == 3 else w[n]) for n in REPLICATED}
    full.update(join(names, _gather_shards("gather_weights", shards)))
    full["rwkv_r_k"] = w["rwkv_r_k"]
    full["final_norm"] = w["final_norm"]

    late = types.SimpleNamespace(shards={k: [cast[n] for n in names] for k, names in gathered_later.items()},
                                 join=lambda k, gathered: join(groups[k], gathered),
                                 parts=lambda k, g: pair_sums(k, groups[k], g))

    loss, dx, g, reduced = _local_step(x[0], loss_target[0], full, late)
    loss = lax.psum(loss, ("x", "y", "c"))

    groups["down"] = ["ffn1_w_down"]
    reduced["down"] = list(_reduce_shards("reduce_gradients", pair_sums("down", groups["down"], g)))
    got, sib = {}, {}
    for k, names in groups.items():
        got.update(zip(names, reduced[k][:len(names)]))
        sib.update(zip(names, reduced[k][len(names):]))

    small_names = REPLICATED + SMALL_CUT
    small_shapes = [g[n].shape for n in small_names]
    small = _all_reduce_small("reduce_small", _pack([g[n] for n in small_names], _rows_for(small_shapes)))
    g_small = dict(zip(small_names, _unpack(small, small_shapes)))
    chip = 2 * lax.axis_index("x") + lax.axis_index("y")
    for n in SMALL_CUT:
        width = g_small[n].shape[1] // N_CHIPS
        g_small[n] = lax.dynamic_slice_in_dim(g_small[n], chip * width, width, axis=1)

    grad, delta, new_m, new_v = {}, {}, {}, {}
    core = lax.axis_index("c").astype(jnp.int32).reshape(1)
    for n in BIG:
        outs = _adamw_shard("adamw_" + n, core, rows_cut(n, w[n]), rows_cut(n, m[n]), rows_cut(n, v[n]), got[n], sib[n])
        grad[n], delta[n], new_m[n], new_v[n] = (as_given(n, o, w[n]) for o in outs)
    shapes = [w[n].shape for n in small_names]
    rows = _rows_for(shapes)
    packed = [_pack([t[n] for n in small_names], rows) for t in (w, m, v)]
    g_packed = _pack([g_small[n] for n in small_names], rows)
    outs = [_unpack(o, shapes) for o in _adamw_small("adamw_small", *packed, g_packed)]
    for i, n in enumerate(small_names):
        grad[n] = g_small[n].reshape(w[n].shape)
        delta[n], new_m[n], new_v[n] = outs[0][i], outs[1][i], outs[2][i]
    return loss, dx[None], grad, delta, new_m, new_v


def kernel(x, meta_tokens, ffn1_norm, ffn1_w_gate, ffn1_w_up, ffn1_w_down, mix_norm, w_in, rwkv_mu, rwkv_w0, rwkv_w_up, rwkv_a0, rwkv_a_up, rwkv_g_up, rwkv_k_k, rwkv_k_a, rwkv_r_k, rwkv_lnx_w, rwkv_lnx_b, w_out, ffn2_norm, ffn2_w_gate, ffn2_w_up, ffn2_w_down, final_norm, loss_target, m_meta_tokens, m_ffn1_norm, m_ffn1_w_gate, m_ffn1_w_up, m_ffn1_w_down, m_mix_norm, m_w_in, m_rwkv_mu, m_rwkv_w0, m_rwkv_w_up, m_rwkv_a0, m_rwkv_a_up, m_rwkv_g_up, m_rwkv_k_k, m_rwkv_k_a, m_rwkv_r_k, m_rwkv_lnx_w, m_rwkv_lnx_b, m_w_out, m_ffn2_norm, m_ffn2_w_gate, m_ffn2_w_up, m_ffn2_w_down, m_final_norm, v_meta_tokens, v_ffn1_norm, v_ffn1_w_gate, v_ffn1_w_up, v_ffn1_w_down, v_mix_norm, v_w_in, v_rwkv_mu, v_rwkv_w0, v_rwkv_w_up, v_rwkv_a0, v_rwkv_a_up, v_rwkv_g_up, v_rwkv_k_k, v_rwkv_k_a, v_rwkv_r_k, v_rwkv_lnx_w, v_rwkv_lnx_b, v_w_out, v_ffn2_norm, v_ffn2_w_gate, v_ffn2_w_up, v_ffn2_w_down, v_final_norm):
    w = dict(zip(WEIGHTS, (meta_tokens, ffn1_norm, ffn1_w_gate, ffn1_w_up, ffn1_w_down, mix_norm, w_in, rwkv_mu, rwkv_w0, rwkv_w_up, rwkv_a0, rwkv_a_up, rwkv_g_up, rwkv_k_k, rwkv_k_a, rwkv_r_k, rwkv_lnx_w, rwkv_lnx_b, w_out, ffn2_norm, ffn2_w_gate, ffn2_w_up, ffn2_w_down, final_norm)))
    m = dict(zip(WEIGHTS, (m_meta_tokens, m_ffn1_norm, m_ffn1_w_gate, m_ffn1_w_up, m_ffn1_w_down, m_mix_norm, m_w_in, m_rwkv_mu, m_rwkv_w0, m_rwkv_w_up, m_rwkv_a0, m_rwkv_a_up, m_rwkv_g_up, m_rwkv_k_k, m_rwkv_k_a, m_rwkv_r_k, m_rwkv_lnx_w, m_rwkv_lnx_b, m_w_out, m_ffn2_norm, m_ffn2_w_gate, m_ffn2_w_up, m_ffn2_w_down, m_final_norm)))
    v = dict(zip(WEIGHTS, (v_meta_tokens, v_ffn1_norm, v_ffn1_w_gate, v_ffn1_w_up, v_ffn1_w_down, v_mix_norm, v_w_in, v_rwkv_mu, v_rwkv_w0, v_rwkv_w_up, v_rwkv_a0, v_rwkv_a_up, v_rwkv_g_up, v_rwkv_k_k, v_rwkv_k_a, v_rwkv_r_k, v_rwkv_lnx_w, v_rwkv_lnx_b, v_w_out, v_ffn2_norm, v_ffn2_w_gate, v_ffn2_w_up, v_ffn2_w_down, v_final_norm)))
    loss, grad_x, grad, delta, new_m, new_v = _step(x, loss_target, w, m, v)
    return (loss, grad_x, *[grad[n] for n in WEIGHTS], *[delta[n] for n in WEIGHTS],
            *[new_m[n] for n in WEIGHTS], *[new_v[n] for n in WEIGHTS])
```

```python
import functools
import types

import jax
import jax.numpy as jnp
from jax import lax
from jax.experimental import pallas as pl
from jax.experimental.pallas import tpu as pltpu

F32 = jnp.float32
BF16 = jnp.bfloat16

RMS_EPS = 1e-6
LNX_EPS = 64e-5
N_META = 16
ROW0 = 128
META_PAD = ROW0 - N_META
HEAD = 64
N_HEADS = 8
GROUP = N_HEADS * HEAD
LORA_W, LORA_A, LORA_G = 32, 32, 96
LORA_PAD = 256
RW_COLS = 3 * GROUP + LORA_PAD
IN_COLS_PAD = 3 * GROUP + RW_COLS
ATT_BLOCK = 128
CHUNK = 64
SCAN_CHUNKS = 2
VMEM_LIMIT = 56 * 1024 * 1024

ADAM_LR, ADAM_B1, ADAM_B2, ADAM_EPS, ADAM_WD, ADAM_STEP = 0.001, 0.9, 0.999, 1e-08, 0.01, 10

MESH = pl.DeviceIdType.MESH


def _params(*sem):
    return pltpu.CompilerParams(dimension_semantics=tuple(sem), vmem_limit_bytes=VMEM_LIMIT)


def _dot(a, b):
    return lax.dot_general(a, b, (((1,), (0,)), ((), ())), preferred_element_type=F32)


def _dot_nt(a, b):
    return lax.dot_general(a, b, (((1,), (1,)), ((), ())), preferred_element_type=F32)


def _dot_tn(a, b):
    return lax.dot_general(a, b, (((0,), (0,)), ((), ())), preferred_element_type=F32)


def _split2(x):
    hi = x.astype(BF16)
    return hi, (x - hi.astype(F32)).astype(BF16)


def _sigmoid(x):
    return 1.0 / (1.0 + jnp.exp(-x))


def _rms_fwd(x, g):
    rstd = lax.rsqrt(jnp.mean(x * x, axis=-1, keepdims=True) + RMS_EPS)
    xhat = x * rstd
    return xhat * g, xhat, rstd


def _rms_bwd(dn, xhat, rstd, g):
    dxhat = dn * g
    dx = rstd * (dxhat - xhat * jnp.mean(dxhat * xhat, axis=-1, keepdims=True))
    return dx, jnp.sum(dn * xhat, axis=0, keepdims=True)


def _row_tile(rows):
    return 384 if rows % 384 == 0 else 128


def _half_tile(cols):
    return cols // 2 if cols % 256 == 0 else cols


def _tall_tile(rows, parts):
    return rows // parts if rows % (16 * parts) == 0 else _row_tile(rows)


def _call_with_exchange(name, body, grid, in_specs, out_specs, out_shape, scratch, operands, params, exchange):
    if exchange is None or not exchange[1]:
        return pl.pallas_call(body, name=name, grid=grid, in_specs=in_specs, out_specs=out_specs,
                              out_shape=out_shape, scratch_shapes=scratch, compiler_params=params)(*operands)
    kind, arrays = exchange
    ns, n_in, n_out, n_scr = len(arrays), len(in_specs), len(out_specs), len(scratch)
    if kind == "gather":
        results = [jax.ShapeDtypeStruct((N_CHIPS,) + s.shape, s.dtype) for s in arrays]
        sems = _gather_sems(ns)
    else:
        results = [jax.ShapeDtypeStruct(s.shape, s.dtype) for s in arrays] * 2
        sems = _reduce_sems(ns)
    n_res = len(results)

    def carried(*refs):
        at = n_in + ns + n_out
        sent, landed = refs[n_in:n_in + ns], refs[at:at + n_res]
        own_scratch, sem_refs = refs[at + n_res:at + n_res + n_scr], refs[at + n_res + n_scr:]
        first, last = _first_and_last_step(grid)
        if kind == "gather":
            start, finish = _gather_exchange(sent, landed, sem_refs)
        else:
            start, finish = _reduce_exchange(sent, landed[:ns], landed[ns:], sem_refs)
        pl.when(first)(start)
        body(*refs[:n_in], *refs[n_in + ns:at], *own_scratch)
        pl.when(last)(finish)

    return pl.pallas_call(
        carried, name=name, grid=grid, in_specs=list(in_specs) + [HBM] * ns, out_specs=list(out_specs) + [HBM] * n_res,
        out_shape=list(out_shape) + results, scratch_shapes=list(scratch) + sems, compiler_params=params,
    )(*operands, *arrays)


def _ffn_fwd(name, h, g, wg, wu, wd, exchange=None):
    rows, d = h.shape
    f = wg.shape[0]
    tm, tf = _row_tile(rows), _half_tile(f)
    nj = f // tf

    def body(h_ref, g_ref, wg_ref, wu_ref, wd_ref, ho_ref, a_ref, b_ref, n_sc, acc_sc):
        j = pl.program_id(1)

        @pl.when(j == 0)
        def _():
            n, _, _ = _rms_fwd(h_ref[...], g_ref[...])
            n_sc[...] = n.astype(BF16)
            acc_sc[...] = jnp.zeros_like(acc_sc)

        n = n_sc[...]
        a = _dot_nt(n, wg_ref[...])
        b = _dot_nt(n, wu_ref[...])
        a_ref[...] = a
        b_ref[...] = b
        s = a * _sigmoid(a) * b
        acc_sc[...] += _dot(s.astype(BF16), wd_ref[...])

        @pl.when(j == nj - 1)
        def _():
            ho_ref[...] = h_ref[...] + 0.5 * acc_sc[...]

    return _call_with_exchange(
        name, body, (rows // tm, nj),
        [pl.BlockSpec((tm, d), lambda i, j: (i, 0)),
         pl.BlockSpec((1, d), lambda i, j: (0, 0)),
         pl.BlockSpec((tf, d), lambda i, j: (j, 0)),
         pl.BlockSpec((tf, d), lambda i, j: (j, 0)),
         pl.BlockSpec((tf, d), lambda i, j: (j, 0))],
        [pl.BlockSpec((tm, d), lambda i, j: (i, 0)),
         pl.BlockSpec((tm, tf), lambda i, j: (i, j)),
         pl.BlockSpec((tm, tf), lambda i, j: (i, j))],
        [jax.ShapeDtypeStruct((rows, d), F32),
         jax.ShapeDtypeStruct((rows, f), F32),
         jax.ShapeDtypeStruct((rows, f), F32)],
        [pltpu.VMEM((tm, d), BF16), pltpu.VMEM((tm, d), F32)],
        (h, g, wg, wu, wd), _params("arbitrary", "arbitrary"), exchange)


def _ffn_bwd(name, dh, h, g, a, b, wg, wu, wd, exchange=None):
    rows, d = h.shape
    f = wg.shape[0]
    tm, tf = _row_tile(rows), _half_tile(f)
    ni, nj = rows // tm, f // tf

    def body(dh_ref, h_ref, g_ref, a_ref, b_ref, wg_ref, wu_ref, wd_ref,
             dhi_ref, da_ref, db_ref, s_ref, n_ref, dhh_ref, dg_ref, dn_sc):
        i, j = pl.program_id(0), pl.program_id(1)

        @pl.when(j == 0)
        def _():
            n, _, _ = _rms_fwd(h_ref[...], g_ref[...])
            n_ref[...] = n.astype(BF16)
            dhh_ref[...] = (0.5 * dh_ref[...]).astype(BF16)
            dn_sc[...] = jnp.zeros_like(dn_sc)

        @pl.when((i == 0) & (j == 0))
        def _():
            dg_ref[...] = jnp.zeros_like(dg_ref)

        ds = _dot_nt(dhh_ref[...], wd_ref[...])
        av, bv = a_ref[...], b_ref[...]
        sig = _sigmoid(av)
        silu = av * sig
        s_ref[...] = (silu * bv).astype(BF16)
        db = (ds * silu).astype(BF16)
        da = (ds * bv * (sig * (1.0 + av * (1.0 - sig)))).astype(BF16)
        da_ref[...] = da
        db_ref[...] = db
        dn_sc[...] += _dot(da, wg_ref[...]) + _dot(db, wu_ref[...])

        @pl.when(j == nj - 1)
        def _():
            gv = g_ref[...]
            _, xhat, rstd = _rms_fwd(h_ref[...], gv)
            dx, dg = _rms_bwd(dn_sc[...], xhat, rstd, gv)
            dhi_ref[...] = dh_ref[...] + dx
            dg_ref[...] += dg

    return _call_with_exchange(
        name, body, (ni, nj),
        [pl.BlockSpec((tm, d), lambda i, j: (i, 0)),
         pl.BlockSpec((tm, d), lambda i, j: (i, 0)),
         pl.BlockSpec((1, d), lambda i, j: (0, 0)),
         pl.BlockSpec((tm, tf), lambda i, j: (i, j)),
         pl.BlockSpec((tm, tf), lambda i, j: (i, j)),
         pl.BlockSpec((tf, d), lambda i, j: (j, 0)),
         pl.BlockSpec((tf, d), lambda i, j: (j, 0)),
         pl.BlockSpec((tf, d), lambda i, j: (j, 0))],
        [pl.BlockSpec((tm, d), lambda i, j: (i, 0)),
         pl.BlockSpec((tm, tf), lambda i, j: (i, j)),
         pl.BlockSpec((tm, tf), lambda i, j: (i, j)),
         pl.BlockSpec((tm, tf), lambda i, j: (i, j)),
         pl.BlockSpec((tm, d), lambda i, j: (i, 0)),
         pl.BlockSpec((tm, d), lambda i, j: (i, 0)),
         pl.BlockSpec((1, d), lambda i, j: (0, 0))],
        [jax.ShapeDtypeStruct((rows, d), F32),
         jax.ShapeDtypeStruct((rows, f), BF16),
         jax.ShapeDtypeStruct((rows, f), BF16),
         jax.ShapeDtypeStruct((rows, f), BF16),
         jax.ShapeDtypeStruct((rows, d), BF16),
         jax.ShapeDtypeStruct((rows, d), BF16),
         jax.ShapeDtypeStruct((1, d), F32)],
        [pltpu.VMEM((tm, d), F32)],
        (dh, h, g, a, b, wg, wu, wd), _params("arbitrary", "arbitrary"), exchange)


def _mm_tn(name, a, b, exchange=None):
    k, m = a.shape
    n = b.shape[1]
    tk = _tall_tile(k, 3)
    tm = _half_tile(m) if m > 1024 else m
    tn = _half_tile(n) if n > 1024 else n
    nk = k // tk

    def body(a_ref, b_ref, o_ref, acc):
        kk = pl.program_id(2)

        @pl.when(kk == 0)
        def _():
            acc[...] = jnp.zeros_like(acc)

        acc[...] += _dot_tn(a_ref[...], b_ref[...])

        @pl.when(kk == nk - 1)
        def _():
            o_ref[...] = acc[...].astype(BF16)

    outs = _call_with_exchange(
        name, body, (m // tm, n // tn, nk),
        [pl.BlockSpec((tk, tm), lambda i, j, kk: (kk, i)),
         pl.BlockSpec((tk, tn), lambda i, j, kk: (kk, j))],
        [pl.BlockSpec((tm, tn), lambda i, j, kk: (i, j))],
        [jax.ShapeDtypeStruct((m, n), BF16)],
        [pltpu.VMEM((tm, tn), F32)],
        (a, b), _params("arbitrary", "arbitrary", "arbitrary"), exchange)
    return outs if exchange else outs[0]


def _norm_proj(name, h, g, w):
    rows, d = h.shape
    n = w.shape[1]
    split = 3 * GROUP
    tm = _row_tile(rows)

    def body(h_ref, g_ref, w_ref, qkv_ref, p_ref, n_ref):
        nv, _, _ = _rms_fwd(h_ref[...], g_ref[...])
        nb = nv.astype(BF16)
        n_ref[...] = nb
        qkv_ref[...] = _dot(nb, w_ref[:, :split]).astype(BF16)
        p_ref[...] = _dot(nb, w_ref[:, split:])

    return pl.pallas_call(
        body, name=name, grid=(rows // tm,),
        in_specs=[pl.BlockSpec((tm, d), lambda i: (i, 0)),
                  pl.BlockSpec((1, d), lambda i: (0, 0)),
                  pl.BlockSpec((d, n), lambda i: (0, 0))],
        out_specs=[pl.BlockSpec((tm, split), lambda i: (i, 0)),
                   pl.BlockSpec((tm, n - split), lambda i: (i, 0)),
                   pl.BlockSpec((tm, d), lambda i: (i, 0))],
        out_shape=[jax.ShapeDtypeStruct((rows, split), BF16), jax.ShapeDtypeStruct((rows, n - split), F32),
                   jax.ShapeDtypeStruct((rows, d), BF16)],
        compiler_params=_params("arbitrary"),
    )(h, g, w)


def _out_proj(name, h, sb, rw, w):
    rows, d = h.shape
    gw = sb.shape[1]
    tm = _row_tile(rows)

    def body(h_ref, sb_ref, rw_ref, w_ref, o_ref, mix_ref):
        mix_ref[:, :gw] = sb_ref[...].astype(BF16)
        mix_ref[:, gw:] = rw_ref[...].astype(BF16)
        o_ref[...] = h_ref[...] + _dot(mix_ref[...], w_ref[...])

    return pl.pallas_call(
        body, name=name, grid=(rows // tm,),
        in_specs=[pl.BlockSpec((tm, d), lambda i: (i, 0)),
                  pl.BlockSpec((tm, gw), lambda i: (i, 0)),
                  pl.BlockSpec((tm, gw), lambda i: (i, 0)),
                  pl.BlockSpec((2 * gw, d), lambda i: (0, 0))],
        out_specs=[pl.BlockSpec((tm, d), lambda i: (i, 0)),
                   pl.BlockSpec((tm, 2 * gw), lambda i: (i, 0))],
        out_shape=[jax.ShapeDtypeStruct((rows, d), F32), jax.ShapeDtypeStruct((rows, 2 * gw), BF16)],
        compiler_params=_params("arbitrary"),
    )(h, sb, rw, w)


def _out_proj_bwd(name, dh, w):
    rows, d = dh.shape
    k = w.shape[0]
    tm = _row_tile(rows)

    def body(dh_ref, w_ref, dsb_ref, drw_ref, dhb_ref):
        dhb = dh_ref[...].astype(BF16)
        dhb_ref[...] = dhb
        dsb_ref[...] = _dot_nt(dhb, w_ref[:GROUP, :]).astype(BF16)
        drw_ref[...] = _dot_nt(dhb, w_ref[GROUP:, :])

    return pl.pallas_call(
        body, name=name, grid=(rows // tm,),
        in_specs=[pl.BlockSpec((tm, d), lambda i: (i, 0)),
                  pl.BlockSpec((k, d), lambda i: (0, 0))],
        out_specs=[pl.BlockSpec((tm, GROUP), lambda i: (i, 0)),
                   pl.BlockSpec((tm, GROUP), lambda i: (i, 0)),
                   pl.BlockSpec((tm, d), lambda i: (i, 0))],
        out_shape=[jax.ShapeDtypeStruct((rows, GROUP), BF16), jax.ShapeDtypeStruct((rows, GROUP), F32),
                   jax.ShapeDtypeStruct((rows, d), BF16)],
        compiler_params=_params("arbitrary"),
    )(dh, w)


def _norm_proj_bwd(name, dproj, w, h, g, dh):
    rows, n = dproj.shape
    d = w.shape[0]
    tm = _row_tile(rows)

    def body(dp_ref, w_ref, h_ref, g_ref, dh_ref, o_ref, dg_ref):
        @pl.when(pl.program_id(0) == 0)
        def _():
            dg_ref[...] = jnp.zeros_like(dg_ref)

        dn = _dot_nt(dp_ref[...], w_ref[...])
        gv = g_ref[...]
        _, xhat, rstd = _rms_fwd(h_ref[...], gv)
        dx, dg = _rms_bwd(dn, xhat, rstd, gv)
        o_ref[...] = dh_ref[...] + dx
        dg_ref[...] += dg

    return pl.pallas_call(
        body, name=name, grid=(rows // tm,),
        in_specs=[pl.BlockSpec((tm, n), lambda i: (i, 0)),
                  pl.BlockSpec((d, n), lambda i: (0, 0)),
                  pl.BlockSpec((tm, d), lambda i: (i, 0)),
                  pl.BlockSpec((1, d), lambda i: (0, 0)),
                  pl.BlockSpec((tm, d), lambda i: (i, 0))],
        out_specs=[pl.BlockSpec((tm, d), lambda i: (i, 0)),
                   pl.BlockSpec((1, d), lambda i: (0, 0))],
        out_shape=[jax.ShapeDtypeStruct((rows, d), F32), jax.ShapeDtypeStruct((1, d), F32)],
        compiler_params=_params("arbitrary"),
    )(dproj, w, h, g, dh)


def _loss_head(name, h, g, tgt):
    rows, d = h.shape
    tm = _row_tile(rows)

    def body(h_ref, g_ref, t_ref, loss_ref, dh_ref, dg_ref):
        i = pl.program_id(0)

        @pl.when(i == 0)
        def _():
            loss_ref[...] = jnp.zeros_like(loss_ref)
            dg_ref[...] = jnp.zeros_like(dg_ref)

        gv = g_ref[...]
        y, xhat, rstd = _rms_fwd(h_ref[...], gv)
        row = i * tm + lax.broadcasted_iota(jnp.int32, (tm, 1), 0)
        diff = jnp.where(row >= ROW0, y - t_ref[...], 0.0)
        part = 0.5 * jnp.sum(jnp.sum(diff * diff, axis=-1, keepdims=True), axis=0, keepdims=True) / d
        loss_ref[...] += jnp.broadcast_to(part, loss_ref.shape)
        dx, dg = _rms_bwd(diff / d, xhat, rstd, gv)
        dh_ref[...] = dx
        dg_ref[...] += dg

    return pl.pallas_call(
        body, name=name, grid=(rows // tm,),
        in_specs=[pl.BlockSpec((tm, d), lambda i: (i, 0)),
                  pl.BlockSpec((1, d), lambda i: (0, 0)),
                  pl.BlockSpec((tm, d), lambda i: (i, 0))],
        out_specs=[pl.BlockSpec((8, 128), lambda i: (0, 0)),
                   pl.BlockSpec((tm, d), lambda i: (i, 0)),
                   pl.BlockSpec((1, d), lambda i: (0, 0))],
        out_shape=[jax.ShapeDtypeStruct((8, 128), F32),
                   jax.ShapeDtypeStruct((rows, d), F32),
                   jax.ShapeDtypeStruct((1, d), F32)],
        compiler_params=_params("arbitrary"),
    )(h, g, tgt)


def _sb_block(qb, kb, q0, jb, scale):
    bq, bk = qb.shape[0], kb.shape[0]
    z = _dot_nt(qb, kb) * scale
    qpos = q0 + lax.broadcasted_iota(jnp.int32, (bq, bk), 0)
    kpos = jb * bk + lax.broadcasted_iota(jnp.int32, (bq, bk), 1)
    valid = (kpos < qpos) & (kpos >= META_PAD)
    e = jnp.exp(-jnp.abs(z))
    log_keep = jnp.where(valid, -(jnp.maximum(z, 0.0) + jnp.log(1.0 + e)), 0.0)
    return z, valid, e, log_keep


def _tri2(n, cmp):
    r = lax.broadcasted_iota(jnp.int32, (2 * n, n), 0) % n
    c = lax.broadcasted_iota(jnp.int32, (2 * n, n), 1)
    return cmp(r, c).astype(BF16)


def _dot_split(x, t2):
    hi, lo = _split2(x)
    return _dot(jnp.concatenate([hi, lo], axis=1), t2)


ATT_HEADS = 128 // HEAD
ATT_CUT = -104.0
ATT_TILES = GROUP // 128


def _lanes(hh):
    return slice(hh * HEAD, (hh + 1) * HEAD)


def _first_and_last_step(grid):
    here = [pl.program_id(a) for a in range(len(grid))]
    first, last = here[0] == 0, here[0] == grid[0] - 1
    for a in range(1, len(grid)):
        first, last = first & (here[a] == 0), last & (here[a] == grid[a] - 1)
    return first, last


def _sb_fwd(name, qkv, shards=()):
    rows = qkv.shape[0]
    nh, dh = N_HEADS, HEAD
    bq, bk, hg = _row_tile(rows), ATT_BLOCK, ATT_HEADS
    per = bq // bk
    scale = dh ** -0.5
    ns = len(shards)
    grid = (nh // hg, rows // bq)

    def body(q_ref, k_ref, v_ref, *rest):
        o_ref, rt_ref, cnt_ref = rest[ns:ns + 3]
        if ns:
            first, last = _first_and_last_step(grid)
            start, finish = _gather_exchange(rest[:ns], rest[ns + 3:2 * ns + 3], rest[2 * ns + 3:])
            pl.when(first)(start)
        i = pl.program_id(1)
        after = _tri2(bk, lambda r, c: r > c)
        nkb = (i + 1) * per

        def live(state):
            n, carry = state
            top = jnp.max(carry[0][0])
            for hh in range(1, hg):
                top = jnp.maximum(top, jnp.max(carry[hh][0]))
            return (n < nkb) & (top >= ATT_CUT)

        def visit(carry, jb, r0):
            off = pl.multiple_of(jb * bk, bk)
            out = []
            for hh in range(hg):
                rest, acc = carry[hh]
                kb = k_ref[pl.ds(off, bk), _lanes(hh)]
                vb = v_ref[pl.ds(off, bk), _lanes(hh)]
                z, valid, _, log_keep = _sb_block(q_ref[r0:, _lanes(hh)], kb, i * bq + r0, jb, scale)
                log_rest = rest[r0:] + _dot_split(log_keep, after)
                attn = jnp.where(valid, jnp.exp(z + log_keep + log_rest), 0.0)
                new_rest = rest[r0:] + jnp.sum(log_keep, axis=-1, keepdims=True)
                new_acc = acc[r0:] + _dot(attn.astype(BF16), vb)
                if r0:
                    new_rest = jnp.concatenate([rest[:r0], new_rest], axis=0)
                    new_acc = jnp.concatenate([acc[:r0], new_acc], axis=0)
                out.append((new_rest, new_acc))
            return tuple(out)

        carry = tuple((jnp.zeros((bq, 1), F32), jnp.zeros((bq, dh), F32)) for _ in range(hg))
        for dgl in reversed(range(per)):
            carry = visit(carry, i * per + dgl, dgl * bk)
        n, res = lax.while_loop(live, lambda s: (s[0] + 1, visit(s[1], nkb - 1 - s[0], 0)), (jnp.int32(per), carry))
        for hh in range(hg):
            rt_ref[hh] = res[hh][0]
            o_ref[:, _lanes(hh)] = res[hh][1]
            cnt_ref[hh] = jnp.full((bq, 1), n, F32)
        if ns:
            pl.when(last)(finish)

    return pl.pallas_call(
        body, name=name, grid=grid,
        in_specs=[pl.BlockSpec((bq, 128), lambda h, i: (i, h)),
                  pl.BlockSpec((rows, 128), lambda h, i: (0, ATT_TILES + h)),
                  pl.BlockSpec((rows, 128), lambda h, i: (0, 2 * ATT_TILES + h))] + [HBM] * ns,
        out_specs=[pl.BlockSpec((bq, 128), lambda h, i: (i, h)),
                   pl.BlockSpec((hg, bq, 1), lambda h, i: (h, i, 0)),
                   pl.BlockSpec((hg, bq, 1), lambda h, i: (h, i, 0))] + [HBM] * ns,
        out_shape=[jax.ShapeDtypeStruct((rows, GROUP), F32), jax.ShapeDtypeStruct((nh, rows, 1), F32),
                   jax.ShapeDtypeStruct((nh, rows, 1), F32)]
        + [jax.ShapeDtypeStruct((N_CHIPS,) + s.shape, s.dtype) for s in shards],
        scratch_shapes=_gather_sems(ns) if ns else [],
        compiler_params=_params("arbitrary", "arbitrary"),
    )(qkv, qkv, qkv, *shards)


def _sb_bwd(name, qkv, rt, cnt, do, parts=()):
    rows = qkv.shape[0]
    nh, dh = N_HEADS, HEAD
    bq, bk, hg = _row_tile(rows), ATT_BLOCK, ATT_HEADS
    per = bq // bk
    scale = dh ** -0.5
    ns = len(parts)
    grid = (nh // hg, rows // bq)

    def body(q_ref, k_ref, v_ref, rt_ref, cnt_ref, do_ref, *rest):
        dq_ref, dk_ref, dv_ref = rest[ns:ns + 3]
        if ns:
            at_first, at_last = _first_and_last_step(grid)
            start, finish = _reduce_exchange(rest[:ns], rest[ns + 3:2 * ns + 3], rest[2 * ns + 3:3 * ns + 3],
                                             rest[3 * ns + 3:])
            pl.when(at_first)(start)
        i = pl.program_id(1)

        @pl.when(i == 0)
        def _():
            dk_ref[...] = jnp.zeros_like(dk_ref)
            dv_ref[...] = jnp.zeros_like(dv_ref)

        upto = _tri2(bk, lambda r, c: r <= c)
        before = _tri2(bk, lambda r, c: r < c)
        nkb = (i + 1) * per
        first = nkb - jnp.max(cnt_ref[0]).astype(jnp.int32)

        def visit(carry, jb, r0):
            off = pl.multiple_of(jb * bk, bk)
            out = []
            for hh in range(hg):
                keep_sum, g_sum, dq = carry[hh]
                qb, dob = q_ref[r0:, _lanes(hh)], do_ref[r0:, _lanes(hh)]
                kb = k_ref[pl.ds(off, bk), _lanes(hh)]
                vb = v_ref[pl.ds(off, bk), _lanes(hh)]
                z, valid, e, log_keep = _sb_block(qb, kb, i * bq + r0, jb, scale)
                log_rest = rt_ref[hh, r0:, :] - keep_sum[r0:] - _dot_split(log_keep, upto)
                attn = jnp.where(valid, jnp.exp(z + log_keep + log_rest), 0.0)
                g = attn * _dot_nt(dob, vb)
                g_before = g_sum[r0:] + _dot_split(g, before)
                inv = 1.0 / (1.0 + e)
                sig = jnp.where(z >= 0, inv, e * inv)
                dz = (jnp.where(valid, g * (1.0 - sig) - g_before * sig, 0.0) * scale).astype(BF16)
                dk_ref[pl.ds(off, bk), _lanes(hh)] += _dot_tn(dz, qb)
                dv_ref[pl.ds(off, bk), _lanes(hh)] += _dot_tn(attn.astype(BF16), dob)
                new = (keep_sum[r0:] + jnp.sum(log_keep, axis=-1, keepdims=True),
                       g_sum[r0:] + jnp.sum(g, axis=-1, keepdims=True),
                       dq[r0:] + _dot(dz, kb))
                if r0:
                    new = tuple(jnp.concatenate([old[:r0], x], axis=0) for old, x in zip(carry[hh], new))
                out.append(new)
            return tuple(out)

        zero = jnp.zeros((bq, 1), F32)
        res = lax.fori_loop(first, nkb - per, lambda jb, c: visit(c, jb, 0),
                            tuple((zero, zero, jnp.zeros((bq, dh), F32)) for _ in range(hg)))
        for dgl in range(per):
            res = visit(res, i * per + dgl, dgl * bk)
        for hh in range(hg):
            dq_ref[:, _lanes(hh)] = res[hh][2]
        if ns:
            pl.when(at_last)(finish)

    return pl.pallas_call(
        body, name=name, grid=grid,
        in_specs=[pl.BlockSpec((bq, 128), lambda h, i: (i, h)),
                  pl.BlockSpec((rows, 128), lambda h, i: (0, ATT_TILES + h)),
                  pl.BlockSpec((rows, 128), lambda h, i: (0, 2 * ATT_TILES + h)),
                  pl.BlockSpec((hg, bq, 1), lambda h, i: (h, i, 0)),
                  pl.BlockSpec((hg, bq, 1), lambda h, i: (h, i, 0)),
                  pl.BlockSpec((bq, 128), lambda h, i: (i, h))] + [HBM] * ns,
        out_specs=[pl.BlockSpec((bq, 128), lambda h, i: (i, h)),
                   pl.BlockSpec((rows, 128), lambda h, i: (0, h)),
                   pl.BlockSpec((rows, 128), lambda h, i: (0, h))] + [HBM] * (2 * ns),
        out_shape=[jax.ShapeDtypeStruct((rows, GROUP), F32)] * 3
        + [jax.ShapeDtypeStruct(s.shape, s.dtype) for s in parts] * 2,
        scratch_shapes=_reduce_sems(ns) if ns else [],
        compiler_params=_params("arbitrary", "arbitrary"),
    )(qkv, qkv, qkv, rt, cnt, do, *parts)


def _head_sum(x, ones_bd):
    return _dot_split(x, ones_bd)


def _rwkv_pre(p, p_prev, mu, w0, a0, k_k, k_a, w_up, a_up, g_up, ones_bd):
    xs = p + (p_prev - p) * mu
    r = xs[:, :GROUP]
    k0 = xs[:, GROUP:2 * GROUP]
    v = xs[:, 2 * GROUP:3 * GROUP]
    lo = xs[:, 3 * GROUP:]
    wa = w0 + _dot(jnp.tanh(lo).astype(BF16), w_up.astype(BF16))
    w = -(jnp.maximum(-wa, 0.0) + jnp.log(1.0 + jnp.exp(-jnp.abs(wa)))) - 0.5
    log_decay = -jnp.exp(w)
    alpha = _sigmoid(a0 + _dot(lo.astype(BF16), a_up.astype(BF16)))
    gate = _dot(_sigmoid(lo).astype(BF16), g_up.astype(BF16))
    kk = k0 * k_k
    kk = kk * lax.rsqrt(jnp.maximum(_head_sum(kk * kk, ones_bd), 1e-24))
    k = k0 * (1.0 + (alpha - 1.0) * k_a)
    return r, log_decay, k, v, -kk, kk * alpha, gate


def _rwkv_post(y, r, k, v, gate, lnx_w, lnx_b, r_k, ones_bd):
    mean = _head_sum(y, ones_bd) * (1.0 / HEAD)
    yc = y - mean
    var = _head_sum(yc * yc, ones_bd) * (1.0 / HEAD)
    yn = yc * lax.rsqrt(var + LNX_EPS) * lnx_w + lnx_b
    bonus = _head_sum(r * k * r_k, ones_bd) * v
    return (yn + bonus) * gate


_PRE_VEC = 5
_PRE_MAT = 3


def _heads(x):
    return jnp.stack([x[:, _lanes(h)] for h in range(N_HEADS)])


def _unheads(x):
    return jnp.concatenate([x[h] for h in range(N_HEADS)], axis=1)


def _edge_spec(tm, width):
    return pl.BlockSpec((8, width), lambda i: (jnp.maximum(i * (tm // 8) - 1, 0), 0))


def _previous_rows(p_ref, edge_ref):
    p = p_ref[...]
    edge = jnp.where(pl.program_id(0) == 0, 0.0, edge_ref[7:8, :])
    row = lax.broadcasted_iota(jnp.int32, (p.shape[0], 1), 0)
    return jnp.where(row == 0, edge, pltpu.roll(p, 1, axis=0))


def _rwkv_pre_fwd(name, p, vecs, mats, ones_bd):
    rows = p.shape[0]
    tm = _row_tile(rows)
    row_spec = lambda w: pl.BlockSpec((tm, w), lambda i: (i, 0))
    full = lambda a: pl.BlockSpec(a.shape, lambda i: (0,) * a.ndim)

    def body(p_ref, edge_ref, *refs):
        ins = [r[...] for r in refs[:_PRE_VEC + _PRE_MAT + 1]]
        outs = refs[_PRE_VEC + _PRE_MAT + 1:]
        for o_ref, val in zip(outs, _rwkv_pre(p_ref[...], _previous_rows(p_ref, edge_ref), *ins)):
            o_ref[...] = val

    return pl.pallas_call(
        body, name=name, grid=(rows // tm,),
        in_specs=[row_spec(RW_COLS), _edge_spec(tm, RW_COLS)] + [full(a) for a in (*vecs, *mats, ones_bd)],
        out_specs=[row_spec(GROUP)] * 7,
        out_shape=[jax.ShapeDtypeStruct((rows, GROUP), F32)] * 7,
        compiler_params=_params("arbitrary"),
    )(p, p, *vecs, *mats, ones_bd)


def _rwkv_pre_bwd(name, p, vecs, mats, ones_bd, cts_scan, ct_gate, cts_b):
    rows = p.shape[0]
    tm = _row_tile(rows)
    n_par = _PRE_VEC + _PRE_MAT
    row_spec = lambda w: pl.BlockSpec((tm, w), lambda i: (i, 0))
    full = lambda a: pl.BlockSpec(a.shape, lambda i: (0,) * a.ndim)

    def body(*refs):
        p_ref, edge_ref = refs[0], refs[1]
        par = [r[...] for r in refs[2:2 + n_par]]
        ones = refs[2 + n_par][...]
        cta = [r[...] for r in refs[3 + n_par:10 + n_par]]
        ctb = [r[...] for r in refs[10 + n_par:13 + n_par]]
        outs = refs[13 + n_par:]
        ct = (cta[0] + ctb[0], cta[1], cta[2] + ctb[1], cta[3] + ctb[2], cta[4], cta[5], cta[6])
        _, vjp = jax.vjp(lambda pv, ppv, *pr: _rwkv_pre(pv, ppv, *pr, ones),
                         p_ref[...], _previous_rows(p_ref, edge_ref), *par)
        grads = vjp(ct)
        outs[0][...] = grads[0]
        outs[1][...] = grads[1]

        @pl.when(pl.program_id(0) == 0)
        def _():
            for o_ref in outs[2:]:
                o_ref[...] = jnp.zeros_like(o_ref)

        for o_ref, gval in zip(outs[2:], grads[2:]):
            o_ref[...] += gval

    par_arrays = (*vecs, *mats)
    return pl.pallas_call(
        body, name=name, grid=(rows // tm,),
        in_specs=([row_spec(RW_COLS), _edge_spec(tm, RW_COLS)] + [full(a) for a in (*par_arrays, ones_bd)]
                  + [row_spec(GROUP)] * 10),
        out_specs=[row_spec(RW_COLS)] * 2 + [full(a) for a in par_arrays],
        out_shape=([jax.ShapeDtypeStruct((rows, RW_COLS), F32)] * 2
                   + [jax.ShapeDtypeStruct(a.shape, F32) for a in par_arrays]),
        compiler_params=_params("arbitrary"),
    )(p, p, *par_arrays, ones_bd, *cts_scan, ct_gate, *cts_b)


def _rwkv_post_fwd(name, y, r, k, v, gate, vecs, ones_bd):
    rows = r.shape[0]
    tm = _row_tile(rows)
    row_spec = pl.BlockSpec((tm, GROUP), lambda i: (i, 0))
    full = lambda a: pl.BlockSpec(a.shape, lambda i: (0,) * a.ndim)

    def body(*refs):
        refs[-1][...] = _rwkv_post(*(r_[...] for r_ in refs[:-1]))

    return pl.pallas_call(
        body, name=name, grid=(rows // tm,),
        in_specs=[row_spec] * 5 + [full(a) for a in (*vecs, ones_bd)],
        out_specs=row_spec,
        out_shape=jax.ShapeDtypeStruct((rows, GROUP), F32),
        compiler_params=_params("arbitrary"),
    )(y, r, k, v, gate, *vecs, ones_bd)


def _rwkv_post_bwd(name, y, r, k, v, gate, vecs, ones_bd, dout):
    rows = r.shape[0]
    tm = _row_tile(rows)
    row_spec = pl.BlockSpec((tm, GROUP), lambda i: (i, 0))
    full = lambda a: pl.BlockSpec(a.shape, lambda i: (0,) * a.ndim)

    def body(*refs):
        vals = [r_[...] for r_ in refs[:8]]
        ones = refs[8][...]
        dout_v = refs[9][...]
        outs = refs[10:]
        _, vjp = jax.vjp(lambda *a: _rwkv_post(*a, ones), *vals)
        grads = vjp(dout_v)
        for o_ref, gval in zip(outs[:5], grads[:5]):
            o_ref[...] = gval

        @pl.when(pl.program_id(0) == 0)
        def _():
            for o_ref in outs[5:]:
                o_ref[...] = jnp.zeros_like(o_ref)

        for o_ref, gval in zip(outs[5:], grads[5:]):
            o_ref[...] += gval

    return pl.pallas_call(
        body, name=name, grid=(rows // tm,),
        in_specs=[row_spec] * 5 + [full(a) for a in (*vecs, ones_bd)] + [row_spec],
        out_specs=[row_spec] * 5 + [full(a) for a in vecs],
        out_shape=[jax.ShapeDtypeStruct((rows, GROUP), F32)] * 5 + [jax.ShapeDtypeStruct(a.shape, F32) for a in vecs],
        compiler_params=_params("arbitrary"),
    )(y, r, k, v, gate, *vecs, ones_bd, dout)


_NN = (((2,), (1,)), ((0,), (0,)))
_NT = (((2,), (2,)), ((0,), (0,)))
_TN = (((1,), (1,)), ((0,), (0,)))


_BWD_FORMS = {"nn": (("nt", False), ("tn", False)),
              "nt": (("nn", False), ("tn", True)),
              "tn": (("nt", True), ("nn", False))}
_DIMS = {"nn": _NN, "nt": _NT, "tn": _TN}


def _bdot(a, b, form):
    return lax.dot_general(a.astype(BF16), b.astype(BF16), _DIMS[form], preferred_element_type=F32)


@functools.partial(jax.custom_vjp, nondiff_argnums=(2,))
def _bmm(a, b, form):
    return _bdot(a, b, form)


def _bmm_fwd(a, b, form):
    return _bdot(a, b, form), (a.astype(BF16), b.astype(BF16))


def _bmm_bwd(form, res, dc):
    a, b = res
    (fa, swap_a), (fb, swap_b) = _BWD_FORMS[form]
    da = _bdot(b, dc, fa) if swap_a else _bdot(dc, b, fa)
    db = _bdot(dc, a, fb) if swap_b else _bdot(a, dc, fb)
    return da, db


_bmm.defvjp(_bmm_fwd, _bmm_bwd)


@jax.custom_vjp
def _cumsum_steps(x):
    return _tri_apply(x, lambda r, c: r >= c)


def _tri_apply(x, cmp):
    nh, c, _ = x.shape
    tri = cmp(lax.broadcasted_iota(jnp.int32, (c, c), 0), lax.broadcasted_iota(jnp.int32, (c, c), 1))
    tri = jnp.broadcast_to(tri.astype(BF16)[None], (nh, c, c))
    hi, lo = _split2(x)
    return (lax.dot_general(tri, hi, _NN, preferred_element_type=F32)
            + lax.dot_general(tri, lo, _NN, preferred_element_type=F32))


_cumsum_steps.defvjp(lambda x: (_cumsum_steps(x), None), lambda _, d: (_tri_apply(d, lambda r, c: r <= c),))


def _chunk(state, r, log_w, k, v, a, b):
    nh, c, _ = r.shape
    row = lax.broadcasted_iota(jnp.int32, (c, c), 0)
    col = lax.broadcasted_iota(jnp.int32, (c, c), 1)
    cum = _cumsum_steps(log_w)
    mid = cum[:, c // 2 - 1:c // 2, :]
    a_t = a * jnp.exp(cum - log_w - mid)
    r_t = r * jnp.exp(cum - mid)
    back = jnp.exp(mid - cum)
    b_t = b * back
    k_t = k * back
    strict, incl = (row > col)[None], (row >= col)[None]
    ar = jnp.concatenate([a_t, r_t], axis=1)
    on_b = _bmm(ar, b_t, "nt")
    on_k = _bmm(ar, k_t, "nt")
    n_mat = jnp.where(strict, on_b[:, :c], 0.0)
    p_mat = jnp.where(incl, on_b[:, c:], 0.0)
    m_mat = jnp.where(strict, on_k[:, :c], 0.0)
    q_mat = jnp.where(incl, on_k[:, c:], 0.0)
    inv, power, span = n_mat, _bmm(n_mat, n_mat, "nn"), 2
    while span < c:
        both = _bmm(jnp.concatenate([power, inv], axis=1), power, "nn")
        inv = inv + power + both[:, c:]
        power = both[:, :c]
        span *= 2
    s_mid = state * jnp.swapaxes(jnp.exp(mid), 1, 2)
    x = _bmm(jnp.concatenate([a_t, m_mat], axis=2), jnp.concatenate([s_mid, v], axis=1), "nn")
    u = x + _bmm(inv, x, "nn")
    y = _bmm(jnp.concatenate([r_t, p_mat, q_mat], axis=2), jnp.concatenate([s_mid, u, v], axis=1), "nn")
    grown = _bmm(jnp.concatenate([b_t, k_t], axis=1), jnp.concatenate([u, v], axis=1), "tn")
    s_new = (s_mid + grown) * jnp.swapaxes(jnp.exp(cum[:, c - 1:c, :] - mid), 1, 2)
    return y, s_new


def _scan_fwd(name, ops):
    rows = ops[0].shape[0]
    nh, dh = N_HEADS, HEAD
    nc, per = rows // CHUNK, SCAN_CHUNKS
    spec = pl.BlockSpec((per * CHUNK, GROUP), lambda c: (c, 0))

    def body(r_ref, w_ref, k_ref, v_ref, a_ref, b_ref, y_ref, st_ref, state):
        @pl.when(pl.program_id(0) == 0)
        def _():
            state[...] = jnp.zeros_like(state)

        s = state[...]
        for u in range(per):
            at = slice(u * CHUNK, (u + 1) * CHUNK)
            st_ref[u] = s
            y, s = _chunk(s, *(_heads(ref[at, :]) for ref in (r_ref, w_ref, k_ref, v_ref, a_ref, b_ref)))
            y_ref[at, :] = _unheads(y)
        state[...] = s

    return pl.pallas_call(
        body, name=name, grid=(nc // per,),
        in_specs=[spec] * 6,
        out_specs=[spec, pl.BlockSpec((per, nh, dh, dh), lambda c: (c, 0, 0, 0))],
        out_shape=[jax.ShapeDtypeStruct((rows, GROUP), F32), jax.ShapeDtypeStruct((nc, nh, dh, dh), F32)],
        scratch_shapes=[pltpu.VMEM((nh, dh, dh), F32)],
        compiler_params=_params("arbitrary"),
    )(*ops)


def _scan_bwd(name, ops, states, dy):
    rows = ops[0].shape[0]
    nh, dh = N_HEADS, HEAD
    nc, per = rows // CHUNK, SCAN_CHUNKS
    steps = nc // per
    spec = pl.BlockSpec((per * CHUNK, GROUP), lambda c: (steps - 1 - c, 0))

    def body(r_ref, w_ref, k_ref, v_ref, a_ref, b_ref, st_ref, dy_ref, *rest):
        outs, dstate = rest[:6], rest[6]

        @pl.when(pl.program_id(0) == 0)
        def _():
            dstate[...] = jnp.zeros_like(dstate)

        ds = dstate[...]
        for u in reversed(range(per)):
            at = slice(u * CHUNK, (u + 1) * CHUNK)
            _, vjp = jax.vjp(_chunk, st_ref[u],
                             *(_heads(ref[at, :]) for ref in (r_ref, w_ref, k_ref, v_ref, a_ref, b_ref)))
            grads = vjp((_heads(dy_ref[at, :]), ds))
            ds = grads[0]
            for o_ref, gval in zip(outs, grads[1:]):
                o_ref[at, :] = _unheads(gval)
        dstate[...] = ds

    return pl.pallas_call(
        body, name=name, grid=(steps,),
        in_specs=[spec] * 6 + [pl.BlockSpec((per, nh, dh, dh), lambda c: (steps - 1 - c, 0, 0, 0)), spec],
        out_specs=[spec] * 6,
        out_shape=[jax.ShapeDtypeStruct((rows, GROUP), F32)] * 6,
        scratch_shapes=[pltpu.VMEM((nh, dh, dh), F32)],
        compiler_params=_params("arbitrary"),
    )(*ops, states, dy)


def _shift_up(x):
    return jnp.concatenate([x[1:], jnp.zeros((1, x.shape[1]), x.dtype)], axis=0)


def _pad_rows(x, rows):
    return jnp.concatenate([x, jnp.zeros((rows - x.shape[0],) + x.shape[1:], x.dtype)], axis=0)


def _pad_cols(x, cols):
    return jnp.concatenate([x, jnp.zeros(x.shape[:-1] + (cols - x.shape[-1],), x.dtype)], axis=-1)


def _lora_pad(w_up, a_up, g_up):
    z = lambda n: jnp.zeros((n, GROUP), F32)
    return (jnp.concatenate([w_up, z(LORA_PAD - LORA_W)], 0),
            jnp.concatenate([z(LORA_W), a_up, z(LORA_PAD - LORA_W - LORA_A)], 0),
            jnp.concatenate([z(LORA_W + LORA_A), g_up, z(LORA_PAD - LORA_W - LORA_A - LORA_G)], 0))


MID = ['w_in']
LATE = ['ffn2_w_gate', 'ffn2_w_up', 'ffn2_w_down', 'w_out']


def _local_step(x, tgt, w, late=None):
    d = x.shape[1]
    zeros = jnp.zeros((META_PAD, d), F32)
    h0 = jnp.concatenate([zeros, w["meta_tokens"], x], axis=0)
    tgt_p = jnp.concatenate([jnp.zeros((ROW0, d), F32), tgt], axis=0)
    ones_bd = ((lax.broadcasted_iota(jnp.int32, (2 * GROUP, GROUP), 0) % GROUP) // HEAD
               == lax.broadcasted_iota(jnp.int32, (2 * GROUP, GROUP), 1) // HEAD).astype(BF16)
    pre_vecs = (_pad_cols(w["rwkv_mu"], RW_COLS), w["rwkv_w0"], w["rwkv_a0"], w["rwkv_k_k"], w["rwkv_k_a"])
    pre_mats = _lora_pad(w["rwkv_w_up"], w["rwkv_a_up"], w["rwkv_g_up"])
    post_vecs = (w["rwkv_lnx_w"], w["rwkv_lnx_b"], w["rwkv_r_k"].reshape(1, GROUP))

    h1, a1, b1, *gathered = _ffn_fwd("ffn1_fwd", h0, w["ffn1_norm"], w["ffn1_w_gate"], w["ffn1_w_up"],
                                     w["ffn1_w_down"], late and ("gather", late.shards["mid"]))
    if late is not None:
        w = {**w, **late.join("mid", gathered)}
    w_in = _pad_cols(w["w_in"], IN_COLS_PAD)
    qkv, p, n2 = _norm_proj("in_proj", h1, w["mix_norm"], w_in)
    sb, rest_total, visited, *gathered = _sb_fwd("sb_fwd", qkv, late.shards["late"] if late else ())
    if late is not None:
        w = {**w, **late.join("late", gathered)}
    pre = _rwkv_pre_fwd("rwkv_pre_fwd", p, pre_vecs, pre_mats, ones_bd)
    scan_ops, token_ops = pre[:6], (pre[0], pre[2], pre[3], pre[6])
    y, states = _scan_fwd("rwkv_scan_fwd", scan_ops)
    rw = _rwkv_post_fwd("rwkv_post_fwd", y, *token_ops, post_vecs, ones_bd)
    h2, mix = _out_proj("out_proj", h1, sb, rw, w["w_out"])
    h3, a2, b2 = _ffn_fwd("ffn2_fwd", h2, w["ffn2_norm"], w["ffn2_w_gate"], w["ffn2_w_up"], w["ffn2_w_down"])
    loss8, dh3, g_final = _loss_head("loss_head", h3, w["final_norm"].reshape(1, d), tgt_p)

    g = {"final_norm": g_final.reshape(d)}
    dh2, da2, db2, s2, n3, dhh3, g["ffn2_norm"] = _ffn_bwd(
        "ffn2_bwd", dh3, h2, w["ffn2_norm"], a2, b2, w["ffn2_w_gate"], w["ffn2_w_up"], w["ffn2_w_down"])
    g["ffn2_w_gate"] = _mm_tn("ffn2_dgate", da2, n3)
    g["ffn2_w_up"] = _mm_tn("ffn2_dup", db2, n3)
    g["ffn2_w_down"] = _mm_tn("ffn2_ddown", s2, dhh3)
    dsb, drw, dh2b = _out_proj_bwd("out_proj_bwd", dh2, w["w_out"])
    g["w_out"] = _mm_tn("out_proj_dw", mix, dh2b)
    dq, dk, dv, *reduced_late = _sb_bwd("sb_bwd", qkv, rest_total, visited, dsb, late.parts("late", g) if late else ())
    post_g = _rwkv_post_bwd("rwkv_post_bwd", y, *token_ops, post_vecs, ones_bd, drw)
    g["rwkv_lnx_w"], g["rwkv_lnx_b"] = post_g[5], post_g[6]
    g["rwkv_r_k"] = post_g[7].reshape(1, N_HEADS, HEAD)
    scan_g = _scan_bwd("rwkv_scan_bwd", scan_ops, states, post_g[0])
    pre_g = _rwkv_pre_bwd("rwkv_pre_bwd", p, pre_vecs, pre_mats, ones_bd, scan_g, post_g[4], post_g[1:4])
    g["rwkv_mu"] = pre_g[2][:, :w["rwkv_mu"].shape[1]]
    g["rwkv_w0"], g["rwkv_a0"], g["rwkv_k_k"], g["rwkv_k_a"] = pre_g[3:7]
    g["rwkv_w_up"] = pre_g[7][:LORA_W]
    g["rwkv_a_up"] = pre_g[8][LORA_W:LORA_W + LORA_A]
    g["rwkv_g_up"] = pre_g[9][LORA_W + LORA_A:LORA_W + LORA_A + LORA_G]
    dp = pre_g[0] + _shift_up(pre_g[1])
    live = (jnp.arange(h0.shape[0]) >= META_PAD)[:, None]
    dproj = jnp.where(live, jnp.concatenate([dq, dk, dv, dp], axis=1), 0.0).astype(BF16)
    g["w_in"] = _mm_tn("in_proj_dw", n2, dproj)[:, :w["w_in"].shape[1]]
    dh1, g["mix_norm"] = _norm_proj_bwd("in_proj_bwd", dproj, w_in, h1, w["mix_norm"], dh2)
    dh0, da1, db1, s1, n1, dhh1, g["ffn1_norm"], *reduced_mid = _ffn_bwd(
        "ffn1_bwd", dh1, h0, w["ffn1_norm"], a1, b1, w["ffn1_w_gate"], w["ffn1_w_up"], w["ffn1_w_down"],
        late and ("reduce", late.parts("mid", g)))
    g["ffn1_w_gate"] = _mm_tn("ffn1_dgate", da1, n1)
    reduced_gate, reduced_up = [], []
    if late is None:
        g["ffn1_w_up"] = _mm_tn("ffn1_dup", db1, n1)
        g["ffn1_w_down"] = _mm_tn("ffn1_ddown", s1, dhh1)
    else:
        g["ffn1_w_up"], *reduced_gate = _mm_tn("ffn1_dup", db1, n1, ("reduce", late.parts("gate", g)))
        g["ffn1_w_down"], *reduced_up = _mm_tn("ffn1_ddown", s1, dhh1, ("reduce", late.parts("up", g)))
    g["meta_tokens"] = dh0[META_PAD:ROW0]
    return loss8[0, 0], dh0[ROW0:], g, {"mid": reduced_mid, "late": reduced_late, "gate": reduced_gate,
                                        "up": reduced_up}


N_CHIPS = 4
N_DEV = 8
HBM = pl.BlockSpec(memory_space=pltpu.HBM)


def _place():
    return lax.axis_index("x"), lax.axis_index("y"), lax.axis_index("c")


def _other_chips(x, y):
    return [(1 - x, y), (x, 1 - y), (1 - x, 1 - y)]


def _gather_sems(n):
    return [pltpu.SemaphoreType.DMA((3 * n,)), pltpu.SemaphoreType.DMA((3 * n,)), pltpu.SemaphoreType.DMA((n,)),
            pltpu.SemaphoreType.DMA((3 * n,)), pltpu.SemaphoreType.DMA((3 * n,))]


def _gather_exchange(ins, outs, sems):
    n = len(ins)
    half = [r.shape[0] // 2 for r in ins]
    send, recv, local, d2d_send, d2d_recv = sems
    x, y, c = _place()
    me = 2 * x + y
    chips = _other_chips(x, y)

    def rows_of(k, h):
        return pl.ds(pl.multiple_of(h * half[k], 8), half[k])

    def own(k):
        return pltpu.make_async_copy(ins[k], outs[k].at[me], local.at[k])

    def copy(j, k, slot):
        return pltpu.make_async_remote_copy(
            src_ref=ins[k].at[rows_of(k, c)], dst_ref=outs[k].at[slot, rows_of(k, c)],
            send_sem=send.at[j * n + k], recv_sem=recv.at[j * n + k],
            device_id=(chips[j][0], chips[j][1], c), device_id_type=MESH)

    def passed(j, k, h):
        slot = 2 * chips[j][0] + chips[j][1]
        return pltpu.make_async_remote_copy(
            src_ref=outs[k].at[slot, rows_of(k, h)], dst_ref=outs[k].at[slot, rows_of(k, h)],
            send_sem=d2d_send.at[j * n + k], recv_sem=d2d_recv.at[j * n + k],
            device_id=(x, y, 1 - c), device_id_type=MESH)

    def start():
        for k in range(n):
            own(k).start()
        for j in range(3):
            for k in range(n):
                copy(j, k, me).start()

    def finish():
        for j in range(3):
            for k in range(n):
                copy(j, k, 2 * chips[j][0] + chips[j][1]).wait_recv()
                passed(j, k, c).start()
        for j in range(3):
            for k in range(n):
                passed(j, k, 1 - c).wait_recv()
        for j in range(3):
            for k in range(n):
                copy(j, k, me).wait_send()
                passed(j, k, c).wait_send()
        for k in range(n):
            own(k).wait()

    return start, finish


def _gather_shards(name, shards):
    n = len(shards)

    def body(*refs):
        start, finish = _gather_exchange(refs[:n], refs[n:2 * n], refs[2 * n:])
        start()
        finish()

    return pl.pallas_call(
        body, name=name,
        in_specs=[HBM] * n, out_specs=[HBM] * n,
        out_shape=[jax.ShapeDtypeStruct((N_CHIPS,) + s.shape, s.dtype) for s in shards],
        scratch_shapes=_gather_sems(n),
    )(*shards)


def _pair_exchange(name, parts):
    n = len(parts)
    half = [s.shape[1] // 2 for s in parts]

    def body(*refs):
        ins, outs = refs[:n], refs[n:2 * n]
        send, recv = refs[2 * n:]
        x, y, c = _place()

        def copy(k):
            rows = pl.ds(pl.multiple_of((1 - c) * half[k], 8), half[k])
            return pltpu.make_async_remote_copy(
                src_ref=ins[k].at[:, rows], dst_ref=outs[k], send_sem=send.at[k], recv_sem=recv.at[k],
                device_id=(x, y, 1 - c), device_id_type=MESH)

        for k in range(n):
            copy(k).start()
        for k in range(n):
            copy(k).wait_recv()
        for k in range(n):
            copy(k).wait_send()

    return pl.pallas_call(
        body, name=name,
        in_specs=[HBM] * n, out_specs=[HBM] * n,
        out_shape=[jax.ShapeDtypeStruct((s.shape[0], s.shape[1] // 2, s.shape[2]), s.dtype) for s in parts],
        scratch_shapes=[pltpu.SemaphoreType.DMA((n,)), pltpu.SemaphoreType.DMA((n,))],
    )(*parts)


def _pair_add(name, part, other):
    nch, rows, cols = part.shape
    half = rows // 2

    def body(p_ref, o_ref, out_ref):
        c = lax.axis_index("c")
        mine = p_ref[0, pl.ds(pl.multiple_of(c * half, 16), half), :]
        out_ref[0] = (mine.astype(F32) + o_ref[0].astype(F32)).astype(out_ref.dtype)

    return pl.pallas_call(
        body, name=name, grid=(nch,),
        in_specs=[pl.BlockSpec((1, rows, cols), lambda j: (j, 0, 0)),
                  pl.BlockSpec((1, half, cols), lambda j: (j, 0, 0))],
        out_specs=pl.BlockSpec((1, half, cols), lambda j: (j, 0, 0)),
        out_shape=jax.ShapeDtypeStruct((nch, half, cols), part.dtype),
        compiler_params=_params("arbitrary"),
    )(part, other)


def _reduce_sems(n):
    return [pltpu.SemaphoreType.DMA((3 * n,)), pltpu.SemaphoreType.DMA((3 * n,)), pltpu.SemaphoreType.DMA((n,)),
            pltpu.SemaphoreType.DMA((n,)), pltpu.SemaphoreType.DMA((n,))]


def _reduce_exchange(ins, got, sib, sems):
    n = len(ins)
    send, recv, local, d2d_send, d2d_recv = sems
    x, y, c = _place()
    me = 2 * x + y
    chips = _other_chips(x, y)

    def own(k):
        return pltpu.make_async_copy(ins[k].at[me], got[k].at[me], local.at[k])

    def copy(j, k, shard, slot):
        return pltpu.make_async_remote_copy(
            src_ref=ins[k].at[shard], dst_ref=got[k].at[slot], send_sem=send.at[j * n + k],
            recv_sem=recv.at[j * n + k], device_id=(chips[j][0], chips[j][1], c), device_id_type=MESH)

    def swap(k):
        return pltpu.make_async_remote_copy(
            src_ref=got[k], dst_ref=sib[k], send_sem=d2d_send.at[k], recv_sem=d2d_recv.at[k],
            device_id=(x, y, 1 - c), device_id_type=MESH)

    def start():
        for k in range(n):
            own(k).start()
        for j in range(3):
            for k in range(n):
                copy(j, k, 2 * chips[j][0] + chips[j][1], me).start()

    def finish():
        for k in range(n):
            own(k).wait()
            for j in range(3):
                copy(j, k, me, 2 * chips[j][0] + chips[j][1]).wait_recv()
            swap(k).start()
        for k in range(n):
            swap(k).wait_recv()
        for j in range(3):
            for k in range(n):
                copy(j, k, me, me).wait_send()
        for k in range(n):
            swap(k).wait_send()

    return start, finish


def _reduce_shards(name, parts):
    n = len(parts)

    def body(*refs):
        start, finish = _reduce_exchange(refs[:n], refs[n:2 * n], refs[2 * n:3 * n], refs[3 * n:])
        start()
        finish()

    return pl.pallas_call(
        body, name=name,
        in_specs=[HBM] * n, out_specs=[HBM] * (2 * n),
        out_shape=[jax.ShapeDtypeStruct(s.shape, s.dtype) for s in parts] * 2,
        scratch_shapes=_reduce_sems(n),
    )(*parts)


def _all_reduce_small(name, vec):
    rows = vec.shape[0]

    def body(v_ref, o_ref, buf, send, recv):
        x, y, c = _place()
        me = 4 * x + 2 * y + c
        peers = [(x ^ (r >> 2), y ^ ((r >> 1) & 1), c ^ (r & 1)) for r in range(1, N_DEV)]

        def copy(r, slot):
            px, py, pc = peers[r]
            return pltpu.make_async_remote_copy(
                src_ref=v_ref, dst_ref=buf.at[slot], send_sem=send.at[r], recv_sem=recv.at[r],
                device_id=(px, py, pc), device_id_type=MESH)

        sent = [copy(r, me) for r in range(N_DEV - 1)]
        for cp in sent:
            cp.start()
        buf[me] = v_ref[...]
        for r in range(N_DEV - 1):
            px, py, pc = peers[r]
            copy(r, 4 * px + 2 * py + pc).wait_recv()
        total = buf[0]
        for dev in range(1, N_DEV):
            total = total + buf[dev]
        o_ref[...] = total
        for cp in sent:
            cp.wait_send()

    return pl.pallas_call(
        body, name=name,
        in_specs=[pl.BlockSpec(memory_space=pltpu.VMEM)], out_specs=pl.BlockSpec(memory_space=pltpu.VMEM),
        out_shape=jax.ShapeDtypeStruct(vec.shape, F32),
        scratch_shapes=[pltpu.VMEM((N_DEV, rows, 128), F32),
                        pltpu.SemaphoreType.DMA((N_DEV - 1,)), pltpu.SemaphoreType.DMA((N_DEV - 1,))],
        compiler_params=pltpu.CompilerParams(vmem_limit_bytes=VMEM_LIMIT),
    )(vec)


def _adamw(w, g, m, v):
    m = ADAM_B1 * m + (1.0 - ADAM_B1) * g
    v = ADAM_B2 * v + (1.0 - ADAM_B2) * (g * g)
    m_hat = m / (1.0 - ADAM_B1 ** ADAM_STEP)
    v_hat = v / (1.0 - ADAM_B2 ** ADAM_STEP)
    return -ADAM_LR * (m_hat / (jnp.sqrt(v_hat) + ADAM_EPS) + ADAM_WD * w), m, v


def _adamw_shard(name, core, w, m, v, got, sib):
    rows, cols = w.shape
    tr = rows // 4
    spec = pl.BlockSpec((tr, cols), lambda i, c_ref: (i, 0))
    spec4 = pl.BlockSpec((N_CHIPS, tr, cols), lambda i, c_ref: (0, i % 2, 0))

    def body(c_ref, w_ref, m_ref, v_ref, got_ref, sib_ref, g_ref, d_ref, mo_ref, vo_ref):
        def four(ref):
            return ((ref[0].astype(F32) + ref[1].astype(F32)) + ref[2].astype(F32)) + ref[3].astype(F32)

        g = jnp.where(pl.program_id(0) // 2 == c_ref[0], four(got_ref), four(sib_ref))
        g_ref[...] = g
        d_ref[...], mo_ref[...], vo_ref[...] = _adamw(w_ref[...], g, m_ref[...], v_ref[...])

    return pl.pallas_call(
        body, name=name,
        grid_spec=pltpu.PrefetchScalarGridSpec(
            num_scalar_prefetch=1, grid=(4,),
            in_specs=[spec, spec, spec, spec4, spec4], out_specs=[spec] * 4),
        out_shape=[jax.ShapeDtypeStruct((rows, cols), F32)] * 4,
        compiler_params=_params("arbitrary"),
    )(core, w, m, v, got, sib)


def _adamw_small(name, w, m, v, g):
    def body(w_ref, m_ref, v_ref, g_ref, d_ref, mo_ref, vo_ref):
        d_ref[...], mo_ref[...], vo_ref[...] = _adamw(w_ref[...], g_ref[...], m_ref[...], v_ref[...])

    return pl.pallas_call(body, name=name, out_shape=[jax.ShapeDtypeStruct(w.shape, F32)] * 3)(w, m, v, g)


def _cast_bf16(name, arrays):
    n = len(arrays)

    def body(*refs):
        for i_ref, o_ref in zip(refs[:n], refs[n:]):
            o_ref[...] = i_ref[...].astype(BF16)

    return pl.pallas_call(
        body, name=name, out_shape=[jax.ShapeDtypeStruct(a.shape, BF16) for a in arrays],
        compiler_params=pltpu.CompilerParams(vmem_limit_bytes=VMEM_LIMIT),
    )(*arrays)


def _pack(arrays, rows):
    flat = jnp.concatenate([a.reshape(-1) for a in arrays])
    return jnp.concatenate([flat, jnp.zeros((rows * 128 - flat.shape[0],), F32)]).reshape(rows, 128)


def _unpack(packed, shapes):
    flat, out, at = packed.reshape(-1), [], 0
    for s in shapes:
        size = 1
        for dim in s:
            size *= dim
        out.append(flat[at:at + size].reshape(s))
        at += size
    return out


def _rows_for(shapes):
    total = 0
    for s in shapes:
        size = 1
        for dim in s:
            size *= dim
        total += size
    return -(-total // 1024) * 8


WEIGHTS = ['meta_tokens', 'ffn1_norm', 'ffn1_w_gate', 'ffn1_w_up', 'ffn1_w_down', 'mix_norm', 'w_in', 'rwkv_mu',
           'rwkv_w0', 'rwkv_w_up', 'rwkv_a0', 'rwkv_a_up', 'rwkv_g_up', 'rwkv_k_k', 'rwkv_k_a', 'rwkv_r_k',
           'rwkv_lnx_w', 'rwkv_lnx_b', 'w_out', 'ffn2_norm', 'ffn2_w_gate', 'ffn2_w_up', 'ffn2_w_down', 'final_norm']
COL_CUT = ['ffn1_w_gate', 'ffn1_w_up', 'w_in', 'ffn2_w_gate', 'ffn2_w_up']
ROW_CUT = ['ffn1_w_down', 'w_out', 'ffn2_w_down']
SMALL_CUT = ['meta_tokens', 'rwkv_w_up', 'rwkv_a_up', 'rwkv_g_up']
TRANSPOSED = ['ffn1_w_gate', 'ffn1_w_up', 'ffn2_w_gate', 'ffn2_w_up']
BIG = COL_CUT + ROW_CUT
REPLICATED = [n for n in WEIGHTS if n not in BIG + SMALL_CUT]


def _join_cols(a):
    return a.transpose(1, 0, 2).reshape(a.shape[1], N_CHIPS * a.shape[2])


def _cut_cols(a):
    return a.reshape(a.shape[0], N_CHIPS, a.shape[1] // N_CHIPS).transpose(1, 0, 2)


def _step(x, loss_target, w, m, v):
    two = lambda a: a.reshape(a.shape[-2], a.shape[-1])

    def rows_cut(n, a):
        return jnp.swapaxes(two(a), 0, 1) if n in TRANSPOSED else two(a)

    def as_given(n, a, like):
        return (jnp.swapaxes(a, 0, 1) if n in TRANSPOSED else a).reshape(like.shape)

    col_cut = [n for n in COL_CUT + SMALL_CUT if n not in TRANSPOSED]

    def join(names, gathered):
        return {n: (_join_cols(a) if n in col_cut else a.reshape(-1, a.shape[-1])) for n, a in zip(names, gathered)}

    def pair_sums(tag, names, g):
        parts = [_cut_cols(g[n]) if n in col_cut else g[n].reshape(N_CHIPS, -1, g[n].shape[-1]) for n in names]
        arrived = _pair_exchange("pair_exchange_" + tag, parts)
        return [_pair_add("pair_add_" + n, p, o) for n, p, o in zip(names, parts, arrived)]

    first = [n for n in BIG if n not in MID + LATE]
    gathered_later = {"mid": MID, "late": LATE}
    groups = {**gathered_later, "gate": ["ffn1_w_gate"], "up": ["ffn1_w_up"]}
    cast = dict(zip(BIG, _cast_bf16("cast_weights", [rows_cut(n, w[n]) for n in BIG])))
    names = first + SMALL_CUT
    shards = [cast[n] for n in first] + [two(w[n]) for n in SMALL_CUT]
    full = {n: (two(w[n]) if w[n].ndim == 3 else w[n]) for n in REPLICATED}
    full.update(join(names, _gather_shards("gather_weights", shards)))
    full["rwkv_r_k"] = w["rwkv_r_k"]
    full["final_norm"] = w["final_norm"]

    late = types.SimpleNamespace(shards={k: [cast[n] for n in names] for k, names in gathered_later.items()},
                                 join=lambda k, gathered: join(groups[k], gathered),
                                 parts=lambda k, g: pair_sums(k, groups[k], g))

    loss, dx, g, reduced = _local_step(x[0], loss_target[0], full, late)
    loss = lax.psum(loss, ("x", "y", "c"))

    groups["down"] = ["ffn1_w_down"]
    reduced["down"] = list(_reduce_shards("reduce_gradients", pair_sums("down", groups["down"], g)))
    got, sib = {}, {}
    for k, names in groups.items():
        got.update(zip(names, reduced[k][:len(names)]))
        sib.update(zip(names, reduced[k][len(names):]))

    small_names = REPLICATED + SMALL_CUT
    small_shapes = [g[n].shape for n in small_names]
    small = _all_reduce_small("reduce_small", _pack([g[n] for n in small_names], _rows_for(small_shapes)))
    g_small = dict(zip(small_names, _unpack(small, small_shapes)))
    chip = 2 * lax.axis_index("x") + lax.axis_index("y")
    for n in SMALL_CUT:
        width = g_small[n].shape[1] // N_CHIPS
        g_small[n] = lax.dynamic_slice_in_dim(g_small[n], chip * width, width, axis=1)

    grad, delta, new_m, new_v = {}, {}, {}, {}
    core = lax.axis_index("c").astype(jnp.int32).reshape(1)
    for n in BIG:
        outs = _adamw_shard("adamw_" + n, core, rows_cut(n, w[n]), rows_cut(n, m[n]), rows_cut(n, v[n]), got[n], sib[n])
        grad[n], delta[n], new_m[n], new_v[n] = (as_given(n, o, w[n]) for o in outs)
    shapes = [w[n].shape for n in small_names]
    rows = _rows_for(shapes)
    packed = [_pack([t[n] for n in small_names], rows) for t in (w, m, v)]
    g_packed = _pack([g_small[n] for n in small_names], rows)
    outs = [_unpack(o, shapes) for o in _adamw_small("adamw_small", *packed, g_packed)]
    for i, n in enumerate(small_names):
        grad[n] = g_small[n].reshape(w[n].shape)
        delta[n], new_m[n], new_v[n] = outs[0][i], outs[1][i], outs[2][i]
    return loss, dx[None], grad, delta, new_m, new_v


def kernel(x, meta_tokens, ffn1_norm, ffn1_w_gate, ffn1_w_up, ffn1_w_down, mix_norm, w_in, rwkv_mu, rwkv_w0, rwkv_w_up, rwkv_a0, rwkv_a_up, rwkv_g_up, rwkv_k_k, rwkv_k_a, rwkv_r_k, rwkv_lnx_w, rwkv_lnx_b, w_out, ffn2_norm, ffn2_w_gate, ffn2_w_up, ffn2_w_down, final_norm, loss_target, m_meta_tokens, m_ffn1_norm, m_ffn1_w_gate, m_ffn1_w_up, m_ffn1_w_down, m_mix_norm, m_w_in, m_rwkv_mu, m_rwkv_w0, m_rwkv_w_up, m_rwkv_a0, m_rwkv_a_up, m_rwkv_g_up, m_rwkv_k_k, m_rwkv_k_a, m_rwkv_r_k, m_rwkv_lnx_w, m_rwkv_lnx_b, m_w_out, m_ffn2_norm, m_ffn2_w_gate, m_ffn2_w_up, m_ffn2_w_down, m_final_norm, v_meta_tokens, v_ffn1_norm, v_ffn1_w_gate, v_ffn1_w_up, v_ffn1_w_down, v_mix_norm, v_w_in, v_rwkv_mu, v_rwkv_w0, v_rwkv_w_up, v_rwkv_a0, v_rwkv_a_up, v_rwkv_g_up, v_rwkv_k_k, v_rwkv_k_a, v_rwkv_r_k, v_rwkv_lnx_w, v_rwkv_lnx_b, v_w_out, v_ffn2_norm, v_ffn2_w_gate, v_ffn2_w_up, v_ffn2_w_down, v_final_norm):
    w = dict(zip(WEIGHTS, (meta_tokens, ffn1_norm, ffn1_w_gate, ffn1_w_up, ffn1_w_down, mix_norm, w_in, rwkv_mu, rwkv_w0, rwkv_w_up, rwkv_a0, rwkv_a_up, rwkv_g_up, rwkv_k_k, rwkv_k_a, rwkv_r_k, rwkv_lnx_w, rwkv_lnx_b, w_out, ffn2_norm, ffn2_w_gate, ffn2_w_up, ffn2_w_down, final_norm)))
    m = dict(zip(WEIGHTS, (m_meta_tokens, m_ffn1_norm, m_ffn1_w_gate, m_ffn1_w_up, m_ffn1_w_down, m_mix_norm, m_w_in, m_rwkv_mu, m_rwkv_w0, m_rwkv_w_up, m_rwkv_a0, m_rwkv_a_up, m_rwkv_g_up, m_rwkv_k_k, m_rwkv_k_a, m_rwkv_r_k, m_rwkv_lnx_w, m_rwkv_lnx_b, m_w_out, m_ffn2_norm, m_ffn2_w_gate, m_ffn2_w_up, m_ffn2_w_down, m_final_norm)))
    v = dict(zip(WEIGHTS, (v_meta_tokens, v_ffn1_norm, v_ffn1_w_gate, v_ffn1_w_up, v_ffn1_w_down, v_mix_norm, v_w_in, v_rwkv_mu, v_rwkv_w0, v_rwkv_w_up, v_rwkv_a0, v_rwkv_a_up, v_rwkv_g_up, v_rwkv_k_k, v_rwkv_k_a, v_rwkv_r_k, v_rwkv_lnx_w, v_rwkv_lnx_b, v_w_out, v_ffn2_norm, v_ffn2_w_gate, v_ffn2_w_up, v_ffn2_w_down, v_final_norm)))
    loss, grad_x, grad, delta, new_m, new_v = _step(x, loss_target, w, m, v)
    return (loss, grad_x, *[grad[n] for n in WEIGHTS], *[delta[n] for n in WEIGHTS],
            *[new_m[n] for n in WEIGHTS], *[new_v[n] for n in WEIGHTS])
```

```python
import functools
import types

import jax
import jax.numpy as jnp
from jax import lax
from jax.experimental import pallas as pl
from jax.experimental.pallas import tpu as pltpu

F32 = jnp.float32
BF16 = jnp.bfloat16

RMS_EPS = 1e-6
LNX_EPS = 64e-5
N_META = 16
ROW0 = 128
META_PAD = ROW0 - N_META
HEAD = 64
N_HEADS = 8
GROUP = N_HEADS * HEAD
LORA_W, LORA_A, LORA_G = 32, 32, 96
LORA_PAD = 256
RW_COLS = 3 * GROUP + LORA_PAD
IN_COLS_PAD = 3 * GROUP + RW_COLS
ATT_BLOCK = 128
CHUNK = 64
SCAN_CHUNKS = 2
VMEM_LIMIT = 56 * 1024 * 1024

ADAM_LR, ADAM_B1, ADAM_B2, ADAM_EPS, ADAM_WD, ADAM_STEP = 0.001, 0.9, 0.999, 1e-08, 0.01, 10

MESH = pl.DeviceIdType.MESH


def _params(*sem):
    return pltpu.CompilerParams(dimension_semantics=tuple(sem), vmem_limit_bytes=VMEM_LIMIT)


def _dot(a, b):
    return lax.dot_general(a, b, (((1,), (0,)), ((), ())), preferred_element_type=F32)


def _dot_nt(a, b):
    return lax.dot_general(a, b, (((1,), (1,)), ((), ())), preferred_element_type=F32)


def _dot_tn(a, b):
    return lax.dot_general(a, b, (((0,), (0,)), ((), ())), preferred_element_type=F32)


def _split2(x):
    hi = x.astype(BF16)
    return hi, (x - hi.astype(F32)).astype(BF16)


def _sigmoid(x):
    return 1.0 / (1.0 + jnp.exp(-x))


def _rms_fwd(x, g):
    rstd = lax.rsqrt(jnp.mean(x * x, axis=-1, keepdims=True) + RMS_EPS)
    xhat = x * rstd
    return xhat * g, xhat, rstd


def _rms_bwd(dn, xhat, rstd, g):
    dxhat = dn * g
    dx = rstd * (dxhat - xhat * jnp.mean(dxhat * xhat, axis=-1, keepdims=True))
    return dx, jnp.sum(dn * xhat, axis=0, keepdims=True)


def _row_tile(rows):
    return 384 if rows % 384 == 0 else 128


def _half_tile(cols):
    return cols // 2 if cols % 256 == 0 else cols


def _tall_tile(rows, parts):
    return rows // parts if rows % (16 * parts) == 0 else _row_tile(rows)


def _call_with_exchange(name, body, grid, in_specs, out_specs, out_shape, scratch, operands, params, exchange):
    if exchange is None or not exchange[1]:
        return pl.pallas_call(body, name=name, grid=grid, in_specs=in_specs, out_specs=out_specs,
                              out_shape=out_shape, scratch_shapes=scratch, compiler_params=params)(*operands)
    kind, arrays = exchange
    ns, n_in, n_out, n_scr = len(arrays), len(in_specs), len(out_specs), len(scratch)
    whole = lambda a: pl.BlockSpec(a.shape, lambda *_: (0,) * a.ndim)
    if kind == "gather":
        results = [jax.ShapeDtypeStruct((N_CHIPS,) + s.shape, s.dtype) for s in arrays]
        sems, sent_specs, landed_specs = _gather_sems(ns), [HBM] * ns, [HBM] * ns
    elif kind == "reduce":
        results = [jax.ShapeDtypeStruct(s.shape, s.dtype) for s in arrays] * 2
        sems, sent_specs, landed_specs = _reduce_sems(ns), [HBM] * ns, [HBM] * (2 * ns)
    else:
        results = [jax.ShapeDtypeStruct(arrays[0].shape, F32)]
        sems, sent_specs, landed_specs = _all_reduce_scratch(arrays[0]), [whole(arrays[0])], [whole(arrays[0])]
    n_res = len(results)

    def carried(*refs):
        at = n_in + ns + n_out
        sent, landed = refs[n_in:n_in + ns], refs[at:at + n_res]
        own_scratch, sem_refs = refs[at + n_res:at + n_res + n_scr], refs[at + n_res + n_scr:]
        first, last = _first_and_last_step(grid)
        if kind == "gather":
            start, finish = _gather_exchange(sent, landed, sem_refs)
        elif kind == "reduce":
            start, finish = _reduce_exchange(sent, landed[:ns], landed[ns:], sem_refs)
        else:
            start, finish = _all_reduce_exchange(sent[0], landed[0], *sem_refs)
        pl.when(first)(start)
        body(*refs[:n_in], *refs[n_in + ns:at], *own_scratch)
        pl.when(last)(finish)

    return pl.pallas_call(
        carried, name=name, grid=grid, in_specs=list(in_specs) + sent_specs, out_specs=list(out_specs) + landed_specs,
        out_shape=list(out_shape) + results, scratch_shapes=list(scratch) + sems, compiler_params=params,
    )(*operands, *arrays)


def _ffn_fwd(name, h, g, wg, wu, wd, exchange=None):
    rows, d = h.shape
    f = wg.shape[0]
    tm, tf = _row_tile(rows), _half_tile(f)
    nj = f // tf

    def body(h_ref, g_ref, wg_ref, wu_ref, wd_ref, ho_ref, a_ref, b_ref, n_sc, acc_sc):
        j = pl.program_id(1)

        @pl.when(j == 0)
        def _():
            n, _, _ = _rms_fwd(h_ref[...], g_ref[...])
            n_sc[...] = n.astype(BF16)
            acc_sc[...] = jnp.zeros_like(acc_sc)

        n = n_sc[...]
        a = _dot_nt(n, wg_ref[...])
        b = _dot_nt(n, wu_ref[...])
        a_ref[...] = a
        b_ref[...] = b
        s = a * _sigmoid(a) * b
        acc_sc[...] += _dot(s.astype(BF16), wd_ref[...])

        @pl.when(j == nj - 1)
        def _():
            ho_ref[...] = h_ref[...] + 0.5 * acc_sc[...]

    return _call_with_exchange(
        name, body, (rows // tm, nj),
        [pl.BlockSpec((tm, d), lambda i, j: (i, 0)),
         pl.BlockSpec((1, d), lambda i, j: (0, 0)),
         pl.BlockSpec((tf, d), lambda i, j: (j, 0)),
         pl.BlockSpec((tf, d), lambda i, j: (j, 0)),
         pl.BlockSpec((tf, d), lambda i, j: (j, 0))],
        [pl.BlockSpec((tm, d), lambda i, j: (i, 0)),
         pl.BlockSpec((tm, tf), lambda i, j: (i, j)),
         pl.BlockSpec((tm, tf), lambda i, j: (i, j))],
        [jax.ShapeDtypeStruct((rows, d), F32),
         jax.ShapeDtypeStruct((rows, f), F32),
         jax.ShapeDtypeStruct((rows, f), F32)],
        [pltpu.VMEM((tm, d), BF16), pltpu.VMEM((tm, d), F32)],
        (h, g, wg, wu, wd), _params("arbitrary", "arbitrary"), exchange)


def _ffn_bwd(name, dh, h, g, a, b, wg, wu, wd, exchange=None):
    rows, d = h.shape
    f = wg.shape[0]
    tm, tf = _row_tile(rows), _half_tile(f)
    ni, nj = rows // tm, f // tf

    def body(dh_ref, h_ref, g_ref, a_ref, b_ref, wg_ref, wu_ref, wd_ref,
             dhi_ref, da_ref, db_ref, s_ref, n_ref, dhh_ref, dg_ref, dn_sc):
        i, j = pl.program_id(0), pl.program_id(1)

        @pl.when(j == 0)
        def _():
            n, _, _ = _rms_fwd(h_ref[...], g_ref[...])
            n_ref[...] = n.astype(BF16)
            dhh_ref[...] = (0.5 * dh_ref[...]).astype(BF16)
            dn_sc[...] = jnp.zeros_like(dn_sc)

        @pl.when((i == 0) & (j == 0))
        def _():
            dg_ref[...] = jnp.zeros_like(dg_ref)

        ds = _dot_nt(dhh_ref[...], wd_ref[...])
        av, bv = a_ref[...], b_ref[...]
        sig = _sigmoid(av)
        silu = av * sig
        s_ref[...] = (silu * bv).astype(BF16)
        db = (ds * silu).astype(BF16)
        da = (ds * bv * (sig * (1.0 + av * (1.0 - sig)))).astype(BF16)
        da_ref[...] = da
        db_ref[...] = db
        dn_sc[...] += _dot(da, wg_ref[...]) + _dot(db, wu_ref[...])

        @pl.when(j == nj - 1)
        def _():
            gv = g_ref[...]
            _, xhat, rstd = _rms_fwd(h_ref[...], gv)
            dx, dg = _rms_bwd(dn_sc[...], xhat, rstd, gv)
            dhi_ref[...] = dh_ref[...] + dx
            dg_ref[...] += dg

    return _call_with_exchange(
        name, body, (ni, nj),
        [pl.BlockSpec((tm, d), lambda i, j: (i, 0)),
         pl.BlockSpec((tm, d), lambda i, j: (i, 0)),
         pl.BlockSpec((1, d), lambda i, j: (0, 0)),
         pl.BlockSpec((tm, tf), lambda i, j: (i, j)),
         pl.BlockSpec((tm, tf), lambda i, j: (i, j)),
         pl.BlockSpec((tf, d), lambda i, j: (j, 0)),
         pl.BlockSpec((tf, d), lambda i, j: (j, 0)),
         pl.BlockSpec((tf, d), lambda i, j: (j, 0))],
        [pl.BlockSpec((tm, d), lambda i, j: (i, 0)),
         pl.BlockSpec((tm, tf), lambda i, j: (i, j)),
         pl.BlockSpec((tm, tf), lambda i, j: (i, j)),
         pl.BlockSpec((tm, tf), lambda i, j: (i, j)),
         pl.BlockSpec((tm, d), lambda i, j: (i, 0)),
         pl.BlockSpec((tm, d), lambda i, j: (i, 0)),
         pl.BlockSpec((1, d), lambda i, j: (0, 0))],
        [jax.ShapeDtypeStruct((rows, d), F32),
         jax.ShapeDtypeStruct((rows, f), BF16),
         jax.ShapeDtypeStruct((rows, f), BF16),
         jax.ShapeDtypeStruct((rows, f), BF16),
         jax.ShapeDtypeStruct((rows, d), BF16),
         jax.ShapeDtypeStruct((rows, d), BF16),
         jax.ShapeDtypeStruct((1, d), F32)],
        [pltpu.VMEM((tm, d), F32)],
        (dh, h, g, a, b, wg, wu, wd), _params("arbitrary", "arbitrary"), exchange)


def _mm_tn(name, a, b, exchange=None):
    k, m = a.shape
    n = b.shape[1]
    tk = _tall_tile(k, 3)
    tm = _half_tile(m) if m > 1024 else m
    tn = _half_tile(n) if n > 1024 else n
    nk = k // tk

    def body(a_ref, b_ref, o_ref, acc):
        kk = pl.program_id(2)

        @pl.when(kk == 0)
        def _():
            acc[...] = jnp.zeros_like(acc)

        acc[...] += _dot_tn(a_ref[...], b_ref[...])

        @pl.when(kk == nk - 1)
        def _():
            o_ref[...] = acc[...].astype(BF16)

    outs = _call_with_exchange(
        name, body, (m // tm, n // tn, nk),
        [pl.BlockSpec((tk, tm), lambda i, j, kk: (kk, i)),
         pl.BlockSpec((tk, tn), lambda i, j, kk: (kk, j))],
        [pl.BlockSpec((tm, tn), lambda i, j, kk: (i, j))],
        [jax.ShapeDtypeStruct((m, n), BF16)],
        [pltpu.VMEM((tm, tn), F32)],
        (a, b), _params("arbitrary", "arbitrary", "arbitrary"), exchange)
    return outs if exchange else outs[0]


def _norm_proj(name, h, g, w):
    rows, d = h.shape
    n = w.shape[1]
    split = 3 * GROUP
    tm = _row_tile(rows)

    def body(h_ref, g_ref, w_ref, qkv_ref, p_ref, n_ref):
        nv, _, _ = _rms_fwd(h_ref[...], g_ref[...])
        nb = nv.astype(BF16)
        n_ref[...] = nb
        qkv_ref[...] = _dot(nb, w_ref[:, :split]).astype(BF16)
        p_ref[...] = _dot(nb, w_ref[:, split:])

    return pl.pallas_call(
        body, name=name, grid=(rows // tm,),
        in_specs=[pl.BlockSpec((tm, d), lambda i: (i, 0)),
                  pl.BlockSpec((1, d), lambda i: (0, 0)),
                  pl.BlockSpec((d, n), lambda i: (0, 0))],
        out_specs=[pl.BlockSpec((tm, split), lambda i: (i, 0)),
                   pl.BlockSpec((tm, n - split), lambda i: (i, 0)),
                   pl.BlockSpec((tm, d), lambda i: (i, 0))],
        out_shape=[jax.ShapeDtypeStruct((rows, split), BF16), jax.ShapeDtypeStruct((rows, n - split), F32),
                   jax.ShapeDtypeStruct((rows, d), BF16)],
        compiler_params=_params("arbitrary"),
    )(h, g, w)


def _out_proj(name, h, sb, rw, w):
    rows, d = h.shape
    gw = sb.shape[1]
    tm = _row_tile(rows)

    def body(h_ref, sb_ref, rw_ref, w_ref, o_ref, mix_ref):
        mix_ref[:, :gw] = sb_ref[...].astype(BF16)
        mix_ref[:, gw:] = rw_ref[...].astype(BF16)
        o_ref[...] = h_ref[...] + _dot(mix_ref[...], w_ref[...])

    return pl.pallas_call(
        body, name=name, grid=(rows // tm,),
        in_specs=[pl.BlockSpec((tm, d), lambda i: (i, 0)),
                  pl.BlockSpec((tm, gw), lambda i: (i, 0)),
                  pl.BlockSpec((tm, gw), lambda i: (i, 0)),
                  pl.BlockSpec((2 * gw, d), lambda i: (0, 0))],
        out_specs=[pl.BlockSpec((tm, d), lambda i: (i, 0)),
                   pl.BlockSpec((tm, 2 * gw), lambda i: (i, 0))],
        out_shape=[jax.ShapeDtypeStruct((rows, d), F32), jax.ShapeDtypeStruct((rows, 2 * gw), BF16)],
        compiler_params=_params("arbitrary"),
    )(h, sb, rw, w)


def _out_proj_bwd(name, dh, w):
    rows, d = dh.shape
    k = w.shape[0]
    tm = _row_tile(rows)

    def body(dh_ref, w_ref, dsb_ref, drw_ref, dhb_ref):
        dhb = dh_ref[...].astype(BF16)
        dhb_ref[...] = dhb
        dsb_ref[...] = _dot_nt(dhb, w_ref[:GROUP, :]).astype(BF16)
        drw_ref[...] = _dot_nt(dhb, w_ref[GROUP:, :])

    return pl.pallas_call(
        body, name=name, grid=(rows // tm,),
        in_specs=[pl.BlockSpec((tm, d), lambda i: (i, 0)),
                  pl.BlockSpec((k, d), lambda i: (0, 0))],
        out_specs=[pl.BlockSpec((tm, GROUP), lambda i: (i, 0)),
                   pl.BlockSpec((tm, GROUP), lambda i: (i, 0)),
                   pl.BlockSpec((tm, d), lambda i: (i, 0))],
        out_shape=[jax.ShapeDtypeStruct((rows, GROUP), BF16), jax.ShapeDtypeStruct((rows, GROUP), F32),
                   jax.ShapeDtypeStruct((rows, d), BF16)],
        compiler_params=_params("arbitrary"),
    )(dh, w)


def _norm_proj_bwd(name, dproj, w, h, g, dh):
    rows, n = dproj.shape
    d = w.shape[0]
    tm = _row_tile(rows)

    def body(dp_ref, w_ref, h_ref, g_ref, dh_ref, o_ref, dg_ref):
        @pl.when(pl.program_id(0) == 0)
        def _():
            dg_ref[...] = jnp.zeros_like(dg_ref)

        dn = _dot_nt(dp_ref[...], w_ref[...])
        gv = g_ref[...]
        _, xhat, rstd = _rms_fwd(h_ref[...], gv)
        dx, dg = _rms_bwd(dn, xhat, rstd, gv)
        o_ref[...] = dh_ref[...] + dx
        dg_ref[...] += dg

    return pl.pallas_call(
        body, name=name, grid=(rows // tm,),
        in_specs=[pl.BlockSpec((tm, n), lambda i: (i, 0)),
                  pl.BlockSpec((d, n), lambda i: (0, 0)),
                  pl.BlockSpec((tm, d), lambda i: (i, 0)),
                  pl.BlockSpec((1, d), lambda i: (0, 0)),
                  pl.BlockSpec((tm, d), lambda i: (i, 0))],
        out_specs=[pl.BlockSpec((tm, d), lambda i: (i, 0)),
                   pl.BlockSpec((1, d), lambda i: (0, 0))],
        out_shape=[jax.ShapeDtypeStruct((rows, d), F32), jax.ShapeDtypeStruct((1, d), F32)],
        compiler_params=_params("arbitrary"),
    )(dproj, w, h, g, dh)


def _loss_head(name, h, g, tgt):
    rows, d = h.shape
    tm = _row_tile(rows)

    def body(h_ref, g_ref, t_ref, loss_ref, dh_ref, dg_ref):
        i = pl.program_id(0)

        @pl.when(i == 0)
        def _():
            loss_ref[...] = jnp.zeros_like(loss_ref)
            dg_ref[...] = jnp.zeros_like(dg_ref)

        gv = g_ref[...]
        y, xhat, rstd = _rms_fwd(h_ref[...], gv)
        row = i * tm + lax.broadcasted_iota(jnp.int32, (tm, 1), 0)
        diff = jnp.where(row >= ROW0, y - t_ref[...], 0.0)
        part = 0.5 * jnp.sum(jnp.sum(diff * diff, axis=-1, keepdims=True), axis=0, keepdims=True) / d
        loss_ref[...] += jnp.broadcast_to(part, loss_ref.shape)
        dx, dg = _rms_bwd(diff / d, xhat, rstd, gv)
        dh_ref[...] = dx
        dg_ref[...] += dg

    return pl.pallas_call(
        body, name=name, grid=(rows // tm,),
        in_specs=[pl.BlockSpec((tm, d), lambda i: (i, 0)),
                  pl.BlockSpec((1, d), lambda i: (0, 0)),
                  pl.BlockSpec((tm, d), lambda i: (i, 0))],
        out_specs=[pl.BlockSpec((8, 128), lambda i: (0, 0)),
                   pl.BlockSpec((tm, d), lambda i: (i, 0)),
                   pl.BlockSpec((1, d), lambda i: (0, 0))],
        out_shape=[jax.ShapeDtypeStruct((8, 128), F32),
                   jax.ShapeDtypeStruct((rows, d), F32),
                   jax.ShapeDtypeStruct((1, d), F32)],
        compiler_params=_params("arbitrary"),
    )(h, g, tgt)


def _sb_block(qb, kb, q0, jb, scale):
    bq, bk = qb.shape[0], kb.shape[0]
    z = _dot_nt(qb, kb) * scale
    qpos = q0 + lax.broadcasted_iota(jnp.int32, (bq, bk), 0)
    kpos = jb * bk + lax.broadcasted_iota(jnp.int32, (bq, bk), 1)
    valid = (kpos < qpos) & (kpos >= META_PAD)
    e = jnp.exp(-jnp.abs(z))
    log_keep = jnp.where(valid, -(jnp.maximum(z, 0.0) + jnp.log(1.0 + e)), 0.0)
    return z, valid, e, log_keep


def _tri2(n, cmp):
    r = lax.broadcasted_iota(jnp.int32, (2 * n, n), 0) % n
    c = lax.broadcasted_iota(jnp.int32, (2 * n, n), 1)
    return cmp(r, c).astype(BF16)


def _dot_split(x, t2):
    hi, lo = _split2(x)
    return _dot(jnp.concatenate([hi, lo], axis=1), t2)


ATT_HEADS = 128 // HEAD
ATT_CUT = -104.0
ATT_TILES = GROUP // 128


def _lanes(hh):
    return slice(hh * HEAD, (hh + 1) * HEAD)


def _first_and_last_step(grid):
    here = [pl.program_id(a) for a in range(len(grid))]
    first, last = here[0] == 0, here[0] == grid[0] - 1
    for a in range(1, len(grid)):
        first, last = first & (here[a] == 0), last & (here[a] == grid[a] - 1)
    return first, last


def _sb_fwd(name, qkv, shards=()):
    rows = qkv.shape[0]
    nh, dh = N_HEADS, HEAD
    bq, bk, hg = _row_tile(rows), ATT_BLOCK, ATT_HEADS
    per = bq // bk
    scale = dh ** -0.5
    ns = len(shards)
    grid = (nh // hg, rows // bq)

    def body(q_ref, k_ref, v_ref, *rest):
        o_ref, rt_ref, cnt_ref = rest[ns:ns + 3]
        if ns:
            first, last = _first_and_last_step(grid)
            start, finish = _gather_exchange(rest[:ns], rest[ns + 3:2 * ns + 3], rest[2 * ns + 3:])
            pl.when(first)(start)
        i = pl.program_id(1)
        after = _tri2(bk, lambda r, c: r > c)
        nkb = (i + 1) * per

        def live(state):
            n, carry = state
            top = jnp.max(carry[0][0])
            for hh in range(1, hg):
                top = jnp.maximum(top, jnp.max(carry[hh][0]))
            return (n < nkb) & (top >= ATT_CUT)

        def visit(carry, jb, r0):
            off = pl.multiple_of(jb * bk, bk)
            out = []
            for hh in range(hg):
                rest, acc = carry[hh]
                kb = k_ref[pl.ds(off, bk), _lanes(hh)]
                vb = v_ref[pl.ds(off, bk), _lanes(hh)]
                z, valid, _, log_keep = _sb_block(q_ref[r0:, _lanes(hh)], kb, i * bq + r0, jb, scale)
                log_rest = rest[r0:] + _dot_split(log_keep, after)
                attn = jnp.where(valid, jnp.exp(z + log_keep + log_rest), 0.0)
                new_rest = rest[r0:] + jnp.sum(log_keep, axis=-1, keepdims=True)
                new_acc = acc[r0:] + _dot(attn.astype(BF16), vb)
                if r0:
                    new_rest = jnp.concatenate([rest[:r0], new_rest], axis=0)
                    new_acc = jnp.concatenate([acc[:r0], new_acc], axis=0)
                out.append((new_rest, new_acc))
            return tuple(out)

        carry = tuple((jnp.zeros((bq, 1), F32), jnp.zeros((bq, dh), F32)) for _ in range(hg))
        for dgl in reversed(range(per)):
            carry = visit(carry, i * per + dgl, dgl * bk)
        n, res = lax.while_loop(live, lambda s: (s[0] + 1, visit(s[1], nkb - 1 - s[0], 0)), (jnp.int32(per), carry))
        for hh in range(hg):
            rt_ref[hh] = res[hh][0]
            o_ref[:, _lanes(hh)] = res[hh][1]
            cnt_ref[hh] = jnp.full((bq, 1), n, F32)
        if ns:
            pl.when(last)(finish)

    return pl.pallas_call(
        body, name=name, grid=grid,
        in_specs=[pl.BlockSpec((bq, 128), lambda h, i: (i, h)),
                  pl.BlockSpec((rows, 128), lambda h, i: (0, ATT_TILES + h)),
                  pl.BlockSpec((rows, 128), lambda h, i: (0, 2 * ATT_TILES + h))] + [HBM] * ns,
        out_specs=[pl.BlockSpec((bq, 128), lambda h, i: (i, h)),
                   pl.BlockSpec((hg, bq, 1), lambda h, i: (h, i, 0)),
                   pl.BlockSpec((hg, bq, 1), lambda h, i: (h, i, 0))] + [HBM] * ns,
        out_shape=[jax.ShapeDtypeStruct((rows, GROUP), F32), jax.ShapeDtypeStruct((nh, rows, 1), F32),
                   jax.ShapeDtypeStruct((nh, rows, 1), F32)]
        + [jax.ShapeDtypeStruct((N_CHIPS,) + s.shape, s.dtype) for s in shards],
        scratch_shapes=_gather_sems(ns) if ns else [],
        compiler_params=_params("arbitrary", "arbitrary"),
    )(qkv, qkv, qkv, *shards)


def _sb_bwd(name, qkv, rt, cnt, do, parts=()):
    rows = qkv.shape[0]
    nh, dh = N_HEADS, HEAD
    bq, bk, hg = _row_tile(rows), ATT_BLOCK, ATT_HEADS
    per = bq // bk
    scale = dh ** -0.5
    ns = len(parts)
    grid = (nh // hg, rows // bq)

    def body(q_ref, k_ref, v_ref, rt_ref, cnt_ref, do_ref, *rest):
        dq_ref, dk_ref, dv_ref = rest[ns:ns + 3]
        if ns:
            at_first, at_last = _first_and_last_step(grid)
            start, finish = _reduce_exchange(rest[:ns], rest[ns + 3:2 * ns + 3], rest[2 * ns + 3:3 * ns + 3],
                                             rest[3 * ns + 3:])
            pl.when(at_first)(start)
        i = pl.program_id(1)

        @pl.when(i == 0)
        def _():
            dk_ref[...] = jnp.zeros_like(dk_ref)
            dv_ref[...] = jnp.zeros_like(dv_ref)

        upto = _tri2(bk, lambda r, c: r <= c)
        before = _tri2(bk, lambda r, c: r < c)
        nkb = (i + 1) * per
        first = nkb - jnp.max(cnt_ref[0]).astype(jnp.int32)

        def visit(carry, jb, r0):
            off = pl.multiple_of(jb * bk, bk)
            out = []
            for hh in range(hg):
                keep_sum, g_sum, dq = carry[hh]
                qb, dob = q_ref[r0:, _lanes(hh)], do_ref[r0:, _lanes(hh)]
                kb = k_ref[pl.ds(off, bk), _lanes(hh)]
                vb = v_ref[pl.ds(off, bk), _lanes(hh)]
                z, valid, e, log_keep = _sb_block(qb, kb, i * bq + r0, jb, scale)
                log_rest = rt_ref[hh, r0:, :] - keep_sum[r0:] - _dot_split(log_keep, upto)
                attn = jnp.where(valid, jnp.exp(z + log_keep + log_rest), 0.0)
                g = attn * _dot_nt(dob, vb)
                g_before = g_sum[r0:] + _dot_split(g, before)
                inv = 1.0 / (1.0 + e)
                sig = jnp.where(z >= 0, inv, e * inv)
                dz = (jnp.where(valid, g * (1.0 - sig) - g_before * sig, 0.0) * scale).astype(BF16)
                dk_ref[pl.ds(off, bk), _lanes(hh)] += _dot_tn(dz, qb)
                dv_ref[pl.ds(off, bk), _lanes(hh)] += _dot_tn(attn.astype(BF16), dob)
                new = (keep_sum[r0:] + jnp.sum(log_keep, axis=-1, keepdims=True),
                       g_sum[r0:] + jnp.sum(g, axis=-1, keepdims=True),
                       dq[r0:] + _dot(dz, kb))
                if r0:
                    new = tuple(jnp.concatenate([old[:r0], x], axis=0) for old, x in zip(carry[hh], new))
                out.append(new)
            return tuple(out)

        zero = jnp.zeros((bq, 1), F32)
        res = lax.fori_loop(first, nkb - per, lambda jb, c: visit(c, jb, 0),
                            tuple((zero, zero, jnp.zeros((bq, dh), F32)) for _ in range(hg)))
        for dgl in range(per):
            res = visit(res, i * per + dgl, dgl * bk)
        for hh in range(hg):
            dq_ref[:, _lanes(hh)] = res[hh][2]
        if ns:
            pl.when(at_last)(finish)

    return pl.pallas_call(
        body, name=name, grid=grid,
        in_specs=[pl.BlockSpec((bq, 128), lambda h, i: (i, h)),
                  pl.BlockSpec((rows, 128), lambda h, i: (0, ATT_TILES + h)),
                  pl.BlockSpec((rows, 128), lambda h, i: (0, 2 * ATT_TILES + h)),
                  pl.BlockSpec((hg, bq, 1), lambda h, i: (h, i, 0)),
                  pl.BlockSpec((hg, bq, 1), lambda h, i: (h, i, 0)),
                  pl.BlockSpec((bq, 128), lambda h, i: (i, h))] + [HBM] * ns,
        out_specs=[pl.BlockSpec((bq, 128), lambda h, i: (i, h)),
                   pl.BlockSpec((rows, 128), lambda h, i: (0, h)),
                   pl.BlockSpec((rows, 128), lambda h, i: (0, h))] + [HBM] * (2 * ns),
        out_shape=[jax.ShapeDtypeStruct((rows, GROUP), F32)] * 3
        + [jax.ShapeDtypeStruct(s.shape, s.dtype) for s in parts] * 2,
        scratch_shapes=_reduce_sems(ns) if ns else [],
        compiler_params=_params("arbitrary", "arbitrary"),
    )(qkv, qkv, qkv, rt, cnt, do, *parts)


def _head_sum(x, ones_bd):
    return _dot_split(x, ones_bd)


def _rwkv_pre(p, p_prev, mu, w0, a0, k_k, k_a, w_up, a_up, g_up, ones_bd):
    xs = p + (p_prev - p) * mu
    r = xs[:, :GROUP]
    k0 = xs[:, GROUP:2 * GROUP]
    v = xs[:, 2 * GROUP:3 * GROUP]
    lo = xs[:, 3 * GROUP:]
    wa = w0 + _dot(jnp.tanh(lo).astype(BF16), w_up.astype(BF16))
    w = -(jnp.maximum(-wa, 0.0) + jnp.log(1.0 + jnp.exp(-jnp.abs(wa)))) - 0.5
    log_decay = -jnp.exp(w)
    alpha = _sigmoid(a0 + _dot(lo.astype(BF16), a_up.astype(BF16)))
    gate = _dot(_sigmoid(lo).astype(BF16), g_up.astype(BF16))
    kk = k0 * k_k
    kk = kk * lax.rsqrt(jnp.maximum(_head_sum(kk * kk, ones_bd), 1e-24))
    k = k0 * (1.0 + (alpha - 1.0) * k_a)
    return r, log_decay, k, v, -kk, kk * alpha, gate


def _rwkv_post(y, r, k, v, gate, lnx_w, lnx_b, r_k, ones_bd):
    mean = _head_sum(y, ones_bd) * (1.0 / HEAD)
    yc = y - mean
    var = _head_sum(yc * yc, ones_bd) * (1.0 / HEAD)
    yn = yc * lax.rsqrt(var + LNX_EPS) * lnx_w + lnx_b
    bonus = _head_sum(r * k * r_k, ones_bd) * v
    return (yn + bonus) * gate


_PRE_VEC = 5
_PRE_MAT = 3


def _heads(x):
    return jnp.stack([x[:, _lanes(h)] for h in range(N_HEADS)])


def _unheads(x):
    return jnp.concatenate([x[h] for h in range(N_HEADS)], axis=1)


def _edge_spec(tm, width):
    return pl.BlockSpec((8, width), lambda i: (jnp.maximum(i * (tm // 8) - 1, 0), 0))


def _previous_rows(p_ref, edge_ref):
    p = p_ref[...]
    edge = jnp.where(pl.program_id(0) == 0, 0.0, edge_ref[7:8, :])
    row = lax.broadcasted_iota(jnp.int32, (p.shape[0], 1), 0)
    return jnp.where(row == 0, edge, pltpu.roll(p, 1, axis=0))


def _rwkv_pre_fwd(name, p, vecs, mats, ones_bd):
    rows = p.shape[0]
    tm = _row_tile(rows)
    row_spec = lambda w: pl.BlockSpec((tm, w), lambda i: (i, 0))
    full = lambda a: pl.BlockSpec(a.shape, lambda i: (0,) * a.ndim)

    def body(p_ref, edge_ref, *refs):
        ins = [r[...] for r in refs[:_PRE_VEC + _PRE_MAT + 1]]
        outs = refs[_PRE_VEC + _PRE_MAT + 1:]
        for o_ref, val in zip(outs, _rwkv_pre(p_ref[...], _previous_rows(p_ref, edge_ref), *ins)):
            o_ref[...] = val

    return pl.pallas_call(
        body, name=name, grid=(rows // tm,),
        in_specs=[row_spec(RW_COLS), _edge_spec(tm, RW_COLS)] + [full(a) for a in (*vecs, *mats, ones_bd)],
        out_specs=[row_spec(GROUP)] * 7,
        out_shape=[jax.ShapeDtypeStruct((rows, GROUP), F32)] * 7,
        compiler_params=_params("arbitrary"),
    )(p, p, *vecs, *mats, ones_bd)


def _rwkv_pre_bwd(name, p, vecs, mats, ones_bd, cts_scan, ct_gate, cts_b):
    rows = p.shape[0]
    tm = _row_tile(rows)
    n_par = _PRE_VEC + _PRE_MAT
    row_spec = lambda w: pl.BlockSpec((tm, w), lambda i: (i, 0))
    full = lambda a: pl.BlockSpec(a.shape, lambda i: (0,) * a.ndim)

    def body(*refs):
        p_ref, edge_ref = refs[0], refs[1]
        par = [r[...] for r in refs[2:2 + n_par]]
        ones = refs[2 + n_par][...]
        cta = [r[...] for r in refs[3 + n_par:10 + n_par]]
        ctb = [r[...] for r in refs[10 + n_par:13 + n_par]]
        outs = refs[13 + n_par:]
        ct = (cta[0] + ctb[0], cta[1], cta[2] + ctb[1], cta[3] + ctb[2], cta[4], cta[5], cta[6])
        _, vjp = jax.vjp(lambda pv, ppv, *pr: _rwkv_pre(pv, ppv, *pr, ones),
                         p_ref[...], _previous_rows(p_ref, edge_ref), *par)
        grads = vjp(ct)
        outs[0][...] = grads[0]
        outs[1][...] = grads[1]

        @pl.when(pl.program_id(0) == 0)
        def _():
            for o_ref in outs[2:]:
                o_ref[...] = jnp.zeros_like(o_ref)

        for o_ref, gval in zip(outs[2:], grads[2:]):
            o_ref[...] += gval

    par_arrays = (*vecs, *mats)
    return pl.pallas_call(
        body, name=name, grid=(rows // tm,),
        in_specs=([row_spec(RW_COLS), _edge_spec(tm, RW_COLS)] + [full(a) for a in (*par_arrays, ones_bd)]
                  + [row_spec(GROUP)] * 10),
        out_specs=[row_spec(RW_COLS)] * 2 + [full(a) for a in par_arrays],
        out_shape=([jax.ShapeDtypeStruct((rows, RW_COLS), F32)] * 2
                   + [jax.ShapeDtypeStruct(a.shape, F32) for a in par_arrays]),
        compiler_params=_params("arbitrary"),
    )(p, p, *par_arrays, ones_bd, *cts_scan, ct_gate, *cts_b)


def _rwkv_post_fwd(name, y, r, k, v, gate, vecs, ones_bd):
    rows = r.shape[0]
    tm = _row_tile(rows)
    row_spec = pl.BlockSpec((tm, GROUP), lambda i: (i, 0))
    full = lambda a: pl.BlockSpec(a.shape, lambda i: (0,) * a.ndim)

    def body(*refs):
        refs[-1][...] = _rwkv_post(*(r_[...] for r_ in refs[:-1]))

    return pl.pallas_call(
        body, name=name, grid=(rows // tm,),
        in_specs=[row_spec] * 5 + [full(a) for a in (*vecs, ones_bd)],
        out_specs=row_spec,
        out_shape=jax.ShapeDtypeStruct((rows, GROUP), F32),
        compiler_params=_params("arbitrary"),
    )(y, r, k, v, gate, *vecs, ones_bd)


def _rwkv_post_bwd(name, y, r, k, v, gate, vecs, ones_bd, dout):
    rows = r.shape[0]
    tm = _row_tile(rows)
    row_spec = pl.BlockSpec((tm, GROUP), lambda i: (i, 0))
    full = lambda a: pl.BlockSpec(a.shape, lambda i: (0,) * a.ndim)

    def body(*refs):
        vals = [r_[...] for r_ in refs[:8]]
        ones = refs[8][...]
        dout_v = refs[9][...]
        outs = refs[10:]
        _, vjp = jax.vjp(lambda *a: _rwkv_post(*a, ones), *vals)
        grads = vjp(dout_v)
        for o_ref, gval in zip(outs[:5], grads[:5]):
            o_ref[...] = gval

        @pl.when(pl.program_id(0) == 0)
        def _():
            for o_ref in outs[5:]:
                o_ref[...] = jnp.zeros_like(o_ref)

        for o_ref, gval in zip(outs[5:], grads[5:]):
            o_ref[...] += gval

    return pl.pallas_call(
        body, name=name, grid=(rows // tm,),
        in_specs=[row_spec] * 5 + [full(a) for a in (*vecs, ones_bd)] + [row_spec],
        out_specs=[row_spec] * 5 + [full(a) for a in vecs],
        out_shape=[jax.ShapeDtypeStruct((rows, GROUP), F32)] * 5 + [jax.ShapeDtypeStruct(a.shape, F32) for a in vecs],
        compiler_params=_params("arbitrary"),
    )(y, r, k, v, gate, *vecs, ones_bd, dout)


_NN = (((2,), (1,)), ((0,), (0,)))
_NT = (((2,), (2,)), ((0,), (0,)))
_TN = (((1,), (1,)), ((0,), (0,)))


_BWD_FORMS = {"nn": (("nt", False), ("tn", False)),
              "nt": (("nn", False), ("tn", True)),
              "tn": (("nt", True), ("nn", False))}
_DIMS = {"nn": _NN, "nt": _NT, "tn": _TN}


def _bdot(a, b, form):
    return lax.dot_general(a.astype(BF16), b.astype(BF16), _DIMS[form], preferred_element_type=F32)


@functools.partial(jax.custom_vjp, nondiff_argnums=(2,))
def _bmm(a, b, form):
    return _bdot(a, b, form)


def _bmm_fwd(a, b, form):
    return _bdot(a, b, form), (a.astype(BF16), b.astype(BF16))


def _bmm_bwd(form, res, dc):
    a, b = res
    (fa, swap_a), (fb, swap_b) = _BWD_FORMS[form]
    da = _bdot(b, dc, fa) if swap_a else _bdot(dc, b, fa)
    db = _bdot(dc, a, fb) if swap_b else _bdot(a, dc, fb)
    return da, db


_bmm.defvjp(_bmm_fwd, _bmm_bwd)


@jax.custom_vjp
def _cumsum_steps(x):
    return _tri_apply(x, lambda r, c: r >= c)


def _tri_apply(x, cmp):
    nh, c, _ = x.shape
    tri = cmp(lax.broadcasted_iota(jnp.int32, (c, c), 0), lax.broadcasted_iota(jnp.int32, (c, c), 1))
    tri = jnp.broadcast_to(tri.astype(BF16)[None], (nh, c, c))
    hi, lo = _split2(x)
    return (lax.dot_general(tri, hi, _NN, preferred_element_type=F32)
            + lax.dot_general(tri, lo, _NN, preferred_element_type=F32))


_cumsum_steps.defvjp(lambda x: (_cumsum_steps(x), None), lambda _, d: (_tri_apply(d, lambda r, c: r <= c),))


@jax.custom_vjp
def _neumann(n_mat):
    c = n_mat.shape[1]
    inv, power, span = n_mat, _bmm(n_mat, n_mat, "nn"), 2
    while span < c:
        both = _bmm(jnp.concatenate([power, inv], axis=1), power, "nn")
        inv = inv + power + both[:, c:]
        power = both[:, :c]
        span *= 2
    return inv


def _neumann_fwd(n_mat):
    inv = _neumann(n_mat)
    return inv, inv


def _neumann_bwd(inv, d):
    left = d + _bmm(inv, d, "tn")
    return (left + _bmm(left, inv, "nt"),)


_neumann.defvjp(_neumann_fwd, _neumann_bwd)


def _chunk(state, r, log_w, k, v, a, b):
    nh, c, _ = r.shape
    row = lax.broadcasted_iota(jnp.int32, (c, c), 0)
    col = lax.broadcasted_iota(jnp.int32, (c, c), 1)
    cum = _cumsum_steps(log_w)
    mid = cum[:, c // 2 - 1:c // 2, :]
    a_t = a * jnp.exp(cum - log_w - mid)
    r_t = r * jnp.exp(cum - mid)
    back = jnp.exp(mid - cum)
    b_t = b * back
    k_t = k * back
    strict, incl = (row > col)[None], (row >= col)[None]
    ar = jnp.concatenate([a_t, r_t], axis=1)
    on_b = _bmm(ar, b_t, "nt")
    on_k = _bmm(ar, k_t, "nt")
    n_mat = jnp.where(strict, on_b[:, :c], 0.0)
    p_mat = jnp.where(incl, on_b[:, c:], 0.0)
    m_mat = jnp.where(strict, on_k[:, :c], 0.0)
    q_mat = jnp.where(incl, on_k[:, c:], 0.0)
    inv = _neumann(n_mat)
    s_mid = state * jnp.swapaxes(jnp.exp(mid), 1, 2)
    x = _bmm(jnp.concatenate([a_t, m_mat], axis=2), jnp.concatenate([s_mid, v], axis=1), "nn")
    u = x + _bmm(inv, x, "nn")
    y = _bmm(jnp.concatenate([r_t, p_mat, q_mat], axis=2), jnp.concatenate([s_mid, u, v], axis=1), "nn")
    grown = _bmm(jnp.concatenate([b_t, k_t], axis=1), jnp.concatenate([u, v], axis=1), "tn")
    s_new = (s_mid + grown) * jnp.swapaxes(jnp.exp(cum[:, c - 1:c, :] - mid), 1, 2)
    return y, s_new


def _scan_fwd(name, ops):
    rows = ops[0].shape[0]
    nh, dh = N_HEADS, HEAD
    nc, per = rows // CHUNK, SCAN_CHUNKS
    spec = pl.BlockSpec((per * CHUNK, GROUP), lambda c: (c, 0))

    def body(r_ref, w_ref, k_ref, v_ref, a_ref, b_ref, y_ref, st_ref, state):
        @pl.when(pl.program_id(0) == 0)
        def _():
            state[...] = jnp.zeros_like(state)

        s = state[...]
        for u in range(per):
            at = slice(u * CHUNK, (u + 1) * CHUNK)
            st_ref[u] = s
            y, s = _chunk(s, *(_heads(ref[at, :]) for ref in (r_ref, w_ref, k_ref, v_ref, a_ref, b_ref)))
            y_ref[at, :] = _unheads(y)
        state[...] = s

    return pl.pallas_call(
        body, name=name, grid=(nc // per,),
        in_specs=[spec] * 6,
        out_specs=[spec, pl.BlockSpec((per, nh, dh, dh), lambda c: (c, 0, 0, 0))],
        out_shape=[jax.ShapeDtypeStruct((rows, GROUP), F32), jax.ShapeDtypeStruct((nc, nh, dh, dh), F32)],
        scratch_shapes=[pltpu.VMEM((nh, dh, dh), F32)],
        compiler_params=_params("arbitrary"),
    )(*ops)


def _scan_bwd(name, ops, states, dy):
    rows = ops[0].shape[0]
    nh, dh = N_HEADS, HEAD
    nc, per = rows // CHUNK, SCAN_CHUNKS
    steps = nc // per
    spec = pl.BlockSpec((per * CHUNK, GROUP), lambda c: (steps - 1 - c, 0))

    def body(r_ref, w_ref, k_ref, v_ref, a_ref, b_ref, st_ref, dy_ref, *rest):
        outs, dstate = rest[:6], rest[6]

        @pl.when(pl.program_id(0) == 0)
        def _():
            dstate[...] = jnp.zeros_like(dstate)

        ds = dstate[...]
        for u in reversed(range(per)):
            at = slice(u * CHUNK, (u + 1) * CHUNK)
            _, vjp = jax.vjp(_chunk, st_ref[u],
                             *(_heads(ref[at, :]) for ref in (r_ref, w_ref, k_ref, v_ref, a_ref, b_ref)))
            grads = vjp((_heads(dy_ref[at, :]), ds))
            ds = grads[0]
            for o_ref, gval in zip(outs, grads[1:]):
                o_ref[at, :] = _unheads(gval)
        dstate[...] = ds

    return pl.pallas_call(
        body, name=name, grid=(steps,),
        in_specs=[spec] * 6 + [pl.BlockSpec((per, nh, dh, dh), lambda c: (steps - 1 - c, 0, 0, 0)), spec],
        out_specs=[spec] * 6,
        out_shape=[jax.ShapeDtypeStruct((rows, GROUP), F32)] * 6,
        scratch_shapes=[pltpu.VMEM((nh, dh, dh), F32)],
        compiler_params=_params("arbitrary"),
    )(*ops, states, dy)


def _shift_up(x):
    return jnp.concatenate([x[1:], jnp.zeros((1, x.shape[1]), x.dtype)], axis=0)


def _pad_rows(x, rows):
    return jnp.concatenate([x, jnp.zeros((rows - x.shape[0],) + x.shape[1:], x.dtype)], axis=0)


def _pad_cols(x, cols):
    return jnp.concatenate([x, jnp.zeros(x.shape[:-1] + (cols - x.shape[-1],), x.dtype)], axis=-1)


def _lora_pad(w_up, a_up, g_up):
    z = lambda n: jnp.zeros((n, GROUP), F32)
    return (jnp.concatenate([w_up, z(LORA_PAD - LORA_W)], 0),
            jnp.concatenate([z(LORA_W), a_up, z(LORA_PAD - LORA_W - LORA_A)], 0),
            jnp.concatenate([z(LORA_W + LORA_A), g_up, z(LORA_PAD - LORA_W - LORA_A - LORA_G)], 0))


MID = ['w_in']
LATE = ['ffn2_w_gate', 'ffn2_w_up', 'ffn2_w_down', 'w_out']


def _local_step(x, tgt, w, late=None):
    d = x.shape[1]
    zeros = jnp.zeros((META_PAD, d), F32)
    h0 = jnp.concatenate([zeros, w["meta_tokens"], x], axis=0)
    tgt_p = jnp.concatenate([jnp.zeros((ROW0, d), F32), tgt], axis=0)
    ones_bd = ((lax.broadcasted_iota(jnp.int32, (2 * GROUP, GROUP), 0) % GROUP) // HEAD
               == lax.broadcasted_iota(jnp.int32, (2 * GROUP, GROUP), 1) // HEAD).astype(BF16)
    pre_vecs = (_pad_cols(w["rwkv_mu"], RW_COLS), w["rwkv_w0"], w["rwkv_a0"], w["rwkv_k_k"], w["rwkv_k_a"])
    pre_mats = _lora_pad(w["rwkv_w_up"], w["rwkv_a_up"], w["rwkv_g_up"])
    post_vecs = (w["rwkv_lnx_w"], w["rwkv_lnx_b"], w["rwkv_r_k"].reshape(1, GROUP))

    h1, a1, b1, *gathered = _ffn_fwd("ffn1_fwd", h0, w["ffn1_norm"], w["ffn1_w_gate"], w["ffn1_w_up"],
                                     w["ffn1_w_down"], late and ("gather", late.shards["mid"]))
    if late is not None:
        w = {**w, **late.join("mid", gathered)}
    w_in = _pad_cols(w["w_in"], IN_COLS_PAD)
    qkv, p, n2 = _norm_proj("in_proj", h1, w["mix_norm"], w_in)
    sb, rest_total, visited, *gathered = _sb_fwd("sb_fwd", qkv, late.shards["late"] if late else ())
    if late is not None:
        w = {**w, **late.join("late", gathered)}
    pre = _rwkv_pre_fwd("rwkv_pre_fwd", p, pre_vecs, pre_mats, ones_bd)
    scan_ops, token_ops = pre[:6], (pre[0], pre[2], pre[3], pre[6])
    y, states = _scan_fwd("rwkv_scan_fwd", scan_ops)
    rw = _rwkv_post_fwd("rwkv_post_fwd", y, *token_ops, post_vecs, ones_bd)
    h2, mix = _out_proj("out_proj", h1, sb, rw, w["w_out"])
    h3, a2, b2 = _ffn_fwd("ffn2_fwd", h2, w["ffn2_norm"], w["ffn2_w_gate"], w["ffn2_w_up"], w["ffn2_w_down"])
    loss8, dh3, g_final = _loss_head("loss_head", h3, w["final_norm"].reshape(1, d), tgt_p)

    g = {"final_norm": g_final.reshape(d)}
    dh2, da2, db2, s2, n3, dhh3, g["ffn2_norm"] = _ffn_bwd(
        "ffn2_bwd", dh3, h2, w["ffn2_norm"], a2, b2, w["ffn2_w_gate"], w["ffn2_w_up"], w["ffn2_w_down"])
    g["ffn2_w_gate"] = _mm_tn("ffn2_dgate", da2, n3)
    g["ffn2_w_up"] = _mm_tn("ffn2_dup", db2, n3)
    g["ffn2_w_down"] = _mm_tn("ffn2_ddown", s2, dhh3)
    dsb, drw, dh2b = _out_proj_bwd("out_proj_bwd", dh2, w["w_out"])
    g["w_out"] = _mm_tn("out_proj_dw", mix, dh2b)
    dq, dk, dv, *reduced_late = _sb_bwd("sb_bwd", qkv, rest_total, visited, dsb, late.parts("late", g) if late else ())
    post_g = _rwkv_post_bwd("rwkv_post_bwd", y, *token_ops, post_vecs, ones_bd, drw)
    g["rwkv_lnx_w"], g["rwkv_lnx_b"] = post_g[5], post_g[6]
    g["rwkv_r_k"] = post_g[7].reshape(1, N_HEADS, HEAD)
    scan_g = _scan_bwd("rwkv_scan_bwd", scan_ops, states, post_g[0])
    pre_g = _rwkv_pre_bwd("rwkv_pre_bwd", p, pre_vecs, pre_mats, ones_bd, scan_g, post_g[4], post_g[1:4])
    g["rwkv_mu"] = pre_g[2][:, :w["rwkv_mu"].shape[1]]
    g["rwkv_w0"], g["rwkv_a0"], g["rwkv_k_k"], g["rwkv_k_a"] = pre_g[3:7]
    g["rwkv_w_up"] = pre_g[7][:LORA_W]
    g["rwkv_a_up"] = pre_g[8][LORA_W:LORA_W + LORA_A]
    g["rwkv_g_up"] = pre_g[9][LORA_W + LORA_A:LORA_W + LORA_A + LORA_G]
    dp = pre_g[0] + _shift_up(pre_g[1])
    live = (jnp.arange(h0.shape[0]) >= META_PAD)[:, None]
    dproj = jnp.where(live, jnp.concatenate([dq, dk, dv, dp], axis=1), 0.0).astype(BF16)
    g["w_in"] = _mm_tn("in_proj_dw", n2, dproj)[:, :w["w_in"].shape[1]]
    dh1, g["mix_norm"] = _norm_proj_bwd("in_proj_bwd", dproj, w_in, h1, w["mix_norm"], dh2)
    dh0, da1, db1, s1, n1, dhh1, g["ffn1_norm"], *reduced_mid = _ffn_bwd(
        "ffn1_bwd", dh1, h0, w["ffn1_norm"], a1, b1, w["ffn1_w_gate"], w["ffn1_w_up"], w["ffn1_w_down"],
        late and ("reduce", late.parts("mid", g)))
    g["meta_tokens"] = dh0[META_PAD:ROW0]
    reduced = {"mid": reduced_mid, "late": reduced_late}
    if late is None:
        g["ffn1_w_gate"] = _mm_tn("ffn1_dgate", da1, n1)
        g["ffn1_w_up"] = _mm_tn("ffn1_dup", db1, n1)
        g["ffn1_w_down"] = _mm_tn("ffn1_ddown", s1, dhh1)
    else:
        g["ffn1_w_gate"], reduced["small"] = _mm_tn("ffn1_dgate", da1, n1, ("all_reduce", [late.small(g)]))
        g["ffn1_w_up"], *reduced["gate"] = _mm_tn("ffn1_dup", db1, n1, ("reduce", late.parts("gate", g)))
        g["ffn1_w_down"], *reduced["up"] = _mm_tn("ffn1_ddown", s1, dhh1, ("reduce", late.parts("up", g)))
    return loss8[0, 0], dh0[ROW0:], g, reduced


N_CHIPS = 4
N_DEV = 8
HBM = pl.BlockSpec(memory_space=pltpu.HBM)


def _place():
    return lax.axis_index("x"), lax.axis_index("y"), lax.axis_index("c")


def _other_chips(x, y):
    return [(1 - x, y), (x, 1 - y), (1 - x, 1 - y)]


def _gather_sems(n):
    return [pltpu.SemaphoreType.DMA((3 * n,)), pltpu.SemaphoreType.DMA((3 * n,)), pltpu.SemaphoreType.DMA((n,)),
            pltpu.SemaphoreType.DMA((3 * n,)), pltpu.SemaphoreType.DMA((3 * n,))]


def _gather_exchange(ins, outs, sems):
    n = len(ins)
    half = [r.shape[0] // 2 for r in ins]
    send, recv, local, d2d_send, d2d_recv = sems
    x, y, c = _place()
    me = 2 * x + y
    chips = _other_chips(x, y)

    def rows_of(k, h):
        return pl.ds(pl.multiple_of(h * half[k], 8), half[k])

    def own(k):
        return pltpu.make_async_copy(ins[k], outs[k].at[me], local.at[k])

    def copy(j, k, slot):
        return pltpu.make_async_remote_copy(
            src_ref=ins[k].at[rows_of(k, c)], dst_ref=outs[k].at[slot, rows_of(k, c)],
            send_sem=send.at[j * n + k], recv_sem=recv.at[j * n + k],
            device_id=(chips[j][0], chips[j][1], c), device_id_type=MESH)

    def passed(j, k, h):
        slot = 2 * chips[j][0] + chips[j][1]
        return pltpu.make_async_remote_copy(
            src_ref=outs[k].at[slot, rows_of(k, h)], dst_ref=outs[k].at[slot, rows_of(k, h)],
            send_sem=d2d_send.at[j * n + k], recv_sem=d2d_recv.at[j * n + k],
            device_id=(x, y, 1 - c), device_id_type=MESH)

    def start():
        for k in range(n):
            own(k).start()
        for j in range(3):
            for k in range(n):
                copy(j, k, me).start()

    def finish():
        for j in range(3):
            for k in range(n):
                copy(j, k, 2 * chips[j][0] + chips[j][1]).wait_recv()
                passed(j, k, c).start()
        for j in range(3):
            for k in range(n):
                passed(j, k, 1 - c).wait_recv()
        for j in range(3):
            for k in range(n):
                copy(j, k, me).wait_send()
                passed(j, k, c).wait_send()
        for k in range(n):
            own(k).wait()

    return start, finish


def _gather_shards(name, shards):
    n = len(shards)

    def body(*refs):
        start, finish = _gather_exchange(refs[:n], refs[n:2 * n], refs[2 * n:])
        start()
        finish()

    return pl.pallas_call(
        body, name=name,
        in_specs=[HBM] * n, out_specs=[HBM] * n,
        out_shape=[jax.ShapeDtypeStruct((N_CHIPS,) + s.shape, s.dtype) for s in shards],
        scratch_shapes=_gather_sems(n),
    )(*shards)


def _pair_exchange(name, parts):
    n = len(parts)
    half = [s.shape[1] // 2 for s in parts]

    def body(*refs):
        ins, outs = refs[:n], refs[n:2 * n]
        send, recv = refs[2 * n:]
        x, y, c = _place()

        def copy(k):
            rows = pl.ds(pl.multiple_of((1 - c) * half[k], 8), half[k])
            return pltpu.make_async_remote_copy(
                src_ref=ins[k].at[:, rows], dst_ref=outs[k], send_sem=send.at[k], recv_sem=recv.at[k],
                device_id=(x, y, 1 - c), device_id_type=MESH)

        for k in range(n):
            copy(k).start()
        for k in range(n):
            copy(k).wait_recv()
        for k in range(n):
            copy(k).wait_send()

    return pl.pallas_call(
        body, name=name,
        in_specs=[HBM] * n, out_specs=[HBM] * n,
        out_shape=[jax.ShapeDtypeStruct((s.shape[0], s.shape[1] // 2, s.shape[2]), s.dtype) for s in parts],
        scratch_shapes=[pltpu.SemaphoreType.DMA((n,)), pltpu.SemaphoreType.DMA((n,))],
    )(*parts)


def _pair_add(name, part, other):
    nch, rows, cols = part.shape
    half = rows // 2

    def body(p_ref, o_ref, out_ref):
        c = lax.axis_index("c")
        mine = p_ref[0, pl.ds(pl.multiple_of(c * half, 16), half), :]
        out_ref[0] = (mine.astype(F32) + o_ref[0].astype(F32)).astype(out_ref.dtype)

    return pl.pallas_call(
        body, name=name, grid=(nch,),
        in_specs=[pl.BlockSpec((1, rows, cols), lambda j: (j, 0, 0)),
                  pl.BlockSpec((1, half, cols), lambda j: (j, 0, 0))],
        out_specs=pl.BlockSpec((1, half, cols), lambda j: (j, 0, 0)),
        out_shape=jax.ShapeDtypeStruct((nch, half, cols), part.dtype),
        compiler_params=_params("arbitrary"),
    )(part, other)


def _reduce_sems(n):
    return [pltpu.SemaphoreType.DMA((3 * n,)), pltpu.SemaphoreType.DMA((3 * n,)), pltpu.SemaphoreType.DMA((n,)),
            pltpu.SemaphoreType.DMA((n,)), pltpu.SemaphoreType.DMA((n,))]


def _reduce_exchange(ins, got, sib, sems):
    n = len(ins)
    send, recv, local, d2d_send, d2d_recv = sems
    x, y, c = _place()
    me = 2 * x + y
    chips = _other_chips(x, y)

    def own(k):
        return pltpu.make_async_copy(ins[k].at[me], got[k].at[me], local.at[k])

    def copy(j, k, shard, slot):
        return pltpu.make_async_remote_copy(
            src_ref=ins[k].at[shard], dst_ref=got[k].at[slot], send_sem=send.at[j * n + k],
            recv_sem=recv.at[j * n + k], device_id=(chips[j][0], chips[j][1], c), device_id_type=MESH)

    def swap(k):
        return pltpu.make_async_remote_copy(
            src_ref=got[k], dst_ref=sib[k], send_sem=d2d_send.at[k], recv_sem=d2d_recv.at[k],
            device_id=(x, y, 1 - c), device_id_type=MESH)

    def start():
        for k in range(n):
            own(k).start()
        for j in range(3):
            for k in range(n):
                copy(j, k, 2 * chips[j][0] + chips[j][1], me).start()

    def finish():
        for k in range(n):
            own(k).wait()
            for j in range(3):
                copy(j, k, me, 2 * chips[j][0] + chips[j][1]).wait_recv()
            swap(k).start()
        for k in range(n):
            swap(k).wait_recv()
        for j in range(3):
            for k in range(n):
                copy(j, k, me, me).wait_send()
        for k in range(n):
            swap(k).wait_send()

    return start, finish


def _reduce_shards(name, parts):
    n = len(parts)

    def body(*refs):
        start, finish = _reduce_exchange(refs[:n], refs[n:2 * n], refs[2 * n:3 * n], refs[3 * n:])
        start()
        finish()

    return pl.pallas_call(
        body, name=name,
        in_specs=[HBM] * n, out_specs=[HBM] * (2 * n),
        out_shape=[jax.ShapeDtypeStruct(s.shape, s.dtype) for s in parts] * 2,
        scratch_shapes=_reduce_sems(n),
    )(*parts)


def _all_reduce_scratch(vec):
    return [pltpu.VMEM((N_DEV,) + vec.shape, F32),
            pltpu.SemaphoreType.DMA((N_DEV - 1,)), pltpu.SemaphoreType.DMA((N_DEV - 1,))]


def _all_reduce_exchange(v_ref, o_ref, buf, send, recv):
    x, y, c = _place()
    me = 4 * x + 2 * y + c
    peers = [(x ^ (r >> 2), y ^ ((r >> 1) & 1), c ^ (r & 1)) for r in range(1, N_DEV)]

    def copy(r, slot):
        px, py, pc = peers[r]
        return pltpu.make_async_remote_copy(
            src_ref=v_ref, dst_ref=buf.at[slot], send_sem=send.at[r], recv_sem=recv.at[r],
            device_id=(px, py, pc), device_id_type=MESH)

    def start():
        for r in range(N_DEV - 1):
            copy(r, me).start()
        buf[me] = v_ref[...]

    def finish():
        for r in range(N_DEV - 1):
            px, py, pc = peers[r]
            copy(r, 4 * px + 2 * py + pc).wait_recv()
        total = buf[0]
        for dev in range(1, N_DEV):
            total = total + buf[dev]
        o_ref[...] = total
        for r in range(N_DEV - 1):
            copy(r, me).wait_send()

    return start, finish


def _adamw(w, g, m, v):
    m = ADAM_B1 * m + (1.0 - ADAM_B1) * g
    v = ADAM_B2 * v + (1.0 - ADAM_B2) * (g * g)
    m_hat = m / (1.0 - ADAM_B1 ** ADAM_STEP)
    v_hat = v / (1.0 - ADAM_B2 ** ADAM_STEP)
    return -ADAM_LR * (m_hat / (jnp.sqrt(v_hat) + ADAM_EPS) + ADAM_WD * w), m, v


def _adamw_shard(name, core, w, m, v, got, sib):
    rows, cols = w.shape
    tr = rows // 4
    spec = pl.BlockSpec((tr, cols), lambda i, c_ref: (i, 0))
    spec4 = pl.BlockSpec((N_CHIPS, tr, cols), lambda i, c_ref: (0, i % 2, 0))

    def body(c_ref, w_ref, m_ref, v_ref, got_ref, sib_ref, g_ref, d_ref, mo_ref, vo_ref):
        def four(ref):
            return ((ref[0].astype(F32) + ref[1].astype(F32)) + ref[2].astype(F32)) + ref[3].astype(F32)

        g = jnp.where(pl.program_id(0) // 2 == c_ref[0], four(got_ref), four(sib_ref))
        g_ref[...] = g
        d_ref[...], mo_ref[...], vo_ref[...] = _adamw(w_ref[...], g, m_ref[...], v_ref[...])

    return pl.pallas_call(
        body, name=name,
        grid_spec=pltpu.PrefetchScalarGridSpec(
            num_scalar_prefetch=1, grid=(4,),
            in_specs=[spec, spec, spec, spec4, spec4], out_specs=[spec] * 4),
        out_shape=[jax.ShapeDtypeStruct((rows, cols), F32)] * 4,
        compiler_params=_params("arbitrary"),
    )(core, w, m, v, got, sib)


def _adamw_small(name, w, m, v, g):
    def body(w_ref, m_ref, v_ref, g_ref, d_ref, mo_ref, vo_ref):
        d_ref[...], mo_ref[...], vo_ref[...] = _adamw(w_ref[...], g_ref[...], m_ref[...], v_ref[...])

    return pl.pallas_call(body, name=name, out_shape=[jax.ShapeDtypeStruct(w.shape, F32)] * 3)(w, m, v, g)


def _cast_bf16(name, arrays):
    n = len(arrays)

    def body(*refs):
        for i_ref, o_ref in zip(refs[:n], refs[n:]):
            o_ref[...] = i_ref[...].astype(BF16)

    return pl.pallas_call(
        body, name=name, out_shape=[jax.ShapeDtypeStruct(a.shape, BF16) for a in arrays],
        compiler_params=pltpu.CompilerParams(vmem_limit_bytes=VMEM_LIMIT),
    )(*arrays)


def _pack(arrays, rows):
    flat = jnp.concatenate([a.reshape(-1) for a in arrays])
    return jnp.concatenate([flat, jnp.zeros((rows * 128 - flat.shape[0],), F32)]).reshape(rows, 128)


def _unpack(packed, shapes):
    flat, out, at = packed.reshape(-1), [], 0
    for s in shapes:
        size = 1
        for dim in s:
            size *= dim
        out.append(flat[at:at + size].reshape(s))
        at += size
    return out


def _rows_for(shapes):
    total = 0
    for s in shapes:
        size = 1
        for dim in s:
            size *= dim
        total += size
    return -(-total // 1024) * 8


WEIGHTS = ['meta_tokens', 'ffn1_norm', 'ffn1_w_gate', 'ffn1_w_up', 'ffn1_w_down', 'mix_norm', 'w_in', 'rwkv_mu',
           'rwkv_w0', 'rwkv_w_up', 'rwkv_a0', 'rwkv_a_up', 'rwkv_g_up', 'rwkv_k_k', 'rwkv_k_a', 'rwkv_r_k',
           'rwkv_lnx_w', 'rwkv_lnx_b', 'w_out', 'ffn2_norm', 'ffn2_w_gate', 'ffn2_w_up', 'ffn2_w_down', 'final_norm']
COL_CUT = ['ffn1_w_gate', 'ffn1_w_up', 'w_in', 'ffn2_w_gate', 'ffn2_w_up']
ROW_CUT = ['ffn1_w_down', 'w_out', 'ffn2_w_down']
SMALL_CUT = ['meta_tokens', 'rwkv_w_up', 'rwkv_a_up', 'rwkv_g_up']
TRANSPOSED = ['ffn1_w_gate', 'ffn1_w_up', 'ffn2_w_gate', 'ffn2_w_up']
BIG = COL_CUT + ROW_CUT
REPLICATED = [n for n in WEIGHTS if n not in BIG + SMALL_CUT]


def _join_cols(a):
    return a.transpose(1, 0, 2).reshape(a.shape[1], N_CHIPS * a.shape[2])


def _cut_cols(a):
    return a.reshape(a.shape[0], N_CHIPS, a.shape[1] // N_CHIPS).transpose(1, 0, 2)


def _step(x, loss_target, w, m, v):
    two = lambda a: a.reshape(a.shape[-2], a.shape[-1])

    def rows_cut(n, a):
        return jnp.swapaxes(two(a), 0, 1) if n in TRANSPOSED else two(a)

    def as_given(n, a, like):
        return (jnp.swapaxes(a, 0, 1) if n in TRANSPOSED else a).reshape(like.shape)

    col_cut = [n for n in COL_CUT + SMALL_CUT if n not in TRANSPOSED]

    def join(names, gathered):
        return {n: (_join_cols(a) if n in col_cut else a.reshape(-1, a.shape[-1])) for n, a in zip(names, gathered)}

    def pair_sums(tag, names, g):
        parts = [_cut_cols(g[n]) if n in col_cut else g[n].reshape(N_CHIPS, -1, g[n].shape[-1]) for n in names]
        arrived = _pair_exchange("pair_exchange_" + tag, parts)
        return [_pair_add("pair_add_" + n, p, o) for n, p, o in zip(names, parts, arrived)]

    first = [n for n in BIG if n not in MID + LATE]
    gathered_later = {"mid": MID, "late": LATE}
    groups = {**gathered_later, "gate": ["ffn1_w_gate"], "up": ["ffn1_w_up"]}
    cast = dict(zip(BIG, _cast_bf16("cast_weights", [rows_cut(n, w[n]) for n in BIG])))
    names = first + SMALL_CUT
    shards = [cast[n] for n in first] + [two(w[n]) for n in SMALL_CUT]
    full = {n: (two(w[n]) if w[n].ndim == 3 else w[n]) for n in REPLICATED}
    full.update(join(names, _gather_shards("gather_weights", shards)))
    full["rwkv_r_k"] = w["rwkv_r_k"]
    full["final_norm"] = w["final_norm"]

    small_names = REPLICATED + SMALL_CUT

    def small(g):
        return _pack([g[n] for n in small_names], _rows_for([g[n].shape for n in small_names]))

    late = types.SimpleNamespace(shards={k: [cast[n] for n in names] for k, names in gathered_later.items()},
                                 join=lambda k, gathered: join(groups[k], gathered),
                                 parts=lambda k, g: pair_sums(k, groups[k], g), small=small)

    loss, dx, g, reduced = _local_step(x[0], loss_target[0], full, late)
    loss = lax.psum(loss, ("x", "y", "c"))

    groups["down"] = ["ffn1_w_down"]
    reduced["down"] = list(_reduce_shards("reduce_gradients", pair_sums("down", groups["down"], g)))
    got, sib = {}, {}
    for k, names in groups.items():
        got.update(zip(names, reduced[k][:len(names)]))
        sib.update(zip(names, reduced[k][len(names):]))
    g_small = dict(zip(small_names, _unpack(reduced["small"], [g[n].shape for n in small_names])))
    chip = 2 * lax.axis_index("x") + lax.axis_index("y")
    for n in SMALL_CUT:
        width = g_small[n].shape[1] // N_CHIPS
        g_small[n] = lax.dynamic_slice_in_dim(g_small[n], chip * width, width, axis=1)

    grad, delta, new_m, new_v = {}, {}, {}, {}
    core = lax.axis_index("c").astype(jnp.int32).reshape(1)
    for n in BIG:
        outs = _adamw_shard("adamw_" + n, core, rows_cut(n, w[n]), rows_cut(n, m[n]), rows_cut(n, v[n]), got[n], sib[n])
        grad[n], delta[n], new_m[n], new_v[n] = (as_given(n, o, w[n]) for o in outs)
    shapes = [w[n].shape for n in small_names]
    rows = _rows_for(shapes)
    packed = [_pack([t[n] for n in small_names], rows) for t in (w, m, v)]
    g_packed = _pack([g_small[n] for n in small_names], rows)
    outs = [_unpack(o, shapes) for o in _adamw_small("adamw_small", *packed, g_packed)]
    for i, n in enumerate(small_names):
        grad[n] = g_small[n].reshape(w[n].shape)
        delta[n], new_m[n], new_v[n] = outs[0][i], outs[1][i], outs[2][i]
    return loss, dx[None], grad, delta, new_m, new_v


def kernel(x, meta_tokens, ffn1_norm, ffn1_w_gate, ffn1_w_up, ffn1_w_down, mix_norm, w_in, rwkv_mu, rwkv_w0, rwkv_w_up, rwkv_a0, rwkv_a_up, rwkv_g_up, rwkv_k_k, rwkv_k_a, rwkv_r_k, rwkv_lnx_w, rwkv_lnx_b, w_out, ffn2_norm, ffn2_w_gate, ffn2_w_up, ffn2_w_down, final_norm, loss_target, m_meta_tokens, m_ffn1_norm, m_ffn1_w_gate, m_ffn1_w_up, m_ffn1_w_down, m_mix_norm, m_w_in, m_rwkv_mu, m_rwkv_w0, m_rwkv_w_up, m_rwkv_a0, m_rwkv_a_up, m_rwkv_g_up, m_rwkv_k_k, m_rwkv_k_a, m_rwkv_r_k, m_rwkv_lnx_w, m_rwkv_lnx_b, m_w_out, m_ffn2_norm, m_ffn2_w_gate, m_ffn2_w_up, m_ffn2_w_down, m_final_norm, v_meta_tokens, v_ffn1_norm, v_ffn1_w_gate, v_ffn1_w_up, v_ffn1_w_down, v_mix_norm, v_w_in, v_rwkv_mu, v_rwkv_w0, v_rwkv_w_up, v_rwkv_a0, v_rwkv_a_up, v_rwkv_g_up, v_rwkv_k_k, v_rwkv_k_a, v_rwkv_r_k, v_rwkv_lnx_w, v_rwkv_lnx_b, v_w_out, v_ffn2_norm, v_ffn2_w_gate, v_ffn2_w_up, v_ffn2_w_down, v_final_norm):
    w = dict(zip(WEIGHTS, (meta_tokens, ffn1_norm, ffn1_w_gate, ffn1_w_up, ffn1_w_down, mix_norm, w_in, rwkv_mu, rwkv_w0, rwkv_w_up, rwkv_a0, rwkv_a_up, rwkv_g_up, rwkv_k_k, rwkv_k_a, rwkv_r_k, rwkv_lnx_w, rwkv_lnx_b, w_out, ffn2_norm, ffn2_w_gate, ffn2_w_up, ffn2_w_down, final_norm)))
    m = dict(zip(WEIGHTS, (m_meta_tokens, m_ffn1_norm, m_ffn1_w_gate, m_ffn1_w_up, m_ffn1_w_down, m_mix_norm, m_w_in, m_rwkv_mu, m_rwkv_w0, m_rwkv_w_up, m_rwkv_a0, m_rwkv_a_up, m_rwkv_g_up, m_rwkv_k_k, m_rwkv_k_a, m_rwkv_r_k, m_rwkv_lnx_w, m_rwkv_lnx_b, m_w_out, m_ffn2_norm, m_ffn2_w_gate, m_ffn2_w_up, m_ffn2_w_down, m_final_norm)))
    v = dict(zip(WEIGHTS, (v_meta_tokens, v_ffn1_norm, v_ffn1_w_gate, v_ffn1_w_up, v_ffn1_w_down, v_mix_norm, v_w_in, v_rwkv_mu, v_rwkv_w0, v_rwkv_w_up, v_rwkv_a0, v_rwkv_a_up, v_rwkv_g_up, v_rwkv_k_k, v_rwkv_k_a, v_rwkv_r_k, v_rwkv_lnx_w, v_rwkv_lnx_b, v_w_out, v_ffn2_norm, v_ffn2_w_gate, v_ffn2_w_up, v_ffn2_w_down, v_final_norm)))
    loss, grad_x, grad, delta, new_m, new_v = _step(x, loss_target, w, m, v)
    return (loss, grad_x, *[grad[n] for n in WEIGHTS], *[delta[n] for n in WEIGHTS],
            *[new_m[n] for n in WEIGHTS], *[new_v[n] for n in WEIGHTS])
```

```python
import functools
import types

import jax
import jax.numpy as jnp
from jax import lax
from jax.experimental import pallas as pl
from jax.experimental.pallas import tpu as pltpu

F32 = jnp.float32
BF16 = jnp.bfloat16

RMS_EPS = 1e-6
LNX_EPS = 64e-5
N_META = 16
ROW0 = 128
META_PAD = ROW0 - N_META
HEAD = 64
N_HEADS = 8
GROUP = N_HEADS * HEAD
LORA_W, LORA_A, LORA_G = 32, 32, 96
LORA_PAD = 256
RW_COLS = 3 * GROUP + LORA_PAD
IN_COLS_PAD = 3 * GROUP + RW_COLS
ATT_BLOCK = 128
CHUNK = 64
SCAN_CHUNKS = 2
VMEM_LIMIT = 56 * 1024 * 1024

ADAM_LR, ADAM_B1, ADAM_B2, ADAM_EPS, ADAM_WD, ADAM_STEP = 0.001, 0.9, 0.999, 1e-08, 0.01, 10

MESH = pl.DeviceIdType.MESH


def _params(*sem):
    return pltpu.CompilerParams(dimension_semantics=tuple(sem), vmem_limit_bytes=VMEM_LIMIT)


def _dot(a, b):
    return lax.dot_general(a, b, (((1,), (0,)), ((), ())), preferred_element_type=F32)


def _dot_nt(a, b):
    return lax.dot_general(a, b, (((1,), (1,)), ((), ())), preferred_element_type=F32)


def _dot_tn(a, b):
    return lax.dot_general(a, b, (((0,), (0,)), ((), ())), preferred_element_type=F32)


def _split2(x):
    hi = x.astype(BF16)
    return hi, (x - hi.astype(F32)).astype(BF16)


def _sigmoid(x):
    return 1.0 / (1.0 + jnp.exp(-x))


def _rms_fwd(x, g):
    rstd = lax.rsqrt(jnp.mean(x * x, axis=-1, keepdims=True) + RMS_EPS)
    xhat = x * rstd
    return xhat * g, xhat, rstd


def _rms_bwd(dn, xhat, rstd, g):
    dxhat = dn * g
    dx = rstd * (dxhat - xhat * jnp.mean(dxhat * xhat, axis=-1, keepdims=True))
    return dx, jnp.sum(dn * xhat, axis=0, keepdims=True)


def _row_tile(rows):
    return 384 if rows % 384 == 0 else 128


def _half_tile(cols):
    return cols // 2 if cols % 256 == 0 else cols


def _tall_tile(rows, parts):
    return rows // parts if rows % (16 * parts) == 0 else _row_tile(rows)


def _call_with_exchange(name, body, grid, in_specs, out_specs, out_shape, scratch, operands, params, exchange):
    if exchange is None or not exchange[1]:
        return pl.pallas_call(body, name=name, grid=grid, in_specs=in_specs, out_specs=out_specs,
                              out_shape=out_shape, scratch_shapes=scratch, compiler_params=params)(*operands)
    kind, arrays = exchange
    ns, n_in, n_out, n_scr = len(arrays), len(in_specs), len(out_specs), len(scratch)
    whole = lambda a: pl.BlockSpec(a.shape, lambda *_: (0,) * a.ndim)
    if kind == "gather":
        results = [jax.ShapeDtypeStruct((N_CHIPS,) + s.shape, s.dtype) for s in arrays]
        sems, sent_specs, landed_specs = _gather_sems(ns), [HBM] * ns, [HBM] * ns
    elif kind == "reduce":
        results = [jax.ShapeDtypeStruct(s.shape, s.dtype) for s in arrays] * 2
        sems, sent_specs, landed_specs = _reduce_sems(ns), [HBM] * ns, [HBM] * (2 * ns)
    else:
        results = [jax.ShapeDtypeStruct(arrays[0].shape, F32)]
        sems, sent_specs, landed_specs = _all_reduce_scratch(arrays[0]), [whole(arrays[0])], [whole(arrays[0])]
    n_res = len(results)

    def carried(*refs):
        at = n_in + ns + n_out
        sent, landed = refs[n_in:n_in + ns], refs[at:at + n_res]
        own_scratch, sem_refs = refs[at + n_res:at + n_res + n_scr], refs[at + n_res + n_scr:]
        first, last = _first_and_last_step(grid)
        if kind == "gather":
            start, finish = _gather_exchange(sent, landed, sem_refs)
        elif kind == "reduce":
            start, finish = _reduce_exchange(sent, landed[:ns], landed[ns:], sem_refs)
        else:
            start, finish = _all_reduce_exchange(sent[0], landed[0], *sem_refs)
        pl.when(first)(start)
        body(*refs[:n_in], *refs[n_in + ns:at], *own_scratch)
        pl.when(last)(finish)

    return pl.pallas_call(
        carried, name=name, grid=grid, in_specs=list(in_specs) + sent_specs, out_specs=list(out_specs) + landed_specs,
        out_shape=list(out_shape) + results, scratch_shapes=list(scratch) + sems, compiler_params=params,
    )(*operands, *arrays)


def _ffn_fwd(name, h, g, wg, wu, wd, exchange=None):
    rows, d = h.shape
    f = wg.shape[0]
    tm, tf = _row_tile(rows), _half_tile(f)
    nj = f // tf

    def body(h_ref, g_ref, wg_ref, wu_ref, wd_ref, ho_ref, a_ref, b_ref, n_sc, acc_sc):
        j = pl.program_id(1)

        @pl.when(j == 0)
        def _():
            n, _, _ = _rms_fwd(h_ref[...], g_ref[...])
            n_sc[...] = n.astype(BF16)
            acc_sc[...] = jnp.zeros_like(acc_sc)

        n = n_sc[...]
        a = _dot_nt(n, wg_ref[...])
        b = _dot_nt(n, wu_ref[...])
        a_ref[...] = a
        b_ref[...] = b
        s = a * _sigmoid(a) * b
        acc_sc[...] += _dot(s.astype(BF16), wd_ref[...])

        @pl.when(j == nj - 1)
        def _():
            ho_ref[...] = h_ref[...] + 0.5 * acc_sc[...]

    return _call_with_exchange(
        name, body, (rows // tm, nj),
        [pl.BlockSpec((tm, d), lambda i, j: (i, 0)),
         pl.BlockSpec((1, d), lambda i, j: (0, 0)),
         pl.BlockSpec((tf, d), lambda i, j: (j, 0)),
         pl.BlockSpec((tf, d), lambda i, j: (j, 0)),
         pl.BlockSpec((tf, d), lambda i, j: (j, 0))],
        [pl.BlockSpec((tm, d), lambda i, j: (i, 0)),
         pl.BlockSpec((tm, tf), lambda i, j: (i, j)),
         pl.BlockSpec((tm, tf), lambda i, j: (i, j))],
        [jax.ShapeDtypeStruct((rows, d), F32),
         jax.ShapeDtypeStruct((rows, f), F32),
         jax.ShapeDtypeStruct((rows, f), F32)],
        [pltpu.VMEM((tm, d), BF16), pltpu.VMEM((tm, d), F32)],
        (h, g, wg, wu, wd), _params("arbitrary", "arbitrary"), exchange)


def _ffn_bwd(name, dh, h, g, a, b, wg, wu, wd, exchange=None):
    rows, d = h.shape
    f = wg.shape[0]
    tm, tf = _row_tile(rows), _half_tile(f)
    ni, nj = rows // tm, f // tf

    def body(dh_ref, h_ref, g_ref, a_ref, b_ref, wg_ref, wu_ref, wd_ref,
             dhi_ref, da_ref, db_ref, s_ref, n_ref, dhh_ref, dg_ref, dn_sc):
        i, j = pl.program_id(0), pl.program_id(1)

        @pl.when(j == 0)
        def _():
            n, _, _ = _rms_fwd(h_ref[...], g_ref[...])
            n_ref[...] = n.astype(BF16)
            dhh_ref[...] = (0.5 * dh_ref[...]).astype(BF16)
            dn_sc[...] = jnp.zeros_like(dn_sc)

        @pl.when((i == 0) & (j == 0))
        def _():
            dg_ref[...] = jnp.zeros_like(dg_ref)

        ds = _dot_nt(dhh_ref[...], wd_ref[...])
        av, bv = a_ref[...], b_ref[...]
        sig = _sigmoid(av)
        silu = av * sig
        s_ref[...] = (silu * bv).astype(BF16)
        db = (ds * silu).astype(BF16)
        da = (ds * bv * (sig * (1.0 + av * (1.0 - sig)))).astype(BF16)
        da_ref[...] = da
        db_ref[...] = db
        dn_sc[...] += _dot(da, wg_ref[...]) + _dot(db, wu_ref[...])

        @pl.when(j == nj - 1)
        def _():
            gv = g_ref[...]
            _, xhat, rstd = _rms_fwd(h_ref[...], gv)
            dx, dg = _rms_bwd(dn_sc[...], xhat, rstd, gv)
            dhi_ref[...] = dh_ref[...] + dx
            dg_ref[...] += dg

    return _call_with_exchange(
        name, body, (ni, nj),
        [pl.BlockSpec((tm, d), lambda i, j: (i, 0)),
         pl.BlockSpec((tm, d), lambda i, j: (i, 0)),
         pl.BlockSpec((1, d), lambda i, j: (0, 0)),
         pl.BlockSpec((tm, tf), lambda i, j: (i, j)),
         pl.BlockSpec((tm, tf), lambda i, j: (i, j)),
         pl.BlockSpec((tf, d), lambda i, j: (j, 0)),
         pl.BlockSpec((tf, d), lambda i, j: (j, 0)),
         pl.BlockSpec((tf, d), lambda i, j: (j, 0))],
        [pl.BlockSpec((tm, d), lambda i, j: (i, 0)),
         pl.BlockSpec((tm, tf), lambda i, j: (i, j)),
         pl.BlockSpec((tm, tf), lambda i, j: (i, j)),
         pl.BlockSpec((tm, tf), lambda i, j: (i, j)),
         pl.BlockSpec((tm, d), lambda i, j: (i, 0)),
         pl.BlockSpec((tm, d), lambda i, j: (i, 0)),
         pl.BlockSpec((1, d), lambda i, j: (0, 0))],
        [jax.ShapeDtypeStruct((rows, d), F32),
         jax.ShapeDtypeStruct((rows, f), BF16),
         jax.ShapeDtypeStruct((rows, f), BF16),
         jax.ShapeDtypeStruct((rows, f), BF16),
         jax.ShapeDtypeStruct((rows, d), BF16),
         jax.ShapeDtypeStruct((rows, d), BF16),
         jax.ShapeDtypeStruct((1, d), F32)],
        [pltpu.VMEM((tm, d), F32)],
        (dh, h, g, a, b, wg, wu, wd), _params("arbitrary", "arbitrary"), exchange)


def _mm_tn(name, a, b, exchange=None):
    k, m = a.shape
    n = b.shape[1]
    tk = _tall_tile(k, 3)
    tm = _half_tile(m) if m > 1024 else m
    tn = _half_tile(n) if n > 1024 else n
    nk = k // tk

    def body(a_ref, b_ref, o_ref, acc):
        kk = pl.program_id(2)

        @pl.when(kk == 0)
        def _():
            acc[...] = jnp.zeros_like(acc)

        acc[...] += _dot_tn(a_ref[...], b_ref[...])

        @pl.when(kk == nk - 1)
        def _():
            o_ref[...] = acc[...].astype(BF16)

    outs = _call_with_exchange(
        name, body, (m // tm, n // tn, nk),
        [pl.BlockSpec((tk, tm), lambda i, j, kk: (kk, i)),
         pl.BlockSpec((tk, tn), lambda i, j, kk: (kk, j))],
        [pl.BlockSpec((tm, tn), lambda i, j, kk: (i, j))],
        [jax.ShapeDtypeStruct((m, n), BF16)],
        [pltpu.VMEM((tm, tn), F32)],
        (a, b), _params("arbitrary", "arbitrary", "arbitrary"), exchange)
    return outs if exchange else outs[0]


def _norm_proj(name, h, g, w):
    rows, d = h.shape
    n = w.shape[1]
    split = 3 * GROUP
    tm = _row_tile(rows)

    def body(h_ref, g_ref, w_ref, qkv_ref, p_ref, n_ref):
        nv, _, _ = _rms_fwd(h_ref[...], g_ref[...])
        nb = nv.astype(BF16)
        n_ref[...] = nb
        qkv_ref[...] = _dot(nb, w_ref[:, :split]).astype(BF16)
        p_ref[...] = _dot(nb, w_ref[:, split:])

    return pl.pallas_call(
        body, name=name, grid=(rows // tm,),
        in_specs=[pl.BlockSpec((tm, d), lambda i: (i, 0)),
                  pl.BlockSpec((1, d), lambda i: (0, 0)),
                  pl.BlockSpec((d, n), lambda i: (0, 0))],
        out_specs=[pl.BlockSpec((tm, split), lambda i: (i, 0)),
                   pl.BlockSpec((tm, n - split), lambda i: (i, 0)),
                   pl.BlockSpec((tm, d), lambda i: (i, 0))],
        out_shape=[jax.ShapeDtypeStruct((rows, split), BF16), jax.ShapeDtypeStruct((rows, n - split), F32),
                   jax.ShapeDtypeStruct((rows, d), BF16)],
        compiler_params=_params("arbitrary"),
    )(h, g, w)


def _out_proj(name, h, sb, rw, w):
    rows, d = h.shape
    gw = sb.shape[1]
    tm = _row_tile(rows)

    def body(h_ref, sb_ref, rw_ref, w_ref, o_ref, mix_ref):
        mix_ref[:, :gw] = sb_ref[...].astype(BF16)
        mix_ref[:, gw:] = rw_ref[...].astype(BF16)
        o_ref[...] = h_ref[...] + _dot(mix_ref[...], w_ref[...])

    return pl.pallas_call(
        body, name=name, grid=(rows // tm,),
        in_specs=[pl.BlockSpec((tm, d), lambda i: (i, 0)),
                  pl.BlockSpec((tm, gw), lambda i: (i, 0)),
                  pl.BlockSpec((tm, gw), lambda i: (i, 0)),
                  pl.BlockSpec((2 * gw, d), lambda i: (0, 0))],
        out_specs=[pl.BlockSpec((tm, d), lambda i: (i, 0)),
                   pl.BlockSpec((tm, 2 * gw), lambda i: (i, 0))],
        out_shape=[jax.ShapeDtypeStruct((rows, d), F32), jax.ShapeDtypeStruct((rows, 2 * gw), BF16)],
        compiler_params=_params("arbitrary"),
    )(h, sb, rw, w)


def _out_proj_bwd(name, dh, w):
    rows, d = dh.shape
    k = w.shape[0]
    tm = _row_tile(rows)

    def body(dh_ref, w_ref, dsb_ref, drw_ref, dhb_ref):
        dhb = dh_ref[...].astype(BF16)
        dhb_ref[...] = dhb
        dsb_ref[...] = _dot_nt(dhb, w_ref[:GROUP, :]).astype(BF16)
        drw_ref[...] = _dot_nt(dhb, w_ref[GROUP:, :])

    return pl.pallas_call(
        body, name=name, grid=(rows // tm,),
        in_specs=[pl.BlockSpec((tm, d), lambda i: (i, 0)),
                  pl.BlockSpec((k, d), lambda i: (0, 0))],
        out_specs=[pl.BlockSpec((tm, GROUP), lambda i: (i, 0)),
                   pl.BlockSpec((tm, GROUP), lambda i: (i, 0)),
                   pl.BlockSpec((tm, d), lambda i: (i, 0))],
        out_shape=[jax.ShapeDtypeStruct((rows, GROUP), BF16), jax.ShapeDtypeStruct((rows, GROUP), F32),
                   jax.ShapeDtypeStruct((rows, d), BF16)],
        compiler_params=_params("arbitrary"),
    )(dh, w)


def _norm_proj_bwd(name, dproj, w, h, g, dh):
    rows, n = dproj.shape
    d = w.shape[0]
    tm = _row_tile(rows)

    def body(dp_ref, w_ref, h_ref, g_ref, dh_ref, o_ref, dg_ref):
        @pl.when(pl.program_id(0) == 0)
        def _():
            dg_ref[...] = jnp.zeros_like(dg_ref)

        dn = _dot_nt(dp_ref[...], w_ref[...])
        gv = g_ref[...]
        _, xhat, rstd = _rms_fwd(h_ref[...], gv)
        dx, dg = _rms_bwd(dn, xhat, rstd, gv)
        o_ref[...] = dh_ref[...] + dx
        dg_ref[...] += dg

    return pl.pallas_call(
        body, name=name, grid=(rows // tm,),
        in_specs=[pl.BlockSpec((tm, n), lambda i: (i, 0)),
                  pl.BlockSpec((d, n), lambda i: (0, 0)),
                  pl.BlockSpec((tm, d), lambda i: (i, 0)),
                  pl.BlockSpec((1, d), lambda i: (0, 0)),
                  pl.BlockSpec((tm, d), lambda i: (i, 0))],
        out_specs=[pl.BlockSpec((tm, d), lambda i: (i, 0)),
                   pl.BlockSpec((1, d), lambda i: (0, 0))],
        out_shape=[jax.ShapeDtypeStruct((rows, d), F32), jax.ShapeDtypeStruct((1, d), F32)],
        compiler_params=_params("arbitrary"),
    )(dproj, w, h, g, dh)


def _loss_head(name, h, g, tgt):
    rows, d = h.shape
    tm = _row_tile(rows)

    def body(h_ref, g_ref, t_ref, loss_ref, dh_ref, dg_ref):
        i = pl.program_id(0)

        @pl.when(i == 0)
        def _():
            loss_ref[...] = jnp.zeros_like(loss_ref)
            dg_ref[...] = jnp.zeros_like(dg_ref)

        gv = g_ref[...]
        y, xhat, rstd = _rms_fwd(h_ref[...], gv)
        row = i * tm + lax.broadcasted_iota(jnp.int32, (tm, 1), 0)
        diff = jnp.where(row >= ROW0, y - t_ref[...], 0.0)
        part = 0.5 * jnp.sum(jnp.sum(diff * diff, axis=-1, keepdims=True), axis=0, keepdims=True) / d
        loss_ref[...] += jnp.broadcast_to(part, loss_ref.shape)
        dx, dg = _rms_bwd(diff / d, xhat, rstd, gv)
        dh_ref[...] = dx
        dg_ref[...] += dg

    return pl.pallas_call(
        body, name=name, grid=(rows // tm,),
        in_specs=[pl.BlockSpec((tm, d), lambda i: (i, 0)),
                  pl.BlockSpec((1, d), lambda i: (0, 0)),
                  pl.BlockSpec((tm, d), lambda i: (i, 0))],
        out_specs=[pl.BlockSpec((8, 128), lambda i: (0, 0)),
                   pl.BlockSpec((tm, d), lambda i: (i, 0)),
                   pl.BlockSpec((1, d), lambda i: (0, 0))],
        out_shape=[jax.ShapeDtypeStruct((8, 128), F32),
                   jax.ShapeDtypeStruct((rows, d), F32),
                   jax.ShapeDtypeStruct((1, d), F32)],
        compiler_params=_params("arbitrary"),
    )(h, g, tgt)


def _sb_block(qb, kb, q0, jb, scale):
    bq, bk = qb.shape[0], kb.shape[0]
    z = _dot_nt(qb, kb) * scale
    qpos = q0 + lax.broadcasted_iota(jnp.int32, (bq, bk), 0)
    kpos = jb * bk + lax.broadcasted_iota(jnp.int32, (bq, bk), 1)
    valid = (kpos < qpos) & (kpos >= META_PAD)
    e = jnp.exp(-jnp.abs(z))
    log_keep = jnp.where(valid, -(jnp.maximum(z, 0.0) + jnp.log(1.0 + e)), 0.0)
    return z, valid, e, log_keep


def _tri2(n, cmp):
    r = lax.broadcasted_iota(jnp.int32, (2 * n, n), 0) % n
    c = lax.broadcasted_iota(jnp.int32, (2 * n, n), 1)
    return cmp(r, c).astype(BF16)


def _dot_split(x, t2):
    hi, lo = _split2(x)
    return _dot(jnp.concatenate([hi, lo], axis=1), t2)


ATT_HEADS = 128 // HEAD
ATT_CUT = -104.0
ATT_TILES = GROUP // 128


def _lanes(hh):
    return slice(hh * HEAD, (hh + 1) * HEAD)


def _first_and_last_step(grid):
    here = [pl.program_id(a) for a in range(len(grid))]
    first, last = here[0] == 0, here[0] == grid[0] - 1
    for a in range(1, len(grid)):
        first, last = first & (here[a] == 0), last & (here[a] == grid[a] - 1)
    return first, last


def _sb_fwd(name, qkv, shards=()):
    rows = qkv.shape[0]
    nh, dh = N_HEADS, HEAD
    bq, bk, hg = _row_tile(rows), ATT_BLOCK, ATT_HEADS
    per = bq // bk
    scale = dh ** -0.5
    ns = len(shards)
    grid = (nh // hg, rows // bq)

    def body(q_ref, k_ref, v_ref, *rest):
        o_ref, rt_ref, cnt_ref = rest[ns:ns + 3]
        if ns:
            first, last = _first_and_last_step(grid)
            start, finish = _gather_exchange(rest[:ns], rest[ns + 3:2 * ns + 3], rest[2 * ns + 3:])
            pl.when(first)(start)
        i = pl.program_id(1)
        after = _tri2(bk, lambda r, c: r > c)
        nkb = (i + 1) * per

        def live(state):
            n, carry = state
            top = jnp.max(carry[0][0])
            for hh in range(1, hg):
                top = jnp.maximum(top, jnp.max(carry[hh][0]))
            return (n < nkb) & (top >= ATT_CUT)

        def visit(carry, jb, r0):
            off = pl.multiple_of(jb * bk, bk)
            out = []
            for hh in range(hg):
                rest, acc = carry[hh]
                kb = k_ref[pl.ds(off, bk), _lanes(hh)]
                vb = v_ref[pl.ds(off, bk), _lanes(hh)]
                z, valid, _, log_keep = _sb_block(q_ref[r0:, _lanes(hh)], kb, i * bq + r0, jb, scale)
                log_rest = rest[r0:] + _dot_split(log_keep, after)
                attn = jnp.where(valid, jnp.exp(z + log_keep + log_rest), 0.0)
                new_rest = rest[r0:] + jnp.sum(log_keep, axis=-1, keepdims=True)
                new_acc = acc[r0:] + _dot(attn.astype(BF16), vb)
                if r0:
                    new_rest = jnp.concatenate([rest[:r0], new_rest], axis=0)
                    new_acc = jnp.concatenate([acc[:r0], new_acc], axis=0)
                out.append((new_rest, new_acc))
            return tuple(out)

        carry = tuple((jnp.zeros((bq, 1), F32), jnp.zeros((bq, dh), F32)) for _ in range(hg))
        for dgl in reversed(range(per)):
            carry = visit(carry, i * per + dgl, dgl * bk)
        n, res = lax.while_loop(live, lambda s: (s[0] + 1, visit(s[1], nkb - 1 - s[0], 0)), (jnp.int32(per), carry))
        for hh in range(hg):
            rt_ref[hh] = res[hh][0]
            o_ref[:, _lanes(hh)] = res[hh][1]
            cnt_ref[hh] = jnp.full((bq, 1), n, F32)
        if ns:
            pl.when(last)(finish)

    return pl.pallas_call(
        body, name=name, grid=grid,
        in_specs=[pl.BlockSpec((bq, 128), lambda h, i: (i, h)),
                  pl.BlockSpec((rows, 128), lambda h, i: (0, ATT_TILES + h)),
                  pl.BlockSpec((rows, 128), lambda h, i: (0, 2 * ATT_TILES + h))] + [HBM] * ns,
        out_specs=[pl.BlockSpec((bq, 128), lambda h, i: (i, h)),
                   pl.BlockSpec((hg, bq, 1), lambda h, i: (h, i, 0)),
                   pl.BlockSpec((hg, bq, 1), lambda h, i: (h, i, 0))] + [HBM] * ns,
        out_shape=[jax.ShapeDtypeStruct((rows, GROUP), F32), jax.ShapeDtypeStruct((nh, rows, 1), F32),
                   jax.ShapeDtypeStruct((nh, rows, 1), F32)]
        + [jax.ShapeDtypeStruct((N_CHIPS,) + s.shape, s.dtype) for s in shards],
        scratch_shapes=_gather_sems(ns) if ns else [],
        compiler_params=_params("arbitrary", "arbitrary"),
    )(qkv, qkv, qkv, *shards)


def _sb_bwd(name, qkv, rt, cnt, do, parts=()):
    rows = qkv.shape[0]
    nh, dh = N_HEADS, HEAD
    bq, bk, hg = _row_tile(rows), ATT_BLOCK, ATT_HEADS
    per = bq // bk
    scale = dh ** -0.5
    ns = len(parts)
    grid = (nh // hg, rows // bq)

    def body(q_ref, k_ref, v_ref, rt_ref, cnt_ref, do_ref, *rest):
        dq_ref, dk_ref, dv_ref = rest[ns:ns + 3]
        if ns:
            at_first, at_last = _first_and_last_step(grid)
            start, finish = _reduce_exchange(rest[:ns], rest[ns + 3:2 * ns + 3], rest[2 * ns + 3:3 * ns + 3],
                                             rest[3 * ns + 3:])
            pl.when(at_first)(start)
        i = pl.program_id(1)

        @pl.when(i == 0)
        def _():
            dk_ref[...] = jnp.zeros_like(dk_ref)
            dv_ref[...] = jnp.zeros_like(dv_ref)

        upto = _tri2(bk, lambda r, c: r <= c)
        before = _tri2(bk, lambda r, c: r < c)
        nkb = (i + 1) * per
        first = nkb - jnp.max(cnt_ref[0]).astype(jnp.int32)

        def visit(carry, jb, r0):
            off = pl.multiple_of(jb * bk, bk)
            out = []
            for hh in range(hg):
                keep_sum, g_sum, dq = carry[hh]
                qb, dob = q_ref[r0:, _lanes(hh)], do_ref[r0:, _lanes(hh)]
                kb = k_ref[pl.ds(off, bk), _lanes(hh)]
                vb = v_ref[pl.ds(off, bk), _lanes(hh)]
                z, valid, e, log_keep = _sb_block(qb, kb, i * bq + r0, jb, scale)
                log_rest = rt_ref[hh, r0:, :] - keep_sum[r0:] - _dot_split(log_keep, upto)
                attn = jnp.where(valid, jnp.exp(z + log_keep + log_rest), 0.0)
                g = attn * _dot_nt(dob, vb)
                g_before = g_sum[r0:] + _dot_split(g, before)
                inv = 1.0 / (1.0 + e)
                sig = jnp.where(z >= 0, inv, e * inv)
                dz = (jnp.where(valid, g * (1.0 - sig) - g_before * sig, 0.0) * scale).astype(BF16)
                dk_ref[pl.ds(off, bk), _lanes(hh)] += _dot_tn(dz, qb)
                dv_ref[pl.ds(off, bk), _lanes(hh)] += _dot_tn(attn.astype(BF16), dob)
                new = (keep_sum[r0:] + jnp.sum(log_keep, axis=-1, keepdims=True),
                       g_sum[r0:] + jnp.sum(g, axis=-1, keepdims=True),
                       dq[r0:] + _dot(dz, kb))
                if r0:
                    new = tuple(jnp.concatenate([old[:r0], x], axis=0) for old, x in zip(carry[hh], new))
                out.append(new)
            return tuple(out)

        zero = jnp.zeros((bq, 1), F32)
        res = lax.fori_loop(first, nkb - per, lambda jb, c: visit(c, jb, 0),
                            tuple((zero, zero, jnp.zeros((bq, dh), F32)) for _ in range(hg)))
        for dgl in range(per):
            res = visit(res, i * per + dgl, dgl * bk)
        for hh in range(hg):
            dq_ref[:, _lanes(hh)] = res[hh][2]
        if ns:
            pl.when(at_last)(finish)

    return pl.pallas_call(
        body, name=name, grid=grid,
        in_specs=[pl.BlockSpec((bq, 128), lambda h, i: (i, h)),
                  pl.BlockSpec((rows, 128), lambda h, i: (0, ATT_TILES + h)),
                  pl.BlockSpec((rows, 128), lambda h, i: (0, 2 * ATT_TILES + h)),
                  pl.BlockSpec((hg, bq, 1), lambda h, i: (h, i, 0)),
                  pl.BlockSpec((hg, bq, 1), lambda h, i: (h, i, 0)),
                  pl.BlockSpec((bq, 128), lambda h, i: (i, h))] + [HBM] * ns,
        out_specs=[pl.BlockSpec((bq, 128), lambda h, i: (i, h)),
                   pl.BlockSpec((rows, 128), lambda h, i: (0, h)),
                   pl.BlockSpec((rows, 128), lambda h, i: (0, h))] + [HBM] * (2 * ns),
        out_shape=[jax.ShapeDtypeStruct((rows, GROUP), F32)] * 3
        + [jax.ShapeDtypeStruct(s.shape, s.dtype) for s in parts] * 2,
        scratch_shapes=_reduce_sems(ns) if ns else [],
        compiler_params=_params("arbitrary", "arbitrary"),
    )(qkv, qkv, qkv, rt, cnt, do, *parts)


def _head_sum(x, ones_bd):
    return _dot_split(x, ones_bd)


def _rwkv_pre(p, p_prev, mu, w0, a0, k_k, k_a, w_up, a_up, g_up, ones_bd):
    xs = p + (p_prev - p) * mu
    r = xs[:, :GROUP]
    k0 = xs[:, GROUP:2 * GROUP]
    v = xs[:, 2 * GROUP:3 * GROUP]
    lo = xs[:, 3 * GROUP:]
    wa = w0 + _dot(jnp.tanh(lo).astype(BF16), w_up.astype(BF16))
    w = -(jnp.maximum(-wa, 0.0) + jnp.log(1.0 + jnp.exp(-jnp.abs(wa)))) - 0.5
    log_decay = -jnp.exp(w)
    alpha = _sigmoid(a0 + _dot(lo.astype(BF16), a_up.astype(BF16)))
    gate = _dot(_sigmoid(lo).astype(BF16), g_up.astype(BF16))
    kk = k0 * k_k
    kk = kk * lax.rsqrt(jnp.maximum(_head_sum(kk * kk, ones_bd), 1e-24))
    k = k0 * (1.0 + (alpha - 1.0) * k_a)
    return r, log_decay, k, v, -kk, kk * alpha, gate


def _rwkv_post(y, r, k, v, gate, lnx_w, lnx_b, r_k, ones_bd):
    mean = _head_sum(y, ones_bd) * (1.0 / HEAD)
    yc = y - mean
    var = _head_sum(yc * yc, ones_bd) * (1.0 / HEAD)
    yn = yc * lax.rsqrt(var + LNX_EPS) * lnx_w + lnx_b
    bonus = _head_sum(r * k * r_k, ones_bd) * v
    return (yn + bonus) * gate


_PRE_VEC = 5
_PRE_MAT = 3


def _heads(x):
    return jnp.stack([x[:, _lanes(h)] for h in range(N_HEADS)])


def _unheads(x):
    return jnp.concatenate([x[h] for h in range(N_HEADS)], axis=1)


def _edge_spec(tm, width, tile_of):
    return pl.BlockSpec((8, width), lambda i: (jnp.maximum(tile_of(i) * (tm // 8) - 1, 0), 0))


def _previous_rows(p_ref, edge_ref, tile):
    p = p_ref[...]
    edge = jnp.where(tile == 0, 0.0, edge_ref[7:8, :])
    row = lax.broadcasted_iota(jnp.int32, (p.shape[0], 1), 0)
    return jnp.where(row == 0, edge, pltpu.roll(p, 1, axis=0))


def _rwkv_pre_fwd(name, p, vecs, mats, ones_bd):
    rows = p.shape[0]
    tm = _row_tile(rows)
    row_spec = lambda w: pl.BlockSpec((tm, w), lambda i: (i, 0))
    full = lambda a: pl.BlockSpec(a.shape, lambda i: (0,) * a.ndim)

    def body(p_ref, edge_ref, *refs):
        ins = [r[...] for r in refs[:_PRE_VEC + _PRE_MAT + 1]]
        outs = refs[_PRE_VEC + _PRE_MAT + 1:]
        prev = _previous_rows(p_ref, edge_ref, pl.program_id(0))
        for o_ref, val in zip(outs, _rwkv_pre(p_ref[...], prev, *ins)):
            o_ref[...] = val

    return pl.pallas_call(
        body, name=name, grid=(rows // tm,),
        in_specs=([row_spec(RW_COLS), _edge_spec(tm, RW_COLS, lambda i: i)]
                  + [full(a) for a in (*vecs, *mats, ones_bd)]),
        out_specs=[row_spec(GROUP)] * 7,
        out_shape=[jax.ShapeDtypeStruct((rows, GROUP), F32)] * 7,
        compiler_params=_params("arbitrary"),
    )(p, p, *vecs, *mats, ones_bd)


def _rwkv_pre_bwd(name, p, vecs, mats, ones_bd, cts_scan, ct_gate, cts_b):
    rows = p.shape[0]
    tm = _row_tile(rows)
    nt = rows // tm
    n_par = _PRE_VEC + _PRE_MAT
    tile_of = lambda i: nt - 1 - i
    row_spec = lambda w: pl.BlockSpec((tm, w), lambda i: (tile_of(i), 0))
    full = lambda a: pl.BlockSpec(a.shape, lambda i: (0,) * a.ndim)

    def body(*refs):
        p_ref, edge_ref = refs[0], refs[1]
        par = [r[...] for r in refs[2:2 + n_par]]
        ones = refs[2 + n_par][...]
        cta = [r[...] for r in refs[3 + n_par:10 + n_par]]
        ctb = [r[...] for r in refs[10 + n_par:13 + n_par]]
        dp_ref, par_outs, carry = refs[13 + n_par], refs[14 + n_par:-1], refs[-1]
        step = pl.program_id(0)

        @pl.when(step == 0)
        def _():
            carry[...] = jnp.zeros_like(carry)
            for o_ref in par_outs:
                o_ref[...] = jnp.zeros_like(o_ref)

        ct = (cta[0] + ctb[0], cta[1], cta[2] + ctb[1], cta[3] + ctb[2], cta[4], cta[5], cta[6])
        _, vjp = jax.vjp(lambda pv, ppv, *pr: _rwkv_pre(pv, ppv, *pr, ones),
                         p_ref[...], _previous_rows(p_ref, edge_ref, tile_of(step)), *par)
        grads = vjp(ct)
        row = lax.broadcasted_iota(jnp.int32, (tm, 1), 0)
        dp_ref[...] = grads[0] + jnp.where(row == tm - 1, carry[0:1, :], pltpu.roll(grads[1], tm - 1, axis=0))
        carry[0:1, :] = grads[1][0:1, :]
        for o_ref, gval in zip(par_outs, grads[2:]):
            o_ref[...] += gval

    par_arrays = (*vecs, *mats)
    return pl.pallas_call(
        body, name=name, grid=(nt,),
        in_specs=([row_spec(RW_COLS), _edge_spec(tm, RW_COLS, tile_of)] + [full(a) for a in (*par_arrays, ones_bd)]
                  + [row_spec(GROUP)] * 10),
        out_specs=[row_spec(RW_COLS)] + [full(a) for a in par_arrays],
        out_shape=[jax.ShapeDtypeStruct((rows, RW_COLS), F32)] + [jax.ShapeDtypeStruct(a.shape, F32) for a in par_arrays],
        scratch_shapes=[pltpu.VMEM((8, RW_COLS), F32)],
        compiler_params=_params("arbitrary"),
    )(p, p, *par_arrays, ones_bd, *cts_scan, ct_gate, *cts_b)


def _rwkv_post_fwd(name, y, r, k, v, gate, vecs, ones_bd):
    rows = r.shape[0]
    tm = _row_tile(rows)
    row_spec = pl.BlockSpec((tm, GROUP), lambda i: (i, 0))
    full = lambda a: pl.BlockSpec(a.shape, lambda i: (0,) * a.ndim)

    def body(*refs):
        refs[-1][...] = _rwkv_post(*(r_[...] for r_ in refs[:-1]))

    return pl.pallas_call(
        body, name=name, grid=(rows // tm,),
        in_specs=[row_spec] * 5 + [full(a) for a in (*vecs, ones_bd)],
        out_specs=row_spec,
        out_shape=jax.ShapeDtypeStruct((rows, GROUP), F32),
        compiler_params=_params("arbitrary"),
    )(y, r, k, v, gate, *vecs, ones_bd)


def _rwkv_post_bwd(name, y, r, k, v, gate, vecs, ones_bd, dout):
    rows = r.shape[0]
    tm = _row_tile(rows)
    row_spec = pl.BlockSpec((tm, GROUP), lambda i: (i, 0))
    full = lambda a: pl.BlockSpec(a.shape, lambda i: (0,) * a.ndim)

    def body(*refs):
        vals = [r_[...] for r_ in refs[:8]]
        ones = refs[8][...]
        dout_v = refs[9][...]
        outs = refs[10:]
        _, vjp = jax.vjp(lambda *a: _rwkv_post(*a, ones), *vals)
        grads = vjp(dout_v)
        for o_ref, gval in zip(outs[:5], grads[:5]):
            o_ref[...] = gval

        @pl.when(pl.program_id(0) == 0)
        def _():
            for o_ref in outs[5:]:
                o_ref[...] = jnp.zeros_like(o_ref)

        for o_ref, gval in zip(outs[5:], grads[5:]):
            o_ref[...] += gval

    return pl.pallas_call(
        body, name=name, grid=(rows // tm,),
        in_specs=[row_spec] * 5 + [full(a) for a in (*vecs, ones_bd)] + [row_spec],
        out_specs=[row_spec] * 5 + [full(a) for a in vecs],
        out_shape=[jax.ShapeDtypeStruct((rows, GROUP), F32)] * 5 + [jax.ShapeDtypeStruct(a.shape, F32) for a in vecs],
        compiler_params=_params("arbitrary"),
    )(y, r, k, v, gate, *vecs, ones_bd, dout)


_NN = (((2,), (1,)), ((0,), (0,)))
_NT = (((2,), (2,)), ((0,), (0,)))
_TN = (((1,), (1,)), ((0,), (0,)))


_BWD_FORMS = {"nn": (("nt", False), ("tn", False)),
              "nt": (("nn", False), ("tn", True)),
              "tn": (("nt", True), ("nn", False))}
_DIMS = {"nn": _NN, "nt": _NT, "tn": _TN}


def _bdot(a, b, form):
    return lax.dot_general(a.astype(BF16), b.astype(BF16), _DIMS[form], preferred_element_type=F32)


@functools.partial(jax.custom_vjp, nondiff_argnums=(2,))
def _bmm(a, b, form):
    return _bdot(a, b, form)


def _bmm_fwd(a, b, form):
    return _bdot(a, b, form), (a.astype(BF16), b.astype(BF16))


def _bmm_bwd(form, res, dc):
    a, b = res
    (fa, swap_a), (fb, swap_b) = _BWD_FORMS[form]
    da = _bdot(b, dc, fa) if swap_a else _bdot(dc, b, fa)
    db = _bdot(dc, a, fb) if swap_b else _bdot(a, dc, fb)
    return da, db


_bmm.defvjp(_bmm_fwd, _bmm_bwd)


@jax.custom_vjp
def _cumsum_steps(x):
    return _tri_apply(x, lambda r, c: r >= c)


def _tri_apply(x, cmp):
    nh, c, _ = x.shape
    tri = cmp(lax.broadcasted_iota(jnp.int32, (c, c), 0), lax.broadcasted_iota(jnp.int32, (c, c), 1))
    tri = jnp.broadcast_to(tri.astype(BF16)[None], (nh, c, c))
    hi, lo = _split2(x)
    return (lax.dot_general(tri, hi, _NN, preferred_element_type=F32)
            + lax.dot_general(tri, lo, _NN, preferred_element_type=F32))


_cumsum_steps.defvjp(lambda x: (_cumsum_steps(x), None), lambda _, d: (_tri_apply(d, lambda r, c: r <= c),))


@jax.custom_vjp
def _neumann(n_mat):
    c = n_mat.shape[1]
    inv, power, span = n_mat, _bmm(n_mat, n_mat, "nn"), 2
    while span < c:
        both = _bmm(jnp.concatenate([power, inv], axis=1), power, "nn")
        inv = inv + power + both[:, c:]
        power = both[:, :c]
        span *= 2
    return inv


def _neumann_fwd(n_mat):
    inv = _neumann(n_mat)
    return inv, inv


def _neumann_bwd(inv, d):
    left = d + _bmm(inv, d, "tn")
    return (left + _bmm(left, inv, "nt"),)


_neumann.defvjp(_neumann_fwd, _neumann_bwd)


def _chunk(state, r, log_w, k, v, a, b):
    nh, c, _ = r.shape
    row = lax.broadcasted_iota(jnp.int32, (c, c), 0)
    col = lax.broadcasted_iota(jnp.int32, (c, c), 1)
    cum = _cumsum_steps(log_w)
    mid = cum[:, c // 2 - 1:c // 2, :]
    a_t = a * jnp.exp(cum - log_w - mid)
    r_t = r * jnp.exp(cum - mid)
    back = jnp.exp(mid - cum)
    b_t = b * back
    k_t = k * back
    strict, incl = (row > col)[None], (row >= col)[None]
    ar = jnp.concatenate([a_t, r_t], axis=1)
    on_b = _bmm(ar, b_t, "nt")
    on_k = _bmm(ar, k_t, "nt")
    n_mat = jnp.where(strict, on_b[:, :c], 0.0)
    p_mat = jnp.where(incl, on_b[:, c:], 0.0)
    m_mat = jnp.where(strict, on_k[:, :c], 0.0)
    q_mat = jnp.where(incl, on_k[:, c:], 0.0)
    inv = _neumann(n_mat)
    s_mid = state * jnp.swapaxes(jnp.exp(mid), 1, 2)
    x = _bmm(jnp.concatenate([a_t, m_mat], axis=2), jnp.concatenate([s_mid, v], axis=1), "nn")
    u = x + _bmm(inv, x, "nn")
    y = _bmm(jnp.concatenate([r_t, p_mat, q_mat], axis=2), jnp.concatenate([s_mid, u, v], axis=1), "nn")
    grown = _bmm(jnp.concatenate([b_t, k_t], axis=1), jnp.concatenate([u, v], axis=1), "tn")
    s_new = (s_mid + grown) * jnp.swapaxes(jnp.exp(cum[:, c - 1:c, :] - mid), 1, 2)
    return y, s_new


def _scan_fwd(name, ops):
    rows = ops[0].shape[0]
    nh, dh = N_HEADS, HEAD
    nc, per = rows // CHUNK, SCAN_CHUNKS
    spec = pl.BlockSpec((per * CHUNK, GROUP), lambda c: (c, 0))

    def body(r_ref, w_ref, k_ref, v_ref, a_ref, b_ref, y_ref, st_ref, state):
        @pl.when(pl.program_id(0) == 0)
        def _():
            state[...] = jnp.zeros_like(state)

        s = state[...]
        for u in range(per):
            at = slice(u * CHUNK, (u + 1) * CHUNK)
            st_ref[u] = s
            y, s = _chunk(s, *(_heads(ref[at, :]) for ref in (r_ref, w_ref, k_ref, v_ref, a_ref, b_ref)))
            y_ref[at, :] = _unheads(y)
        state[...] = s

    return pl.pallas_call(
        body, name=name, grid=(nc // per,),
        in_specs=[spec] * 6,
        out_specs=[spec, pl.BlockSpec((per, nh, dh, dh), lambda c: (c, 0, 0, 0))],
        out_shape=[jax.ShapeDtypeStruct((rows, GROUP), F32), jax.ShapeDtypeStruct((nc, nh, dh, dh), F32)],
        scratch_shapes=[pltpu.VMEM((nh, dh, dh), F32)],
        compiler_params=_params("arbitrary"),
    )(*ops)


def _scan_bwd(name, ops, states, dy):
    rows = ops[0].shape[0]
    nh, dh = N_HEADS, HEAD
    nc, per = rows // CHUNK, SCAN_CHUNKS
    steps = nc // per
    spec = pl.BlockSpec((per * CHUNK, GROUP), lambda c: (steps - 1 - c, 0))

    def body(r_ref, w_ref, k_ref, v_ref, a_ref, b_ref, st_ref, dy_ref, *rest):
        outs, dstate = rest[:6], rest[6]

        @pl.when(pl.program_id(0) == 0)
        def _():
            dstate[...] = jnp.zeros_like(dstate)

        ds = dstate[...]
        for u in reversed(range(per)):
            at = slice(u * CHUNK, (u + 1) * CHUNK)
            _, vjp = jax.vjp(_chunk, st_ref[u],
                             *(_heads(ref[at, :]) for ref in (r_ref, w_ref, k_ref, v_ref, a_ref, b_ref)))
            grads = vjp((_heads(dy_ref[at, :]), ds))
            ds = grads[0]
            for o_ref, gval in zip(outs, grads[1:]):
                o_ref[at, :] = _unheads(gval)
        dstate[...] = ds

    return pl.pallas_call(
        body, name=name, grid=(steps,),
        in_specs=[spec] * 6 + [pl.BlockSpec((per, nh, dh, dh), lambda c: (steps - 1 - c, 0, 0, 0)), spec],
        out_specs=[spec] * 6,
        out_shape=[jax.ShapeDtypeStruct((rows, GROUP), F32)] * 6,
        scratch_shapes=[pltpu.VMEM((nh, dh, dh), F32)],
        compiler_params=_params("arbitrary"),
    )(*ops, states, dy)


def _pad_cols(x, cols):
    return jnp.concatenate([x, jnp.zeros(x.shape[:-1] + (cols - x.shape[-1],), x.dtype)], axis=-1)


def _lora_pad(w_up, a_up, g_up):
    z = lambda n: jnp.zeros((n, GROUP), F32)
    return (jnp.concatenate([w_up, z(LORA_PAD - LORA_W)], 0),
            jnp.concatenate([z(LORA_W), a_up, z(LORA_PAD - LORA_W - LORA_A)], 0),
            jnp.concatenate([z(LORA_W + LORA_A), g_up, z(LORA_PAD - LORA_W - LORA_A - LORA_G)], 0))


MID = ['w_in']
LATE = ['ffn2_w_gate', 'ffn2_w_up', 'ffn2_w_down', 'w_out']


def _local_step(x, tgt, w, late=None):
    d = x.shape[1]
    zeros = jnp.zeros((META_PAD, d), F32)
    h0 = jnp.concatenate([zeros, w["meta_tokens"], x], axis=0)
    tgt_p = jnp.concatenate([jnp.zeros((ROW0, d), F32), tgt], axis=0)
    ones_bd = ((lax.broadcasted_iota(jnp.int32, (2 * GROUP, GROUP), 0) % GROUP) // HEAD
               == lax.broadcasted_iota(jnp.int32, (2 * GROUP, GROUP), 1) // HEAD).astype(BF16)
    pre_vecs = (_pad_cols(w["rwkv_mu"], RW_COLS), w["rwkv_w0"], w["rwkv_a0"], w["rwkv_k_k"], w["rwkv_k_a"])
    pre_mats = _lora_pad(w["rwkv_w_up"], w["rwkv_a_up"], w["rwkv_g_up"])
    post_vecs = (w["rwkv_lnx_w"], w["rwkv_lnx_b"], w["rwkv_r_k"].reshape(1, GROUP))

    h1, a1, b1, *gathered = _ffn_fwd("ffn1_fwd", h0, w["ffn1_norm"], w["ffn1_w_gate"], w["ffn1_w_up"],
                                     w["ffn1_w_down"], late and ("gather", late.shards["mid"]))
    if late is not None:
        w = {**w, **late.join("mid", gathered)}
    w_in = _pad_cols(w["w_in"], IN_COLS_PAD)
    qkv, p, n2 = _norm_proj("in_proj", h1, w["mix_norm"], w_in)
    sb, rest_total, visited, *gathered = _sb_fwd("sb_fwd", qkv, late.shards["late"] if late else ())
    if late is not None:
        w = {**w, **late.join("late", gathered)}
    pre = _rwkv_pre_fwd("rwkv_pre_fwd", p, pre_vecs, pre_mats, ones_bd)
    scan_ops, token_ops = pre[:6], (pre[0], pre[2], pre[3], pre[6])
    y, states = _scan_fwd("rwkv_scan_fwd", scan_ops)
    rw = _rwkv_post_fwd("rwkv_post_fwd", y, *token_ops, post_vecs, ones_bd)
    h2, mix = _out_proj("out_proj", h1, sb, rw, w["w_out"])
    h3, a2, b2 = _ffn_fwd("ffn2_fwd", h2, w["ffn2_norm"], w["ffn2_w_gate"], w["ffn2_w_up"], w["ffn2_w_down"])
    loss8, dh3, g_final = _loss_head("loss_head", h3, w["final_norm"].reshape(1, d), tgt_p)

    g = {"final_norm": g_final.reshape(d)}
    dh2, da2, db2, s2, n3, dhh3, g["ffn2_norm"] = _ffn_bwd(
        "ffn2_bwd", dh3, h2, w["ffn2_norm"], a2, b2, w["ffn2_w_gate"], w["ffn2_w_up"], w["ffn2_w_down"])
    g["ffn2_w_gate"] = _mm_tn("ffn2_dgate", da2, n3)
    g["ffn2_w_up"] = _mm_tn("ffn2_dup", db2, n3)
    g["ffn2_w_down"] = _mm_tn("ffn2_ddown", s2, dhh3)
    dsb, drw, dh2b = _out_proj_bwd("out_proj_bwd", dh2, w["w_out"])
    g["w_out"] = _mm_tn("out_proj_dw", mix, dh2b)
    dq, dk, dv, *reduced_late = _sb_bwd("sb_bwd", qkv, rest_total, visited, dsb, late.parts("late", g) if late else ())
    post_g = _rwkv_post_bwd("rwkv_post_bwd", y, *token_ops, post_vecs, ones_bd, drw)
    g["rwkv_lnx_w"], g["rwkv_lnx_b"] = post_g[5], post_g[6]
    g["rwkv_r_k"] = post_g[7].reshape(1, N_HEADS, HEAD)
    scan_g = _scan_bwd("rwkv_scan_bwd", scan_ops, states, post_g[0])
    pre_g = _rwkv_pre_bwd("rwkv_pre_bwd", p, pre_vecs, pre_mats, ones_bd, scan_g, post_g[4], post_g[1:4])
    dp = pre_g[0]
    g["rwkv_mu"] = pre_g[1][:, :w["rwkv_mu"].shape[1]]
    g["rwkv_w0"], g["rwkv_a0"], g["rwkv_k_k"], g["rwkv_k_a"] = pre_g[2:6]
    g["rwkv_w_up"] = pre_g[6][:LORA_W]
    g["rwkv_a_up"] = pre_g[7][LORA_W:LORA_W + LORA_A]
    g["rwkv_g_up"] = pre_g[8][LORA_W + LORA_A:LORA_W + LORA_A + LORA_G]
    live = (jnp.arange(h0.shape[0]) >= META_PAD)[:, None]
    dproj = jnp.where(live, jnp.concatenate([dq, dk, dv, dp], axis=1), 0.0).astype(BF16)
    g["w_in"] = _mm_tn("in_proj_dw", n2, dproj)[:, :w["w_in"].shape[1]]
    dh1, g["mix_norm"] = _norm_proj_bwd("in_proj_bwd", dproj, w_in, h1, w["mix_norm"], dh2)
    dh0, da1, db1, s1, n1, dhh1, g["ffn1_norm"], *reduced_mid = _ffn_bwd(
        "ffn1_bwd", dh1, h0, w["ffn1_norm"], a1, b1, w["ffn1_w_gate"], w["ffn1_w_up"], w["ffn1_w_down"],
        late and ("reduce", late.parts("mid", g)))
    g["meta_tokens"] = dh0[META_PAD:ROW0]
    reduced = {"mid": reduced_mid, "late": reduced_late}
    if late is None:
        g["ffn1_w_gate"] = _mm_tn("ffn1_dgate", da1, n1)
        g["ffn1_w_up"] = _mm_tn("ffn1_dup", db1, n1)
        g["ffn1_w_down"] = _mm_tn("ffn1_ddown", s1, dhh1)
    else:
        g["ffn1_w_gate"], reduced["small"] = _mm_tn("ffn1_dgate", da1, n1, ("all_reduce", [late.small(g)]))
        g["ffn1_w_up"], *reduced["gate"] = _mm_tn("ffn1_dup", db1, n1, ("reduce", late.parts("gate", g)))
        g["ffn1_w_down"], *reduced["up"] = _mm_tn("ffn1_ddown", s1, dhh1, ("reduce", late.parts("up", g)))
    return loss8[0, 0], dh0[ROW0:], g, reduced


N_CHIPS = 4
N_DEV = 8
HBM = pl.BlockSpec(memory_space=pltpu.HBM)


def _place():
    return lax.axis_index("x"), lax.axis_index("y"), lax.axis_index("c")


def _other_chips(x, y):
    return [(1 - x, y), (x, 1 - y), (1 - x, 1 - y)]


def _gather_sems(n):
    return [pltpu.SemaphoreType.DMA((3 * n,)), pltpu.SemaphoreType.DMA((3 * n,)), pltpu.SemaphoreType.DMA((n,)),
            pltpu.SemaphoreType.DMA((3 * n,)), pltpu.SemaphoreType.DMA((3 * n,))]


def _gather_exchange(ins, outs, sems):
    n = len(ins)
    half = [r.shape[0] // 2 for r in ins]
    send, recv, local, d2d_send, d2d_recv = sems
    x, y, c = _place()
    me = 2 * x + y
    chips = _other_chips(x, y)

    def rows_of(k, h):
        return pl.ds(pl.multiple_of(h * half[k], 8), half[k])

    def own(k):
        return pltpu.make_async_copy(ins[k], outs[k].at[me], local.at[k])

    def copy(j, k, slot):
        return pltpu.make_async_remote_copy(
            src_ref=ins[k].at[rows_of(k, c)], dst_ref=outs[k].at[slot, rows_of(k, c)],
            send_sem=send.at[j * n + k], recv_sem=recv.at[j * n + k],
            device_id=(chips[j][0], chips[j][1], c), device_id_type=MESH)

    def passed(j, k, h):
        slot = 2 * chips[j][0] + chips[j][1]
        return pltpu.make_async_remote_copy(
            src_ref=outs[k].at[slot, rows_of(k, h)], dst_ref=outs[k].at[slot, rows_of(k, h)],
            send_sem=d2d_send.at[j * n + k], recv_sem=d2d_recv.at[j * n + k],
            device_id=(x, y, 1 - c), device_id_type=MESH)

    def start():
        for k in range(n):
            own(k).start()
        for j in range(3):
            for k in range(n):
                copy(j, k, me).start()

    def finish():
        for j in range(3):
            for k in range(n):
                copy(j, k, 2 * chips[j][0] + chips[j][1]).wait_recv()
                passed(j, k, c).start()
        for j in range(3):
            for k in range(n):
                passed(j, k, 1 - c).wait_recv()
        for j in range(3):
            for k in range(n):
                copy(j, k, me).wait_send()
                passed(j, k, c).wait_send()
        for k in range(n):
            own(k).wait()

    return start, finish


def _gather_shards(name, shards):
    n = len(shards)

    def body(*refs):
        start, finish = _gather_exchange(refs[:n], refs[n:2 * n], refs[2 * n:])
        start()
        finish()

    return pl.pallas_call(
        body, name=name,
        in_specs=[HBM] * n, out_specs=[HBM] * n,
        out_shape=[jax.ShapeDtypeStruct((N_CHIPS,) + s.shape, s.dtype) for s in shards],
        scratch_shapes=_gather_sems(n),
    )(*shards)


def _pair_exchange(name, parts):
    n = len(parts)
    half = [s.shape[1] // 2 for s in parts]

    def body(*refs):
        ins, outs = refs[:n], refs[n:2 * n]
        send, recv = refs[2 * n:]
        x, y, c = _place()

        def copy(k):
            rows = pl.ds(pl.multiple_of((1 - c) * half[k], 8), half[k])
            return pltpu.make_async_remote_copy(
                src_ref=ins[k].at[:, rows], dst_ref=outs[k], send_sem=send.at[k], recv_sem=recv.at[k],
                device_id=(x, y, 1 - c), device_id_type=MESH)

        for k in range(n):
            copy(k).start()
        for k in range(n):
            copy(k).wait_recv()
        for k in range(n):
            copy(k).wait_send()

    return pl.pallas_call(
        body, name=name,
        in_specs=[HBM] * n, out_specs=[HBM] * n,
        out_shape=[jax.ShapeDtypeStruct((s.shape[0], s.shape[1] // 2, s.shape[2]), s.dtype) for s in parts],
        scratch_shapes=[pltpu.SemaphoreType.DMA((n,)), pltpu.SemaphoreType.DMA((n,))],
    )(*parts)


def _pair_add(name, part, other):
    nch, rows, cols = part.shape
    half = rows // 2

    def body(p_ref, o_ref, out_ref):
        c = lax.axis_index("c")
        mine = p_ref[0, pl.ds(pl.multiple_of(c * half, 16), half), :]
        out_ref[0] = (mine.astype(F32) + o_ref[0].astype(F32)).astype(out_ref.dtype)

    return pl.pallas_call(
        body, name=name, grid=(nch,),
        in_specs=[pl.BlockSpec((1, rows, cols), lambda j: (j, 0, 0)),
                  pl.BlockSpec((1, half, cols), lambda j: (j, 0, 0))],
        out_specs=pl.BlockSpec((1, half, cols), lambda j: (j, 0, 0)),
        out_shape=jax.ShapeDtypeStruct((nch, half, cols), part.dtype),
        compiler_params=_params("arbitrary"),
    )(part, other)


def _reduce_sems(n):
    return [pltpu.SemaphoreType.DMA((3 * n,)), pltpu.SemaphoreType.DMA((3 * n,)), pltpu.SemaphoreType.DMA((n,)),
            pltpu.SemaphoreType.DMA((n,)), pltpu.SemaphoreType.DMA((n,))]


def _reduce_exchange(ins, got, sib, sems):
    n = len(ins)
    send, recv, local, d2d_send, d2d_recv = sems
    x, y, c = _place()
    me = 2 * x + y
    chips = _other_chips(x, y)

    def own(k):
        return pltpu.make_async_copy(ins[k].at[me], got[k].at[me], local.at[k])

    def copy(j, k, shard, slot):
        return pltpu.make_async_remote_copy(
            src_ref=ins[k].at[shard], dst_ref=got[k].at[slot], send_sem=send.at[j * n + k],
            recv_sem=recv.at[j * n + k], device_id=(chips[j][0], chips[j][1], c), device_id_type=MESH)

    def swap(k):
        return pltpu.make_async_remote_copy(
            src_ref=got[k], dst_ref=sib[k], send_sem=d2d_send.at[k], recv_sem=d2d_recv.at[k],
            device_id=(x, y, 1 - c), device_id_type=MESH)

    def start():
        for k in range(n):
            own(k).start()
        for j in range(3):
            for k in range(n):
                copy(j, k, 2 * chips[j][0] + chips[j][1], me).start()

    def finish():
        for k in range(n):
            own(k).wait()
            for j in range(3):
                copy(j, k, me, 2 * chips[j][0] + chips[j][1]).wait_recv()
            swap(k).start()
        for k in range(n):
            swap(k).wait_recv()
        for j in range(3):
            for k in range(n):
                copy(j, k, me, me).wait_send()
        for k in range(n):
            swap(k).wait_send()

    return start, finish


def _reduce_shards(name, parts):
    n = len(parts)

    def body(*refs):
        start, finish = _reduce_exchange(refs[:n], refs[n:2 * n], refs[2 * n:3 * n], refs[3 * n:])
        start()
        finish()

    return pl.pallas_call(
        body, name=name,
        in_specs=[HBM] * n, out_specs=[HBM] * (2 * n),
        out_shape=[jax.ShapeDtypeStruct(s.shape, s.dtype) for s in parts] * 2,
        scratch_shapes=_reduce_sems(n),
    )(*parts)


def _all_reduce_scratch(vec):
    return [pltpu.VMEM((N_DEV,) + vec.shape, F32),
            pltpu.SemaphoreType.DMA((N_DEV - 1,)), pltpu.SemaphoreType.DMA((N_DEV - 1,))]


def _all_reduce_exchange(v_ref, o_ref, buf, send, recv):
    x, y, c = _place()
    me = 4 * x + 2 * y + c
    peers = [(x ^ (r >> 2), y ^ ((r >> 1) & 1), c ^ (r & 1)) for r in range(1, N_DEV)]

    def copy(r, slot):
        px, py, pc = peers[r]
        return pltpu.make_async_remote_copy(
            src_ref=v_ref, dst_ref=buf.at[slot], send_sem=send.at[r], recv_sem=recv.at[r],
            device_id=(px, py, pc), device_id_type=MESH)

    def start():
        for r in range(N_DEV - 1):
            copy(r, me).start()
        buf[me] = v_ref[...]

    def finish():
        for r in range(N_DEV - 1):
            px, py, pc = peers[r]
            copy(r, 4 * px + 2 * py + pc).wait_recv()
        total = buf[0]
        for dev in range(1, N_DEV):
            total = total + buf[dev]
        o_ref[...] = total
        for r in range(N_DEV - 1):
            copy(r, me).wait_send()

    return start, finish


def _adamw(w, g, m, v):
    m = ADAM_B1 * m + (1.0 - ADAM_B1) * g
    v = ADAM_B2 * v + (1.0 - ADAM_B2) * (g * g)
    m_hat = m / (1.0 - ADAM_B1 ** ADAM_STEP)
    v_hat = v / (1.0 - ADAM_B2 ** ADAM_STEP)
    return -ADAM_LR * (m_hat / (jnp.sqrt(v_hat) + ADAM_EPS) + ADAM_WD * w), m, v


def _adamw_shard(name, core, w, m, v, got, sib):
    rows, cols = w.shape
    tr = rows // 4
    spec = pl.BlockSpec((tr, cols), lambda i, c_ref: (i, 0))
    spec4 = pl.BlockSpec((N_CHIPS, tr, cols), lambda i, c_ref: (0, i % 2, 0))

    def body(c_ref, w_ref, m_ref, v_ref, got_ref, sib_ref, g_ref, d_ref, mo_ref, vo_ref):
        def four(ref):
            return ((ref[0].astype(F32) + ref[1].astype(F32)) + ref[2].astype(F32)) + ref[3].astype(F32)

        g = jnp.where(pl.program_id(0) // 2 == c_ref[0], four(got_ref), four(sib_ref))
        g_ref[...] = g
        d_ref[...], mo_ref[...], vo_ref[...] = _adamw(w_ref[...], g, m_ref[...], v_ref[...])

    return pl.pallas_call(
        body, name=name,
        grid_spec=pltpu.PrefetchScalarGridSpec(
            num_scalar_prefetch=1, grid=(4,),
            in_specs=[spec, spec, spec, spec4, spec4], out_specs=[spec] * 4),
        out_shape=[jax.ShapeDtypeStruct((rows, cols), F32)] * 4,
        compiler_params=_params("arbitrary"),
    )(core, w, m, v, got, sib)


def _adamw_small(name, w, m, v, g):
    def body(w_ref, m_ref, v_ref, g_ref, d_ref, mo_ref, vo_ref):
        d_ref[...], mo_ref[...], vo_ref[...] = _adamw(w_ref[...], g_ref[...], m_ref[...], v_ref[...])

    return pl.pallas_call(body, name=name, out_shape=[jax.ShapeDtypeStruct(w.shape, F32)] * 3)(w, m, v, g)


def _cast_bf16(name, arrays):
    n = len(arrays)

    def body(*refs):
        for i_ref, o_ref in zip(refs[:n], refs[n:]):
            o_ref[...] = i_ref[...].astype(BF16)

    return pl.pallas_call(
        body, name=name, out_shape=[jax.ShapeDtypeStruct(a.shape, BF16) for a in arrays],
        compiler_params=pltpu.CompilerParams(vmem_limit_bytes=VMEM_LIMIT),
    )(*arrays)


def _pack(arrays, rows):
    flat = jnp.concatenate([a.reshape(-1) for a in arrays])
    return jnp.concatenate([flat, jnp.zeros((rows * 128 - flat.shape[0],), F32)]).reshape(rows, 128)


def _unpack(packed, shapes):
    flat, out, at = packed.reshape(-1), [], 0
    for s in shapes:
        size = 1
        for dim in s:
            size *= dim
        out.append(flat[at:at + size].reshape(s))
        at += size
    return out


def _rows_for(shapes):
    total = 0
    for s in shapes:
        size = 1
        for dim in s:
            size *= dim
        total += size
    return -(-total // 1024) * 8


WEIGHTS = ['meta_tokens', 'ffn1_norm', 'ffn1_w_gate', 'ffn1_w_up', 'ffn1_w_down', 'mix_norm', 'w_in', 'rwkv_mu',
           'rwkv_w0', 'rwkv_w_up', 'rwkv_a0', 'rwkv_a_up', 'rwkv_g_up', 'rwkv_k_k', 'rwkv_k_a', 'rwkv_r_k',
           'rwkv_lnx_w', 'rwkv_lnx_b', 'w_out', 'ffn2_norm', 'ffn2_w_gate', 'ffn2_w_up', 'ffn2_w_down', 'final_norm']
COL_CUT = ['ffn1_w_gate', 'ffn1_w_up', 'w_in', 'ffn2_w_gate', 'ffn2_w_up']
ROW_CUT = ['ffn1_w_down', 'w_out', 'ffn2_w_down']
SMALL_CUT = ['meta_tokens', 'rwkv_w_up', 'rwkv_a_up', 'rwkv_g_up']
TRANSPOSED = ['ffn1_w_gate', 'ffn1_w_up', 'ffn2_w_gate', 'ffn2_w_up']
BIG = COL_CUT + ROW_CUT
REPLICATED = [n for n in WEIGHTS if n not in BIG + SMALL_CUT]


def _join_cols(a):
    return a.transpose(1, 0, 2).reshape(a.shape[1], N_CHIPS * a.shape[2])


def _cut_cols(a):
    return a.reshape(a.shape[0], N_CHIPS, a.shape[1] // N_CHIPS).transpose(1, 0, 2)


def _step(x, loss_target, w, m, v):
    two = lambda a: a.reshape(a.shape[-2], a.shape[-1])

    def rows_cut(n, a):
        return jnp.swapaxes(two(a), 0, 1) if n in TRANSPOSED else two(a)

    def as_given(n, a, like):
        return (jnp.swapaxes(a, 0, 1) if n in TRANSPOSED else a).reshape(like.shape)

    col_cut = [n for n in COL_CUT + SMALL_CUT if n not in TRANSPOSED]

    def join(names, gathered):
        return {n: (_join_cols(a) if n in col_cut else a.reshape(-1, a.shape[-1])) for n, a in zip(names, gathered)}

    def pair_sums(tag, names, g):
        parts = [_cut_cols(g[n]) if n in col_cut else g[n].reshape(N_CHIPS, -1, g[n].shape[-1]) for n in names]
        arrived = _pair_exchange("pair_exchange_" + tag, parts)
        return [_pair_add("pair_add_" + n, p, o) for n, p, o in zip(names, parts, arrived)]

    first = [n for n in BIG if n not in MID + LATE]
    gathered_later = {"mid": MID, "late": LATE}
    groups = {**gathered_later, "gate": ["ffn1_w_gate"], "up": ["ffn1_w_up"]}
    cast = dict(zip(BIG, _cast_bf16("cast_weights", [rows_cut(n, w[n]) for n in BIG])))
    names = first + SMALL_CUT
    shards = [cast[n] for n in first] + [two(w[n]) for n in SMALL_CUT]
    full = {n: (two(w[n]) if w[n].ndim == 3 else w[n]) for n in REPLICATED}
    full.update(join(names, _gather_shards("gather_weights", shards)))
    full["rwkv_r_k"] = w["rwkv_r_k"]
    full["final_norm"] = w["final_norm"]

    small_names = REPLICATED + SMALL_CUT

    def small(g):
        return _pack([g[n] for n in small_names], _rows_for([g[n].shape for n in small_names]))

    late = types.SimpleNamespace(shards={k: [cast[n] for n in names] for k, names in gathered_later.items()},
                                 join=lambda k, gathered: join(groups[k], gathered),
                                 parts=lambda k, g: pair_sums(k, groups[k], g), small=small)

    loss, dx, g, reduced = _local_step(x[0], loss_target[0], full, late)
    loss = lax.psum(loss, ("x", "y", "c"))

    groups["down"] = ["ffn1_w_down"]
    reduced["down"] = list(_reduce_shards("reduce_gradients", pair_sums("down", groups["down"], g)))
    got, sib = {}, {}
    for k, names in groups.items():
        got.update(zip(names, reduced[k][:len(names)]))
        sib.update(zip(names, reduced[k][len(names):]))
    g_small = dict(zip(small_names, _unpack(reduced["small"], [g[n].shape for n in small_names])))
    chip = 2 * lax.axis_index("x") + lax.axis_index("y")
    for n in SMALL_CUT:
        width = g_small[n].shape[1] // N_CHIPS
        g_small[n] = lax.dynamic_slice_in_dim(g_small[n], chip * width, width, axis=1)

    grad, delta, new_m, new_v = {}, {}, {}, {}
    core = lax.axis_index("c").astype(jnp.int32).reshape(1)
    for n in BIG:
        outs = _adamw_shard("adamw_" + n, core, rows_cut(n, w[n]), rows_cut(n, m[n]), rows_cut(n, v[n]), got[n], sib[n])
        grad[n], delta[n], new_m[n], new_v[n] = (as_given(n, o, w[n]) for o in outs)
    shapes = [w[n].shape for n in small_names]
    rows = _rows_for(shapes)
    packed = [_pack([t[n] for n in small_names], rows) for t in (w, m, v)]
    g_packed = _pack([g_small[n] for n in small_names], rows)
    outs = [_unpack(o, shapes) for o in _adamw_small("adamw_small", *packed, g_packed)]
    for i, n in enumerate(small_names):
        grad[n] = g_small[n].reshape(w[n].shape)
        delta[n], new_m[n], new_v[n] = outs[0][i], outs[1][i], outs[2][i]
    return loss, dx[None], grad, delta, new_m, new_v


def kernel(x, meta_tokens, ffn1_norm, ffn1_w_gate, ffn1_w_up, ffn1_w_down, mix_norm, w_in, rwkv_mu, rwkv_w0, rwkv_w_up, rwkv_a0, rwkv_a_up, rwkv_g_up, rwkv_k_k, rwkv_k_a, rwkv_r_k, rwkv_lnx_w, rwkv_lnx_b, w_out, ffn2_norm, ffn2_w_gate, ffn2_w_up, ffn2_w_down, final_norm, loss_target, m_meta_tokens, m_ffn1_norm, m_ffn1_w_gate, m_ffn1_w_up, m_ffn1_w_down, m_mix_norm, m_w_in, m_rwkv_mu, m_rwkv_w0, m_rwkv_w_up, m_rwkv_a0, m_rwkv_a_up, m_rwkv_g_up, m_rwkv_k_k, m_rwkv_k_a, m_rwkv_r_k, m_rwkv_lnx_w, m_rwkv_lnx_b, m_w_out, m_ffn2_norm, m_ffn2_w_gate, m_ffn2_w_up, m_ffn2_w_down, m_final_norm, v_meta_tokens, v_ffn1_norm, v_ffn1_w_gate, v_ffn1_w_up, v_ffn1_w_down, v_mix_norm, v_w_in, v_rwkv_mu, v_rwkv_w0, v_rwkv_w_up, v_rwkv_a0, v_rwkv_a_up, v_rwkv_g_up, v_rwkv_k_k, v_rwkv_k_a, v_rwkv_r_k, v_rwkv_lnx_w, v_rwkv_lnx_b, v_w_out, v_ffn2_norm, v_ffn2_w_gate, v_ffn2_w_up, v_ffn2_w_down, v_final_norm):
    w = dict(zip(WEIGHTS, (meta_tokens, ffn1_norm, ffn1_w_gate, ffn1_w_up, ffn1_w_down, mix_norm, w_in, rwkv_mu, rwkv_w0, rwkv_w_up, rwkv_a0, rwkv_a_up, rwkv_g_up, rwkv_k_k, rwkv_k_a, rwkv_r_k, rwkv_lnx_w, rwkv_lnx_b, w_out, ffn2_norm, ffn2_w_gate, ffn2_w_up, ffn2_w_down, final_norm)))
    m = dict(zip(WEIGHTS, (m_meta_tokens, m_ffn1_norm, m_ffn1_w_gate, m_ffn1_w_up, m_ffn1_w_down, m_mix_norm, m_w_in, m_rwkv_mu, m_rwkv_w0, m_rwkv_w_up, m_rwkv_a0, m_rwkv_a_up, m_rwkv_g_up, m_rwkv_k_k, m_rwkv_k_a, m_rwkv_r_k, m_rwkv_lnx_w, m_rwkv_lnx_b, m_w_out, m_ffn2_norm, m_ffn2_w_gate, m_ffn2_w_up, m_ffn2_w_down, m_final_norm)))
    v = dict(zip(WEIGHTS, (v_meta_tokens, v_ffn1_norm, v_ffn1_w_gate, v_ffn1_w_up, v_ffn1_w_down, v_mix_norm, v_w_in, v_rwkv_mu, v_rwkv_w0, v_rwkv_w_up, v_rwkv_a0, v_rwkv_a_up, v_rwkv_g_up, v_rwkv_k_k, v_rwkv_k_a, v_rwkv_r_k, v_rwkv_lnx_w, v_rwkv_lnx_b, v_w_out, v_ffn2_norm, v_ffn2_w_gate, v_ffn2_w_up, v_ffn2_w_down, v_final_norm)))
    loss, grad_x, grad, delta, new_m, new_v = _step(x, loss_target, w, m, v)
    return (loss, grad_x, *[grad[n] for n in WEIGHTS], *[delta[n] for n in WEIGHTS],
            *[new_m[n] for n in WEIGHTS], *[new_v[n] for n in WEIGHTS])
```

```python
import functools
import types

import jax
import jax.numpy as jnp
from jax import lax
from jax.experimental import pallas as pl
from jax.experimental.pallas import tpu as pltpu

F32 = jnp.float32
BF16 = jnp.bfloat16

RMS_EPS = 1e-6
LNX_EPS = 64e-5
N_META = 16
ROW0 = 128
META_PAD = ROW0 - N_META
HEAD = 64
N_HEADS = 8
GROUP = N_HEADS * HEAD
LORA_W, LORA_A, LORA_G = 32, 32, 96
LORA_PAD = 256
RW_COLS = 3 * GROUP + LORA_PAD
IN_COLS_PAD = 3 * GROUP + RW_COLS
ATT_BLOCK = 128
CHUNK = 64
SCAN_CHUNKS = 2
VMEM_LIMIT = 56 * 1024 * 1024

ADAM_LR, ADAM_B1, ADAM_B2, ADAM_EPS, ADAM_WD, ADAM_STEP = 0.001, 0.9, 0.999, 1e-08, 0.01, 10

MESH = pl.DeviceIdType.MESH


def _params(*sem):
    return pltpu.CompilerParams(dimension_semantics=tuple(sem), vmem_limit_bytes=VMEM_LIMIT)


def _dot(a, b):
    return lax.dot_general(a, b, (((1,), (0,)), ((), ())), preferred_element_type=F32)


def _dot_nt(a, b):
    return lax.dot_general(a, b, (((1,), (1,)), ((), ())), preferred_element_type=F32)


def _dot_tn(a, b):
    return lax.dot_general(a, b, (((0,), (0,)), ((), ())), preferred_element_type=F32)


def _split2(x):
    hi = x.astype(BF16)
    return hi, (x - hi.astype(F32)).astype(BF16)


def _sigmoid(x):
    return 1.0 / (1.0 + jnp.exp(-x))


def _rms_fwd(x, g):
    rstd = lax.rsqrt(jnp.mean(x * x, axis=-1, keepdims=True) + RMS_EPS)
    xhat = x * rstd
    return xhat * g, xhat, rstd


def _rms_bwd(dn, xhat, rstd, g):
    dxhat = dn * g
    dx = rstd * (dxhat - xhat * jnp.mean(dxhat * xhat, axis=-1, keepdims=True))
    return dx, jnp.sum(dn * xhat, axis=0, keepdims=True)


def _row_tile(rows):
    return 384 if rows % 384 == 0 else 128


def _half_tile(cols):
    return cols // 2 if cols % 256 == 0 else cols


def _tall_tile(rows, parts):
    return rows // parts if rows % (16 * parts) == 0 else _row_tile(rows)


def _call_with_exchange(name, body, grid, in_specs, out_specs, out_shape, scratch, operands, params, exchange):
    if exchange is None or not exchange[1]:
        return pl.pallas_call(body, name=name, grid=grid, in_specs=in_specs, out_specs=out_specs,
                              out_shape=out_shape, scratch_shapes=scratch, compiler_params=params)(*operands)
    kind, arrays = exchange
    ns, n_in, n_out, n_scr = len(arrays), len(in_specs), len(out_specs), len(scratch)
    whole = lambda a: pl.BlockSpec(a.shape, lambda *_: (0,) * a.ndim)
    if kind == "gather":
        results = [jax.ShapeDtypeStruct((N_CHIPS,) + s.shape, s.dtype) for s in arrays]
        sems, sent_specs, landed_specs = _gather_sems(ns), [HBM] * ns, [HBM] * ns
    elif kind == "reduce":
        results = [jax.ShapeDtypeStruct(s.shape, s.dtype) for s in arrays] * 2
        sems, sent_specs, landed_specs = _reduce_sems(ns), [HBM] * ns, [HBM] * (2 * ns)
    else:
        results = [jax.ShapeDtypeStruct(arrays[0].shape, F32)]
        sems, sent_specs, landed_specs = _all_reduce_scratch(arrays[0]), [whole(arrays[0])], [whole(arrays[0])]
    n_res = len(results)

    def carried(*refs):
        at = n_in + ns + n_out
        sent, landed = refs[n_in:n_in + ns], refs[at:at + n_res]
        own_scratch, sem_refs = refs[at + n_res:at + n_res + n_scr], refs[at + n_res + n_scr:]
        first, last = _first_and_last_step(grid)
        if kind == "gather":
            start, finish = _gather_exchange(sent, landed, sem_refs)
        elif kind == "reduce":
            start, finish = _reduce_exchange(sent, landed[:ns], landed[ns:], sem_refs)
        else:
            start, finish = _all_reduce_exchange(sent[0], landed[0], *sem_refs)
        pl.when(first)(start)
        body(*refs[:n_in], *refs[n_in + ns:at], *own_scratch)
        pl.when(last)(finish)

    return pl.pallas_call(
        carried, name=name, grid=grid, in_specs=list(in_specs) + sent_specs, out_specs=list(out_specs) + landed_specs,
        out_shape=list(out_shape) + results, scratch_shapes=list(scratch) + sems, compiler_params=params,
    )(*operands, *arrays)


def _ffn_fwd(name, h, g, wg, wu, wd, exchange=None):
    rows, d = h.shape
    f = wg.shape[0]
    tm, tf = _row_tile(rows), _half_tile(f)
    nj = f // tf

    def body(h_ref, g_ref, wg_ref, wu_ref, wd_ref, ho_ref, a_ref, b_ref, n_sc, acc_sc):
        j = pl.program_id(1)

        @pl.when(j == 0)
        def _():
            n, _, _ = _rms_fwd(h_ref[...], g_ref[...])
            n_sc[...] = n.astype(BF16)
            acc_sc[...] = jnp.zeros_like(acc_sc)

        n = n_sc[...]
        a = _dot_nt(n, wg_ref[...])
        b = _dot_nt(n, wu_ref[...])
        a_ref[...] = a
        b_ref[...] = b
        s = a * _sigmoid(a) * b
        acc_sc[...] += _dot(s.astype(BF16), wd_ref[...])

        @pl.when(j == nj - 1)
        def _():
            ho_ref[...] = h_ref[...] + 0.5 * acc_sc[...]

    return _call_with_exchange(
        name, body, (rows // tm, nj),
        [pl.BlockSpec((tm, d), lambda i, j: (i, 0)),
         pl.BlockSpec((1, d), lambda i, j: (0, 0)),
         pl.BlockSpec((tf, d), lambda i, j: (j, 0)),
         pl.BlockSpec((tf, d), lambda i, j: (j, 0)),
         pl.BlockSpec((tf, d), lambda i, j: (j, 0))],
        [pl.BlockSpec((tm, d), lambda i, j: (i, 0)),
         pl.BlockSpec((tm, tf), lambda i, j: (i, j)),
         pl.BlockSpec((tm, tf), lambda i, j: (i, j))],
        [jax.ShapeDtypeStruct((rows, d), F32),
         jax.ShapeDtypeStruct((rows, f), F32),
         jax.ShapeDtypeStruct((rows, f), F32)],
        [pltpu.VMEM((tm, d), BF16), pltpu.VMEM((tm, d), F32)],
        (h, g, wg, wu, wd), _params("arbitrary", "arbitrary"), exchange)


def _ffn_bwd(name, dh, h, g, a, b, wg, wu, wd, exchange=None):
    rows, d = h.shape
    f = wg.shape[0]
    tm, tf = _row_tile(rows), _half_tile(f)
    ni, nj = rows // tm, f // tf

    def body(dh_ref, h_ref, g_ref, a_ref, b_ref, wg_ref, wu_ref, wd_ref,
             dhi_ref, da_ref, db_ref, s_ref, n_ref, dhh_ref, dg_ref, dn_sc):
        i, j = pl.program_id(0), pl.program_id(1)

        @pl.when(j == 0)
        def _():
            n, _, _ = _rms_fwd(h_ref[...], g_ref[...])
            n_ref[...] = n.astype(BF16)
            dhh_ref[...] = (0.5 * dh_ref[...]).astype(BF16)
            dn_sc[...] = jnp.zeros_like(dn_sc)

        @pl.when((i == 0) & (j == 0))
        def _():
            dg_ref[...] = jnp.zeros_like(dg_ref)

        ds = _dot_nt(dhh_ref[...], wd_ref[...])
        av, bv = a_ref[...], b_ref[...]
        sig = _sigmoid(av)
        silu = av * sig
        s_ref[...] = (silu * bv).astype(BF16)
        db = (ds * silu).astype(BF16)
        da = (ds * bv * (sig * (1.0 + av * (1.0 - sig)))).astype(BF16)
        da_ref[...] = da
        db_ref[...] = db
        dn_sc[...] += _dot(da, wg_ref[...]) + _dot(db, wu_ref[...])

        @pl.when(j == nj - 1)
        def _():
            gv = g_ref[...]
            _, xhat, rstd = _rms_fwd(h_ref[...], gv)
            dx, dg = _rms_bwd(dn_sc[...], xhat, rstd, gv)
            dhi_ref[...] = dh_ref[...] + dx
            dg_ref[...] += dg

    return _call_with_exchange(
        name, body, (ni, nj),
        [pl.BlockSpec((tm, d), lambda i, j: (i, 0)),
         pl.BlockSpec((tm, d), lambda i, j: (i, 0)),
         pl.BlockSpec((1, d), lambda i, j: (0, 0)),
         pl.BlockSpec((tm, tf), lambda i, j: (i, j)),
         pl.BlockSpec((tm, tf), lambda i, j: (i, j)),
         pl.BlockSpec((tf, d), lambda i, j: (j, 0)),
         pl.BlockSpec((tf, d), lambda i, j: (j, 0)),
         pl.BlockSpec((tf, d), lambda i, j: (j, 0))],
        [pl.BlockSpec((tm, d), lambda i, j: (i, 0)),
         pl.BlockSpec((tm, tf), lambda i, j: (i, j)),
         pl.BlockSpec((tm, tf), lambda i, j: (i, j)),
         pl.BlockSpec((tm, tf), lambda i, j: (i, j)),
         pl.BlockSpec((tm, d), lambda i, j: (i, 0)),
         pl.BlockSpec((tm, d), lambda i, j: (i, 0)),
         pl.BlockSpec((1, d), lambda i, j: (0, 0))],
        [jax.ShapeDtypeStruct((rows, d), F32),
         jax.ShapeDtypeStruct((rows, f), BF16),
         jax.ShapeDtypeStruct((rows, f), BF16),
         jax.ShapeDtypeStruct((rows, f), BF16),
         jax.ShapeDtypeStruct((rows, d), BF16),
         jax.ShapeDtypeStruct((rows, d), BF16),
         jax.ShapeDtypeStruct((1, d), F32)],
        [pltpu.VMEM((tm, d), F32)],
        (dh, h, g, a, b, wg, wu, wd), _params("arbitrary", "arbitrary"), exchange)


def _mm_tn(name, a, b, exchange=None):
    k, m = a.shape
    n = b.shape[1]
    tk = _tall_tile(k, 3)
    tm = _half_tile(m) if m > 1024 else m
    tn = _half_tile(n) if n > 1024 else n
    nk = k // tk

    def body(a_ref, b_ref, o_ref, acc):
        kk = pl.program_id(2)

        @pl.when(kk == 0)
        def _():
            acc[...] = jnp.zeros_like(acc)

        acc[...] += _dot_tn(a_ref[...], b_ref[...])

        @pl.when(kk == nk - 1)
        def _():
            o_ref[...] = acc[...].astype(BF16)

    outs = _call_with_exchange(
        name, body, (m // tm, n // tn, nk),
        [pl.BlockSpec((tk, tm), lambda i, j, kk: (kk, i)),
         pl.BlockSpec((tk, tn), lambda i, j, kk: (kk, j))],
        [pl.BlockSpec((tm, tn), lambda i, j, kk: (i, j))],
        [jax.ShapeDtypeStruct((m, n), BF16)],
        [pltpu.VMEM((tm, tn), F32)],
        (a, b), _params("arbitrary", "arbitrary", "arbitrary"), exchange)
    return outs if exchange else outs[0]


def _norm_proj(name, h, g, w):
    rows, d = h.shape
    n = w.shape[1]
    split = 3 * GROUP
    tm = _row_tile(rows)

    def body(h_ref, g_ref, w_ref, qkv_ref, p_ref, n_ref):
        nv, _, _ = _rms_fwd(h_ref[...], g_ref[...])
        nb = nv.astype(BF16)
        n_ref[...] = nb
        qkv_ref[...] = _dot(nb, w_ref[:, :split]).astype(BF16)
        p_ref[...] = _dot(nb, w_ref[:, split:])

    return pl.pallas_call(
        body, name=name, grid=(rows // tm,),
        in_specs=[pl.BlockSpec((tm, d), lambda i: (i, 0)),
                  pl.BlockSpec((1, d), lambda i: (0, 0)),
                  pl.BlockSpec((d, n), lambda i: (0, 0))],
        out_specs=[pl.BlockSpec((tm, split), lambda i: (i, 0)),
                   pl.BlockSpec((tm, n - split), lambda i: (i, 0)),
                   pl.BlockSpec((tm, d), lambda i: (i, 0))],
        out_shape=[jax.ShapeDtypeStruct((rows, split), BF16), jax.ShapeDtypeStruct((rows, n - split), F32),
                   jax.ShapeDtypeStruct((rows, d), BF16)],
        compiler_params=_params("arbitrary"),
    )(h, g, w)


def _out_proj(name, h, sb, rw, w):
    rows, d = h.shape
    gw = sb.shape[1]
    tm = _row_tile(rows)

    def body(h_ref, sb_ref, rw_ref, w_ref, o_ref, mix_ref):
        mix_ref[:, :gw] = sb_ref[...].astype(BF16)
        mix_ref[:, gw:] = rw_ref[...].astype(BF16)
        o_ref[...] = h_ref[...] + _dot(mix_ref[...], w_ref[...])

    return pl.pallas_call(
        body, name=name, grid=(rows // tm,),
        in_specs=[pl.BlockSpec((tm, d), lambda i: (i, 0)),
                  pl.BlockSpec((tm, gw), lambda i: (i, 0)),
                  pl.BlockSpec((tm, gw), lambda i: (i, 0)),
                  pl.BlockSpec((2 * gw, d), lambda i: (0, 0))],
        out_specs=[pl.BlockSpec((tm, d), lambda i: (i, 0)),
                   pl.BlockSpec((tm, 2 * gw), lambda i: (i, 0))],
        out_shape=[jax.ShapeDtypeStruct((rows, d), F32), jax.ShapeDtypeStruct((rows, 2 * gw), BF16)],
        compiler_params=_params("arbitrary"),
    )(h, sb, rw, w)


def _out_proj_bwd(name, dh, w):
    rows, d = dh.shape
    k = w.shape[0]
    tm = _row_tile(rows)

    def body(dh_ref, w_ref, dsb_ref, drw_ref, dhb_ref):
        dhb = dh_ref[...].astype(BF16)
        dhb_ref[...] = dhb
        dsb_ref[...] = _dot_nt(dhb, w_ref[:GROUP, :]).astype(BF16)
        drw_ref[...] = _dot_nt(dhb, w_ref[GROUP:, :])

    return pl.pallas_call(
        body, name=name, grid=(rows // tm,),
        in_specs=[pl.BlockSpec((tm, d), lambda i: (i, 0)),
                  pl.BlockSpec((k, d), lambda i: (0, 0))],
        out_specs=[pl.BlockSpec((tm, GROUP), lambda i: (i, 0)),
                   pl.BlockSpec((tm, GROUP), lambda i: (i, 0)),
                   pl.BlockSpec((tm, d), lambda i: (i, 0))],
        out_shape=[jax.ShapeDtypeStruct((rows, GROUP), BF16), jax.ShapeDtypeStruct((rows, GROUP), F32),
                   jax.ShapeDtypeStruct((rows, d), BF16)],
        compiler_params=_params("arbitrary"),
    )(dh, w)


def _norm_proj_bwd(name, dproj, w, h, g, dh):
    rows, n = dproj.shape
    d = w.shape[0]
    tm = _row_tile(rows)

    def body(dp_ref, w_ref, h_ref, g_ref, dh_ref, o_ref, dg_ref):
        @pl.when(pl.program_id(0) == 0)
        def _():
            dg_ref[...] = jnp.zeros_like(dg_ref)

        dn = _dot_nt(dp_ref[...], w_ref[...])
        gv = g_ref[...]
        _, xhat, rstd = _rms_fwd(h_ref[...], gv)
        dx, dg = _rms_bwd(dn, xhat, rstd, gv)
        o_ref[...] = dh_ref[...] + dx
        dg_ref[...] += dg

    return pl.pallas_call(
        body, name=name, grid=(rows // tm,),
        in_specs=[pl.BlockSpec((tm, n), lambda i: (i, 0)),
                  pl.BlockSpec((d, n), lambda i: (0, 0)),
                  pl.BlockSpec((tm, d), lambda i: (i, 0)),
                  pl.BlockSpec((1, d), lambda i: (0, 0)),
                  pl.BlockSpec((tm, d), lambda i: (i, 0))],
        out_specs=[pl.BlockSpec((tm, d), lambda i: (i, 0)),
                   pl.BlockSpec((1, d), lambda i: (0, 0))],
        out_shape=[jax.ShapeDtypeStruct((rows, d), F32), jax.ShapeDtypeStruct((1, d), F32)],
        compiler_params=_params("arbitrary"),
    )(dproj, w, h, g, dh)


def _loss_head(name, h, g, tgt):
    rows, d = h.shape
    tm = _row_tile(rows)

    def body(h_ref, g_ref, t_ref, loss_ref, dh_ref, dg_ref):
        i = pl.program_id(0)

        @pl.when(i == 0)
        def _():
            loss_ref[...] = jnp.zeros_like(loss_ref)
            dg_ref[...] = jnp.zeros_like(dg_ref)

        gv = g_ref[...]
        y, xhat, rstd = _rms_fwd(h_ref[...], gv)
        row = i * tm + lax.broadcasted_iota(jnp.int32, (tm, 1), 0)
        diff = jnp.where(row >= ROW0, y - t_ref[...], 0.0)
        part = 0.5 * jnp.sum(jnp.sum(diff * diff, axis=-1, keepdims=True), axis=0, keepdims=True) / d
        loss_ref[...] += jnp.broadcast_to(part, loss_ref.shape)
        dx, dg = _rms_bwd(diff / d, xhat, rstd, gv)
        dh_ref[...] = dx
        dg_ref[...] += dg

    return pl.pallas_call(
        body, name=name, grid=(rows // tm,),
        in_specs=[pl.BlockSpec((tm, d), lambda i: (i, 0)),
                  pl.BlockSpec((1, d), lambda i: (0, 0)),
                  pl.BlockSpec((tm, d), lambda i: (i, 0))],
        out_specs=[pl.BlockSpec((8, 128), lambda i: (0, 0)),
                   pl.BlockSpec((tm, d), lambda i: (i, 0)),
                   pl.BlockSpec((1, d), lambda i: (0, 0))],
        out_shape=[jax.ShapeDtypeStruct((8, 128), F32),
                   jax.ShapeDtypeStruct((rows, d), F32),
                   jax.ShapeDtypeStruct((1, d), F32)],
        compiler_params=_params("arbitrary"),
    )(h, g, tgt)


def _sb_block(qb, kb, q0, jb, scale):
    bq, bk = qb.shape[0], kb.shape[0]
    z = _dot_nt(qb, kb) * scale
    qpos = q0 + lax.broadcasted_iota(jnp.int32, (bq, bk), 0)
    kpos = jb * bk + lax.broadcasted_iota(jnp.int32, (bq, bk), 1)
    valid = (kpos < qpos) & (kpos >= META_PAD)
    e = jnp.exp(-jnp.abs(z))
    log_keep = jnp.where(valid, -(jnp.maximum(z, 0.0) + jnp.log(1.0 + e)), 0.0)
    return z, valid, e, log_keep


def _tri2(n, cmp):
    r = lax.broadcasted_iota(jnp.int32, (2 * n, n), 0) % n
    c = lax.broadcasted_iota(jnp.int32, (2 * n, n), 1)
    return cmp(r, c).astype(BF16)


def _dot_split(x, t2):
    hi, lo = _split2(x)
    return _dot(jnp.concatenate([hi, lo], axis=1), t2)


ATT_HEADS = 4
ATT_WIDTH = ATT_HEADS * HEAD
ATT_CUT = -104.0
ATT_TILES = GROUP // ATT_WIDTH


def _lanes(hh):
    return slice(hh * HEAD, (hh + 1) * HEAD)


def _first_and_last_step(grid):
    here = [pl.program_id(a) for a in range(len(grid))]
    first, last = here[0] == 0, here[0] == grid[0] - 1
    for a in range(1, len(grid)):
        first, last = first & (here[a] == 0), last & (here[a] == grid[a] - 1)
    return first, last


def _sb_fwd(name, qkv, shards=()):
    rows = qkv.shape[0]
    nh, dh = N_HEADS, HEAD
    bq, bk, hg = _row_tile(rows), ATT_BLOCK, ATT_HEADS
    per = bq // bk
    scale = dh ** -0.5
    ns = len(shards)
    grid = (nh // hg, rows // bq)

    def body(q_ref, k_ref, v_ref, *rest):
        o_ref, rt_ref, cnt_ref = rest[ns:ns + 3]
        if ns:
            first, last = _first_and_last_step(grid)
            start, finish = _gather_exchange(rest[:ns], rest[ns + 3:2 * ns + 3], rest[2 * ns + 3:])
            pl.when(first)(start)
        i = pl.program_id(1)
        after = _tri2(bk, lambda r, c: r > c)
        nkb = (i + 1) * per

        def live(state):
            n, carry = state
            top = jnp.max(carry[0][0])
            for hh in range(1, hg):
                top = jnp.maximum(top, jnp.max(carry[hh][0]))
            return (n < nkb) & (top >= ATT_CUT)

        def visit(carry, jb, r0):
            off = pl.multiple_of(jb * bk, bk)
            out = []
            for hh in range(hg):
                rest, acc = carry[hh]
                kb = k_ref[pl.ds(off, bk), _lanes(hh)]
                vb = v_ref[pl.ds(off, bk), _lanes(hh)]
                z, valid, _, log_keep = _sb_block(q_ref[r0:, _lanes(hh)], kb, i * bq + r0, jb, scale)
                log_rest = rest[r0:] + _dot_split(log_keep, after)
                attn = jnp.where(valid, jnp.exp(z + log_keep + log_rest), 0.0)
                new_rest = rest[r0:] + jnp.sum(log_keep, axis=-1, keepdims=True)
                new_acc = acc[r0:] + _dot(attn.astype(BF16), vb)
                if r0:
                    new_rest = jnp.concatenate([rest[:r0], new_rest], axis=0)
                    new_acc = jnp.concatenate([acc[:r0], new_acc], axis=0)
                out.append((new_rest, new_acc))
            return tuple(out)

        carry = tuple((jnp.zeros((bq, 1), F32), jnp.zeros((bq, dh), F32)) for _ in range(hg))
        for dgl in reversed(range(per)):
            carry = visit(carry, i * per + dgl, dgl * bk)
        n, res = lax.while_loop(live, lambda s: (s[0] + 1, visit(s[1], nkb - 1 - s[0], 0)), (jnp.int32(per), carry))
        for hh in range(hg):
            rt_ref[hh] = res[hh][0]
            o_ref[:, _lanes(hh)] = res[hh][1]
            cnt_ref[hh] = jnp.full((bq, 1), n, F32)
        if ns:
            pl.when(last)(finish)

    return pl.pallas_call(
        body, name=name, grid=grid,
        in_specs=[pl.BlockSpec((bq, ATT_WIDTH), lambda h, i: (i, h)),
                  pl.BlockSpec((rows, ATT_WIDTH), lambda h, i: (0, ATT_TILES + h)),
                  pl.BlockSpec((rows, ATT_WIDTH), lambda h, i: (0, 2 * ATT_TILES + h))] + [HBM] * ns,
        out_specs=[pl.BlockSpec((bq, ATT_WIDTH), lambda h, i: (i, h)),
                   pl.BlockSpec((hg, bq, 1), lambda h, i: (h, i, 0)),
                   pl.BlockSpec((hg, bq, 1), lambda h, i: (h, i, 0))] + [HBM] * ns,
        out_shape=[jax.ShapeDtypeStruct((rows, GROUP), F32), jax.ShapeDtypeStruct((nh, rows, 1), F32),
                   jax.ShapeDtypeStruct((nh, rows, 1), F32)]
        + [jax.ShapeDtypeStruct((N_CHIPS,) + s.shape, s.dtype) for s in shards],
        scratch_shapes=_gather_sems(ns) if ns else [],
        compiler_params=_params("arbitrary", "arbitrary"),
    )(qkv, qkv, qkv, *shards)


def _sb_bwd(name, qkv, rt, cnt, do, parts=()):
    rows = qkv.shape[0]
    nh, dh = N_HEADS, HEAD
    bq, bk, hg = _row_tile(rows), ATT_BLOCK, ATT_HEADS
    per = bq // bk
    scale = dh ** -0.5
    ns = len(parts)
    grid = (nh // hg, rows // bq)

    def body(q_ref, k_ref, v_ref, rt_ref, cnt_ref, do_ref, *rest):
        dq_ref, dk_ref, dv_ref = rest[ns:ns + 3]
        if ns:
            at_first, at_last = _first_and_last_step(grid)
            start, finish = _reduce_exchange(rest[:ns], rest[ns + 3:2 * ns + 3], rest[2 * ns + 3:3 * ns + 3],
                                             rest[3 * ns + 3:])
            pl.when(at_first)(start)
        i = pl.program_id(1)

        @pl.when(i == 0)
        def _():
            dk_ref[...] = jnp.zeros_like(dk_ref)
            dv_ref[...] = jnp.zeros_like(dv_ref)

        upto = _tri2(bk, lambda r, c: r <= c)
        before = _tri2(bk, lambda r, c: r < c)
        nkb = (i + 1) * per
        first = nkb - jnp.max(cnt_ref[0]).astype(jnp.int32)

        def visit(carry, jb, r0):
            off = pl.multiple_of(jb * bk, bk)
            out = []
            for hh in range(hg):
                keep_sum, g_sum, dq = carry[hh]
                qb, dob = q_ref[r0:, _lanes(hh)], do_ref[r0:, _lanes(hh)]
                kb = k_ref[pl.ds(off, bk), _lanes(hh)]
                vb = v_ref[pl.ds(off, bk), _lanes(hh)]
                z, valid, e, log_keep = _sb_block(qb, kb, i * bq + r0, jb, scale)
                log_rest = rt_ref[hh, r0:, :] - keep_sum[r0:] - _dot_split(log_keep, upto)
                attn = jnp.where(valid, jnp.exp(z + log_keep + log_rest), 0.0)
                g = attn * _dot_nt(dob, vb)
                g_before = g_sum[r0:] + _dot_split(g, before)
                inv = 1.0 / (1.0 + e)
                sig = jnp.where(z >= 0, inv, e * inv)
                dz = (jnp.where(valid, g * (1.0 - sig) - g_before * sig, 0.0) * scale).astype(BF16)
                dk_ref[pl.ds(off, bk), _lanes(hh)] += _dot_tn(dz, qb)
                dv_ref[pl.ds(off, bk), _lanes(hh)] += _dot_tn(attn.astype(BF16), dob)
                new = (keep_sum[r0:] + jnp.sum(log_keep, axis=-1, keepdims=True),
                       g_sum[r0:] + jnp.sum(g, axis=-1, keepdims=True),
                       dq[r0:] + _dot(dz, kb))
                if r0:
                    new = tuple(jnp.concatenate([old[:r0], x], axis=0) for old, x in zip(carry[hh], new))
                out.append(new)
            return tuple(out)

        zero = jnp.zeros((bq, 1), F32)
        res = lax.fori_loop(first, nkb - per, lambda jb, c: visit(c, jb, 0),
                            tuple((zero, zero, jnp.zeros((bq, dh), F32)) for _ in range(hg)))
        for dgl in range(per):
            res = visit(res, i * per + dgl, dgl * bk)
        for hh in range(hg):
            dq_ref[:, _lanes(hh)] = res[hh][2]
        if ns:
            pl.when(at_last)(finish)

    return pl.pallas_call(
        body, name=name, grid=grid,
        in_specs=[pl.BlockSpec((bq, ATT_WIDTH), lambda h, i: (i, h)),
                  pl.BlockSpec((rows, ATT_WIDTH), lambda h, i: (0, ATT_TILES + h)),
                  pl.BlockSpec((rows, ATT_WIDTH), lambda h, i: (0, 2 * ATT_TILES + h)),
                  pl.BlockSpec((hg, bq, 1), lambda h, i: (h, i, 0)),
                  pl.BlockSpec((hg, bq, 1), lambda h, i: (h, i, 0)),
                  pl.BlockSpec((bq, ATT_WIDTH), lambda h, i: (i, h))] + [HBM] * ns,
        out_specs=[pl.BlockSpec((bq, ATT_WIDTH), lambda h, i: (i, h)),
                   pl.BlockSpec((rows, ATT_WIDTH), lambda h, i: (0, h)),
                   pl.BlockSpec((rows, ATT_WIDTH), lambda h, i: (0, h))] + [HBM] * (2 * ns),
        out_shape=[jax.ShapeDtypeStruct((rows, GROUP), F32)] * 3
        + [jax.ShapeDtypeStruct(s.shape, s.dtype) for s in parts] * 2,
        scratch_shapes=_reduce_sems(ns) if ns else [],
        compiler_params=_params("arbitrary", "arbitrary"),
    )(qkv, qkv, qkv, rt, cnt, do, *parts)


def _head_sum(x, ones_bd):
    return _dot_split(x, ones_bd)


def _rwkv_pre(p, p_prev, mu, w0, a0, k_k, k_a, w_up, a_up, g_up, ones_bd):
    xs = p + (p_prev - p) * mu
    r = xs[:, :GROUP]
    k0 = xs[:, GROUP:2 * GROUP]
    v = xs[:, 2 * GROUP:3 * GROUP]
    lo = xs[:, 3 * GROUP:]
    wa = w0 + _dot(jnp.tanh(lo).astype(BF16), w_up.astype(BF16))
    w = -(jnp.maximum(-wa, 0.0) + jnp.log(1.0 + jnp.exp(-jnp.abs(wa)))) - 0.5
    log_decay = -jnp.exp(w)
    alpha = _sigmoid(a0 + _dot(lo.astype(BF16), a_up.astype(BF16)))
    gate = _dot(_sigmoid(lo).astype(BF16), g_up.astype(BF16))
    kk = k0 * k_k
    kk = kk * lax.rsqrt(jnp.maximum(_head_sum(kk * kk, ones_bd), 1e-24))
    k = k0 * (1.0 + (alpha - 1.0) * k_a)
    return r, log_decay, k, v, -kk, kk * alpha, gate


def _rwkv_post(y, r, k, v, gate, lnx_w, lnx_b, r_k, ones_bd):
    mean = _head_sum(y, ones_bd) * (1.0 / HEAD)
    yc = y - mean
    var = _head_sum(yc * yc, ones_bd) * (1.0 / HEAD)
    yn = yc * lax.rsqrt(var + LNX_EPS) * lnx_w + lnx_b
    bonus = _head_sum(r * k * r_k, ones_bd) * v
    return (yn + bonus) * gate


_PRE_VEC = 5
_PRE_MAT = 3


def _heads(x):
    return jnp.stack([x[:, _lanes(h)] for h in range(N_HEADS)])


def _unheads(x):
    return jnp.concatenate([x[h] for h in range(N_HEADS)], axis=1)


def _edge_spec(tm, width, tile_of):
    return pl.BlockSpec((8, width), lambda i: (jnp.maximum(tile_of(i) * (tm // 8) - 1, 0), 0))


def _previous_rows(p_ref, edge_ref, tile):
    p = p_ref[...]
    edge = jnp.where(tile == 0, 0.0, edge_ref[7:8, :])
    row = lax.broadcasted_iota(jnp.int32, (p.shape[0], 1), 0)
    return jnp.where(row == 0, edge, pltpu.roll(p, 1, axis=0))


def _rwkv_pre_fwd(name, p, vecs, mats, ones_bd):
    rows = p.shape[0]
    tm = _row_tile(rows)
    row_spec = lambda w: pl.BlockSpec((tm, w), lambda i: (i, 0))
    full = lambda a: pl.BlockSpec(a.shape, lambda i: (0,) * a.ndim)

    def body(p_ref, edge_ref, *refs):
        ins = [r[...] for r in refs[:_PRE_VEC + _PRE_MAT + 1]]
        outs = refs[_PRE_VEC + _PRE_MAT + 1:]
        prev = _previous_rows(p_ref, edge_ref, pl.program_id(0))
        for o_ref, val in zip(outs, _rwkv_pre(p_ref[...], prev, *ins)):
            o_ref[...] = val

    return pl.pallas_call(
        body, name=name, grid=(rows // tm,),
        in_specs=([row_spec(RW_COLS), _edge_spec(tm, RW_COLS, lambda i: i)]
                  + [full(a) for a in (*vecs, *mats, ones_bd)]),
        out_specs=[row_spec(GROUP)] * 7,
        out_shape=[jax.ShapeDtypeStruct((rows, GROUP), F32)] * 7,
        compiler_params=_params("arbitrary"),
    )(p, p, *vecs, *mats, ones_bd)


def _rwkv_pre_bwd(name, p, vecs, mats, ones_bd, cts_scan, ct_gate, cts_b):
    rows = p.shape[0]
    tm = _row_tile(rows)
    nt = rows // tm
    n_par = _PRE_VEC + _PRE_MAT
    tile_of = lambda i: nt - 1 - i
    row_spec = lambda w: pl.BlockSpec((tm, w), lambda i: (tile_of(i), 0))
    full = lambda a: pl.BlockSpec(a.shape, lambda i: (0,) * a.ndim)

    def body(*refs):
        p_ref, edge_ref = refs[0], refs[1]
        par = [r[...] for r in refs[2:2 + n_par]]
        ones = refs[2 + n_par][...]
        cta = [r[...] for r in refs[3 + n_par:10 + n_par]]
        ctb = [r[...] for r in refs[10 + n_par:13 + n_par]]
        dp_ref, par_outs, carry = refs[13 + n_par], refs[14 + n_par:-1], refs[-1]
        step = pl.program_id(0)

        @pl.when(step == 0)
        def _():
            carry[...] = jnp.zeros_like(carry)
            for o_ref in par_outs:
                o_ref[...] = jnp.zeros_like(o_ref)

        ct = (cta[0] + ctb[0], cta[1], cta[2] + ctb[1], cta[3] + ctb[2], cta[4], cta[5], cta[6])
        _, vjp = jax.vjp(lambda pv, ppv, *pr: _rwkv_pre(pv, ppv, *pr, ones),
                         p_ref[...], _previous_rows(p_ref, edge_ref, tile_of(step)), *par)
        grads = vjp(ct)
        row = lax.broadcasted_iota(jnp.int32, (tm, 1), 0)
        dp_ref[...] = grads[0] + jnp.where(row == tm - 1, carry[0:1, :], pltpu.roll(grads[1], tm - 1, axis=0))
        carry[0:1, :] = grads[1][0:1, :]
        for o_ref, gval in zip(par_outs, grads[2:]):
            o_ref[...] += gval

    par_arrays = (*vecs, *mats)
    return pl.pallas_call(
        body, name=name, grid=(nt,),
        in_specs=([row_spec(RW_COLS), _edge_spec(tm, RW_COLS, tile_of)] + [full(a) for a in (*par_arrays, ones_bd)]
                  + [row_spec(GROUP)] * 10),
        out_specs=[row_spec(RW_COLS)] + [full(a) for a in par_arrays],
        out_shape=[jax.ShapeDtypeStruct((rows, RW_COLS), F32)] + [jax.ShapeDtypeStruct(a.shape, F32) for a in par_arrays],
        scratch_shapes=[pltpu.VMEM((8, RW_COLS), F32)],
        compiler_params=_params("arbitrary"),
    )(p, p, *par_arrays, ones_bd, *cts_scan, ct_gate, *cts_b)


def _rwkv_post_fwd(name, y, r, k, v, gate, vecs, ones_bd):
    rows = r.shape[0]
    tm = _row_tile(rows)
    row_spec = pl.BlockSpec((tm, GROUP), lambda i: (i, 0))
    full = lambda a: pl.BlockSpec(a.shape, lambda i: (0,) * a.ndim)

    def body(*refs):
        refs[-1][...] = _rwkv_post(*(r_[...] for r_ in refs[:-1]))

    return pl.pallas_call(
        body, name=name, grid=(rows // tm,),
        in_specs=[row_spec] * 5 + [full(a) for a in (*vecs, ones_bd)],
        out_specs=row_spec,
        out_shape=jax.ShapeDtypeStruct((rows, GROUP), F32),
        compiler_params=_params("arbitrary"),
    )(y, r, k, v, gate, *vecs, ones_bd)


def _rwkv_post_bwd(name, y, r, k, v, gate, vecs, ones_bd, dout):
    rows = r.shape[0]
    tm = _row_tile(rows)
    row_spec = pl.BlockSpec((tm, GROUP), lambda i: (i, 0))
    full = lambda a: pl.BlockSpec(a.shape, lambda i: (0,) * a.ndim)

    def body(*refs):
        vals = [r_[...] for r_ in refs[:8]]
        ones = refs[8][...]
        dout_v = refs[9][...]
        outs = refs[10:]
        _, vjp = jax.vjp(lambda *a: _rwkv_post(*a, ones), *vals)
        grads = vjp(dout_v)
        for o_ref, gval in zip(outs[:5], grads[:5]):
            o_ref[...] = gval

        @pl.when(pl.program_id(0) == 0)
        def _():
            for o_ref in outs[5:]:
                o_ref[...] = jnp.zeros_like(o_ref)

        for o_ref, gval in zip(outs[5:], grads[5:]):
            o_ref[...] += gval

    return pl.pallas_call(
        body, name=name, grid=(rows // tm,),
        in_specs=[row_spec] * 5 + [full(a) for a in (*vecs, ones_bd)] + [row_spec],
        out_specs=[row_spec] * 5 + [full(a) for a in vecs],
        out_shape=[jax.ShapeDtypeStruct((rows, GROUP), F32)] * 5 + [jax.ShapeDtypeStruct(a.shape, F32) for a in vecs],
        compiler_params=_params("arbitrary"),
    )(y, r, k, v, gate, *vecs, ones_bd, dout)


_NN = (((2,), (1,)), ((0,), (0,)))
_NT = (((2,), (2,)), ((0,), (0,)))
_TN = (((1,), (1,)), ((0,), (0,)))


_BWD_FORMS = {"nn": (("nt", False), ("tn", False)),
              "nt": (("nn", False), ("tn", True)),
              "tn": (("nt", True), ("nn", False))}
_DIMS = {"nn": _NN, "nt": _NT, "tn": _TN}


def _bdot(a, b, form):
    return lax.dot_general(a.astype(BF16), b.astype(BF16), _DIMS[form], preferred_element_type=F32)


@functools.partial(jax.custom_vjp, nondiff_argnums=(2,))
def _bmm(a, b, form):
    return _bdot(a, b, form)


def _bmm_fwd(a, b, form):
    return _bdot(a, b, form), (a.astype(BF16), b.astype(BF16))


def _bmm_bwd(form, res, dc):
    a, b = res
    (fa, swap_a), (fb, swap_b) = _BWD_FORMS[form]
    da = _bdot(b, dc, fa) if swap_a else _bdot(dc, b, fa)
    db = _bdot(dc, a, fb) if swap_b else _bdot(a, dc, fb)
    return da, db


_bmm.defvjp(_bmm_fwd, _bmm_bwd)


@jax.custom_vjp
def _cumsum_steps(x):
    return _tri_apply(x, lambda r, c: r >= c)


def _tri_apply(x, cmp):
    nh, c, _ = x.shape
    tri = cmp(lax.broadcasted_iota(jnp.int32, (c, c), 0), lax.broadcasted_iota(jnp.int32, (c, c), 1))
    tri = jnp.broadcast_to(tri.astype(BF16)[None], (nh, c, c))
    hi, lo = _split2(x)
    return (lax.dot_general(tri, hi, _NN, preferred_element_type=F32)
            + lax.dot_general(tri, lo, _NN, preferred_element_type=F32))


_cumsum_steps.defvjp(lambda x: (_cumsum_steps(x), None), lambda _, d: (_tri_apply(d, lambda r, c: r <= c),))


@jax.custom_vjp
def _neumann(n_mat):
    c = n_mat.shape[1]
    inv, power, span = n_mat, _bmm(n_mat, n_mat, "nn"), 2
    while span < c:
        both = _bmm(jnp.concatenate([power, inv], axis=1), power, "nn")
        inv = inv + power + both[:, c:]
        power = both[:, :c]
        span *= 2
    return inv


def _neumann_fwd(n_mat):
    inv = _neumann(n_mat)
    return inv, inv


def _neumann_bwd(inv, d):
    left = d + _bmm(inv, d, "tn")
    return (left + _bmm(left, inv, "nt"),)


_neumann.defvjp(_neumann_fwd, _neumann_bwd)


def _chunk(state, r, log_w, k, v, a, b):
    nh, c, _ = r.shape
    row = lax.broadcasted_iota(jnp.int32, (c, c), 0)
    col = lax.broadcasted_iota(jnp.int32, (c, c), 1)
    cum = _cumsum_steps(log_w)
    mid = cum[:, c // 2 - 1:c // 2, :]
    a_t = a * jnp.exp(cum - log_w - mid)
    r_t = r * jnp.exp(cum - mid)
    back = jnp.exp(mid - cum)
    b_t = b * back
    k_t = k * back
    strict, incl = (row > col)[None], (row >= col)[None]
    ar = jnp.concatenate([a_t, r_t], axis=1)
    on_b = _bmm(ar, b_t, "nt")
    on_k = _bmm(ar, k_t, "nt")
    n_mat = jnp.where(strict, on_b[:, :c], 0.0)
    p_mat = jnp.where(incl, on_b[:, c:], 0.0)
    m_mat = jnp.where(strict, on_k[:, :c], 0.0)
    q_mat = jnp.where(incl, on_k[:, c:], 0.0)
    inv = _neumann(n_mat)
    s_mid = state * jnp.swapaxes(jnp.exp(mid), 1, 2)
    x = _bmm(jnp.concatenate([a_t, m_mat], axis=2), jnp.concatenate([s_mid, v], axis=1), "nn")
    u = x + _bmm(inv, x, "nn")
    y = _bmm(jnp.concatenate([r_t, p_mat, q_mat], axis=2), jnp.concatenate([s_mid, u, v], axis=1), "nn")
    grown = _bmm(jnp.concatenate([b_t, k_t], axis=1), jnp.concatenate([u, v], axis=1), "tn")
    s_new = (s_mid + grown) * jnp.swapaxes(jnp.exp(cum[:, c - 1:c, :] - mid), 1, 2)
    return y, s_new


def _scan_fwd(name, ops):
    rows = ops[0].shape[0]
    nh, dh = N_HEADS, HEAD
    nc, per = rows // CHUNK, SCAN_CHUNKS
    spec = pl.BlockSpec((per * CHUNK, GROUP), lambda c: (c, 0))

    def body(r_ref, w_ref, k_ref, v_ref, a_ref, b_ref, y_ref, st_ref, state):
        @pl.when(pl.program_id(0) == 0)
        def _():
            state[...] = jnp.zeros_like(state)

        s = state[...]
        for u in range(per):
            at = slice(u * CHUNK, (u + 1) * CHUNK)
            st_ref[u] = s
            y, s = _chunk(s, *(_heads(ref[at, :]) for ref in (r_ref, w_ref, k_ref, v_ref, a_ref, b_ref)))
            y_ref[at, :] = _unheads(y)
        state[...] = s

    return pl.pallas_call(
        body, name=name, grid=(nc // per,),
        in_specs=[spec] * 6,
        out_specs=[spec, pl.BlockSpec((per, nh, dh, dh), lambda c: (c, 0, 0, 0))],
        out_shape=[jax.ShapeDtypeStruct((rows, GROUP), F32), jax.ShapeDtypeStruct((nc, nh, dh, dh), F32)],
        scratch_shapes=[pltpu.VMEM((nh, dh, dh), F32)],
        compiler_params=_params("arbitrary"),
    )(*ops)


def _scan_bwd(name, ops, states, dy):
    rows = ops[0].shape[0]
    nh, dh = N_HEADS, HEAD
    nc, per = rows // CHUNK, SCAN_CHUNKS
    steps = nc // per
    spec = pl.BlockSpec((per * CHUNK, GROUP), lambda c: (steps - 1 - c, 0))

    def body(r_ref, w_ref, k_ref, v_ref, a_ref, b_ref, st_ref, dy_ref, *rest):
        outs, dstate = rest[:6], rest[6]

        @pl.when(pl.program_id(0) == 0)
        def _():
            dstate[...] = jnp.zeros_like(dstate)

        ds = dstate[...]
        for u in reversed(range(per)):
            at = slice(u * CHUNK, (u + 1) * CHUNK)
            _, vjp = jax.vjp(_chunk, st_ref[u],
                             *(_heads(ref[at, :]) for ref in (r_ref, w_ref, k_ref, v_ref, a_ref, b_ref)))
            grads = vjp((_heads(dy_ref[at, :]), ds))
            ds = grads[0]
            for o_ref, gval in zip(outs, grads[1:]):
                o_ref[at, :] = _unheads(gval)
        dstate[...] = ds

    return pl.pallas_call(
        body, name=name, grid=(steps,),
        in_specs=[spec] * 6 + [pl.BlockSpec((per, nh, dh, dh), lambda c: (steps - 1 - c, 0, 0, 0)), spec],
        out_specs=[spec] * 6,
        out_shape=[jax.ShapeDtypeStruct((rows, GROUP), F32)] * 6,
        scratch_shapes=[pltpu.VMEM((nh, dh, dh), F32)],
        compiler_params=_params("arbitrary"),
    )(*ops, states, dy)


def _pad_cols(x, cols):
    return jnp.concatenate([x, jnp.zeros(x.shape[:-1] + (cols - x.shape[-1],), x.dtype)], axis=-1)


def _lora_pad(w_up, a_up, g_up):
    z = lambda n: jnp.zeros((n, GROUP), F32)
    return (jnp.concatenate([w_up, z(LORA_PAD - LORA_W)], 0),
            jnp.concatenate([z(LORA_W), a_up, z(LORA_PAD - LORA_W - LORA_A)], 0),
            jnp.concatenate([z(LORA_W + LORA_A), g_up, z(LORA_PAD - LORA_W - LORA_A - LORA_G)], 0))


MID = ['w_in']
LATE = ['ffn2_w_gate', 'ffn2_w_up', 'ffn2_w_down', 'w_out']


def _local_step(x, tgt, w, late=None):
    d = x.shape[1]
    zeros = jnp.zeros((META_PAD, d), F32)
    h0 = jnp.concatenate([zeros, w["meta_tokens"], x], axis=0)
    tgt_p = jnp.concatenate([jnp.zeros((ROW0, d), F32), tgt], axis=0)
    ones_bd = ((lax.broadcasted_iota(jnp.int32, (2 * GROUP, GROUP), 0) % GROUP) // HEAD
               == lax.broadcasted_iota(jnp.int32, (2 * GROUP, GROUP), 1) // HEAD).astype(BF16)
    pre_vecs = (_pad_cols(w["rwkv_mu"], RW_COLS), w["rwkv_w0"], w["rwkv_a0"], w["rwkv_k_k"], w["rwkv_k_a"])
    pre_mats = _lora_pad(w["rwkv_w_up"], w["rwkv_a_up"], w["rwkv_g_up"])
    post_vecs = (w["rwkv_lnx_w"], w["rwkv_lnx_b"], w["rwkv_r_k"].reshape(1, GROUP))

    h1, a1, b1, *gathered = _ffn_fwd("ffn1_fwd", h0, w["ffn1_norm"], w["ffn1_w_gate"], w["ffn1_w_up"],
                                     w["ffn1_w_down"], late and ("gather", late.shards["mid"]))
    if late is not None:
        w = {**w, **late.join("mid", gathered)}
    w_in = _pad_cols(w["w_in"], IN_COLS_PAD)
    qkv, p, n2 = _norm_proj("in_proj", h1, w["mix_norm"], w_in)
    sb, rest_total, visited, *gathered = _sb_fwd("sb_fwd", qkv, late.shards["late"] if late else ())
    if late is not None:
        w = {**w, **late.join("late", gathered)}
    pre = _rwkv_pre_fwd("rwkv_pre_fwd", p, pre_vecs, pre_mats, ones_bd)
    scan_ops, token_ops = pre[:6], (pre[0], pre[2], pre[3], pre[6])
    y, states = _scan_fwd("rwkv_scan_fwd", scan_ops)
    rw = _rwkv_post_fwd("rwkv_post_fwd", y, *token_ops, post_vecs, ones_bd)
    h2, mix = _out_proj("out_proj", h1, sb, rw, w["w_out"])
    h3, a2, b2 = _ffn_fwd("ffn2_fwd", h2, w["ffn2_norm"], w["ffn2_w_gate"], w["ffn2_w_up"], w["ffn2_w_down"])
    loss8, dh3, g_final = _loss_head("loss_head", h3, w["final_norm"].reshape(1, d), tgt_p)

    g = {"final_norm": g_final.reshape(d)}
    dh2, da2, db2, s2, n3, dhh3, g["ffn2_norm"] = _ffn_bwd(
        "ffn2_bwd", dh3, h2, w["ffn2_norm"], a2, b2, w["ffn2_w_gate"], w["ffn2_w_up"], w["ffn2_w_down"])
    g["ffn2_w_gate"] = _mm_tn("ffn2_dgate", da2, n3)
    g["ffn2_w_up"] = _mm_tn("ffn2_dup", db2, n3)
    g["ffn2_w_down"] = _mm_tn("ffn2_ddown", s2, dhh3)
    dsb, drw, dh2b = _out_proj_bwd("out_proj_bwd", dh2, w["w_out"])
    g["w_out"] = _mm_tn("out_proj_dw", mix, dh2b)
    dq, dk, dv, *reduced_late = _sb_bwd("sb_bwd", qkv, rest_total, visited, dsb, late.parts("late", g) if late else ())
    post_g = _rwkv_post_bwd("rwkv_post_bwd", y, *token_ops, post_vecs, ones_bd, drw)
    g["rwkv_lnx_w"], g["rwkv_lnx_b"] = post_g[5], post_g[6]
    g["rwkv_r_k"] = post_g[7].reshape(1, N_HEADS, HEAD)
    scan_g = _scan_bwd("rwkv_scan_bwd", scan_ops, states, post_g[0])
    pre_g = _rwkv_pre_bwd("rwkv_pre_bwd", p, pre_vecs, pre_mats, ones_bd, scan_g, post_g[4], post_g[1:4])
    dp = pre_g[0]
    g["rwkv_mu"] = pre_g[1][:, :w["rwkv_mu"].shape[1]]
    g["rwkv_w0"], g["rwkv_a0"], g["rwkv_k_k"], g["rwkv_k_a"] = pre_g[2:6]
    g["rwkv_w_up"] = pre_g[6][:LORA_W]
    g["rwkv_a_up"] = pre_g[7][LORA_W:LORA_W + LORA_A]
    g["rwkv_g_up"] = pre_g[8][LORA_W + LORA_A:LORA_W + LORA_A + LORA_G]
    live = (jnp.arange(h0.shape[0]) >= META_PAD)[:, None]
    dproj = jnp.where(live, jnp.concatenate([dq, dk, dv, dp], axis=1), 0.0).astype(BF16)
    g["w_in"] = _mm_tn("in_proj_dw", n2, dproj)[:, :w["w_in"].shape[1]]
    dh1, g["mix_norm"] = _norm_proj_bwd("in_proj_bwd", dproj, w_in, h1, w["mix_norm"], dh2)
    dh0, da1, db1, s1, n1, dhh1, g["ffn1_norm"], *reduced_mid = _ffn_bwd(
        "ffn1_bwd", dh1, h0, w["ffn1_norm"], a1, b1, w["ffn1_w_gate"], w["ffn1_w_up"], w["ffn1_w_down"],
        late and ("reduce", late.parts("mid", g)))
    g["meta_tokens"] = dh0[META_PAD:ROW0]
    reduced = {"mid": reduced_mid, "late": reduced_late}
    if late is None:
        g["ffn1_w_gate"] = _mm_tn("ffn1_dgate", da1, n1)
        g["ffn1_w_up"] = _mm_tn("ffn1_dup", db1, n1)
        g["ffn1_w_down"] = _mm_tn("ffn1_ddown", s1, dhh1)
    else:
        g["ffn1_w_gate"], reduced["small"] = _mm_tn("ffn1_dgate", da1, n1, ("all_reduce", [late.small(g)]))
        g["ffn1_w_up"], *reduced["gate"] = _mm_tn("ffn1_dup", db1, n1, ("reduce", late.parts("gate", g)))
        g["ffn1_w_down"], *reduced["up"] = _mm_tn("ffn1_ddown", s1, dhh1, ("reduce", late.parts("up", g)))
    return loss8[0, 0], dh0[ROW0:], g, reduced


N_CHIPS = 4
N_DEV = 8
HBM = pl.BlockSpec(memory_space=pltpu.HBM)


def _place():
    return lax.axis_index("x"), lax.axis_index("y"), lax.axis_index("c")


def _other_chips(x, y):
    return [(1 - x, y), (x, 1 - y), (1 - x, 1 - y)]


def _gather_sems(n):
    return [pltpu.SemaphoreType.DMA((3 * n,)), pltpu.SemaphoreType.DMA((3 * n,)), pltpu.SemaphoreType.DMA((n,)),
            pltpu.SemaphoreType.DMA((3 * n,)), pltpu.SemaphoreType.DMA((3 * n,))]


def _gather_exchange(ins, outs, sems):
    n = len(ins)
    half = [r.shape[0] // 2 for r in ins]
    send, recv, local, d2d_send, d2d_recv = sems
    x, y, c = _place()
    me = 2 * x + y
    chips = _other_chips(x, y)

    def rows_of(k, h):
        return pl.ds(pl.multiple_of(h * half[k], 8), half[k])

    def own(k):
        return pltpu.make_async_copy(ins[k], outs[k].at[me], local.at[k])

    def copy(j, k, slot):
        return pltpu.make_async_remote_copy(
            src_ref=ins[k].at[rows_of(k, c)], dst_ref=outs[k].at[slot, rows_of(k, c)],
            send_sem=send.at[j * n + k], recv_sem=recv.at[j * n + k],
            device_id=(chips[j][0], chips[j][1], c), device_id_type=MESH)

    def passed(j, k, h):
        slot = 2 * chips[j][0] + chips[j][1]
        return pltpu.make_async_remote_copy(
            src_ref=outs[k].at[slot, rows_of(k, h)], dst_ref=outs[k].at[slot, rows_of(k, h)],
            send_sem=d2d_send.at[j * n + k], recv_sem=d2d_recv.at[j * n + k],
            device_id=(x, y, 1 - c), device_id_type=MESH)

    def start():
        for k in range(n):
            own(k).start()
        for j in range(3):
            for k in range(n):
                copy(j, k, me).start()

    def finish():
        for j in range(3):
            for k in range(n):
                copy(j, k, 2 * chips[j][0] + chips[j][1]).wait_recv()
                passed(j, k, c).start()
        for j in range(3):
            for k in range(n):
                passed(j, k, 1 - c).wait_recv()
        for j in range(3):
            for k in range(n):
                copy(j, k, me).wait_send()
                passed(j, k, c).wait_send()
        for k in range(n):
            own(k).wait()

    return start, finish


def _gather_shards(name, shards):
    n = len(shards)

    def body(*refs):
        start, finish = _gather_exchange(refs[:n], refs[n:2 * n], refs[2 * n:])
        start()
        finish()

    return pl.pallas_call(
        body, name=name,
        in_specs=[HBM] * n, out_specs=[HBM] * n,
        out_shape=[jax.ShapeDtypeStruct((N_CHIPS,) + s.shape, s.dtype) for s in shards],
        scratch_shapes=_gather_sems(n),
    )(*shards)


def _pair_exchange(name, parts):
    n = len(parts)
    half = [s.shape[1] // 2 for s in parts]

    def body(*refs):
        ins, outs = refs[:n], refs[n:2 * n]
        send, recv = refs[2 * n:]
        x, y, c = _place()

        def copy(k):
            rows = pl.ds(pl.multiple_of((1 - c) * half[k], 8), half[k])
            return pltpu.make_async_remote_copy(
                src_ref=ins[k].at[:, rows], dst_ref=outs[k], send_sem=send.at[k], recv_sem=recv.at[k],
                device_id=(x, y, 1 - c), device_id_type=MESH)

        for k in range(n):
            copy(k).start()
        for k in range(n):
            copy(k).wait_recv()
        for k in range(n):
            copy(k).wait_send()

    return pl.pallas_call(
        body, name=name,
        in_specs=[HBM] * n, out_specs=[HBM] * n,
        out_shape=[jax.ShapeDtypeStruct((s.shape[0], s.shape[1] // 2, s.shape[2]), s.dtype) for s in parts],
        scratch_shapes=[pltpu.SemaphoreType.DMA((n,)), pltpu.SemaphoreType.DMA((n,))],
    )(*parts)


def _pair_add(name, part, other):
    nch, rows, cols = part.shape
    half = rows // 2

    def body(p_ref, o_ref, out_ref):
        c = lax.axis_index("c")
        mine = p_ref[0, pl.ds(pl.multiple_of(c * half, 16), half), :]
        out_ref[0] = (mine.astype(F32) + o_ref[0].astype(F32)).astype(out_ref.dtype)

    return pl.pallas_call(
        body, name=name, grid=(nch,),
        in_specs=[pl.BlockSpec((1, rows, cols), lambda j: (j, 0, 0)),
                  pl.BlockSpec((1, half, cols), lambda j: (j, 0, 0))],
        out_specs=pl.BlockSpec((1, half, cols), lambda j: (j, 0, 0)),
        out_shape=jax.ShapeDtypeStruct((nch, half, cols), part.dtype),
        compiler_params=_params("arbitrary"),
    )(part, other)


def _reduce_sems(n):
    return [pltpu.SemaphoreType.DMA((3 * n,)), pltpu.SemaphoreType.DMA((3 * n,)), pltpu.SemaphoreType.DMA((n,)),
            pltpu.SemaphoreType.DMA((n,)), pltpu.SemaphoreType.DMA((n,))]


def _reduce_exchange(ins, got, sib, sems):
    n = len(ins)
    send, recv, local, d2d_send, d2d_recv = sems
    x, y, c = _place()
    me = 2 * x + y
    chips = _other_chips(x, y)

    def own(k):
        return pltpu.make_async_copy(ins[k].at[me], got[k].at[me], local.at[k])

    def copy(j, k, shard, slot):
        return pltpu.make_async_remote_copy(
            src_ref=ins[k].at[shard], dst_ref=got[k].at[slot], send_sem=send.at[j * n + k],
            recv_sem=recv.at[j * n + k], device_id=(chips[j][0], chips[j][1], c), device_id_type=MESH)

    def swap(k):
        return pltpu.make_async_remote_copy(
            src_ref=got[k], dst_ref=sib[k], send_sem=d2d_send.at[k], recv_sem=d2d_recv.at[k],
            device_id=(x, y, 1 - c), device_id_type=MESH)

    def start():
        for k in range(n):
            own(k).start()
        for j in range(3):
            for k in range(n):
                copy(j, k, 2 * chips[j][0] + chips[j][1], me).start()

    def finish():
        for k in range(n):
            own(k).wait()
            for j in range(3):
                copy(j, k, me, 2 * chips[j][0] + chips[j][1]).wait_recv()
            swap(k).start()
        for k in range(n):
            swap(k).wait_recv()
        for j in range(3):
            for k in range(n):
                copy(j, k, me, me).wait_send()
        for k in range(n):
            swap(k).wait_send()

    return start, finish


def _reduce_shards(name, parts):
    n = len(parts)

    def body(*refs):
        start, finish = _reduce_exchange(refs[:n], refs[n:2 * n], refs[2 * n:3 * n], refs[3 * n:])
        start()
        finish()

    return pl.pallas_call(
        body, name=name,
        in_specs=[HBM] * n, out_specs=[HBM] * (2 * n),
        out_shape=[jax.ShapeDtypeStruct(s.shape, s.dtype) for s in parts] * 2,
        scratch_shapes=_reduce_sems(n),
    )(*parts)


def _all_reduce_scratch(vec):
    return [pltpu.VMEM((N_DEV,) + vec.shape, F32),
            pltpu.SemaphoreType.DMA((N_DEV - 1,)), pltpu.SemaphoreType.DMA((N_DEV - 1,))]


def _all_reduce_exchange(v_ref, o_ref, buf, send, recv):
    x, y, c = _place()
    me = 4 * x + 2 * y + c
    peers = [(x ^ (r >> 2), y ^ ((r >> 1) & 1), c ^ (r & 1)) for r in range(1, N_DEV)]

    def copy(r, slot):
        px, py, pc = peers[r]
        return pltpu.make_async_remote_copy(
            src_ref=v_ref, dst_ref=buf.at[slot], send_sem=send.at[r], recv_sem=recv.at[r],
            device_id=(px, py, pc), device_id_type=MESH)

    def start():
        for r in range(N_DEV - 1):
            copy(r, me).start()
        buf[me] = v_ref[...]

    def finish():
        for r in range(N_DEV - 1):
            px, py, pc = peers[r]
            copy(r, 4 * px + 2 * py + pc).wait_recv()
        total = buf[0]
        for dev in range(1, N_DEV):
            total = total + buf[dev]
        o_ref[...] = total
        for r in range(N_DEV - 1):
            copy(r, me).wait_send()

    return start, finish


def _adamw(w, g, m, v):
    m = ADAM_B1 * m + (1.0 - ADAM_B1) * g
    v = ADAM_B2 * v + (1.0 - ADAM_B2) * (g * g)
    m_hat = m / (1.0 - ADAM_B1 ** ADAM_STEP)
    v_hat = v / (1.0 - ADAM_B2 ** ADAM_STEP)
    return -ADAM_LR * (m_hat / (jnp.sqrt(v_hat) + ADAM_EPS) + ADAM_WD * w), m, v


def _adamw_shard(name, core, w, m, v, got, sib):
    rows, cols = w.shape
    tr = rows // 4
    spec = pl.BlockSpec((tr, cols), lambda i, c_ref: (i, 0))
    spec4 = pl.BlockSpec((N_CHIPS, tr, cols), lambda i, c_ref: (0, i % 2, 0))

    def body(c_ref, w_ref, m_ref, v_ref, got_ref, sib_ref, g_ref, d_ref, mo_ref, vo_ref):
        def four(ref):
            return ((ref[0].astype(F32) + ref[1].astype(F32)) + ref[2].astype(F32)) + ref[3].astype(F32)

        g = jnp.where(pl.program_id(0) // 2 == c_ref[0], four(got_ref), four(sib_ref))
        g_ref[...] = g
        d_ref[...], mo_ref[...], vo_ref[...] = _adamw(w_ref[...], g, m_ref[...], v_ref[...])

    return pl.pallas_call(
        body, name=name,
        grid_spec=pltpu.PrefetchScalarGridSpec(
            num_scalar_prefetch=1, grid=(4,),
            in_specs=[spec, spec, spec, spec4, spec4], out_specs=[spec] * 4),
        out_shape=[jax.ShapeDtypeStruct((rows, cols), F32)] * 4,
        compiler_params=_params("arbitrary"),
    )(core, w, m, v, got, sib)


def _adamw_small(name, w, m, v, g):
    def body(w_ref, m_ref, v_ref, g_ref, d_ref, mo_ref, vo_ref):
        d_ref[...], mo_ref[...], vo_ref[...] = _adamw(w_ref[...], g_ref[...], m_ref[...], v_ref[...])

    return pl.pallas_call(body, name=name, out_shape=[jax.ShapeDtypeStruct(w.shape, F32)] * 3)(w, m, v, g)


def _cast_bf16(name, arrays):
    n = len(arrays)

    def body(*refs):
        for i_ref, o_ref in zip(refs[:n], refs[n:]):
            o_ref[...] = i_ref[...].astype(BF16)

    return pl.pallas_call(
        body, name=name, out_shape=[jax.ShapeDtypeStruct(a.shape, BF16) for a in arrays],
        compiler_params=pltpu.CompilerParams(vmem_limit_bytes=VMEM_LIMIT),
    )(*arrays)


def _pack(arrays, rows):
    flat = jnp.concatenate([a.reshape(-1) for a in arrays])
    return jnp.concatenate([flat, jnp.zeros((rows * 128 - flat.shape[0],), F32)]).reshape(rows, 128)


def _unpack(packed, shapes):
    flat, out, at = packed.reshape(-1), [], 0
    for s in shapes:
        size = 1
        for dim in s:
            size *= dim
        out.append(flat[at:at + size].reshape(s))
        at += size
    return out


def _rows_for(shapes):
    total = 0
    for s in shapes:
        size = 1
        for dim in s:
            size *= dim
        total += size
    return -(-total // 1024) * 8


WEIGHTS = ['meta_tokens', 'ffn1_norm', 'ffn1_w_gate', 'ffn1_w_up', 'ffn1_w_down', 'mix_norm', 'w_in', 'rwkv_mu',
           'rwkv_w0', 'rwkv_w_up', 'rwkv_a0', 'rwkv_a_up', 'rwkv_g_up', 'rwkv_k_k', 'rwkv_k_a', 'rwkv_r_k',
           'rwkv_lnx_w', 'rwkv_lnx_b', 'w_out', 'ffn2_norm', 'ffn2_w_gate', 'ffn2_w_up', 'ffn2_w_down', 'final_norm']
COL_CUT = ['ffn1_w_gate', 'ffn1_w_up', 'w_in', 'ffn2_w_gate', 'ffn2_w_up']
ROW_CUT = ['ffn1_w_down', 'w_out', 'ffn2_w_down']
SMALL_CUT = ['meta_tokens', 'rwkv_w_up', 'rwkv_a_up', 'rwkv_g_up']
TRANSPOSED = ['ffn1_w_gate', 'ffn1_w_up', 'ffn2_w_gate', 'ffn2_w_up']
BIG = COL_CUT + ROW_CUT
REPLICATED = [n for n in WEIGHTS if n not in BIG + SMALL_CUT]


def _join_cols(a):
    return a.transpose(1, 0, 2).reshape(a.shape[1], N_CHIPS * a.shape[2])


def _cut_cols(a):
    return a.reshape(a.shape[0], N_CHIPS, a.shape[1] // N_CHIPS).transpose(1, 0, 2)


def _step(x, loss_target, w, m, v):
    two = lambda a: a.reshape(a.shape[-2], a.shape[-1])

    def rows_cut(n, a):
        return jnp.swapaxes(two(a), 0, 1) if n in TRANSPOSED else two(a)

    def as_given(n, a, like):
        return (jnp.swapaxes(a, 0, 1) if n in TRANSPOSED else a).reshape(like.shape)

    col_cut = [n for n in COL_CUT + SMALL_CUT if n not in TRANSPOSED]

    def join(names, gathered):
        return {n: (_join_cols(a) if n in col_cut else a.reshape(-1, a.shape[-1])) for n, a in zip(names, gathered)}

    def pair_sums(tag, names, g):
        parts = [_cut_cols(g[n]) if n in col_cut else g[n].reshape(N_CHIPS, -1, g[n].shape[-1]) for n in names]
        arrived = _pair_exchange("pair_exchange_" + tag, parts)
        return [_pair_add("pair_add_" + n, p, o) for n, p, o in zip(names, parts, arrived)]

    first = [n for n in BIG if n not in MID + LATE]
    gathered_later = {"mid": MID, "late": LATE}
    groups = {**gathered_later, "gate": ["ffn1_w_gate"], "up": ["ffn1_w_up"]}
    cast = dict(zip(BIG, _cast_bf16("cast_weights", [rows_cut(n, w[n]) for n in BIG])))
    names = first + SMALL_CUT
    shards = [cast[n] for n in first] + [two(w[n]) for n in SMALL_CUT]
    full = {n: (two(w[n]) if w[n].ndim == 3 else w[n]) for n in REPLICATED}
    full.update(join(names, _gather_shards("gather_weights", shards)))
    full["rwkv_r_k"] = w["rwkv_r_k"]
    full["final_norm"] = w["final_norm"]

    small_names = REPLICATED + SMALL_CUT

    def small(g):
        return _pack([g[n] for n in small_names], _rows_for([g[n].shape for n in small_names]))

    late = types.SimpleNamespace(shards={k: [cast[n] for n in names] for k, names in gathered_later.items()},
                                 join=lambda k, gathered: join(groups[k], gathered),
                                 parts=lambda k, g: pair_sums(k, groups[k], g), small=small)

    loss, dx, g, reduced = _local_step(x[0], loss_target[0], full, late)
    loss = lax.psum(loss, ("x", "y", "c"))

    groups["down"] = ["ffn1_w_down"]
    reduced["down"] = list(_reduce_shards("reduce_gradients", pair_sums("down", groups["down"], g)))
    got, sib = {}, {}
    for k, names in groups.items():
        got.update(zip(names, reduced[k][:len(names)]))
        sib.update(zip(names, reduced[k][len(names):]))
    g_small = dict(zip(small_names, _unpack(reduced["small"], [g[n].shape for n in small_names])))
    chip = 2 * lax.axis_index("x") + lax.axis_index("y")
    for n in SMALL_CUT:
        width = g_small[n].shape[1] // N_CHIPS
        g_small[n] = lax.dynamic_slice_in_dim(g_small[n], chip * width, width, axis=1)

    grad, delta, new_m, new_v = {}, {}, {}, {}
    core = lax.axis_index("c").astype(jnp.int32).reshape(1)
    for n in BIG:
        outs = _adamw_shard("adamw_" + n, core, rows_cut(n, w[n]), rows_cut(n, m[n]), rows_cut(n, v[n]), got[n], sib[n])
        grad[n], delta[n], new_m[n], new_v[n] = (as_given(n, o, w[n]) for o in outs)
    shapes = [w[n].shape for n in small_names]
    rows = _rows_for(shapes)
    packed = [_pack([t[n] for n in small_names], rows) for t in (w, m, v)]
    g_packed = _pack([g_small[n] for n in small_names], rows)
    outs = [_unpack(o, shapes) for o in _adamw_small("adamw_small", *packed, g_packed)]
    for i, n in enumerate(small_names):
        grad[n] = g_small[n].reshape(w[n].shape)
        delta[n], new_m[n], new_v[n] = outs[0][i], outs[1][i], outs[2][i]
    return loss, dx[None], grad, delta, new_m, new_v


def kernel(x, meta_tokens, ffn1_norm, ffn1_w_gate, ffn1_w_up, ffn1_w_down, mix_norm, w_in, rwkv_mu, rwkv_w0, rwkv_w_up, rwkv_a0, rwkv_a_up, rwkv_g_up, rwkv_k_k, rwkv_k_a, rwkv_r_k, rwkv_lnx_w, rwkv_lnx_b, w_out, ffn2_norm, ffn2_w_gate, ffn2_w_up, ffn2_w_down, final_norm, loss_target, m_meta_tokens, m_ffn1_norm, m_ffn1_w_gate, m_ffn1_w_up, m_ffn1_w_down, m_mix_norm, m_w_in, m_rwkv_mu, m_rwkv_w0, m_rwkv_w_up, m_rwkv_a0, m_rwkv_a_up, m_rwkv_g_up, m_rwkv_k_k, m_rwkv_k_a, m_rwkv_r_k, m_rwkv_lnx_w, m_rwkv_lnx_b, m_w_out, m_ffn2_norm, m_ffn2_w_gate, m_ffn2_w_up, m_ffn2_w_down, m_final_norm, v_meta_tokens, v_ffn1_norm, v_ffn1_w_gate, v_ffn1_w_up, v_ffn1_w_down, v_mix_norm, v_w_in, v_rwkv_mu, v_rwkv_w0, v_rwkv_w_up, v_rwkv_a0, v_rwkv_a_up, v_rwkv_g_up, v_rwkv_k_k, v_rwkv_k_a, v_rwkv_r_k, v_rwkv_lnx_w, v_rwkv_lnx_b, v_w_out, v_ffn2_norm, v_ffn2_w_gate, v_ffn2_w_up, v_ffn2_w_down, v_final_norm):
    w = dict(zip(WEIGHTS, (meta_tokens, ffn1_norm, ffn1_w_gate, ffn1_w_up, ffn1_w_down, mix_norm, w_in, rwkv_mu, rwkv_w0, rwkv_w_up, rwkv_a0, rwkv_a_up, rwkv_g_up, rwkv_k_k, rwkv_k_a, rwkv_r_k, rwkv_lnx_w, rwkv_lnx_b, w_out, ffn2_norm, ffn2_w_gate, ffn2_w_up, ffn2_w_down, final_norm)))
    m = dict(zip(WEIGHTS, (m_meta_tokens, m_ffn1_norm, m_ffn1_w_gate, m_ffn1_w_up, m_ffn1_w_down, m_mix_norm, m_w_in, m_rwkv_mu, m_rwkv_w0, m_rwkv_w_up, m_rwkv_a0, m_rwkv_a_up, m_rwkv_g_up, m_rwkv_k_k, m_rwkv_k_a, m_rwkv_r_k, m_rwkv_lnx_w, m_rwkv_lnx_b, m_w_out, m_ffn2_norm, m_ffn2_w_gate, m_ffn2_w_up, m_ffn2_w_down, m_final_norm)))
    v = dict(zip(WEIGHTS, (v_meta_tokens, v_ffn1_norm, v_ffn1_w_gate, v_ffn1_w_up, v_ffn1_w_down, v_mix_norm, v_w_in, v_rwkv_mu, v_rwkv_w0, v_rwkv_w_up, v_rwkv_a0, v_rwkv_a_up, v_rwkv_g_up, v_rwkv_k_k, v_rwkv_k_a, v_rwkv_r_k, v_rwkv_lnx_w, v_rwkv_lnx_b, v_w_out, v_ffn2_norm, v_ffn2_w_gate, v_ffn2_w_up, v_ffn2_w_down, v_final_norm)))
    loss, grad_x, grad, delta, new_m, new_v = _step(x, loss_target, w, m, v)
    return (loss, grad_x, *[grad[n] for n in WEIGHTS], *[delta[n] for n in WEIGHTS],
            *[new_m[n] for n in WEIGHTS], *[new_v[n] for n in WEIGHTS])
```

```python
import functools
import types

import jax
import jax.numpy as jnp
from jax import lax
from jax.experimental import pallas as pl
from jax.experimental.pallas import tpu as pltpu

F32 = jnp.float32
BF16 = jnp.bfloat16

RMS_EPS = 1e-6
LNX_EPS = 64e-5
N_META = 16
ROW0 = 128
META_PAD = ROW0 - N_META
HEAD = 64
N_HEADS = 8
GROUP = N_HEADS * HEAD
LORA_W, LORA_A, LORA_G = 32, 32, 96
LORA_PAD = 256
RW_COLS = 3 * GROUP + LORA_PAD
IN_COLS_PAD = 3 * GROUP + RW_COLS
ATT_BLOCK = 128
CHUNK = 64
SCAN_CHUNKS = 2
VMEM_LIMIT = 56 * 1024 * 1024

ADAM_LR, ADAM_B1, ADAM_B2, ADAM_EPS, ADAM_WD, ADAM_STEP = 0.001, 0.9, 0.999, 1e-08, 0.01, 10

MESH = pl.DeviceIdType.MESH


def _params(*sem):
    return pltpu.CompilerParams(dimension_semantics=tuple(sem), vmem_limit_bytes=VMEM_LIMIT)


def _dot(a, b):
    return lax.dot_general(a, b, (((1,), (0,)), ((), ())), preferred_element_type=F32)


def _dot_nt(a, b):
    return lax.dot_general(a, b, (((1,), (1,)), ((), ())), preferred_element_type=F32)


def _dot_tn(a, b):
    return lax.dot_general(a, b, (((0,), (0,)), ((), ())), preferred_element_type=F32)


def _split2(x):
    hi = x.astype(BF16)
    return hi, (x - hi.astype(F32)).astype(BF16)


def _sigmoid(x):
    return 1.0 / (1.0 + jnp.exp(-x))


def _rms_fwd(x, g):
    rstd = lax.rsqrt(jnp.mean(x * x, axis=-1, keepdims=True) + RMS_EPS)
    xhat = x * rstd
    return xhat * g, xhat, rstd


def _rms_bwd(dn, xhat, rstd, g):
    dxhat = dn * g
    dx = rstd * (dxhat - xhat * jnp.mean(dxhat * xhat, axis=-1, keepdims=True))
    return dx, jnp.sum(dn * xhat, axis=0, keepdims=True)


def _row_tile(rows):
    return 384 if rows % 384 == 0 else 128


def _half_tile(cols):
    return cols // 2 if cols % 256 == 0 else cols


def _tall_tile(rows, parts):
    return rows // parts if rows % (16 * parts) == 0 else _row_tile(rows)


def _call_with_exchange(name, body, grid, in_specs, out_specs, out_shape, scratch, operands, params, exchange):
    if exchange is None or not exchange[1]:
        return pl.pallas_call(body, name=name, grid=grid, in_specs=in_specs, out_specs=out_specs,
                              out_shape=out_shape, scratch_shapes=scratch, compiler_params=params)(*operands)
    kind, arrays = exchange
    ns, n_in, n_out, n_scr = len(arrays), len(in_specs), len(out_specs), len(scratch)
    whole = lambda a: pl.BlockSpec(a.shape, lambda *_: (0,) * a.ndim)
    if kind == "gather":
        results = [jax.ShapeDtypeStruct((N_CHIPS,) + s.shape, s.dtype) for s in arrays]
        sems, sent_specs, landed_specs = _gather_sems(ns), [HBM] * ns, [HBM] * ns
    elif kind == "reduce":
        results = [jax.ShapeDtypeStruct(s.shape, s.dtype) for s in arrays] * 2
        sems, sent_specs, landed_specs = _reduce_sems(ns), [HBM] * ns, [HBM] * (2 * ns)
    else:
        results = [jax.ShapeDtypeStruct(arrays[0].shape, F32)]
        sems, sent_specs, landed_specs = _all_reduce_scratch(arrays[0]), [whole(arrays[0])], [whole(arrays[0])]
    n_res = len(results)

    def carried(*refs):
        at = n_in + ns + n_out
        sent, landed = refs[n_in:n_in + ns], refs[at:at + n_res]
        own_scratch, sem_refs = refs[at + n_res:at + n_res + n_scr], refs[at + n_res + n_scr:]
        first, last = _first_and_last_step(grid)
        if kind == "gather":
            start, finish = _gather_exchange(sent, landed, sem_refs)
        elif kind == "reduce":
            start, finish = _reduce_exchange(sent, landed[:ns], landed[ns:], sem_refs)
        else:
            start, finish = _all_reduce_exchange(sent[0], landed[0], *sem_refs)
        pl.when(first)(start)
        body(*refs[:n_in], *refs[n_in + ns:at], *own_scratch)
        pl.when(last)(finish)

    return pl.pallas_call(
        carried, name=name, grid=grid, in_specs=list(in_specs) + sent_specs, out_specs=list(out_specs) + landed_specs,
        out_shape=list(out_shape) + results, scratch_shapes=list(scratch) + sems, compiler_params=params,
    )(*operands, *arrays)


def _ffn_fwd(name, h, g, wg, wu, wd, exchange=None):
    rows, d = h.shape
    f = wg.shape[0]
    tm, tf = _row_tile(rows), _half_tile(f)
    nj = f // tf

    def body(h_ref, g_ref, wg_ref, wu_ref, wd_ref, ho_ref, a_ref, b_ref, n_sc, acc_sc):
        j = pl.program_id(1)

        @pl.when(j == 0)
        def _():
            n, _, _ = _rms_fwd(h_ref[...], g_ref[...])
            n_sc[...] = n.astype(BF16)
            acc_sc[...] = jnp.zeros_like(acc_sc)

        n = n_sc[...]
        a = _dot_nt(n, wg_ref[...])
        b = _dot_nt(n, wu_ref[...])
        a_ref[...] = a
        b_ref[...] = b
        s = a * _sigmoid(a) * b
        acc_sc[...] += _dot(s.astype(BF16), wd_ref[...])

        @pl.when(j == nj - 1)
        def _():
            ho_ref[...] = h_ref[...] + 0.5 * acc_sc[...]

    return _call_with_exchange(
        name, body, (rows // tm, nj),
        [pl.BlockSpec((tm, d), lambda i, j: (i, 0)),
         pl.BlockSpec((1, d), lambda i, j: (0, 0)),
         pl.BlockSpec((tf, d), lambda i, j: (j, 0)),
         pl.BlockSpec((tf, d), lambda i, j: (j, 0)),
         pl.BlockSpec((tf, d), lambda i, j: (j, 0))],
        [pl.BlockSpec((tm, d), lambda i, j: (i, 0)),
         pl.BlockSpec((tm, tf), lambda i, j: (i, j)),
         pl.BlockSpec((tm, tf), lambda i, j: (i, j))],
        [jax.ShapeDtypeStruct((rows, d), F32),
         jax.ShapeDtypeStruct((rows, f), F32),
         jax.ShapeDtypeStruct((rows, f), F32)],
        [pltpu.VMEM((tm, d), BF16), pltpu.VMEM((tm, d), F32)],
        (h, g, wg, wu, wd), _params("arbitrary", "arbitrary"), exchange)


def _ffn_bwd(name, dh, h, g, a, b, wg, wu, wd, exchange=None):
    rows, d = h.shape
    f = wg.shape[0]
    tm, tf = _row_tile(rows), _half_tile(f)
    ni, nj = rows // tm, f // tf

    def body(dh_ref, h_ref, g_ref, a_ref, b_ref, wg_ref, wu_ref, wd_ref,
             dhi_ref, da_ref, db_ref, s_ref, n_ref, dhh_ref, dg_ref, dn_sc):
        i, j = pl.program_id(0), pl.program_id(1)

        @pl.when(j == 0)
        def _():
            n, _, _ = _rms_fwd(h_ref[...], g_ref[...])
            n_ref[...] = n.astype(BF16)
            dhh_ref[...] = (0.5 * dh_ref[...]).astype(BF16)
            dn_sc[...] = jnp.zeros_like(dn_sc)

        @pl.when((i == 0) & (j == 0))
        def _():
            dg_ref[...] = jnp.zeros_like(dg_ref)

        ds = _dot_nt(dhh_ref[...], wd_ref[...])
        av, bv = a_ref[...], b_ref[...]
        sig = _sigmoid(av)
        silu = av * sig
        s_ref[...] = (silu * bv).astype(BF16)
        db = (ds * silu).astype(BF16)
        da = (ds * bv * (sig * (1.0 + av * (1.0 - sig)))).astype(BF16)
        da_ref[...] = da
        db_ref[...] = db
        dn_sc[...] += _dot(da, wg_ref[...]) + _dot(db, wu_ref[...])

        @pl.when(j == nj - 1)
        def _():
            gv = g_ref[...]
            _, xhat, rstd = _rms_fwd(h_ref[...], gv)
            dx, dg = _rms_bwd(dn_sc[...], xhat, rstd, gv)
            dhi_ref[...] = dh_ref[...] + dx
            dg_ref[...] += dg

    return _call_with_exchange(
        name, body, (ni, nj),
        [pl.BlockSpec((tm, d), lambda i, j: (i, 0)),
         pl.BlockSpec((tm, d), lambda i, j: (i, 0)),
         pl.BlockSpec((1, d), lambda i, j: (0, 0)),
         pl.BlockSpec((tm, tf), lambda i, j: (i, j)),
         pl.BlockSpec((tm, tf), lambda i, j: (i, j)),
         pl.BlockSpec((tf, d), lambda i, j: (j, 0)),
         pl.BlockSpec((tf, d), lambda i, j: (j, 0)),
         pl.BlockSpec((tf, d), lambda i, j: (j, 0))],
        [pl.BlockSpec((tm, d), lambda i, j: (i, 0)),
         pl.BlockSpec((tm, tf), lambda i, j: (i, j)),
         pl.BlockSpec((tm, tf), lambda i, j: (i, j)),
         pl.BlockSpec((tm, tf), lambda i, j: (i, j)),
         pl.BlockSpec((tm, d), lambda i, j: (i, 0)),
         pl.BlockSpec((tm, d), lambda i, j: (i, 0)),
         pl.BlockSpec((1, d), lambda i, j: (0, 0))],
        [jax.ShapeDtypeStruct((rows, d), F32),
         jax.ShapeDtypeStruct((rows, f), BF16),
         jax.ShapeDtypeStruct((rows, f), BF16),
         jax.ShapeDtypeStruct((rows, f), BF16),
         jax.ShapeDtypeStruct((rows, d), BF16),
         jax.ShapeDtypeStruct((rows, d), BF16),
         jax.ShapeDtypeStruct((1, d), F32)],
        [pltpu.VMEM((tm, d), F32)],
        (dh, h, g, a, b, wg, wu, wd), _params("arbitrary", "arbitrary"), exchange)


def _mm_tn(name, a, b, exchange=None):
    k, m = a.shape
    n = b.shape[1]
    tk = _tall_tile(k, 3)
    tm = _half_tile(m) if m > 1024 else m
    tn = _half_tile(n) if n > 1024 else n
    nk = k // tk

    def body(a_ref, b_ref, o_ref, acc):
        kk = pl.program_id(2)

        @pl.when(kk == 0)
        def _():
            acc[...] = jnp.zeros_like(acc)

        acc[...] += _dot_tn(a_ref[...], b_ref[...])

        @pl.when(kk == nk - 1)
        def _():
            o_ref[...] = acc[...].astype(BF16)

    outs = _call_with_exchange(
        name, body, (m // tm, n // tn, nk),
        [pl.BlockSpec((tk, tm), lambda i, j, kk: (kk, i)),
         pl.BlockSpec((tk, tn), lambda i, j, kk: (kk, j))],
        [pl.BlockSpec((tm, tn), lambda i, j, kk: (i, j))],
        [jax.ShapeDtypeStruct((m, n), BF16)],
        [pltpu.VMEM((tm, tn), F32)],
        (a, b), _params("arbitrary", "arbitrary", "arbitrary"), exchange)
    return outs if exchange else outs[0]


def _norm_proj(name, h, g, w):
    rows, d = h.shape
    n = w.shape[1]
    split = 3 * GROUP
    tm = _row_tile(rows)

    def body(h_ref, g_ref, w_ref, qkv_ref, p_ref, n_ref):
        nv, _, _ = _rms_fwd(h_ref[...], g_ref[...])
        nb = nv.astype(BF16)
        n_ref[...] = nb
        qkv_ref[...] = _dot(nb, w_ref[:, :split]).astype(BF16)
        p_ref[...] = _dot(nb, w_ref[:, split:])

    return pl.pallas_call(
        body, name=name, grid=(rows // tm,),
        in_specs=[pl.BlockSpec((tm, d), lambda i: (i, 0)),
                  pl.BlockSpec((1, d), lambda i: (0, 0)),
                  pl.BlockSpec((d, n), lambda i: (0, 0))],
        out_specs=[pl.BlockSpec((tm, split), lambda i: (i, 0)),
                   pl.BlockSpec((tm, n - split), lambda i: (i, 0)),
                   pl.BlockSpec((tm, d), lambda i: (i, 0))],
        out_shape=[jax.ShapeDtypeStruct((rows, split), BF16), jax.ShapeDtypeStruct((rows, n - split), F32),
                   jax.ShapeDtypeStruct((rows, d), BF16)],
        compiler_params=_params("arbitrary"),
    )(h, g, w)


def _out_proj(name, h, sb, rw, w):
    rows, d = h.shape
    gw = sb.shape[1]
    tm = _row_tile(rows)

    def body(h_ref, sb_ref, rw_ref, w_ref, o_ref, mix_ref):
        mix_ref[:, :gw] = sb_ref[...].astype(BF16)
        mix_ref[:, gw:] = rw_ref[...].astype(BF16)
        o_ref[...] = h_ref[...] + _dot(mix_ref[...], w_ref[...])

    return pl.pallas_call(
        body, name=name, grid=(rows // tm,),
        in_specs=[pl.BlockSpec((tm, d), lambda i: (i, 0)),
                  pl.BlockSpec((tm, gw), lambda i: (i, 0)),
                  pl.BlockSpec((tm, gw), lambda i: (i, 0)),
                  pl.BlockSpec((2 * gw, d), lambda i: (0, 0))],
        out_specs=[pl.BlockSpec((tm, d), lambda i: (i, 0)),
                   pl.BlockSpec((tm, 2 * gw), lambda i: (i, 0))],
        out_shape=[jax.ShapeDtypeStruct((rows, d), F32), jax.ShapeDtypeStruct((rows, 2 * gw), BF16)],
        compiler_params=_params("arbitrary"),
    )(h, sb, rw, w)


def _out_proj_bwd(name, dh, w):
    rows, d = dh.shape
    k = w.shape[0]
    tm = _row_tile(rows)

    def body(dh_ref, w_ref, dsb_ref, drw_ref, dhb_ref):
        dhb = dh_ref[...].astype(BF16)
        dhb_ref[...] = dhb
        dsb_ref[...] = _dot_nt(dhb, w_ref[:GROUP, :]).astype(BF16)
        drw_ref[...] = _dot_nt(dhb, w_ref[GROUP:, :])

    return pl.pallas_call(
        body, name=name, grid=(rows // tm,),
        in_specs=[pl.BlockSpec((tm, d), lambda i: (i, 0)),
                  pl.BlockSpec((k, d), lambda i: (0, 0))],
        out_specs=[pl.BlockSpec((tm, GROUP), lambda i: (i, 0)),
                   pl.BlockSpec((tm, GROUP), lambda i: (i, 0)),
                   pl.BlockSpec((tm, d), lambda i: (i, 0))],
        out_shape=[jax.ShapeDtypeStruct((rows, GROUP), BF16), jax.ShapeDtypeStruct((rows, GROUP), F32),
                   jax.ShapeDtypeStruct((rows, d), BF16)],
        compiler_params=_params("arbitrary"),
    )(dh, w)


def _norm_proj_bwd(name, dproj, w, h, g, dh):
    rows, n = dproj.shape
    d = w.shape[0]
    tm = _row_tile(rows)

    def body(dp_ref, w_ref, h_ref, g_ref, dh_ref, o_ref, dg_ref):
        @pl.when(pl.program_id(0) == 0)
        def _():
            dg_ref[...] = jnp.zeros_like(dg_ref)

        dn = _dot_nt(dp_ref[...], w_ref[...])
        gv = g_ref[...]
        _, xhat, rstd = _rms_fwd(h_ref[...], gv)
        dx, dg = _rms_bwd(dn, xhat, rstd, gv)
        o_ref[...] = dh_ref[...] + dx
        dg_ref[...] += dg

    return pl.pallas_call(
        body, name=name, grid=(rows // tm,),
        in_specs=[pl.BlockSpec((tm, n), lambda i: (i, 0)),
                  pl.BlockSpec((d, n), lambda i: (0, 0)),
                  pl.BlockSpec((tm, d), lambda i: (i, 0)),
                  pl.BlockSpec((1, d), lambda i: (0, 0)),
                  pl.BlockSpec((tm, d), lambda i: (i, 0))],
        out_specs=[pl.BlockSpec((tm, d), lambda i: (i, 0)),
                   pl.BlockSpec((1, d), lambda i: (0, 0))],
        out_shape=[jax.ShapeDtypeStruct((rows, d), F32), jax.ShapeDtypeStruct((1, d), F32)],
        compiler_params=_params("arbitrary"),
    )(dproj, w, h, g, dh)


def _loss_head(name, h, g, tgt):
    rows, d = h.shape
    tm = _row_tile(rows)

    def body(h_ref, g_ref, t_ref, loss_ref, dh_ref, dg_ref):
        i = pl.program_id(0)

        @pl.when(i == 0)
        def _():
            loss_ref[...] = jnp.zeros_like(loss_ref)
            dg_ref[...] = jnp.zeros_like(dg_ref)

        gv = g_ref[...]
        y, xhat, rstd = _rms_fwd(h_ref[...], gv)
        row = i * tm + lax.broadcasted_iota(jnp.int32, (tm, 1), 0)
        diff = jnp.where(row >= ROW0, y - t_ref[...], 0.0)
        part = 0.5 * jnp.sum(jnp.sum(diff * diff, axis=-1, keepdims=True), axis=0, keepdims=True) / d
        loss_ref[...] += jnp.broadcast_to(part, loss_ref.shape)
        dx, dg = _rms_bwd(diff / d, xhat, rstd, gv)
        dh_ref[...] = dx
        dg_ref[...] += dg

    return pl.pallas_call(
        body, name=name, grid=(rows // tm,),
        in_specs=[pl.BlockSpec((tm, d), lambda i: (i, 0)),
                  pl.BlockSpec((1, d), lambda i: (0, 0)),
                  pl.BlockSpec((tm, d), lambda i: (i, 0))],
        out_specs=[pl.BlockSpec((8, 128), lambda i: (0, 0)),
                   pl.BlockSpec((tm, d), lambda i: (i, 0)),
                   pl.BlockSpec((1, d), lambda i: (0, 0))],
        out_shape=[jax.ShapeDtypeStruct((8, 128), F32),
                   jax.ShapeDtypeStruct((rows, d), F32),
                   jax.ShapeDtypeStruct((1, d), F32)],
        compiler_params=_params("arbitrary"),
    )(h, g, tgt)


def _sb_block(qb, kb, q0, jb, scale):
    bq, bk = qb.shape[0], kb.shape[0]
    z = _dot_nt(qb, kb) * scale
    qpos = q0 + lax.broadcasted_iota(jnp.int32, (bq, bk), 0)
    kpos = jb * bk + lax.broadcasted_iota(jnp.int32, (bq, bk), 1)
    valid = (kpos < qpos) & (kpos >= META_PAD)
    e = jnp.exp(-jnp.abs(z))
    log_keep = jnp.where(valid, -(jnp.maximum(z, 0.0) + jnp.log(1.0 + e)), 0.0)
    return z, valid, e, log_keep


def _tri2(n, cmp):
    r = lax.broadcasted_iota(jnp.int32, (2 * n, n), 0) % n
    c = lax.broadcasted_iota(jnp.int32, (2 * n, n), 1)
    return cmp(r, c).astype(BF16)


def _dot_split(x, t2):
    hi, lo = _split2(x)
    return _dot(jnp.concatenate([hi, lo], axis=1), t2)


ATT_HEADS = 8
ATT_WIDTH = ATT_HEADS * HEAD
ATT_CUT = -104.0
ATT_TILES = GROUP // ATT_WIDTH


def _lanes(hh):
    return slice(hh * HEAD, (hh + 1) * HEAD)


def _first_and_last_step(grid):
    here = [pl.program_id(a) for a in range(len(grid))]
    first, last = here[0] == 0, here[0] == grid[0] - 1
    for a in range(1, len(grid)):
        first, last = first & (here[a] == 0), last & (here[a] == grid[a] - 1)
    return first, last


def _sb_fwd(name, qkv, shards=()):
    rows = qkv.shape[0]
    nh, dh = N_HEADS, HEAD
    bq, bk, hg = _row_tile(rows), ATT_BLOCK, ATT_HEADS
    per = bq // bk
    scale = dh ** -0.5
    ns = len(shards)
    grid = (nh // hg, rows // bq)

    def body(q_ref, k_ref, v_ref, *rest):
        o_ref, rt_ref, cnt_ref = rest[ns:ns + 3]
        if ns:
            first, last = _first_and_last_step(grid)
            start, finish = _gather_exchange(rest[:ns], rest[ns + 3:2 * ns + 3], rest[2 * ns + 3:])
            pl.when(first)(start)
        i = pl.program_id(1)
        after = _tri2(bk, lambda r, c: r > c)
        nkb = (i + 1) * per

        def live(state):
            n, carry = state
            top = jnp.max(carry[0][0])
            for hh in range(1, hg):
                top = jnp.maximum(top, jnp.max(carry[hh][0]))
            return (n < nkb) & (top >= ATT_CUT)

        def visit(carry, jb, r0):
            off = pl.multiple_of(jb * bk, bk)
            out = []
            for hh in range(hg):
                rest, acc = carry[hh]
                kb = k_ref[pl.ds(off, bk), _lanes(hh)]
                vb = v_ref[pl.ds(off, bk), _lanes(hh)]
                z, valid, _, log_keep = _sb_block(q_ref[r0:, _lanes(hh)], kb, i * bq + r0, jb, scale)
                log_rest = rest[r0:] + _dot_split(log_keep, after)
                attn = jnp.where(valid, jnp.exp(z + log_keep + log_rest), 0.0)
                new_rest = rest[r0:] + jnp.sum(log_keep, axis=-1, keepdims=True)
                new_acc = acc[r0:] + _dot(attn.astype(BF16), vb)
                if r0:
                    new_rest = jnp.concatenate([rest[:r0], new_rest], axis=0)
                    new_acc = jnp.concatenate([acc[:r0], new_acc], axis=0)
                out.append((new_rest, new_acc))
            return tuple(out)

        carry = tuple((jnp.zeros((bq, 1), F32), jnp.zeros((bq, dh), F32)) for _ in range(hg))
        for dgl in reversed(range(per)):
            carry = visit(carry, i * per + dgl, dgl * bk)
        n, res = lax.while_loop(live, lambda s: (s[0] + 1, visit(s[1], nkb - 1 - s[0], 0)), (jnp.int32(per), carry))
        for hh in range(hg):
            rt_ref[hh] = res[hh][0]
            o_ref[:, _lanes(hh)] = res[hh][1]
            cnt_ref[hh] = jnp.full((bq, 1), n, F32)
        if ns:
            pl.when(last)(finish)

    return pl.pallas_call(
        body, name=name, grid=grid,
        in_specs=[pl.BlockSpec((bq, ATT_WIDTH), lambda h, i: (i, h)),
                  pl.BlockSpec((rows, ATT_WIDTH), lambda h, i: (0, ATT_TILES + h)),
                  pl.BlockSpec((rows, ATT_WIDTH), lambda h, i: (0, 2 * ATT_TILES + h))] + [HBM] * ns,
        out_specs=[pl.BlockSpec((bq, ATT_WIDTH), lambda h, i: (i, h)),
                   pl.BlockSpec((hg, bq, 1), lambda h, i: (h, i, 0)),
                   pl.BlockSpec((hg, bq, 1), lambda h, i: (h, i, 0))] + [HBM] * ns,
        out_shape=[jax.ShapeDtypeStruct((rows, GROUP), F32), jax.ShapeDtypeStruct((nh, rows, 1), F32),
                   jax.ShapeDtypeStruct((nh, rows, 1), F32)]
        + [jax.ShapeDtypeStruct((N_CHIPS,) + s.shape, s.dtype) for s in shards],
        scratch_shapes=_gather_sems(ns) if ns else [],
        compiler_params=_params("arbitrary", "arbitrary"),
    )(qkv, qkv, qkv, *shards)


def _sb_bwd(name, qkv, rt, cnt, do, parts=()):
    rows = qkv.shape[0]
    nh, dh = N_HEADS, HEAD
    bq, bk, hg = _row_tile(rows), ATT_BLOCK, ATT_HEADS
    per = bq // bk
    scale = dh ** -0.5
    ns = len(parts)
    grid = (nh // hg, rows // bq)

    def body(q_ref, k_ref, v_ref, rt_ref, cnt_ref, do_ref, *rest):
        dq_ref, dk_ref, dv_ref = rest[ns:ns + 3]
        if ns:
            at_first, at_last = _first_and_last_step(grid)
            start, finish = _reduce_exchange(rest[:ns], rest[ns + 3:2 * ns + 3], rest[2 * ns + 3:3 * ns + 3],
                                             rest[3 * ns + 3:])
            pl.when(at_first)(start)
        i = pl.program_id(1)

        @pl.when(i == 0)
        def _():
            dk_ref[...] = jnp.zeros_like(dk_ref)
            dv_ref[...] = jnp.zeros_like(dv_ref)

        upto = _tri2(bk, lambda r, c: r <= c)
        before = _tri2(bk, lambda r, c: r < c)
        nkb = (i + 1) * per
        first = nkb - jnp.max(cnt_ref[0]).astype(jnp.int32)

        def visit(carry, jb, r0):
            off = pl.multiple_of(jb * bk, bk)
            out = []
            for hh in range(hg):
                keep_sum, g_sum, dq = carry[hh]
                qb, dob = q_ref[r0:, _lanes(hh)], do_ref[r0:, _lanes(hh)]
                kb = k_ref[pl.ds(off, bk), _lanes(hh)]
                vb = v_ref[pl.ds(off, bk), _lanes(hh)]
                z, valid, e, log_keep = _sb_block(qb, kb, i * bq + r0, jb, scale)
                log_rest = rt_ref[hh, r0:, :] - keep_sum[r0:] - _dot_split(log_keep, upto)
                attn = jnp.where(valid, jnp.exp(z + log_keep + log_rest), 0.0)
                g = attn * _dot_nt(dob, vb)
                g_before = g_sum[r0:] + _dot_split(g, before)
                inv = 1.0 / (1.0 + e)
                sig = jnp.where(z >= 0, inv, e * inv)
                dz = (jnp.where(valid, g * (1.0 - sig) - g_before * sig, 0.0) * scale).astype(BF16)
                dk_ref[pl.ds(off, bk), _lanes(hh)] += _dot_tn(dz, qb)
                dv_ref[pl.ds(off, bk), _lanes(hh)] += _dot_tn(attn.astype(BF16), dob)
                new = (keep_sum[r0:] + jnp.sum(log_keep, axis=-1, keepdims=True),
                       g_sum[r0:] + jnp.sum(g, axis=-1, keepdims=True),
                       dq[r0:] + _dot(dz, kb))
                if r0:
                    new = tuple(jnp.concatenate([old[:r0], x], axis=0) for old, x in zip(carry[hh], new))
                out.append(new)
            return tuple(out)

        zero = jnp.zeros((bq, 1), F32)
        res = lax.fori_loop(first, nkb - per, lambda jb, c: visit(c, jb, 0),
                            tuple((zero, zero, jnp.zeros((bq, dh), F32)) for _ in range(hg)))
        for dgl in range(per):
            res = visit(res, i * per + dgl, dgl * bk)
        for hh in range(hg):
            dq_ref[:, _lanes(hh)] = res[hh][2]
        if ns:
            pl.when(at_last)(finish)

    return pl.pallas_call(
        body, name=name, grid=grid,
        in_specs=[pl.BlockSpec((bq, ATT_WIDTH), lambda h, i: (i, h)),
                  pl.BlockSpec((rows, ATT_WIDTH), lambda h, i: (0, ATT_TILES + h)),
                  pl.BlockSpec((rows, ATT_WIDTH), lambda h, i: (0, 2 * ATT_TILES + h)),
                  pl.BlockSpec((hg, bq, 1), lambda h, i: (h, i, 0)),
                  pl.BlockSpec((hg, bq, 1), lambda h, i: (h, i, 0)),
                  pl.BlockSpec((bq, ATT_WIDTH), lambda h, i: (i, h))] + [HBM] * ns,
        out_specs=[pl.BlockSpec((bq, ATT_WIDTH), lambda h, i: (i, h)),
                   pl.BlockSpec((rows, ATT_WIDTH), lambda h, i: (0, h)),
                   pl.BlockSpec((rows, ATT_WIDTH), lambda h, i: (0, h))] + [HBM] * (2 * ns),
        out_shape=[jax.ShapeDtypeStruct((rows, GROUP), F32)] * 3
        + [jax.ShapeDtypeStruct(s.shape, s.dtype) for s in parts] * 2,
        scratch_shapes=_reduce_sems(ns) if ns else [],
        compiler_params=_params("arbitrary", "arbitrary"),
    )(qkv, qkv, qkv, rt, cnt, do, *parts)


def _head_sum(x, ones_bd):
    return _dot_split(x, ones_bd)


def _rwkv_pre(p, p_prev, mu, w0, a0, k_k, k_a, w_up, a_up, g_up, ones_bd):
    xs = p + (p_prev - p) * mu
    r = xs[:, :GROUP]
    k0 = xs[:, GROUP:2 * GROUP]
    v = xs[:, 2 * GROUP:3 * GROUP]
    lo = xs[:, 3 * GROUP:]
    wa = w0 + _dot(jnp.tanh(lo).astype(BF16), w_up.astype(BF16))
    w = -(jnp.maximum(-wa, 0.0) + jnp.log(1.0 + jnp.exp(-jnp.abs(wa)))) - 0.5
    log_decay = -jnp.exp(w)
    alpha = _sigmoid(a0 + _dot(lo.astype(BF16), a_up.astype(BF16)))
    gate = _dot(_sigmoid(lo).astype(BF16), g_up.astype(BF16))
    kk = k0 * k_k
    kk = kk * lax.rsqrt(jnp.maximum(_head_sum(kk * kk, ones_bd), 1e-24))
    k = k0 * (1.0 + (alpha - 1.0) * k_a)
    return r, log_decay, k, v, -kk, kk * alpha, gate


def _rwkv_post(y, r, k, v, gate, lnx_w, lnx_b, r_k, ones_bd):
    mean = _head_sum(y, ones_bd) * (1.0 / HEAD)
    yc = y - mean
    var = _head_sum(yc * yc, ones_bd) * (1.0 / HEAD)
    yn = yc * lax.rsqrt(var + LNX_EPS) * lnx_w + lnx_b
    bonus = _head_sum(r * k * r_k, ones_bd) * v
    return (yn + bonus) * gate


_PRE_VEC = 5
_PRE_MAT = 3


def _heads(x):
    return jnp.stack([x[:, _lanes(h)] for h in range(N_HEADS)])


def _unheads(x):
    return jnp.concatenate([x[h] for h in range(N_HEADS)], axis=1)


def _edge_spec(tm, width, tile_of):
    return pl.BlockSpec((8, width), lambda i: (jnp.maximum(tile_of(i) * (tm // 8) - 1, 0), 0))


def _previous_rows(p_ref, edge_ref, tile):
    p = p_ref[...]
    edge = jnp.where(tile == 0, 0.0, edge_ref[7:8, :])
    row = lax.broadcasted_iota(jnp.int32, (p.shape[0], 1), 0)
    return jnp.where(row == 0, edge, pltpu.roll(p, 1, axis=0))


def _rwkv_pre_fwd(name, p, vecs, mats, ones_bd):
    rows = p.shape[0]
    tm = _row_tile(rows)
    row_spec = lambda w: pl.BlockSpec((tm, w), lambda i: (i, 0))
    full = lambda a: pl.BlockSpec(a.shape, lambda i: (0,) * a.ndim)

    def body(p_ref, edge_ref, *refs):
        ins = [r[...] for r in refs[:_PRE_VEC + _PRE_MAT + 1]]
        outs = refs[_PRE_VEC + _PRE_MAT + 1:]
        prev = _previous_rows(p_ref, edge_ref, pl.program_id(0))
        for o_ref, val in zip(outs, _rwkv_pre(p_ref[...], prev, *ins)):
            o_ref[...] = val

    return pl.pallas_call(
        body, name=name, grid=(rows // tm,),
        in_specs=([row_spec(RW_COLS), _edge_spec(tm, RW_COLS, lambda i: i)]
                  + [full(a) for a in (*vecs, *mats, ones_bd)]),
        out_specs=[row_spec(GROUP)] * 7,
        out_shape=[jax.ShapeDtypeStruct((rows, GROUP), F32)] * 7,
        compiler_params=_params("arbitrary"),
    )(p, p, *vecs, *mats, ones_bd)


def _rwkv_pre_bwd(name, p, vecs, mats, ones_bd, cts_scan, ct_gate, cts_b):
    rows = p.shape[0]
    tm = _row_tile(rows)
    nt = rows // tm
    n_par = _PRE_VEC + _PRE_MAT
    tile_of = lambda i: nt - 1 - i
    row_spec = lambda w: pl.BlockSpec((tm, w), lambda i: (tile_of(i), 0))
    full = lambda a: pl.BlockSpec(a.shape, lambda i: (0,) * a.ndim)

    def body(*refs):
        p_ref, edge_ref = refs[0], refs[1]
        par = [r[...] for r in refs[2:2 + n_par]]
        ones = refs[2 + n_par][...]
        cta = [r[...] for r in refs[3 + n_par:10 + n_par]]
        ctb = [r[...] for r in refs[10 + n_par:13 + n_par]]
        dp_ref, par_outs, carry = refs[13 + n_par], refs[14 + n_par:-1], refs[-1]
        step = pl.program_id(0)

        @pl.when(step == 0)
        def _():
            carry[...] = jnp.zeros_like(carry)
            for o_ref in par_outs:
                o_ref[...] = jnp.zeros_like(o_ref)

        ct = (cta[0] + ctb[0], cta[1], cta[2] + ctb[1], cta[3] + ctb[2], cta[4], cta[5], cta[6])
        _, vjp = jax.vjp(lambda pv, ppv, *pr: _rwkv_pre(pv, ppv, *pr, ones),
                         p_ref[...], _previous_rows(p_ref, edge_ref, tile_of(step)), *par)
        grads = vjp(ct)
        row = lax.broadcasted_iota(jnp.int32, (tm, 1), 0)
        dp_ref[...] = grads[0] + jnp.where(row == tm - 1, carry[0:1, :], pltpu.roll(grads[1], tm - 1, axis=0))
        carry[0:1, :] = grads[1][0:1, :]
        for o_ref, gval in zip(par_outs, grads[2:]):
            o_ref[...] += gval

    par_arrays = (*vecs, *mats)
    return pl.pallas_call(
        body, name=name, grid=(nt,),
        in_specs=([row_spec(RW_COLS), _edge_spec(tm, RW_COLS, tile_of)] + [full(a) for a in (*par_arrays, ones_bd)]
                  + [row_spec(GROUP)] * 10),
        out_specs=[row_spec(RW_COLS)] + [full(a) for a in par_arrays],
        out_shape=[jax.ShapeDtypeStruct((rows, RW_COLS), F32)] + [jax.ShapeDtypeStruct(a.shape, F32) for a in par_arrays],
        scratch_shapes=[pltpu.VMEM((8, RW_COLS), F32)],
        compiler_params=_params("arbitrary"),
    )(p, p, *par_arrays, ones_bd, *cts_scan, ct_gate, *cts_b)


def _rwkv_post_fwd(name, y, r, k, v, gate, vecs, ones_bd):
    rows = r.shape[0]
    tm = _row_tile(rows)
    row_spec = pl.BlockSpec((tm, GROUP), lambda i: (i, 0))
    full = lambda a: pl.BlockSpec(a.shape, lambda i: (0,) * a.ndim)

    def body(*refs):
        refs[-1][...] = _rwkv_post(*(r_[...] for r_ in refs[:-1]))

    return pl.pallas_call(
        body, name=name, grid=(rows // tm,),
        in_specs=[row_spec] * 5 + [full(a) for a in (*vecs, ones_bd)],
        out_specs=row_spec,
        out_shape=jax.ShapeDtypeStruct((rows, GROUP), F32),
        compiler_params=_params("arbitrary"),
    )(y, r, k, v, gate, *vecs, ones_bd)


def _rwkv_post_bwd(name, y, r, k, v, gate, vecs, ones_bd, dout):
    rows = r.shape[0]
    tm = _row_tile(rows)
    row_spec = pl.BlockSpec((tm, GROUP), lambda i: (i, 0))
    full = lambda a: pl.BlockSpec(a.shape, lambda i: (0,) * a.ndim)

    def body(*refs):
        vals = [r_[...] for r_ in refs[:8]]
        ones = refs[8][...]
        dout_v = refs[9][...]
        outs = refs[10:]
        _, vjp = jax.vjp(lambda *a: _rwkv_post(*a, ones), *vals)
        grads = vjp(dout_v)
        for o_ref, gval in zip(outs[:5], grads[:5]):
            o_ref[...] = gval

        @pl.when(pl.program_id(0) == 0)
        def _():
            for o_ref in outs[5:]:
                o_ref[...] = jnp.zeros_like(o_ref)

        for o_ref, gval in zip(outs[5:], grads[5:]):
            o_ref[...] += gval

    return pl.pallas_call(
        body, name=name, grid=(rows // tm,),
        in_specs=[row_spec] * 5 + [full(a) for a in (*vecs, ones_bd)] + [row_spec],
        out_specs=[row_spec] * 5 + [full(a) for a in vecs],
        out_shape=[jax.ShapeDtypeStruct((rows, GROUP), F32)] * 5 + [jax.ShapeDtypeStruct(a.shape, F32) for a in vecs],
        compiler_params=_params("arbitrary"),
    )(y, r, k, v, gate, *vecs, ones_bd, dout)


_NN = (((2,), (1,)), ((0,), (0,)))
_NT = (((2,), (2,)), ((0,), (0,)))
_TN = (((1,), (1,)), ((0,), (0,)))


_BWD_FORMS = {"nn": (("nt", False), ("tn", False)),
              "nt": (("nn", False), ("tn", True)),
              "tn": (("nt", True), ("nn", False))}
_DIMS = {"nn": _NN, "nt": _NT, "tn": _TN}


def _bdot(a, b, form):
    return lax.dot_general(a.astype(BF16), b.astype(BF16), _DIMS[form], preferred_element_type=F32)


@functools.partial(jax.custom_vjp, nondiff_argnums=(2,))
def _bmm(a, b, form):
    return _bdot(a, b, form)


def _bmm_fwd(a, b, form):
    return _bdot(a, b, form), (a.astype(BF16), b.astype(BF16))


def _bmm_bwd(form, res, dc):
    a, b = res
    (fa, swap_a), (fb, swap_b) = _BWD_FORMS[form]
    da = _bdot(b, dc, fa) if swap_a else _bdot(dc, b, fa)
    db = _bdot(dc, a, fb) if swap_b else _bdot(a, dc, fb)
    return da, db


_bmm.defvjp(_bmm_fwd, _bmm_bwd)


@jax.custom_vjp
def _cumsum_steps(x):
    return _tri_apply(x, lambda r, c: r >= c)


def _tri_apply(x, cmp):
    nh, c, _ = x.shape
    tri = cmp(lax.broadcasted_iota(jnp.int32, (c, c), 0), lax.broadcasted_iota(jnp.int32, (c, c), 1))
    tri = jnp.broadcast_to(tri.astype(BF16)[None], (nh, c, c))
    hi, lo = _split2(x)
    return (lax.dot_general(tri, hi, _NN, preferred_element_type=F32)
            + lax.dot_general(tri, lo, _NN, preferred_element_type=F32))


_cumsum_steps.defvjp(lambda x: (_cumsum_steps(x), None), lambda _, d: (_tri_apply(d, lambda r, c: r <= c),))


@jax.custom_vjp
def _neumann(n_mat):
    c = n_mat.shape[1]
    inv, power, span = n_mat, _bmm(n_mat, n_mat, "nn"), 2
    while span < c:
        both = _bmm(jnp.concatenate([power, inv], axis=1), power, "nn")
        inv = inv + power + both[:, c:]
        power = both[:, :c]
        span *= 2
    return inv


def _neumann_fwd(n_mat):
    inv = _neumann(n_mat)
    return inv, inv


def _neumann_bwd(inv, d):
    left = d + _bmm(inv, d, "tn")
    return (left + _bmm(left, inv, "nt"),)


_neumann.defvjp(_neumann_fwd, _neumann_bwd)


def _chunk(state, r, log_w, k, v, a, b):
    nh, c, _ = r.shape
    row = lax.broadcasted_iota(jnp.int32, (c, c), 0)
    col = lax.broadcasted_iota(jnp.int32, (c, c), 1)
    cum = _cumsum_steps(log_w)
    mid = cum[:, c // 2 - 1:c // 2, :]
    a_t = a * jnp.exp(cum - log_w - mid)
    r_t = r * jnp.exp(cum - mid)
    back = jnp.exp(mid - cum)
    b_t = b * back
    k_t = k * back
    strict, incl = (row > col)[None], (row >= col)[None]
    ar = jnp.concatenate([a_t, r_t], axis=1)
    on_b = _bmm(ar, b_t, "nt")
    on_k = _bmm(ar, k_t, "nt")
    n_mat = jnp.where(strict, on_b[:, :c], 0.0)
    p_mat = jnp.where(incl, on_b[:, c:], 0.0)
    m_mat = jnp.where(strict, on_k[:, :c], 0.0)
    q_mat = jnp.where(incl, on_k[:, c:], 0.0)
    inv = _neumann(n_mat)
    s_mid = state * jnp.swapaxes(jnp.exp(mid), 1, 2)
    x = _bmm(jnp.concatenate([a_t, m_mat], axis=2), jnp.concatenate([s_mid, v], axis=1), "nn")
    u = x + _bmm(inv, x, "nn")
    y = _bmm(jnp.concatenate([r_t, p_mat, q_mat], axis=2), jnp.concatenate([s_mid, u, v], axis=1), "nn")
    grown = _bmm(jnp.concatenate([b_t, k_t], axis=1), jnp.concatenate([u, v], axis=1), "tn")
    s_new = (s_mid + grown) * jnp.swapaxes(jnp.exp(cum[:, c - 1:c, :] - mid), 1, 2)
    return y, s_new


def _scan_fwd(name, ops):
    rows = ops[0].shape[0]
    nh, dh = N_HEADS, HEAD
    nc, per = rows // CHUNK, SCAN_CHUNKS
    spec = pl.BlockSpec((per * CHUNK, GROUP), lambda c: (c, 0))

    def body(r_ref, w_ref, k_ref, v_ref, a_ref, b_ref, y_ref, st_ref, state):
        @pl.when(pl.program_id(0) == 0)
        def _():
            state[...] = jnp.zeros_like(state)

        s = state[...]
        for u in range(per):
            at = slice(u * CHUNK, (u + 1) * CHUNK)
            st_ref[u] = s
            y, s = _chunk(s, *(_heads(ref[at, :]) for ref in (r_ref, w_ref, k_ref, v_ref, a_ref, b_ref)))
            y_ref[at, :] = _unheads(y)
        state[...] = s

    return pl.pallas_call(
        body, name=name, grid=(nc // per,),
        in_specs=[spec] * 6,
        out_specs=[spec, pl.BlockSpec((per, nh, dh, dh), lambda c: (c, 0, 0, 0))],
        out_shape=[jax.ShapeDtypeStruct((rows, GROUP), F32), jax.ShapeDtypeStruct((nc, nh, dh, dh), F32)],
        scratch_shapes=[pltpu.VMEM((nh, dh, dh), F32)],
        compiler_params=_params("arbitrary"),
    )(*ops)


def _scan_bwd(name, ops, states, dy):
    rows = ops[0].shape[0]
    nh, dh = N_HEADS, HEAD
    nc, per = rows // CHUNK, SCAN_CHUNKS
    steps = nc // per
    spec = pl.BlockSpec((per * CHUNK, GROUP), lambda c: (steps - 1 - c, 0))

    def body(r_ref, w_ref, k_ref, v_ref, a_ref, b_ref, st_ref, dy_ref, *rest):
        outs, dstate = rest[:6], rest[6]

        @pl.when(pl.program_id(0) == 0)
        def _():
            dstate[...] = jnp.zeros_like(dstate)

        ds = dstate[...]
        for u in reversed(range(per)):
            at = slice(u * CHUNK, (u + 1) * CHUNK)
            _, vjp = jax.vjp(_chunk, st_ref[u],
                             *(_heads(ref[at, :]) for ref in (r_ref, w_ref, k_ref, v_ref, a_ref, b_ref)))
            grads = vjp((_heads(dy_ref[at, :]), ds))
            ds = grads[0]
            for o_ref, gval in zip(outs, grads[1:]):
                o_ref[at, :] = _unheads(gval)
        dstate[...] = ds

    return pl.pallas_call(
        body, name=name, grid=(steps,),
        in_specs=[spec] * 6 + [pl.BlockSpec((per, nh, dh, dh), lambda c: (steps - 1 - c, 0, 0, 0)), spec],
        out_specs=[spec] * 6,
        out_shape=[jax.ShapeDtypeStruct((rows, GROUP), F32)] * 6,
        scratch_shapes=[pltpu.VMEM((nh, dh, dh), F32)],
        compiler_params=_params("arbitrary"),
    )(*ops, states, dy)


def _pad_cols(x, cols):
    return jnp.concatenate([x, jnp.zeros(x.shape[:-1] + (cols - x.shape[-1],), x.dtype)], axis=-1)


def _lora_pad(w_up, a_up, g_up):
    z = lambda n: jnp.zeros((n, GROUP), F32)
    return (jnp.concatenate([w_up, z(LORA_PAD - LORA_W)], 0),
            jnp.concatenate([z(LORA_W), a_up, z(LORA_PAD - LORA_W - LORA_A)], 0),
            jnp.concatenate([z(LORA_W + LORA_A), g_up, z(LORA_PAD - LORA_W - LORA_A - LORA_G)], 0))


MID = ['w_in']
LATE = ['ffn2_w_gate', 'ffn2_w_up', 'ffn2_w_down', 'w_out']


def _local_step(x, tgt, w, late=None):
    d = x.shape[1]
    zeros = jnp.zeros((META_PAD, d), F32)
    h0 = jnp.concatenate([zeros, w["meta_tokens"], x], axis=0)
    tgt_p = jnp.concatenate([jnp.zeros((ROW0, d), F32), tgt], axis=0)
    ones_bd = ((lax.broadcasted_iota(jnp.int32, (2 * GROUP, GROUP), 0) % GROUP) // HEAD
               == lax.broadcasted_iota(jnp.int32, (2 * GROUP, GROUP), 1) // HEAD).astype(BF16)
    pre_vecs = (_pad_cols(w["rwkv_mu"], RW_COLS), w["rwkv_w0"], w["rwkv_a0"], w["rwkv_k_k"], w["rwkv_k_a"])
    pre_mats = _lora_pad(w["rwkv_w_up"], w["rwkv_a_up"], w["rwkv_g_up"])
    post_vecs = (w["rwkv_lnx_w"], w["rwkv_lnx_b"], w["rwkv_r_k"].reshape(1, GROUP))

    h1, a1, b1, *gathered = _ffn_fwd("ffn1_fwd", h0, w["ffn1_norm"], w["ffn1_w_gate"], w["ffn1_w_up"],
                                     w["ffn1_w_down"], late and ("gather", late.shards["mid"]))
    if late is not None:
        w = {**w, **late.join("mid", gathered)}
    w_in = _pad_cols(w["w_in"], IN_COLS_PAD)
    qkv, p, n2 = _norm_proj("in_proj", h1, w["mix_norm"], w_in)
    sb, rest_total, visited, *gathered = _sb_fwd("sb_fwd", qkv, late.shards["late"] if late else ())
    if late is not None:
        w = {**w, **late.join("late", gathered)}
    pre = _rwkv_pre_fwd("rwkv_pre_fwd", p, pre_vecs, pre_mats, ones_bd)
    scan_ops, token_ops = pre[:6], (pre[0], pre[2], pre[3], pre[6])
    y, states = _scan_fwd("rwkv_scan_fwd", scan_ops)
    rw = _rwkv_post_fwd("rwkv_post_fwd", y, *token_ops, post_vecs, ones_bd)
    h2, mix = _out_proj("out_proj", h1, sb, rw, w["w_out"])
    h3, a2, b2 = _ffn_fwd("ffn2_fwd", h2, w["ffn2_norm"], w["ffn2_w_gate"], w["ffn2_w_up"], w["ffn2_w_down"])
    loss8, dh3, g_final = _loss_head("loss_head", h3, w["final_norm"].reshape(1, d), tgt_p)

    g = {"final_norm": g_final.reshape(d)}
    dh2, da2, db2, s2, n3, dhh3, g["ffn2_norm"] = _ffn_bwd(
        "ffn2_bwd", dh3, h2, w["ffn2_norm"], a2, b2, w["ffn2_w_gate"], w["ffn2_w_up"], w["ffn2_w_down"])
    g["ffn2_w_gate"] = _mm_tn("ffn2_dgate", da2, n3)
    g["ffn2_w_up"] = _mm_tn("ffn2_dup", db2, n3)
    g["ffn2_w_down"] = _mm_tn("ffn2_ddown", s2, dhh3)
    dsb, drw, dh2b = _out_proj_bwd("out_proj_bwd", dh2, w["w_out"])
    g["w_out"] = _mm_tn("out_proj_dw", mix, dh2b)
    dq, dk, dv, *reduced_late = _sb_bwd("sb_bwd", qkv, rest_total, visited, dsb, late.parts("late", g) if late else ())
    post_g = _rwkv_post_bwd("rwkv_post_bwd", y, *token_ops, post_vecs, ones_bd, drw)
    g["rwkv_lnx_w"], g["rwkv_lnx_b"] = post_g[5], post_g[6]
    g["rwkv_r_k"] = post_g[7].reshape(1, N_HEADS, HEAD)
    scan_g = _scan_bwd("rwkv_scan_bwd", scan_ops, states, post_g[0])
    pre_g = _rwkv_pre_bwd("rwkv_pre_bwd", p, pre_vecs, pre_mats, ones_bd, scan_g, post_g[4], post_g[1:4])
    dp = pre_g[0]
    g["rwkv_mu"] = pre_g[1][:, :w["rwkv_mu"].shape[1]]
    g["rwkv_w0"], g["rwkv_a0"], g["rwkv_k_k"], g["rwkv_k_a"] = pre_g[2:6]
    g["rwkv_w_up"] = pre_g[6][:LORA_W]
    g["rwkv_a_up"] = pre_g[7][LORA_W:LORA_W + LORA_A]
    g["rwkv_g_up"] = pre_g[8][LORA_W + LORA_A:LORA_W + LORA_A + LORA_G]
    live = (jnp.arange(h0.shape[0]) >= META_PAD)[:, None]
    dproj = jnp.where(live, jnp.concatenate([dq, dk, dv, dp], axis=1), 0.0).astype(BF16)
    g["w_in"] = _mm_tn("in_proj_dw", n2, dproj)[:, :w["w_in"].shape[1]]
    dh1, g["mix_norm"] = _norm_proj_bwd("in_proj_bwd", dproj, w_in, h1, w["mix_norm"], dh2)
    dh0, da1, db1, s1, n1, dhh1, g["ffn1_norm"], *reduced_mid = _ffn_bwd(
        "ffn1_bwd", dh1, h0, w["ffn1_norm"], a1, b1, w["ffn1_w_gate"], w["ffn1_w_up"], w["ffn1_w_down"],
        late and ("reduce", late.parts("mid", g)))
    g["meta_tokens"] = dh0[META_PAD:ROW0]
    reduced = {"mid": reduced_mid, "late": reduced_late}
    if late is None:
        g["ffn1_w_gate"] = _mm_tn("ffn1_dgate", da1, n1)
        g["ffn1_w_up"] = _mm_tn("ffn1_dup", db1, n1)
        g["ffn1_w_down"] = _mm_tn("ffn1_ddown", s1, dhh1)
    else:
        g["ffn1_w_gate"], reduced["small"] = _mm_tn("ffn1_dgate", da1, n1, ("all_reduce", [late.small(g)]))
        g["ffn1_w_up"], *reduced["gate"] = _mm_tn("ffn1_dup", db1, n1, ("reduce", late.parts("gate", g)))
        g["ffn1_w_down"], *reduced["up"] = _mm_tn("ffn1_ddown", s1, dhh1, ("reduce", late.parts("up", g)))
    return loss8[0, 0], dh0[ROW0:], g, reduced


N_CHIPS = 4
N_DEV = 8
HBM = pl.BlockSpec(memory_space=pltpu.HBM)


def _place():
    return lax.axis_index("x"), lax.axis_index("y"), lax.axis_index("c")


def _other_chips(x, y):
    return [(1 - x, y), (x, 1 - y), (1 - x, 1 - y)]


def _gather_sems(n):
    return [pltpu.SemaphoreType.DMA((3 * n,)), pltpu.SemaphoreType.DMA((3 * n,)), pltpu.SemaphoreType.DMA((n,)),
            pltpu.SemaphoreType.DMA((3 * n,)), pltpu.SemaphoreType.DMA((3 * n,))]


def _gather_exchange(ins, outs, sems):
    n = len(ins)
    half = [r.shape[0] // 2 for r in ins]
    send, recv, local, d2d_send, d2d_recv = sems
    x, y, c = _place()
    me = 2 * x + y
    chips = _other_chips(x, y)

    def rows_of(k, h):
        return pl.ds(pl.multiple_of(h * half[k], 8), half[k])

    def own(k):
        return pltpu.make_async_copy(ins[k], outs[k].at[me], local.at[k])

    def copy(j, k, slot):
        return pltpu.make_async_remote_copy(
            src_ref=ins[k].at[rows_of(k, c)], dst_ref=outs[k].at[slot, rows_of(k, c)],
            send_sem=send.at[j * n + k], recv_sem=recv.at[j * n + k],
            device_id=(chips[j][0], chips[j][1], c), device_id_type=MESH)

    def passed(j, k, h):
        slot = 2 * chips[j][0] + chips[j][1]
        return pltpu.make_async_remote_copy(
            src_ref=outs[k].at[slot, rows_of(k, h)], dst_ref=outs[k].at[slot, rows_of(k, h)],
            send_sem=d2d_send.at[j * n + k], recv_sem=d2d_recv.at[j * n + k],
            device_id=(x, y, 1 - c), device_id_type=MESH)

    def start():
        for k in range(n):
            own(k).start()
        for j in range(3):
            for k in range(n):
                copy(j, k, me).start()

    def finish():
        for j in range(3):
            for k in range(n):
                copy(j, k, 2 * chips[j][0] + chips[j][1]).wait_recv()
                passed(j, k, c).start()
        for j in range(3):
            for k in range(n):
                passed(j, k, 1 - c).wait_recv()
        for j in range(3):
            for k in range(n):
                copy(j, k, me).wait_send()
                passed(j, k, c).wait_send()
        for k in range(n):
            own(k).wait()

    return start, finish


def _gather_shards(name, shards):
    n = len(shards)

    def body(*refs):
        start, finish = _gather_exchange(refs[:n], refs[n:2 * n], refs[2 * n:])
        start()
        finish()

    return pl.pallas_call(
        body, name=name,
        in_specs=[HBM] * n, out_specs=[HBM] * n,
        out_shape=[jax.ShapeDtypeStruct((N_CHIPS,) + s.shape, s.dtype) for s in shards],
        scratch_shapes=_gather_sems(n),
    )(*shards)


def _pair_exchange(name, parts):
    n = len(parts)
    half = [s.shape[1] // 2 for s in parts]

    def body(*refs):
        ins, outs = refs[:n], refs[n:2 * n]
        send, recv = refs[2 * n:]
        x, y, c = _place()

        def copy(k):
            rows = pl.ds(pl.multiple_of((1 - c) * half[k], 8), half[k])
            return pltpu.make_async_remote_copy(
                src_ref=ins[k].at[:, rows], dst_ref=outs[k], send_sem=send.at[k], recv_sem=recv.at[k],
                device_id=(x, y, 1 - c), device_id_type=MESH)

        for k in range(n):
            copy(k).start()
        for k in range(n):
            copy(k).wait_recv()
        for k in range(n):
            copy(k).wait_send()

    return pl.pallas_call(
        body, name=name,
        in_specs=[HBM] * n, out_specs=[HBM] * n,
        out_shape=[jax.ShapeDtypeStruct((s.shape[0], s.shape[1] // 2, s.shape[2]), s.dtype) for s in parts],
        scratch_shapes=[pltpu.SemaphoreType.DMA((n,)), pltpu.SemaphoreType.DMA((n,))],
    )(*parts)


def _pair_add(name, part, other):
    nch, rows, cols = part.shape
    half = rows // 2

    def body(p_ref, o_ref, out_ref):
        c = lax.axis_index("c")
        mine = p_ref[0, pl.ds(pl.multiple_of(c * half, 16), half), :]
        out_ref[0] = (mine.astype(F32) + o_ref[0].astype(F32)).astype(out_ref.dtype)

    return pl.pallas_call(
        body, name=name, grid=(nch,),
        in_specs=[pl.BlockSpec((1, rows, cols), lambda j: (j, 0, 0)),
                  pl.BlockSpec((1, half, cols), lambda j: (j, 0, 0))],
        out_specs=pl.BlockSpec((1, half, cols), lambda j: (j, 0, 0)),
        out_shape=jax.ShapeDtypeStruct((nch, half, cols), part.dtype),
        compiler_params=_params("arbitrary"),
    )(part, other)


def _reduce_sems(n):
    return [pltpu.SemaphoreType.DMA((3 * n,)), pltpu.SemaphoreType.DMA((3 * n,)), pltpu.SemaphoreType.DMA((n,)),
            pltpu.SemaphoreType.DMA((n,)), pltpu.SemaphoreType.DMA((n,))]


def _reduce_exchange(ins, got, sib, sems):
    n = len(ins)
    send, recv, local, d2d_send, d2d_recv = sems
    x, y, c = _place()
    me = 2 * x + y
    chips = _other_chips(x, y)

    def own(k):
        return pltpu.make_async_copy(ins[k].at[me], got[k].at[me], local.at[k])

    def copy(j, k, shard, slot):
        return pltpu.make_async_remote_copy(
            src_ref=ins[k].at[shard], dst_ref=got[k].at[slot], send_sem=send.at[j * n + k],
            recv_sem=recv.at[j * n + k], device_id=(chips[j][0], chips[j][1], c), device_id_type=MESH)

    def swap(k):
        return pltpu.make_async_remote_copy(
            src_ref=got[k], dst_ref=sib[k], send_sem=d2d_send.at[k], recv_sem=d2d_recv.at[k],
            device_id=(x, y, 1 - c), device_id_type=MESH)

    def start():
        for k in range(n):
            own(k).start()
        for j in range(3):
            for k in range(n):
                copy(j, k, 2 * chips[j][0] + chips[j][1], me).start()

    def finish():
        for k in range(n):
            own(k).wait()
            for j in range(3):
                copy(j, k, me, 2 * chips[j][0] + chips[j][1]).wait_recv()
            swap(k).start()
        for k in range(n):
            swap(k).wait_recv()
        for j in range(3):
            for k in range(n):
                copy(j, k, me, me).wait_send()
        for k in range(n):
            swap(k).wait_send()

    return start, finish


def _reduce_shards(name, parts):
    n = len(parts)

    def body(*refs):
        start, finish = _reduce_exchange(refs[:n], refs[n:2 * n], refs[2 * n:3 * n], refs[3 * n:])
        start()
        finish()

    return pl.pallas_call(
        body, name=name,
        in_specs=[HBM] * n, out_specs=[HBM] * (2 * n),
        out_shape=[jax.ShapeDtypeStruct(s.shape, s.dtype) for s in parts] * 2,
        scratch_shapes=_reduce_sems(n),
    )(*parts)


def _all_reduce_scratch(vec):
    return [pltpu.VMEM((N_DEV,) + vec.shape, F32),
            pltpu.SemaphoreType.DMA((N_DEV - 1,)), pltpu.SemaphoreType.DMA((N_DEV - 1,))]


def _all_reduce_exchange(v_ref, o_ref, buf, send, recv):
    x, y, c = _place()
    me = 4 * x + 2 * y + c
    peers = [(x ^ (r >> 2), y ^ ((r >> 1) & 1), c ^ (r & 1)) for r in range(1, N_DEV)]

    def copy(r, slot):
        px, py, pc = peers[r]
        return pltpu.make_async_remote_copy(
            src_ref=v_ref, dst_ref=buf.at[slot], send_sem=send.at[r], recv_sem=recv.at[r],
            device_id=(px, py, pc), device_id_type=MESH)

    def start():
        for r in range(N_DEV - 1):
            copy(r, me).start()
        buf[me] = v_ref[...]

    def finish():
        for r in range(N_DEV - 1):
            px, py, pc = peers[r]
            copy(r, 4 * px + 2 * py + pc).wait_recv()
        total = buf[0]
        for dev in range(1, N_DEV):
            total = total + buf[dev]
        o_ref[...] = total
        for r in range(N_DEV - 1):
            copy(r, me).wait_send()

    return start, finish


def _adamw(w, g, m, v):
    m = ADAM_B1 * m + (1.0 - ADAM_B1) * g
    v = ADAM_B2 * v + (1.0 - ADAM_B2) * (g * g)
    m_hat = m / (1.0 - ADAM_B1 ** ADAM_STEP)
    v_hat = v / (1.0 - ADAM_B2 ** ADAM_STEP)
    return -ADAM_LR * (m_hat / (jnp.sqrt(v_hat) + ADAM_EPS) + ADAM_WD * w), m, v


def _adamw_shard(name, core, w, m, v, got, sib):
    rows, cols = w.shape
    tr = rows // 4
    spec = pl.BlockSpec((tr, cols), lambda i, c_ref: (i, 0))
    got_spec = pl.BlockSpec((N_CHIPS, tr, cols), lambda i, c_ref: (0, jnp.where(i // 2 == c_ref[0], i % 2, 0), 0))
    sib_spec = pl.BlockSpec((N_CHIPS, tr, cols), lambda i, c_ref: (0, jnp.where(i // 2 == c_ref[0], 0, i % 2), 0))

    def body(c_ref, w_ref, m_ref, v_ref, got_ref, sib_ref, g_ref, d_ref, mo_ref, vo_ref):
        def four(ref):
            return ((ref[0].astype(F32) + ref[1].astype(F32)) + ref[2].astype(F32)) + ref[3].astype(F32)

        g = jnp.where(pl.program_id(0) // 2 == c_ref[0], four(got_ref), four(sib_ref))
        g_ref[...] = g
        d_ref[...], mo_ref[...], vo_ref[...] = _adamw(w_ref[...], g, m_ref[...], v_ref[...])

    return pl.pallas_call(
        body, name=name,
        grid_spec=pltpu.PrefetchScalarGridSpec(
            num_scalar_prefetch=1, grid=(4,),
            in_specs=[spec, spec, spec, got_spec, sib_spec], out_specs=[spec] * 4),
        out_shape=[jax.ShapeDtypeStruct((rows, cols), F32)] * 4,
        compiler_params=_params("arbitrary"),
    )(core, w, m, v, got, sib)


def _adamw_small(name, w, m, v, g):
    def body(w_ref, m_ref, v_ref, g_ref, d_ref, mo_ref, vo_ref):
        d_ref[...], mo_ref[...], vo_ref[...] = _adamw(w_ref[...], g_ref[...], m_ref[...], v_ref[...])

    return pl.pallas_call(body, name=name, out_shape=[jax.ShapeDtypeStruct(w.shape, F32)] * 3)(w, m, v, g)


def _cast_bf16(name, arrays):
    n = len(arrays)

    def body(*refs):
        for i_ref, o_ref in zip(refs[:n], refs[n:]):
            o_ref[...] = i_ref[...].astype(BF16)

    return pl.pallas_call(
        body, name=name, out_shape=[jax.ShapeDtypeStruct(a.shape, BF16) for a in arrays],
        compiler_params=pltpu.CompilerParams(vmem_limit_bytes=VMEM_LIMIT),
    )(*arrays)


def _pack(arrays, rows):
    flat = jnp.concatenate([a.reshape(-1) for a in arrays])
    return jnp.concatenate([flat, jnp.zeros((rows * 128 - flat.shape[0],), F32)]).reshape(rows, 128)


def _unpack(packed, shapes):
    flat, out, at = packed.reshape(-1), [], 0
    for s in shapes:
        size = 1
        for dim in s:
            size *= dim
        out.append(flat[at:at + size].reshape(s))
        at += size
    return out


def _rows_for(shapes):
    total = 0
    for s in shapes:
        size = 1
        for dim in s:
            size *= dim
        total += size
    return -(-total // 1024) * 8


WEIGHTS = ['meta_tokens', 'ffn1_norm', 'ffn1_w_gate', 'ffn1_w_up', 'ffn1_w_down', 'mix_norm', 'w_in', 'rwkv_mu',
           'rwkv_w0', 'rwkv_w_up', 'rwkv_a0', 'rwkv_a_up', 'rwkv_g_up', 'rwkv_k_k', 'rwkv_k_a', 'rwkv_r_k',
           'rwkv_lnx_w', 'rwkv_lnx_b', 'w_out', 'ffn2_norm', 'ffn2_w_gate', 'ffn2_w_up', 'ffn2_w_down', 'final_norm']
COL_CUT = ['ffn1_w_gate', 'ffn1_w_up', 'w_in', 'ffn2_w_gate', 'ffn2_w_up']
ROW_CUT = ['ffn1_w_down', 'w_out', 'ffn2_w_down']
SMALL_CUT = ['meta_tokens', 'rwkv_w_up', 'rwkv_a_up', 'rwkv_g_up']
TRANSPOSED = ['ffn1_w_gate', 'ffn1_w_up', 'ffn2_w_gate', 'ffn2_w_up']
BIG = COL_CUT + ROW_CUT
REPLICATED = [n for n in WEIGHTS if n not in BIG + SMALL_CUT]


def _join_cols(a):
    return a.transpose(1, 0, 2).reshape(a.shape[1], N_CHIPS * a.shape[2])


def _cut_cols(a):
    return a.reshape(a.shape[0], N_CHIPS, a.shape[1] // N_CHIPS).transpose(1, 0, 2)


def _step(x, loss_target, w, m, v):
    two = lambda a: a.reshape(a.shape[-2], a.shape[-1])

    def rows_cut(n, a):
        return jnp.swapaxes(two(a), 0, 1) if n in TRANSPOSED else two(a)

    def as_given(n, a, like):
        return (jnp.swapaxes(a, 0, 1) if n in TRANSPOSED else a).reshape(like.shape)

    col_cut = [n for n in COL_CUT + SMALL_CUT if n not in TRANSPOSED]

    def join(names, gathered):
        return {n: (_join_cols(a) if n in col_cut else a.reshape(-1, a.shape[-1])) for n, a in zip(names, gathered)}

    def pair_sums(tag, names, g):
        parts = [_cut_cols(g[n]) if n in col_cut else g[n].reshape(N_CHIPS, -1, g[n].shape[-1]) for n in names]
        arrived = _pair_exchange("pair_exchange_" + tag, parts)
        return [_pair_add("pair_add_" + n, p, o) for n, p, o in zip(names, parts, arrived)]

    first = [n for n in BIG if n not in MID + LATE]
    gathered_later = {"mid": MID, "late": LATE}
    groups = {**gathered_later, "gate": ["ffn1_w_gate"], "up": ["ffn1_w_up"]}
    cast = dict(zip(BIG, _cast_bf16("cast_weights", [rows_cut(n, w[n]) for n in BIG])))
    names = first + SMALL_CUT
    shards = [cast[n] for n in first] + [two(w[n]) for n in SMALL_CUT]
    full = {n: (two(w[n]) if w[n].ndim == 3 else w[n]) for n in REPLICATED}
    full.update(join(names, _gather_shards("gather_weights", shards)))
    full["rwkv_r_k"] = w["rwkv_r_k"]
    full["final_norm"] = w["final_norm"]

    small_names = REPLICATED + SMALL_CUT

    def small(g):
        return _pack([g[n] for n in small_names], _rows_for([g[n].shape for n in small_names]))

    late = types.SimpleNamespace(shards={k: [cast[n] for n in names] for k, names in gathered_later.items()},
                                 join=lambda k, gathered: join(groups[k], gathered),
                                 parts=lambda k, g: pair_sums(k, groups[k], g), small=small)

    loss, dx, g, reduced = _local_step(x[0], loss_target[0], full, late)
    loss = lax.psum(loss, ("x", "y", "c"))

    groups["down"] = ["ffn1_w_down"]
    reduced["down"] = list(_reduce_shards("reduce_gradients", pair_sums("down", groups["down"], g)))
    got, sib = {}, {}
    for k, names in groups.items():
        got.update(zip(names, reduced[k][:len(names)]))
        sib.update(zip(names, reduced[k][len(names):]))
    g_small = dict(zip(small_names, _unpack(reduced["small"], [g[n].shape for n in small_names])))
    chip = 2 * lax.axis_index("x") + lax.axis_index("y")
    for n in SMALL_CUT:
        width = g_small[n].shape[1] // N_CHIPS
        g_small[n] = lax.dynamic_slice_in_dim(g_small[n], chip * width, width, axis=1)

    grad, delta, new_m, new_v = {}, {}, {}, {}
    core = lax.axis_index("c").astype(jnp.int32).reshape(1)
    for n in BIG:
        outs = _adamw_shard("adamw_" + n, core, rows_cut(n, w[n]), rows_cut(n, m[n]), rows_cut(n, v[n]), got[n], sib[n])
        grad[n], delta[n], new_m[n], new_v[n] = (as_given(n, o, w[n]) for o in outs)
    shapes = [w[n].shape for n in small_names]
    rows = _rows_for(shapes)
    packed = [_pack([t[n] for n in small_names], rows) for t in (w, m, v)]
    g_packed = _pack([g_small[n] for n in small_names], rows)
    outs = [_unpack(o, shapes) for o in _adamw_small("adamw_small", *packed, g_packed)]
    for i, n in enumerate(small_names):
        grad[n] = g_small[n].reshape(w[n].shape)
        delta[n], new_m[n], new_v[n] = outs[0][i], outs[1][i], outs[2][i]
    return loss, dx[None], grad, delta, new_m, new_v


def kernel(x, meta_tokens, ffn1_norm, ffn1_w_gate, ffn1_w_up, ffn1_w_down, mix_norm, w_in, rwkv_mu, rwkv_w0, rwkv_w_up, rwkv_a0, rwkv_a_up, rwkv_g_up, rwkv_k_k, rwkv_k_a, rwkv_r_k, rwkv_lnx_w, rwkv_lnx_b, w_out, ffn2_norm, ffn2_w_gate, ffn2_w_up, ffn2_w_down, final_norm, loss_target, m_meta_tokens, m_ffn1_norm, m_ffn1_w_gate, m_ffn1_w_up, m_ffn1_w_down, m_mix_norm, m_w_in, m_rwkv_mu, m_rwkv_w0, m_rwkv_w_up, m_rwkv_a0, m_rwkv_a_up, m_rwkv_g_up, m_rwkv_k_k, m_rwkv_k_a, m_rwkv_r_k, m_rwkv_lnx_w, m_rwkv_lnx_b, m_w_out, m_ffn2_norm, m_ffn2_w_gate, m_ffn2_w_up, m_ffn2_w_down, m_final_norm, v_meta_tokens, v_ffn1_norm, v_ffn1_w_gate, v_ffn1_w_up, v_ffn1_w_down, v_mix_norm, v_w_in, v_rwkv_mu, v_rwkv_w0, v_rwkv_w_up, v_rwkv_a0, v_rwkv_a_up, v_rwkv_g_up, v_rwkv_k_k, v_rwkv_k_a, v_rwkv_r_k, v_rwkv_lnx_w, v_rwkv_lnx_b, v_w_out, v_ffn2_norm, v_ffn2_w_gate, v_ffn2_w_up, v_ffn2_w_down, v_final_norm):
    w = dict(zip(WEIGHTS, (meta_tokens, ffn1_norm, ffn1_w_gate, ffn1_w_up, ffn1_w_down, mix_norm, w_in, rwkv_mu, rwkv_w0, rwkv_w_up, rwkv_a0, rwkv_a_up, rwkv_g_up, rwkv_k_k, rwkv_k_a, rwkv_r_k, rwkv_lnx_w, rwkv_lnx_b, w_out, ffn2_norm, ffn2_w_gate, ffn2_w_up, ffn2_w_down, final_norm)))
    m = dict(zip(WEIGHTS, (m_meta_tokens, m_ffn1_norm, m_ffn1_w_gate, m_ffn1_w_up, m_ffn1_w_down, m_mix_norm, m_w_in, m_rwkv_mu, m_rwkv_w0, m_rwkv_w_up, m_rwkv_a0, m_rwkv_a_up, m_rwkv_g_up, m_rwkv_k_k, m_rwkv_k_a, m_rwkv_r_k, m_rwkv_lnx_w, m_rwkv_lnx_b, m_w_out, m_ffn2_norm, m_ffn2_w_gate, m_ffn2_w_up, m_ffn2_w_down, m_final_norm)))
    v = dict(zip(WEIGHTS, (v_meta_tokens, v_ffn1_norm, v_ffn1_w_gate, v_ffn1_w_up, v_ffn1_w_down, v_mix_norm, v_w_in, v_rwkv_mu, v_rwkv_w0, v_rwkv_w_up, v_rwkv_a0, v_rwkv_a_up, v_rwkv_g_up, v_rwkv_k_k, v_rwkv_k_a, v_rwkv_r_k, v_rwkv_lnx_w, v_rwkv_lnx_b, v_w_out, v_ffn2_norm, v_ffn2_w_gate, v_ffn2_w_up, v_ffn2_w_down, v_final_norm)))
    loss, grad_x, grad, delta, new_m, new_v = _step(x, loss_target, w, m, v)
    return (loss, grad_x, *[grad[n] for n in WEIGHTS], *[delta[n] for n in WEIGHTS],
            *[new_m[n] for n in WEIGHTS], *[new_v[n] for n in WEIGHTS])
```

```python
import functools
import types

import jax
import jax.numpy as jnp
from jax import lax
from jax.experimental import pallas as pl
from jax.experimental.pallas import tpu as pltpu

F32 = jnp.float32
BF16 = jnp.bfloat16

RMS_EPS = 1e-6
LNX_EPS = 64e-5
N_META = 16
ROW0 = 128
META_PAD = ROW0 - N_META
HEAD = 64
N_HEADS = 8
GROUP = N_HEADS * HEAD
LORA_W, LORA_A, LORA_G = 32, 32, 96
LORA_PAD = 256
RW_COLS = 3 * GROUP + LORA_PAD
IN_COLS_PAD = 3 * GROUP + RW_COLS
ATT_BLOCK = 128
CHUNK = 64
SCAN_CHUNKS = 2
VMEM_LIMIT = 56 * 1024 * 1024

ADAM_LR, ADAM_B1, ADAM_B2, ADAM_EPS, ADAM_WD, ADAM_STEP = 0.001, 0.9, 0.999, 1e-08, 0.01, 10

MESH = pl.DeviceIdType.MESH


def _params(*sem):
    return pltpu.CompilerParams(dimension_semantics=tuple(sem), vmem_limit_bytes=VMEM_LIMIT)


def _dot(a, b):
    return lax.dot_general(a, b, (((1,), (0,)), ((), ())), preferred_element_type=F32)


def _dot_nt(a, b):
    return lax.dot_general(a, b, (((1,), (1,)), ((), ())), preferred_element_type=F32)


def _dot_tn(a, b):
    return lax.dot_general(a, b, (((0,), (0,)), ((), ())), preferred_element_type=F32)


def _split2(x):
    hi = x.astype(BF16)
    return hi, (x - hi.astype(F32)).astype(BF16)


def _sigmoid(x):
    return 1.0 / (1.0 + jnp.exp(-x))


def _rms_fwd(x, g):
    rstd = lax.rsqrt(jnp.mean(x * x, axis=-1, keepdims=True) + RMS_EPS)
    xhat = x * rstd
    return xhat * g, xhat, rstd


def _rms_bwd(dn, xhat, rstd, g):
    dxhat = dn * g
    dx = rstd * (dxhat - xhat * jnp.mean(dxhat * xhat, axis=-1, keepdims=True))
    return dx, jnp.sum(dn * xhat, axis=0, keepdims=True)


def _row_tile(rows):
    return 384 if rows % 384 == 0 else 128


def _half_tile(cols):
    return cols // 2 if cols % 256 == 0 else cols


def _tall_tile(rows, parts):
    return rows // parts if rows % (16 * parts) == 0 else _row_tile(rows)


def _call_with_exchange(name, body, grid, in_specs, out_specs, out_shape, scratch, operands, params, exchange):
    if exchange is None or not exchange[1]:
        return pl.pallas_call(body, name=name, grid=grid, in_specs=in_specs, out_specs=out_specs,
                              out_shape=out_shape, scratch_shapes=scratch, compiler_params=params)(*operands)
    kind, arrays = exchange
    ns, n_in, n_out, n_scr = len(arrays), len(in_specs), len(out_specs), len(scratch)
    whole = lambda a: pl.BlockSpec(a.shape, lambda *_: (0,) * a.ndim)
    if kind == "gather":
        results = [jax.ShapeDtypeStruct((N_CHIPS,) + s.shape, s.dtype) for s in arrays]
        sems, sent_specs, landed_specs = _gather_sems(ns), [HBM] * ns, [HBM] * ns
    elif kind == "reduce":
        results = [jax.ShapeDtypeStruct(s.shape, s.dtype) for s in arrays] * 2
        sems, sent_specs, landed_specs = _reduce_sems(ns), [HBM] * ns, [HBM] * (2 * ns)
    else:
        results = [jax.ShapeDtypeStruct(arrays[0].shape, F32)]
        sems, sent_specs, landed_specs = _all_reduce_scratch(arrays[0]), [whole(arrays[0])], [whole(arrays[0])]
    n_res = len(results)

    def carried(*refs):
        at = n_in + ns + n_out
        sent, landed = refs[n_in:n_in + ns], refs[at:at + n_res]
        own_scratch, sem_refs = refs[at + n_res:at + n_res + n_scr], refs[at + n_res + n_scr:]
        first, before_last, last = _exchange_steps(grid)
        if kind == "gather":
            start, forward, finish = _gather_exchange(sent, landed, sem_refs)
        elif kind == "reduce":
            start, forward, finish = _reduce_exchange(sent, landed[:ns], landed[ns:], sem_refs)
        else:
            start, forward, finish = _all_reduce_exchange(sent[0], landed[0], *sem_refs)
        pl.when(first)(start)
        body(*refs[:n_in], *refs[n_in + ns:at], *own_scratch)
        pl.when(before_last)(forward)
        pl.when(last)(finish)

    return pl.pallas_call(
        carried, name=name, grid=grid, in_specs=list(in_specs) + sent_specs, out_specs=list(out_specs) + landed_specs,
        out_shape=list(out_shape) + results, scratch_shapes=list(scratch) + sems, compiler_params=params,
    )(*operands, *arrays)


def _ffn_fwd(name, h, g, wg, wu, wd, exchange=None):
    rows, d = h.shape
    f = wg.shape[0]
    tm, tf = _row_tile(rows), _half_tile(f)
    nj = f // tf

    def body(h_ref, g_ref, wg_ref, wu_ref, wd_ref, ho_ref, a_ref, b_ref, n_sc, acc_sc):
        j = pl.program_id(1)

        @pl.when(j == 0)
        def _():
            n, _, _ = _rms_fwd(h_ref[...], g_ref[...])
            n_sc[...] = n.astype(BF16)
            acc_sc[...] = jnp.zeros_like(acc_sc)

        n = n_sc[...]
        a = _dot_nt(n, wg_ref[...])
        b = _dot_nt(n, wu_ref[...])
        a_ref[...] = a
        b_ref[...] = b
        s = a * _sigmoid(a) * b
        acc_sc[...] += _dot(s.astype(BF16), wd_ref[...])

        @pl.when(j == nj - 1)
        def _():
            ho_ref[...] = h_ref[...] + 0.5 * acc_sc[...]

    return _call_with_exchange(
        name, body, (rows // tm, nj),
        [pl.BlockSpec((tm, d), lambda i, j: (i, 0)),
         pl.BlockSpec((1, d), lambda i, j: (0, 0)),
         pl.BlockSpec((tf, d), lambda i, j: (j, 0)),
         pl.BlockSpec((tf, d), lambda i, j: (j, 0)),
         pl.BlockSpec((tf, d), lambda i, j: (j, 0))],
        [pl.BlockSpec((tm, d), lambda i, j: (i, 0)),
         pl.BlockSpec((tm, tf), lambda i, j: (i, j)),
         pl.BlockSpec((tm, tf), lambda i, j: (i, j))],
        [jax.ShapeDtypeStruct((rows, d), F32),
         jax.ShapeDtypeStruct((rows, f), F32),
         jax.ShapeDtypeStruct((rows, f), F32)],
        [pltpu.VMEM((tm, d), BF16), pltpu.VMEM((tm, d), F32)],
        (h, g, wg, wu, wd), _params("arbitrary", "arbitrary"), exchange)


def _ffn_bwd(name, dh, h, g, a, b, wg, wu, wd, exchange=None):
    rows, d = h.shape
    f = wg.shape[0]
    tm, tf = _row_tile(rows), _half_tile(f)
    ni, nj = rows // tm, f // tf

    def body(dh_ref, h_ref, g_ref, a_ref, b_ref, wg_ref, wu_ref, wd_ref,
             dhi_ref, da_ref, db_ref, s_ref, n_ref, dhh_ref, dg_ref, dn_sc):
        i, j = pl.program_id(0), pl.program_id(1)

        @pl.when(j == 0)
        def _():
            n, _, _ = _rms_fwd(h_ref[...], g_ref[...])
            n_ref[...] = n.astype(BF16)
            dhh_ref[...] = (0.5 * dh_ref[...]).astype(BF16)
            dn_sc[...] = jnp.zeros_like(dn_sc)

        @pl.when((i == 0) & (j == 0))
        def _():
            dg_ref[...] = jnp.zeros_like(dg_ref)

        ds = _dot_nt(dhh_ref[...], wd_ref[...])
        av, bv = a_ref[...], b_ref[...]
        sig = _sigmoid(av)
        silu = av * sig
        s_ref[...] = (silu * bv).astype(BF16)
        db = (ds * silu).astype(BF16)
        da = (ds * bv * (sig * (1.0 + av * (1.0 - sig)))).astype(BF16)
        da_ref[...] = da
        db_ref[...] = db
        dn_sc[...] += _dot(da, wg_ref[...]) + _dot(db, wu_ref[...])

        @pl.when(j == nj - 1)
        def _():
            gv = g_ref[...]
            _, xhat, rstd = _rms_fwd(h_ref[...], gv)
            dx, dg = _rms_bwd(dn_sc[...], xhat, rstd, gv)
            dhi_ref[...] = dh_ref[...] + dx
            dg_ref[...] += dg

    return _call_with_exchange(
        name, body, (ni, nj),
        [pl.BlockSpec((tm, d), lambda i, j: (i, 0)),
         pl.BlockSpec((tm, d), lambda i, j: (i, 0)),
         pl.BlockSpec((1, d), lambda i, j: (0, 0)),
         pl.BlockSpec((tm, tf), lambda i, j: (i, j)),
         pl.BlockSpec((tm, tf), lambda i, j: (i, j)),
         pl.BlockSpec((tf, d), lambda i, j: (j, 0)),
         pl.BlockSpec((tf, d), lambda i, j: (j, 0)),
         pl.BlockSpec((tf, d), lambda i, j: (j, 0))],
        [pl.BlockSpec((tm, d), lambda i, j: (i, 0)),
         pl.BlockSpec((tm, tf), lambda i, j: (i, j)),
         pl.BlockSpec((tm, tf), lambda i, j: (i, j)),
         pl.BlockSpec((tm, tf), lambda i, j: (i, j)),
         pl.BlockSpec((tm, d), lambda i, j: (i, 0)),
         pl.BlockSpec((tm, d), lambda i, j: (i, 0)),
         pl.BlockSpec((1, d), lambda i, j: (0, 0))],
        [jax.ShapeDtypeStruct((rows, d), F32),
         jax.ShapeDtypeStruct((rows, f), BF16),
         jax.ShapeDtypeStruct((rows, f), BF16),
         jax.ShapeDtypeStruct((rows, f), BF16),
         jax.ShapeDtypeStruct((rows, d), BF16),
         jax.ShapeDtypeStruct((rows, d), BF16),
         jax.ShapeDtypeStruct((1, d), F32)],
        [pltpu.VMEM((tm, d), F32)],
        (dh, h, g, a, b, wg, wu, wd), _params("arbitrary", "arbitrary"), exchange)


def _mm_tn(name, a, b, exchange=None):
    k, m = a.shape
    n = b.shape[1]
    tk = _tall_tile(k, 3)
    tm = _half_tile(m) if m > 1024 else m
    tn = _half_tile(n) if n > 1024 else n
    nk = k // tk

    def body(a_ref, b_ref, o_ref, acc):
        kk = pl.program_id(2)

        @pl.when(kk == 0)
        def _():
            acc[...] = jnp.zeros_like(acc)

        acc[...] += _dot_tn(a_ref[...], b_ref[...])

        @pl.when(kk == nk - 1)
        def _():
            o_ref[...] = acc[...].astype(BF16)

    outs = _call_with_exchange(
        name, body, (m // tm, n // tn, nk),
        [pl.BlockSpec((tk, tm), lambda i, j, kk: (kk, i)),
         pl.BlockSpec((tk, tn), lambda i, j, kk: (kk, j))],
        [pl.BlockSpec((tm, tn), lambda i, j, kk: (i, j))],
        [jax.ShapeDtypeStruct((m, n), BF16)],
        [pltpu.VMEM((tm, tn), F32)],
        (a, b), _params("arbitrary", "arbitrary", "arbitrary"), exchange)
    return outs if exchange else outs[0]


def _norm_proj(name, h, g, w):
    rows, d = h.shape
    n = w.shape[1]
    split = 3 * GROUP
    tm = _row_tile(rows)

    def body(h_ref, g_ref, w_ref, qkv_ref, p_ref, n_ref):
        nv, _, _ = _rms_fwd(h_ref[...], g_ref[...])
        nb = nv.astype(BF16)
        n_ref[...] = nb
        qkv_ref[...] = _dot(nb, w_ref[:, :split]).astype(BF16)
        p_ref[...] = _dot(nb, w_ref[:, split:])

    return pl.pallas_call(
        body, name=name, grid=(rows // tm,),
        in_specs=[pl.BlockSpec((tm, d), lambda i: (i, 0)),
                  pl.BlockSpec((1, d), lambda i: (0, 0)),
                  pl.BlockSpec((d, n), lambda i: (0, 0))],
        out_specs=[pl.BlockSpec((tm, split), lambda i: (i, 0)),
                   pl.BlockSpec((tm, n - split), lambda i: (i, 0)),
                   pl.BlockSpec((tm, d), lambda i: (i, 0))],
        out_shape=[jax.ShapeDtypeStruct((rows, split), BF16), jax.ShapeDtypeStruct((rows, n - split), F32),
                   jax.ShapeDtypeStruct((rows, d), BF16)],
        compiler_params=_params("arbitrary"),
    )(h, g, w)


def _out_proj(name, h, sb, rw, w):
    rows, d = h.shape
    gw = sb.shape[1]
    tm = _row_tile(rows)

    def body(h_ref, sb_ref, rw_ref, w_ref, o_ref, mix_ref):
        mix_ref[:, :gw] = sb_ref[...].astype(BF16)
        mix_ref[:, gw:] = rw_ref[...].astype(BF16)
        o_ref[...] = h_ref[...] + _dot(mix_ref[...], w_ref[...])

    return pl.pallas_call(
        body, name=name, grid=(rows // tm,),
        in_specs=[pl.BlockSpec((tm, d), lambda i: (i, 0)),
                  pl.BlockSpec((tm, gw), lambda i: (i, 0)),
                  pl.BlockSpec((tm, gw), lambda i: (i, 0)),
                  pl.BlockSpec((2 * gw, d), lambda i: (0, 0))],
        out_specs=[pl.BlockSpec((tm, d), lambda i: (i, 0)),
                   pl.BlockSpec((tm, 2 * gw), lambda i: (i, 0))],
        out_shape=[jax.ShapeDtypeStruct((rows, d), F32), jax.ShapeDtypeStruct((rows, 2 * gw), BF16)],
        compiler_params=_params("arbitrary"),
    )(h, sb, rw, w)


def _out_proj_bwd(name, dh, w):
    rows, d = dh.shape
    k = w.shape[0]
    tm = _row_tile(rows)

    def body(dh_ref, w_ref, dsb_ref, drw_ref, dhb_ref):
        dhb = dh_ref[...].astype(BF16)
        dhb_ref[...] = dhb
        dsb_ref[...] = _dot_nt(dhb, w_ref[:GROUP, :]).astype(BF16)
        drw_ref[...] = _dot_nt(dhb, w_ref[GROUP:, :])

    return pl.pallas_call(
        body, name=name, grid=(rows // tm,),
        in_specs=[pl.BlockSpec((tm, d), lambda i: (i, 0)),
                  pl.BlockSpec((k, d), lambda i: (0, 0))],
        out_specs=[pl.BlockSpec((tm, GROUP), lambda i: (i, 0)),
                   pl.BlockSpec((tm, GROUP), lambda i: (i, 0)),
                   pl.BlockSpec((tm, d), lambda i: (i, 0))],
        out_shape=[jax.ShapeDtypeStruct((rows, GROUP), BF16), jax.ShapeDtypeStruct((rows, GROUP), F32),
                   jax.ShapeDtypeStruct((rows, d), BF16)],
        compiler_params=_params("arbitrary"),
    )(dh, w)


def _norm_proj_bwd(name, dproj, w, h, g, dh):
    rows, n = dproj.shape
    d = w.shape[0]
    tm = _row_tile(rows)

    def body(dp_ref, w_ref, h_ref, g_ref, dh_ref, o_ref, dg_ref):
        @pl.when(pl.program_id(0) == 0)
        def _():
            dg_ref[...] = jnp.zeros_like(dg_ref)

        dn = _dot_nt(dp_ref[...], w_ref[...])
        gv = g_ref[...]
        _, xhat, rstd = _rms_fwd(h_ref[...], gv)
        dx, dg = _rms_bwd(dn, xhat, rstd, gv)
        o_ref[...] = dh_ref[...] + dx
        dg_ref[...] += dg

    return pl.pallas_call(
        body, name=name, grid=(rows // tm,),
        in_specs=[pl.BlockSpec((tm, n), lambda i: (i, 0)),
                  pl.BlockSpec((d, n), lambda i: (0, 0)),
                  pl.BlockSpec((tm, d), lambda i: (i, 0)),
                  pl.BlockSpec((1, d), lambda i: (0, 0)),
                  pl.BlockSpec((tm, d), lambda i: (i, 0))],
        out_specs=[pl.BlockSpec((tm, d), lambda i: (i, 0)),
                   pl.BlockSpec((1, d), lambda i: (0, 0))],
        out_shape=[jax.ShapeDtypeStruct((rows, d), F32), jax.ShapeDtypeStruct((1, d), F32)],
        compiler_params=_params("arbitrary"),
    )(dproj, w, h, g, dh)


def _loss_head(name, h, g, tgt):
    rows, d = h.shape
    tm = _row_tile(rows)

    def body(h_ref, g_ref, t_ref, loss_ref, dh_ref, dg_ref):
        i = pl.program_id(0)

        @pl.when(i == 0)
        def _():
            loss_ref[...] = jnp.zeros_like(loss_ref)
            dg_ref[...] = jnp.zeros_like(dg_ref)

        gv = g_ref[...]
        y, xhat, rstd = _rms_fwd(h_ref[...], gv)
        row = i * tm + lax.broadcasted_iota(jnp.int32, (tm, 1), 0)
        diff = jnp.where(row >= ROW0, y - t_ref[...], 0.0)
        part = 0.5 * jnp.sum(jnp.sum(diff * diff, axis=-1, keepdims=True), axis=0, keepdims=True) / d
        loss_ref[...] += jnp.broadcast_to(part, loss_ref.shape)
        dx, dg = _rms_bwd(diff / d, xhat, rstd, gv)
        dh_ref[...] = dx
        dg_ref[...] += dg

    return pl.pallas_call(
        body, name=name, grid=(rows // tm,),
        in_specs=[pl.BlockSpec((tm, d), lambda i: (i, 0)),
                  pl.BlockSpec((1, d), lambda i: (0, 0)),
                  pl.BlockSpec((tm, d), lambda i: (i, 0))],
        out_specs=[pl.BlockSpec((8, 128), lambda i: (0, 0)),
                   pl.BlockSpec((tm, d), lambda i: (i, 0)),
                   pl.BlockSpec((1, d), lambda i: (0, 0))],
        out_shape=[jax.ShapeDtypeStruct((8, 128), F32),
                   jax.ShapeDtypeStruct((rows, d), F32),
                   jax.ShapeDtypeStruct((1, d), F32)],
        compiler_params=_params("arbitrary"),
    )(h, g, tgt)


def _sb_block(qb, kb, q0, jb, scale):
    bq, bk = qb.shape[0], kb.shape[0]
    z = _dot_nt(qb, kb) * scale
    qpos = q0 + lax.broadcasted_iota(jnp.int32, (bq, bk), 0)
    kpos = jb * bk + lax.broadcasted_iota(jnp.int32, (bq, bk), 1)
    valid = (kpos < qpos) & (kpos >= META_PAD)
    e = jnp.exp(-jnp.abs(z))
    log_keep = jnp.where(valid, -(jnp.maximum(z, 0.0) + jnp.log(1.0 + e)), 0.0)
    return z, valid, e, log_keep


def _tri2(n, cmp):
    r = lax.broadcasted_iota(jnp.int32, (2 * n, n), 0) % n
    c = lax.broadcasted_iota(jnp.int32, (2 * n, n), 1)
    return cmp(r, c).astype(BF16)


def _dot_split(x, t2):
    hi, lo = _split2(x)
    return _dot(jnp.concatenate([hi, lo], axis=1), t2)


ATT_HEADS = 8
ATT_WIDTH = ATT_HEADS * HEAD
ATT_CUT = -104.0
ATT_TILES = GROUP // ATT_WIDTH


def _lanes(hh):
    return slice(hh * HEAD, (hh + 1) * HEAD)


def _exchange_steps(grid):
    step, total = pl.program_id(0), 1
    for a in range(1, len(grid)):
        step = step * grid[a] + pl.program_id(a)
    for size in grid:
        total *= size
    return step == 0, step == max(total - 2, 0), step == total - 1


def _sb_fwd(name, qkv, shards=()):
    rows = qkv.shape[0]
    nh, dh = N_HEADS, HEAD
    bq, bk, hg = _row_tile(rows), ATT_BLOCK, ATT_HEADS
    per = bq // bk
    scale = dh ** -0.5
    ns = len(shards)
    grid = (nh // hg, rows // bq)

    def body(q_ref, k_ref, v_ref, *rest):
        o_ref, rt_ref, cnt_ref = rest[ns:ns + 3]
        if ns:
            first, before_last, last = _exchange_steps(grid)
            start, forward, finish = _gather_exchange(rest[:ns], rest[ns + 3:2 * ns + 3], rest[2 * ns + 3:])
            pl.when(first)(start)
        i = pl.program_id(1)
        after = _tri2(bk, lambda r, c: r > c)
        nkb = (i + 1) * per

        def live(state):
            n, carry = state
            top = jnp.max(carry[0][0])
            for hh in range(1, hg):
                top = jnp.maximum(top, jnp.max(carry[hh][0]))
            return (n < nkb) & (top >= ATT_CUT)

        def visit(carry, jb, r0):
            off = pl.multiple_of(jb * bk, bk)
            out = []
            for hh in range(hg):
                rest, acc = carry[hh]
                kb = k_ref[pl.ds(off, bk), _lanes(hh)]
                vb = v_ref[pl.ds(off, bk), _lanes(hh)]
                z, valid, _, log_keep = _sb_block(q_ref[r0:, _lanes(hh)], kb, i * bq + r0, jb, scale)
                log_rest = rest[r0:] + _dot_split(log_keep, after)
                attn = jnp.where(valid, jnp.exp(z + log_keep + log_rest), 0.0)
                new_rest = rest[r0:] + jnp.sum(log_keep, axis=-1, keepdims=True)
                new_acc = acc[r0:] + _dot(attn.astype(BF16), vb)
                if r0:
                    new_rest = jnp.concatenate([rest[:r0], new_rest], axis=0)
                    new_acc = jnp.concatenate([acc[:r0], new_acc], axis=0)
                out.append((new_rest, new_acc))
            return tuple(out)

        carry = tuple((jnp.zeros((bq, 1), F32), jnp.zeros((bq, dh), F32)) for _ in range(hg))
        for dgl in reversed(range(per)):
            carry = visit(carry, i * per + dgl, dgl * bk)
        n, res = lax.while_loop(live, lambda s: (s[0] + 1, visit(s[1], nkb - 1 - s[0], 0)), (jnp.int32(per), carry))
        for hh in range(hg):
            rt_ref[hh] = res[hh][0]
            o_ref[:, _lanes(hh)] = res[hh][1]
            cnt_ref[hh] = jnp.full((bq, 1), n, F32)
        if ns:
            pl.when(before_last)(forward)
            pl.when(last)(finish)

    return pl.pallas_call(
        body, name=name, grid=grid,
        in_specs=[pl.BlockSpec((bq, ATT_WIDTH), lambda h, i: (i, h)),
                  pl.BlockSpec((rows, ATT_WIDTH), lambda h, i: (0, ATT_TILES + h)),
                  pl.BlockSpec((rows, ATT_WIDTH), lambda h, i: (0, 2 * ATT_TILES + h))] + [HBM] * ns,
        out_specs=[pl.BlockSpec((bq, ATT_WIDTH), lambda h, i: (i, h)),
                   pl.BlockSpec((hg, bq, 1), lambda h, i: (h, i, 0)),
                   pl.BlockSpec((hg, bq, 1), lambda h, i: (h, i, 0))] + [HBM] * ns,
        out_shape=[jax.ShapeDtypeStruct((rows, GROUP), F32), jax.ShapeDtypeStruct((nh, rows, 1), F32),
                   jax.ShapeDtypeStruct((nh, rows, 1), F32)]
        + [jax.ShapeDtypeStruct((N_CHIPS,) + s.shape, s.dtype) for s in shards],
        scratch_shapes=_gather_sems(ns) if ns else [],
        compiler_params=_params("arbitrary", "arbitrary"),
    )(qkv, qkv, qkv, *shards)


def _sb_bwd(name, qkv, rt, cnt, do, parts=()):
    rows = qkv.shape[0]
    nh, dh = N_HEADS, HEAD
    bq, bk, hg = _row_tile(rows), ATT_BLOCK, ATT_HEADS
    per = bq // bk
    scale = dh ** -0.5
    ns = len(parts)
    grid = (nh // hg, rows // bq)

    def body(q_ref, k_ref, v_ref, rt_ref, cnt_ref, do_ref, *rest):
        dq_ref, dk_ref, dv_ref = rest[ns:ns + 3]
        if ns:
            at_first, before_last, at_last = _exchange_steps(grid)
            start, forward, finish = _reduce_exchange(rest[:ns], rest[ns + 3:2 * ns + 3], rest[2 * ns + 3:3 * ns + 3],
                                             rest[3 * ns + 3:])
            pl.when(at_first)(start)
        i = pl.program_id(1)

        @pl.when(i == 0)
        def _():
            dk_ref[...] = jnp.zeros_like(dk_ref)
            dv_ref[...] = jnp.zeros_like(dv_ref)

        upto = _tri2(bk, lambda r, c: r <= c)
        before = _tri2(bk, lambda r, c: r < c)
        nkb = (i + 1) * per
        first = nkb - jnp.max(cnt_ref[0]).astype(jnp.int32)

        def visit(carry, jb, r0):
            off = pl.multiple_of(jb * bk, bk)
            out = []
            for hh in range(hg):
                keep_sum, g_sum, dq = carry[hh]
                qb, dob = q_ref[r0:, _lanes(hh)], do_ref[r0:, _lanes(hh)]
                kb = k_ref[pl.ds(off, bk), _lanes(hh)]
                vb = v_ref[pl.ds(off, bk), _lanes(hh)]
                z, valid, e, log_keep = _sb_block(qb, kb, i * bq + r0, jb, scale)
                log_rest = rt_ref[hh, r0:, :] - keep_sum[r0:] - _dot_split(log_keep, upto)
                attn = jnp.where(valid, jnp.exp(z + log_keep + log_rest), 0.0)
                g = attn * _dot_nt(dob, vb)
                g_before = g_sum[r0:] + _dot_split(g, before)
                inv = 1.0 / (1.0 + e)
                sig = jnp.where(z >= 0, inv, e * inv)
                dz = (jnp.where(valid, g * (1.0 - sig) - g_before * sig, 0.0) * scale).astype(BF16)
                dk_ref[pl.ds(off, bk), _lanes(hh)] += _dot_tn(dz, qb)
                dv_ref[pl.ds(off, bk), _lanes(hh)] += _dot_tn(attn.astype(BF16), dob)
                new = (keep_sum[r0:] + jnp.sum(log_keep, axis=-1, keepdims=True),
                       g_sum[r0:] + jnp.sum(g, axis=-1, keepdims=True),
                       dq[r0:] + _dot(dz, kb))
                if r0:
                    new = tuple(jnp.concatenate([old[:r0], x], axis=0) for old, x in zip(carry[hh], new))
                out.append(new)
            return tuple(out)

        zero = jnp.zeros((bq, 1), F32)
        res = lax.fori_loop(first, nkb - per, lambda jb, c: visit(c, jb, 0),
                            tuple((zero, zero, jnp.zeros((bq, dh), F32)) for _ in range(hg)))
        for dgl in range(per):
            res = visit(res, i * per + dgl, dgl * bk)
        for hh in range(hg):
            dq_ref[:, _lanes(hh)] = res[hh][2]
        if ns:
            pl.when(before_last)(forward)
            pl.when(at_last)(finish)

    return pl.pallas_call(
        body, name=name, grid=grid,
        in_specs=[pl.BlockSpec((bq, ATT_WIDTH), lambda h, i: (i, h)),
                  pl.BlockSpec((rows, ATT_WIDTH), lambda h, i: (0, ATT_TILES + h)),
                  pl.BlockSpec((rows, ATT_WIDTH), lambda h, i: (0, 2 * ATT_TILES + h)),
                  pl.BlockSpec((hg, bq, 1), lambda h, i: (h, i, 0)),
                  pl.BlockSpec((hg, bq, 1), lambda h, i: (h, i, 0)),
                  pl.BlockSpec((bq, ATT_WIDTH), lambda h, i: (i, h))] + [HBM] * ns,
        out_specs=[pl.BlockSpec((bq, ATT_WIDTH), lambda h, i: (i, h)),
                   pl.BlockSpec((rows, ATT_WIDTH), lambda h, i: (0, h)),
                   pl.BlockSpec((rows, ATT_WIDTH), lambda h, i: (0, h))] + [HBM] * (2 * ns),
        out_shape=[jax.ShapeDtypeStruct((rows, GROUP), F32)] * 3
        + [jax.ShapeDtypeStruct(s.shape, s.dtype) for s in parts] * 2,
        scratch_shapes=_reduce_sems(ns) if ns else [],
        compiler_params=_params("arbitrary", "arbitrary"),
    )(qkv, qkv, qkv, rt, cnt, do, *parts)


def _head_sum(x, ones_bd):
    return _dot_split(x, ones_bd)


def _rwkv_pre(p, p_prev, mu, w0, a0, k_k, k_a, w_up, a_up, g_up, ones_bd):
    xs = p + (p_prev - p) * mu
    r = xs[:, :GROUP]
    k0 = xs[:, GROUP:2 * GROUP]
    v = xs[:, 2 * GROUP:3 * GROUP]
    lo = xs[:, 3 * GROUP:]
    wa = w0 + _dot(jnp.tanh(lo).astype(BF16), w_up.astype(BF16))
    w = -(jnp.maximum(-wa, 0.0) + jnp.log(1.0 + jnp.exp(-jnp.abs(wa)))) - 0.5
    log_decay = -jnp.exp(w)
    alpha = _sigmoid(a0 + _dot(lo.astype(BF16), a_up.astype(BF16)))
    gate = _dot(_sigmoid(lo).astype(BF16), g_up.astype(BF16))
    kk = k0 * k_k
    kk = kk * lax.rsqrt(jnp.maximum(_head_sum(kk * kk, ones_bd), 1e-24))
    k = k0 * (1.0 + (alpha - 1.0) * k_a)
    return r, log_decay, k, v, -kk, kk * alpha, gate


def _rwkv_post(y, r, k, v, gate, lnx_w, lnx_b, r_k, ones_bd):
    mean = _head_sum(y, ones_bd) * (1.0 / HEAD)
    yc = y - mean
    var = _head_sum(yc * yc, ones_bd) * (1.0 / HEAD)
    yn = yc * lax.rsqrt(var + LNX_EPS) * lnx_w + lnx_b
    bonus = _head_sum(r * k * r_k, ones_bd) * v
    return (yn + bonus) * gate


_PRE_VEC = 5
_PRE_MAT = 3


def _heads(x):
    return jnp.stack([x[:, _lanes(h)] for h in range(N_HEADS)])


def _unheads(x):
    return jnp.concatenate([x[h] for h in range(N_HEADS)], axis=1)


def _edge_spec(tm, width, tile_of):
    return pl.BlockSpec((8, width), lambda i: (jnp.maximum(tile_of(i) * (tm // 8) - 1, 0), 0))


def _previous_rows(p_ref, edge_ref, tile):
    p = p_ref[...]
    edge = jnp.where(tile == 0, 0.0, edge_ref[7:8, :])
    row = lax.broadcasted_iota(jnp.int32, (p.shape[0], 1), 0)
    return jnp.where(row == 0, edge, pltpu.roll(p, 1, axis=0))


def _rwkv_pre_fwd(name, p, vecs, mats, ones_bd):
    rows = p.shape[0]
    tm = _row_tile(rows)
    row_spec = lambda w: pl.BlockSpec((tm, w), lambda i: (i, 0))
    full = lambda a: pl.BlockSpec(a.shape, lambda i: (0,) * a.ndim)

    def body(p_ref, edge_ref, *refs):
        ins = [r[...] for r in refs[:_PRE_VEC + _PRE_MAT + 1]]
        outs = refs[_PRE_VEC + _PRE_MAT + 1:]
        prev = _previous_rows(p_ref, edge_ref, pl.program_id(0))
        for o_ref, val in zip(outs, _rwkv_pre(p_ref[...], prev, *ins)):
            o_ref[...] = val

    return pl.pallas_call(
        body, name=name, grid=(rows // tm,),
        in_specs=([row_spec(RW_COLS), _edge_spec(tm, RW_COLS, lambda i: i)]
                  + [full(a) for a in (*vecs, *mats, ones_bd)]),
        out_specs=[row_spec(GROUP)] * 7,
        out_shape=[jax.ShapeDtypeStruct((rows, GROUP), F32)] * 7,
        compiler_params=_params("arbitrary"),
    )(p, p, *vecs, *mats, ones_bd)


def _rwkv_pre_bwd(name, p, vecs, mats, ones_bd, cts_scan, ct_gate, cts_b):
    rows = p.shape[0]
    tm = _row_tile(rows)
    nt = rows // tm
    n_par = _PRE_VEC + _PRE_MAT
    tile_of = lambda i: nt - 1 - i
    row_spec = lambda w: pl.BlockSpec((tm, w), lambda i: (tile_of(i), 0))
    full = lambda a: pl.BlockSpec(a.shape, lambda i: (0,) * a.ndim)

    def body(*refs):
        p_ref, edge_ref = refs[0], refs[1]
        par = [r[...] for r in refs[2:2 + n_par]]
        ones = refs[2 + n_par][...]
        cta = [r[...] for r in refs[3 + n_par:10 + n_par]]
        ctb = [r[...] for r in refs[10 + n_par:13 + n_par]]
        dp_ref, par_outs, carry = refs[13 + n_par], refs[14 + n_par:-1], refs[-1]
        step = pl.program_id(0)

        @pl.when(step == 0)
        def _():
            carry[...] = jnp.zeros_like(carry)
            for o_ref in par_outs:
                o_ref[...] = jnp.zeros_like(o_ref)

        ct = (cta[0] + ctb[0], cta[1], cta[2] + ctb[1], cta[3] + ctb[2], cta[4], cta[5], cta[6])
        _, vjp = jax.vjp(lambda pv, ppv, *pr: _rwkv_pre(pv, ppv, *pr, ones),
                         p_ref[...], _previous_rows(p_ref, edge_ref, tile_of(step)), *par)
        grads = vjp(ct)
        row = lax.broadcasted_iota(jnp.int32, (tm, 1), 0)
        dp_ref[...] = grads[0] + jnp.where(row == tm - 1, carry[0:1, :], pltpu.roll(grads[1], tm - 1, axis=0))
        carry[0:1, :] = grads[1][0:1, :]
        for o_ref, gval in zip(par_outs, grads[2:]):
            o_ref[...] += gval

    par_arrays = (*vecs, *mats)
    return pl.pallas_call(
        body, name=name, grid=(nt,),
        in_specs=([row_spec(RW_COLS), _edge_spec(tm, RW_COLS, tile_of)] + [full(a) for a in (*par_arrays, ones_bd)]
                  + [row_spec(GROUP)] * 10),
        out_specs=[row_spec(RW_COLS)] + [full(a) for a in par_arrays],
        out_shape=[jax.ShapeDtypeStruct((rows, RW_COLS), F32)] + [jax.ShapeDtypeStruct(a.shape, F32) for a in par_arrays],
        scratch_shapes=[pltpu.VMEM((8, RW_COLS), F32)],
        compiler_params=_params("arbitrary"),
    )(p, p, *par_arrays, ones_bd, *cts_scan, ct_gate, *cts_b)


def _rwkv_post_fwd(name, y, r, k, v, gate, vecs, ones_bd):
    rows = r.shape[0]
    tm = _row_tile(rows)
    row_spec = pl.BlockSpec((tm, GROUP), lambda i: (i, 0))
    full = lambda a: pl.BlockSpec(a.shape, lambda i: (0,) * a.ndim)

    def body(*refs):
        refs[-1][...] = _rwkv_post(*(r_[...] for r_ in refs[:-1]))

    return pl.pallas_call(
        body, name=name, grid=(rows // tm,),
        in_specs=[row_spec] * 5 + [full(a) for a in (*vecs, ones_bd)],
        out_specs=row_spec,
        out_shape=jax.ShapeDtypeStruct((rows, GROUP), F32),
        compiler_params=_params("arbitrary"),
    )(y, r, k, v, gate, *vecs, ones_bd)


def _rwkv_post_bwd(name, y, r, k, v, gate, vecs, ones_bd, dout):
    rows = r.shape[0]
    tm = _row_tile(rows)
    row_spec = pl.BlockSpec((tm, GROUP), lambda i: (i, 0))
    full = lambda a: pl.BlockSpec(a.shape, lambda i: (0,) * a.ndim)

    def body(*refs):
        vals = [r_[...] for r_ in refs[:8]]
        ones = refs[8][...]
        dout_v = refs[9][...]
        outs = refs[10:]
        _, vjp = jax.vjp(lambda *a: _rwkv_post(*a, ones), *vals)
        grads = vjp(dout_v)
        for o_ref, gval in zip(outs[:5], grads[:5]):
            o_ref[...] = gval

        @pl.when(pl.program_id(0) == 0)
        def _():
            for o_ref in outs[5:]:
                o_ref[...] = jnp.zeros_like(o_ref)

        for o_ref, gval in zip(outs[5:], grads[5:]):
            o_ref[...] += gval

    return pl.pallas_call(
        body, name=name, grid=(rows // tm,),
        in_specs=[row_spec] * 5 + [full(a) for a in (*vecs, ones_bd)] + [row_spec],
        out_specs=[row_spec] * 5 + [full(a) for a in vecs],
        out_shape=[jax.ShapeDtypeStruct((rows, GROUP), F32)] * 5 + [jax.ShapeDtypeStruct(a.shape, F32) for a in vecs],
        compiler_params=_params("arbitrary"),
    )(y, r, k, v, gate, *vecs, ones_bd, dout)


_NN = (((2,), (1,)), ((0,), (0,)))
_NT = (((2,), (2,)), ((0,), (0,)))
_TN = (((1,), (1,)), ((0,), (0,)))


_BWD_FORMS = {"nn": (("nt", False), ("tn", False)),
              "nt": (("nn", False), ("tn", True)),
              "tn": (("nt", True), ("nn", False))}
_DIMS = {"nn": _NN, "nt": _NT, "tn": _TN}


def _bdot(a, b, form):
    return lax.dot_general(a.astype(BF16), b.astype(BF16), _DIMS[form], preferred_element_type=F32)


@functools.partial(jax.custom_vjp, nondiff_argnums=(2,))
def _bmm(a, b, form):
    return _bdot(a, b, form)


def _bmm_fwd(a, b, form):
    return _bdot(a, b, form), (a.astype(BF16), b.astype(BF16))


def _bmm_bwd(form, res, dc):
    a, b = res
    (fa, swap_a), (fb, swap_b) = _BWD_FORMS[form]
    da = _bdot(b, dc, fa) if swap_a else _bdot(dc, b, fa)
    db = _bdot(dc, a, fb) if swap_b else _bdot(a, dc, fb)
    return da, db


_bmm.defvjp(_bmm_fwd, _bmm_bwd)


@jax.custom_vjp
def _cumsum_steps(x):
    return _tri_apply(x, lambda r, c: r >= c)


def _tri_apply(x, cmp):
    nh, c, _ = x.shape
    tri = cmp(lax.broadcasted_iota(jnp.int32, (c, c), 0), lax.broadcasted_iota(jnp.int32, (c, c), 1))
    tri = jnp.broadcast_to(tri.astype(BF16)[None], (nh, c, c))
    hi, lo = _split2(x)
    return (lax.dot_general(tri, hi, _NN, preferred_element_type=F32)
            + lax.dot_general(tri, lo, _NN, preferred_element_type=F32))


_cumsum_steps.defvjp(lambda x: (_cumsum_steps(x), None), lambda _, d: (_tri_apply(d, lambda r, c: r <= c),))


@jax.custom_vjp
def _neumann(n_mat):
    c = n_mat.shape[1]
    inv, power, span = n_mat, _bmm(n_mat, n_mat, "nn"), 2
    while span < c:
        both = _bmm(jnp.concatenate([power, inv], axis=1), power, "nn")
        inv = inv + power + both[:, c:]
        power = both[:, :c]
        span *= 2
    return inv


def _neumann_fwd(n_mat):
    inv = _neumann(n_mat)
    return inv, inv


def _neumann_bwd(inv, d):
    left = d + _bmm(inv, d, "tn")
    return (left + _bmm(left, inv, "nt"),)


_neumann.defvjp(_neumann_fwd, _neumann_bwd)


def _chunk(state, r, log_w, k, v, a, b):
    nh, c, _ = r.shape
    row = lax.broadcasted_iota(jnp.int32, (c, c), 0)
    col = lax.broadcasted_iota(jnp.int32, (c, c), 1)
    cum = _cumsum_steps(log_w)
    mid = cum[:, c // 2 - 1:c // 2, :]
    a_t = a * jnp.exp(cum - log_w - mid)
    r_t = r * jnp.exp(cum - mid)
    back = jnp.exp(mid - cum)
    b_t = b * back
    k_t = k * back
    strict, incl = (row > col)[None], (row >= col)[None]
    ar = jnp.concatenate([a_t, r_t], axis=1)
    on_b = _bmm(ar, b_t, "nt")
    on_k = _bmm(ar, k_t, "nt")
    n_mat = jnp.where(strict, on_b[:, :c], 0.0)
    p_mat = jnp.where(incl, on_b[:, c:], 0.0)
    m_mat = jnp.where(strict, on_k[:, :c], 0.0)
    q_mat = jnp.where(incl, on_k[:, c:], 0.0)
    inv = _neumann(n_mat)
    s_mid = state * jnp.swapaxes(jnp.exp(mid), 1, 2)
    x = _bmm(jnp.concatenate([a_t, m_mat], axis=2), jnp.concatenate([s_mid, v], axis=1), "nn")
    u = x + _bmm(inv, x, "nn")
    y = _bmm(jnp.concatenate([r_t, p_mat, q_mat], axis=2), jnp.concatenate([s_mid, u, v], axis=1), "nn")
    grown = _bmm(jnp.concatenate([b_t, k_t], axis=1), jnp.concatenate([u, v], axis=1), "tn")
    s_new = (s_mid + grown) * jnp.swapaxes(jnp.exp(cum[:, c - 1:c, :] - mid), 1, 2)
    return y, s_new


def _scan_fwd(name, ops):
    rows = ops[0].shape[0]
    nh, dh = N_HEADS, HEAD
    nc, per = rows // CHUNK, SCAN_CHUNKS
    spec = pl.BlockSpec((per * CHUNK, GROUP), lambda c: (c, 0))

    def body(r_ref, w_ref, k_ref, v_ref, a_ref, b_ref, y_ref, st_ref, state):
        @pl.when(pl.program_id(0) == 0)
        def _():
            state[...] = jnp.zeros_like(state)

        s = state[...]
        for u in range(per):
            at = slice(u * CHUNK, (u + 1) * CHUNK)
            st_ref[u] = s
            y, s = _chunk(s, *(_heads(ref[at, :]) for ref in (r_ref, w_ref, k_ref, v_ref, a_ref, b_ref)))
            y_ref[at, :] = _unheads(y)
        state[...] = s

    return pl.pallas_call(
        body, name=name, grid=(nc // per,),
        in_specs=[spec] * 6,
        out_specs=[spec, pl.BlockSpec((per, nh, dh, dh), lambda c: (c, 0, 0, 0))],
        out_shape=[jax.ShapeDtypeStruct((rows, GROUP), F32), jax.ShapeDtypeStruct((nc, nh, dh, dh), F32)],
        scratch_shapes=[pltpu.VMEM((nh, dh, dh), F32)],
        compiler_params=_params("arbitrary"),
    )(*ops)


def _scan_bwd(name, ops, states, dy):
    rows = ops[0].shape[0]
    nh, dh = N_HEADS, HEAD
    nc, per = rows // CHUNK, SCAN_CHUNKS
    steps = nc // per
    spec = pl.BlockSpec((per * CHUNK, GROUP), lambda c: (steps - 1 - c, 0))

    def body(r_ref, w_ref, k_ref, v_ref, a_ref, b_ref, st_ref, dy_ref, *rest):
        outs, dstate = rest[:6], rest[6]

        @pl.when(pl.program_id(0) == 0)
        def _():
            dstate[...] = jnp.zeros_like(dstate)

        ds = dstate[...]
        for u in reversed(range(per)):
            at = slice(u * CHUNK, (u + 1) * CHUNK)
            _, vjp = jax.vjp(_chunk, st_ref[u],
                             *(_heads(ref[at, :]) for ref in (r_ref, w_ref, k_ref, v_ref, a_ref, b_ref)))
            grads = vjp((_heads(dy_ref[at, :]), ds))
            ds = grads[0]
            for o_ref, gval in zip(outs, grads[1:]):
                o_ref[at, :] = _unheads(gval)
        dstate[...] = ds

    return pl.pallas_call(
        body, name=name, grid=(steps,),
        in_specs=[spec] * 6 + [pl.BlockSpec((per, nh, dh, dh), lambda c: (steps - 1 - c, 0, 0, 0)), spec],
        out_specs=[spec] * 6,
        out_shape=[jax.ShapeDtypeStruct((rows, GROUP), F32)] * 6,
        scratch_shapes=[pltpu.VMEM((nh, dh, dh), F32)],
        compiler_params=_params("arbitrary"),
    )(*ops, states, dy)


def _pad_cols(x, cols):
    return jnp.concatenate([x, jnp.zeros(x.shape[:-1] + (cols - x.shape[-1],), x.dtype)], axis=-1)


def _lora_pad(w_up, a_up, g_up):
    z = lambda n: jnp.zeros((n, GROUP), F32)
    return (jnp.concatenate([w_up, z(LORA_PAD - LORA_W)], 0),
            jnp.concatenate([z(LORA_W), a_up, z(LORA_PAD - LORA_W - LORA_A)], 0),
            jnp.concatenate([z(LORA_W + LORA_A), g_up, z(LORA_PAD - LORA_W - LORA_A - LORA_G)], 0))


MID = ['w_in']
LATE = ['ffn2_w_gate', 'ffn2_w_up', 'ffn2_w_down', 'w_out']


def _local_step(x, tgt, w, late=None):
    d = x.shape[1]
    zeros = jnp.zeros((META_PAD, d), F32)
    h0 = jnp.concatenate([zeros, w["meta_tokens"], x], axis=0)
    tgt_p = jnp.concatenate([jnp.zeros((ROW0, d), F32), tgt], axis=0)
    ones_bd = ((lax.broadcasted_iota(jnp.int32, (2 * GROUP, GROUP), 0) % GROUP) // HEAD
               == lax.broadcasted_iota(jnp.int32, (2 * GROUP, GROUP), 1) // HEAD).astype(BF16)
    pre_vecs = (_pad_cols(w["rwkv_mu"], RW_COLS), w["rwkv_w0"], w["rwkv_a0"], w["rwkv_k_k"], w["rwkv_k_a"])
    pre_mats = _lora_pad(w["rwkv_w_up"], w["rwkv_a_up"], w["rwkv_g_up"])
    post_vecs = (w["rwkv_lnx_w"], w["rwkv_lnx_b"], w["rwkv_r_k"].reshape(1, GROUP))

    h1, a1, b1, *gathered = _ffn_fwd("ffn1_fwd", h0, w["ffn1_norm"], w["ffn1_w_gate"], w["ffn1_w_up"],
                                     w["ffn1_w_down"], late and ("gather", late.shards["mid"]))
    if late is not None:
        w = {**w, **late.join("mid", gathered)}
    w_in = _pad_cols(w["w_in"], IN_COLS_PAD)
    qkv, p, n2 = _norm_proj("in_proj", h1, w["mix_norm"], w_in)
    sb, rest_total, visited, *gathered = _sb_fwd("sb_fwd", qkv, late.shards["late"] if late else ())
    if late is not None:
        w = {**w, **late.join("late", gathered)}
    pre = _rwkv_pre_fwd("rwkv_pre_fwd", p, pre_vecs, pre_mats, ones_bd)
    scan_ops, token_ops = pre[:6], (pre[0], pre[2], pre[3], pre[6])
    y, states = _scan_fwd("rwkv_scan_fwd", scan_ops)
    rw = _rwkv_post_fwd("rwkv_post_fwd", y, *token_ops, post_vecs, ones_bd)
    h2, mix = _out_proj("out_proj", h1, sb, rw, w["w_out"])
    h3, a2, b2 = _ffn_fwd("ffn2_fwd", h2, w["ffn2_norm"], w["ffn2_w_gate"], w["ffn2_w_up"], w["ffn2_w_down"])
    loss8, dh3, g_final = _loss_head("loss_head", h3, w["final_norm"].reshape(1, d), tgt_p)

    g = {"final_norm": g_final.reshape(d)}
    dh2, da2, db2, s2, n3, dhh3, g["ffn2_norm"] = _ffn_bwd(
        "ffn2_bwd", dh3, h2, w["ffn2_norm"], a2, b2, w["ffn2_w_gate"], w["ffn2_w_up"], w["ffn2_w_down"])
    g["ffn2_w_gate"] = _mm_tn("ffn2_dgate", da2, n3)
    g["ffn2_w_up"] = _mm_tn("ffn2_dup", db2, n3)
    g["ffn2_w_down"] = _mm_tn("ffn2_ddown", s2, dhh3)
    dsb, drw, dh2b = _out_proj_bwd("out_proj_bwd", dh2, w["w_out"])
    g["w_out"] = _mm_tn("out_proj_dw", mix, dh2b)
    dq, dk, dv, *reduced_late = _sb_bwd("sb_bwd", qkv, rest_total, visited, dsb, late.parts("late", g) if late else ())
    post_g = _rwkv_post_bwd("rwkv_post_bwd", y, *token_ops, post_vecs, ones_bd, drw)
    g["rwkv_lnx_w"], g["rwkv_lnx_b"] = post_g[5], post_g[6]
    g["rwkv_r_k"] = post_g[7].reshape(1, N_HEADS, HEAD)
    scan_g = _scan_bwd("rwkv_scan_bwd", scan_ops, states, post_g[0])
    pre_g = _rwkv_pre_bwd("rwkv_pre_bwd", p, pre_vecs, pre_mats, ones_bd, scan_g, post_g[4], post_g[1:4])
    dp = pre_g[0]
    g["rwkv_mu"] = pre_g[1][:, :w["rwkv_mu"].shape[1]]
    g["rwkv_w0"], g["rwkv_a0"], g["rwkv_k_k"], g["rwkv_k_a"] = pre_g[2:6]
    g["rwkv_w_up"] = pre_g[6][:LORA_W]
    g["rwkv_a_up"] = pre_g[7][LORA_W:LORA_W + LORA_A]
    g["rwkv_g_up"] = pre_g[8][LORA_W + LORA_A:LORA_W + LORA_A + LORA_G]
    live = (jnp.arange(h0.shape[0]) >= META_PAD)[:, None]
    dproj = jnp.where(live, jnp.concatenate([dq, dk, dv, dp], axis=1), 0.0).astype(BF16)
    g["w_in"] = _mm_tn("in_proj_dw", n2, dproj)[:, :w["w_in"].shape[1]]
    dh1, g["mix_norm"] = _norm_proj_bwd("in_proj_bwd", dproj, w_in, h1, w["mix_norm"], dh2)
    dh0, da1, db1, s1, n1, dhh1, g["ffn1_norm"], *reduced_mid = _ffn_bwd(
        "ffn1_bwd", dh1, h0, w["ffn1_norm"], a1, b1, w["ffn1_w_gate"], w["ffn1_w_up"], w["ffn1_w_down"],
        late and ("reduce", late.parts("mid", g)))
    g["meta_tokens"] = dh0[META_PAD:ROW0]
    reduced = {"mid": reduced_mid, "late": reduced_late}
    if late is None:
        g["ffn1_w_gate"] = _mm_tn("ffn1_dgate", da1, n1)
        g["ffn1_w_up"] = _mm_tn("ffn1_dup", db1, n1)
        g["ffn1_w_down"] = _mm_tn("ffn1_ddown", s1, dhh1)
    else:
        g["ffn1_w_gate"], reduced["small"] = _mm_tn("ffn1_dgate", da1, n1, ("all_reduce", [late.small(g)]))
        g["ffn1_w_up"], *reduced["gate"] = _mm_tn("ffn1_dup", db1, n1, ("reduce", late.parts("gate", g)))
        g["ffn1_w_down"], *reduced["up"] = _mm_tn("ffn1_ddown", s1, dhh1, ("reduce", late.parts("up", g)))
    return loss8[0, 0], dh0[ROW0:], g, reduced


N_CHIPS = 4
N_DEV = 8
HBM = pl.BlockSpec(memory_space=pltpu.HBM)


def _place():
    return lax.axis_index("x"), lax.axis_index("y"), lax.axis_index("c")


def _other_chips(x, y):
    return [(1 - x, y), (x, 1 - y), (1 - x, 1 - y)]


def _gather_sems(n):
    return [pltpu.SemaphoreType.DMA((3 * n,)), pltpu.SemaphoreType.DMA((3 * n,)), pltpu.SemaphoreType.DMA((n,)),
            pltpu.SemaphoreType.DMA((3 * n,)), pltpu.SemaphoreType.DMA((3 * n,))]


def _gather_exchange(ins, outs, sems):
    n = len(ins)
    half = [r.shape[0] // 2 for r in ins]
    send, recv, local, d2d_send, d2d_recv = sems
    x, y, c = _place()
    me = 2 * x + y
    chips = _other_chips(x, y)

    def rows_of(k, h):
        return pl.ds(pl.multiple_of(h * half[k], 8), half[k])

    def own(k):
        return pltpu.make_async_copy(ins[k], outs[k].at[me], local.at[k])

    def copy(j, k, slot):
        return pltpu.make_async_remote_copy(
            src_ref=ins[k].at[rows_of(k, c)], dst_ref=outs[k].at[slot, rows_of(k, c)],
            send_sem=send.at[j * n + k], recv_sem=recv.at[j * n + k],
            device_id=(chips[j][0], chips[j][1], c), device_id_type=MESH)

    def passed(j, k, h):
        slot = 2 * chips[j][0] + chips[j][1]
        return pltpu.make_async_remote_copy(
            src_ref=outs[k].at[slot, rows_of(k, h)], dst_ref=outs[k].at[slot, rows_of(k, h)],
            send_sem=d2d_send.at[j * n + k], recv_sem=d2d_recv.at[j * n + k],
            device_id=(x, y, 1 - c), device_id_type=MESH)

    def start():
        for k in range(n):
            own(k).start()
        for j in range(3):
            for k in range(n):
                copy(j, k, me).start()

    def forward():
        for j in range(3):
            for k in range(n):
                copy(j, k, 2 * chips[j][0] + chips[j][1]).wait_recv()
                passed(j, k, c).start()

    def finish():
        for j in range(3):
            for k in range(n):
                passed(j, k, 1 - c).wait_recv()
        for j in range(3):
            for k in range(n):
                copy(j, k, me).wait_send()
                passed(j, k, c).wait_send()
        for k in range(n):
            own(k).wait()

    return start, forward, finish


def _gather_shards(name, shards):
    n = len(shards)

    def body(*refs):
        for stage in _gather_exchange(refs[:n], refs[n:2 * n], refs[2 * n:]):
            stage()

    return pl.pallas_call(
        body, name=name,
        in_specs=[HBM] * n, out_specs=[HBM] * n,
        out_shape=[jax.ShapeDtypeStruct((N_CHIPS,) + s.shape, s.dtype) for s in shards],
        scratch_shapes=_gather_sems(n),
    )(*shards)


def _pair_exchange(name, parts):
    n = len(parts)
    half = [s.shape[1] // 2 for s in parts]

    def body(*refs):
        ins, outs = refs[:n], refs[n:2 * n]
        send, recv = refs[2 * n:]
        x, y, c = _place()

        def copy(k):
            rows = pl.ds(pl.multiple_of((1 - c) * half[k], 8), half[k])
            return pltpu.make_async_remote_copy(
                src_ref=ins[k].at[:, rows], dst_ref=outs[k], send_sem=send.at[k], recv_sem=recv.at[k],
                device_id=(x, y, 1 - c), device_id_type=MESH)

        for k in range(n):
            copy(k).start()
        for k in range(n):
            copy(k).wait_recv()
        for k in range(n):
            copy(k).wait_send()

    return pl.pallas_call(
        body, name=name,
        in_specs=[HBM] * n, out_specs=[HBM] * n,
        out_shape=[jax.ShapeDtypeStruct((s.shape[0], s.shape[1] // 2, s.shape[2]), s.dtype) for s in parts],
        scratch_shapes=[pltpu.SemaphoreType.DMA((n,)), pltpu.SemaphoreType.DMA((n,))],
    )(*parts)


def _pair_add(name, part, other):
    nch, rows, cols = part.shape
    half = rows // 2

    def body(p_ref, o_ref, out_ref):
        c = lax.axis_index("c")
        mine = p_ref[0, pl.ds(pl.multiple_of(c * half, 16), half), :]
        out_ref[0] = (mine.astype(F32) + o_ref[0].astype(F32)).astype(out_ref.dtype)

    return pl.pallas_call(
        body, name=name, grid=(nch,),
        in_specs=[pl.BlockSpec((1, rows, cols), lambda j: (j, 0, 0)),
                  pl.BlockSpec((1, half, cols), lambda j: (j, 0, 0))],
        out_specs=pl.BlockSpec((1, half, cols), lambda j: (j, 0, 0)),
        out_shape=jax.ShapeDtypeStruct((nch, half, cols), part.dtype),
        compiler_params=_params("arbitrary"),
    )(part, other)


def _reduce_sems(n):
    return [pltpu.SemaphoreType.DMA((3 * n,)), pltpu.SemaphoreType.DMA((3 * n,)), pltpu.SemaphoreType.DMA((n,)),
            pltpu.SemaphoreType.DMA((n,)), pltpu.SemaphoreType.DMA((n,))]


def _reduce_exchange(ins, got, sib, sems):
    n = len(ins)
    send, recv, local, d2d_send, d2d_recv = sems
    x, y, c = _place()
    me = 2 * x + y
    chips = _other_chips(x, y)

    def own(k):
        return pltpu.make_async_copy(ins[k].at[me], got[k].at[me], local.at[k])

    def copy(j, k, shard, slot):
        return pltpu.make_async_remote_copy(
            src_ref=ins[k].at[shard], dst_ref=got[k].at[slot], send_sem=send.at[j * n + k],
            recv_sem=recv.at[j * n + k], device_id=(chips[j][0], chips[j][1], c), device_id_type=MESH)

    def swap(k):
        return pltpu.make_async_remote_copy(
            src_ref=got[k], dst_ref=sib[k], send_sem=d2d_send.at[k], recv_sem=d2d_recv.at[k],
            device_id=(x, y, 1 - c), device_id_type=MESH)

    def start():
        for k in range(n):
            own(k).start()
        for j in range(3):
            for k in range(n):
                copy(j, k, 2 * chips[j][0] + chips[j][1], me).start()

    def forward():
        for k in range(n):
            own(k).wait()
            for j in range(3):
                copy(j, k, me, 2 * chips[j][0] + chips[j][1]).wait_recv()
            swap(k).start()

    def finish():
        for k in range(n):
            swap(k).wait_recv()
        for j in range(3):
            for k in range(n):
                copy(j, k, me, me).wait_send()
        for k in range(n):
            swap(k).wait_send()

    return start, forward, finish


def _reduce_shards(name, parts):
    n = len(parts)

    def body(*refs):
        for stage in _reduce_exchange(refs[:n], refs[n:2 * n], refs[2 * n:3 * n], refs[3 * n:]):
            stage()

    return pl.pallas_call(
        body, name=name,
        in_specs=[HBM] * n, out_specs=[HBM] * (2 * n),
        out_shape=[jax.ShapeDtypeStruct(s.shape, s.dtype) for s in parts] * 2,
        scratch_shapes=_reduce_sems(n),
    )(*parts)


def _all_reduce_scratch(vec):
    return [pltpu.VMEM((N_DEV,) + vec.shape, F32),
            pltpu.SemaphoreType.DMA((N_DEV - 1,)), pltpu.SemaphoreType.DMA((N_DEV - 1,))]


def _all_reduce_exchange(v_ref, o_ref, buf, send, recv):
    x, y, c = _place()
    me = 4 * x + 2 * y + c
    peers = [(x ^ (r >> 2), y ^ ((r >> 1) & 1), c ^ (r & 1)) for r in range(1, N_DEV)]

    def copy(r, slot):
        px, py, pc = peers[r]
        return pltpu.make_async_remote_copy(
            src_ref=v_ref, dst_ref=buf.at[slot], send_sem=send.at[r], recv_sem=recv.at[r],
            device_id=(px, py, pc), device_id_type=MESH)

    def start():
        for r in range(N_DEV - 1):
            copy(r, me).start()
        buf[me] = v_ref[...]

    def finish():
        for r in range(N_DEV - 1):
            px, py, pc = peers[r]
            copy(r, 4 * px + 2 * py + pc).wait_recv()
        total = buf[0]
        for dev in range(1, N_DEV):
            total = total + buf[dev]
        o_ref[...] = total
        for r in range(N_DEV - 1):
            copy(r, me).wait_send()

    return start, lambda: None, finish


def _adamw(w, g, m, v):
    m = ADAM_B1 * m + (1.0 - ADAM_B1) * g
    v = ADAM_B2 * v + (1.0 - ADAM_B2) * (g * g)
    m_hat = m / (1.0 - ADAM_B1 ** ADAM_STEP)
    v_hat = v / (1.0 - ADAM_B2 ** ADAM_STEP)
    return -ADAM_LR * (m_hat / (jnp.sqrt(v_hat) + ADAM_EPS) + ADAM_WD * w), m, v


def _adamw_shard(name, core, w, m, v, got, sib):
    rows, cols = w.shape
    tr = rows // 4
    spec = pl.BlockSpec((tr, cols), lambda i, c_ref: (i, 0))
    got_spec = pl.BlockSpec((N_CHIPS, tr, cols), lambda i, c_ref: (0, jnp.where(i // 2 == c_ref[0], i % 2, 0), 0))
    sib_spec = pl.BlockSpec((N_CHIPS, tr, cols), lambda i, c_ref: (0, jnp.where(i // 2 == c_ref[0], 0, i % 2), 0))

    def body(c_ref, w_ref, m_ref, v_ref, got_ref, sib_ref, g_ref, d_ref, mo_ref, vo_ref):
        def four(ref):
            return ((ref[0].astype(F32) + ref[1].astype(F32)) + ref[2].astype(F32)) + ref[3].astype(F32)

        g = jnp.where(pl.program_id(0) // 2 == c_ref[0], four(got_ref), four(sib_ref))
        g_ref[...] = g
        d_ref[...], mo_ref[...], vo_ref[...] = _adamw(w_ref[...], g, m_ref[...], v_ref[...])

    return pl.pallas_call(
        body, name=name,
        grid_spec=pltpu.PrefetchScalarGridSpec(
            num_scalar_prefetch=1, grid=(4,),
            in_specs=[spec, spec, spec, got_spec, sib_spec], out_specs=[spec] * 4),
        out_shape=[jax.ShapeDtypeStruct((rows, cols), F32)] * 4,
        compiler_params=_params("arbitrary"),
    )(core, w, m, v, got, sib)


def _adamw_small(name, w, m, v, g):
    def body(w_ref, m_ref, v_ref, g_ref, d_ref, mo_ref, vo_ref):
        d_ref[...], mo_ref[...], vo_ref[...] = _adamw(w_ref[...], g_ref[...], m_ref[...], v_ref[...])

    return pl.pallas_call(body, name=name, out_shape=[jax.ShapeDtypeStruct(w.shape, F32)] * 3)(w, m, v, g)


def _cast_bf16(name, arrays):
    n = len(arrays)

    def body(*refs):
        for i_ref, o_ref in zip(refs[:n], refs[n:]):
            o_ref[...] = i_ref[...].astype(BF16)

    return pl.pallas_call(
        body, name=name, out_shape=[jax.ShapeDtypeStruct(a.shape, BF16) for a in arrays],
        compiler_params=pltpu.CompilerParams(vmem_limit_bytes=VMEM_LIMIT),
    )(*arrays)


def _pack(arrays, rows):
    flat = jnp.concatenate([a.reshape(-1) for a in arrays])
    return jnp.concatenate([flat, jnp.zeros((rows * 128 - flat.shape[0],), F32)]).reshape(rows, 128)


def _unpack(packed, shapes):
    flat, out, at = packed.reshape(-1), [], 0
    for s in shapes:
        size = 1
        for dim in s:
            size *= dim
        out.append(flat[at:at + size].reshape(s))
        at += size
    return out


def _rows_for(shapes):
    total = 0
    for s in shapes:
        size = 1
        for dim in s:
            size *= dim
        total += size
    return -(-total // 1024) * 8


WEIGHTS = ['meta_tokens', 'ffn1_norm', 'ffn1_w_gate', 'ffn1_w_up', 'ffn1_w_down', 'mix_norm', 'w_in', 'rwkv_mu',
           'rwkv_w0', 'rwkv_w_up', 'rwkv_a0', 'rwkv_a_up', 'rwkv_g_up', 'rwkv_k_k', 'rwkv_k_a', 'rwkv_r_k',
           'rwkv_lnx_w', 'rwkv_lnx_b', 'w_out', 'ffn2_norm', 'ffn2_w_gate', 'ffn2_w_up', 'ffn2_w_down', 'final_norm']
COL_CUT = ['ffn1_w_gate', 'ffn1_w_up', 'w_in', 'ffn2_w_gate', 'ffn2_w_up']
ROW_CUT = ['ffn1_w_down', 'w_out', 'ffn2_w_down']
SMALL_CUT = ['meta_tokens', 'rwkv_w_up', 'rwkv_a_up', 'rwkv_g_up']
TRANSPOSED = ['ffn1_w_gate', 'ffn1_w_up', 'ffn2_w_gate', 'ffn2_w_up']
BIG = COL_CUT + ROW_CUT
REPLICATED = [n for n in WEIGHTS if n not in BIG + SMALL_CUT]


def _join_cols(a):
    return a.transpose(1, 0, 2).reshape(a.shape[1], N_CHIPS * a.shape[2])


def _cut_cols(a):
    return a.reshape(a.shape[0], N_CHIPS, a.shape[1] // N_CHIPS).transpose(1, 0, 2)


def _step(x, loss_target, w, m, v):
    two = lambda a: a.reshape(a.shape[-2], a.shape[-1])

    def rows_cut(n, a):
        return jnp.swapaxes(two(a), 0, 1) if n in TRANSPOSED else two(a)

    def as_given(n, a, like):
        return (jnp.swapaxes(a, 0, 1) if n in TRANSPOSED else a).reshape(like.shape)

    col_cut = [n for n in COL_CUT + SMALL_CUT if n not in TRANSPOSED]

    def join(names, gathered):
        return {n: (_join_cols(a) if n in col_cut else a.reshape(-1, a.shape[-1])) for n, a in zip(names, gathered)}

    def pair_sums(tag, names, g):
        parts = [_cut_cols(g[n]) if n in col_cut else g[n].reshape(N_CHIPS, -1, g[n].shape[-1]) for n in names]
        arrived = _pair_exchange("pair_exchange_" + tag, parts)
        return [_pair_add("pair_add_" + n, p, o) for n, p, o in zip(names, parts, arrived)]

    first = [n for n in BIG if n not in MID + LATE]
    gathered_later = {"mid": MID, "late": LATE}
    groups = {**gathered_later, "gate": ["ffn1_w_gate"], "up": ["ffn1_w_up"]}
    cast = dict(zip(BIG, _cast_bf16("cast_weights", [rows_cut(n, w[n]) for n in BIG])))
    names = first + SMALL_CUT
    shards = [cast[n] for n in first] + [two(w[n]) for n in SMALL_CUT]
    full = {n: (two(w[n]) if w[n].ndim == 3 else w[n]) for n in REPLICATED}
    full.update(join(names, _gather_shards("gather_weights", shards)))
    full["rwkv_r_k"] = w["rwkv_r_k"]
    full["final_norm"] = w["final_norm"]

    small_names = REPLICATED + SMALL_CUT

    def small(g):
        return _pack([g[n] for n in small_names], _rows_for([g[n].shape for n in small_names]))

    late = types.SimpleNamespace(shards={k: [cast[n] for n in names] for k, names in gathered_later.items()},
                                 join=lambda k, gathered: join(groups[k], gathered),
                                 parts=lambda k, g: pair_sums(k, groups[k], g), small=small)

    loss, dx, g, reduced = _local_step(x[0], loss_target[0], full, late)
    loss = lax.psum(loss, ("x", "y", "c"))

    groups["down"] = ["ffn1_w_down"]
    reduced["down"] = list(_reduce_shards("reduce_gradients", pair_sums("down", groups["down"], g)))
    got, sib = {}, {}
    for k, names in groups.items():
        got.update(zip(names, reduced[k][:len(names)]))
        sib.update(zip(names, reduced[k][len(names):]))
    g_small = dict(zip(small_names, _unpack(reduced["small"], [g[n].shape for n in small_names])))
    chip = 2 * lax.axis_index("x") + lax.axis_index("y")
    for n in SMALL_CUT:
        width = g_small[n].shape[1] // N_CHIPS
        g_small[n] = lax.dynamic_slice_in_dim(g_small[n], chip * width, width, axis=1)

    grad, delta, new_m, new_v = {}, {}, {}, {}
    core = lax.axis_index("c").astype(jnp.int32).reshape(1)
    for n in BIG:
        outs = _adamw_shard("adamw_" + n, core, rows_cut(n, w[n]), rows_cut(n, m[n]), rows_cut(n, v[n]), got[n], sib[n])
        grad[n], delta[n], new_m[n], new_v[n] = (as_given(n, o, w[n]) for o in outs)
    shapes = [w[n].shape for n in small_names]
    rows = _rows_for(shapes)
    packed = [_pack([t[n] for n in small_names], rows) for t in (w, m, v)]
    g_packed = _pack([g_small[n] for n in small_names], rows)
    outs = [_unpack(o, shapes) for o in _adamw_small("adamw_small", *packed, g_packed)]
    for i, n in enumerate(small_names):
        grad[n] = g_small[n].reshape(w[n].shape)
        delta[n], new_m[n], new_v[n] = outs[0][i], outs[1][i], outs[2][i]
    return loss, dx[None], grad, delta, new_m, new_v


def kernel(x, meta_tokens, ffn1_norm, ffn1_w_gate, ffn1_w_up, ffn1_w_down, mix_norm, w_in, rwkv_mu, rwkv_w0, rwkv_w_up, rwkv_a0, rwkv_a_up, rwkv_g_up, rwkv_k_k, rwkv_k_a, rwkv_r_k, rwkv_lnx_w, rwkv_lnx_b, w_out, ffn2_norm, ffn2_w_gate, ffn2_w_up, ffn2_w_down, final_norm, loss_target, m_meta_tokens, m_ffn1_norm, m_ffn1_w_gate, m_ffn1_w_up, m_ffn1_w_down, m_mix_norm, m_w_in, m_rwkv_mu, m_rwkv_w0, m_rwkv_w_up, m_rwkv_a0, m_rwkv_a_up, m_rwkv_g_up, m_rwkv_k_k, m_rwkv_k_a, m_rwkv_r_k, m_rwkv_lnx_w, m_rwkv_lnx_b, m_w_out, m_ffn2_norm, m_ffn2_w_gate, m_ffn2_w_up, m_ffn2_w_down, m_final_norm, v_meta_tokens, v_ffn1_norm, v_ffn1_w_gate, v_ffn1_w_up, v_ffn1_w_down, v_mix_norm, v_w_in, v_rwkv_mu, v_rwkv_w0, v_rwkv_w_up, v_rwkv_a0, v_rwkv_a_up, v_rwkv_g_up, v_rwkv_k_k, v_rwkv_k_a, v_rwkv_r_k, v_rwkv_lnx_w, v_rwkv_lnx_b, v_w_out, v_ffn2_norm, v_ffn2_w_gate, v_ffn2_w_up, v_ffn2_w_down, v_final_norm):
    w = dict(zip(WEIGHTS, (meta_tokens, ffn1_norm, ffn1_w_gate, ffn1_w_up, ffn1_w_down, mix_norm, w_in, rwkv_mu, rwkv_w0, rwkv_w_up, rwkv_a0, rwkv_a_up, rwkv_g_up, rwkv_k_k, rwkv_k_a, rwkv_r_k, rwkv_lnx_w, rwkv_lnx_b, w_out, ffn2_norm, ffn2_w_gate, ffn2_w_up, ffn2_w_down, final_norm)))
    m = dict(zip(WEIGHTS, (m_meta_tokens, m_ffn1_norm, m_ffn1_w_gate, m_ffn1_w_up, m_ffn1_w_down, m_mix_norm, m_w_in, m_rwkv_mu, m_rwkv_w0, m_rwkv_w_up, m_rwkv_a0, m_rwkv_a_up, m_rwkv_g_up, m_rwkv_k_k, m_rwkv_k_a, m_rwkv_r_k, m_rwkv_lnx_w, m_rwkv_lnx_b, m_w_out, m_ffn2_norm, m_ffn2_w_gate, m_ffn2_w_up, m_ffn2_w_down, m_final_norm)))
    v = dict(zip(WEIGHTS, (v_meta_tokens, v_ffn1_norm, v_ffn1_w_gate, v_ffn1_w_up, v_ffn1_w_down, v_mix_norm, v_w_in, v_rwkv_mu, v_rwkv_w0, v_rwkv_w_up, v_rwkv_a0, v_rwkv_a_up, v_rwkv_g_up, v_rwkv_k_k, v_rwkv_k_a, v_rwkv_r_k, v_rwkv_lnx_w, v_rwkv_lnx_b, v_w_out, v_ffn2_norm, v_ffn2_w_gate, v_ffn2_w_up, v_ffn2_w_down, v_final_norm)))
    loss, grad_x, grad, delta, new_m, new_v = _step(x, loss_target, w, m, v)
    return (loss, grad_x, *[grad[n] for n in WEIGHTS], *[delta[n] for n in WEIGHTS],
            *[new_m[n] for n in WEIGHTS], *[new_v[n] for n in WEIGHTS])
```

```python
import functools
import types

import jax
import jax.numpy as jnp
from jax import lax
from jax.experimental import pallas as pl
from jax.experimental.pallas import tpu as pltpu

F32 = jnp.float32
BF16 = jnp.bfloat16

RMS_EPS = 1e-6
LNX_EPS = 64e-5
N_META = 16
ROW0 = 128
META_PAD = ROW0 - N_META
HEAD = 64
N_HEADS = 8
GROUP = N_HEADS * HEAD
LORA_W, LORA_A, LORA_G = 32, 32, 96
LORA_PAD = 256
RW_COLS = 3 * GROUP + LORA_PAD
IN_COLS_PAD = 3 * GROUP + RW_COLS
ATT_BLOCK = 128
CHUNK = 64
SCAN_CHUNKS = 2
VMEM_LIMIT = 56 * 1024 * 1024

ADAM_LR, ADAM_B1, ADAM_B2, ADAM_EPS, ADAM_WD, ADAM_STEP = 0.001, 0.9, 0.999, 1e-08, 0.01, 10

MESH = pl.DeviceIdType.MESH


def _params(*sem):
    return pltpu.CompilerParams(dimension_semantics=tuple(sem), vmem_limit_bytes=VMEM_LIMIT)


def _dot(a, b):
    return lax.dot_general(a, b, (((1,), (0,)), ((), ())), preferred_element_type=F32)


def _dot_nt(a, b):
    return lax.dot_general(a, b, (((1,), (1,)), ((), ())), preferred_element_type=F32)


def _dot_tn(a, b):
    return lax.dot_general(a, b, (((0,), (0,)), ((), ())), preferred_element_type=F32)


def _split2(x):
    hi = x.astype(BF16)
    return hi, (x - hi.astype(F32)).astype(BF16)


def _sigmoid(x):
    return 1.0 / (1.0 + jnp.exp(-x))


def _rms_fwd(x, g):
    rstd = lax.rsqrt(jnp.mean(x * x, axis=-1, keepdims=True) + RMS_EPS)
    xhat = x * rstd
    return xhat * g, xhat, rstd


def _rms_bwd(dn, xhat, rstd, g):
    dxhat = dn * g
    dx = rstd * (dxhat - xhat * jnp.mean(dxhat * xhat, axis=-1, keepdims=True))
    return dx, jnp.sum(dn * xhat, axis=0, keepdims=True)


def _row_tile(rows):
    return 384 if rows % 384 == 0 else 128


def _half_tile(cols):
    return cols // 2 if cols % 256 == 0 else cols


def _tall_tile(rows, parts):
    return rows // parts if rows % (16 * parts) == 0 else _row_tile(rows)


def _call_with_exchange(name, body, grid, in_specs, out_specs, out_shape, scratch, operands, params, exchange):
    if exchange is None or not exchange[1]:
        return pl.pallas_call(body, name=name, grid=grid, in_specs=in_specs, out_specs=out_specs,
                              out_shape=out_shape, scratch_shapes=scratch, compiler_params=params)(*operands)
    kind, arrays = exchange
    ns, n_in, n_out, n_scr = len(arrays), len(in_specs), len(out_specs), len(scratch)
    whole = lambda a: pl.BlockSpec(a.shape, lambda *_: (0,) * a.ndim)
    if kind == "gather":
        results = [jax.ShapeDtypeStruct((N_CHIPS,) + s.shape, s.dtype) for s in arrays]
        sems, sent_specs, landed_specs = _gather_sems(ns), [HBM] * ns, [HBM] * ns
    elif kind == "reduce":
        results = [jax.ShapeDtypeStruct(s.shape, s.dtype) for s in arrays] * 2
        sems, sent_specs, landed_specs = _reduce_sems(ns), [HBM] * ns, [HBM] * (2 * ns)
    else:
        results = [jax.ShapeDtypeStruct(arrays[0].shape, F32)]
        sems, sent_specs, landed_specs = _all_reduce_scratch(arrays[0]), [whole(arrays[0])], [whole(arrays[0])]
    n_res = len(results)

    def carried(*refs):
        at = n_in + ns + n_out
        sent, landed = refs[n_in:n_in + ns], refs[at:at + n_res]
        own_scratch, sem_refs = refs[at + n_res:at + n_res + n_scr], refs[at + n_res + n_scr:]
        first, before_last, last = _exchange_steps(grid)
        if kind == "gather":
            start, forward, finish = _gather_exchange(sent, landed, sem_refs)
        elif kind == "reduce":
            start, forward, finish = _reduce_exchange(sent, landed[:ns], landed[ns:], sem_refs)
        else:
            start, forward, finish = _all_reduce_exchange(sent[0], landed[0], *sem_refs)
        pl.when(first)(start)
        body(*refs[:n_in], *refs[n_in + ns:at], *own_scratch)
        pl.when(before_last)(forward)
        pl.when(last)(finish)

    return pl.pallas_call(
        carried, name=name, grid=grid, in_specs=list(in_specs) + sent_specs, out_specs=list(out_specs) + landed_specs,
        out_shape=list(out_shape) + results, scratch_shapes=list(scratch) + sems, compiler_params=params,
    )(*operands, *arrays)


def _ffn_fwd(name, h, g, wg, wu, wd, exchange=None):
    rows, d = h.shape
    f = wg.shape[0]
    tm, tf = _row_tile(rows), _half_tile(f)
    nj = f // tf

    def body(h_ref, g_ref, wg_ref, wu_ref, wd_ref, ho_ref, a_ref, b_ref, n_sc, acc_sc):
        j = pl.program_id(1)

        @pl.when(j == 0)
        def _():
            n, _, _ = _rms_fwd(h_ref[...], g_ref[...])
            n_sc[...] = n.astype(BF16)
            acc_sc[...] = jnp.zeros_like(acc_sc)

        n = n_sc[...]
        a = _dot_nt(n, wg_ref[...])
        b = _dot_nt(n, wu_ref[...])
        a_ref[...] = a
        b_ref[...] = b
        s = a * _sigmoid(a) * b
        acc_sc[...] += _dot(s.astype(BF16), wd_ref[...])

        @pl.when(j == nj - 1)
        def _():
            ho_ref[...] = h_ref[...] + 0.5 * acc_sc[...]

    return _call_with_exchange(
        name, body, (rows // tm, nj),
        [pl.BlockSpec((tm, d), lambda i, j: (i, 0)),
         pl.BlockSpec((1, d), lambda i, j: (0, 0)),
         pl.BlockSpec((tf, d), lambda i, j: (j, 0)),
         pl.BlockSpec((tf, d), lambda i, j: (j, 0)),
         pl.BlockSpec((tf, d), lambda i, j: (j, 0))],
        [pl.BlockSpec((tm, d), lambda i, j: (i, 0)),
         pl.BlockSpec((tm, tf), lambda i, j: (i, j)),
         pl.BlockSpec((tm, tf), lambda i, j: (i, j))],
        [jax.ShapeDtypeStruct((rows, d), F32),
         jax.ShapeDtypeStruct((rows, f), F32),
         jax.ShapeDtypeStruct((rows, f), F32)],
        [pltpu.VMEM((tm, d), BF16), pltpu.VMEM((tm, d), F32)],
        (h, g, wg, wu, wd), _params("arbitrary", "arbitrary"), exchange)


def _ffn_bwd(name, dh, h, g, a, b, wg, wu, wd, exchange=None):
    rows, d = h.shape
    f = wg.shape[0]
    tm, tf = _row_tile(rows), _half_tile(f)
    ni, nj = rows // tm, f // tf

    def body(dh_ref, h_ref, g_ref, a_ref, b_ref, wg_ref, wu_ref, wd_ref,
             dhi_ref, da_ref, db_ref, s_ref, n_ref, dhh_ref, dg_ref, dn_sc):
        i, j = pl.program_id(0), pl.program_id(1)

        @pl.when(j == 0)
        def _():
            n, _, _ = _rms_fwd(h_ref[...], g_ref[...])
            n_ref[...] = n.astype(BF16)
            dhh_ref[...] = (0.5 * dh_ref[...]).astype(BF16)
            dn_sc[...] = jnp.zeros_like(dn_sc)

        @pl.when((i == 0) & (j == 0))
        def _():
            dg_ref[...] = jnp.zeros_like(dg_ref)

        ds = _dot_nt(dhh_ref[...], wd_ref[...])
        av, bv = a_ref[...], b_ref[...]
        sig = _sigmoid(av)
        silu = av * sig
        s_ref[...] = (silu * bv).astype(BF16)
        db = (ds * silu).astype(BF16)
        da = (ds * bv * (sig * (1.0 + av * (1.0 - sig)))).astype(BF16)
        da_ref[...] = da
        db_ref[...] = db
        dn_sc[...] += _dot(da, wg_ref[...]) + _dot(db, wu_ref[...])

        @pl.when(j == nj - 1)
        def _():
            gv = g_ref[...]
            _, xhat, rstd = _rms_fwd(h_ref[...], gv)
            dx, dg = _rms_bwd(dn_sc[...], xhat, rstd, gv)
            dhi_ref[...] = dh_ref[...] + dx
            dg_ref[...] += dg

    return _call_with_exchange(
        name, body, (ni, nj),
        [pl.BlockSpec((tm, d), lambda i, j: (i, 0)),
         pl.BlockSpec((tm, d), lambda i, j: (i, 0)),
         pl.BlockSpec((1, d), lambda i, j: (0, 0)),
         pl.BlockSpec((tm, tf), lambda i, j: (i, j)),
         pl.BlockSpec((tm, tf), lambda i, j: (i, j)),
         pl.BlockSpec((tf, d), lambda i, j: (j, 0)),
         pl.BlockSpec((tf, d), lambda i, j: (j, 0)),
         pl.BlockSpec((tf, d), lambda i, j: (j, 0))],
        [pl.BlockSpec((tm, d), lambda i, j: (i, 0)),
         pl.BlockSpec((tm, tf), lambda i, j: (i, j)),
         pl.BlockSpec((tm, tf), lambda i, j: (i, j)),
         pl.BlockSpec((tm, tf), lambda i, j: (i, j)),
         pl.BlockSpec((tm, d), lambda i, j: (i, 0)),
         pl.BlockSpec((tm, d), lambda i, j: (i, 0)),
         pl.BlockSpec((1, d), lambda i, j: (0, 0))],
        [jax.ShapeDtypeStruct((rows, d), F32),
         jax.ShapeDtypeStruct((rows, f), BF16),
         jax.ShapeDtypeStruct((rows, f), BF16),
         jax.ShapeDtypeStruct((rows, f), BF16),
         jax.ShapeDtypeStruct((rows, d), BF16),
         jax.ShapeDtypeStruct((rows, d), BF16),
         jax.ShapeDtypeStruct((1, d), F32)],
        [pltpu.VMEM((tm, d), F32)],
        (dh, h, g, a, b, wg, wu, wd), _params("arbitrary", "arbitrary"), exchange)


def _mm_tn(name, a, b, exchange=None):
    k, m = a.shape
    n = b.shape[1]
    tk = _tall_tile(k, 3)
    tm = _half_tile(m) if m > 1024 else m
    tn = _half_tile(n) if n > 1024 else n
    nk = k // tk

    def body(a_ref, b_ref, o_ref, acc):
        kk = pl.program_id(2)

        @pl.when(kk == 0)
        def _():
            acc[...] = jnp.zeros_like(acc)

        acc[...] += _dot_tn(a_ref[...], b_ref[...])

        @pl.when(kk == nk - 1)
        def _():
            o_ref[...] = acc[...].astype(BF16)

    outs = _call_with_exchange(
        name, body, (m // tm, n // tn, nk),
        [pl.BlockSpec((tk, tm), lambda i, j, kk: (kk, i)),
         pl.BlockSpec((tk, tn), lambda i, j, kk: (kk, j))],
        [pl.BlockSpec((tm, tn), lambda i, j, kk: (i, j))],
        [jax.ShapeDtypeStruct((m, n), BF16)],
        [pltpu.VMEM((tm, tn), F32)],
        (a, b), _params("arbitrary", "arbitrary", "arbitrary"), exchange)
    return outs if exchange else outs[0]


def _norm_proj(name, h, g, w):
    rows, d = h.shape
    n = w.shape[1]
    split = 3 * GROUP
    tm = _row_tile(rows)

    def body(h_ref, g_ref, w_ref, qkv_ref, p_ref, n_ref):
        nv, _, _ = _rms_fwd(h_ref[...], g_ref[...])
        nb = nv.astype(BF16)
        n_ref[...] = nb
        qkv_ref[...] = _dot(nb, w_ref[:, :split]).astype(BF16)
        p_ref[...] = _dot(nb, w_ref[:, split:])

    return pl.pallas_call(
        body, name=name, grid=(rows // tm,),
        in_specs=[pl.BlockSpec((tm, d), lambda i: (i, 0)),
                  pl.BlockSpec((1, d), lambda i: (0, 0)),
                  pl.BlockSpec((d, n), lambda i: (0, 0))],
        out_specs=[pl.BlockSpec((tm, split), lambda i: (i, 0)),
                   pl.BlockSpec((tm, n - split), lambda i: (i, 0)),
                   pl.BlockSpec((tm, d), lambda i: (i, 0))],
        out_shape=[jax.ShapeDtypeStruct((rows, split), BF16), jax.ShapeDtypeStruct((rows, n - split), F32),
                   jax.ShapeDtypeStruct((rows, d), BF16)],
        compiler_params=_params("arbitrary"),
    )(h, g, w)


def _out_proj(name, h, sb, rw, w):
    rows, d = h.shape
    gw = sb.shape[1]
    tm = _row_tile(rows)

    def body(h_ref, sb_ref, rw_ref, w_ref, o_ref, mix_ref):
        mix_ref[:, :gw] = sb_ref[...].astype(BF16)
        mix_ref[:, gw:] = rw_ref[...].astype(BF16)
        o_ref[...] = h_ref[...] + _dot(mix_ref[...], w_ref[...])

    return pl.pallas_call(
        body, name=name, grid=(rows // tm,),
        in_specs=[pl.BlockSpec((tm, d), lambda i: (i, 0)),
                  pl.BlockSpec((tm, gw), lambda i: (i, 0)),
                  pl.BlockSpec((tm, gw), lambda i: (i, 0)),
                  pl.BlockSpec((2 * gw, d), lambda i: (0, 0))],
        out_specs=[pl.BlockSpec((tm, d), lambda i: (i, 0)),
                   pl.BlockSpec((tm, 2 * gw), lambda i: (i, 0))],
        out_shape=[jax.ShapeDtypeStruct((rows, d), F32), jax.ShapeDtypeStruct((rows, 2 * gw), BF16)],
        compiler_params=_params("arbitrary"),
    )(h, sb, rw, w)


def _out_proj_bwd(name, dh, w):
    rows, d = dh.shape
    k = w.shape[0]
    tm = _row_tile(rows)

    def body(dh_ref, w_ref, dsb_ref, drw_ref, dhb_ref):
        dhb = dh_ref[...].astype(BF16)
        dhb_ref[...] = dhb
        dsb_ref[...] = _dot_nt(dhb, w_ref[:GROUP, :]).astype(BF16)
        drw_ref[...] = _dot_nt(dhb, w_ref[GROUP:, :])

    return pl.pallas_call(
        body, name=name, grid=(rows // tm,),
        in_specs=[pl.BlockSpec((tm, d), lambda i: (i, 0)),
                  pl.BlockSpec((k, d), lambda i: (0, 0))],
        out_specs=[pl.BlockSpec((tm, GROUP), lambda i: (i, 0)),
                   pl.BlockSpec((tm, GROUP), lambda i: (i, 0)),
                   pl.BlockSpec((tm, d), lambda i: (i, 0))],
        out_shape=[jax.ShapeDtypeStruct((rows, GROUP), BF16), jax.ShapeDtypeStruct((rows, GROUP), F32),
                   jax.ShapeDtypeStruct((rows, d), BF16)],
        compiler_params=_params("arbitrary"),
    )(dh, w)


def _norm_proj_bwd(name, dproj, w, h, g, dh):
    rows, n = dproj.shape
    d = w.shape[0]
    tm = _row_tile(rows)

    def body(dp_ref, w_ref, h_ref, g_ref, dh_ref, o_ref, dg_ref):
        @pl.when(pl.program_id(0) == 0)
        def _():
            dg_ref[...] = jnp.zeros_like(dg_ref)

        dn = _dot_nt(dp_ref[...], w_ref[...])
        gv = g_ref[...]
        _, xhat, rstd = _rms_fwd(h_ref[...], gv)
        dx, dg = _rms_bwd(dn, xhat, rstd, gv)
        o_ref[...] = dh_ref[...] + dx
        dg_ref[...] += dg

    return pl.pallas_call(
        body, name=name, grid=(rows // tm,),
        in_specs=[pl.BlockSpec((tm, n), lambda i: (i, 0)),
                  pl.BlockSpec((d, n), lambda i: (0, 0)),
                  pl.BlockSpec((tm, d), lambda i: (i, 0)),
                  pl.BlockSpec((1, d), lambda i: (0, 0)),
                  pl.BlockSpec((tm, d), lambda i: (i, 0))],
        out_specs=[pl.BlockSpec((tm, d), lambda i: (i, 0)),
                   pl.BlockSpec((1, d), lambda i: (0, 0))],
        out_shape=[jax.ShapeDtypeStruct((rows, d), F32), jax.ShapeDtypeStruct((1, d), F32)],
        compiler_params=_params("arbitrary"),
    )(dproj, w, h, g, dh)


def _loss_head(name, h, g, tgt):
    rows, d = h.shape
    tm = _row_tile(rows)

    def body(h_ref, g_ref, t_ref, loss_ref, dh_ref, dg_ref):
        i = pl.program_id(0)

        @pl.when(i == 0)
        def _():
            loss_ref[...] = jnp.zeros_like(loss_ref)
            dg_ref[...] = jnp.zeros_like(dg_ref)

        gv = g_ref[...]
        y, xhat, rstd = _rms_fwd(h_ref[...], gv)
        row = i * tm + lax.broadcasted_iota(jnp.int32, (tm, 1), 0)
        diff = jnp.where(row >= ROW0, y - t_ref[...], 0.0)
        part = 0.5 * jnp.sum(jnp.sum(diff * diff, axis=-1, keepdims=True), axis=0, keepdims=True) / d
        loss_ref[...] += jnp.broadcast_to(part, loss_ref.shape)
        dx, dg = _rms_bwd(diff / d, xhat, rstd, gv)
        dh_ref[...] = dx
        dg_ref[...] += dg

    return pl.pallas_call(
        body, name=name, grid=(rows // tm,),
        in_specs=[pl.BlockSpec((tm, d), lambda i: (i, 0)),
                  pl.BlockSpec((1, d), lambda i: (0, 0)),
                  pl.BlockSpec((tm, d), lambda i: (i, 0))],
        out_specs=[pl.BlockSpec((8, 128), lambda i: (0, 0)),
                   pl.BlockSpec((tm, d), lambda i: (i, 0)),
                   pl.BlockSpec((1, d), lambda i: (0, 0))],
        out_shape=[jax.ShapeDtypeStruct((8, 128), F32),
                   jax.ShapeDtypeStruct((rows, d), F32),
                   jax.ShapeDtypeStruct((1, d), F32)],
        compiler_params=_params("arbitrary"),
    )(h, g, tgt)


def _sb_block(qb, kb, q0, jb, scale):
    bq, bk = qb.shape[0], kb.shape[0]
    z = _dot_nt(qb, kb) * scale
    qpos = q0 + lax.broadcasted_iota(jnp.int32, (bq, bk), 0)
    kpos = jb * bk + lax.broadcasted_iota(jnp.int32, (bq, bk), 1)
    valid = (kpos < qpos) & (kpos >= META_PAD)
    e = jnp.exp(-jnp.abs(z))
    log_keep = jnp.where(valid, -(jnp.maximum(z, 0.0) + jnp.log(1.0 + e)), 0.0)
    return z, valid, e, log_keep


def _tri2(n, cmp):
    r = lax.broadcasted_iota(jnp.int32, (2 * n, n), 0) % n
    c = lax.broadcasted_iota(jnp.int32, (2 * n, n), 1)
    return cmp(r, c).astype(BF16)


def _dot_split(x, t2):
    hi, lo = _split2(x)
    return _dot(jnp.concatenate([hi, lo], axis=1), t2)


ATT_HEADS = 8
ATT_WIDTH = ATT_HEADS * HEAD
ATT_CUT = -104.0
ATT_TILES = GROUP // ATT_WIDTH


def _lanes(hh):
    return slice(hh * HEAD, (hh + 1) * HEAD)


def _exchange_steps(grid):
    step, total = pl.program_id(0), 1
    for a in range(1, len(grid)):
        step = step * grid[a] + pl.program_id(a)
    for size in grid:
        total *= size
    return step == 0, step == max(total - 2, 0), step == total - 1


def _sb_fwd(name, qkv, shards=()):
    rows = qkv.shape[0]
    nh, dh = N_HEADS, HEAD
    bq, bk, hg = _row_tile(rows), ATT_BLOCK, ATT_HEADS
    per = bq // bk
    scale = dh ** -0.5
    ns = len(shards)
    grid = (nh // hg, rows // bq)

    def body(q_ref, k_ref, v_ref, *rest):
        o_ref, rt_ref, cnt_ref = rest[ns:ns + 3]
        if ns:
            first, before_last, last = _exchange_steps(grid)
            start, forward, finish = _gather_exchange(rest[:ns], rest[ns + 3:2 * ns + 3], rest[2 * ns + 3:])
            pl.when(first)(start)
        i = pl.program_id(1)
        after = _tri2(bk, lambda r, c: r > c)
        nkb = (i + 1) * per

        def live(state):
            n, carry = state
            top = jnp.max(carry[0][0])
            for hh in range(1, hg):
                top = jnp.maximum(top, jnp.max(carry[hh][0]))
            return (n < nkb) & (top >= ATT_CUT)

        def visit(carry, jb, r0):
            off = pl.multiple_of(jb * bk, bk)
            out = []
            for hh in range(hg):
                rest, acc = carry[hh]
                kb = k_ref[pl.ds(off, bk), _lanes(hh)]
                vb = v_ref[pl.ds(off, bk), _lanes(hh)]
                z, valid, _, log_keep = _sb_block(q_ref[r0:, _lanes(hh)], kb, i * bq + r0, jb, scale)
                log_rest = rest[r0:] + _dot_split(log_keep, after)
                attn = jnp.where(valid, jnp.exp(z + log_keep + log_rest), 0.0)
                new_rest = rest[r0:] + jnp.sum(log_keep, axis=-1, keepdims=True)
                new_acc = acc[r0:] + _dot(attn.astype(BF16), vb)
                if r0:
                    new_rest = jnp.concatenate([rest[:r0], new_rest], axis=0)
                    new_acc = jnp.concatenate([acc[:r0], new_acc], axis=0)
                out.append((new_rest, new_acc))
            return tuple(out)

        carry = tuple((jnp.zeros((bq, 1), F32), jnp.zeros((bq, dh), F32)) for _ in range(hg))
        for dgl in reversed(range(per)):
            carry = visit(carry, i * per + dgl, dgl * bk)
        n, res = lax.while_loop(live, lambda s: (s[0] + 1, visit(s[1], nkb - 1 - s[0], 0)), (jnp.int32(per), carry))
        for hh in range(hg):
            rt_ref[hh] = res[hh][0]
            o_ref[:, _lanes(hh)] = res[hh][1]
            cnt_ref[hh] = jnp.full((bq, 1), n, F32)
        if ns:
            pl.when(before_last)(forward)
            pl.when(last)(finish)

    return pl.pallas_call(
        body, name=name, grid=grid,
        in_specs=[pl.BlockSpec((bq, ATT_WIDTH), lambda h, i: (i, h)),
                  pl.BlockSpec((rows, ATT_WIDTH), lambda h, i: (0, ATT_TILES + h)),
                  pl.BlockSpec((rows, ATT_WIDTH), lambda h, i: (0, 2 * ATT_TILES + h))] + [HBM] * ns,
        out_specs=[pl.BlockSpec((bq, ATT_WIDTH), lambda h, i: (i, h)),
                   pl.BlockSpec((hg, bq, 1), lambda h, i: (h, i, 0)),
                   pl.BlockSpec((hg, bq, 1), lambda h, i: (h, i, 0))] + [HBM] * ns,
        out_shape=[jax.ShapeDtypeStruct((rows, GROUP), F32), jax.ShapeDtypeStruct((nh, rows, 1), F32),
                   jax.ShapeDtypeStruct((nh, rows, 1), F32)]
        + [jax.ShapeDtypeStruct((N_CHIPS,) + s.shape, s.dtype) for s in shards],
        scratch_shapes=_gather_sems(ns) if ns else [],
        compiler_params=_params("arbitrary", "arbitrary"),
    )(qkv, qkv, qkv, *shards)


def _sb_bwd(name, qkv, rt, cnt, do, parts=()):
    rows = qkv.shape[0]
    nh, dh = N_HEADS, HEAD
    bq, bk, hg = _row_tile(rows), ATT_BLOCK, ATT_HEADS
    per = bq // bk
    scale = dh ** -0.5
    ns = len(parts)
    grid = (nh // hg, rows // bq)

    def body(q_ref, k_ref, v_ref, rt_ref, cnt_ref, do_ref, *rest):
        dq_ref, dk_ref, dv_ref = rest[ns:ns + 3]
        if ns:
            at_first, before_last, at_last = _exchange_steps(grid)
            start, forward, finish = _reduce_exchange(rest[:ns], rest[ns + 3:2 * ns + 3], rest[2 * ns + 3:3 * ns + 3],
                                             rest[3 * ns + 3:])
            pl.when(at_first)(start)
        i = pl.program_id(1)

        @pl.when(i == 0)
        def _():
            dk_ref[...] = jnp.zeros_like(dk_ref)
            dv_ref[...] = jnp.zeros_like(dv_ref)

        upto = _tri2(bk, lambda r, c: r <= c)
        before = _tri2(bk, lambda r, c: r < c)
        nkb = (i + 1) * per
        first = nkb - jnp.max(cnt_ref[0]).astype(jnp.int32)

        def visit(carry, jb, r0):
            off = pl.multiple_of(jb * bk, bk)
            out = []
            for hh in range(hg):
                keep_sum, g_sum, dq = carry[hh]
                qb, dob = q_ref[r0:, _lanes(hh)], do_ref[r0:, _lanes(hh)]
                kb = k_ref[pl.ds(off, bk), _lanes(hh)]
                vb = v_ref[pl.ds(off, bk), _lanes(hh)]
                z, valid, e, log_keep = _sb_block(qb, kb, i * bq + r0, jb, scale)
                log_rest = rt_ref[hh, r0:, :] - keep_sum[r0:] - _dot_split(log_keep, upto)
                attn = jnp.where(valid, jnp.exp(z + log_keep + log_rest), 0.0)
                g = attn * _dot_nt(dob, vb)
                g_before = g_sum[r0:] + _dot_split(g, before)
                inv = 1.0 / (1.0 + e)
                sig = jnp.where(z >= 0, inv, e * inv)
                dz = (jnp.where(valid, g * (1.0 - sig) - g_before * sig, 0.0) * scale).astype(BF16)
                dk_ref[pl.ds(off, bk), _lanes(hh)] += _dot_tn(dz, qb)
                dv_ref[pl.ds(off, bk), _lanes(hh)] += _dot_tn(attn.astype(BF16), dob)
                new = (keep_sum[r0:] + jnp.sum(log_keep, axis=-1, keepdims=True),
                       g_sum[r0:] + jnp.sum(g, axis=-1, keepdims=True),
                       dq[r0:] + _dot(dz, kb))
                if r0:
                    new = tuple(jnp.concatenate([old[:r0], x], axis=0) for old, x in zip(carry[hh], new))
                out.append(new)
            return tuple(out)

        zero = jnp.zeros((bq, 1), F32)
        res = lax.fori_loop(first, nkb - per, lambda jb, c: visit(c, jb, 0),
                            tuple((zero, zero, jnp.zeros((bq, dh), F32)) for _ in range(hg)))
        for dgl in range(per):
            res = visit(res, i * per + dgl, dgl * bk)
        for hh in range(hg):
            dq_ref[:, _lanes(hh)] = res[hh][2]
        if ns:
            pl.when(before_last)(forward)
            pl.when(at_last)(finish)

    return pl.pallas_call(
        body, name=name, grid=grid,
        in_specs=[pl.BlockSpec((bq, ATT_WIDTH), lambda h, i: (i, h)),
                  pl.BlockSpec((rows, ATT_WIDTH), lambda h, i: (0, ATT_TILES + h)),
                  pl.BlockSpec((rows, ATT_WIDTH), lambda h, i: (0, 2 * ATT_TILES + h)),
                  pl.BlockSpec((hg, bq, 1), lambda h, i: (h, i, 0)),
                  pl.BlockSpec((hg, bq, 1), lambda h, i: (h, i, 0)),
                  pl.BlockSpec((bq, ATT_WIDTH), lambda h, i: (i, h))] + [HBM] * ns,
        out_specs=[pl.BlockSpec((bq, ATT_WIDTH), lambda h, i: (i, h)),
                   pl.BlockSpec((rows, ATT_WIDTH), lambda h, i: (0, h)),
                   pl.BlockSpec((rows, ATT_WIDTH), lambda h, i: (0, h))] + [HBM] * (2 * ns),
        out_shape=[jax.ShapeDtypeStruct((rows, GROUP), F32)] * 3
        + [jax.ShapeDtypeStruct(s.shape, s.dtype) for s in parts] * 2,
        scratch_shapes=_reduce_sems(ns) if ns else [],
        compiler_params=_params("arbitrary", "arbitrary"),
    )(qkv, qkv, qkv, rt, cnt, do, *parts)


def _head_sum(x, ones_bd):
    return _dot_split(x, ones_bd)


def _rwkv_pre(p, p_prev, mu, w0, a0, k_k, k_a, w_up, a_up, g_up, ones_bd):
    xs = p + (p_prev - p) * mu
    r = xs[:, :GROUP]
    k0 = xs[:, GROUP:2 * GROUP]
    v = xs[:, 2 * GROUP:3 * GROUP]
    lo = xs[:, 3 * GROUP:]
    wa = w0 + _dot(jnp.tanh(lo).astype(BF16), w_up.astype(BF16))
    w = -(jnp.maximum(-wa, 0.0) + jnp.log(1.0 + jnp.exp(-jnp.abs(wa)))) - 0.5
    log_decay = -jnp.exp(w)
    alpha = _sigmoid(a0 + _dot(lo.astype(BF16), a_up.astype(BF16)))
    gate = _dot(_sigmoid(lo).astype(BF16), g_up.astype(BF16))
    kk = k0 * k_k
    kk = kk * lax.rsqrt(jnp.maximum(_head_sum(kk * kk, ones_bd), 1e-24))
    k = k0 * (1.0 + (alpha - 1.0) * k_a)
    return r, log_decay, k, v, -kk, kk * alpha, gate


def _rwkv_post(y, r, k, v, gate, lnx_w, lnx_b, r_k, ones_bd):
    mean = _head_sum(y, ones_bd) * (1.0 / HEAD)
    yc = y - mean
    var = _head_sum(yc * yc, ones_bd) * (1.0 / HEAD)
    yn = yc * lax.rsqrt(var + LNX_EPS) * lnx_w + lnx_b
    bonus = _head_sum(r * k * r_k, ones_bd) * v
    return (yn + bonus) * gate


_PRE_VEC = 5
_PRE_MAT = 3


def _heads(x):
    return jnp.stack([x[:, _lanes(h)] for h in range(N_HEADS)])


def _unheads(x):
    return jnp.concatenate([x[h] for h in range(N_HEADS)], axis=1)


def _edge_spec(tm, width, tile_of):
    return pl.BlockSpec((8, width), lambda i: (jnp.maximum(tile_of(i) * (tm // 8) - 1, 0), 0))


def _previous_rows(p_ref, edge_ref, tile):
    p = p_ref[...]
    edge = jnp.where(tile == 0, 0.0, edge_ref[7:8, :])
    row = lax.broadcasted_iota(jnp.int32, (p.shape[0], 1), 0)
    return jnp.where(row == 0, edge, pltpu.roll(p, 1, axis=0))


def _rwkv_pre_fwd(name, p, vecs, mats, ones_bd):
    rows = p.shape[0]
    tm = _row_tile(rows)
    row_spec = lambda w: pl.BlockSpec((tm, w), lambda i: (i, 0))
    full = lambda a: pl.BlockSpec(a.shape, lambda i: (0,) * a.ndim)

    def body(p_ref, edge_ref, *refs):
        ins = [r[...] for r in refs[:_PRE_VEC + _PRE_MAT + 1]]
        outs = refs[_PRE_VEC + _PRE_MAT + 1:]
        prev = _previous_rows(p_ref, edge_ref, pl.program_id(0))
        for o_ref, val in zip(outs, _rwkv_pre(p_ref[...], prev, *ins)):
            o_ref[...] = val

    return pl.pallas_call(
        body, name=name, grid=(rows // tm,),
        in_specs=([row_spec(RW_COLS), _edge_spec(tm, RW_COLS, lambda i: i)]
                  + [full(a) for a in (*vecs, *mats, ones_bd)]),
        out_specs=[row_spec(GROUP)] * 7,
        out_shape=[jax.ShapeDtypeStruct((rows, GROUP), F32)] * 7,
        compiler_params=_params("arbitrary"),
    )(p, p, *vecs, *mats, ones_bd)


def _rwkv_pre_bwd(name, p, vecs, mats, ones_bd, cts_scan, ct_gate, cts_b):
    rows = p.shape[0]
    tm = _row_tile(rows)
    nt = rows // tm
    n_par = _PRE_VEC + _PRE_MAT
    tile_of = lambda i: nt - 1 - i
    row_spec = lambda w: pl.BlockSpec((tm, w), lambda i: (tile_of(i), 0))
    full = lambda a: pl.BlockSpec(a.shape, lambda i: (0,) * a.ndim)

    def body(*refs):
        p_ref, edge_ref = refs[0], refs[1]
        par = [r[...] for r in refs[2:2 + n_par]]
        ones = refs[2 + n_par][...]
        cta = [r[...] for r in refs[3 + n_par:10 + n_par]]
        ctb = [r[...] for r in refs[10 + n_par:13 + n_par]]
        dp_ref, par_outs, carry = refs[13 + n_par], refs[14 + n_par:-1], refs[-1]
        step = pl.program_id(0)

        @pl.when(step == 0)
        def _():
            carry[...] = jnp.zeros_like(carry)
            for o_ref in par_outs:
                o_ref[...] = jnp.zeros_like(o_ref)

        ct = (cta[0] + ctb[0], cta[1], cta[2] + ctb[1], cta[3] + ctb[2], cta[4], cta[5], cta[6])
        _, vjp = jax.vjp(lambda pv, ppv, *pr: _rwkv_pre(pv, ppv, *pr, ones),
                         p_ref[...], _previous_rows(p_ref, edge_ref, tile_of(step)), *par)
        grads = vjp(ct)
        row = lax.broadcasted_iota(jnp.int32, (tm, 1), 0)
        dp_ref[...] = grads[0] + jnp.where(row == tm - 1, carry[0:1, :], pltpu.roll(grads[1], tm - 1, axis=0))
        carry[0:1, :] = grads[1][0:1, :]
        for o_ref, gval in zip(par_outs, grads[2:]):
            o_ref[...] += gval

    par_arrays = (*vecs, *mats)
    return pl.pallas_call(
        body, name=name, grid=(nt,),
        in_specs=([row_spec(RW_COLS), _edge_spec(tm, RW_COLS, tile_of)] + [full(a) for a in (*par_arrays, ones_bd)]
                  + [row_spec(GROUP)] * 10),
        out_specs=[row_spec(RW_COLS)] + [full(a) for a in par_arrays],
        out_shape=[jax.ShapeDtypeStruct((rows, RW_COLS), F32)] + [jax.ShapeDtypeStruct(a.shape, F32) for a in par_arrays],
        scratch_shapes=[pltpu.VMEM((8, RW_COLS), F32)],
        compiler_params=_params("arbitrary"),
    )(p, p, *par_arrays, ones_bd, *cts_scan, ct_gate, *cts_b)


def _rwkv_post_fwd(name, y, r, k, v, gate, vecs, ones_bd):
    rows = r.shape[0]
    tm = _row_tile(rows)
    row_spec = pl.BlockSpec((tm, GROUP), lambda i: (i, 0))
    full = lambda a: pl.BlockSpec(a.shape, lambda i: (0,) * a.ndim)

    def body(*refs):
        refs[-1][...] = _rwkv_post(*(r_[...] for r_ in refs[:-1]))

    return pl.pallas_call(
        body, name=name, grid=(rows // tm,),
        in_specs=[row_spec] * 5 + [full(a) for a in (*vecs, ones_bd)],
        out_specs=row_spec,
        out_shape=jax.ShapeDtypeStruct((rows, GROUP), F32),
        compiler_params=_params("arbitrary"),
    )(y, r, k, v, gate, *vecs, ones_bd)


def _rwkv_post_bwd(name, y, r, k, v, gate, vecs, ones_bd, dout):
    rows = r.shape[0]
    tm = _row_tile(rows)
    row_spec = pl.BlockSpec((tm, GROUP), lambda i: (i, 0))
    full = lambda a: pl.BlockSpec(a.shape, lambda i: (0,) * a.ndim)

    def body(*refs):
        vals = [r_[...] for r_ in refs[:8]]
        ones = refs[8][...]
        dout_v = refs[9][...]
        outs = refs[10:]
        _, vjp = jax.vjp(lambda *a: _rwkv_post(*a, ones), *vals)
        grads = vjp(dout_v)
        for o_ref, gval in zip(outs[:5], grads[:5]):
            o_ref[...] = gval

        @pl.when(pl.program_id(0) == 0)
        def _():
            for o_ref in outs[5:]:
                o_ref[...] = jnp.zeros_like(o_ref)

        for o_ref, gval in zip(outs[5:], grads[5:]):
            o_ref[...] += gval

    return pl.pallas_call(
        body, name=name, grid=(rows // tm,),
        in_specs=[row_spec] * 5 + [full(a) for a in (*vecs, ones_bd)] + [row_spec],
        out_specs=[row_spec] * 5 + [full(a) for a in vecs],
        out_shape=[jax.ShapeDtypeStruct((rows, GROUP), F32)] * 5 + [jax.ShapeDtypeStruct(a.shape, F32) for a in vecs],
        compiler_params=_params("arbitrary"),
    )(y, r, k, v, gate, *vecs, ones_bd, dout)


_NN = (((2,), (1,)), ((0,), (0,)))
_NT = (((2,), (2,)), ((0,), (0,)))
_TN = (((1,), (1,)), ((0,), (0,)))


_BWD_FORMS = {"nn": (("nt", False), ("tn", False)),
              "nt": (("nn", False), ("tn", True)),
              "tn": (("nt", True), ("nn", False))}
_DIMS = {"nn": _NN, "nt": _NT, "tn": _TN}


def _bdot(a, b, form):
    return lax.dot_general(a.astype(BF16), b.astype(BF16), _DIMS[form], preferred_element_type=F32)


@functools.partial(jax.custom_vjp, nondiff_argnums=(2,))
def _bmm(a, b, form):
    return _bdot(a, b, form)


def _bmm_fwd(a, b, form):
    return _bdot(a, b, form), (a.astype(BF16), b.astype(BF16))


def _bmm_bwd(form, res, dc):
    a, b = res
    (fa, swap_a), (fb, swap_b) = _BWD_FORMS[form]
    da = _bdot(b, dc, fa) if swap_a else _bdot(dc, b, fa)
    db = _bdot(dc, a, fb) if swap_b else _bdot(a, dc, fb)
    return da, db


_bmm.defvjp(_bmm_fwd, _bmm_bwd)


@jax.custom_vjp
def _cumsum_steps(x):
    return _tri_apply(x, lambda r, c: r >= c)


def _tri_apply(x, cmp):
    nh, c, _ = x.shape
    tri = cmp(lax.broadcasted_iota(jnp.int32, (c, c), 0), lax.broadcasted_iota(jnp.int32, (c, c), 1))
    tri = jnp.broadcast_to(tri.astype(BF16)[None], (nh, c, c))
    hi, lo = _split2(x)
    return (lax.dot_general(tri, hi, _NN, preferred_element_type=F32)
            + lax.dot_general(tri, lo, _NN, preferred_element_type=F32))


_cumsum_steps.defvjp(lambda x: (_cumsum_steps(x), None), lambda _, d: (_tri_apply(d, lambda r, c: r <= c),))


@jax.custom_vjp
def _neumann(n_mat):
    c = n_mat.shape[1]
    inv, power, span = n_mat, _bmm(n_mat, n_mat, "nn"), 2
    while span < c:
        both = _bmm(jnp.concatenate([power, inv], axis=1), power, "nn")
        inv = inv + power + both[:, c:]
        power = both[:, :c]
        span *= 2
    return inv


def _neumann_fwd(n_mat):
    inv = _neumann(n_mat)
    return inv, inv


def _neumann_bwd(inv, d):
    left = d + _bmm(inv, d, "tn")
    return (left + _bmm(left, inv, "nt"),)


_neumann.defvjp(_neumann_fwd, _neumann_bwd)


def _chunk(state, r, log_w, k, v, a, b):
    nh, c, _ = r.shape
    row = lax.broadcasted_iota(jnp.int32, (c, c), 0)
    col = lax.broadcasted_iota(jnp.int32, (c, c), 1)
    cum = _cumsum_steps(log_w)
    mid = cum[:, c // 2 - 1:c // 2, :]
    a_t = a * jnp.exp(cum - log_w - mid)
    r_t = r * jnp.exp(cum - mid)
    back = jnp.exp(mid - cum)
    b_t = b * back
    k_t = k * back
    strict, incl = (row > col)[None], (row >= col)[None]
    ar = jnp.concatenate([a_t, r_t], axis=1)
    on_b = _bmm(ar, b_t, "nt")
    on_k = _bmm(ar, k_t, "nt")
    n_mat = jnp.where(strict, on_b[:, :c], 0.0)
    p_mat = jnp.where(incl, on_b[:, c:], 0.0)
    m_mat = jnp.where(strict, on_k[:, :c], 0.0)
    q_mat = jnp.where(incl, on_k[:, c:], 0.0)
    inv = _neumann(n_mat)
    s_mid = state * jnp.swapaxes(jnp.exp(mid), 1, 2)
    x = _bmm(jnp.concatenate([a_t, m_mat], axis=2), jnp.concatenate([s_mid, v], axis=1), "nn")
    u = x + _bmm(inv, x, "nn")
    y = _bmm(jnp.concatenate([r_t, p_mat, q_mat], axis=2), jnp.concatenate([s_mid, u, v], axis=1), "nn")
    grown = _bmm(jnp.concatenate([b_t, k_t], axis=1), jnp.concatenate([u, v], axis=1), "tn")
    s_new = (s_mid + grown) * jnp.swapaxes(jnp.exp(cum[:, c - 1:c, :] - mid), 1, 2)
    return y, s_new


def _scan_fwd(name, ops):
    rows = ops[0].shape[0]
    nh, dh = N_HEADS, HEAD
    nc, per = rows // CHUNK, SCAN_CHUNKS
    spec = pl.BlockSpec((per * CHUNK, GROUP), lambda c: (c, 0))

    def body(r_ref, w_ref, k_ref, v_ref, a_ref, b_ref, y_ref, st_ref, state):
        @pl.when(pl.program_id(0) == 0)
        def _():
            state[...] = jnp.zeros_like(state)

        s = state[...]
        for u in range(per):
            at = slice(u * CHUNK, (u + 1) * CHUNK)
            st_ref[u] = s
            y, s = _chunk(s, *(_heads(ref[at, :]) for ref in (r_ref, w_ref, k_ref, v_ref, a_ref, b_ref)))
            y_ref[at, :] = _unheads(y)
        state[...] = s

    return pl.pallas_call(
        body, name=name, grid=(nc // per,),
        in_specs=[spec] * 6,
        out_specs=[spec, pl.BlockSpec((per, nh, dh, dh), lambda c: (c, 0, 0, 0))],
        out_shape=[jax.ShapeDtypeStruct((rows, GROUP), F32), jax.ShapeDtypeStruct((nc, nh, dh, dh), F32)],
        scratch_shapes=[pltpu.VMEM((nh, dh, dh), F32)],
        compiler_params=_params("arbitrary"),
    )(*ops)


def _scan_bwd(name, ops, states, dy):
    rows = ops[0].shape[0]
    nh, dh = N_HEADS, HEAD
    nc, per = rows // CHUNK, SCAN_CHUNKS
    steps = nc // per
    spec = pl.BlockSpec((per * CHUNK, GROUP), lambda c: (steps - 1 - c, 0))

    def body(r_ref, w_ref, k_ref, v_ref, a_ref, b_ref, st_ref, dy_ref, *rest):
        outs, dstate = rest[:6], rest[6]

        @pl.when(pl.program_id(0) == 0)
        def _():
            dstate[...] = jnp.zeros_like(dstate)

        ds = dstate[...]
        for u in reversed(range(per)):
            at = slice(u * CHUNK, (u + 1) * CHUNK)
            _, vjp = jax.vjp(_chunk, st_ref[u],
                             *(_heads(ref[at, :]) for ref in (r_ref, w_ref, k_ref, v_ref, a_ref, b_ref)))
            grads = vjp((_heads(dy_ref[at, :]), ds))
            ds = grads[0]
            for o_ref, gval in zip(outs, grads[1:]):
                o_ref[at, :] = _unheads(gval)
        dstate[...] = ds

    return pl.pallas_call(
        body, name=name, grid=(steps,),
        in_specs=[spec] * 6 + [pl.BlockSpec((per, nh, dh, dh), lambda c: (steps - 1 - c, 0, 0, 0)), spec],
        out_specs=[spec] * 6,
        out_shape=[jax.ShapeDtypeStruct((rows, GROUP), F32)] * 6,
        scratch_shapes=[pltpu.VMEM((nh, dh, dh), F32)],
        compiler_params=_params("arbitrary"),
    )(*ops, states, dy)


def _pad_cols(x, cols):
    return jnp.concatenate([x, jnp.zeros(x.shape[:-1] + (cols - x.shape[-1],), x.dtype)], axis=-1)


def _lora_pad(w_up, a_up, g_up):
    z = lambda n: jnp.zeros((n, GROUP), F32)
    return (jnp.concatenate([w_up, z(LORA_PAD - LORA_W)], 0),
            jnp.concatenate([z(LORA_W), a_up, z(LORA_PAD - LORA_W - LORA_A)], 0),
            jnp.concatenate([z(LORA_W + LORA_A), g_up, z(LORA_PAD - LORA_W - LORA_A - LORA_G)], 0))


MID = ['w_in']
LATE = ['ffn2_w_gate', 'ffn2_w_up', 'ffn2_w_down', 'w_out']


def _local_step(x, tgt, w, late=None):
    d = x.shape[1]
    zeros = jnp.zeros((META_PAD, d), F32)
    h0 = jnp.concatenate([zeros, w["meta_tokens"], x], axis=0)
    tgt_p = jnp.concatenate([jnp.zeros((ROW0, d), F32), tgt], axis=0)
    ones_bd = ((lax.broadcasted_iota(jnp.int32, (2 * GROUP, GROUP), 0) % GROUP) // HEAD
               == lax.broadcasted_iota(jnp.int32, (2 * GROUP, GROUP), 1) // HEAD).astype(BF16)
    pre_vecs = (_pad_cols(w["rwkv_mu"], RW_COLS), w["rwkv_w0"], w["rwkv_a0"], w["rwkv_k_k"], w["rwkv_k_a"])
    pre_mats = _lora_pad(w["rwkv_w_up"], w["rwkv_a_up"], w["rwkv_g_up"])
    post_vecs = (w["rwkv_lnx_w"], w["rwkv_lnx_b"], w["rwkv_r_k"].reshape(1, GROUP))

    h1, a1, b1, *gathered = _ffn_fwd("ffn1_fwd", h0, w["ffn1_norm"], w["ffn1_w_gate"], w["ffn1_w_up"],
                                     w["ffn1_w_down"], late and ("gather", late.shards["mid"]))
    if late is not None:
        w = {**w, **late.join("mid", gathered)}
    w_in = _pad_cols(w["w_in"], IN_COLS_PAD)
    qkv, p, n2 = _norm_proj("in_proj", h1, w["mix_norm"], w_in)
    sb, rest_total, visited, *gathered = _sb_fwd("sb_fwd", qkv, late.shards["late"] if late else ())
    if late is not None:
        w = {**w, **late.join("late", gathered)}
    pre = _rwkv_pre_fwd("rwkv_pre_fwd", p, pre_vecs, pre_mats, ones_bd)
    scan_ops, token_ops = pre[:6], (pre[0], pre[2], pre[3], pre[6])
    y, states = _scan_fwd("rwkv_scan_fwd", scan_ops)
    rw = _rwkv_post_fwd("rwkv_post_fwd", y, *token_ops, post_vecs, ones_bd)
    h2, mix = _out_proj("out_proj", h1, sb, rw, w["w_out"])
    h3, a2, b2 = _ffn_fwd("ffn2_fwd", h2, w["ffn2_norm"], w["ffn2_w_gate"], w["ffn2_w_up"], w["ffn2_w_down"])
    loss8, dh3, g_final = _loss_head("loss_head", h3, w["final_norm"].reshape(1, d), tgt_p)

    g = {"final_norm": g_final.reshape(d)}
    dh2, da2, db2, s2, n3, dhh3, g["ffn2_norm"] = _ffn_bwd(
        "ffn2_bwd", dh3, h2, w["ffn2_norm"], a2, b2, w["ffn2_w_gate"], w["ffn2_w_up"], w["ffn2_w_down"])
    g["ffn2_w_gate"] = _mm_tn("ffn2_dgate", da2, n3)
    g["ffn2_w_up"] = _mm_tn("ffn2_dup", db2, n3)
    g["ffn2_w_down"] = _mm_tn("ffn2_ddown", s2, dhh3)
    dsb, drw, dh2b = _out_proj_bwd("out_proj_bwd", dh2, w["w_out"])
    g["w_out"] = _mm_tn("out_proj_dw", mix, dh2b)
    dq, dk, dv, *reduced_late = _sb_bwd("sb_bwd", qkv, rest_total, visited, dsb, late.parts("late", g) if late else ())
    post_g = _rwkv_post_bwd("rwkv_post_bwd", y, *token_ops, post_vecs, ones_bd, drw)
    g["rwkv_lnx_w"], g["rwkv_lnx_b"] = post_g[5], post_g[6]
    g["rwkv_r_k"] = post_g[7].reshape(1, N_HEADS, HEAD)
    scan_g = _scan_bwd("rwkv_scan_bwd", scan_ops, states, post_g[0])
    pre_g = _rwkv_pre_bwd("rwkv_pre_bwd", p, pre_vecs, pre_mats, ones_bd, scan_g, post_g[4], post_g[1:4])
    dp = pre_g[0]
    g["rwkv_mu"] = pre_g[1][:, :w["rwkv_mu"].shape[1]]
    g["rwkv_w0"], g["rwkv_a0"], g["rwkv_k_k"], g["rwkv_k_a"] = pre_g[2:6]
    g["rwkv_w_up"] = pre_g[6][:LORA_W]
    g["rwkv_a_up"] = pre_g[7][LORA_W:LORA_W + LORA_A]
    g["rwkv_g_up"] = pre_g[8][LORA_W + LORA_A:LORA_W + LORA_A + LORA_G]
    live = (jnp.arange(h0.shape[0]) >= META_PAD)[:, None]
    dproj = jnp.where(live, jnp.concatenate([dq, dk, dv, dp], axis=1), 0.0).astype(BF16)
    g["w_in"] = _mm_tn("in_proj_dw", n2, dproj)[:, :w["w_in"].shape[1]]
    dh1, g["mix_norm"] = _norm_proj_bwd("in_proj_bwd", dproj, w_in, h1, w["mix_norm"], dh2)
    dh0, da1, db1, s1, n1, dhh1, g["ffn1_norm"], *reduced_mid = _ffn_bwd(
        "ffn1_bwd", dh1, h0, w["ffn1_norm"], a1, b1, w["ffn1_w_gate"], w["ffn1_w_up"], w["ffn1_w_down"],
        late and ("reduce", late.parts("mid", g)))
    g["meta_tokens"] = dh0[META_PAD:ROW0]
    reduced = {"mid": reduced_mid, "late": reduced_late}
    if late is None:
        g["ffn1_w_gate"] = _mm_tn("ffn1_dgate", da1, n1)
        g["ffn1_w_up"] = _mm_tn("ffn1_dup", db1, n1)
        g["ffn1_w_down"] = _mm_tn("ffn1_ddown", s1, dhh1)
    else:
        g["ffn1_w_gate"], reduced["small"] = _mm_tn("ffn1_dgate", da1, n1, ("all_reduce", [late.small(g, loss8[0, 0])]))
        g["ffn1_w_up"], *reduced["gate"] = _mm_tn("ffn1_dup", db1, n1, ("reduce", late.parts("gate", g)))
        g["ffn1_w_down"], *reduced["up"] = _mm_tn("ffn1_ddown", s1, dhh1, ("reduce", late.parts("up", g)))
    return loss8[0, 0], dh0[ROW0:], g, reduced


N_CHIPS = 4
N_DEV = 8
HBM = pl.BlockSpec(memory_space=pltpu.HBM)


def _place():
    return lax.axis_index("x"), lax.axis_index("y"), lax.axis_index("c")


def _other_chips(x, y):
    return [(1 - x, y), (x, 1 - y), (1 - x, 1 - y)]


def _gather_sems(n):
    return [pltpu.SemaphoreType.DMA((3 * n,)), pltpu.SemaphoreType.DMA((3 * n,)), pltpu.SemaphoreType.DMA((n,)),
            pltpu.SemaphoreType.DMA((3 * n,)), pltpu.SemaphoreType.DMA((3 * n,))]


def _gather_exchange(ins, outs, sems):
    n = len(ins)
    half = [r.shape[0] // 2 for r in ins]
    send, recv, local, d2d_send, d2d_recv = sems
    x, y, c = _place()
    me = 2 * x + y
    chips = _other_chips(x, y)

    def rows_of(k, h):
        return pl.ds(pl.multiple_of(h * half[k], 8), half[k])

    def own(k):
        return pltpu.make_async_copy(ins[k], outs[k].at[me], local.at[k])

    def copy(j, k, slot):
        return pltpu.make_async_remote_copy(
            src_ref=ins[k].at[rows_of(k, c)], dst_ref=outs[k].at[slot, rows_of(k, c)],
            send_sem=send.at[j * n + k], recv_sem=recv.at[j * n + k],
            device_id=(chips[j][0], chips[j][1], c), device_id_type=MESH)

    def passed(j, k, h):
        slot = 2 * chips[j][0] + chips[j][1]
        return pltpu.make_async_remote_copy(
            src_ref=outs[k].at[slot, rows_of(k, h)], dst_ref=outs[k].at[slot, rows_of(k, h)],
            send_sem=d2d_send.at[j * n + k], recv_sem=d2d_recv.at[j * n + k],
            device_id=(x, y, 1 - c), device_id_type=MESH)

    def start():
        for k in range(n):
            own(k).start()
        for j in range(3):
            for k in range(n):
                copy(j, k, me).start()

    def forward():
        for j in range(3):
            for k in range(n):
                copy(j, k, 2 * chips[j][0] + chips[j][1]).wait_recv()
                passed(j, k, c).start()

    def finish():
        for j in range(3):
            for k in range(n):
                passed(j, k, 1 - c).wait_recv()
        for j in range(3):
            for k in range(n):
                copy(j, k, me).wait_send()
                passed(j, k, c).wait_send()
        for k in range(n):
            own(k).wait()

    return start, forward, finish


def _gather_shards(name, shards):
    n = len(shards)

    def body(*refs):
        for stage in _gather_exchange(refs[:n], refs[n:2 * n], refs[2 * n:]):
            stage()

    return pl.pallas_call(
        body, name=name,
        in_specs=[HBM] * n, out_specs=[HBM] * n,
        out_shape=[jax.ShapeDtypeStruct((N_CHIPS,) + s.shape, s.dtype) for s in shards],
        scratch_shapes=_gather_sems(n),
    )(*shards)


def _pair_exchange(name, parts):
    n = len(parts)
    half = [s.shape[1] // 2 for s in parts]

    def body(*refs):
        ins, outs = refs[:n], refs[n:2 * n]
        send, recv = refs[2 * n:]
        x, y, c = _place()

        def copy(k):
            rows = pl.ds(pl.multiple_of((1 - c) * half[k], 8), half[k])
            return pltpu.make_async_remote_copy(
                src_ref=ins[k].at[:, rows], dst_ref=outs[k], send_sem=send.at[k], recv_sem=recv.at[k],
                device_id=(x, y, 1 - c), device_id_type=MESH)

        for k in range(n):
            copy(k).start()
        for k in range(n):
            copy(k).wait_recv()
        for k in range(n):
            copy(k).wait_send()

    return pl.pallas_call(
        body, name=name,
        in_specs=[HBM] * n, out_specs=[HBM] * n,
        out_shape=[jax.ShapeDtypeStruct((s.shape[0], s.shape[1] // 2, s.shape[2]), s.dtype) for s in parts],
        scratch_shapes=[pltpu.SemaphoreType.DMA((n,)), pltpu.SemaphoreType.DMA((n,))],
    )(*parts)


def _pair_add(name, part, other):
    nch, rows, cols = part.shape
    half = rows // 2

    def body(p_ref, o_ref, out_ref):
        c = lax.axis_index("c")
        mine = p_ref[0, pl.ds(pl.multiple_of(c * half, 16), half), :]
        out_ref[0] = (mine.astype(F32) + o_ref[0].astype(F32)).astype(out_ref.dtype)

    return pl.pallas_call(
        body, name=name, grid=(nch,),
        in_specs=[pl.BlockSpec((1, rows, cols), lambda j: (j, 0, 0)),
                  pl.BlockSpec((1, half, cols), lambda j: (j, 0, 0))],
        out_specs=pl.BlockSpec((1, half, cols), lambda j: (j, 0, 0)),
        out_shape=jax.ShapeDtypeStruct((nch, half, cols), part.dtype),
        compiler_params=_params("arbitrary"),
    )(part, other)


def _reduce_sems(n):
    return [pltpu.SemaphoreType.DMA((3 * n,)), pltpu.SemaphoreType.DMA((3 * n,)), pltpu.SemaphoreType.DMA((n,)),
            pltpu.SemaphoreType.DMA((n,)), pltpu.SemaphoreType.DMA((n,))]


def _reduce_exchange(ins, got, sib, sems):
    n = len(ins)
    send, recv, local, d2d_send, d2d_recv = sems
    x, y, c = _place()
    me = 2 * x + y
    chips = _other_chips(x, y)

    def own(k):
        return pltpu.make_async_copy(ins[k].at[me], got[k].at[me], local.at[k])

    def copy(j, k, shard, slot):
        return pltpu.make_async_remote_copy(
            src_ref=ins[k].at[shard], dst_ref=got[k].at[slot], send_sem=send.at[j * n + k],
            recv_sem=recv.at[j * n + k], device_id=(chips[j][0], chips[j][1], c), device_id_type=MESH)

    def swap(k):
        return pltpu.make_async_remote_copy(
            src_ref=got[k], dst_ref=sib[k], send_sem=d2d_send.at[k], recv_sem=d2d_recv.at[k],
            device_id=(x, y, 1 - c), device_id_type=MESH)

    def start():
        for k in range(n):
            own(k).start()
        for j in range(3):
            for k in range(n):
                copy(j, k, 2 * chips[j][0] + chips[j][1], me).start()

    def forward():
        for k in range(n):
            own(k).wait()
            for j in range(3):
                copy(j, k, me, 2 * chips[j][0] + chips[j][1]).wait_recv()
            swap(k).start()

    def finish():
        for k in range(n):
            swap(k).wait_recv()
        for j in range(3):
            for k in range(n):
                copy(j, k, me, me).wait_send()
        for k in range(n):
            swap(k).wait_send()

    return start, forward, finish


def _reduce_shards(name, parts):
    n = len(parts)

    def body(*refs):
        for stage in _reduce_exchange(refs[:n], refs[n:2 * n], refs[2 * n:3 * n], refs[3 * n:]):
            stage()

    return pl.pallas_call(
        body, name=name,
        in_specs=[HBM] * n, out_specs=[HBM] * (2 * n),
        out_shape=[jax.ShapeDtypeStruct(s.shape, s.dtype) for s in parts] * 2,
        scratch_shapes=_reduce_sems(n),
    )(*parts)


def _all_reduce_scratch(vec):
    return [pltpu.VMEM((N_DEV,) + vec.shape, F32),
            pltpu.SemaphoreType.DMA((N_DEV - 1,)), pltpu.SemaphoreType.DMA((N_DEV - 1,))]


def _all_reduce_exchange(v_ref, o_ref, buf, send, recv):
    x, y, c = _place()
    me = 4 * x + 2 * y + c
    peers = [(x ^ (r >> 2), y ^ ((r >> 1) & 1), c ^ (r & 1)) for r in range(1, N_DEV)]

    def copy(r, slot):
        px, py, pc = peers[r]
        return pltpu.make_async_remote_copy(
            src_ref=v_ref, dst_ref=buf.at[slot], send_sem=send.at[r], recv_sem=recv.at[r],
            device_id=(px, py, pc), device_id_type=MESH)

    def start():
        for r in range(N_DEV - 1):
            copy(r, me).start()
        buf[me] = v_ref[...]

    def finish():
        for r in range(N_DEV - 1):
            px, py, pc = peers[r]
            copy(r, 4 * px + 2 * py + pc).wait_recv()
        total = buf[0]
        for dev in range(1, N_DEV):
            total = total + buf[dev]
        o_ref[...] = total
        for r in range(N_DEV - 1):
            copy(r, me).wait_send()

    return start, lambda: None, finish


def _adamw(w, g, m, v):
    m = ADAM_B1 * m + (1.0 - ADAM_B1) * g
    v = ADAM_B2 * v + (1.0 - ADAM_B2) * (g * g)
    m_hat = m / (1.0 - ADAM_B1 ** ADAM_STEP)
    v_hat = v / (1.0 - ADAM_B2 ** ADAM_STEP)
    return -ADAM_LR * (m_hat / (jnp.sqrt(v_hat) + ADAM_EPS) + ADAM_WD * w), m, v


def _adamw_shard(name, core, w, m, v, got, sib):
    rows, cols = w.shape
    tr = rows // 4
    spec = pl.BlockSpec((tr, cols), lambda i, c_ref: (i, 0))
    got_spec = pl.BlockSpec((N_CHIPS, tr, cols), lambda i, c_ref: (0, jnp.where(i // 2 == c_ref[0], i % 2, 0), 0))
    sib_spec = pl.BlockSpec((N_CHIPS, tr, cols), lambda i, c_ref: (0, jnp.where(i // 2 == c_ref[0], 0, i % 2), 0))

    def body(c_ref, w_ref, m_ref, v_ref, got_ref, sib_ref, g_ref, d_ref, mo_ref, vo_ref):
        def four(ref):
            return ((ref[0].astype(F32) + ref[1].astype(F32)) + ref[2].astype(F32)) + ref[3].astype(F32)

        g = jnp.where(pl.program_id(0) // 2 == c_ref[0], four(got_ref), four(sib_ref))
        g_ref[...] = g
        d_ref[...], mo_ref[...], vo_ref[...] = _adamw(w_ref[...], g, m_ref[...], v_ref[...])

    return pl.pallas_call(
        body, name=name,
        grid_spec=pltpu.PrefetchScalarGridSpec(
            num_scalar_prefetch=1, grid=(4,),
            in_specs=[spec, spec, spec, got_spec, sib_spec], out_specs=[spec] * 4),
        out_shape=[jax.ShapeDtypeStruct((rows, cols), F32)] * 4,
        compiler_params=_params("arbitrary"),
    )(core, w, m, v, got, sib)


def _adamw_small(name, w, m, v, g):
    def body(w_ref, m_ref, v_ref, g_ref, d_ref, mo_ref, vo_ref):
        d_ref[...], mo_ref[...], vo_ref[...] = _adamw(w_ref[...], g_ref[...], m_ref[...], v_ref[...])

    return pl.pallas_call(body, name=name, out_shape=[jax.ShapeDtypeStruct(w.shape, F32)] * 3)(w, m, v, g)


def _cast_bf16(name, arrays):
    n = len(arrays)

    def body(*refs):
        for i_ref, o_ref in zip(refs[:n], refs[n:]):
            o_ref[...] = i_ref[...].astype(BF16)

    return pl.pallas_call(
        body, name=name, out_shape=[jax.ShapeDtypeStruct(a.shape, BF16) for a in arrays],
        compiler_params=pltpu.CompilerParams(vmem_limit_bytes=VMEM_LIMIT),
    )(*arrays)


def _pack(arrays, rows):
    flat = jnp.concatenate([a.reshape(-1) for a in arrays])
    return jnp.concatenate([flat, jnp.zeros((rows * 128 - flat.shape[0],), F32)]).reshape(rows, 128)


def _unpack(packed, shapes):
    flat, out, at = packed.reshape(-1), [], 0
    for s in shapes:
        size = 1
        for dim in s:
            size *= dim
        out.append(flat[at:at + size].reshape(s))
        at += size
    return out


def _rows_for(shapes):
    total = 0
    for s in shapes:
        size = 1
        for dim in s:
            size *= dim
        total += size
    return -(-total // 1024) * 8


WEIGHTS = ['meta_tokens', 'ffn1_norm', 'ffn1_w_gate', 'ffn1_w_up', 'ffn1_w_down', 'mix_norm', 'w_in', 'rwkv_mu',
           'rwkv_w0', 'rwkv_w_up', 'rwkv_a0', 'rwkv_a_up', 'rwkv_g_up', 'rwkv_k_k', 'rwkv_k_a', 'rwkv_r_k',
           'rwkv_lnx_w', 'rwkv_lnx_b', 'w_out', 'ffn2_norm', 'ffn2_w_gate', 'ffn2_w_up', 'ffn2_w_down', 'final_norm']
COL_CUT = ['ffn1_w_gate', 'ffn1_w_up', 'w_in', 'ffn2_w_gate', 'ffn2_w_up']
ROW_CUT = ['ffn1_w_down', 'w_out', 'ffn2_w_down']
SMALL_CUT = ['meta_tokens', 'rwkv_w_up', 'rwkv_a_up', 'rwkv_g_up']
TRANSPOSED = ['ffn1_w_gate', 'ffn1_w_up', 'ffn2_w_gate', 'ffn2_w_up']
BIG = COL_CUT + ROW_CUT
REPLICATED = [n for n in WEIGHTS if n not in BIG + SMALL_CUT]


def _join_cols(a):
    return a.transpose(1, 0, 2).reshape(a.shape[1], N_CHIPS * a.shape[2])


def _cut_cols(a):
    return a.reshape(a.shape[0], N_CHIPS, a.shape[1] // N_CHIPS).transpose(1, 0, 2)


def _step(x, loss_target, w, m, v):
    two = lambda a: a.reshape(a.shape[-2], a.shape[-1])

    def rows_cut(n, a):
        return jnp.swapaxes(two(a), 0, 1) if n in TRANSPOSED else two(a)

    def as_given(n, a, like):
        return (jnp.swapaxes(a, 0, 1) if n in TRANSPOSED else a).reshape(like.shape)

    col_cut = [n for n in COL_CUT + SMALL_CUT if n not in TRANSPOSED]

    def join(names, gathered):
        return {n: (_join_cols(a) if n in col_cut else a.reshape(-1, a.shape[-1])) for n, a in zip(names, gathered)}

    def pair_sums(tag, names, g):
        parts = [_cut_cols(g[n]) if n in col_cut else g[n].reshape(N_CHIPS, -1, g[n].shape[-1]) for n in names]
        arrived = _pair_exchange("pair_exchange_" + tag, parts)
        return [_pair_add("pair_add_" + n, p, o) for n, p, o in zip(names, parts, arrived)]

    first = [n for n in BIG if n not in MID + LATE]
    gathered_later = {"mid": MID, "late": LATE}
    groups = {**gathered_later, "gate": ["ffn1_w_gate"], "up": ["ffn1_w_up"]}
    cast = dict(zip(BIG, _cast_bf16("cast_weights", [rows_cut(n, w[n]) for n in BIG])))
    names = first + SMALL_CUT
    shards = [cast[n] for n in first] + [two(w[n]) for n in SMALL_CUT]
    full = {n: (two(w[n]) if w[n].ndim == 3 else w[n]) for n in REPLICATED}
    full.update(join(names, _gather_shards("gather_weights", shards)))
    full["rwkv_r_k"] = w["rwkv_r_k"]
    full["final_norm"] = w["final_norm"]

    small_names = REPLICATED + SMALL_CUT

    def small(g, loss):
        arrays = [loss.reshape(1)] + [g[n] for n in small_names]
        return _pack(arrays, _rows_for([a.shape for a in arrays]))

    late = types.SimpleNamespace(shards={k: [cast[n] for n in names] for k, names in gathered_later.items()},
                                 join=lambda k, gathered: join(groups[k], gathered),
                                 parts=lambda k, g: pair_sums(k, groups[k], g), small=small)

    _, dx, g, reduced = _local_step(x[0], loss_target[0], full, late)

    groups["down"] = ["ffn1_w_down"]
    reduced["down"] = list(_reduce_shards("reduce_gradients", pair_sums("down", groups["down"], g)))
    got, sib = {}, {}
    for k, names in groups.items():
        got.update(zip(names, reduced[k][:len(names)]))
        sib.update(zip(names, reduced[k][len(names):]))
    loss, *summed = _unpack(reduced["small"], [(1,)] + [g[n].shape for n in small_names])
    loss = loss.reshape(())
    g_small = dict(zip(small_names, summed))
    chip = 2 * lax.axis_index("x") + lax.axis_index("y")
    for n in SMALL_CUT:
        width = g_small[n].shape[1] // N_CHIPS
        g_small[n] = lax.dynamic_slice_in_dim(g_small[n], chip * width, width, axis=1)

    grad, delta, new_m, new_v = {}, {}, {}, {}
    core = lax.axis_index("c").astype(jnp.int32).reshape(1)
    for n in BIG:
        outs = _adamw_shard("adamw_" + n, core, rows_cut(n, w[n]), rows_cut(n, m[n]), rows_cut(n, v[n]), got[n], sib[n])
        grad[n], delta[n], new_m[n], new_v[n] = (as_given(n, o, w[n]) for o in outs)
    shapes = [w[n].shape for n in small_names]
    rows = _rows_for(shapes)
    packed = [_pack([t[n] for n in small_names], rows) for t in (w, m, v)]
    g_packed = _pack([g_small[n] for n in small_names], rows)
    outs = [_unpack(o, shapes) for o in _adamw_small("adamw_small", *packed, g_packed)]
    for i, n in enumerate(small_names):
        grad[n] = g_small[n].reshape(w[n].shape)
        delta[n], new_m[n], new_v[n] = outs[0][i], outs[1][i], outs[2][i]
    return loss, dx[None], grad, delta, new_m, new_v


def kernel(x, meta_tokens, ffn1_norm, ffn1_w_gate, ffn1_w_up, ffn1_w_down, mix_norm, w_in, rwkv_mu, rwkv_w0, rwkv_w_up, rwkv_a0, rwkv_a_up, rwkv_g_up, rwkv_k_k, rwkv_k_a, rwkv_r_k, rwkv_lnx_w, rwkv_lnx_b, w_out, ffn2_norm, ffn2_w_gate, ffn2_w_up, ffn2_w_down, final_norm, loss_target, m_meta_tokens, m_ffn1_norm, m_ffn1_w_gate, m_ffn1_w_up, m_ffn1_w_down, m_mix_norm, m_w_in, m_rwkv_mu, m_rwkv_w0, m_rwkv_w_up, m_rwkv_a0, m_rwkv_a_up, m_rwkv_g_up, m_rwkv_k_k, m_rwkv_k_a, m_rwkv_r_k, m_rwkv_lnx_w, m_rwkv_lnx_b, m_w_out, m_ffn2_norm, m_ffn2_w_gate, m_ffn2_w_up, m_ffn2_w_down, m_final_norm, v_meta_tokens, v_ffn1_norm, v_ffn1_w_gate, v_ffn1_w_up, v_ffn1_w_down, v_mix_norm, v_w_in, v_rwkv_mu, v_rwkv_w0, v_rwkv_w_up, v_rwkv_a0, v_rwkv_a_up, v_rwkv_g_up, v_rwkv_k_k, v_rwkv_k_a, v_rwkv_r_k, v_rwkv_lnx_w, v_rwkv_lnx_b, v_w_out, v_ffn2_norm, v_ffn2_w_gate, v_ffn2_w_up, v_ffn2_w_down, v_final_norm):
    w = dict(zip(WEIGHTS, (meta_tokens, ffn1_norm, ffn1_w_gate, ffn1_w_up, ffn1_w_down, mix_norm, w_in, rwkv_mu, rwkv_w0, rwkv_w_up, rwkv_a0, rwkv_a_up, rwkv_g_up, rwkv_k_k, rwkv_k_a, rwkv_r_k, rwkv_lnx_w, rwkv_lnx_b, w_out, ffn2_norm, ffn2_w_gate, ffn2_w_up, ffn2_w_down, final_norm)))
    m = dict(zip(WEIGHTS, (m_meta_tokens, m_ffn1_norm, m_ffn1_w_gate, m_ffn1_w_up, m_ffn1_w_down, m_mix_norm, m_w_in, m_rwkv_mu, m_rwkv_w0, m_rwkv_w_up, m_rwkv_a0, m_rwkv_a_up, m_rwkv_g_up, m_rwkv_k_k, m_rwkv_k_a, m_rwkv_r_k, m_rwkv_lnx_w, m_rwkv_lnx_b, m_w_out, m_ffn2_norm, m_ffn2_w_gate, m_ffn2_w_up, m_ffn2_w_down, m_final_norm)))
    v = dict(zip(WEIGHTS, (v_meta_tokens, v_ffn1_norm, v_ffn1_w_gate, v_ffn1_w_up, v_ffn1_w_down, v_mix_norm, v_w_in, v_rwkv_mu, v_rwkv_w0, v_rwkv_w_up, v_rwkv_a0, v_rwkv_a_up, v_rwkv_g_up, v_rwkv_k_k, v_rwkv_k_a, v_rwkv_r_k, v_rwkv_lnx_w, v_rwkv_lnx_b, v_w_out, v_ffn2_norm, v_ffn2_w_gate, v_ffn2_w_up, v_ffn2_w_down, v_final_norm)))
    loss, grad_x, grad, delta, new_m, new_v = _step(x, loss_target, w, m, v)
    return (loss, grad_x, *[grad[n] for n in WEIGHTS], *[delta[n] for n in WEIGHTS],
            *[new_m[n] for n in WEIGHTS], *[new_v[n] for n in WEIGHTS])
```

```python
import functools
import types

import jax
import jax.numpy as jnp
from jax import lax
from jax.experimental import pallas as pl
from jax.experimental.pallas import tpu as pltpu

F32 = jnp.float32
BF16 = jnp.bfloat16

RMS_EPS = 1e-6
LNX_EPS = 64e-5
N_META = 16
ROW0 = 128
META_PAD = ROW0 - N_META
HEAD = 64
N_HEADS = 8
GROUP = N_HEADS * HEAD
LORA_W, LORA_A, LORA_G = 32, 32, 96
LORA_PAD = 256
RW_COLS = 3 * GROUP + LORA_PAD
IN_COLS_PAD = 3 * GROUP + RW_COLS
ATT_BLOCK = 128
CHUNK = 64
SCAN_CHUNKS = 2
VMEM_LIMIT = 56 * 1024 * 1024

ADAM_LR, ADAM_B1, ADAM_B2, ADAM_EPS, ADAM_WD, ADAM_STEP = 0.001, 0.9, 0.999, 1e-08, 0.01, 10

MESH = pl.DeviceIdType.MESH


def _params(*sem):
    return pltpu.CompilerParams(dimension_semantics=tuple(sem), vmem_limit_bytes=VMEM_LIMIT)


def _dot(a, b):
    return lax.dot_general(a, b, (((1,), (0,)), ((), ())), preferred_element_type=F32)


def _dot_nt(a, b):
    return lax.dot_general(a, b, (((1,), (1,)), ((), ())), preferred_element_type=F32)


def _dot_tn(a, b):
    return lax.dot_general(a, b, (((0,), (0,)), ((), ())), preferred_element_type=F32)


def _split2(x):
    hi = x.astype(BF16)
    return hi, (x - hi.astype(F32)).astype(BF16)


def _sigmoid(x):
    return 1.0 / (1.0 + jnp.exp(-x))


def _rms_fwd(x, g):
    rstd = lax.rsqrt(jnp.mean(x * x, axis=-1, keepdims=True) + RMS_EPS)
    xhat = x * rstd
    return xhat * g, xhat, rstd


def _rms_bwd(dn, xhat, rstd, g):
    dxhat = dn * g
    dx = rstd * (dxhat - xhat * jnp.mean(dxhat * xhat, axis=-1, keepdims=True))
    return dx, jnp.sum(dn * xhat, axis=0, keepdims=True)


def _row_tile(rows):
    return 384 if rows % 384 == 0 else 128


def _half_tile(cols):
    return cols // 2 if cols % 256 == 0 else cols


def _tall_tile(rows, parts):
    return rows // parts if rows % (16 * parts) == 0 else _row_tile(rows)


def _call_with_exchange(name, body, grid, in_specs, out_specs, out_shape, scratch, operands, params, exchange):
    if exchange is None or not exchange[1]:
        return pl.pallas_call(body, name=name, grid=grid, in_specs=in_specs, out_specs=out_specs,
                              out_shape=out_shape, scratch_shapes=scratch, compiler_params=params)(*operands)
    kind, arrays = exchange
    ns, n_in, n_out, n_scr = len(arrays), len(in_specs), len(out_specs), len(scratch)
    whole = lambda a: pl.BlockSpec(a.shape, lambda *_: (0,) * a.ndim)
    if kind == "gather":
        results = [jax.ShapeDtypeStruct((N_CHIPS,) + s.shape, s.dtype) for s in arrays]
        sems, sent_specs, landed_specs = _gather_sems(ns), [HBM] * ns, [HBM] * ns
    elif kind == "reduce":
        results = [jax.ShapeDtypeStruct(s.shape, s.dtype) for s in arrays] * 2
        sems, sent_specs, landed_specs = _reduce_sems(ns), [HBM] * ns, [HBM] * (2 * ns)
    else:
        results = [jax.ShapeDtypeStruct(arrays[0].shape, F32)]
        sems, sent_specs, landed_specs = _all_reduce_scratch(arrays[0]), [whole(arrays[0])], [whole(arrays[0])]
    n_res = len(results)

    def carried(*refs):
        at = n_in + ns + n_out
        sent, landed = refs[n_in:n_in + ns], refs[at:at + n_res]
        own_scratch, sem_refs = refs[at + n_res:at + n_res + n_scr], refs[at + n_res + n_scr:]
        first, before_last, last = _exchange_steps(grid)
        if kind == "gather":
            start, forward, finish = _gather_exchange(sent, landed, sem_refs)
        elif kind == "reduce":
            start, forward, finish = _reduce_exchange(sent, landed[:ns], landed[ns:], sem_refs)
        else:
            start, forward, finish = _all_reduce_exchange(sent[0], landed[0], *sem_refs)
        pl.when(first)(start)
        body(*refs[:n_in], *refs[n_in + ns:at], *own_scratch)
        pl.when(before_last)(forward)
        pl.when(last)(finish)

    return pl.pallas_call(
        carried, name=name, grid=grid, in_specs=list(in_specs) + sent_specs, out_specs=list(out_specs) + landed_specs,
        out_shape=list(out_shape) + results, scratch_shapes=list(scratch) + sems, compiler_params=params,
    )(*operands, *arrays)


def _ffn_fwd(name, h, g, wg, wu, wd, exchange=None):
    rows, d = h.shape
    f = wg.shape[0]
    tm, tf = _row_tile(rows), _half_tile(f)
    nj = f // tf

    def body(h_ref, g_ref, wg_ref, wu_ref, wd_ref, ho_ref, a_ref, b_ref, n_sc, acc_sc):
        j = pl.program_id(1)

        @pl.when(j == 0)
        def _():
            n, _, _ = _rms_fwd(h_ref[...], g_ref[...])
            n_sc[...] = n.astype(BF16)
            acc_sc[...] = jnp.zeros_like(acc_sc)

        n = n_sc[...]
        a = _dot_nt(n, wg_ref[...])
        b = _dot_nt(n, wu_ref[...])
        a_ref[...] = a
        b_ref[...] = b
        s = a * _sigmoid(a) * b
        acc_sc[...] += _dot(s.astype(BF16), wd_ref[...])

        @pl.when(j == nj - 1)
        def _():
            ho_ref[...] = h_ref[...] + 0.5 * acc_sc[...]

    return _call_with_exchange(
        name, body, (rows // tm, nj),
        [pl.BlockSpec((tm, d), lambda i, j: (i, 0)),
         pl.BlockSpec((1, d), lambda i, j: (0, 0)),
         pl.BlockSpec((tf, d), lambda i, j: (j, 0)),
         pl.BlockSpec((tf, d), lambda i, j: (j, 0)),
         pl.BlockSpec((tf, d), lambda i, j: (j, 0))],
        [pl.BlockSpec((tm, d), lambda i, j: (i, 0)),
         pl.BlockSpec((tm, tf), lambda i, j: (i, j)),
         pl.BlockSpec((tm, tf), lambda i, j: (i, j))],
        [jax.ShapeDtypeStruct((rows, d), F32),
         jax.ShapeDtypeStruct((rows, f), F32),
         jax.ShapeDtypeStruct((rows, f), F32)],
        [pltpu.VMEM((tm, d), BF16), pltpu.VMEM((tm, d), F32)],
        (h, g, wg, wu, wd), _params("arbitrary", "arbitrary"), exchange)


def _ffn_bwd(name, dh, h, g, a, b, wg, wu, wd, exchange=None):
    rows, d = h.shape
    f = wg.shape[0]
    tm, tf = _row_tile(rows), _half_tile(f)
    ni, nj = rows // tm, f // tf

    def body(dh_ref, h_ref, g_ref, a_ref, b_ref, wg_ref, wu_ref, wd_ref,
             dhi_ref, da_ref, db_ref, s_ref, n_ref, dhh_ref, dg_ref, dn_sc):
        i, j = pl.program_id(0), pl.program_id(1)

        @pl.when(j == 0)
        def _():
            n, _, _ = _rms_fwd(h_ref[...], g_ref[...])
            n_ref[...] = n.astype(BF16)
            dhh_ref[...] = (0.5 * dh_ref[...]).astype(BF16)
            dn_sc[...] = jnp.zeros_like(dn_sc)

        @pl.when((i == 0) & (j == 0))
        def _():
            dg_ref[...] = jnp.zeros_like(dg_ref)

        ds = _dot_nt(dhh_ref[...], wd_ref[...])
        av, bv = a_ref[...], b_ref[...]
        sig = _sigmoid(av)
        silu = av * sig
        s_ref[...] = (silu * bv).astype(BF16)
        db = (ds * silu).astype(BF16)
        da = (ds * bv * (sig * (1.0 + av * (1.0 - sig)))).astype(BF16)
        da_ref[...] = da
        db_ref[...] = db
        dn_sc[...] += _dot(da, wg_ref[...]) + _dot(db, wu_ref[...])

        @pl.when(j == nj - 1)
        def _():
            gv = g_ref[...]
            _, xhat, rstd = _rms_fwd(h_ref[...], gv)
            dx, dg = _rms_bwd(dn_sc[...], xhat, rstd, gv)
            dhi_ref[...] = dh_ref[...] + dx
            dg_ref[...] += dg

    return _call_with_exchange(
        name, body, (ni, nj),
        [pl.BlockSpec((tm, d), lambda i, j: (i, 0)),
         pl.BlockSpec((tm, d), lambda i, j: (i, 0)),
         pl.BlockSpec((1, d), lambda i, j: (0, 0)),
         pl.BlockSpec((tm, tf), lambda i, j: (i, j)),
         pl.BlockSpec((tm, tf), lambda i, j: (i, j)),
         pl.BlockSpec((tf, d), lambda i, j: (j, 0)),
         pl.BlockSpec((tf, d), lambda i, j: (j, 0)),
         pl.BlockSpec((tf, d), lambda i, j: (j, 0))],
        [pl.BlockSpec((tm, d), lambda i, j: (i, 0)),
         pl.BlockSpec((tm, tf), lambda i, j: (i, j)),
         pl.BlockSpec((tm, tf), lambda i, j: (i, j)),
         pl.BlockSpec((tm, tf), lambda i, j: (i, j)),
         pl.BlockSpec((tm, d), lambda i, j: (i, 0)),
         pl.BlockSpec((tm, d), lambda i, j: (i, 0)),
         pl.BlockSpec((1, d), lambda i, j: (0, 0))],
        [jax.ShapeDtypeStruct((rows, d), F32),
         jax.ShapeDtypeStruct((rows, f), BF16),
         jax.ShapeDtypeStruct((rows, f), BF16),
         jax.ShapeDtypeStruct((rows, f), BF16),
         jax.ShapeDtypeStruct((rows, d), BF16),
         jax.ShapeDtypeStruct((rows, d), BF16),
         jax.ShapeDtypeStruct((1, d), F32)],
        [pltpu.VMEM((tm, d), F32)],
        (dh, h, g, a, b, wg, wu, wd), _params("arbitrary", "arbitrary"), exchange)


def _mm_tn(name, a, b, exchange=None):
    k, m = a.shape
    n = b.shape[1]
    tk = _tall_tile(k, 3)
    tm = _half_tile(m) if m > 1024 else m
    tn = _half_tile(n) if n > 1024 else n
    nk = k // tk

    def body(a_ref, b_ref, o_ref, acc):
        kk = pl.program_id(2)

        @pl.when(kk == 0)
        def _():
            acc[...] = jnp.zeros_like(acc)

        acc[...] += _dot_tn(a_ref[...], b_ref[...])

        @pl.when(kk == nk - 1)
        def _():
            o_ref[...] = acc[...].astype(BF16)

    outs = _call_with_exchange(
        name, body, (m // tm, n // tn, nk),
        [pl.BlockSpec((tk, tm), lambda i, j, kk: (kk, i)),
         pl.BlockSpec((tk, tn), lambda i, j, kk: (kk, j))],
        [pl.BlockSpec((tm, tn), lambda i, j, kk: (i, j))],
        [jax.ShapeDtypeStruct((m, n), BF16)],
        [pltpu.VMEM((tm, tn), F32)],
        (a, b), _params("arbitrary", "arbitrary", "arbitrary"), exchange)
    return outs if exchange else outs[0]


def _norm_proj(name, h, g, w):
    rows, d = h.shape
    n = w.shape[1]
    split = 3 * GROUP
    tm = _row_tile(rows)

    def body(h_ref, g_ref, w_ref, qkv_ref, p_ref, n_ref):
        nv, _, _ = _rms_fwd(h_ref[...], g_ref[...])
        nb = nv.astype(BF16)
        n_ref[...] = nb
        qkv_ref[...] = _dot(nb, w_ref[:, :split]).astype(BF16)
        p_ref[...] = _dot(nb, w_ref[:, split:])

    return pl.pallas_call(
        body, name=name, grid=(rows // tm,),
        in_specs=[pl.BlockSpec((tm, d), lambda i: (i, 0)),
                  pl.BlockSpec((1, d), lambda i: (0, 0)),
                  pl.BlockSpec((d, n), lambda i: (0, 0))],
        out_specs=[pl.BlockSpec((tm, split), lambda i: (i, 0)),
                   pl.BlockSpec((tm, n - split), lambda i: (i, 0)),
                   pl.BlockSpec((tm, d), lambda i: (i, 0))],
        out_shape=[jax.ShapeDtypeStruct((rows, split), BF16), jax.ShapeDtypeStruct((rows, n - split), F32),
                   jax.ShapeDtypeStruct((rows, d), BF16)],
        compiler_params=_params("arbitrary"),
    )(h, g, w)


def _out_proj(name, h, sb, rw, w):
    rows, d = h.shape
    gw = sb.shape[1]
    tm = _row_tile(rows)

    def body(h_ref, sb_ref, rw_ref, w_ref, o_ref, mix_ref):
        mix_ref[:, :gw] = sb_ref[...].astype(BF16)
        mix_ref[:, gw:] = rw_ref[...].astype(BF16)
        o_ref[...] = h_ref[...] + _dot(mix_ref[...], w_ref[...])

    return pl.pallas_call(
        body, name=name, grid=(rows // tm,),
        in_specs=[pl.BlockSpec((tm, d), lambda i: (i, 0)),
                  pl.BlockSpec((tm, gw), lambda i: (i, 0)),
                  pl.BlockSpec((tm, gw), lambda i: (i, 0)),
                  pl.BlockSpec((2 * gw, d), lambda i: (0, 0))],
        out_specs=[pl.BlockSpec((tm, d), lambda i: (i, 0)),
                   pl.BlockSpec((tm, 2 * gw), lambda i: (i, 0))],
        out_shape=[jax.ShapeDtypeStruct((rows, d), F32), jax.ShapeDtypeStruct((rows, 2 * gw), BF16)],
        compiler_params=_params("arbitrary"),
    )(h, sb, rw, w)


def _out_proj_bwd(name, dh, w):
    rows, d = dh.shape
    k = w.shape[0]
    tm = _row_tile(rows)

    def body(dh_ref, w_ref, dsb_ref, drw_ref, dhb_ref):
        dhb = dh_ref[...].astype(BF16)
        dhb_ref[...] = dhb
        dsb_ref[...] = _dot_nt(dhb, w_ref[:GROUP, :]).astype(BF16)
        drw_ref[...] = _dot_nt(dhb, w_ref[GROUP:, :])

    return pl.pallas_call(
        body, name=name, grid=(rows // tm,),
        in_specs=[pl.BlockSpec((tm, d), lambda i: (i, 0)),
                  pl.BlockSpec((k, d), lambda i: (0, 0))],
        out_specs=[pl.BlockSpec((tm, GROUP), lambda i: (i, 0)),
                   pl.BlockSpec((tm, GROUP), lambda i: (i, 0)),
                   pl.BlockSpec((tm, d), lambda i: (i, 0))],
        out_shape=[jax.ShapeDtypeStruct((rows, GROUP), BF16), jax.ShapeDtypeStruct((rows, GROUP), F32),
                   jax.ShapeDtypeStruct((rows, d), BF16)],
        compiler_params=_params("arbitrary"),
    )(dh, w)


def _norm_proj_bwd(name, dproj, w, h, g, dh):
    rows, n = dproj.shape
    d = w.shape[0]
    tm = _row_tile(rows)

    def body(dp_ref, w_ref, h_ref, g_ref, dh_ref, o_ref, dg_ref):
        @pl.when(pl.program_id(0) == 0)
        def _():
            dg_ref[...] = jnp.zeros_like(dg_ref)

        dn = _dot_nt(dp_ref[...], w_ref[...])
        gv = g_ref[...]
        _, xhat, rstd = _rms_fwd(h_ref[...], gv)
        dx, dg = _rms_bwd(dn, xhat, rstd, gv)
        o_ref[...] = dh_ref[...] + dx
        dg_ref[...] += dg

    return pl.pallas_call(
        body, name=name, grid=(rows // tm,),
        in_specs=[pl.BlockSpec((tm, n), lambda i: (i, 0)),
                  pl.BlockSpec((d, n), lambda i: (0, 0)),
                  pl.BlockSpec((tm, d), lambda i: (i, 0)),
                  pl.BlockSpec((1, d), lambda i: (0, 0)),
                  pl.BlockSpec((tm, d), lambda i: (i, 0))],
        out_specs=[pl.BlockSpec((tm, d), lambda i: (i, 0)),
                   pl.BlockSpec((1, d), lambda i: (0, 0))],
        out_shape=[jax.ShapeDtypeStruct((rows, d), F32), jax.ShapeDtypeStruct((1, d), F32)],
        compiler_params=_params("arbitrary"),
    )(dproj, w, h, g, dh)


def _loss_head(name, h, g, tgt):
    rows, d = h.shape
    tm = _row_tile(rows)

    def body(h_ref, g_ref, t_ref, loss_ref, dh_ref, dg_ref):
        i = pl.program_id(0)

        @pl.when(i == 0)
        def _():
            loss_ref[...] = jnp.zeros_like(loss_ref)
            dg_ref[...] = jnp.zeros_like(dg_ref)

        gv = g_ref[...]
        y, xhat, rstd = _rms_fwd(h_ref[...], gv)
        row = i * tm + lax.broadcasted_iota(jnp.int32, (tm, 1), 0)
        diff = jnp.where(row >= ROW0, y - t_ref[...], 0.0)
        part = 0.5 * jnp.sum(jnp.sum(diff * diff, axis=-1, keepdims=True), axis=0, keepdims=True) / d
        loss_ref[...] += jnp.broadcast_to(part, loss_ref.shape)
        dx, dg = _rms_bwd(diff / d, xhat, rstd, gv)
        dh_ref[...] = dx
        dg_ref[...] += dg

    return pl.pallas_call(
        body, name=name, grid=(rows // tm,),
        in_specs=[pl.BlockSpec((tm, d), lambda i: (i, 0)),
                  pl.BlockSpec((1, d), lambda i: (0, 0)),
                  pl.BlockSpec((tm, d), lambda i: (i, 0))],
        out_specs=[pl.BlockSpec((8, 128), lambda i: (0, 0)),
                   pl.BlockSpec((tm, d), lambda i: (i, 0)),
                   pl.BlockSpec((1, d), lambda i: (0, 0))],
        out_shape=[jax.ShapeDtypeStruct((8, 128), F32),
                   jax.ShapeDtypeStruct((rows, d), F32),
                   jax.ShapeDtypeStruct((1, d), F32)],
        compiler_params=_params("arbitrary"),
    )(h, g, tgt)


def _sb_block(qb, kb, q0, jb, scale):
    bq, bk = qb.shape[0], kb.shape[0]
    z = _dot_nt(qb, kb) * scale
    qpos = q0 + lax.broadcasted_iota(jnp.int32, (bq, bk), 0)
    kpos = jb * bk + lax.broadcasted_iota(jnp.int32, (bq, bk), 1)
    valid = (kpos < qpos) & (kpos >= META_PAD)
    e = jnp.exp(-jnp.abs(z))
    log_keep = jnp.where(valid, -(jnp.maximum(z, 0.0) + jnp.log(1.0 + e)), 0.0)
    return z, valid, e, log_keep


def _tri2(n, cmp):
    r = lax.broadcasted_iota(jnp.int32, (2 * n, n), 0) % n
    c = lax.broadcasted_iota(jnp.int32, (2 * n, n), 1)
    return cmp(r, c).astype(BF16)


def _dot_split(x, t2):
    hi, lo = _split2(x)
    return _dot(jnp.concatenate([hi, lo], axis=1), t2)


ATT_HEADS = 8
ATT_WIDTH = ATT_HEADS * HEAD
ATT_CUT = -104.0
ATT_TILES = GROUP // ATT_WIDTH


def _lanes(hh):
    return slice(hh * HEAD, (hh + 1) * HEAD)


def _exchange_steps(grid):
    step, total = pl.program_id(0), 1
    for a in range(1, len(grid)):
        step = step * grid[a] + pl.program_id(a)
    for size in grid:
        total *= size
    return step == 0, step == max(total - 2, 0), step == total - 1


def _sb_fwd(name, qkv, shards=()):
    rows = qkv.shape[0]
    nh, dh = N_HEADS, HEAD
    bq, bk, hg = _row_tile(rows), ATT_BLOCK, ATT_HEADS
    per = bq // bk
    scale = dh ** -0.5
    ns = len(shards)
    grid = (nh // hg, rows // bq)

    def body(q_ref, k_ref, v_ref, *rest):
        o_ref, rt_ref, cnt_ref = rest[ns:ns + 3]
        if ns:
            first, before_last, last = _exchange_steps(grid)
            start, forward, finish = _gather_exchange(rest[:ns], rest[ns + 3:2 * ns + 3], rest[2 * ns + 3:])
            pl.when(first)(start)
        i = pl.program_id(1)
        after = _tri2(bk, lambda r, c: r > c)
        nkb = (i + 1) * per

        def live(state):
            n, carry = state
            top = jnp.max(carry[0][0])
            for hh in range(1, hg):
                top = jnp.maximum(top, jnp.max(carry[hh][0]))
            return (n < nkb) & (top >= ATT_CUT)

        def visit(carry, jb, r0):
            off = pl.multiple_of(jb * bk, bk)
            out = []
            for hh in range(hg):
                rest, acc = carry[hh]
                kb = k_ref[pl.ds(off, bk), _lanes(hh)]
                vb = v_ref[pl.ds(off, bk), _lanes(hh)]
                z, valid, _, log_keep = _sb_block(q_ref[r0:, _lanes(hh)], kb, i * bq + r0, jb, scale)
                log_rest = rest[r0:] + _dot_split(log_keep, after)
                attn = jnp.where(valid, jnp.exp(z + log_keep + log_rest), 0.0)
                new_rest = rest[r0:] + jnp.sum(log_keep, axis=-1, keepdims=True)
                new_acc = acc[r0:] + _dot(attn.astype(BF16), vb)
                if r0:
                    new_rest = jnp.concatenate([rest[:r0], new_rest], axis=0)
                    new_acc = jnp.concatenate([acc[:r0], new_acc], axis=0)
                out.append((new_rest, new_acc))
            return tuple(out)

        carry = tuple((jnp.zeros((bq, 1), F32), jnp.zeros((bq, dh), F32)) for _ in range(hg))
        for dgl in reversed(range(per)):
            carry = visit(carry, i * per + dgl, dgl * bk)
        n, res = lax.while_loop(live, lambda s: (s[0] + 1, visit(s[1], nkb - 1 - s[0], 0)), (jnp.int32(per), carry))
        for hh in range(hg):
            rt_ref[hh] = res[hh][0]
            o_ref[:, _lanes(hh)] = res[hh][1]
            cnt_ref[hh] = jnp.full((bq, 1), n, F32)
        if ns:
            pl.when(before_last)(forward)
            pl.when(last)(finish)

    return pl.pallas_call(
        body, name=name, grid=grid,
        in_specs=[pl.BlockSpec((bq, ATT_WIDTH), lambda h, i: (i, h)),
                  pl.BlockSpec((rows, ATT_WIDTH), lambda h, i: (0, ATT_TILES + h)),
                  pl.BlockSpec((rows, ATT_WIDTH), lambda h, i: (0, 2 * ATT_TILES + h))] + [HBM] * ns,
        out_specs=[pl.BlockSpec((bq, ATT_WIDTH), lambda h, i: (i, h)),
                   pl.BlockSpec((hg, bq, 1), lambda h, i: (h, i, 0)),
                   pl.BlockSpec((hg, bq, 1), lambda h, i: (h, i, 0))] + [HBM] * ns,
        out_shape=[jax.ShapeDtypeStruct((rows, GROUP), F32), jax.ShapeDtypeStruct((nh, rows, 1), F32),
                   jax.ShapeDtypeStruct((nh, rows, 1), F32)]
        + [jax.ShapeDtypeStruct((N_CHIPS,) + s.shape, s.dtype) for s in shards],
        scratch_shapes=_gather_sems(ns) if ns else [],
        compiler_params=_params("arbitrary", "arbitrary"),
    )(qkv, qkv, qkv, *shards)


def _sb_bwd(name, qkv, rt, cnt, do, parts=()):
    rows = qkv.shape[0]
    nh, dh = N_HEADS, HEAD
    bq, bk, hg = _row_tile(rows), ATT_BLOCK, ATT_HEADS
    per = bq // bk
    scale = dh ** -0.5
    ns = len(parts)
    grid = (nh // hg, rows // bq)

    def body(q_ref, k_ref, v_ref, rt_ref, cnt_ref, do_ref, *rest):
        dq_ref, dk_ref, dv_ref = rest[ns:ns + 3]
        if ns:
            at_first, before_last, at_last = _exchange_steps(grid)
            start, forward, finish = _reduce_exchange(rest[:ns], rest[ns + 3:2 * ns + 3], rest[2 * ns + 3:3 * ns + 3],
                                             rest[3 * ns + 3:])
            pl.when(at_first)(start)
        i = pl.program_id(1)

        @pl.when(i == 0)
        def _():
            dk_ref[...] = jnp.zeros_like(dk_ref)
            dv_ref[...] = jnp.zeros_like(dv_ref)

        upto = _tri2(bk, lambda r, c: r <= c)
        before = _tri2(bk, lambda r, c: r < c)
        nkb = (i + 1) * per
        first = nkb - jnp.max(cnt_ref[0]).astype(jnp.int32)

        def visit(carry, jb, r0):
            off = pl.multiple_of(jb * bk, bk)
            out = []
            for hh in range(hg):
                keep_sum, g_sum, dq = carry[hh]
                qb, dob = q_ref[r0:, _lanes(hh)], do_ref[r0:, _lanes(hh)]
                kb = k_ref[pl.ds(off, bk), _lanes(hh)]
                vb = v_ref[pl.ds(off, bk), _lanes(hh)]
                z, valid, e, log_keep = _sb_block(qb, kb, i * bq + r0, jb, scale)
                log_rest = rt_ref[hh, r0:, :] - keep_sum[r0:] - _dot_split(log_keep, upto)
                attn = jnp.where(valid, jnp.exp(z + log_keep + log_rest), 0.0)
                g = attn * _dot_nt(dob, vb)
                g_before = g_sum[r0:] + _dot_split(g, before)
                inv = 1.0 / (1.0 + e)
                sig = jnp.where(z >= 0, inv, e * inv)
                dz = (jnp.where(valid, g * (1.0 - sig) - g_before * sig, 0.0) * scale).astype(BF16)
                dk_ref[pl.ds(off, bk), _lanes(hh)] += _dot_tn(dz, qb)
                dv_ref[pl.ds(off, bk), _lanes(hh)] += _dot_tn(attn.astype(BF16), dob)
                new = (keep_sum[r0:] + jnp.sum(log_keep, axis=-1, keepdims=True),
                       g_sum[r0:] + jnp.sum(g, axis=-1, keepdims=True),
                       dq[r0:] + _dot(dz, kb))
                if r0:
                    new = tuple(jnp.concatenate([old[:r0], x], axis=0) for old, x in zip(carry[hh], new))
                out.append(new)
            return tuple(out)

        zero = jnp.zeros((bq, 1), F32)
        res = lax.fori_loop(first, nkb - per, lambda jb, c: visit(c, jb, 0),
                            tuple((zero, zero, jnp.zeros((bq, dh), F32)) for _ in range(hg)))
        for dgl in range(per):
            res = visit(res, i * per + dgl, dgl * bk)
        for hh in range(hg):
            dq_ref[:, _lanes(hh)] = res[hh][2]
        if ns:
            pl.when(before_last)(forward)
            pl.when(at_last)(finish)

    return pl.pallas_call(
        body, name=name, grid=grid,
        in_specs=[pl.BlockSpec((bq, ATT_WIDTH), lambda h, i: (i, h)),
                  pl.BlockSpec((rows, ATT_WIDTH), lambda h, i: (0, ATT_TILES + h)),
                  pl.BlockSpec((rows, ATT_WIDTH), lambda h, i: (0, 2 * ATT_TILES + h)),
                  pl.BlockSpec((hg, bq, 1), lambda h, i: (h, i, 0)),
                  pl.BlockSpec((hg, bq, 1), lambda h, i: (h, i, 0)),
                  pl.BlockSpec((bq, ATT_WIDTH), lambda h, i: (i, h))] + [HBM] * ns,
        out_specs=[pl.BlockSpec((bq, ATT_WIDTH), lambda h, i: (i, h)),
                   pl.BlockSpec((rows, ATT_WIDTH), lambda h, i: (0, h)),
                   pl.BlockSpec((rows, ATT_WIDTH), lambda h, i: (0, h))] + [HBM] * (2 * ns),
        out_shape=[jax.ShapeDtypeStruct((rows, GROUP), F32)] * 3
        + [jax.ShapeDtypeStruct(s.shape, s.dtype) for s in parts] * 2,
        scratch_shapes=_reduce_sems(ns) if ns else [],
        compiler_params=_params("arbitrary", "arbitrary"),
    )(qkv, qkv, qkv, rt, cnt, do, *parts)


def _head_sum(x, ones_bd):
    return _dot_split(x, ones_bd)


def _rwkv_pre(p, p_prev, mu, w0, a0, k_k, k_a, w_up, a_up, g_up, ones_bd):
    xs = p + (p_prev - p) * mu
    r = xs[:, :GROUP]
    k0 = xs[:, GROUP:2 * GROUP]
    v = xs[:, 2 * GROUP:3 * GROUP]
    lo = xs[:, 3 * GROUP:]
    wa = w0 + _dot(jnp.tanh(lo).astype(BF16), w_up.astype(BF16))
    w = -(jnp.maximum(-wa, 0.0) + jnp.log(1.0 + jnp.exp(-jnp.abs(wa)))) - 0.5
    log_decay = -jnp.exp(w)
    alpha = _sigmoid(a0 + _dot(lo.astype(BF16), a_up.astype(BF16)))
    gate = _dot(_sigmoid(lo).astype(BF16), g_up.astype(BF16))
    kk = k0 * k_k
    kk = kk * lax.rsqrt(jnp.maximum(_head_sum(kk * kk, ones_bd), 1e-24))
    k = k0 * (1.0 + (alpha - 1.0) * k_a)
    return r, log_decay, k, v, -kk, kk * alpha, gate


def _rwkv_post(y, r, k, v, gate, lnx_w, lnx_b, r_k, ones_bd):
    mean = _head_sum(y, ones_bd) * (1.0 / HEAD)
    yc = y - mean
    var = _head_sum(yc * yc, ones_bd) * (1.0 / HEAD)
    yn = yc * lax.rsqrt(var + LNX_EPS) * lnx_w + lnx_b
    bonus = _head_sum(r * k * r_k, ones_bd) * v
    return (yn + bonus) * gate


_PRE_VEC = 5
_PRE_MAT = 3


def _heads(x):
    return jnp.stack([x[:, _lanes(h)] for h in range(N_HEADS)])


def _unheads(x):
    return jnp.concatenate([x[h] for h in range(N_HEADS)], axis=1)


def _edge_spec(tm, width, tile_of):
    return pl.BlockSpec((8, width), lambda i: (jnp.maximum(tile_of(i) * (tm // 8) - 1, 0), 0))


def _previous_rows(p_ref, edge_ref, tile):
    p = p_ref[...]
    edge = jnp.where(tile == 0, 0.0, edge_ref[7:8, :])
    row = lax.broadcasted_iota(jnp.int32, (p.shape[0], 1), 0)
    return jnp.where(row == 0, edge, pltpu.roll(p, 1, axis=0))


def _rwkv_pre_fwd(name, p, vecs, mats, ones_bd):
    rows = p.shape[0]
    tm = _row_tile(rows)
    row_spec = lambda w: pl.BlockSpec((tm, w), lambda i: (i, 0))
    full = lambda a: pl.BlockSpec(a.shape, lambda i: (0,) * a.ndim)

    def body(p_ref, edge_ref, *refs):
        ins = [r[...] for r in refs[:_PRE_VEC + _PRE_MAT + 1]]
        outs = refs[_PRE_VEC + _PRE_MAT + 1:]
        prev = _previous_rows(p_ref, edge_ref, pl.program_id(0))
        for o_ref, val in zip(outs, _rwkv_pre(p_ref[...], prev, *ins)):
            o_ref[...] = val

    return pl.pallas_call(
        body, name=name, grid=(rows // tm,),
        in_specs=([row_spec(RW_COLS), _edge_spec(tm, RW_COLS, lambda i: i)]
                  + [full(a) for a in (*vecs, *mats, ones_bd)]),
        out_specs=[row_spec(GROUP)] * 7,
        out_shape=[jax.ShapeDtypeStruct((rows, GROUP), F32)] * 7,
        compiler_params=_params("arbitrary"),
    )(p, p, *vecs, *mats, ones_bd)


def _rwkv_pre_bwd(name, p, vecs, mats, ones_bd, cts_scan, ct_gate, cts_b):
    rows = p.shape[0]
    tm = _row_tile(rows)
    nt = rows // tm
    n_par = _PRE_VEC + _PRE_MAT
    tile_of = lambda i: nt - 1 - i
    row_spec = lambda w: pl.BlockSpec((tm, w), lambda i: (tile_of(i), 0))
    full = lambda a: pl.BlockSpec(a.shape, lambda i: (0,) * a.ndim)

    def body(*refs):
        p_ref, edge_ref = refs[0], refs[1]
        par = [r[...] for r in refs[2:2 + n_par]]
        ones = refs[2 + n_par][...]
        cta = [r[...] for r in refs[3 + n_par:10 + n_par]]
        ctb = [r[...] for r in refs[10 + n_par:13 + n_par]]
        dp_ref, par_outs, carry = refs[13 + n_par], refs[14 + n_par:-1], refs[-1]
        step = pl.program_id(0)

        @pl.when(step == 0)
        def _():
            carry[...] = jnp.zeros_like(carry)
            for o_ref in par_outs:
                o_ref[...] = jnp.zeros_like(o_ref)

        ct = (cta[0] + ctb[0], cta[1], cta[2] + ctb[1], cta[3] + ctb[2], cta[4], cta[5], cta[6])
        _, vjp = jax.vjp(lambda pv, ppv, *pr: _rwkv_pre(pv, ppv, *pr, ones),
                         p_ref[...], _previous_rows(p_ref, edge_ref, tile_of(step)), *par)
        grads = vjp(ct)
        row = lax.broadcasted_iota(jnp.int32, (tm, 1), 0)
        dp_ref[...] = grads[0] + jnp.where(row == tm - 1, carry[0:1, :], pltpu.roll(grads[1], tm - 1, axis=0))
        carry[0:1, :] = grads[1][0:1, :]
        for o_ref, gval in zip(par_outs, grads[2:]):
            o_ref[...] += gval

    par_arrays = (*vecs, *mats)
    return pl.pallas_call(
        body, name=name, grid=(nt,),
        in_specs=([row_spec(RW_COLS), _edge_spec(tm, RW_COLS, tile_of)] + [full(a) for a in (*par_arrays, ones_bd)]
                  + [row_spec(GROUP)] * 10),
        out_specs=[row_spec(RW_COLS)] + [full(a) for a in par_arrays],
        out_shape=[jax.ShapeDtypeStruct((rows, RW_COLS), F32)] + [jax.ShapeDtypeStruct(a.shape, F32) for a in par_arrays],
        scratch_shapes=[pltpu.VMEM((8, RW_COLS), F32)],
        compiler_params=_params("arbitrary"),
    )(p, p, *par_arrays, ones_bd, *cts_scan, ct_gate, *cts_b)


def _rwkv_post_fwd(name, y, r, k, v, gate, vecs, ones_bd):
    rows = r.shape[0]
    tm = _row_tile(rows)
    row_spec = pl.BlockSpec((tm, GROUP), lambda i: (i, 0))
    full = lambda a: pl.BlockSpec(a.shape, lambda i: (0,) * a.ndim)

    def body(*refs):
        refs[-1][...] = _rwkv_post(*(r_[...] for r_ in refs[:-1]))

    return pl.pallas_call(
        body, name=name, grid=(rows // tm,),
        in_specs=[row_spec] * 5 + [full(a) for a in (*vecs, ones_bd)],
        out_specs=row_spec,
        out_shape=jax.ShapeDtypeStruct((rows, GROUP), F32),
        compiler_params=_params("arbitrary"),
    )(y, r, k, v, gate, *vecs, ones_bd)


def _rwkv_post_bwd(name, y, r, k, v, gate, vecs, ones_bd, dout):
    rows = r.shape[0]
    tm = _row_tile(rows)
    row_spec = pl.BlockSpec((tm, GROUP), lambda i: (i, 0))
    full = lambda a: pl.BlockSpec(a.shape, lambda i: (0,) * a.ndim)

    def body(*refs):
        vals = [r_[...] for r_ in refs[:8]]
        ones = refs[8][...]
        dout_v = refs[9][...]
        outs = refs[10:]
        _, vjp = jax.vjp(lambda *a: _rwkv_post(*a, ones), *vals)
        grads = vjp(dout_v)
        for o_ref, gval in zip(outs[:5], grads[:5]):
            o_ref[...] = gval

        @pl.when(pl.program_id(0) == 0)
        def _():
            for o_ref in outs[5:]:
                o_ref[...] = jnp.zeros_like(o_ref)

        for o_ref, gval in zip(outs[5:], grads[5:]):
            o_ref[...] += gval

    return pl.pallas_call(
        body, name=name, grid=(rows // tm,),
        in_specs=[row_spec] * 5 + [full(a) for a in (*vecs, ones_bd)] + [row_spec],
        out_specs=[row_spec] * 5 + [full(a) for a in vecs],
        out_shape=[jax.ShapeDtypeStruct((rows, GROUP), F32)] * 5 + [jax.ShapeDtypeStruct(a.shape, F32) for a in vecs],
        compiler_params=_params("arbitrary"),
    )(y, r, k, v, gate, *vecs, ones_bd, dout)


_NN = (((2,), (1,)), ((0,), (0,)))
_NT = (((2,), (2,)), ((0,), (0,)))
_TN = (((1,), (1,)), ((0,), (0,)))


_BWD_FORMS = {"nn": (("nt", False), ("tn", False)),
              "nt": (("nn", False), ("tn", True)),
              "tn": (("nt", True), ("nn", False))}
_DIMS = {"nn": _NN, "nt": _NT, "tn": _TN}


def _bdot(a, b, form):
    return lax.dot_general(a.astype(BF16), b.astype(BF16), _DIMS[form], preferred_element_type=F32)


@functools.partial(jax.custom_vjp, nondiff_argnums=(2,))
def _bmm(a, b, form):
    return _bdot(a, b, form)


def _bmm_fwd(a, b, form):
    return _bdot(a, b, form), (a.astype(BF16), b.astype(BF16))


def _bmm_bwd(form, res, dc):
    a, b = res
    (fa, swap_a), (fb, swap_b) = _BWD_FORMS[form]
    da = _bdot(b, dc, fa) if swap_a else _bdot(dc, b, fa)
    db = _bdot(dc, a, fb) if swap_b else _bdot(a, dc, fb)
    return da, db


_bmm.defvjp(_bmm_fwd, _bmm_bwd)


@jax.custom_vjp
def _cumsum_steps(x):
    return _tri_apply(x, lambda r, c: r >= c)


def _tri_apply(x, cmp):
    nh, c, _ = x.shape
    tri = cmp(lax.broadcasted_iota(jnp.int32, (c, c), 0), lax.broadcasted_iota(jnp.int32, (c, c), 1))
    tri = jnp.broadcast_to(tri.astype(BF16)[None], (nh, c, c))
    hi, lo = _split2(x)
    return (lax.dot_general(tri, hi, _NN, preferred_element_type=F32)
            + lax.dot_general(tri, lo, _NN, preferred_element_type=F32))


_cumsum_steps.defvjp(lambda x: (_cumsum_steps(x), None), lambda _, d: (_tri_apply(d, lambda r, c: r <= c),))


@jax.custom_vjp
def _neumann(n_mat):
    c = n_mat.shape[1]
    inv, power, span = n_mat, _bmm(n_mat, n_mat, "nn"), 2
    while span < c:
        both = _bmm(jnp.concatenate([power, inv], axis=1), power, "nn")
        inv = inv + power + both[:, c:]
        power = both[:, :c]
        span *= 2
    return inv


def _neumann_fwd(n_mat):
    inv = _neumann(n_mat)
    return inv, inv


def _neumann_bwd(inv, d):
    left = d + _bmm(inv, d, "tn")
    return (left + _bmm(left, inv, "nt"),)


_neumann.defvjp(_neumann_fwd, _neumann_bwd)


def _chunk(state, r, log_w, k, v, a, b):
    nh, c, _ = r.shape
    row = lax.broadcasted_iota(jnp.int32, (c, c), 0)
    col = lax.broadcasted_iota(jnp.int32, (c, c), 1)
    cum = _cumsum_steps(log_w)
    mid = cum[:, c // 2 - 1:c // 2, :]
    a_t = a * jnp.exp(cum - log_w - mid)
    r_t = r * jnp.exp(cum - mid)
    back = jnp.exp(mid - cum)
    b_t = b * back
    k_t = k * back
    strict, incl = (row > col)[None], (row >= col)[None]
    ar = jnp.concatenate([a_t, r_t], axis=1)
    on_b = _bmm(ar, b_t, "nt")
    on_k = _bmm(ar, k_t, "nt")
    n_mat = jnp.where(strict, on_b[:, :c], 0.0)
    p_mat = jnp.where(incl, on_b[:, c:], 0.0)
    m_mat = jnp.where(strict, on_k[:, :c], 0.0)
    q_mat = jnp.where(incl, on_k[:, c:], 0.0)
    inv = _neumann(n_mat)
    s_mid = state * jnp.swapaxes(jnp.exp(mid), 1, 2)
    x = _bmm(jnp.concatenate([a_t, m_mat], axis=2), jnp.concatenate([s_mid, v], axis=1), "nn")
    u = x + _bmm(inv, x, "nn")
    y = _bmm(jnp.concatenate([r_t, p_mat, q_mat], axis=2), jnp.concatenate([s_mid, u, v], axis=1), "nn")
    grown = _bmm(jnp.concatenate([b_t, k_t], axis=1), jnp.concatenate([u, v], axis=1), "tn")
    s_new = (s_mid + grown) * jnp.swapaxes(jnp.exp(cum[:, c - 1:c, :] - mid), 1, 2)
    return y, s_new


def _scan_fwd(name, ops):
    rows = ops[0].shape[0]
    nh, dh = N_HEADS, HEAD
    nc, per = rows // CHUNK, SCAN_CHUNKS
    spec = pl.BlockSpec((per * CHUNK, GROUP), lambda c: (c, 0))

    def body(r_ref, w_ref, k_ref, v_ref, a_ref, b_ref, y_ref, st_ref, state):
        @pl.when(pl.program_id(0) == 0)
        def _():
            state[...] = jnp.zeros_like(state)

        s = state[...]
        for u in range(per):
            at = slice(u * CHUNK, (u + 1) * CHUNK)
            st_ref[u] = s
            y, s = _chunk(s, *(_heads(ref[at, :]) for ref in (r_ref, w_ref, k_ref, v_ref, a_ref, b_ref)))
            y_ref[at, :] = _unheads(y)
        state[...] = s

    return pl.pallas_call(
        body, name=name, grid=(nc // per,),
        in_specs=[spec] * 6,
        out_specs=[spec, pl.BlockSpec((per, nh, dh, dh), lambda c: (c, 0, 0, 0))],
        out_shape=[jax.ShapeDtypeStruct((rows, GROUP), F32), jax.ShapeDtypeStruct((nc, nh, dh, dh), F32)],
        scratch_shapes=[pltpu.VMEM((nh, dh, dh), F32)],
        compiler_params=_params("arbitrary"),
    )(*ops)


def _scan_bwd(name, ops, states, dy):
    rows = ops[0].shape[0]
    nh, dh = N_HEADS, HEAD
    nc, per = rows // CHUNK, SCAN_CHUNKS
    steps = nc // per
    spec = pl.BlockSpec((per * CHUNK, GROUP), lambda c: (steps - 1 - c, 0))

    def body(r_ref, w_ref, k_ref, v_ref, a_ref, b_ref, st_ref, dy_ref, *rest):
        outs, dstate = rest[:6], rest[6]

        @pl.when(pl.program_id(0) == 0)
        def _():
            dstate[...] = jnp.zeros_like(dstate)

        ds = dstate[...]
        for u in reversed(range(per)):
            at = slice(u * CHUNK, (u + 1) * CHUNK)
            _, vjp = jax.vjp(_chunk, st_ref[u],
                             *(_heads(ref[at, :]) for ref in (r_ref, w_ref, k_ref, v_ref, a_ref, b_ref)))
            grads = vjp((_heads(dy_ref[at, :]), ds))
            ds = grads[0]
            for o_ref, gval in zip(outs, grads[1:]):
                o_ref[at, :] = _unheads(gval)
        dstate[...] = ds

    return pl.pallas_call(
        body, name=name, grid=(steps,),
        in_specs=[spec] * 6 + [pl.BlockSpec((per, nh, dh, dh), lambda c: (steps - 1 - c, 0, 0, 0)), spec],
        out_specs=[spec] * 6,
        out_shape=[jax.ShapeDtypeStruct((rows, GROUP), F32)] * 6,
        scratch_shapes=[pltpu.VMEM((nh, dh, dh), F32)],
        compiler_params=_params("arbitrary"),
    )(*ops, states, dy)


def _pad_cols(x, cols):
    return jnp.concatenate([x, jnp.zeros(x.shape[:-1] + (cols - x.shape[-1],), x.dtype)], axis=-1)


def _lora_pad(w_up, a_up, g_up):
    z = lambda n: jnp.zeros((n, GROUP), F32)
    return (jnp.concatenate([w_up, z(LORA_PAD - LORA_W)], 0),
            jnp.concatenate([z(LORA_W), a_up, z(LORA_PAD - LORA_W - LORA_A)], 0),
            jnp.concatenate([z(LORA_W + LORA_A), g_up, z(LORA_PAD - LORA_W - LORA_A - LORA_G)], 0))


MID = ['w_in']
LATE = ['ffn2_w_gate', 'ffn2_w_up', 'ffn2_w_down', 'w_out']


def _local_step(x, tgt, w, late=None):
    d = x.shape[1]
    zeros = jnp.zeros((META_PAD, d), F32)
    h0 = jnp.concatenate([zeros, w["meta_tokens"], x], axis=0)
    tgt_p = jnp.concatenate([jnp.zeros((ROW0, d), F32), tgt], axis=0)
    ones_bd = ((lax.broadcasted_iota(jnp.int32, (2 * GROUP, GROUP), 0) % GROUP) // HEAD
               == lax.broadcasted_iota(jnp.int32, (2 * GROUP, GROUP), 1) // HEAD).astype(BF16)
    pre_vecs = (_pad_cols(w["rwkv_mu"], RW_COLS), w["rwkv_w0"], w["rwkv_a0"], w["rwkv_k_k"], w["rwkv_k_a"])
    pre_mats = _lora_pad(w["rwkv_w_up"], w["rwkv_a_up"], w["rwkv_g_up"])
    post_vecs = (w["rwkv_lnx_w"], w["rwkv_lnx_b"], w["rwkv_r_k"].reshape(1, GROUP))

    h1, a1, b1, *gathered = _ffn_fwd("ffn1_fwd", h0, w["ffn1_norm"], w["ffn1_w_gate"], w["ffn1_w_up"],
                                     w["ffn1_w_down"], late and ("gather", late.shards["mid"]))
    if late is not None:
        w = {**w, **late.join("mid", gathered)}
    w_in = _pad_cols(w["w_in"], IN_COLS_PAD)
    qkv, p, n2 = _norm_proj("in_proj", h1, w["mix_norm"], w_in)
    sb, rest_total, visited, *gathered = _sb_fwd("sb_fwd", qkv, late.shards["late"] if late else ())
    if late is not None:
        w = {**w, **late.join("late", gathered)}
    pre = _rwkv_pre_fwd("rwkv_pre_fwd", p, pre_vecs, pre_mats, ones_bd)
    scan_ops, token_ops = pre[:6], (pre[0], pre[2], pre[3], pre[6])
    y, states = _scan_fwd("rwkv_scan_fwd", scan_ops)
    rw = _rwkv_post_fwd("rwkv_post_fwd", y, *token_ops, post_vecs, ones_bd)
    h2, mix = _out_proj("out_proj", h1, sb, rw, w["w_out"])
    h3, a2, b2 = _ffn_fwd("ffn2_fwd", h2, w["ffn2_norm"], w["ffn2_w_gate"], w["ffn2_w_up"], w["ffn2_w_down"])
    loss8, dh3, g_final = _loss_head("loss_head", h3, w["final_norm"].reshape(1, d), tgt_p)

    g = {"final_norm": g_final.reshape(d)}
    dh2, da2, db2, s2, n3, dhh3, g["ffn2_norm"] = _ffn_bwd(
        "ffn2_bwd", dh3, h2, w["ffn2_norm"], a2, b2, w["ffn2_w_gate"], w["ffn2_w_up"], w["ffn2_w_down"])
    g["ffn2_w_gate"] = _mm_tn("ffn2_dgate", da2, n3)
    g["ffn2_w_up"] = _mm_tn("ffn2_dup", db2, n3)
    g["ffn2_w_down"] = _mm_tn("ffn2_ddown", s2, dhh3)
    dsb, drw, dh2b = _out_proj_bwd("out_proj_bwd", dh2, w["w_out"])
    g["w_out"] = _mm_tn("out_proj_dw", mix, dh2b)
    dq, dk, dv, *reduced_late = _sb_bwd("sb_bwd", qkv, rest_total, visited, dsb, late.parts("late", g) if late else ())
    post_g = _rwkv_post_bwd("rwkv_post_bwd", y, *token_ops, post_vecs, ones_bd, drw)
    g["rwkv_lnx_w"], g["rwkv_lnx_b"] = post_g[5], post_g[6]
    g["rwkv_r_k"] = post_g[7].reshape(1, N_HEADS, HEAD)
    scan_g = _scan_bwd("rwkv_scan_bwd", scan_ops, states, post_g[0])
    pre_g = _rwkv_pre_bwd("rwkv_pre_bwd", p, pre_vecs, pre_mats, ones_bd, scan_g, post_g[4], post_g[1:4])
    dp = pre_g[0]
    g["rwkv_mu"] = pre_g[1][:, :w["rwkv_mu"].shape[1]]
    g["rwkv_w0"], g["rwkv_a0"], g["rwkv_k_k"], g["rwkv_k_a"] = pre_g[2:6]
    g["rwkv_w_up"] = pre_g[6][:LORA_W]
    g["rwkv_a_up"] = pre_g[7][LORA_W:LORA_W + LORA_A]
    g["rwkv_g_up"] = pre_g[8][LORA_W + LORA_A:LORA_W + LORA_A + LORA_G]
    live = (jnp.arange(h0.shape[0]) >= META_PAD)[:, None]
    dproj = jnp.where(live, jnp.concatenate([dq, dk, dv, dp], axis=1), 0.0).astype(BF16)
    g["w_in"] = _mm_tn("in_proj_dw", n2, dproj)[:, :w["w_in"].shape[1]]
    dh1, g["mix_norm"] = _norm_proj_bwd("in_proj_bwd", dproj, w_in, h1, w["mix_norm"], dh2)
    dh0, da1, db1, s1, n1, dhh1, g["ffn1_norm"], *reduced_mid = _ffn_bwd(
        "ffn1_bwd", dh1, h0, w["ffn1_norm"], a1, b1, w["ffn1_w_gate"], w["ffn1_w_up"], w["ffn1_w_down"],
        late and ("reduce", late.parts("mid", g)))
    g["meta_tokens"] = dh0[META_PAD:ROW0]
    reduced = {"mid": reduced_mid, "late": reduced_late}
    if late is None:
        g["ffn1_w_gate"] = _mm_tn("ffn1_dgate", da1, n1)
        g["ffn1_w_up"] = _mm_tn("ffn1_dup", db1, n1)
        g["ffn1_w_down"] = _mm_tn("ffn1_ddown", s1, dhh1)
    else:
        g["ffn1_w_gate"], reduced["small"] = _mm_tn("ffn1_dgate", da1, n1, ("all_reduce", [late.small(g, loss8[0, 0])]))
        g["ffn1_w_up"], *reduced["gate"] = _mm_tn("ffn1_dup", db1, n1, ("reduce", late.parts("gate", g)))
        g["ffn1_w_down"], *reduced["up"] = _mm_tn("ffn1_ddown", s1, dhh1, ("reduce", late.parts("up", g)))
    return loss8[0, 0], dh0[ROW0:], g, reduced


N_CHIPS = 4
N_DEV = 8
HBM = pl.BlockSpec(memory_space=pltpu.HBM)


def _place():
    return lax.axis_index("x"), lax.axis_index("y"), lax.axis_index("c")


def _other_chips(x, y):
    return [(1 - x, y), (x, 1 - y), (1 - x, 1 - y)]


def _gather_sems(n):
    return [pltpu.SemaphoreType.DMA((3 * n,)), pltpu.SemaphoreType.DMA((3 * n,)), pltpu.SemaphoreType.DMA((n,)),
            pltpu.SemaphoreType.DMA((3 * n,)), pltpu.SemaphoreType.DMA((3 * n,))]


def _gather_exchange(ins, outs, sems):
    n = len(ins)
    half = [r.shape[0] // 2 for r in ins]
    send, recv, local, d2d_send, d2d_recv = sems
    x, y, c = _place()
    me = 2 * x + y
    chips = _other_chips(x, y)

    def rows_of(k, h):
        return pl.ds(pl.multiple_of(h * half[k], 8), half[k])

    def own(k):
        return pltpu.make_async_copy(ins[k], outs[k].at[me], local.at[k])

    def copy(j, k, slot):
        return pltpu.make_async_remote_copy(
            src_ref=ins[k].at[rows_of(k, c)], dst_ref=outs[k].at[slot, rows_of(k, c)],
            send_sem=send.at[j * n + k], recv_sem=recv.at[j * n + k],
            device_id=(chips[j][0], chips[j][1], c), device_id_type=MESH)

    def passed(j, k, h):
        slot = 2 * chips[j][0] + chips[j][1]
        return pltpu.make_async_remote_copy(
            src_ref=outs[k].at[slot, rows_of(k, h)], dst_ref=outs[k].at[slot, rows_of(k, h)],
            send_sem=d2d_send.at[j * n + k], recv_sem=d2d_recv.at[j * n + k],
            device_id=(x, y, 1 - c), device_id_type=MESH)

    def start():
        for k in range(n):
            own(k).start()
        for j in range(3):
            for k in range(n):
                copy(j, k, me).start()

    def forward():
        for j in range(3):
            for k in range(n):
                copy(j, k, 2 * chips[j][0] + chips[j][1]).wait_recv()
                passed(j, k, c).start()

    def finish():
        for j in range(3):
            for k in range(n):
                passed(j, k, 1 - c).wait_recv()
        for j in range(3):
            for k in range(n):
                copy(j, k, me).wait_send()
                passed(j, k, c).wait_send()
        for k in range(n):
            own(k).wait()

    return start, forward, finish


def _gather_shards(name, shards):
    n = len(shards)

    def body(*refs):
        for stage in _gather_exchange(refs[:n], refs[n:2 * n], refs[2 * n:]):
            stage()

    return pl.pallas_call(
        body, name=name,
        in_specs=[HBM] * n, out_specs=[HBM] * n,
        out_shape=[jax.ShapeDtypeStruct((N_CHIPS,) + s.shape, s.dtype) for s in shards],
        scratch_shapes=_gather_sems(n),
    )(*shards)


def _pair_sum(name, part):
    nch, rows, cols = part.shape
    half = rows // 2

    def body(p_hbm, p_ref, out_ref, landed, send, recv):
        j = pl.program_id(0)
        x, y, c = _place()

        def copy(k):
            theirs = pl.ds(pl.multiple_of((1 - c) * half, 16), half)
            return pltpu.make_async_remote_copy(
                src_ref=p_hbm.at[k, theirs], dst_ref=landed.at[k], send_sem=send.at[k], recv_sem=recv.at[k],
                device_id=(x, y, 1 - c), device_id_type=MESH)

        @pl.when(j == 0)
        def _():
            for k in range(nch):
                copy(k).start()

        copy(j).wait_recv()
        mine = p_ref[0, pl.ds(pl.multiple_of(c * half, 16), half), :]
        out_ref[0] = (mine.astype(F32) + landed[j].astype(F32)).astype(out_ref.dtype)

        @pl.when(j == nch - 1)
        def _():
            for k in range(nch):
                copy(k).wait_send()

    return pl.pallas_call(
        body, name=name, grid=(nch,),
        in_specs=[HBM, pl.BlockSpec((1, rows, cols), lambda j: (j, 0, 0))],
        out_specs=pl.BlockSpec((1, half, cols), lambda j: (j, 0, 0)),
        out_shape=jax.ShapeDtypeStruct((nch, half, cols), part.dtype),
        scratch_shapes=[pltpu.VMEM((nch, half, cols), part.dtype),
                        pltpu.SemaphoreType.DMA((nch,)), pltpu.SemaphoreType.DMA((nch,))],
        compiler_params=_params("arbitrary"),
    )(part, part)


def _reduce_sems(n):
    return [pltpu.SemaphoreType.DMA((3 * n,)), pltpu.SemaphoreType.DMA((3 * n,)), pltpu.SemaphoreType.DMA((n,)),
            pltpu.SemaphoreType.DMA((n,)), pltpu.SemaphoreType.DMA((n,))]


def _reduce_exchange(ins, got, sib, sems):
    n = len(ins)
    send, recv, local, d2d_send, d2d_recv = sems
    x, y, c = _place()
    me = 2 * x + y
    chips = _other_chips(x, y)

    def own(k):
        return pltpu.make_async_copy(ins[k].at[me], got[k].at[me], local.at[k])

    def copy(j, k, shard, slot):
        return pltpu.make_async_remote_copy(
            src_ref=ins[k].at[shard], dst_ref=got[k].at[slot], send_sem=send.at[j * n + k],
            recv_sem=recv.at[j * n + k], device_id=(chips[j][0], chips[j][1], c), device_id_type=MESH)

    def swap(k):
        return pltpu.make_async_remote_copy(
            src_ref=got[k], dst_ref=sib[k], send_sem=d2d_send.at[k], recv_sem=d2d_recv.at[k],
            device_id=(x, y, 1 - c), device_id_type=MESH)

    def start():
        for k in range(n):
            own(k).start()
        for j in range(3):
            for k in range(n):
                copy(j, k, 2 * chips[j][0] + chips[j][1], me).start()

    def forward():
        for k in range(n):
            own(k).wait()
            for j in range(3):
                copy(j, k, me, 2 * chips[j][0] + chips[j][1]).wait_recv()
            swap(k).start()

    def finish():
        for k in range(n):
            swap(k).wait_recv()
        for j in range(3):
            for k in range(n):
                copy(j, k, me, me).wait_send()
        for k in range(n):
            swap(k).wait_send()

    return start, forward, finish


def _reduce_shards(name, parts):
    n = len(parts)

    def body(*refs):
        for stage in _reduce_exchange(refs[:n], refs[n:2 * n], refs[2 * n:3 * n], refs[3 * n:]):
            stage()

    return pl.pallas_call(
        body, name=name,
        in_specs=[HBM] * n, out_specs=[HBM] * (2 * n),
        out_shape=[jax.ShapeDtypeStruct(s.shape, s.dtype) for s in parts] * 2,
        scratch_shapes=_reduce_sems(n),
    )(*parts)


def _all_reduce_scratch(vec):
    return [pltpu.VMEM((N_DEV,) + vec.shape, F32),
            pltpu.SemaphoreType.DMA((N_DEV - 1,)), pltpu.SemaphoreType.DMA((N_DEV - 1,))]


def _all_reduce_exchange(v_ref, o_ref, buf, send, recv):
    x, y, c = _place()
    me = 4 * x + 2 * y + c
    peers = [(x ^ (r >> 2), y ^ ((r >> 1) & 1), c ^ (r & 1)) for r in range(1, N_DEV)]

    def copy(r, slot):
        px, py, pc = peers[r]
        return pltpu.make_async_remote_copy(
            src_ref=v_ref, dst_ref=buf.at[slot], send_sem=send.at[r], recv_sem=recv.at[r],
            device_id=(px, py, pc), device_id_type=MESH)

    def start():
        for r in range(N_DEV - 1):
            copy(r, me).start()
        buf[me] = v_ref[...]

    def finish():
        for r in range(N_DEV - 1):
            px, py, pc = peers[r]
            copy(r, 4 * px + 2 * py + pc).wait_recv()
        total = buf[0]
        for dev in range(1, N_DEV):
            total = total + buf[dev]
        o_ref[...] = total
        for r in range(N_DEV - 1):
            copy(r, me).wait_send()

    return start, lambda: None, finish


def _adamw(w, g, m, v):
    m = ADAM_B1 * m + (1.0 - ADAM_B1) * g
    v = ADAM_B2 * v + (1.0 - ADAM_B2) * (g * g)
    m_hat = m / (1.0 - ADAM_B1 ** ADAM_STEP)
    v_hat = v / (1.0 - ADAM_B2 ** ADAM_STEP)
    return -ADAM_LR * (m_hat / (jnp.sqrt(v_hat) + ADAM_EPS) + ADAM_WD * w), m, v


def _adamw_shard(name, core, w, m, v, got, sib):
    rows, cols = w.shape
    tr = rows // 4
    spec = pl.BlockSpec((tr, cols), lambda i, c_ref: (i, 0))
    got_spec = pl.BlockSpec((N_CHIPS, tr, cols), lambda i, c_ref: (0, jnp.where(i // 2 == c_ref[0], i % 2, 0), 0))
    sib_spec = pl.BlockSpec((N_CHIPS, tr, cols), lambda i, c_ref: (0, jnp.where(i // 2 == c_ref[0], 0, i % 2), 0))

    def body(c_ref, w_ref, m_ref, v_ref, got_ref, sib_ref, g_ref, d_ref, mo_ref, vo_ref):
        def four(ref):
            return ((ref[0].astype(F32) + ref[1].astype(F32)) + ref[2].astype(F32)) + ref[3].astype(F32)

        g = jnp.where(pl.program_id(0) // 2 == c_ref[0], four(got_ref), four(sib_ref))
        g_ref[...] = g
        d_ref[...], mo_ref[...], vo_ref[...] = _adamw(w_ref[...], g, m_ref[...], v_ref[...])

    return pl.pallas_call(
        body, name=name,
        grid_spec=pltpu.PrefetchScalarGridSpec(
            num_scalar_prefetch=1, grid=(4,),
            in_specs=[spec, spec, spec, got_spec, sib_spec], out_specs=[spec] * 4),
        out_shape=[jax.ShapeDtypeStruct((rows, cols), F32)] * 4,
        compiler_params=_params("arbitrary"),
    )(core, w, m, v, got, sib)


def _adamw_small(name, w, m, v, g):
    def body(w_ref, m_ref, v_ref, g_ref, d_ref, mo_ref, vo_ref):
        d_ref[...], mo_ref[...], vo_ref[...] = _adamw(w_ref[...], g_ref[...], m_ref[...], v_ref[...])

    return pl.pallas_call(body, name=name, out_shape=[jax.ShapeDtypeStruct(w.shape, F32)] * 3)(w, m, v, g)


def _cast_bf16(name, arrays):
    n = len(arrays)

    def body(*refs):
        for i_ref, o_ref in zip(refs[:n], refs[n:]):
            o_ref[...] = i_ref[...].astype(BF16)

    return pl.pallas_call(
        body, name=name, out_shape=[jax.ShapeDtypeStruct(a.shape, BF16) for a in arrays],
        compiler_params=pltpu.CompilerParams(vmem_limit_bytes=VMEM_LIMIT),
    )(*arrays)


def _pack(arrays, rows):
    flat = jnp.concatenate([a.reshape(-1) for a in arrays])
    return jnp.concatenate([flat, jnp.zeros((rows * 128 - flat.shape[0],), F32)]).reshape(rows, 128)


def _unpack(packed, shapes):
    flat, out, at = packed.reshape(-1), [], 0
    for s in shapes:
        size = 1
        for dim in s:
            size *= dim
        out.append(flat[at:at + size].reshape(s))
        at += size
    return out


def _rows_for(shapes):
    total = 0
    for s in shapes:
        size = 1
        for dim in s:
            size *= dim
        total += size
    return -(-total // 1024) * 8


WEIGHTS = ['meta_tokens', 'ffn1_norm', 'ffn1_w_gate', 'ffn1_w_up', 'ffn1_w_down', 'mix_norm', 'w_in', 'rwkv_mu',
           'rwkv_w0', 'rwkv_w_up', 'rwkv_a0', 'rwkv_a_up', 'rwkv_g_up', 'rwkv_k_k', 'rwkv_k_a', 'rwkv_r_k',
           'rwkv_lnx_w', 'rwkv_lnx_b', 'w_out', 'ffn2_norm', 'ffn2_w_gate', 'ffn2_w_up', 'ffn2_w_down', 'final_norm']
COL_CUT = ['ffn1_w_gate', 'ffn1_w_up', 'w_in', 'ffn2_w_gate', 'ffn2_w_up']
ROW_CUT = ['ffn1_w_down', 'w_out', 'ffn2_w_down']
SMALL_CUT = ['meta_tokens', 'rwkv_w_up', 'rwkv_a_up', 'rwkv_g_up']
TRANSPOSED = ['ffn1_w_gate', 'ffn1_w_up', 'ffn2_w_gate', 'ffn2_w_up']
BIG = COL_CUT + ROW_CUT
REPLICATED = [n for n in WEIGHTS if n not in BIG + SMALL_CUT]


def _join_cols(a):
    return a.transpose(1, 0, 2).reshape(a.shape[1], N_CHIPS * a.shape[2])


def _cut_cols(a):
    return a.reshape(a.shape[0], N_CHIPS, a.shape[1] // N_CHIPS).transpose(1, 0, 2)


def _step(x, loss_target, w, m, v):
    two = lambda a: a.reshape(a.shape[-2], a.shape[-1])

    def rows_cut(n, a):
        return jnp.swapaxes(two(a), 0, 1) if n in TRANSPOSED else two(a)

    def as_given(n, a, like):
        return (jnp.swapaxes(a, 0, 1) if n in TRANSPOSED else a).reshape(like.shape)

    col_cut = [n for n in COL_CUT + SMALL_CUT if n not in TRANSPOSED]

    def join(names, gathered):
        return {n: (_join_cols(a) if n in col_cut else a.reshape(-1, a.shape[-1])) for n, a in zip(names, gathered)}

    def pair_sums(names, g):
        parts = [_cut_cols(g[n]) if n in col_cut else g[n].reshape(N_CHIPS, -1, g[n].shape[-1]) for n in names]
        return [_pair_sum("pair_sum_" + n, p) for n, p in zip(names, parts)]

    first = [n for n in BIG if n not in MID + LATE]
    gathered_later = {"mid": MID, "late": LATE}
    groups = {**gathered_later, "gate": ["ffn1_w_gate"], "up": ["ffn1_w_up"]}
    cast = dict(zip(BIG, _cast_bf16("cast_weights", [rows_cut(n, w[n]) for n in BIG])))
    names = first + SMALL_CUT
    shards = [cast[n] for n in first] + [two(w[n]) for n in SMALL_CUT]
    full = {n: (two(w[n]) if w[n].ndim == 3 else w[n]) for n in REPLICATED}
    full.update(join(names, _gather_shards("gather_weights", shards)))
    full["rwkv_r_k"] = w["rwkv_r_k"]
    full["final_norm"] = w["final_norm"]

    small_names = REPLICATED + SMALL_CUT

    def small(g, loss):
        arrays = [loss.reshape(1)] + [g[n] for n in small_names]
        return _pack(arrays, _rows_for([a.shape for a in arrays]))

    late = types.SimpleNamespace(shards={k: [cast[n] for n in names] for k, names in gathered_later.items()},
                                 join=lambda k, gathered: join(groups[k], gathered),
                                 parts=lambda k, g: pair_sums(groups[k], g), small=small)

    _, dx, g, reduced = _local_step(x[0], loss_target[0], full, late)

    groups["down"] = ["ffn1_w_down"]
    reduced["down"] = list(_reduce_shards("reduce_gradients", pair_sums(groups["down"], g)))
    got, sib = {}, {}
    for k, names in groups.items():
        got.update(zip(names, reduced[k][:len(names)]))
        sib.update(zip(names, reduced[k][len(names):]))
    loss, *summed = _unpack(reduced["small"], [(1,)] + [g[n].shape for n in small_names])
    loss = loss.reshape(())
    g_small = dict(zip(small_names, summed))
    chip = 2 * lax.axis_index("x") + lax.axis_index("y")
    for n in SMALL_CUT:
        width = g_small[n].shape[1] // N_CHIPS
        g_small[n] = lax.dynamic_slice_in_dim(g_small[n], chip * width, width, axis=1)

    grad, delta, new_m, new_v = {}, {}, {}, {}
    core = lax.axis_index("c").astype(jnp.int32).reshape(1)
    for n in BIG:
        outs = _adamw_shard("adamw_" + n, core, rows_cut(n, w[n]), rows_cut(n, m[n]), rows_cut(n, v[n]), got[n], sib[n])
        grad[n], delta[n], new_m[n], new_v[n] = (as_given(n, o, w[n]) for o in outs)
    shapes = [w[n].shape for n in small_names]
    rows = _rows_for(shapes)
    packed = [_pack([t[n] for n in small_names], rows) for t in (w, m, v)]
    g_packed = _pack([g_small[n] for n in small_names], rows)
    outs = [_unpack(o, shapes) for o in _adamw_small("adamw_small", *packed, g_packed)]
    for i, n in enumerate(small_names):
        grad[n] = g_small[n].reshape(w[n].shape)
        delta[n], new_m[n], new_v[n] = outs[0][i], outs[1][i], outs[2][i]
    return loss, dx[None], grad, delta, new_m, new_v


def kernel(x, meta_tokens, ffn1_norm, ffn1_w_gate, ffn1_w_up, ffn1_w_down, mix_norm, w_in, rwkv_mu, rwkv_w0, rwkv_w_up, rwkv_a0, rwkv_a_up, rwkv_g_up, rwkv_k_k, rwkv_k_a, rwkv_r_k, rwkv_lnx_w, rwkv_lnx_b, w_out, ffn2_norm, ffn2_w_gate, ffn2_w_up, ffn2_w_down, final_norm, loss_target, m_meta_tokens, m_ffn1_norm, m_ffn1_w_gate, m_ffn1_w_up, m_ffn1_w_down, m_mix_norm, m_w_in, m_rwkv_mu, m_rwkv_w0, m_rwkv_w_up, m_rwkv_a0, m_rwkv_a_up, m_rwkv_g_up, m_rwkv_k_k, m_rwkv_k_a, m_rwkv_r_k, m_rwkv_lnx_w, m_rwkv_lnx_b, m_w_out, m_ffn2_norm, m_ffn2_w_gate, m_ffn2_w_up, m_ffn2_w_down, m_final_norm, v_meta_tokens, v_ffn1_norm, v_ffn1_w_gate, v_ffn1_w_up, v_ffn1_w_down, v_mix_norm, v_w_in, v_rwkv_mu, v_rwkv_w0, v_rwkv_w_up, v_rwkv_a0, v_rwkv_a_up, v_rwkv_g_up, v_rwkv_k_k, v_rwkv_k_a, v_rwkv_r_k, v_rwkv_lnx_w, v_rwkv_lnx_b, v_w_out, v_ffn2_norm, v_ffn2_w_gate, v_ffn2_w_up, v_ffn2_w_down, v_final_norm):
    w = dict(zip(WEIGHTS, (meta_tokens, ffn1_norm, ffn1_w_gate, ffn1_w_up, ffn1_w_down, mix_norm, w_in, rwkv_mu, rwkv_w0, rwkv_w_up, rwkv_a0, rwkv_a_up, rwkv_g_up, rwkv_k_k, rwkv_k_a, rwkv_r_k, rwkv_lnx_w, rwkv_lnx_b, w_out, ffn2_norm, ffn2_w_gate, ffn2_w_up, ffn2_w_down, final_norm)))
    m = dict(zip(WEIGHTS, (m_meta_tokens, m_ffn1_norm, m_ffn1_w_gate, m_ffn1_w_up, m_ffn1_w_down, m_mix_norm, m_w_in, m_rwkv_mu, m_rwkv_w0, m_rwkv_w_up, m_rwkv_a0, m_rwkv_a_up, m_rwkv_g_up, m_rwkv_k_k, m_rwkv_k_a, m_rwkv_r_k, m_rwkv_lnx_w, m_rwkv_lnx_b, m_w_out, m_ffn2_norm, m_ffn2_w_gate, m_ffn2_w_up, m_ffn2_w_down, m_final_norm)))
    v = dict(zip(WEIGHTS, (v_meta_tokens, v_ffn1_norm, v_ffn1_w_gate, v_ffn1_w_up, v_ffn1_w_down, v_mix_norm, v_w_in, v_rwkv_mu, v_rwkv_w0, v_rwkv_w_up, v_rwkv_a0, v_rwkv_a_up, v_rwkv_g_up, v_rwkv_k_k, v_rwkv_k_a, v_rwkv_r_k, v_rwkv_lnx_w, v_rwkv_lnx_b, v_w_out, v_ffn2_norm, v_ffn2_w_gate, v_ffn2_w_up, v_ffn2_w_down, v_final_norm)))
    loss, grad_x, grad, delta, new_m, new_v = _step(x, loss_target, w, m, v)
    return (loss, grad_x, *[grad[n] for n in WEIGHTS], *[delta[n] for n in WEIGHTS],
            *[new_m[n] for n in WEIGHTS], *[new_v[n] for n in WEIGHTS])
```

```python
import functools
import types

import jax
import jax.numpy as jnp
from jax import lax
from jax.experimental import pallas as pl
from jax.experimental.pallas import tpu as pltpu

F32 = jnp.float32
BF16 = jnp.bfloat16

RMS_EPS = 1e-6
LNX_EPS = 64e-5
N_META = 16
ROW0 = 128
META_PAD = ROW0 - N_META
HEAD = 64
N_HEADS = 8
GROUP = N_HEADS * HEAD
LORA_W, LORA_A, LORA_G = 32, 32, 96
LORA_PAD = 256
RW_COLS = 3 * GROUP + LORA_PAD
IN_COLS_PAD = 3 * GROUP + RW_COLS
ATT_BLOCK = 128
CHUNK = 64
SCAN_CHUNKS = 2
VMEM_LIMIT = 56 * 1024 * 1024

ADAM_LR, ADAM_B1, ADAM_B2, ADAM_EPS, ADAM_WD, ADAM_STEP = 0.001, 0.9, 0.999, 1e-08, 0.01, 10

MESH = pl.DeviceIdType.MESH


def _params(*sem):
    return pltpu.CompilerParams(dimension_semantics=tuple(sem), vmem_limit_bytes=VMEM_LIMIT)


def _dot(a, b):
    return lax.dot_general(a, b, (((1,), (0,)), ((), ())), preferred_element_type=F32)


def _dot_nt(a, b):
    return lax.dot_general(a, b, (((1,), (1,)), ((), ())), preferred_element_type=F32)


def _dot_tn(a, b):
    return lax.dot_general(a, b, (((0,), (0,)), ((), ())), preferred_element_type=F32)


def _split2(x):
    hi = x.astype(BF16)
    return hi, (x - hi.astype(F32)).astype(BF16)


def _sigmoid(x):
    return 1.0 / (1.0 + jnp.exp(-x))


def _rms_fwd(x, g):
    rstd = lax.rsqrt(jnp.mean(x * x, axis=-1, keepdims=True) + RMS_EPS)
    xhat = x * rstd
    return xhat * g, xhat, rstd


def _rms_bwd(dn, xhat, rstd, g):
    dxhat = dn * g
    dx = rstd * (dxhat - xhat * jnp.mean(dxhat * xhat, axis=-1, keepdims=True))
    return dx, jnp.sum(dn * xhat, axis=0, keepdims=True)


def _row_tile(rows):
    return 384 if rows % 384 == 0 else 128


def _half_tile(cols):
    return cols // 2 if cols % 256 == 0 else cols


def _tall_tile(rows, parts):
    return rows // parts if rows % (16 * parts) == 0 else _row_tile(rows)


def _call_with_exchange(name, body, grid, in_specs, out_specs, out_shape, scratch, operands, params, exchange):
    if exchange is None or not exchange[1]:
        return pl.pallas_call(body, name=name, grid=grid, in_specs=in_specs, out_specs=out_specs,
                              out_shape=out_shape, scratch_shapes=scratch, compiler_params=params)(*operands)
    kind, arrays = exchange
    ns, n_in, n_out, n_scr = len(arrays), len(in_specs), len(out_specs), len(scratch)
    whole = lambda a: pl.BlockSpec(a.shape, lambda *_: (0,) * a.ndim)
    if kind == "gather":
        results = [jax.ShapeDtypeStruct((N_CHIPS,) + s.shape, s.dtype) for s in arrays]
        sems, sent_specs, landed_specs = _gather_sems(ns), [HBM] * ns, [HBM] * ns
    elif kind == "reduce":
        results = [jax.ShapeDtypeStruct(s.shape, s.dtype) for s in arrays] * 2
        sems, sent_specs, landed_specs = _reduce_sems(ns), [HBM] * ns, [HBM] * (2 * ns)
    else:
        results = [jax.ShapeDtypeStruct(arrays[0].shape, F32)]
        sems, sent_specs, landed_specs = _all_reduce_scratch(arrays[0]), [whole(arrays[0])], [whole(arrays[0])]
    n_res = len(results)

    def carried(*refs):
        at = n_in + ns + n_out
        sent, landed = refs[n_in:n_in + ns], refs[at:at + n_res]
        own_scratch, sem_refs = refs[at + n_res:at + n_res + n_scr], refs[at + n_res + n_scr:]
        first, before_last, last = _exchange_steps(grid)
        if kind == "gather":
            start, forward, finish = _gather_exchange(sent, landed, sem_refs)
        elif kind == "reduce":
            start, forward, finish = _reduce_exchange(sent, landed[:ns], landed[ns:], sem_refs)
        else:
            start, forward, finish = _all_reduce_exchange(sent[0], landed[0], *sem_refs)
        pl.when(first)(start)
        body(*refs[:n_in], *refs[n_in + ns:at], *own_scratch)
        pl.when(before_last)(forward)
        pl.when(last)(finish)

    return pl.pallas_call(
        carried, name=name, grid=grid, in_specs=list(in_specs) + sent_specs, out_specs=list(out_specs) + landed_specs,
        out_shape=list(out_shape) + results, scratch_shapes=list(scratch) + sems, compiler_params=params,
    )(*operands, *arrays)


def _ffn_fwd(name, h, g, wg, wu, wd, exchange=None):
    rows, d = h.shape
    f = wg.shape[0]
    tm, tf = _row_tile(rows), _half_tile(f)
    nj = f // tf

    def body(h_ref, g_ref, wg_ref, wu_ref, wd_ref, ho_ref, a_ref, b_ref, n_sc, acc_sc):
        j = pl.program_id(1)

        @pl.when(j == 0)
        def _():
            n, _, _ = _rms_fwd(h_ref[...], g_ref[...])
            n_sc[...] = n.astype(BF16)
            acc_sc[...] = jnp.zeros_like(acc_sc)

        n = n_sc[...]
        a = _dot_nt(n, wg_ref[...])
        b = _dot_nt(n, wu_ref[...])
        a_ref[...] = a
        b_ref[...] = b
        s = a * _sigmoid(a) * b
        acc_sc[...] += _dot(s.astype(BF16), wd_ref[...])

        @pl.when(j == nj - 1)
        def _():
            ho_ref[...] = h_ref[...] + 0.5 * acc_sc[...]

    return _call_with_exchange(
        name, body, (rows // tm, nj),
        [pl.BlockSpec((tm, d), lambda i, j: (i, 0)),
         pl.BlockSpec((1, d), lambda i, j: (0, 0)),
         pl.BlockSpec((tf, d), lambda i, j: (j, 0)),
         pl.BlockSpec((tf, d), lambda i, j: (j, 0)),
         pl.BlockSpec((tf, d), lambda i, j: (j, 0))],
        [pl.BlockSpec((tm, d), lambda i, j: (i, 0)),
         pl.BlockSpec((tm, tf), lambda i, j: (i, j)),
         pl.BlockSpec((tm, tf), lambda i, j: (i, j))],
        [jax.ShapeDtypeStruct((rows, d), F32),
         jax.ShapeDtypeStruct((rows, f), F32),
         jax.ShapeDtypeStruct((rows, f), F32)],
        [pltpu.VMEM((tm, d), BF16), pltpu.VMEM((tm, d), F32)],
        (h, g, wg, wu, wd), _params("arbitrary", "arbitrary"), exchange)


def _ffn_bwd(name, dh, h, g, a, b, wg, wu, wd, exchange=None):
    rows, d = h.shape
    f = wg.shape[0]
    tm, tf = _row_tile(rows), _half_tile(f)
    ni, nj = rows // tm, f // tf

    def body(dh_ref, h_ref, g_ref, a_ref, b_ref, wg_ref, wu_ref, wd_ref,
             dhi_ref, da_ref, db_ref, s_ref, n_ref, dhh_ref, dg_ref, dn_sc):
        i, j = pl.program_id(0), pl.program_id(1)

        @pl.when(j == 0)
        def _():
            n, _, _ = _rms_fwd(h_ref[...], g_ref[...])
            n_ref[...] = n.astype(BF16)
            dhh_ref[...] = (0.5 * dh_ref[...]).astype(BF16)
            dn_sc[...] = jnp.zeros_like(dn_sc)

        @pl.when((i == 0) & (j == 0))
        def _():
            dg_ref[...] = jnp.zeros_like(dg_ref)

        ds = _dot_nt(dhh_ref[...], wd_ref[...])
        av, bv = a_ref[...], b_ref[...]
        sig = _sigmoid(av)
        silu = av * sig
        s_ref[...] = (silu * bv).astype(BF16)
        db = (ds * silu).astype(BF16)
        da = (ds * bv * (sig * (1.0 + av * (1.0 - sig)))).astype(BF16)
        da_ref[...] = da
        db_ref[...] = db
        dn_sc[...] += _dot(da, wg_ref[...]) + _dot(db, wu_ref[...])

        @pl.when(j == nj - 1)
        def _():
            gv = g_ref[...]
            _, xhat, rstd = _rms_fwd(h_ref[...], gv)
            dx, dg = _rms_bwd(dn_sc[...], xhat, rstd, gv)
            dhi_ref[...] = dh_ref[...] + dx
            dg_ref[...] += dg

    return _call_with_exchange(
        name, body, (ni, nj),
        [pl.BlockSpec((tm, d), lambda i, j: (i, 0)),
         pl.BlockSpec((tm, d), lambda i, j: (i, 0)),
         pl.BlockSpec((1, d), lambda i, j: (0, 0)),
         pl.BlockSpec((tm, tf), lambda i, j: (i, j)),
         pl.BlockSpec((tm, tf), lambda i, j: (i, j)),
         pl.BlockSpec((tf, d), lambda i, j: (j, 0)),
         pl.BlockSpec((tf, d), lambda i, j: (j, 0)),
         pl.BlockSpec((tf, d), lambda i, j: (j, 0))],
        [pl.BlockSpec((tm, d), lambda i, j: (i, 0)),
         pl.BlockSpec((tm, tf), lambda i, j: (i, j)),
         pl.BlockSpec((tm, tf), lambda i, j: (i, j)),
         pl.BlockSpec((tm, tf), lambda i, j: (i, j)),
         pl.BlockSpec((tm, d), lambda i, j: (i, 0)),
         pl.BlockSpec((tm, d), lambda i, j: (i, 0)),
         pl.BlockSpec((1, d), lambda i, j: (0, 0))],
        [jax.ShapeDtypeStruct((rows, d), F32),
         jax.ShapeDtypeStruct((rows, f), BF16),
         jax.ShapeDtypeStruct((rows, f), BF16),
         jax.ShapeDtypeStruct((rows, f), BF16),
         jax.ShapeDtypeStruct((rows, d), BF16),
         jax.ShapeDtypeStruct((rows, d), BF16),
         jax.ShapeDtypeStruct((1, d), F32)],
        [pltpu.VMEM((tm, d), F32)],
        (dh, h, g, a, b, wg, wu, wd), _params("arbitrary", "arbitrary"), exchange)


def _mm_tn(name, a, b, exchange=None):
    k, m = a.shape
    n = b.shape[1]
    tk = _tall_tile(k, 3)
    tm = _half_tile(m) if m > 1024 else m
    tn = _half_tile(n) if n > 1024 else n
    nk = k // tk

    def body(a_ref, b_ref, o_ref, acc):
        kk = pl.program_id(2)

        @pl.when(kk == 0)
        def _():
            acc[...] = jnp.zeros_like(acc)

        acc[...] += _dot_tn(a_ref[...], b_ref[...])

        @pl.when(kk == nk - 1)
        def _():
            o_ref[...] = acc[...].astype(BF16)

    outs = _call_with_exchange(
        name, body, (m // tm, n // tn, nk),
        [pl.BlockSpec((tk, tm), lambda i, j, kk: (kk, i)),
         pl.BlockSpec((tk, tn), lambda i, j, kk: (kk, j))],
        [pl.BlockSpec((tm, tn), lambda i, j, kk: (i, j))],
        [jax.ShapeDtypeStruct((m, n), BF16)],
        [pltpu.VMEM((tm, tn), F32)],
        (a, b), _params("arbitrary", "arbitrary", "arbitrary"), exchange)
    return outs if exchange else outs[0]


def _norm_proj(name, h, g, w):
    rows, d = h.shape
    n = w.shape[1]
    split = 3 * GROUP
    tm = _row_tile(rows)

    def body(h_ref, g_ref, w_ref, qkv_ref, p_ref, n_ref):
        nv, _, _ = _rms_fwd(h_ref[...], g_ref[...])
        nb = nv.astype(BF16)
        n_ref[...] = nb
        qkv_ref[...] = _dot(nb, w_ref[:, :split]).astype(BF16)
        p_ref[...] = _dot(nb, w_ref[:, split:])

    return pl.pallas_call(
        body, name=name, grid=(rows // tm,),
        in_specs=[pl.BlockSpec((tm, d), lambda i: (i, 0)),
                  pl.BlockSpec((1, d), lambda i: (0, 0)),
                  pl.BlockSpec((d, n), lambda i: (0, 0))],
        out_specs=[pl.BlockSpec((tm, split), lambda i: (i, 0)),
                   pl.BlockSpec((tm, n - split), lambda i: (i, 0)),
                   pl.BlockSpec((tm, d), lambda i: (i, 0))],
        out_shape=[jax.ShapeDtypeStruct((rows, split), BF16), jax.ShapeDtypeStruct((rows, n - split), F32),
                   jax.ShapeDtypeStruct((rows, d), BF16)],
        compiler_params=_params("arbitrary"),
    )(h, g, w)


def _out_proj(name, h, sb, rw, w):
    rows, d = h.shape
    gw = sb.shape[1]
    tm = _row_tile(rows)

    def body(h_ref, sb_ref, rw_ref, w_ref, o_ref, mix_ref):
        mix_ref[:, :gw] = sb_ref[...].astype(BF16)
        mix_ref[:, gw:] = rw_ref[...].astype(BF16)
        o_ref[...] = h_ref[...] + _dot(mix_ref[...], w_ref[...])

    return pl.pallas_call(
        body, name=name, grid=(rows // tm,),
        in_specs=[pl.BlockSpec((tm, d), lambda i: (i, 0)),
                  pl.BlockSpec((tm, gw), lambda i: (i, 0)),
                  pl.BlockSpec((tm, gw), lambda i: (i, 0)),
                  pl.BlockSpec((2 * gw, d), lambda i: (0, 0))],
        out_specs=[pl.BlockSpec((tm, d), lambda i: (i, 0)),
                   pl.BlockSpec((tm, 2 * gw), lambda i: (i, 0))],
        out_shape=[jax.ShapeDtypeStruct((rows, d), F32), jax.ShapeDtypeStruct((rows, 2 * gw), BF16)],
        compiler_params=_params("arbitrary"),
    )(h, sb, rw, w)


def _out_proj_bwd(name, dh, w):
    rows, d = dh.shape
    k = w.shape[0]
    tm = _row_tile(rows)

    def body(dh_ref, w_ref, dsb_ref, drw_ref, dhb_ref):
        dhb = dh_ref[...].astype(BF16)
        dhb_ref[...] = dhb
        dsb_ref[...] = _dot_nt(dhb, w_ref[:GROUP, :]).astype(BF16)
        drw_ref[...] = _dot_nt(dhb, w_ref[GROUP:, :])

    return pl.pallas_call(
        body, name=name, grid=(rows // tm,),
        in_specs=[pl.BlockSpec((tm, d), lambda i: (i, 0)),
                  pl.BlockSpec((k, d), lambda i: (0, 0))],
        out_specs=[pl.BlockSpec((tm, GROUP), lambda i: (i, 0)),
                   pl.BlockSpec((tm, GROUP), lambda i: (i, 0)),
                   pl.BlockSpec((tm, d), lambda i: (i, 0))],
        out_shape=[jax.ShapeDtypeStruct((rows, GROUP), BF16), jax.ShapeDtypeStruct((rows, GROUP), F32),
                   jax.ShapeDtypeStruct((rows, d), BF16)],
        compiler_params=_params("arbitrary"),
    )(dh, w)


def _norm_proj_bwd(name, dproj, w, h, g, dh):
    rows, n = dproj.shape
    d = w.shape[0]
    tm = _row_tile(rows)

    def body(dp_ref, w_ref, h_ref, g_ref, dh_ref, o_ref, dg_ref):
        @pl.when(pl.program_id(0) == 0)
        def _():
            dg_ref[...] = jnp.zeros_like(dg_ref)

        dn = _dot_nt(dp_ref[...], w_ref[...])
        gv = g_ref[...]
        _, xhat, rstd = _rms_fwd(h_ref[...], gv)
        dx, dg = _rms_bwd(dn, xhat, rstd, gv)
        o_ref[...] = dh_ref[...] + dx
        dg_ref[...] += dg

    return pl.pallas_call(
        body, name=name, grid=(rows // tm,),
        in_specs=[pl.BlockSpec((tm, n), lambda i: (i, 0)),
                  pl.BlockSpec((d, n), lambda i: (0, 0)),
                  pl.BlockSpec((tm, d), lambda i: (i, 0)),
                  pl.BlockSpec((1, d), lambda i: (0, 0)),
                  pl.BlockSpec((tm, d), lambda i: (i, 0))],
        out_specs=[pl.BlockSpec((tm, d), lambda i: (i, 0)),
                   pl.BlockSpec((1, d), lambda i: (0, 0))],
        out_shape=[jax.ShapeDtypeStruct((rows, d), F32), jax.ShapeDtypeStruct((1, d), F32)],
        compiler_params=_params("arbitrary"),
    )(dproj, w, h, g, dh)


def _loss_head(name, h, g, tgt):
    rows, d = h.shape
    tm = _row_tile(rows)

    def body(h_ref, g_ref, t_ref, loss_ref, dh_ref, dg_ref):
        i = pl.program_id(0)

        @pl.when(i == 0)
        def _():
            loss_ref[...] = jnp.zeros_like(loss_ref)
            dg_ref[...] = jnp.zeros_like(dg_ref)

        gv = g_ref[...]
        y, xhat, rstd = _rms_fwd(h_ref[...], gv)
        row = i * tm + lax.broadcasted_iota(jnp.int32, (tm, 1), 0)
        diff = jnp.where(row >= ROW0, y - t_ref[...], 0.0)
        part = 0.5 * jnp.sum(jnp.sum(diff * diff, axis=-1, keepdims=True), axis=0, keepdims=True) / d
        loss_ref[...] += jnp.broadcast_to(part, loss_ref.shape)
        dx, dg = _rms_bwd(diff / d, xhat, rstd, gv)
        dh_ref[...] = dx
        dg_ref[...] += dg

    return pl.pallas_call(
        body, name=name, grid=(rows // tm,),
        in_specs=[pl.BlockSpec((tm, d), lambda i: (i, 0)),
                  pl.BlockSpec((1, d), lambda i: (0, 0)),
                  pl.BlockSpec((tm, d), lambda i: (i, 0))],
        out_specs=[pl.BlockSpec((8, 128), lambda i: (0, 0)),
                   pl.BlockSpec((tm, d), lambda i: (i, 0)),
                   pl.BlockSpec((1, d), lambda i: (0, 0))],
        out_shape=[jax.ShapeDtypeStruct((8, 128), F32),
                   jax.ShapeDtypeStruct((rows, d), F32),
                   jax.ShapeDtypeStruct((1, d), F32)],
        compiler_params=_params("arbitrary"),
    )(h, g, tgt)


ATT_SCALE = HEAD ** -0.5


def _sb_block(qs, kb, q0, jb, diagonal):
    bq, bk = qs.shape[0], kb.shape[0]
    z = _dot_nt(qs, kb)
    if diagonal:
        qpos = q0 + lax.broadcasted_iota(jnp.int32, (bq, bk), 0)
        kpos = jb * bk + lax.broadcasted_iota(jnp.int32, (bq, bk), 1)
        valid = (kpos < qpos) & (kpos >= META_PAD)
    else:
        valid = jb * bk + lax.broadcasted_iota(jnp.int32, (1, bk), 1) >= META_PAD
    e = jnp.exp(-jnp.abs(z))
    log_keep = jnp.where(valid, -(jnp.maximum(z, 0.0) + jnp.log(1.0 + e)), 0.0)
    return z, valid, e, log_keep


def _tri2(n, cmp):
    r = lax.broadcasted_iota(jnp.int32, (2 * n, n), 0) % n
    c = lax.broadcasted_iota(jnp.int32, (2 * n, n), 1)
    return cmp(r, c).astype(BF16)


def _dot_split(x, t2):
    hi, lo = _split2(x)
    return _dot(jnp.concatenate([hi, lo], axis=1), t2)


ATT_HEADS = 8
ATT_WIDTH = ATT_HEADS * HEAD
ATT_CUT = -104.0
ATT_TILES = GROUP // ATT_WIDTH


def _lanes(hh):
    return slice(hh * HEAD, (hh + 1) * HEAD)


def _exchange_steps(grid):
    step, total = pl.program_id(0), 1
    for a in range(1, len(grid)):
        step = step * grid[a] + pl.program_id(a)
    for size in grid:
        total *= size
    return step == 0, step == max(total - 2, 0), step == total - 1


def _sb_fwd(name, qkv, shards=()):
    rows = qkv.shape[0]
    nh, dh = N_HEADS, HEAD
    bq, bk, hg = _row_tile(rows), ATT_BLOCK, ATT_HEADS
    per = bq // bk
    ns = len(shards)
    grid = (nh // hg, rows // bq)

    def body(q_ref, k_ref, v_ref, *rest):
        o_ref, rt_ref, cnt_ref = rest[ns:ns + 3]
        if ns:
            first, before_last, last = _exchange_steps(grid)
            start, forward, finish = _gather_exchange(rest[:ns], rest[ns + 3:2 * ns + 3], rest[2 * ns + 3:])
            pl.when(first)(start)
        i = pl.program_id(1)
        after = _tri2(bk, lambda r, c: r > c)
        nkb = (i + 1) * per

        def live(state):
            n, carry = state
            top = jnp.max(carry[0][0])
            for hh in range(1, hg):
                top = jnp.maximum(top, jnp.max(carry[hh][0]))
            return (n < nkb) & (top >= ATT_CUT)

        def visit(carry, jb, r0, diagonal):
            off = pl.multiple_of(jb * bk, bk)
            out = []
            for hh in range(hg):
                rest, acc = carry[hh]
                kb = k_ref[pl.ds(off, bk), _lanes(hh)]
                vb = v_ref[pl.ds(off, bk), _lanes(hh)]
                qs = q_ref[r0:, _lanes(hh)] * ATT_SCALE
                z, valid, _, log_keep = _sb_block(qs, kb, i * bq + r0, jb, diagonal)
                log_rest = rest[r0:] + _dot_split(log_keep, after)
                attn = jnp.where(valid, jnp.exp(z + log_keep + log_rest), 0.0)
                new_rest = rest[r0:] + jnp.sum(log_keep, axis=-1, keepdims=True)
                new_acc = acc[r0:] + _dot(attn.astype(BF16), vb)
                if r0:
                    new_rest = jnp.concatenate([rest[:r0], new_rest], axis=0)
                    new_acc = jnp.concatenate([acc[:r0], new_acc], axis=0)
                out.append((new_rest, new_acc))
            return tuple(out)

        carry = tuple((jnp.zeros((bq, 1), F32), jnp.zeros((bq, dh), F32)) for _ in range(hg))
        for dgl in reversed(range(per)):
            carry = visit(carry, i * per + dgl, dgl * bk, True)
        n, res = lax.while_loop(live, lambda s: (s[0] + 1, visit(s[1], nkb - 1 - s[0], 0, False)),
                                (jnp.int32(per), carry))
        for hh in range(hg):
            rt_ref[hh] = res[hh][0]
            o_ref[:, _lanes(hh)] = res[hh][1]
            cnt_ref[hh] = jnp.full((bq, 1), n, F32)
        if ns:
            pl.when(before_last)(forward)
            pl.when(last)(finish)

    return pl.pallas_call(
        body, name=name, grid=grid,
        in_specs=[pl.BlockSpec((bq, ATT_WIDTH), lambda h, i: (i, h)),
                  pl.BlockSpec((rows, ATT_WIDTH), lambda h, i: (0, ATT_TILES + h)),
                  pl.BlockSpec((rows, ATT_WIDTH), lambda h, i: (0, 2 * ATT_TILES + h))] + [HBM] * ns,
        out_specs=[pl.BlockSpec((bq, ATT_WIDTH), lambda h, i: (i, h)),
                   pl.BlockSpec((hg, bq, 1), lambda h, i: (h, i, 0)),
                   pl.BlockSpec((hg, bq, 1), lambda h, i: (h, i, 0))] + [HBM] * ns,
        out_shape=[jax.ShapeDtypeStruct((rows, GROUP), F32), jax.ShapeDtypeStruct((nh, rows, 1), F32),
                   jax.ShapeDtypeStruct((nh, rows, 1), F32)]
        + [jax.ShapeDtypeStruct((N_CHIPS,) + s.shape, s.dtype) for s in shards],
        scratch_shapes=_gather_sems(ns) if ns else [],
        compiler_params=_params("arbitrary", "arbitrary"),
    )(qkv, qkv, qkv, *shards)


def _sb_bwd(name, qkv, rt, cnt, do, parts=()):
    rows = qkv.shape[0]
    nh, dh = N_HEADS, HEAD
    bq, bk, hg = _row_tile(rows), ATT_BLOCK, ATT_HEADS
    per = bq // bk
    ns = len(parts)
    grid = (nh // hg, rows // bq)

    def body(q_ref, k_ref, v_ref, rt_ref, cnt_ref, do_ref, *rest):
        dq_ref, dk_ref, dv_ref = rest[ns:ns + 3]
        if ns:
            at_first, before_last, at_last = _exchange_steps(grid)
            start, forward, finish = _reduce_exchange(rest[:ns], rest[ns + 3:2 * ns + 3], rest[2 * ns + 3:3 * ns + 3],
                                             rest[3 * ns + 3:])
            pl.when(at_first)(start)
        i = pl.program_id(1)

        @pl.when(i == 0)
        def _():
            dk_ref[...] = jnp.zeros_like(dk_ref)
            dv_ref[...] = jnp.zeros_like(dv_ref)

        upto = _tri2(bk, lambda r, c: r <= c)
        before = _tri2(bk, lambda r, c: r < c)
        nkb = (i + 1) * per
        first = nkb - jnp.max(cnt_ref[0]).astype(jnp.int32)

        def visit(carry, jb, r0, diagonal):
            off = pl.multiple_of(jb * bk, bk)
            out = []
            for hh in range(hg):
                keep_sum, g_sum, dq = carry[hh]
                qb, dob = q_ref[r0:, _lanes(hh)] * ATT_SCALE, do_ref[r0:, _lanes(hh)]
                kb = k_ref[pl.ds(off, bk), _lanes(hh)]
                vb = v_ref[pl.ds(off, bk), _lanes(hh)]
                z, valid, e, log_keep = _sb_block(qb, kb, i * bq + r0, jb, diagonal)
                log_rest = rt_ref[hh, r0:, :] - keep_sum[r0:] - _dot_split(log_keep, upto)
                attn = jnp.where(valid, jnp.exp(z + log_keep + log_rest), 0.0)
                g = attn * _dot_nt(dob, vb)
                g_before = g_sum[r0:] + _dot_split(g, before)
                inv = 1.0 / (1.0 + e)
                sig = jnp.where(z >= 0, inv, e * inv)
                dz = jnp.where(valid, g * (1.0 - sig) - g_before * sig, 0.0).astype(BF16)
                dk_ref[pl.ds(off, bk), _lanes(hh)] += _dot_tn(dz, qb)
                dv_ref[pl.ds(off, bk), _lanes(hh)] += _dot_tn(attn.astype(BF16), dob)
                new = (keep_sum[r0:] + jnp.sum(log_keep, axis=-1, keepdims=True),
                       g_sum[r0:] + jnp.sum(g, axis=-1, keepdims=True),
                       dq[r0:] + _dot(dz, kb))
                if r0:
                    new = tuple(jnp.concatenate([old[:r0], x], axis=0) for old, x in zip(carry[hh], new))
                out.append(new)
            return tuple(out)

        zero = jnp.zeros((bq, 1), F32)
        res = lax.fori_loop(first, nkb - per, lambda jb, c: visit(c, jb, 0, False),
                            tuple((zero, zero, jnp.zeros((bq, dh), F32)) for _ in range(hg)))
        for dgl in range(per):
            res = visit(res, i * per + dgl, dgl * bk, True)
        for hh in range(hg):
            dq_ref[:, _lanes(hh)] = res[hh][2] * ATT_SCALE
        if ns:
            pl.when(before_last)(forward)
            pl.when(at_last)(finish)

    return pl.pallas_call(
        body, name=name, grid=grid,
        in_specs=[pl.BlockSpec((bq, ATT_WIDTH), lambda h, i: (i, h)),
                  pl.BlockSpec((rows, ATT_WIDTH), lambda h, i: (0, ATT_TILES + h)),
                  pl.BlockSpec((rows, ATT_WIDTH), lambda h, i: (0, 2 * ATT_TILES + h)),
                  pl.BlockSpec((hg, bq, 1), lambda h, i: (h, i, 0)),
                  pl.BlockSpec((hg, bq, 1), lambda h, i: (h, i, 0)),
                  pl.BlockSpec((bq, ATT_WIDTH), lambda h, i: (i, h))] + [HBM] * ns,
        out_specs=[pl.BlockSpec((bq, ATT_WIDTH), lambda h, i: (i, h)),
                   pl.BlockSpec((rows, ATT_WIDTH), lambda h, i: (0, h)),
                   pl.BlockSpec((rows, ATT_WIDTH), lambda h, i: (0, h))] + [HBM] * (2 * ns),
        out_shape=[jax.ShapeDtypeStruct((rows, GROUP), F32)] * 3
        + [jax.ShapeDtypeStruct(s.shape, s.dtype) for s in parts] * 2,
        scratch_shapes=_reduce_sems(ns) if ns else [],
        compiler_params=_params("arbitrary", "arbitrary"),
    )(qkv, qkv, qkv, rt, cnt, do, *parts)


def _head_sum(x, ones_bd):
    return _dot_split(x, ones_bd)


def _rwkv_pre(p, p_prev, mu, w0, a0, k_k, k_a, w_up, a_up, g_up, ones_bd):
    xs = p + (p_prev - p) * mu
    r = xs[:, :GROUP]
    k0 = xs[:, GROUP:2 * GROUP]
    v = xs[:, 2 * GROUP:3 * GROUP]
    lo = xs[:, 3 * GROUP:]
    wa = w0 + _dot(jnp.tanh(lo).astype(BF16), w_up.astype(BF16))
    w = -(jnp.maximum(-wa, 0.0) + jnp.log(1.0 + jnp.exp(-jnp.abs(wa)))) - 0.5
    log_decay = -jnp.exp(w)
    alpha = _sigmoid(a0 + _dot(lo.astype(BF16), a_up.astype(BF16)))
    gate = _dot(_sigmoid(lo).astype(BF16), g_up.astype(BF16))
    kk = k0 * k_k
    kk = kk * lax.rsqrt(jnp.maximum(_head_sum(kk * kk, ones_bd), 1e-24))
    k = k0 * (1.0 + (alpha - 1.0) * k_a)
    return r, log_decay, k, v, -kk, kk * alpha, gate


def _rwkv_post(y, r, k, v, gate, lnx_w, lnx_b, r_k, ones_bd):
    mean = _head_sum(y, ones_bd) * (1.0 / HEAD)
    yc = y - mean
    var = _head_sum(yc * yc, ones_bd) * (1.0 / HEAD)
    yn = yc * lax.rsqrt(var + LNX_EPS) * lnx_w + lnx_b
    bonus = _head_sum(r * k * r_k, ones_bd) * v
    return (yn + bonus) * gate


_PRE_VEC = 5
_PRE_MAT = 3


def _heads(x):
    return jnp.stack([x[:, _lanes(h)] for h in range(N_HEADS)])


def _unheads(x):
    return jnp.concatenate([x[h] for h in range(N_HEADS)], axis=1)


def _edge_spec(tm, width, tile_of):
    return pl.BlockSpec((8, width), lambda i: (jnp.maximum(tile_of(i) * (tm // 8) - 1, 0), 0))


def _previous_rows(p_ref, edge_ref, tile):
    p = p_ref[...]
    edge = jnp.where(tile == 0, 0.0, edge_ref[7:8, :])
    row = lax.broadcasted_iota(jnp.int32, (p.shape[0], 1), 0)
    return jnp.where(row == 0, edge, pltpu.roll(p, 1, axis=0))


def _rwkv_pre_fwd(name, p, vecs, mats, ones_bd):
    rows = p.shape[0]
    tm = _row_tile(rows)
    row_spec = lambda w: pl.BlockSpec((tm, w), lambda i: (i, 0))
    full = lambda a: pl.BlockSpec(a.shape, lambda i: (0,) * a.ndim)

    def body(p_ref, edge_ref, *refs):
        ins = [r[...] for r in refs[:_PRE_VEC + _PRE_MAT + 1]]
        outs = refs[_PRE_VEC + _PRE_MAT + 1:]
        prev = _previous_rows(p_ref, edge_ref, pl.program_id(0))
        for o_ref, val in zip(outs, _rwkv_pre(p_ref[...], prev, *ins)):
            o_ref[...] = val

    return pl.pallas_call(
        body, name=name, grid=(rows // tm,),
        in_specs=([row_spec(RW_COLS), _edge_spec(tm, RW_COLS, lambda i: i)]
                  + [full(a) for a in (*vecs, *mats, ones_bd)]),
        out_specs=[row_spec(GROUP)] * 7,
        out_shape=[jax.ShapeDtypeStruct((rows, GROUP), F32)] * 7,
        compiler_params=_params("arbitrary"),
    )(p, p, *vecs, *mats, ones_bd)


def _rwkv_pre_bwd(name, p, vecs, mats, ones_bd, cts_scan, ct_gate, cts_b):
    rows = p.shape[0]
    tm = _row_tile(rows)
    nt = rows // tm
    n_par = _PRE_VEC + _PRE_MAT
    tile_of = lambda i: nt - 1 - i
    row_spec = lambda w: pl.BlockSpec((tm, w), lambda i: (tile_of(i), 0))
    full = lambda a: pl.BlockSpec(a.shape, lambda i: (0,) * a.ndim)

    def body(*refs):
        p_ref, edge_ref = refs[0], refs[1]
        par = [r[...] for r in refs[2:2 + n_par]]
        ones = refs[2 + n_par][...]
        cta = [r[...] for r in refs[3 + n_par:10 + n_par]]
        ctb = [r[...] for r in refs[10 + n_par:13 + n_par]]
        dp_ref, par_outs, carry = refs[13 + n_par], refs[14 + n_par:-1], refs[-1]
        step = pl.program_id(0)

        @pl.when(step == 0)
        def _():
            carry[...] = jnp.zeros_like(carry)
            for o_ref in par_outs:
                o_ref[...] = jnp.zeros_like(o_ref)

        ct = (cta[0] + ctb[0], cta[1], cta[2] + ctb[1], cta[3] + ctb[2], cta[4], cta[5], cta[6])
        _, vjp = jax.vjp(lambda pv, ppv, *pr: _rwkv_pre(pv, ppv, *pr, ones),
                         p_ref[...], _previous_rows(p_ref, edge_ref, tile_of(step)), *par)
        grads = vjp(ct)
        row = lax.broadcasted_iota(jnp.int32, (tm, 1), 0)
        dp_ref[...] = grads[0] + jnp.where(row == tm - 1, carry[0:1, :], pltpu.roll(grads[1], tm - 1, axis=0))
        carry[0:1, :] = grads[1][0:1, :]
        for o_ref, gval in zip(par_outs, grads[2:]):
            o_ref[...] += gval

    par_arrays = (*vecs, *mats)
    return pl.pallas_call(
        body, name=name, grid=(nt,),
        in_specs=([row_spec(RW_COLS), _edge_spec(tm, RW_COLS, tile_of)] + [full(a) for a in (*par_arrays, ones_bd)]
                  + [row_spec(GROUP)] * 10),
        out_specs=[row_spec(RW_COLS)] + [full(a) for a in par_arrays],
        out_shape=[jax.ShapeDtypeStruct((rows, RW_COLS), F32)] + [jax.ShapeDtypeStruct(a.shape, F32) for a in par_arrays],
        scratch_shapes=[pltpu.VMEM((8, RW_COLS), F32)],
        compiler_params=_params("arbitrary"),
    )(p, p, *par_arrays, ones_bd, *cts_scan, ct_gate, *cts_b)


def _rwkv_post_fwd(name, y, r, k, v, gate, vecs, ones_bd):
    rows = r.shape[0]
    tm = _row_tile(rows)
    row_spec = pl.BlockSpec((tm, GROUP), lambda i: (i, 0))
    full = lambda a: pl.BlockSpec(a.shape, lambda i: (0,) * a.ndim)

    def body(*refs):
        refs[-1][...] = _rwkv_post(*(r_[...] for r_ in refs[:-1]))

    return pl.pallas_call(
        body, name=name, grid=(rows // tm,),
        in_specs=[row_spec] * 5 + [full(a) for a in (*vecs, ones_bd)],
        out_specs=row_spec,
        out_shape=jax.ShapeDtypeStruct((rows, GROUP), F32),
        compiler_params=_params("arbitrary"),
    )(y, r, k, v, gate, *vecs, ones_bd)


def _rwkv_post_bwd(name, y, r, k, v, gate, vecs, ones_bd, dout):
    rows = r.shape[0]
    tm = _row_tile(rows)
    row_spec = pl.BlockSpec((tm, GROUP), lambda i: (i, 0))
    full = lambda a: pl.BlockSpec(a.shape, lambda i: (0,) * a.ndim)

    def body(*refs):
        vals = [r_[...] for r_ in refs[:8]]
        ones = refs[8][...]
        dout_v = refs[9][...]
        outs = refs[10:]
        _, vjp = jax.vjp(lambda *a: _rwkv_post(*a, ones), *vals)
        grads = vjp(dout_v)
        for o_ref, gval in zip(outs[:5], grads[:5]):
            o_ref[...] = gval

        @pl.when(pl.program_id(0) == 0)
        def _():
            for o_ref in outs[5:]:
                o_ref[...] = jnp.zeros_like(o_ref)

        for o_ref, gval in zip(outs[5:], grads[5:]):
            o_ref[...] += gval

    return pl.pallas_call(
        body, name=name, grid=(rows // tm,),
        in_specs=[row_spec] * 5 + [full(a) for a in (*vecs, ones_bd)] + [row_spec],
        out_specs=[row_spec] * 5 + [full(a) for a in vecs],
        out_shape=[jax.ShapeDtypeStruct((rows, GROUP), F32)] * 5 + [jax.ShapeDtypeStruct(a.shape, F32) for a in vecs],
        compiler_params=_params("arbitrary"),
    )(y, r, k, v, gate, *vecs, ones_bd, dout)


_NN = (((2,), (1,)), ((0,), (0,)))
_NT = (((2,), (2,)), ((0,), (0,)))
_TN = (((1,), (1,)), ((0,), (0,)))


_BWD_FORMS = {"nn": (("nt", False), ("tn", False)),
              "nt": (("nn", False), ("tn", True)),
              "tn": (("nt", True), ("nn", False))}
_DIMS = {"nn": _NN, "nt": _NT, "tn": _TN}


def _bdot(a, b, form):
    return lax.dot_general(a.astype(BF16), b.astype(BF16), _DIMS[form], preferred_element_type=F32)


@functools.partial(jax.custom_vjp, nondiff_argnums=(2,))
def _bmm(a, b, form):
    return _bdot(a, b, form)


def _bmm_fwd(a, b, form):
    return _bdot(a, b, form), (a.astype(BF16), b.astype(BF16))


def _bmm_bwd(form, res, dc):
    a, b = res
    (fa, swap_a), (fb, swap_b) = _BWD_FORMS[form]
    da = _bdot(b, dc, fa) if swap_a else _bdot(dc, b, fa)
    db = _bdot(dc, a, fb) if swap_b else _bdot(a, dc, fb)
    return da, db


_bmm.defvjp(_bmm_fwd, _bmm_bwd)


@jax.custom_vjp
def _cumsum_steps(x):
    return _tri_apply(x, lambda r, c: r >= c)


def _tri_apply(x, cmp):
    nh, c, _ = x.shape
    tri = cmp(lax.broadcasted_iota(jnp.int32, (c, c), 0), lax.broadcasted_iota(jnp.int32, (c, c), 1))
    tri = jnp.broadcast_to(tri.astype(BF16)[None], (nh, c, c))
    hi, lo = _split2(x)
    return (lax.dot_general(tri, hi, _NN, preferred_element_type=F32)
            + lax.dot_general(tri, lo, _NN, preferred_element_type=F32))


_cumsum_steps.defvjp(lambda x: (_cumsum_steps(x), None), lambda _, d: (_tri_apply(d, lambda r, c: r <= c),))


@jax.custom_vjp
def _neumann(n_mat):
    c = n_mat.shape[1]
    inv, power, span = n_mat, _bmm(n_mat, n_mat, "nn"), 2
    while span < c:
        both = _bmm(jnp.concatenate([power, inv], axis=1), power, "nn")
        inv = inv + power + both[:, c:]
        power = both[:, :c]
        span *= 2
    return inv


def _neumann_fwd(n_mat):
    inv = _neumann(n_mat)
    return inv, inv


def _neumann_bwd(inv, d):
    left = d + _bmm(inv, d, "tn")
    return (left + _bmm(left, inv, "nt"),)


_neumann.defvjp(_neumann_fwd, _neumann_bwd)


def _chunk(state, r, log_w, k, v, a, b):
    nh, c, _ = r.shape
    row = lax.broadcasted_iota(jnp.int32, (c, c), 0)
    col = lax.broadcasted_iota(jnp.int32, (c, c), 1)
    cum = _cumsum_steps(log_w)
    mid = cum[:, c // 2 - 1:c // 2, :]
    a_t = a * jnp.exp(cum - log_w - mid)
    r_t = r * jnp.exp(cum - mid)
    back = jnp.exp(mid - cum)
    b_t = b * back
    k_t = k * back
    strict, incl = (row > col)[None], (row >= col)[None]
    ar = jnp.concatenate([a_t, r_t], axis=1)
    on_b = _bmm(ar, b_t, "nt")
    on_k = _bmm(ar, k_t, "nt")
    n_mat = jnp.where(strict, on_b[:, :c], 0.0)
    p_mat = jnp.where(incl, on_b[:, c:], 0.0)
    m_mat = jnp.where(strict, on_k[:, :c], 0.0)
    q_mat = jnp.where(incl, on_k[:, c:], 0.0)
    inv = _neumann(n_mat)
    s_mid = state * jnp.swapaxes(jnp.exp(mid), 1, 2)
    x = _bmm(jnp.concatenate([a_t, m_mat], axis=2), jnp.concatenate([s_mid, v], axis=1), "nn")
    u = x + _bmm(inv, x, "nn")
    y = _bmm(jnp.concatenate([r_t, p_mat, q_mat], axis=2), jnp.concatenate([s_mid, u, v], axis=1), "nn")
    grown = _bmm(jnp.concatenate([b_t, k_t], axis=1), jnp.concatenate([u, v], axis=1), "tn")
    s_new = (s_mid + grown) * jnp.swapaxes(jnp.exp(cum[:, c - 1:c, :] - mid), 1, 2)
    return y, s_new


def _scan_fwd(name, ops):
    rows = ops[0].shape[0]
    nh, dh = N_HEADS, HEAD
    nc, per = rows // CHUNK, SCAN_CHUNKS
    spec = pl.BlockSpec((per * CHUNK, GROUP), lambda c: (c, 0))

    def body(r_ref, w_ref, k_ref, v_ref, a_ref, b_ref, y_ref, st_ref, state):
        @pl.when(pl.program_id(0) == 0)
        def _():
            state[...] = jnp.zeros_like(state)

        s = state[...]
        for u in range(per):
            at = slice(u * CHUNK, (u + 1) * CHUNK)
            st_ref[u] = s
            y, s = _chunk(s, *(_heads(ref[at, :]) for ref in (r_ref, w_ref, k_ref, v_ref, a_ref, b_ref)))
            y_ref[at, :] = _unheads(y)
        state[...] = s

    return pl.pallas_call(
        body, name=name, grid=(nc // per,),
        in_specs=[spec] * 6,
        out_specs=[spec, pl.BlockSpec((per, nh, dh, dh), lambda c: (c, 0, 0, 0))],
        out_shape=[jax.ShapeDtypeStruct((rows, GROUP), F32), jax.ShapeDtypeStruct((nc, nh, dh, dh), F32)],
        scratch_shapes=[pltpu.VMEM((nh, dh, dh), F32)],
        compiler_params=_params("arbitrary"),
    )(*ops)


def _scan_bwd(name, ops, states, dy):
    rows = ops[0].shape[0]
    nh, dh = N_HEADS, HEAD
    nc, per = rows // CHUNK, SCAN_CHUNKS
    steps = nc // per
    spec = pl.BlockSpec((per * CHUNK, GROUP), lambda c: (steps - 1 - c, 0))

    def body(r_ref, w_ref, k_ref, v_ref, a_ref, b_ref, st_ref, dy_ref, *rest):
        outs, dstate = rest[:6], rest[6]

        @pl.when(pl.program_id(0) == 0)
        def _():
            dstate[...] = jnp.zeros_like(dstate)

        ds = dstate[...]
        for u in reversed(range(per)):
            at = slice(u * CHUNK, (u + 1) * CHUNK)
            _, vjp = jax.vjp(_chunk, st_ref[u],
                             *(_heads(ref[at, :]) for ref in (r_ref, w_ref, k_ref, v_ref, a_ref, b_ref)))
            grads = vjp((_heads(dy_ref[at, :]), ds))
            ds = grads[0]
            for o_ref, gval in zip(outs, grads[1:]):
                o_ref[at, :] = _unheads(gval)
        dstate[...] = ds

    return pl.pallas_call(
        body, name=name, grid=(steps,),
        in_specs=[spec] * 6 + [pl.BlockSpec((per, nh, dh, dh), lambda c: (steps - 1 - c, 0, 0, 0)), spec],
        out_specs=[spec] * 6,
        out_shape=[jax.ShapeDtypeStruct((rows, GROUP), F32)] * 6,
        scratch_shapes=[pltpu.VMEM((nh, dh, dh), F32)],
        compiler_params=_params("arbitrary"),
    )(*ops, states, dy)


def _pad_cols(x, cols):
    return jnp.concatenate([x, jnp.zeros(x.shape[:-1] + (cols - x.shape[-1],), x.dtype)], axis=-1)


def _lora_pad(w_up, a_up, g_up):
    z = lambda n: jnp.zeros((n, GROUP), F32)
    return (jnp.concatenate([w_up, z(LORA_PAD - LORA_W)], 0),
            jnp.concatenate([z(LORA_W), a_up, z(LORA_PAD - LORA_W - LORA_A)], 0),
            jnp.concatenate([z(LORA_W + LORA_A), g_up, z(LORA_PAD - LORA_W - LORA_A - LORA_G)], 0))


MID = ['w_in']
LATE = ['ffn2_w_gate', 'ffn2_w_up', 'ffn2_w_down', 'w_out']


def _local_step(x, tgt, w, late=None):
    d = x.shape[1]
    zeros = jnp.zeros((META_PAD, d), F32)
    h0 = jnp.concatenate([zeros, w["meta_tokens"], x], axis=0)
    tgt_p = jnp.concatenate([jnp.zeros((ROW0, d), F32), tgt], axis=0)
    ones_bd = ((lax.broadcasted_iota(jnp.int32, (2 * GROUP, GROUP), 0) % GROUP) // HEAD
               == lax.broadcasted_iota(jnp.int32, (2 * GROUP, GROUP), 1) // HEAD).astype(BF16)
    pre_vecs = (_pad_cols(w["rwkv_mu"], RW_COLS), w["rwkv_w0"], w["rwkv_a0"], w["rwkv_k_k"], w["rwkv_k_a"])
    pre_mats = _lora_pad(w["rwkv_w_up"], w["rwkv_a_up"], w["rwkv_g_up"])
    post_vecs = (w["rwkv_lnx_w"], w["rwkv_lnx_b"], w["rwkv_r_k"].reshape(1, GROUP))

    h1, a1, b1, *gathered = _ffn_fwd("ffn1_fwd", h0, w["ffn1_norm"], w["ffn1_w_gate"], w["ffn1_w_up"],
                                     w["ffn1_w_down"], late and ("gather", late.shards["mid"]))
    if late is not None:
        w = {**w, **late.join("mid", gathered)}
    w_in = _pad_cols(w["w_in"], IN_COLS_PAD)
    qkv, p, n2 = _norm_proj("in_proj", h1, w["mix_norm"], w_in)
    sb, rest_total, visited, *gathered = _sb_fwd("sb_fwd", qkv, late.shards["late"] if late else ())
    if late is not None:
        w = {**w, **late.join("late", gathered)}
    pre = _rwkv_pre_fwd("rwkv_pre_fwd", p, pre_vecs, pre_mats, ones_bd)
    scan_ops, token_ops = pre[:6], (pre[0], pre[2], pre[3], pre[6])
    y, states = _scan_fwd("rwkv_scan_fwd", scan_ops)
    rw = _rwkv_post_fwd("rwkv_post_fwd", y, *token_ops, post_vecs, ones_bd)
    h2, mix = _out_proj("out_proj", h1, sb, rw, w["w_out"])
    h3, a2, b2 = _ffn_fwd("ffn2_fwd", h2, w["ffn2_norm"], w["ffn2_w_gate"], w["ffn2_w_up"], w["ffn2_w_down"])
    loss8, dh3, g_final = _loss_head("loss_head", h3, w["final_norm"].reshape(1, d), tgt_p)

    g = {"final_norm": g_final.reshape(d)}
    dh2, da2, db2, s2, n3, dhh3, g["ffn2_norm"] = _ffn_bwd(
        "ffn2_bwd", dh3, h2, w["ffn2_norm"], a2, b2, w["ffn2_w_gate"], w["ffn2_w_up"], w["ffn2_w_down"])
    g["ffn2_w_gate"] = _mm_tn("ffn2_dgate", da2, n3)
    g["ffn2_w_up"] = _mm_tn("ffn2_dup", db2, n3)
    g["ffn2_w_down"] = _mm_tn("ffn2_ddown", s2, dhh3)
    dsb, drw, dh2b = _out_proj_bwd("out_proj_bwd", dh2, w["w_out"])
    g["w_out"] = _mm_tn("out_proj_dw", mix, dh2b)
    dq, dk, dv, *reduced_late = _sb_bwd("sb_bwd", qkv, rest_total, visited, dsb, late.parts("late", g) if late else ())
    post_g = _rwkv_post_bwd("rwkv_post_bwd", y, *token_ops, post_vecs, ones_bd, drw)
    g["rwkv_lnx_w"], g["rwkv_lnx_b"] = post_g[5], post_g[6]
    g["rwkv_r_k"] = post_g[7].reshape(1, N_HEADS, HEAD)
    scan_g = _scan_bwd("rwkv_scan_bwd", scan_ops, states, post_g[0])
    pre_g = _rwkv_pre_bwd("rwkv_pre_bwd", p, pre_vecs, pre_mats, ones_bd, scan_g, post_g[4], post_g[1:4])
    dp = pre_g[0]
    g["rwkv_mu"] = pre_g[1][:, :w["rwkv_mu"].shape[1]]
    g["rwkv_w0"], g["rwkv_a0"], g["rwkv_k_k"], g["rwkv_k_a"] = pre_g[2:6]
    g["rwkv_w_up"] = pre_g[6][:LORA_W]
    g["rwkv_a_up"] = pre_g[7][LORA_W:LORA_W + LORA_A]
    g["rwkv_g_up"] = pre_g[8][LORA_W + LORA_A:LORA_W + LORA_A + LORA_G]
    live = (jnp.arange(h0.shape[0]) >= META_PAD)[:, None]
    dproj = jnp.where(live, jnp.concatenate([dq, dk, dv, dp], axis=1), 0.0).astype(BF16)
    g["w_in"] = _mm_tn("in_proj_dw", n2, dproj)[:, :w["w_in"].shape[1]]
    dh1, g["mix_norm"] = _norm_proj_bwd("in_proj_bwd", dproj, w_in, h1, w["mix_norm"], dh2)
    dh0, da1, db1, s1, n1, dhh1, g["ffn1_norm"], *reduced_mid = _ffn_bwd(
        "ffn1_bwd", dh1, h0, w["ffn1_norm"], a1, b1, w["ffn1_w_gate"], w["ffn1_w_up"], w["ffn1_w_down"],
        late and ("reduce", late.parts("mid", g)))
    g["meta_tokens"] = dh0[META_PAD:ROW0]
    reduced = {"mid": reduced_mid, "late": reduced_late}
    if late is None:
        g["ffn1_w_gate"] = _mm_tn("ffn1_dgate", da1, n1)
        g["ffn1_w_up"] = _mm_tn("ffn1_dup", db1, n1)
        g["ffn1_w_down"] = _mm_tn("ffn1_ddown", s1, dhh1)
    else:
        g["ffn1_w_gate"], reduced["small"] = _mm_tn("ffn1_dgate", da1, n1, ("all_reduce", [late.small(g, loss8[0, 0])]))
        g["ffn1_w_up"], *reduced["gate"] = _mm_tn("ffn1_dup", db1, n1, ("reduce", late.parts("gate", g)))
        g["ffn1_w_down"], *reduced["up"] = _mm_tn("ffn1_ddown", s1, dhh1, ("reduce", late.parts("up", g)))
    return loss8[0, 0], dh0[ROW0:], g, reduced


N_CHIPS = 4
N_DEV = 8
HBM = pl.BlockSpec(memory_space=pltpu.HBM)


def _place():
    return lax.axis_index("x"), lax.axis_index("y"), lax.axis_index("c")


def _other_chips(x, y):
    return [(1 - x, y), (x, 1 - y), (1 - x, 1 - y)]


def _gather_sems(n):
    return [pltpu.SemaphoreType.DMA((3 * n,)), pltpu.SemaphoreType.DMA((3 * n,)), pltpu.SemaphoreType.DMA((n,)),
            pltpu.SemaphoreType.DMA((3 * n,)), pltpu.SemaphoreType.DMA((3 * n,))]


def _gather_exchange(ins, outs, sems):
    n = len(ins)
    half = [r.shape[0] // 2 for r in ins]
    send, recv, local, d2d_send, d2d_recv = sems
    x, y, c = _place()
    me = 2 * x + y
    chips = _other_chips(x, y)

    def rows_of(k, h):
        return pl.ds(pl.multiple_of(h * half[k], 8), half[k])

    def own(k):
        return pltpu.make_async_copy(ins[k], outs[k].at[me], local.at[k])

    def copy(j, k, slot):
        return pltpu.make_async_remote_copy(
            src_ref=ins[k].at[rows_of(k, c)], dst_ref=outs[k].at[slot, rows_of(k, c)],
            send_sem=send.at[j * n + k], recv_sem=recv.at[j * n + k],
            device_id=(chips[j][0], chips[j][1], c), device_id_type=MESH)

    def passed(j, k, h):
        slot = 2 * chips[j][0] + chips[j][1]
        return pltpu.make_async_remote_copy(
            src_ref=outs[k].at[slot, rows_of(k, h)], dst_ref=outs[k].at[slot, rows_of(k, h)],
            send_sem=d2d_send.at[j * n + k], recv_sem=d2d_recv.at[j * n + k],
            device_id=(x, y, 1 - c), device_id_type=MESH)

    def start():
        for k in range(n):
            own(k).start()
        for j in range(3):
            for k in range(n):
                copy(j, k, me).start()

    def forward():
        for j in range(3):
            for k in range(n):
                copy(j, k, 2 * chips[j][0] + chips[j][1]).wait_recv()
                passed(j, k, c).start()

    def finish():
        for j in range(3):
            for k in range(n):
                passed(j, k, 1 - c).wait_recv()
        for j in range(3):
            for k in range(n):
                copy(j, k, me).wait_send()
                passed(j, k, c).wait_send()
        for k in range(n):
            own(k).wait()

    return start, forward, finish


def _gather_shards(name, shards):
    n = len(shards)

    def body(*refs):
        for stage in _gather_exchange(refs[:n], refs[n:2 * n], refs[2 * n:]):
            stage()

    return pl.pallas_call(
        body, name=name,
        in_specs=[HBM] * n, out_specs=[HBM] * n,
        out_shape=[jax.ShapeDtypeStruct((N_CHIPS,) + s.shape, s.dtype) for s in shards],
        scratch_shapes=_gather_sems(n),
    )(*shards)


def _pair_sum(name, part):
    nch, rows, cols = part.shape
    half = rows // 2

    def body(p_hbm, p_ref, out_ref, landed, send, recv):
        j = pl.program_id(0)
        x, y, c = _place()

        def copy(k):
            theirs = pl.ds(pl.multiple_of((1 - c) * half, 16), half)
            return pltpu.make_async_remote_copy(
                src_ref=p_hbm.at[k, theirs], dst_ref=landed.at[k], send_sem=send.at[k], recv_sem=recv.at[k],
                device_id=(x, y, 1 - c), device_id_type=MESH)

        @pl.when(j == 0)
        def _():
            for k in range(nch):
                copy(k).start()

        copy(j).wait_recv()
        mine = p_ref[0, pl.ds(pl.multiple_of(c * half, 16), half), :]
        out_ref[0] = (mine.astype(F32) + landed[j].astype(F32)).astype(out_ref.dtype)

        @pl.when(j == nch - 1)
        def _():
            for k in range(nch):
                copy(k).wait_send()

    return pl.pallas_call(
        body, name=name, grid=(nch,),
        in_specs=[HBM, pl.BlockSpec((1, rows, cols), lambda j: (j, 0, 0))],
        out_specs=pl.BlockSpec((1, half, cols), lambda j: (j, 0, 0)),
        out_shape=jax.ShapeDtypeStruct((nch, half, cols), part.dtype),
        scratch_shapes=[pltpu.VMEM((nch, half, cols), part.dtype),
                        pltpu.SemaphoreType.DMA((nch,)), pltpu.SemaphoreType.DMA((nch,))],
        compiler_params=_params("arbitrary"),
    )(part, part)


def _reduce_sems(n):
    return [pltpu.SemaphoreType.DMA((3 * n,)), pltpu.SemaphoreType.DMA((3 * n,)), pltpu.SemaphoreType.DMA((n,)),
            pltpu.SemaphoreType.DMA((n,)), pltpu.SemaphoreType.DMA((n,))]


def _reduce_exchange(ins, got, sib, sems):
    n = len(ins)
    send, recv, local, d2d_send, d2d_recv = sems
    x, y, c = _place()
    me = 2 * x + y
    chips = _other_chips(x, y)

    def own(k):
        return pltpu.make_async_copy(ins[k].at[me], got[k].at[me], local.at[k])

    def copy(j, k, shard, slot):
        return pltpu.make_async_remote_copy(
            src_ref=ins[k].at[shard], dst_ref=got[k].at[slot], send_sem=send.at[j * n + k],
            recv_sem=recv.at[j * n + k], device_id=(chips[j][0], chips[j][1], c), device_id_type=MESH)

    def swap(k):
        return pltpu.make_async_remote_copy(
            src_ref=got[k], dst_ref=sib[k], send_sem=d2d_send.at[k], recv_sem=d2d_recv.at[k],
            device_id=(x, y, 1 - c), device_id_type=MESH)

    def start():
        for k in range(n):
            own(k).start()
        for j in range(3):
            for k in range(n):
                copy(j, k, 2 * chips[j][0] + chips[j][1], me).start()

    def forward():
        for k in range(n):
            own(k).wait()
            for j in range(3):
                copy(j, k, me, 2 * chips[j][0] + chips[j][1]).wait_recv()
            swap(k).start()

    def finish():
        for k in range(n):
            swap(k).wait_recv()
        for j in range(3):
            for k in range(n):
                copy(j, k, me, me).wait_send()
        for k in range(n):
            swap(k).wait_send()

    return start, forward, finish


def _reduce_shards(name, parts):
    n = len(parts)

    def body(*refs):
        for stage in _reduce_exchange(refs[:n], refs[n:2 * n], refs[2 * n:3 * n], refs[3 * n:]):
            stage()

    return pl.pallas_call(
        body, name=name,
        in_specs=[HBM] * n, out_specs=[HBM] * (2 * n),
        out_shape=[jax.ShapeDtypeStruct(s.shape, s.dtype) for s in parts] * 2,
        scratch_shapes=_reduce_sems(n),
    )(*parts)


def _all_reduce_scratch(vec):
    return [pltpu.VMEM((N_DEV,) + vec.shape, F32),
            pltpu.SemaphoreType.DMA((N_DEV - 1,)), pltpu.SemaphoreType.DMA((N_DEV - 1,))]


def _all_reduce_exchange(v_ref, o_ref, buf, send, recv):
    x, y, c = _place()
    me = 4 * x + 2 * y + c
    peers = [(x ^ (r >> 2), y ^ ((r >> 1) & 1), c ^ (r & 1)) for r in range(1, N_DEV)]

    def copy(r, slot):
        px, py, pc = peers[r]
        return pltpu.make_async_remote_copy(
            src_ref=v_ref, dst_ref=buf.at[slot], send_sem=send.at[r], recv_sem=recv.at[r],
            device_id=(px, py, pc), device_id_type=MESH)

    def start():
        for r in range(N_DEV - 1):
            copy(r, me).start()
        buf[me] = v_ref[...]

    def finish():
        for r in range(N_DEV - 1):
            px, py, pc = peers[r]
            copy(r, 4 * px + 2 * py + pc).wait_recv()
        total = buf[0]
        for dev in range(1, N_DEV):
            total = total + buf[dev]
        o_ref[...] = total
        for r in range(N_DEV - 1):
            copy(r, me).wait_send()

    return start, lambda: None, finish


def _adamw(w, g, m, v):
    m = ADAM_B1 * m + (1.0 - ADAM_B1) * g
    v = ADAM_B2 * v + (1.0 - ADAM_B2) * (g * g)
    m_hat = m / (1.0 - ADAM_B1 ** ADAM_STEP)
    v_hat = v / (1.0 - ADAM_B2 ** ADAM_STEP)
    return -ADAM_LR * (m_hat / (jnp.sqrt(v_hat) + ADAM_EPS) + ADAM_WD * w), m, v


def _adamw_shard(name, core, w, m, v, got, sib):
    rows, cols = w.shape
    tr = rows // 4
    spec = pl.BlockSpec((tr, cols), lambda i, c_ref: (i, 0))
    got_spec = pl.BlockSpec((N_CHIPS, tr, cols), lambda i, c_ref: (0, jnp.where(i // 2 == c_ref[0], i % 2, 0), 0))
    sib_spec = pl.BlockSpec((N_CHIPS, tr, cols), lambda i, c_ref: (0, jnp.where(i // 2 == c_ref[0], 0, i % 2), 0))

    def body(c_ref, w_ref, m_ref, v_ref, got_ref, sib_ref, g_ref, d_ref, mo_ref, vo_ref):
        def four(ref):
            return ((ref[0].astype(F32) + ref[1].astype(F32)) + ref[2].astype(F32)) + ref[3].astype(F32)

        g = jnp.where(pl.program_id(0) // 2 == c_ref[0], four(got_ref), four(sib_ref))
        g_ref[...] = g
        d_ref[...], mo_ref[...], vo_ref[...] = _adamw(w_ref[...], g, m_ref[...], v_ref[...])

    return pl.pallas_call(
        body, name=name,
        grid_spec=pltpu.PrefetchScalarGridSpec(
            num_scalar_prefetch=1, grid=(4,),
            in_specs=[spec, spec, spec, got_spec, sib_spec], out_specs=[spec] * 4),
        out_shape=[jax.ShapeDtypeStruct((rows, cols), F32)] * 4,
        compiler_params=_params("arbitrary"),
    )(core, w, m, v, got, sib)


def _adamw_small(name, w, m, v, g):
    def body(w_ref, m_ref, v_ref, g_ref, d_ref, mo_ref, vo_ref):
        d_ref[...], mo_ref[...], vo_ref[...] = _adamw(w_ref[...], g_ref[...], m_ref[...], v_ref[...])

    return pl.pallas_call(body, name=name, out_shape=[jax.ShapeDtypeStruct(w.shape, F32)] * 3)(w, m, v, g)


def _cast_bf16(name, arrays):
    n = len(arrays)

    def body(*refs):
        for i_ref, o_ref in zip(refs[:n], refs[n:]):
            o_ref[...] = i_ref[...].astype(BF16)

    return pl.pallas_call(
        body, name=name, out_shape=[jax.ShapeDtypeStruct(a.shape, BF16) for a in arrays],
        compiler_params=pltpu.CompilerParams(vmem_limit_bytes=VMEM_LIMIT),
    )(*arrays)


def _pack(arrays, rows):
    flat = jnp.concatenate([a.reshape(-1) for a in arrays])
    return jnp.concatenate([flat, jnp.zeros((rows * 128 - flat.shape[0],), F32)]).reshape(rows, 128)


def _unpack(packed, shapes):
    flat, out, at = packed.reshape(-1), [], 0
    for s in shapes:
        size = 1
        for dim in s:
            size *= dim
        out.append(flat[at:at + size].reshape(s))
        at += size
    return out


def _rows_for(shapes):
    total = 0
    for s in shapes:
        size = 1
        for dim in s:
            size *= dim
        total += size
    return -(-total // 1024) * 8


WEIGHTS = ['meta_tokens', 'ffn1_norm', 'ffn1_w_gate', 'ffn1_w_up', 'ffn1_w_down', 'mix_norm', 'w_in', 'rwkv_mu',
           'rwkv_w0', 'rwkv_w_up', 'rwkv_a0', 'rwkv_a_up', 'rwkv_g_up', 'rwkv_k_k', 'rwkv_k_a', 'rwkv_r_k',
           'rwkv_lnx_w', 'rwkv_lnx_b', 'w_out', 'ffn2_norm', 'ffn2_w_gate', 'ffn2_w_up', 'ffn2_w_down', 'final_norm']
COL_CUT = ['ffn1_w_gate', 'ffn1_w_up', 'w_in', 'ffn2_w_gate', 'ffn2_w_up']
ROW_CUT = ['ffn1_w_down', 'w_out', 'ffn2_w_down']
SMALL_CUT = ['meta_tokens', 'rwkv_w_up', 'rwkv_a_up', 'rwkv_g_up']
TRANSPOSED = ['ffn1_w_gate', 'ffn1_w_up', 'ffn2_w_gate', 'ffn2_w_up']
BIG = COL_CUT + ROW_CUT
REPLICATED = [n for n in WEIGHTS if n not in BIG + SMALL_CUT]


def _join_cols(a):
    return a.transpose(1, 0, 2).reshape(a.shape[1], N_CHIPS * a.shape[2])


def _cut_cols(a):
    return a.reshape(a.shape[0], N_CHIPS, a.shape[1] // N_CHIPS).transpose(1, 0, 2)


def _step(x, loss_target, w, m, v):
    two = lambda a: a.reshape(a.shape[-2], a.shape[-1])

    def rows_cut(n, a):
        return jnp.swapaxes(two(a), 0, 1) if n in TRANSPOSED else two(a)

    def as_given(n, a, like):
        return (jnp.swapaxes(a, 0, 1) if n in TRANSPOSED else a).reshape(like.shape)

    col_cut = [n for n in COL_CUT + SMALL_CUT if n not in TRANSPOSED]

    def join(names, gathered):
        return {n: (_join_cols(a) if n in col_cut else a.reshape(-1, a.shape[-1])) for n, a in zip(names, gathered)}

    def pair_sums(names, g):
        parts = [_cut_cols(g[n]) if n in col_cut else g[n].reshape(N_CHIPS, -1, g[n].shape[-1]) for n in names]
        return [_pair_sum("pair_sum_" + n, p) for n, p in zip(names, parts)]

    first = [n for n in BIG if n not in MID + LATE]
    gathered_later = {"mid": MID, "late": LATE}
    groups = {**gathered_later, "gate": ["ffn1_w_gate"], "up": ["ffn1_w_up"]}
    cast = dict(zip(BIG, _cast_bf16("cast_weights", [rows_cut(n, w[n]) for n in BIG])))
    names = first + SMALL_CUT
    shards = [cast[n] for n in first] + [two(w[n]) for n in SMALL_CUT]
    full = {n: (two(w[n]) if w[n].ndim == 3 else w[n]) for n in REPLICATED}
    full.update(join(names, _gather_shards("gather_weights", shards)))
    full["rwkv_r_k"] = w["rwkv_r_k"]
    full["final_norm"] = w["final_norm"]

    small_names = REPLICATED + SMALL_CUT

    def small(g, loss):
        arrays = [loss.reshape(1)] + [g[n] for n in small_names]
        return _pack(arrays, _rows_for([a.shape for a in arrays]))

    late = types.SimpleNamespace(shards={k: [cast[n] for n in names] for k, names in gathered_later.items()},
                                 join=lambda k, gathered: join(groups[k], gathered),
                                 parts=lambda k, g: pair_sums(groups[k], g), small=small)

    _, dx, g, reduced = _local_step(x[0], loss_target[0], full, late)

    groups["down"] = ["ffn1_w_down"]
    reduced["down"] = list(_reduce_shards("reduce_gradients", pair_sums(groups["down"], g)))
    got, sib = {}, {}
    for k, names in groups.items():
        got.update(zip(names, reduced[k][:len(names)]))
        sib.update(zip(names, reduced[k][len(names):]))
    loss, *summed = _unpack(reduced["small"], [(1,)] + [g[n].shape for n in small_names])
    loss = loss.reshape(())
    g_small = dict(zip(small_names, summed))
    chip = 2 * lax.axis_index("x") + lax.axis_index("y")
    for n in SMALL_CUT:
        width = g_small[n].shape[1] // N_CHIPS
        g_small[n] = lax.dynamic_slice_in_dim(g_small[n], chip * width, width, axis=1)

    grad, delta, new_m, new_v = {}, {}, {}, {}
    core = lax.axis_index("c").astype(jnp.int32).reshape(1)
    for n in BIG:
        outs = _adamw_shard("adamw_" + n, core, rows_cut(n, w[n]), rows_cut(n, m[n]), rows_cut(n, v[n]), got[n], sib[n])
        grad[n], delta[n], new_m[n], new_v[n] = (as_given(n, o, w[n]) for o in outs)
    shapes = [w[n].shape for n in small_names]
    rows = _rows_for(shapes)
    packed = [_pack([t[n] for n in small_names], rows) for t in (w, m, v)]
    g_packed = _pack([g_small[n] for n in small_names], rows)
    outs = [_unpack(o, shapes) for o in _adamw_small("adamw_small", *packed, g_packed)]
    for i, n in enumerate(small_names):
        grad[n] = g_small[n].reshape(w[n].shape)
        delta[n], new_m[n], new_v[n] = outs[0][i], outs[1][i], outs[2][i]
    return loss, dx[None], grad, delta, new_m, new_v


def kernel(x, meta_tokens, ffn1_norm, ffn1_w_gate, ffn1_w_up, ffn1_w_down, mix_norm, w_in, rwkv_mu, rwkv_w0, rwkv_w_up, rwkv_a0, rwkv_a_up, rwkv_g_up, rwkv_k_k, rwkv_k_a, rwkv_r_k, rwkv_lnx_w, rwkv_lnx_b, w_out, ffn2_norm, ffn2_w_gate, ffn2_w_up, ffn2_w_down, final_norm, loss_target, m_meta_tokens, m_ffn1_norm, m_ffn1_w_gate, m_ffn1_w_up, m_ffn1_w_down, m_mix_norm, m_w_in, m_rwkv_mu, m_rwkv_w0, m_rwkv_w_up, m_rwkv_a0, m_rwkv_a_up, m_rwkv_g_up, m_rwkv_k_k, m_rwkv_k_a, m_rwkv_r_k, m_rwkv_lnx_w, m_rwkv_lnx_b, m_w_out, m_ffn2_norm, m_ffn2_w_gate, m_ffn2_w_up, m_ffn2_w_down, m_final_norm, v_meta_tokens, v_ffn1_norm, v_ffn1_w_gate, v_ffn1_w_up, v_ffn1_w_down, v_mix_norm, v_w_in, v_rwkv_mu, v_rwkv_w0, v_rwkv_w_up, v_rwkv_a0, v_rwkv_a_up, v_rwkv_g_up, v_rwkv_k_k, v_rwkv_k_a, v_rwkv_r_k, v_rwkv_lnx_w, v_rwkv_lnx_b, v_w_out, v_ffn2_norm, v_ffn2_w_gate, v_ffn2_w_up, v_ffn2_w_down, v_final_norm):
    w = dict(zip(WEIGHTS, (meta_tokens, ffn1_norm, ffn1_w_gate, ffn1_w_up, ffn1_w_down, mix_norm, w_in, rwkv_mu, rwkv_w0, rwkv_w_up, rwkv_a0, rwkv_a_up, rwkv_g_up, rwkv_k_k, rwkv_k_a, rwkv_r_k, rwkv_lnx_w, rwkv_lnx_b, w_out, ffn2_norm, ffn2_w_gate, ffn2_w_up, ffn2_w_down, final_norm)))
    m = dict(zip(WEIGHTS, (m_meta_tokens, m_ffn1_norm, m_ffn1_w_gate, m_ffn1_w_up, m_ffn1_w_down, m_mix_norm, m_w_in, m_rwkv_mu, m_rwkv_w0, m_rwkv_w_up, m_rwkv_a0, m_rwkv_a_up, m_rwkv_g_up, m_rwkv_k_k, m_rwkv_k_a, m_rwkv_r_k, m_rwkv_lnx_w, m_rwkv_lnx_b, m_w_out, m_ffn2_norm, m_ffn2_w_gate, m_ffn2_w_up, m_ffn2_w_down, m_final_norm)))
    v = dict(zip(WEIGHTS, (v_meta_tokens, v_ffn1_norm, v_ffn1_w_gate, v_ffn1_w_up, v_ffn1_w_down, v_mix_norm, v_w_in, v_rwkv_mu, v_rwkv_w0, v_rwkv_w_up, v_rwkv_a0, v_rwkv_a_up, v_rwkv_g_up, v_rwkv_k_k, v_rwkv_k_a, v_rwkv_r_k, v_rwkv_lnx_w, v_rwkv_lnx_b, v_w_out, v_ffn2_norm, v_ffn2_w_gate, v_ffn2_w_up, v_ffn2_w_down, v_final_norm)))
    loss, grad_x, grad, delta, new_m, new_v = _step(x, loss_target, w, m, v)
    return (loss, grad_x, *[grad[n] for n in WEIGHTS], *[delta[n] for n in WEIGHTS],
            *[new_m[n] for n in WEIGHTS], *[new_v[n] for n in WEIGHTS])
```

```python
import functools
import types

import jax
import jax.numpy as jnp
from jax import lax
from jax.experimental import pallas as pl
from jax.experimental.pallas import tpu as pltpu

F32 = jnp.float32
BF16 = jnp.bfloat16

RMS_EPS = 1e-6
LNX_EPS = 64e-5
N_META = 16
ROW0 = 128
META_PAD = ROW0 - N_META
HEAD = 64
N_HEADS = 8
GROUP = N_HEADS * HEAD
LORA_W, LORA_A, LORA_G = 32, 32, 96
LORA_PAD = 256
RW_COLS = 3 * GROUP + LORA_PAD
IN_COLS_PAD = 3 * GROUP + RW_COLS
ATT_BLOCK = 128
CHUNK = 64
SCAN_CHUNKS = 2
VMEM_LIMIT = 56 * 1024 * 1024

ADAM_LR, ADAM_B1, ADAM_B2, ADAM_EPS, ADAM_WD, ADAM_STEP = 0.001, 0.9, 0.999, 1e-08, 0.01, 10

MESH = pl.DeviceIdType.MESH


def _params(*sem):
    return pltpu.CompilerParams(dimension_semantics=tuple(sem), vmem_limit_bytes=VMEM_LIMIT)


def _dot(a, b):
    return lax.dot_general(a, b, (((1,), (0,)), ((), ())), preferred_element_type=F32)


def _dot_nt(a, b):
    return lax.dot_general(a, b, (((1,), (1,)), ((), ())), preferred_element_type=F32)


def _dot_tn(a, b):
    return lax.dot_general(a, b, (((0,), (0,)), ((), ())), preferred_element_type=F32)


def _split2(x):
    hi = x.astype(BF16)
    return hi, (x - hi.astype(F32)).astype(BF16)


def _sigmoid(x):
    return 1.0 / (1.0 + jnp.exp(-x))


def _rms_fwd(x, g):
    rstd = lax.rsqrt(jnp.mean(x * x, axis=-1, keepdims=True) + RMS_EPS)
    xhat = x * rstd
    return xhat * g, xhat, rstd


def _rms_bwd(dn, xhat, rstd, g):
    dxhat = dn * g
    dx = rstd * (dxhat - xhat * jnp.mean(dxhat * xhat, axis=-1, keepdims=True))
    return dx, jnp.sum(dn * xhat, axis=0, keepdims=True)


def _row_tile(rows):
    return 384 if rows % 384 == 0 else 128


def _half_tile(cols):
    return cols // 2 if cols % 256 == 0 else cols


def _tall_tile(rows, parts):
    return rows // parts if rows % (16 * parts) == 0 else _row_tile(rows)


def _call_with_exchange(name, body, grid, in_specs, out_specs, out_shape, scratch, operands, params, exchange):
    if exchange is None or not exchange[1]:
        return pl.pallas_call(body, name=name, grid=grid, in_specs=in_specs, out_specs=out_specs,
                              out_shape=out_shape, scratch_shapes=scratch, compiler_params=params)(*operands)
    kind, arrays = exchange
    ns, n_in, n_out, n_scr = len(arrays), len(in_specs), len(out_specs), len(scratch)
    whole = lambda a: pl.BlockSpec(a.shape, lambda *_: (0,) * a.ndim)
    if kind == "gather":
        results = [jax.ShapeDtypeStruct((N_CHIPS,) + s.shape, s.dtype) for s in arrays]
        sems, sent_specs, landed_specs = _gather_sems(ns), [HBM] * ns, [HBM] * ns
    elif kind == "reduce":
        results = [jax.ShapeDtypeStruct(s.shape, s.dtype) for s in arrays] * 2
        sems, sent_specs, landed_specs = _reduce_sems(ns), [HBM] * ns, [HBM] * (2 * ns)
    else:
        results = [jax.ShapeDtypeStruct(arrays[0].shape, F32)]
        sems, sent_specs, landed_specs = _all_reduce_scratch(arrays[0]), [whole(arrays[0])], [whole(arrays[0])]
    n_res = len(results)

    def carried(*refs):
        at = n_in + ns + n_out
        sent, landed = refs[n_in:n_in + ns], refs[at:at + n_res]
        own_scratch, sem_refs = refs[at + n_res:at + n_res + n_scr], refs[at + n_res + n_scr:]
        first, before_last, last = _exchange_steps(grid)
        if kind == "gather":
            start, forward, finish = _gather_exchange(sent, landed, sem_refs)
        elif kind == "reduce":
            start, forward, finish = _reduce_exchange(sent, landed[:ns], landed[ns:], sem_refs)
        else:
            start, forward, finish = _all_reduce_exchange(sent[0], landed[0], *sem_refs)
        pl.when(first)(start)
        body(*refs[:n_in], *refs[n_in + ns:at], *own_scratch)
        pl.when(before_last)(forward)
        pl.when(last)(finish)

    return pl.pallas_call(
        carried, name=name, grid=grid, in_specs=list(in_specs) + sent_specs, out_specs=list(out_specs) + landed_specs,
        out_shape=list(out_shape) + results, scratch_shapes=list(scratch) + sems, compiler_params=params,
    )(*operands, *arrays)


def _ffn_fwd(name, h, g, wg, wu, wd, exchange=None):
    rows, d = h.shape
    f = wg.shape[0]
    tm, tf = _row_tile(rows), _half_tile(f)
    nj = f // tf

    def body(h_ref, g_ref, wg_ref, wu_ref, wd_ref, ho_ref, a_ref, b_ref, n_sc, acc_sc):
        j = pl.program_id(1)

        @pl.when(j == 0)
        def _():
            n, _, _ = _rms_fwd(h_ref[...], g_ref[...])
            n_sc[...] = n.astype(BF16)
            acc_sc[...] = jnp.zeros_like(acc_sc)

        n = n_sc[...]
        a = _dot_nt(n, wg_ref[...])
        b = _dot_nt(n, wu_ref[...])
        a_ref[...] = a
        b_ref[...] = b
        s = a * _sigmoid(a) * b
        acc_sc[...] += _dot(s.astype(BF16), wd_ref[...])

        @pl.when(j == nj - 1)
        def _():
            ho_ref[...] = h_ref[...] + 0.5 * acc_sc[...]

    return _call_with_exchange(
        name, body, (rows // tm, nj),
        [pl.BlockSpec((tm, d), lambda i, j: (i, 0)),
         pl.BlockSpec((1, d), lambda i, j: (0, 0)),
         pl.BlockSpec((tf, d), lambda i, j: (j, 0)),
         pl.BlockSpec((tf, d), lambda i, j: (j, 0)),
         pl.BlockSpec((tf, d), lambda i, j: (j, 0))],
        [pl.BlockSpec((tm, d), lambda i, j: (i, 0)),
         pl.BlockSpec((tm, tf), lambda i, j: (i, j)),
         pl.BlockSpec((tm, tf), lambda i, j: (i, j))],
        [jax.ShapeDtypeStruct((rows, d), F32),
         jax.ShapeDtypeStruct((rows, f), F32),
         jax.ShapeDtypeStruct((rows, f), F32)],
        [pltpu.VMEM((tm, d), BF16), pltpu.VMEM((tm, d), F32)],
        (h, g, wg, wu, wd), _params("arbitrary", "arbitrary"), exchange)


def _ffn_bwd(name, dh, h, g, a, b, wg, wu, wd, exchange=None):
    rows, d = h.shape
    f = wg.shape[0]
    tm, tf = _row_tile(rows), _half_tile(f)
    ni, nj = rows // tm, f // tf

    def body(dh_ref, h_ref, g_ref, a_ref, b_ref, wg_ref, wu_ref, wd_ref,
             dhi_ref, da_ref, db_ref, s_ref, n_ref, dhh_ref, dg_ref, dn_sc):
        i, j = pl.program_id(0), pl.program_id(1)

        @pl.when(j == 0)
        def _():
            n, _, _ = _rms_fwd(h_ref[...], g_ref[...])
            n_ref[...] = n.astype(BF16)
            dhh_ref[...] = (0.5 * dh_ref[...]).astype(BF16)
            dn_sc[...] = jnp.zeros_like(dn_sc)

        @pl.when((i == 0) & (j == 0))
        def _():
            dg_ref[...] = jnp.zeros_like(dg_ref)

        ds = _dot_nt(dhh_ref[...], wd_ref[...])
        av, bv = a_ref[...], b_ref[...]
        sig = _sigmoid(av)
        silu = av * sig
        s_ref[...] = (silu * bv).astype(BF16)
        db = (ds * silu).astype(BF16)
        da = (ds * bv * (sig * (1.0 + av * (1.0 - sig)))).astype(BF16)
        da_ref[...] = da
        db_ref[...] = db
        dn_sc[...] += _dot(da, wg_ref[...]) + _dot(db, wu_ref[...])

        @pl.when(j == nj - 1)
        def _():
            gv = g_ref[...]
            _, xhat, rstd = _rms_fwd(h_ref[...], gv)
            dx, dg = _rms_bwd(dn_sc[...], xhat, rstd, gv)
            dhi_ref[...] = dh_ref[...] + dx
            dg_ref[...] += dg

    return _call_with_exchange(
        name, body, (ni, nj),
        [pl.BlockSpec((tm, d), lambda i, j: (i, 0)),
         pl.BlockSpec((tm, d), lambda i, j: (i, 0)),
         pl.BlockSpec((1, d), lambda i, j: (0, 0)),
         pl.BlockSpec((tm, tf), lambda i, j: (i, j)),
         pl.BlockSpec((tm, tf), lambda i, j: (i, j)),
         pl.BlockSpec((tf, d), lambda i, j: (j, 0)),
         pl.BlockSpec((tf, d), lambda i, j: (j, 0)),
         pl.BlockSpec((tf, d), lambda i, j: (j, 0))],
        [pl.BlockSpec((tm, d), lambda i, j: (i, 0)),
         pl.BlockSpec((tm, tf), lambda i, j: (i, j)),
         pl.BlockSpec((tm, tf), lambda i, j: (i, j)),
         pl.BlockSpec((tm, tf), lambda i, j: (i, j)),
         pl.BlockSpec((tm, d), lambda i, j: (i, 0)),
         pl.BlockSpec((tm, d), lambda i, j: (i, 0)),
         pl.BlockSpec((1, d), lambda i, j: (0, 0))],
        [jax.ShapeDtypeStruct((rows, d), F32),
         jax.ShapeDtypeStruct((rows, f), BF16),
         jax.ShapeDtypeStruct((rows, f), BF16),
         jax.ShapeDtypeStruct((rows, f), BF16),
         jax.ShapeDtypeStruct((rows, d), BF16),
         jax.ShapeDtypeStruct((rows, d), BF16),
         jax.ShapeDtypeStruct((1, d), F32)],
        [pltpu.VMEM((tm, d), F32)],
        (dh, h, g, a, b, wg, wu, wd), _params("arbitrary", "arbitrary"), exchange)


def _mm_tn(name, a, b, exchange=None):
    k, m = a.shape
    n = b.shape[1]
    tk = _tall_tile(k, 3)
    tm = _half_tile(m) if m > 1024 else m
    tn = _half_tile(n) if n > 1024 else n
    nk = k // tk

    def body(a_ref, b_ref, o_ref, acc):
        kk = pl.program_id(2)

        @pl.when(kk == 0)
        def _():
            acc[...] = jnp.zeros_like(acc)

        acc[...] += _dot_tn(a_ref[...], b_ref[...])

        @pl.when(kk == nk - 1)
        def _():
            o_ref[...] = acc[...].astype(BF16)

    outs = _call_with_exchange(
        name, body, (m // tm, n // tn, nk),
        [pl.BlockSpec((tk, tm), lambda i, j, kk: (kk, i)),
         pl.BlockSpec((tk, tn), lambda i, j, kk: (kk, j))],
        [pl.BlockSpec((tm, tn), lambda i, j, kk: (i, j))],
        [jax.ShapeDtypeStruct((m, n), BF16)],
        [pltpu.VMEM((tm, tn), F32)],
        (a, b), _params("arbitrary", "arbitrary", "arbitrary"), exchange)
    return outs if exchange else outs[0]


def _norm_proj(name, h, g, w):
    rows, d = h.shape
    n = w.shape[1]
    split = 3 * GROUP
    tm = _row_tile(rows)

    def body(h_ref, g_ref, w_ref, qkv_ref, p_ref, n_ref):
        nv, _, _ = _rms_fwd(h_ref[...], g_ref[...])
        nb = nv.astype(BF16)
        n_ref[...] = nb
        qkv_ref[...] = _dot(nb, w_ref[:, :split]).astype(BF16)
        p_ref[...] = _dot(nb, w_ref[:, split:])

    return pl.pallas_call(
        body, name=name, grid=(rows // tm,),
        in_specs=[pl.BlockSpec((tm, d), lambda i: (i, 0)),
                  pl.BlockSpec((1, d), lambda i: (0, 0)),
                  pl.BlockSpec((d, n), lambda i: (0, 0))],
        out_specs=[pl.BlockSpec((tm, split), lambda i: (i, 0)),
                   pl.BlockSpec((tm, n - split), lambda i: (i, 0)),
                   pl.BlockSpec((tm, d), lambda i: (i, 0))],
        out_shape=[jax.ShapeDtypeStruct((rows, split), BF16), jax.ShapeDtypeStruct((rows, n - split), F32),
                   jax.ShapeDtypeStruct((rows, d), BF16)],
        compiler_params=_params("arbitrary"),
    )(h, g, w)


def _out_proj(name, h, sb, rw, w):
    rows, d = h.shape
    gw = sb.shape[1]
    tm = _row_tile(rows)

    def body(h_ref, sb_ref, rw_ref, w_ref, o_ref, mix_ref):
        mix_ref[:, :gw] = sb_ref[...].astype(BF16)
        mix_ref[:, gw:] = rw_ref[...].astype(BF16)
        o_ref[...] = h_ref[...] + _dot(mix_ref[...], w_ref[...])

    return pl.pallas_call(
        body, name=name, grid=(rows // tm,),
        in_specs=[pl.BlockSpec((tm, d), lambda i: (i, 0)),
                  pl.BlockSpec((tm, gw), lambda i: (i, 0)),
                  pl.BlockSpec((tm, gw), lambda i: (i, 0)),
                  pl.BlockSpec((2 * gw, d), lambda i: (0, 0))],
        out_specs=[pl.BlockSpec((tm, d), lambda i: (i, 0)),
                   pl.BlockSpec((tm, 2 * gw), lambda i: (i, 0))],
        out_shape=[jax.ShapeDtypeStruct((rows, d), F32), jax.ShapeDtypeStruct((rows, 2 * gw), BF16)],
        compiler_params=_params("arbitrary"),
    )(h, sb, rw, w)


def _out_proj_bwd(name, dh, w):
    rows, d = dh.shape
    k = w.shape[0]
    tm = _row_tile(rows)

    def body(dh_ref, w_ref, dsb_ref, drw_ref, dhb_ref):
        dhb = dh_ref[...].astype(BF16)
        dhb_ref[...] = dhb
        dsb_ref[...] = _dot_nt(dhb, w_ref[:GROUP, :]).astype(BF16)
        drw_ref[...] = _dot_nt(dhb, w_ref[GROUP:, :])

    return pl.pallas_call(
        body, name=name, grid=(rows // tm,),
        in_specs=[pl.BlockSpec((tm, d), lambda i: (i, 0)),
                  pl.BlockSpec((k, d), lambda i: (0, 0))],
        out_specs=[pl.BlockSpec((tm, GROUP), lambda i: (i, 0)),
                   pl.BlockSpec((tm, GROUP), lambda i: (i, 0)),
                   pl.BlockSpec((tm, d), lambda i: (i, 0))],
        out_shape=[jax.ShapeDtypeStruct((rows, GROUP), BF16), jax.ShapeDtypeStruct((rows, GROUP), F32),
                   jax.ShapeDtypeStruct((rows, d), BF16)],
        compiler_params=_params("arbitrary"),
    )(dh, w)


def _norm_proj_bwd(name, dproj, w, h, g, dh):
    rows, n = dproj.shape
    d = w.shape[0]
    tm = _row_tile(rows)

    def body(dp_ref, w_ref, h_ref, g_ref, dh_ref, o_ref, dg_ref):
        @pl.when(pl.program_id(0) == 0)
        def _():
            dg_ref[...] = jnp.zeros_like(dg_ref)

        dn = _dot_nt(dp_ref[...], w_ref[...])
        gv = g_ref[...]
        _, xhat, rstd = _rms_fwd(h_ref[...], gv)
        dx, dg = _rms_bwd(dn, xhat, rstd, gv)
        o_ref[...] = dh_ref[...] + dx
        dg_ref[...] += dg

    return pl.pallas_call(
        body, name=name, grid=(rows // tm,),
        in_specs=[pl.BlockSpec((tm, n), lambda i: (i, 0)),
                  pl.BlockSpec((d, n), lambda i: (0, 0)),
                  pl.BlockSpec((tm, d), lambda i: (i, 0)),
                  pl.BlockSpec((1, d), lambda i: (0, 0)),
                  pl.BlockSpec((tm, d), lambda i: (i, 0))],
        out_specs=[pl.BlockSpec((tm, d), lambda i: (i, 0)),
                   pl.BlockSpec((1, d), lambda i: (0, 0))],
        out_shape=[jax.ShapeDtypeStruct((rows, d), F32), jax.ShapeDtypeStruct((1, d), F32)],
        compiler_params=_params("arbitrary"),
    )(dproj, w, h, g, dh)


def _loss_head(name, h, g, tgt):
    rows, d = h.shape
    tm = _row_tile(rows)

    def body(h_ref, g_ref, t_ref, loss_ref, dh_ref, dg_ref):
        i = pl.program_id(0)

        @pl.when(i == 0)
        def _():
            loss_ref[...] = jnp.zeros_like(loss_ref)
            dg_ref[...] = jnp.zeros_like(dg_ref)

        gv = g_ref[...]
        y, xhat, rstd = _rms_fwd(h_ref[...], gv)
        row = i * tm + lax.broadcasted_iota(jnp.int32, (tm, 1), 0)
        diff = jnp.where(row >= ROW0, y - t_ref[...], 0.0)
        part = 0.5 * jnp.sum(jnp.sum(diff * diff, axis=-1, keepdims=True), axis=0, keepdims=True) / d
        loss_ref[...] += jnp.broadcast_to(part, loss_ref.shape)
        dx, dg = _rms_bwd(diff / d, xhat, rstd, gv)
        dh_ref[...] = dx
        dg_ref[...] += dg

    return pl.pallas_call(
        body, name=name, grid=(rows // tm,),
        in_specs=[pl.BlockSpec((tm, d), lambda i: (i, 0)),
                  pl.BlockSpec((1, d), lambda i: (0, 0)),
                  pl.BlockSpec((tm, d), lambda i: (i, 0))],
        out_specs=[pl.BlockSpec((8, 128), lambda i: (0, 0)),
                   pl.BlockSpec((tm, d), lambda i: (i, 0)),
                   pl.BlockSpec((1, d), lambda i: (0, 0))],
        out_shape=[jax.ShapeDtypeStruct((8, 128), F32),
                   jax.ShapeDtypeStruct((rows, d), F32),
                   jax.ShapeDtypeStruct((1, d), F32)],
        compiler_params=_params("arbitrary"),
    )(h, g, tgt)


def _sb_block(qb, kb, q0, jb, scale):
    bq, bk = qb.shape[0], kb.shape[0]
    z = _dot_nt(qb, kb) * scale
    qpos = q0 + lax.broadcasted_iota(jnp.int32, (bq, bk), 0)
    kpos = jb * bk + lax.broadcasted_iota(jnp.int32, (bq, bk), 1)
    valid = (kpos < qpos) & (kpos >= META_PAD)
    e = jnp.exp(-jnp.abs(z))
    log_keep = jnp.where(valid, -(jnp.maximum(z, 0.0) + jnp.log(1.0 + e)), 0.0)
    return z, valid, e, log_keep


def _tri2(n, cmp):
    r = lax.broadcasted_iota(jnp.int32, (2 * n, n), 0) % n
    c = lax.broadcasted_iota(jnp.int32, (2 * n, n), 1)
    return cmp(r, c).astype(BF16)


def _dot_split(x, t2):
    hi, lo = _split2(x)
    return _dot(jnp.concatenate([hi, lo], axis=1), t2)


ATT_HEADS = 8
ATT_WIDTH = ATT_HEADS * HEAD
ATT_CUT = -104.0
ATT_TILES = GROUP // ATT_WIDTH


def _lanes(hh):
    return slice(hh * HEAD, (hh + 1) * HEAD)


def _exchange_steps(grid):
    step, total = pl.program_id(0), 1
    for a in range(1, len(grid)):
        step = step * grid[a] + pl.program_id(a)
    for size in grid:
        total *= size
    return step == 0, step == max(total - 2, 0), step == total - 1


def _sb_fwd(name, qkv, shards=()):
    rows = qkv.shape[0]
    nh, dh = N_HEADS, HEAD
    bq, bk, hg = _row_tile(rows), ATT_BLOCK, ATT_HEADS
    per = bq // bk
    scale = dh ** -0.5
    ns = len(shards)
    grid = (nh // hg, rows // bq)

    def body(q_ref, k_ref, v_ref, *rest):
        o_ref, rt_ref, cnt_ref = rest[ns:ns + 3]
        if ns:
            first, before_last, last = _exchange_steps(grid)
            start, forward, finish = _gather_exchange(rest[:ns], rest[ns + 3:2 * ns + 3], rest[2 * ns + 3:])
            pl.when(first)(start)
        i = pl.program_id(1)
        after = _tri2(bk, lambda r, c: r > c)
        nkb = (i + 1) * per

        def live(state):
            n, carry = state
            top = jnp.max(carry[0][0])
            for hh in range(1, hg):
                top = jnp.maximum(top, jnp.max(carry[hh][0]))
            return (n < nkb) & (top >= ATT_CUT)

        def visit(carry, jb, r0):
            off = pl.multiple_of(jb * bk, bk)
            out = []
            for hh in range(hg):
                rest, acc = carry[hh]
                kb = k_ref[pl.ds(off, bk), _lanes(hh)]
                vb = v_ref[pl.ds(off, bk), _lanes(hh)]
                z, valid, _, log_keep = _sb_block(q_ref[r0:, _lanes(hh)], kb, i * bq + r0, jb, scale)
                log_rest = rest[r0:] + _dot_split(log_keep, after)
                attn = jnp.where(valid, jnp.exp(z + log_keep + log_rest), 0.0)
                new_rest = rest[r0:] + jnp.sum(log_keep, axis=-1, keepdims=True)
                new_acc = acc[r0:] + _dot(attn.astype(BF16), vb)
                if r0:
                    new_rest = jnp.concatenate([rest[:r0], new_rest], axis=0)
                    new_acc = jnp.concatenate([acc[:r0], new_acc], axis=0)
                out.append((new_rest, new_acc))
            return tuple(out)

        carry = tuple((jnp.zeros((bq, 1), F32), jnp.zeros((bq, dh), F32)) for _ in range(hg))
        for dgl in reversed(range(per)):
            carry = visit(carry, i * per + dgl, dgl * bk)
        n, res = lax.while_loop(live, lambda s: (s[0] + 1, visit(s[1], nkb - 1 - s[0], 0)), (jnp.int32(per), carry))
        for hh in range(hg):
            rt_ref[hh] = res[hh][0]
            o_ref[:, _lanes(hh)] = res[hh][1]
            cnt_ref[hh] = jnp.full((bq, 1), n, F32)
        if ns:
            pl.when(before_last)(forward)
            pl.when(last)(finish)

    return pl.pallas_call(
        body, name=name, grid=grid,
        in_specs=[pl.BlockSpec((bq, ATT_WIDTH), lambda h, i: (i, h)),
                  pl.BlockSpec((rows, ATT_WIDTH), lambda h, i: (0, ATT_TILES + h)),
                  pl.BlockSpec((rows, ATT_WIDTH), lambda h, i: (0, 2 * ATT_TILES + h))] + [HBM] * ns,
        out_specs=[pl.BlockSpec((bq, ATT_WIDTH), lambda h, i: (i, h)),
                   pl.BlockSpec((hg, bq, 1), lambda h, i: (h, i, 0)),
                   pl.BlockSpec((hg, bq, 1), lambda h, i: (h, i, 0))] + [HBM] * ns,
        out_shape=[jax.ShapeDtypeStruct((rows, GROUP), F32), jax.ShapeDtypeStruct((nh, rows, 1), F32),
                   jax.ShapeDtypeStruct((nh, rows, 1), F32)]
        + [jax.ShapeDtypeStruct((N_CHIPS,) + s.shape, s.dtype) for s in shards],
        scratch_shapes=_gather_sems(ns) if ns else [],
        compiler_params=_params("arbitrary", "arbitrary"),
    )(qkv, qkv, qkv, *shards)


def _sb_bwd(name, qkv, rt, cnt, do, parts=()):
    rows = qkv.shape[0]
    nh, dh = N_HEADS, HEAD
    bq, bk, hg = _row_tile(rows), ATT_BLOCK, ATT_HEADS
    per = bq // bk
    scale = dh ** -0.5
    ns = len(parts)
    grid = (nh // hg, rows // bq)

    def body(q_ref, k_ref, v_ref, rt_ref, cnt_ref, do_ref, *rest):
        dq_ref, dk_ref, dv_ref = rest[ns:ns + 3]
        if ns:
            at_first, before_last, at_last = _exchange_steps(grid)
            start, forward, finish = _reduce_exchange(rest[:ns], rest[ns + 3:2 * ns + 3], rest[2 * ns + 3:3 * ns + 3],
                                             rest[3 * ns + 3:])
            pl.when(at_first)(start)
        i = pl.program_id(1)

        @pl.when(i == 0)
        def _():
            dk_ref[...] = jnp.zeros_like(dk_ref)
            dv_ref[...] = jnp.zeros_like(dv_ref)

        upto = _tri2(bk, lambda r, c: r <= c)
        before = _tri2(bk, lambda r, c: r < c)
        nkb = (i + 1) * per
        first = nkb - jnp.max(cnt_ref[0]).astype(jnp.int32)

        def visit(carry, jb, r0):
            off = pl.multiple_of(jb * bk, bk)
            out = []
            for hh in range(hg):
                keep_sum, g_sum, dq = carry[hh]
                qb, dob = q_ref[r0:, _lanes(hh)], do_ref[r0:, _lanes(hh)]
                kb = k_ref[pl.ds(off, bk), _lanes(hh)]
                vb = v_ref[pl.ds(off, bk), _lanes(hh)]
                z, valid, e, log_keep = _sb_block(qb, kb, i * bq + r0, jb, scale)
                log_rest = rt_ref[hh, r0:, :] - keep_sum[r0:] - _dot_split(log_keep, upto)
                attn = jnp.where(valid, jnp.exp(z + log_keep + log_rest), 0.0)
                g = attn * _dot_nt(dob, vb)
                g_before = g_sum[r0:] + _dot_split(g, before)
                inv = 1.0 / (1.0 + e)
                sig = jnp.where(z >= 0, inv, e * inv)
                dz = (jnp.where(valid, g * (1.0 - sig) - g_before * sig, 0.0) * scale).astype(BF16)
                dk_ref[pl.ds(off, bk), _lanes(hh)] += _dot_tn(dz, qb)
                dv_ref[pl.ds(off, bk), _lanes(hh)] += _dot_tn(attn.astype(BF16), dob)
                new = (keep_sum[r0:] + jnp.sum(log_keep, axis=-1, keepdims=True),
                       g_sum[r0:] + jnp.sum(g, axis=-1, keepdims=True),
                       dq[r0:] + _dot(dz, kb))
                if r0:
                    new = tuple(jnp.concatenate([old[:r0], x], axis=0) for old, x in zip(carry[hh], new))
                out.append(new)
            return tuple(out)

        zero = jnp.zeros((bq, 1), F32)
        res = lax.fori_loop(first, nkb - per, lambda jb, c: visit(c, jb, 0),
                            tuple((zero, zero, jnp.zeros((bq, dh), F32)) for _ in range(hg)))
        for dgl in range(per):
            res = visit(res, i * per + dgl, dgl * bk)
        for hh in range(hg):
            dq_ref[:, _lanes(hh)] = res[hh][2]
        if ns:
            pl.when(before_last)(forward)
            pl.when(at_last)(finish)

    return pl.pallas_call(
        body, name=name, grid=grid,
        in_specs=[pl.BlockSpec((bq, ATT_WIDTH), lambda h, i: (i, h)),
                  pl.BlockSpec((rows, ATT_WIDTH), lambda h, i: (0, ATT_TILES + h)),
                  pl.BlockSpec((rows, ATT_WIDTH), lambda h, i: (0, 2 * ATT_TILES + h)),
                  pl.BlockSpec((hg, bq, 1), lambda h, i: (h, i, 0)),
                  pl.BlockSpec((hg, bq, 1), lambda h, i: (h, i, 0)),
                  pl.BlockSpec((bq, ATT_WIDTH), lambda h, i: (i, h))] + [HBM] * ns,
        out_specs=[pl.BlockSpec((bq, ATT_WIDTH), lambda h, i: (i, h)),
                   pl.BlockSpec((rows, ATT_WIDTH), lambda h, i: (0, h)),
                   pl.BlockSpec((rows, ATT_WIDTH), lambda h, i: (0, h))] + [HBM] * (2 * ns),
        out_shape=[jax.ShapeDtypeStruct((rows, GROUP), F32)] * 3
        + [jax.ShapeDtypeStruct(s.shape, s.dtype) for s in parts] * 2,
        scratch_shapes=_reduce_sems(ns) if ns else [],
        compiler_params=_params("arbitrary", "arbitrary"),
    )(qkv, qkv, qkv, rt, cnt, do, *parts)


def _head_sum(x, ones_bd):
    return _dot_split(x, ones_bd)


def _rwkv_pre(p, p_prev, mu, w0, a0, k_k, k_a, w_up, a_up, g_up, ones_bd):
    xs = p + (p_prev - p) * mu
    r = xs[:, :GROUP]
    k0 = xs[:, GROUP:2 * GROUP]
    v = xs[:, 2 * GROUP:3 * GROUP]
    lo = xs[:, 3 * GROUP:]
    wa = w0 + _dot(jnp.tanh(lo).astype(BF16), w_up.astype(BF16))
    w = -(jnp.maximum(-wa, 0.0) + jnp.log(1.0 + jnp.exp(-jnp.abs(wa)))) - 0.5
    log_decay = -jnp.exp(w)
    alpha = _sigmoid(a0 + _dot(lo.astype(BF16), a_up.astype(BF16)))
    gate = _dot(_sigmoid(lo).astype(BF16), g_up.astype(BF16))
    kk = k0 * k_k
    kk = kk * lax.rsqrt(jnp.maximum(_head_sum(kk * kk, ones_bd), 1e-24))
    k = k0 * (1.0 + (alpha - 1.0) * k_a)
    return r, log_decay, k, v, -kk, kk * alpha, gate


def _rwkv_post(y, r, k, v, gate, lnx_w, lnx_b, r_k, ones_bd):
    mean = _head_sum(y, ones_bd) * (1.0 / HEAD)
    yc = y - mean
    var = _head_sum(yc * yc, ones_bd) * (1.0 / HEAD)
    yn = yc * lax.rsqrt(var + LNX_EPS) * lnx_w + lnx_b
    bonus = _head_sum(r * k * r_k, ones_bd) * v
    return (yn + bonus) * gate


_PRE_VEC = 5
_PRE_MAT = 3


def _heads(x):
    return jnp.stack([x[:, _lanes(h)] for h in range(N_HEADS)])


def _unheads(x):
    return jnp.concatenate([x[h] for h in range(N_HEADS)], axis=1)


def _edge_spec(tm, width, tile_of):
    return pl.BlockSpec((8, width), lambda i: (jnp.maximum(tile_of(i) * (tm // 8) - 1, 0), 0))


def _previous_rows(p_ref, edge_ref, tile):
    p = p_ref[...]
    edge = jnp.where(tile == 0, 0.0, edge_ref[7:8, :])
    row = lax.broadcasted_iota(jnp.int32, (p.shape[0], 1), 0)
    return jnp.where(row == 0, edge, pltpu.roll(p, 1, axis=0))


def _rwkv_pre_fwd(name, p, vecs, mats, ones_bd):
    rows = p.shape[0]
    tm = _row_tile(rows)
    row_spec = lambda w: pl.BlockSpec((tm, w), lambda i: (i, 0))
    full = lambda a: pl.BlockSpec(a.shape, lambda i: (0,) * a.ndim)

    def body(p_ref, edge_ref, *refs):
        ins = [r[...] for r in refs[:_PRE_VEC + _PRE_MAT + 1]]
        outs = refs[_PRE_VEC + _PRE_MAT + 1:]
        prev = _previous_rows(p_ref, edge_ref, pl.program_id(0))
        for o_ref, val in zip(outs, _rwkv_pre(p_ref[...], prev, *ins)):
            o_ref[...] = val

    return pl.pallas_call(
        body, name=name, grid=(rows // tm,),
        in_specs=([row_spec(RW_COLS), _edge_spec(tm, RW_COLS, lambda i: i)]
                  + [full(a) for a in (*vecs, *mats, ones_bd)]),
        out_specs=[row_spec(GROUP)] * 7,
        out_shape=[jax.ShapeDtypeStruct((rows, GROUP), F32)] * 7,
        compiler_params=_params("arbitrary"),
    )(p, p, *vecs, *mats, ones_bd)


def _rwkv_pre_bwd(name, p, vecs, mats, ones_bd, cts_scan, ct_gate, cts_b):
    rows = p.shape[0]
    tm = _row_tile(rows)
    nt = rows // tm
    n_par = _PRE_VEC + _PRE_MAT
    tile_of = lambda i: nt - 1 - i
    row_spec = lambda w: pl.BlockSpec((tm, w), lambda i: (tile_of(i), 0))
    full = lambda a: pl.BlockSpec(a.shape, lambda i: (0,) * a.ndim)

    def body(*refs):
        p_ref, edge_ref = refs[0], refs[1]
        par = [r[...] for r in refs[2:2 + n_par]]
        ones = refs[2 + n_par][...]
        cta = [r[...] for r in refs[3 + n_par:10 + n_par]]
        ctb = [r[...] for r in refs[10 + n_par:13 + n_par]]
        dp_ref, par_outs, carry = refs[13 + n_par], refs[14 + n_par:-1], refs[-1]
        step = pl.program_id(0)

        @pl.when(step == 0)
        def _():
            carry[...] = jnp.zeros_like(carry)
            for o_ref in par_outs:
                o_ref[...] = jnp.zeros_like(o_ref)

        ct = (cta[0] + ctb[0], cta[1], cta[2] + ctb[1], cta[3] + ctb[2], cta[4], cta[5], cta[6])
        _, vjp = jax.vjp(lambda pv, ppv, *pr: _rwkv_pre(pv, ppv, *pr, ones),
                         p_ref[...], _previous_rows(p_ref, edge_ref, tile_of(step)), *par)
        grads = vjp(ct)
        row = lax.broadcasted_iota(jnp.int32, (tm, 1), 0)
        dp_ref[...] = grads[0] + jnp.where(row == tm - 1, carry[0:1, :], pltpu.roll(grads[1], tm - 1, axis=0))
        carry[0:1, :] = grads[1][0:1, :]
        for o_ref, gval in zip(par_outs, grads[2:]):
            o_ref[...] += gval

    par_arrays = (*vecs, *mats)
    return pl.pallas_call(
        body, name=name, grid=(nt,),
        in_specs=([row_spec(RW_COLS), _edge_spec(tm, RW_COLS, tile_of)] + [full(a) for a in (*par_arrays, ones_bd)]
                  + [row_spec(GROUP)] * 10),
        out_specs=[row_spec(RW_COLS)] + [full(a) for a in par_arrays],
        out_shape=[jax.ShapeDtypeStruct((rows, RW_COLS), F32)] + [jax.ShapeDtypeStruct(a.shape, F32) for a in par_arrays],
        scratch_shapes=[pltpu.VMEM((8, RW_COLS), F32)],
        compiler_params=_params("arbitrary"),
    )(p, p, *par_arrays, ones_bd, *cts_scan, ct_gate, *cts_b)


def _rwkv_post_fwd(name, y, r, k, v, gate, vecs, ones_bd):
    rows = r.shape[0]
    tm = _row_tile(rows)
    row_spec = pl.BlockSpec((tm, GROUP), lambda i: (i, 0))
    full = lambda a: pl.BlockSpec(a.shape, lambda i: (0,) * a.ndim)

    def body(*refs):
        refs[-1][...] = _rwkv_post(*(r_[...] for r_ in refs[:-1]))

    return pl.pallas_call(
        body, name=name, grid=(rows // tm,),
        in_specs=[row_spec] * 5 + [full(a) for a in (*vecs, ones_bd)],
        out_specs=row_spec,
        out_shape=jax.ShapeDtypeStruct((rows, GROUP), F32),
        compiler_params=_params("arbitrary"),
    )(y, r, k, v, gate, *vecs, ones_bd)


def _rwkv_post_bwd(name, y, r, k, v, gate, vecs, ones_bd, dout):
    rows = r.shape[0]
    tm = _row_tile(rows)
    row_spec = pl.BlockSpec((tm, GROUP), lambda i: (i, 0))
    full = lambda a: pl.BlockSpec(a.shape, lambda i: (0,) * a.ndim)

    def body(*refs):
        vals = [r_[...] for r_ in refs[:8]]
        ones = refs[8][...]
        dout_v = refs[9][...]
        outs = refs[10:]
        _, vjp = jax.vjp(lambda *a: _rwkv_post(*a, ones), *vals)
        grads = vjp(dout_v)
        for o_ref, gval in zip(outs[:5], grads[:5]):
            o_ref[...] = gval

        @pl.when(pl.program_id(0) == 0)
        def _():
            for o_ref in outs[5:]:
                o_ref[...] = jnp.zeros_like(o_ref)

        for o_ref, gval in zip(outs[5:], grads[5:]):
            o_ref[...] += gval

    return pl.pallas_call(
        body, name=name, grid=(rows // tm,),
        in_specs=[row_spec] * 5 + [full(a) for a in (*vecs, ones_bd)] + [row_spec],
        out_specs=[row_spec] * 5 + [full(a) for a in vecs],
        out_shape=[jax.ShapeDtypeStruct((rows, GROUP), F32)] * 5 + [jax.ShapeDtypeStruct(a.shape, F32) for a in vecs],
        compiler_params=_params("arbitrary"),
    )(y, r, k, v, gate, *vecs, ones_bd, dout)


_NN = (((2,), (1,)), ((0,), (0,)))
_NT = (((2,), (2,)), ((0,), (0,)))
_TN = (((1,), (1,)), ((0,), (0,)))


_BWD_FORMS = {"nn": (("nt", False), ("tn", False)),
              "nt": (("nn", False), ("tn", True)),
              "tn": (("nt", True), ("nn", False))}
_DIMS = {"nn": _NN, "nt": _NT, "tn": _TN}


def _bdot(a, b, form):
    return lax.dot_general(a.astype(BF16), b.astype(BF16), _DIMS[form], preferred_element_type=F32)


@functools.partial(jax.custom_vjp, nondiff_argnums=(2,))
def _bmm(a, b, form):
    return _bdot(a, b, form)


def _bmm_fwd(a, b, form):
    return _bdot(a, b, form), (a.astype(BF16), b.astype(BF16))


def _bmm_bwd(form, res, dc):
    a, b = res
    (fa, swap_a), (fb, swap_b) = _BWD_FORMS[form]
    da = _bdot(b, dc, fa) if swap_a else _bdot(dc, b, fa)
    db = _bdot(dc, a, fb) if swap_b else _bdot(a, dc, fb)
    return da, db


_bmm.defvjp(_bmm_fwd, _bmm_bwd)


@jax.custom_vjp
def _cumsum_steps(x):
    return _tri_apply(x, lambda r, c: r >= c)


def _tri_apply(x, cmp):
    nh, c, _ = x.shape
    tri = cmp(lax.broadcasted_iota(jnp.int32, (c, c), 0), lax.broadcasted_iota(jnp.int32, (c, c), 1))
    tri = jnp.broadcast_to(tri.astype(BF16)[None], (nh, c, c))
    hi, lo = _split2(x)
    return (lax.dot_general(tri, hi, _NN, preferred_element_type=F32)
            + lax.dot_general(tri, lo, _NN, preferred_element_type=F32))


_cumsum_steps.defvjp(lambda x: (_cumsum_steps(x), None), lambda _, d: (_tri_apply(d, lambda r, c: r <= c),))


@jax.custom_vjp
def _neumann(n_mat):
    c = n_mat.shape[1]
    inv, power, span = n_mat, _bmm(n_mat, n_mat, "nn"), 2
    while span < c:
        both = _bmm(jnp.concatenate([power, inv], axis=1), power, "nn")
        inv = inv + power + both[:, c:]
        power = both[:, :c]
        span *= 2
    return inv


def _neumann_fwd(n_mat):
    inv = _neumann(n_mat)
    return inv, inv


def _neumann_bwd(inv, d):
    left = d + _bmm(inv, d, "tn")
    return (left + _bmm(left, inv, "nt"),)


_neumann.defvjp(_neumann_fwd, _neumann_bwd)


def _chunk(state, r, log_w, k, v, a, b):
    nh, c, _ = r.shape
    row = lax.broadcasted_iota(jnp.int32, (c, c), 0)
    col = lax.broadcasted_iota(jnp.int32, (c, c), 1)
    cum = _cumsum_steps(log_w)
    mid = cum[:, c // 2 - 1:c // 2, :]
    a_t = a * jnp.exp(cum - log_w - mid)
    r_t = r * jnp.exp(cum - mid)
    back = jnp.exp(mid - cum)
    b_t = b * back
    k_t = k * back
    strict, incl = (row > col)[None], (row >= col)[None]
    ar = jnp.concatenate([a_t, r_t], axis=1)
    on_b = _bmm(ar, b_t, "nt")
    on_k = _bmm(ar, k_t, "nt")
    n_mat = jnp.where(strict, on_b[:, :c], 0.0)
    p_mat = jnp.where(incl, on_b[:, c:], 0.0)
    m_mat = jnp.where(strict, on_k[:, :c], 0.0)
    q_mat = jnp.where(incl, on_k[:, c:], 0.0)
    inv = _neumann(n_mat)
    s_mid = state * jnp.swapaxes(jnp.exp(mid), 1, 2)
    x = _bmm(jnp.concatenate([a_t, m_mat], axis=2), jnp.concatenate([s_mid, v], axis=1), "nn")
    u = x + _bmm(inv, x, "nn")
    y = _bmm(jnp.concatenate([r_t, p_mat, q_mat], axis=2), jnp.concatenate([s_mid, u, v], axis=1), "nn")
    grown = _bmm(jnp.concatenate([b_t, k_t], axis=1), jnp.concatenate([u, v], axis=1), "tn")
    s_new = (s_mid + grown) * jnp.swapaxes(jnp.exp(cum[:, c - 1:c, :] - mid), 1, 2)
    return y, s_new


def _scan_fwd(name, ops):
    rows = ops[0].shape[0]
    nh, dh = N_HEADS, HEAD
    nc, per = rows // CHUNK, SCAN_CHUNKS
    spec = pl.BlockSpec((per * CHUNK, GROUP), lambda c: (c, 0))

    def body(r_ref, w_ref, k_ref, v_ref, a_ref, b_ref, y_ref, st_ref, state):
        @pl.when(pl.program_id(0) == 0)
        def _():
            state[...] = jnp.zeros_like(state)

        s = state[...]
        for u in range(per):
            at = slice(u * CHUNK, (u + 1) * CHUNK)
            st_ref[u] = s
            y, s = _chunk(s, *(_heads(ref[at, :]) for ref in (r_ref, w_ref, k_ref, v_ref, a_ref, b_ref)))
            y_ref[at, :] = _unheads(y)
        state[...] = s

    return pl.pallas_call(
        body, name=name, grid=(nc // per,),
        in_specs=[spec] * 6,
        out_specs=[spec, pl.BlockSpec((per, nh, dh, dh), lambda c: (c, 0, 0, 0))],
        out_shape=[jax.ShapeDtypeStruct((rows, GROUP), F32), jax.ShapeDtypeStruct((nc, nh, dh, dh), F32)],
        scratch_shapes=[pltpu.VMEM((nh, dh, dh), F32)],
        compiler_params=_params("arbitrary"),
    )(*ops)


def _scan_bwd(name, ops, states, dy):
    rows = ops[0].shape[0]
    nh, dh = N_HEADS, HEAD
    nc, per = rows // CHUNK, SCAN_CHUNKS
    steps = nc // per
    spec = pl.BlockSpec((per * CHUNK, GROUP), lambda c: (steps - 1 - c, 0))

    def body(r_ref, w_ref, k_ref, v_ref, a_ref, b_ref, st_ref, dy_ref, *rest):
        outs, dstate = rest[:6], rest[6]

        @pl.when(pl.program_id(0) == 0)
        def _():
            dstate[...] = jnp.zeros_like(dstate)

        ds = dstate[...]
        for u in reversed(range(per)):
            at = slice(u * CHUNK, (u + 1) * CHUNK)
            _, vjp = jax.vjp(_chunk, st_ref[u],
                             *(_heads(ref[at, :]) for ref in (r_ref, w_ref, k_ref, v_ref, a_ref, b_ref)))
            grads = vjp((_heads(dy_ref[at, :]), ds))
            ds = grads[0]
            for o_ref, gval in zip(outs, grads[1:]):
                o_ref[at, :] = _unheads(gval)
        dstate[...] = ds

    return pl.pallas_call(
        body, name=name, grid=(steps,),
        in_specs=[spec] * 6 + [pl.BlockSpec((per, nh, dh, dh), lambda c: (steps - 1 - c, 0, 0, 0)), spec],
        out_specs=[spec] * 6,
        out_shape=[jax.ShapeDtypeStruct((rows, GROUP), F32)] * 6,
        scratch_shapes=[pltpu.VMEM((nh, dh, dh), F32)],
        compiler_params=_params("arbitrary"),
    )(*ops, states, dy)


def _pad_cols(x, cols):
    return jnp.concatenate([x, jnp.zeros(x.shape[:-1] + (cols - x.shape[-1],), x.dtype)], axis=-1)


def _lora_pad(w_up, a_up, g_up):
    z = lambda n: jnp.zeros((n, GROUP), F32)
    return (jnp.concatenate([w_up, z(LORA_PAD - LORA_W)], 0),
            jnp.concatenate([z(LORA_W), a_up, z(LORA_PAD - LORA_W - LORA_A)], 0),
            jnp.concatenate([z(LORA_W + LORA_A), g_up, z(LORA_PAD - LORA_W - LORA_A - LORA_G)], 0))


MID = ['w_in']
LATE = ['ffn2_w_gate', 'ffn2_w_up', 'ffn2_w_down', 'w_out']


def _local_step(x, tgt, w, late=None):
    d = x.shape[1]
    zeros = jnp.zeros((META_PAD, d), F32)
    h0 = jnp.concatenate([zeros, w["meta_tokens"], x], axis=0)
    tgt_p = jnp.concatenate([jnp.zeros((ROW0, d), F32), tgt], axis=0)
    ones_bd = ((lax.broadcasted_iota(jnp.int32, (2 * GROUP, GROUP), 0) % GROUP) // HEAD
               == lax.broadcasted_iota(jnp.int32, (2 * GROUP, GROUP), 1) // HEAD).astype(BF16)
    pre_vecs = (_pad_cols(w["rwkv_mu"], RW_COLS), w["rwkv_w0"], w["rwkv_a0"], w["rwkv_k_k"], w["rwkv_k_a"])
    pre_mats = _lora_pad(w["rwkv_w_up"], w["rwkv_a_up"], w["rwkv_g_up"])
    post_vecs = (w["rwkv_lnx_w"], w["rwkv_lnx_b"], w["rwkv_r_k"].reshape(1, GROUP))

    h1, a1, b1, *gathered = _ffn_fwd("ffn1_fwd", h0, w["ffn1_norm"], w["ffn1_w_gate"], w["ffn1_w_up"],
                                     w["ffn1_w_down"], late and ("gather", late.shards["mid"]))
    if late is not None:
        w = {**w, **late.join("mid", gathered)}
    w_in = _pad_cols(w["w_in"], IN_COLS_PAD)
    qkv, p, n2 = _norm_proj("in_proj", h1, w["mix_norm"], w_in)
    sb, rest_total, visited, *gathered = _sb_fwd("sb_fwd", qkv, late.shards["late"] if late else ())
    if late is not None:
        w = {**w, **late.join("late", gathered)}
    pre = _rwkv_pre_fwd("rwkv_pre_fwd", p, pre_vecs, pre_mats, ones_bd)
    scan_ops, token_ops = pre[:6], (pre[0], pre[2], pre[3], pre[6])
    y, states = _scan_fwd("rwkv_scan_fwd", scan_ops)
    rw = _rwkv_post_fwd("rwkv_post_fwd", y, *token_ops, post_vecs, ones_bd)
    h2, mix = _out_proj("out_proj", h1, sb, rw, w["w_out"])
    h3, a2, b2 = _ffn_fwd("ffn2_fwd", h2, w["ffn2_norm"], w["ffn2_w_gate"], w["ffn2_w_up"], w["ffn2_w_down"])
    loss8, dh3, g_final = _loss_head("loss_head", h3, w["final_norm"].reshape(1, d), tgt_p)

    g = {"final_norm": g_final.reshape(d)}
    dh2, da2, db2, s2, n3, dhh3, g["ffn2_norm"] = _ffn_bwd(
        "ffn2_bwd", dh3, h2, w["ffn2_norm"], a2, b2, w["ffn2_w_gate"], w["ffn2_w_up"], w["ffn2_w_down"])
    g["ffn2_w_gate"] = _mm_tn("ffn2_dgate", da2, n3)
    g["ffn2_w_up"] = _mm_tn("ffn2_dup", db2, n3)
    g["ffn2_w_down"] = _mm_tn("ffn2_ddown", s2, dhh3)
    dsb, drw, dh2b = _out_proj_bwd("out_proj_bwd", dh2, w["w_out"])
    g["w_out"] = _mm_tn("out_proj_dw", mix, dh2b)
    dq, dk, dv, *reduced_late = _sb_bwd("sb_bwd", qkv, rest_total, visited, dsb, late.parts("late", g) if late else ())
    post_g = _rwkv_post_bwd("rwkv_post_bwd", y, *token_ops, post_vecs, ones_bd, drw)
    g["rwkv_lnx_w"], g["rwkv_lnx_b"] = post_g[5], post_g[6]
    g["rwkv_r_k"] = post_g[7].reshape(1, N_HEADS, HEAD)
    scan_g = _scan_bwd("rwkv_scan_bwd", scan_ops, states, post_g[0])
    pre_g = _rwkv_pre_bwd("rwkv_pre_bwd", p, pre_vecs, pre_mats, ones_bd, scan_g, post_g[4], post_g[1:4])
    dp = pre_g[0]
    g["rwkv_mu"] = pre_g[1][:, :w["rwkv_mu"].shape[1]]
    g["rwkv_w0"], g["rwkv_a0"], g["rwkv_k_k"], g["rwkv_k_a"] = pre_g[2:6]
    g["rwkv_w_up"] = pre_g[6][:LORA_W]
    g["rwkv_a_up"] = pre_g[7][LORA_W:LORA_W + LORA_A]
    g["rwkv_g_up"] = pre_g[8][LORA_W + LORA_A:LORA_W + LORA_A + LORA_G]
    live = (jnp.arange(h0.shape[0]) >= META_PAD)[:, None]
    dproj = jnp.where(live, jnp.concatenate([dq, dk, dv, dp], axis=1), 0.0).astype(BF16)
    g["w_in"] = _mm_tn("in_proj_dw", n2, dproj)[:, :w["w_in"].shape[1]]
    dh1, g["mix_norm"] = _norm_proj_bwd("in_proj_bwd", dproj, w_in, h1, w["mix_norm"], dh2)
    dh0, da1, db1, s1, n1, dhh1, g["ffn1_norm"], *reduced_mid = _ffn_bwd(
        "ffn1_bwd", dh1, h0, w["ffn1_norm"], a1, b1, w["ffn1_w_gate"], w["ffn1_w_up"], w["ffn1_w_down"],
        late and ("reduce", late.parts("mid", g)))
    g["meta_tokens"] = dh0[META_PAD:ROW0]
    reduced = {"mid": reduced_mid, "late": reduced_late}
    if late is None:
        g["ffn1_w_gate"] = _mm_tn("ffn1_dgate", da1, n1)
        g["ffn1_w_up"] = _mm_tn("ffn1_dup", db1, n1)
        g["ffn1_w_down"] = _mm_tn("ffn1_ddown", s1, dhh1)
    else:
        g["ffn1_w_gate"], reduced["small"] = _mm_tn("ffn1_dgate", da1, n1, ("all_reduce", [late.small(g, loss8[0, 0])]))
        g["ffn1_w_up"], *reduced["gate"] = _mm_tn("ffn1_dup", db1, n1, ("reduce", late.parts("gate", g)))
        g["ffn1_w_down"], *reduced["up"] = _mm_tn("ffn1_ddown", s1, dhh1, ("reduce", late.parts("up", g)))
    return loss8[0, 0], dh0[ROW0:], g, reduced


N_CHIPS = 4
N_DEV = 8
HBM = pl.BlockSpec(memory_space=pltpu.HBM)


def _place():
    return lax.axis_index("x"), lax.axis_index("y"), lax.axis_index("c")


def _other_chips(x, y):
    return [(1 - x, y), (x, 1 - y), (1 - x, 1 - y)]


def _gather_sems(n):
    return [pltpu.SemaphoreType.DMA((3 * n,)), pltpu.SemaphoreType.DMA((3 * n,)), pltpu.SemaphoreType.DMA((n,)),
            pltpu.SemaphoreType.DMA((3 * n,)), pltpu.SemaphoreType.DMA((3 * n,))]


def _gather_exchange(ins, outs, sems):
    n = len(ins)
    half = [r.shape[0] // 2 for r in ins]
    send, recv, local, d2d_send, d2d_recv = sems
    x, y, c = _place()
    me = 2 * x + y
    chips = _other_chips(x, y)

    def rows_of(k, h):
        return pl.ds(pl.multiple_of(h * half[k], 8), half[k])

    def own(k):
        return pltpu.make_async_copy(ins[k], outs[k].at[me], local.at[k])

    def copy(j, k, slot):
        return pltpu.make_async_remote_copy(
            src_ref=ins[k].at[rows_of(k, c)], dst_ref=outs[k].at[slot, rows_of(k, c)],
            send_sem=send.at[j * n + k], recv_sem=recv.at[j * n + k],
            device_id=(chips[j][0], chips[j][1], c), device_id_type=MESH)

    def passed(j, k, h):
        slot = 2 * chips[j][0] + chips[j][1]
        return pltpu.make_async_remote_copy(
            src_ref=outs[k].at[slot, rows_of(k, h)], dst_ref=outs[k].at[slot, rows_of(k, h)],
            send_sem=d2d_send.at[j * n + k], recv_sem=d2d_recv.at[j * n + k],
            device_id=(x, y, 1 - c), device_id_type=MESH)

    def start():
        for k in range(n):
            own(k).start()
        for j in range(3):
            for k in range(n):
                copy(j, k, me).start()

    def forward():
        for j in range(3):
            for k in range(n):
                copy(j, k, 2 * chips[j][0] + chips[j][1]).wait_recv()
                passed(j, k, c).start()

    def finish():
        for j in range(3):
            for k in range(n):
                passed(j, k, 1 - c).wait_recv()
        for j in range(3):
            for k in range(n):
                copy(j, k, me).wait_send()
                passed(j, k, c).wait_send()
        for k in range(n):
            own(k).wait()

    return start, forward, finish


def _gather_shards(name, shards):
    n = len(shards)

    def body(*refs):
        for stage in _gather_exchange(refs[:n], refs[n:2 * n], refs[2 * n:]):
            stage()

    return pl.pallas_call(
        body, name=name,
        in_specs=[HBM] * n, out_specs=[HBM] * n,
        out_shape=[jax.ShapeDtypeStruct((N_CHIPS,) + s.shape, s.dtype) for s in shards],
        scratch_shapes=_gather_sems(n),
    )(*shards)


def _pair_sum(name, part):
    nch, rows, cols = part.shape
    half = rows // 2

    def body(p_hbm, p_ref, out_ref, landed, send, recv):
        j = pl.program_id(0)
        x, y, c = _place()

        def copy(k):
            theirs = pl.ds(pl.multiple_of((1 - c) * half, 16), half)
            return pltpu.make_async_remote_copy(
                src_ref=p_hbm.at[k, theirs], dst_ref=landed.at[k], send_sem=send.at[k], recv_sem=recv.at[k],
                device_id=(x, y, 1 - c), device_id_type=MESH)

        @pl.when(j == 0)
        def _():
            for k in range(nch):
                copy(k).start()

        copy(j).wait_recv()
        mine = p_ref[0, pl.ds(pl.multiple_of(c * half, 16), half), :]
        out_ref[0] = (mine.astype(F32) + landed[j].astype(F32)).astype(out_ref.dtype)

        @pl.when(j == nch - 1)
        def _():
            for k in range(nch):
                copy(k).wait_send()

    return pl.pallas_call(
        body, name=name, grid=(nch,),
        in_specs=[HBM, pl.BlockSpec((1, rows, cols), lambda j: (j, 0, 0))],
        out_specs=pl.BlockSpec((1, half, cols), lambda j: (j, 0, 0)),
        out_shape=jax.ShapeDtypeStruct((nch, half, cols), part.dtype),
        scratch_shapes=[pltpu.VMEM((nch, half, cols), part.dtype),
                        pltpu.SemaphoreType.DMA((nch,)), pltpu.SemaphoreType.DMA((nch,))],
        compiler_params=_params("arbitrary"),
    )(part, part)


def _reduce_sems(n):
    return [pltpu.SemaphoreType.DMA((3 * n,)), pltpu.SemaphoreType.DMA((3 * n,)), pltpu.SemaphoreType.DMA((n,)),
            pltpu.SemaphoreType.DMA((n,)), pltpu.SemaphoreType.DMA((n,))]


def _reduce_exchange(ins, got, sib, sems):
    n = len(ins)
    send, recv, local, d2d_send, d2d_recv = sems
    x, y, c = _place()
    me = 2 * x + y
    chips = _other_chips(x, y)

    def own(k):
        return pltpu.make_async_copy(ins[k].at[me], got[k].at[me], local.at[k])

    def copy(j, k, shard, slot):
        return pltpu.make_async_remote_copy(
            src_ref=ins[k].at[shard], dst_ref=got[k].at[slot], send_sem=send.at[j * n + k],
            recv_sem=recv.at[j * n + k], device_id=(chips[j][0], chips[j][1], c), device_id_type=MESH)

    def swap(k):
        return pltpu.make_async_remote_copy(
            src_ref=got[k], dst_ref=sib[k], send_sem=d2d_send.at[k], recv_sem=d2d_recv.at[k],
            device_id=(x, y, 1 - c), device_id_type=MESH)

    def start():
        for k in range(n):
            own(k).start()
        for j in range(3):
            for k in range(n):
                copy(j, k, 2 * chips[j][0] + chips[j][1], me).start()

    def forward():
        for k in range(n):
            own(k).wait()
            for j in range(3):
                copy(j, k, me, 2 * chips[j][0] + chips[j][1]).wait_recv()
            swap(k).start()

    def finish():
        for k in range(n):
            swap(k).wait_recv()
        for j in range(3):
            for k in range(n):
                copy(j, k, me, me).wait_send()
        for k in range(n):
            swap(k).wait_send()

    return start, forward, finish


def _reduce_shards(name, parts):
    n = len(parts)

    def body(*refs):
        for stage in _reduce_exchange(refs[:n], refs[n:2 * n], refs[2 * n:3 * n], refs[3 * n:]):
            stage()

    return pl.pallas_call(
        body, name=name,
        in_specs=[HBM] * n, out_specs=[HBM] * (2 * n),
        out_shape=[jax.ShapeDtypeStruct(s.shape, s.dtype) for s in parts] * 2,
        scratch_shapes=_reduce_sems(n),
    )(*parts)


def _all_reduce_scratch(vec):
    return [pltpu.VMEM((N_DEV,) + vec.shape, F32),
            pltpu.SemaphoreType.DMA((N_DEV - 1,)), pltpu.SemaphoreType.DMA((N_DEV - 1,))]


def _all_reduce_exchange(v_ref, o_ref, buf, send, recv):
    x, y, c = _place()
    me = 4 * x + 2 * y + c
    peers = [(x ^ (r >> 2), y ^ ((r >> 1) & 1), c ^ (r & 1)) for r in range(1, N_DEV)]

    def copy(r, slot):
        px, py, pc = peers[r]
        return pltpu.make_async_remote_copy(
            src_ref=v_ref, dst_ref=buf.at[slot], send_sem=send.at[r], recv_sem=recv.at[r],
            device_id=(px, py, pc), device_id_type=MESH)

    def start():
        for r in range(N_DEV - 1):
            copy(r, me).start()
        buf[me] = v_ref[...]

    def finish():
        for r in range(N_DEV - 1):
            px, py, pc = peers[r]
            copy(r, 4 * px + 2 * py + pc).wait_recv()
        total = buf[0]
        for dev in range(1, N_DEV):
            total = total + buf[dev]
        o_ref[...] = total
        for r in range(N_DEV - 1):
            copy(r, me).wait_send()

    return start, lambda: None, finish


def _adamw(w, g, m, v):
    m = ADAM_B1 * m + (1.0 - ADAM_B1) * g
    v = ADAM_B2 * v + (1.0 - ADAM_B2) * (g * g)
    m_hat = m / (1.0 - ADAM_B1 ** ADAM_STEP)
    v_hat = v / (1.0 - ADAM_B2 ** ADAM_STEP)
    return -ADAM_LR * (m_hat / (jnp.sqrt(v_hat) + ADAM_EPS) + ADAM_WD * w), m, v


def _adamw_shard(name, core, w, m, v, got, sib):
    rows, cols = w.shape
    tr = rows // 4
    spec = pl.BlockSpec((tr, cols), lambda i, c_ref: (i, 0))
    got_spec = pl.BlockSpec((N_CHIPS, tr, cols), lambda i, c_ref: (0, jnp.where(i // 2 == c_ref[0], i % 2, 0), 0))
    sib_spec = pl.BlockSpec((N_CHIPS, tr, cols), lambda i, c_ref: (0, jnp.where(i // 2 == c_ref[0], 0, i % 2), 0))

    def body(c_ref, w_ref, m_ref, v_ref, got_ref, sib_ref, g_ref, d_ref, mo_ref, vo_ref):
        def four(ref):
            return ((ref[0].astype(F32) + ref[1].astype(F32)) + ref[2].astype(F32)) + ref[3].astype(F32)

        g = jnp.where(pl.program_id(0) // 2 == c_ref[0], four(got_ref), four(sib_ref))
        g_ref[...] = g
        d_ref[...], mo_ref[...], vo_ref[...] = _adamw(w_ref[...], g, m_ref[...], v_ref[...])

    return pl.pallas_call(
        body, name=name,
        grid_spec=pltpu.PrefetchScalarGridSpec(
            num_scalar_prefetch=1, grid=(4,),
            in_specs=[spec, spec, spec, got_spec, sib_spec], out_specs=[spec] * 4),
        out_shape=[jax.ShapeDtypeStruct((rows, cols), F32)] * 4,
        compiler_params=_params("arbitrary"),
    )(core, w, m, v, got, sib)


def _adamw_small(name, w, m, v, g):
    def body(w_ref, m_ref, v_ref, g_ref, d_ref, mo_ref, vo_ref):
        d_ref[...], mo_ref[...], vo_ref[...] = _adamw(w_ref[...], g_ref[...], m_ref[...], v_ref[...])

    return pl.pallas_call(body, name=name, out_shape=[jax.ShapeDtypeStruct(w.shape, F32)] * 3)(w, m, v, g)


def _cast_bf16(name, arrays, exchange=None):
    n = len(arrays)

    def body(*refs):
        for i_ref, o_ref in zip(refs[:n], refs[n:]):
            o_ref[...] = i_ref[...].astype(BF16)

    specs = [pl.BlockSpec((a.shape[0] // 4, a.shape[1]), lambda i: (i, 0)) for a in arrays]
    return _call_with_exchange(name, body, (4,), specs, specs, [jax.ShapeDtypeStruct(a.shape, BF16) for a in arrays],
                               [], tuple(arrays), _params("arbitrary"), exchange)


def _pack(arrays, rows):
    flat = jnp.concatenate([a.reshape(-1) for a in arrays])
    return jnp.concatenate([flat, jnp.zeros((rows * 128 - flat.shape[0],), F32)]).reshape(rows, 128)


def _unpack(packed, shapes):
    flat, out, at = packed.reshape(-1), [], 0
    for s in shapes:
        size = 1
        for dim in s:
            size *= dim
        out.append(flat[at:at + size].reshape(s))
        at += size
    return out


def _rows_for(shapes):
    total = 0
    for s in shapes:
        size = 1
        for dim in s:
            size *= dim
        total += size
    return -(-total // 1024) * 8


WEIGHTS = ['meta_tokens', 'ffn1_norm', 'ffn1_w_gate', 'ffn1_w_up', 'ffn1_w_down', 'mix_norm', 'w_in', 'rwkv_mu',
           'rwkv_w0', 'rwkv_w_up', 'rwkv_a0', 'rwkv_a_up', 'rwkv_g_up', 'rwkv_k_k', 'rwkv_k_a', 'rwkv_r_k',
           'rwkv_lnx_w', 'rwkv_lnx_b', 'w_out', 'ffn2_norm', 'ffn2_w_gate', 'ffn2_w_up', 'ffn2_w_down', 'final_norm']
COL_CUT = ['ffn1_w_gate', 'ffn1_w_up', 'w_in', 'ffn2_w_gate', 'ffn2_w_up']
ROW_CUT = ['ffn1_w_down', 'w_out', 'ffn2_w_down']
SMALL_CUT = ['meta_tokens', 'rwkv_w_up', 'rwkv_a_up', 'rwkv_g_up']
TRANSPOSED = ['ffn1_w_gate', 'ffn1_w_up', 'ffn2_w_gate', 'ffn2_w_up']
BIG = COL_CUT + ROW_CUT
REPLICATED = [n for n in WEIGHTS if n not in BIG + SMALL_CUT]


def _join_cols(a):
    return a.transpose(1, 0, 2).reshape(a.shape[1], N_CHIPS * a.shape[2])


def _cut_cols(a):
    return a.reshape(a.shape[0], N_CHIPS, a.shape[1] // N_CHIPS).transpose(1, 0, 2)


def _step(x, loss_target, w, m, v):
    two = lambda a: a.reshape(a.shape[-2], a.shape[-1])

    def rows_cut(n, a):
        return jnp.swapaxes(two(a), 0, 1) if n in TRANSPOSED else two(a)

    def as_given(n, a, like):
        return (jnp.swapaxes(a, 0, 1) if n in TRANSPOSED else a).reshape(like.shape)

    col_cut = [n for n in COL_CUT + SMALL_CUT if n not in TRANSPOSED]

    def join(names, gathered):
        return {n: (_join_cols(a) if n in col_cut else a.reshape(-1, a.shape[-1])) for n, a in zip(names, gathered)}

    def pair_sums(names, g):
        parts = [_cut_cols(g[n]) if n in col_cut else g[n].reshape(N_CHIPS, -1, g[n].shape[-1]) for n in names]
        return [_pair_sum("pair_sum_" + n, p) for n, p in zip(names, parts)]

    first = [n for n in BIG if n not in MID + LATE]
    gathered_later = {"mid": MID, "late": LATE}
    groups = {**gathered_later, "gate": ["ffn1_w_gate"], "up": ["ffn1_w_up"]}
    cast = dict(zip(first, _cast_bf16("cast_first", [rows_cut(n, w[n]) for n in first])))
    names = first + SMALL_CUT
    shards = [cast[n] for n in first] + [two(w[n]) for n in SMALL_CUT]
    later = [n for n in BIG if n not in first]
    outs = _cast_bf16("cast_rest", [rows_cut(n, w[n]) for n in later], ("gather", shards))
    cast.update(zip(later, outs[:len(later)]))
    full = {n: (two(w[n]) if w[n].ndim == 3 else w[n]) for n in REPLICATED}
    full.update(join(names, outs[len(later):]))
    full["rwkv_r_k"] = w["rwkv_r_k"]
    full["final_norm"] = w["final_norm"]

    small_names = REPLICATED + SMALL_CUT

    def small(g, loss):
        arrays = [loss.reshape(1)] + [g[n] for n in small_names]
        return _pack(arrays, _rows_for([a.shape for a in arrays]))

    late = types.SimpleNamespace(shards={k: [cast[n] for n in names] for k, names in gathered_later.items()},
                                 join=lambda k, gathered: join(groups[k], gathered),
                                 parts=lambda k, g: pair_sums(groups[k], g), small=small)

    _, dx, g, reduced = _local_step(x[0], loss_target[0], full, late)

    groups["down"] = ["ffn1_w_down"]
    reduced["down"] = list(_reduce_shards("reduce_gradients", pair_sums(groups["down"], g)))
    got, sib = {}, {}
    for k, names in groups.items():
        got.update(zip(names, reduced[k][:len(names)]))
        sib.update(zip(names, reduced[k][len(names):]))
    loss, *summed = _unpack(reduced["small"], [(1,)] + [g[n].shape for n in small_names])
    loss = loss.reshape(())
    g_small = dict(zip(small_names, summed))
    chip = 2 * lax.axis_index("x") + lax.axis_index("y")
    for n in SMALL_CUT:
        width = g_small[n].shape[1] // N_CHIPS
        g_small[n] = lax.dynamic_slice_in_dim(g_small[n], chip * width, width, axis=1)

    grad, delta, new_m, new_v = {}, {}, {}, {}
    core = lax.axis_index("c").astype(jnp.int32).reshape(1)
    for n in BIG:
        outs = _adamw_shard("adamw_" + n, core, rows_cut(n, w[n]), rows_cut(n, m[n]), rows_cut(n, v[n]), got[n], sib[n])
        grad[n], delta[n], new_m[n], new_v[n] = (as_given(n, o, w[n]) for o in outs)
    shapes = [w[n].shape for n in small_names]
    rows = _rows_for(shapes)
    packed = [_pack([t[n] for n in small_names], rows) for t in (w, m, v)]
    g_packed = _pack([g_small[n] for n in small_names], rows)
    outs = [_unpack(o, shapes) for o in _adamw_small("adamw_small", *packed, g_packed)]
    for i, n in enumerate(small_names):
        grad[n] = g_small[n].reshape(w[n].shape)
        delta[n], new_m[n], new_v[n] = outs[0][i], outs[1][i], outs[2][i]
    return loss, dx[None], grad, delta, new_m, new_v


def kernel(x, meta_tokens, ffn1_norm, ffn1_w_gate, ffn1_w_up, ffn1_w_down, mix_norm, w_in, rwkv_mu, rwkv_w0, rwkv_w_up, rwkv_a0, rwkv_a_up, rwkv_g_up, rwkv_k_k, rwkv_k_a, rwkv_r_k, rwkv_lnx_w, rwkv_lnx_b, w_out, ffn2_norm, ffn2_w_gate, ffn2_w_up, ffn2_w_down, final_norm, loss_target, m_meta_tokens, m_ffn1_norm, m_ffn1_w_gate, m_ffn1_w_up, m_ffn1_w_down, m_mix_norm, m_w_in, m_rwkv_mu, m_rwkv_w0, m_rwkv_w_up, m_rwkv_a0, m_rwkv_a_up, m_rwkv_g_up, m_rwkv_k_k, m_rwkv_k_a, m_rwkv_r_k, m_rwkv_lnx_w, m_rwkv_lnx_b, m_w_out, m_ffn2_norm, m_ffn2_w_gate, m_ffn2_w_up, m_ffn2_w_down, m_final_norm, v_meta_tokens, v_ffn1_norm, v_ffn1_w_gate, v_ffn1_w_up, v_ffn1_w_down, v_mix_norm, v_w_in, v_rwkv_mu, v_rwkv_w0, v_rwkv_w_up, v_rwkv_a0, v_rwkv_a_up, v_rwkv_g_up, v_rwkv_k_k, v_rwkv_k_a, v_rwkv_r_k, v_rwkv_lnx_w, v_rwkv_lnx_b, v_w_out, v_ffn2_norm, v_ffn2_w_gate, v_ffn2_w_up, v_ffn2_w_down, v_final_norm):
    w = dict(zip(WEIGHTS, (meta_tokens, ffn1_norm, ffn1_w_gate, ffn1_w_up, ffn1_w_down, mix_norm, w_in, rwkv_mu, rwkv_w0, rwkv_w_up, rwkv_a0, rwkv_a_up, rwkv_g_up, rwkv_k_k, rwkv_k_a, rwkv_r_k, rwkv_lnx_w, rwkv_lnx_b, w_out, ffn2_norm, ffn2_w_gate, ffn2_w_up, ffn2_w_down, final_norm)))
    m = dict(zip(WEIGHTS, (m_meta_tokens, m_ffn1_norm, m_ffn1_w_gate, m_ffn1_w_up, m_ffn1_w_down, m_mix_norm, m_w_in, m_rwkv_mu, m_rwkv_w0, m_rwkv_w_up, m_rwkv_a0, m_rwkv_a_up, m_rwkv_g_up, m_rwkv_k_k, m_rwkv_k_a, m_rwkv_r_k, m_rwkv_lnx_w, m_rwkv_lnx_b, m_w_out, m_ffn2_norm, m_ffn2_w_gate, m_ffn2_w_up, m_ffn2_w_down, m_final_norm)))
    v = dict(zip(WEIGHTS, (v_meta_tokens, v_ffn1_norm, v_ffn1_w_gate, v_ffn1_w_up, v_ffn1_w_down, v_mix_norm, v_w_in, v_rwkv_mu, v_rwkv_w0, v_rwkv_w_up, v_rwkv_a0, v_rwkv_a_up, v_rwkv_g_up, v_rwkv_k_k, v_rwkv_k_a, v_rwkv_r_k, v_rwkv_lnx_w, v_rwkv_lnx_b, v_w_out, v_ffn2_norm, v_ffn2_w_gate, v_ffn2_w_up, v_ffn2_w_down, v_final_norm)))
    loss, grad_x, grad, delta, new_m, new_v = _step(x, loss_target, w, m, v)
    return (loss, grad_x, *[grad[n] for n in WEIGHTS], *[delta[n] for n in WEIGHTS],
            *[new_m[n] for n in WEIGHTS], *[new_v[n] for n in WEIGHTS])
```
